```python
import math
import jax, jax.numpy as jnp
from jax import lax
import numpy as np

D_MODEL = 2048
BATCH = 4
SEQ = 2048
DEPTH = 1
DEC_BATCH = 128
DEC_SEQ = 1
PAST_LEN = 16384
PAGE_SIZE = 128

N_META = 16
SSM_WIDTH = D_MODEL // 2
SSM_GROUP = 16
SSM_GROUPS = SSM_WIDTH // SSM_GROUP
SSM_STATE = 64
MLSTM_WIDTH = D_MODEL // 2
MLSTM_HEADS = 4
MLSTM_DK = MLSTM_WIDTH // MLSTM_HEADS
MLSTM_DV = MLSTM_WIDTH // MLSTM_HEADS
CHUNK = 128
D_FF = -(-8 * D_MODEL // (3 * 256)) * 256
EPS = 1e-5
ALPHA = (2 * DEPTH) ** 0.25
BETA = (8 * DEPTH) ** -0.25

N_IN = SSM_WIDTH + 2 * MLSTM_HEADS * MLSTM_DK + MLSTM_HEADS * MLSTM_DV + MLSTM_WIDTH + 2 * MLSTM_HEADS + 2 * D_MODEL
_SPLIT_SIZES = (SSM_WIDTH, MLSTM_HEADS * MLSTM_DK, MLSTM_HEADS * MLSTM_DK, MLSTM_HEADS * MLSTM_DV,
                MLSTM_WIDTH, 2 * MLSTM_HEADS, D_MODEL)
SPLITS = tuple(int(s) for s in np.cumsum(_SPLIT_SIZES))

kernel_name = "hybrid_s5_mlstm_gated_decoder_step"


def _layernorm(x, g, b):
    xf = x.astype(jnp.float32)
    mu = xf.mean(-1, keepdims=True)
    var = jnp.mean(jnp.square(xf - mu), -1, keepdims=True)
    return ((xf - mu) * lax.rsqrt(var + EPS) * g + b).astype(x.dtype)


def _ssm_combine(e1, e2):
    a1, b1 = e1
    a2, b2 = e2
    return a1 * a2, a2 * b1 + b2


def _s5_branch(u, h_re, h_im, a_re, a_im, log_dt, b_re, b_im, c_re, c_im, d_skip, w_glu, b_glu):
    bt, t, _ = u.shape
    f32 = jnp.float32
    lam = lax.complex(a_re.astype(f32), a_im.astype(f32))
    dt = jnp.exp(log_dt.astype(f32))
    a_bar = jnp.exp(lam * dt)
    b_bar = ((a_bar - 1.0) / lam)[..., None] * lax.complex(b_re.astype(f32), b_im.astype(f32))
    ug = u.reshape(bt, t, SSM_GROUPS, SSM_GROUP)
    bu = jnp.einsum("btgc,gpc->btgp", ug.astype(jnp.complex64), b_bar)
    h0 = lax.complex(h_re.astype(f32), h_im.astype(f32))
    bu = bu.at[:, 0].add(a_bar * h0)
    a_seq = jnp.broadcast_to(a_bar, bu.shape)
    _, hs = lax.associative_scan(_ssm_combine, (a_seq, bu), axis=1)
    c = lax.complex(c_re.astype(f32), c_im.astype(f32))
    y = jnp.einsum("btgp,gcp->btgc", hs, c).real + d_skip.astype(f32).reshape(SSM_GROUPS, SSM_GROUP) * ug
    y = y.reshape(bt, t, SSM_WIDTH)
    g = jax.nn.gelu(y)
    out = g * jax.nn.sigmoid(g @ w_glu.astype(f32) + b_glu.astype(f32))
    h_last = hs[:, -1]
    return out, h_last.real, h_last.imag


def _mlstm_chunk(state, q, k, v, ig, fl):
    c, n, m = state
    length = q.shape[2]
    b = jnp.cumsum(fl, axis=-1)
    causal = jnp.tril(jnp.ones((length, length), dtype=bool))
    dmat = jnp.where(causal, b[..., :, None] - b[..., None, :] + ig[..., None, :], -jnp.inf)
    inter = b + m[..., None]
    m_t = jnp.maximum(inter, dmat.max(-1))
    w_intra = jnp.exp(dmat - m_t[..., None])
    w_inter = jnp.exp(inter - m_t)
    s = jnp.einsum("bhtk,bhsk->bhts", q, k) * w_intra
    num = w_inter[..., None] * jnp.einsum("bhtk,bhkv->bhtv", q, c) + jnp.einsum("bhts,bhsv->bhtv", s, v)
    den = w_inter * jnp.einsum("bhtk,bhk->bht", q, n) + s.sum(-1)
    h = num / jnp.maximum(jnp.abs(den), jnp.exp(-m_t))[..., None]
    m_new = m_t[..., -1]
    w_end = jnp.exp(b[..., -1:] - b + ig - m_new[..., None])
    carry = jnp.exp(b[..., -1] + m - m_new)
    c_new = carry[..., None, None] * c + jnp.einsum("bhs,bhsk,bhsv->bhkv", w_end, k, v)
    n_new = carry[..., None] * n + jnp.einsum("bhs,bhsk->bhk", w_end, k)
    return (c_new, n_new, m_new), h


def _mlstm_branch(q, k, v, ig, fl, state, lead):
    state, h_lead = _mlstm_chunk(state, q[:, :, :lead], k[:, :, :lead], v[:, :, :lead],
                                 ig[:, :, :lead], fl[:, :, :lead])
    t = q.shape[2]
    if t == lead:
        return h_lead, state
    nc = (t - lead) // CHUNK

    def blocks(z):
        z = z[:, :, lead:]
        return jnp.moveaxis(z.reshape(z.shape[:2] + (nc, CHUNK) + z.shape[3:]), 2, 0)

    def step(carry, xs):
        return _mlstm_chunk(carry, *xs)

    state, h_rest = lax.scan(step, state, (blocks(q), blocks(k), blocks(v), blocks(ig), blocks(fl)))
    bsz, nh = q.shape[0], q.shape[1]
    h_rest = jnp.moveaxis(h_rest, 0, 2).reshape(bsz, nh, nc * CHUNK, MLSTM_DV)
    return jnp.concatenate([h_lead, h_rest], axis=2), state


def _layer(x, h_re, h_im, c0, n0, m0, lead, w_in, b_if, a_re, a_im, log_dt, b_re, b_im, c_re, c_im,
           d_skip, w_glu, b_glu, w_a_up, mh_gain, w_b_up, w_out, ln1_g, ln1_b, w_gate, w_up, w_down,
           ln2_g, ln2_b):
    f32 = jnp.float32
    bt, t, _ = x.shape
    proj = (x @ w_in).astype(f32)
    u, q, k, v, o, g_if, g_a, g_b = jnp.split(proj, SPLITS, axis=-1)
    y_a, h_re_new, h_im_new = _s5_branch(u, h_re, h_im, a_re, a_im, log_dt, b_re, b_im, c_re, c_im,
                                         d_skip, w_glu, b_glu)
    def heads(z):
        return z.reshape(bt, t, MLSTM_HEADS, -1).transpose(0, 2, 1, 3)
    g_if = g_if + b_if.astype(f32)
    ig = g_if[..., :MLSTM_HEADS].transpose(0, 2, 1)
    fl = jax.nn.log_sigmoid(g_if[..., MLSTM_HEADS:]).transpose(0, 2, 1)
    state0 = (c0.astype(f32), n0.astype(f32), m0.astype(f32))
    h, (c_new, n_new, m_new) = _mlstm_branch(heads(q), heads(k) * (MLSTM_DK ** -0.5), heads(v), ig, fl,
                                             state0, lead)
    mu = h.mean(-1, keepdims=True)
    var = jnp.mean(jnp.square(h - mu), -1, keepdims=True)
    hn = ((h - mu) * lax.rsqrt(var + EPS)).transpose(0, 2, 1, 3).reshape(bt, t, MLSTM_WIDTH)
    y_b = jax.nn.sigmoid(o) * (hn * mh_gain.astype(f32))
    mix = jax.nn.sigmoid(g_a) * (y_a @ w_a_up.astype(f32)) + jax.nn.sigmoid(g_b) * (y_b @ w_b_up.astype(f32))
    mix = (mix @ w_out.astype(f32)).astype(x.dtype)
    x = _layernorm(ALPHA * x + mix, ln1_g, ln1_b)
    ff = (jax.nn.silu(x @ w_gate) * (x @ w_up)) @ w_down
    x = _layernorm(ALPHA * x + ff.astype(x.dtype), ln2_g, ln2_b)
    return x, (h_re_new, h_im_new, c_new, n_new, m_new)


def setup_inputs(seed: int = 0) -> dict:
    key = jax.random.key(seed)
    ks = jax.random.split(key, 32)
    f32 = jnp.float32

    def nrm(k, shape, s):
        return s * jax.random.normal(k, shape, f32)

    G, P, H = SSM_GROUPS, SSM_STATE, MLSTM_HEADS
    b_if = jnp.concatenate([
        nrm(ks[10], (DEPTH, H), 0.1),
        jnp.broadcast_to(jnp.linspace(3.0, 6.0, H, dtype=f32), (DEPTH, H)) + nrm(ks[11], (DEPTH, H), 0.1)],
        axis=-1)
    return {
        "x_prompt": nrm(ks[0], (BATCH, SEQ, D_MODEL), 1.0),
        "x_sample": nrm(ks[1], (DEC_BATCH, DEC_SEQ, D_MODEL), 1.0),
        "state_ssm_re": nrm(ks[2], (DEPTH, DEC_BATCH, G, P), 1.0),
        "state_ssm_im": nrm(ks[3], (DEPTH, DEC_BATCH, G, P), 1.0),
        "state_mlstm_c": nrm(ks[4], (DEPTH, DEC_BATCH, H, MLSTM_DK, MLSTM_DV), 0.1),
        "state_mlstm_n": nrm(ks[5], (DEPTH, DEC_BATCH, H, MLSTM_DK), 0.1),
        "state_mlstm_m": nrm(ks[6], (DEPTH, DEC_BATCH, H), 1.0),
        "meta_tokens": nrm(ks[7], (N_META, D_MODEL), 1.0),
        "w_in": nrm(ks[8], (DEPTH, D_MODEL, N_IN), D_MODEL ** -0.5),
        "b_if": b_if,
        "ssm_a_re": -0.5 + nrm(ks[12], (DEPTH, G, P), 0.01),
        "ssm_a_im": jnp.pi * jnp.arange(P, dtype=f32) + nrm(ks[13], (DEPTH, G, P), 0.01),
        "ssm_log_dt": jax.random.uniform(ks[14], (DEPTH, G, P), f32, math.log(1e-3), math.log(1e-1)),
        "ssm_b_re": nrm(ks[15], (DEPTH, G, P, SSM_GROUP), (2 * SSM_GROUP) ** -0.5),
        "ssm_b_im": nrm(ks[16], (DEPTH, G, P, SSM_GROUP), (2 * SSM_GROUP) ** -0.5),
        "ssm_c_re": nrm(ks[17], (DEPTH, G, SSM_GROUP, P), (2 * P) ** -0.5),
        "ssm_c_im": nrm(ks[18], (DEPTH, G, SSM_GROUP, P), (2 * P) ** -0.5),
        "ssm_d": nrm(ks[19], (DEPTH, SSM_WIDTH), 1.0),
        "w_glu": nrm(ks[20], (DEPTH, SSM_WIDTH, SSM_WIDTH), SSM_WIDTH ** -0.5),
        "b_glu": nrm(ks[21], (DEPTH, SSM_WIDTH), 0.01),
        "w_a_up": nrm(ks[22], (DEPTH, SSM_WIDTH, D_MODEL), SSM_WIDTH ** -0.5),
        "mh_gain": 1.0 + nrm(ks[23], (DEPTH, MLSTM_WIDTH), 0.01),
        "w_b_up": nrm(ks[24], (DEPTH, MLSTM_WIDTH, D_MODEL), MLSTM_WIDTH ** -0.5),
        "w_out": nrm(ks[25], (DEPTH, D_MODEL, D_MODEL), BETA * D_MODEL ** -0.5),
        "ln1_g": 1.0 + nrm(ks[26], (DEPTH, D_MODEL), 0.01),
        "ln1_b": nrm(ks[27], (DEPTH, D_MODEL), 0.01),
        "w_gate": nrm(ks[28], (DEPTH, D_MODEL, D_FF), D_MODEL ** -0.5),
        "w_up": nrm(ks[29], (DEPTH, D_MODEL, D_FF), D_MODEL ** -0.5),
        "w_down": nrm(ks[30], (DEPTH, D_FF, D_MODEL), BETA * D_FF ** -0.5),
        "ln2_g": 1.0 + nrm(ks[31], (DEPTH, D_MODEL), 0.01),
        "ln2_b": nrm(ks[9], (DEPTH, D_MODEL), 0.01),
    }


def reference(x_prompt, x_sample, state_ssm_re, state_ssm_im, state_mlstm_c, state_mlstm_n, state_mlstm_m,
              meta_tokens, w_in, b_if, ssm_a_re, ssm_a_im, ssm_log_dt, ssm_b_re, ssm_b_im, ssm_c_re, ssm_c_im,
              ssm_d, w_glu, b_glu, w_a_up, mh_gain, w_b_up, w_out, ln1_g, ln1_b, w_gate, w_up, w_down,
              ln2_g, ln2_b):
    f32 = jnp.float32
    bp = x_prompt.shape[0]
    meta = jnp.broadcast_to(meta_tokens.astype(x_prompt.dtype)[None], (bp, N_META, D_MODEL))
    xp = jnp.concatenate([meta, x_prompt], axis=1)
    xs = x_sample
    new_p = ([], [], [], [], [])
    new_s = ([], [], [], [], [])
    for l in range(DEPTH):
        lw = (w_in[l], b_if[l], ssm_a_re[l], ssm_a_im[l], ssm_log_dt[l], ssm_b_re[l], ssm_b_im[l],
              ssm_c_re[l], ssm_c_im[l], ssm_d[l], w_glu[l], b_glu[l], w_a_up[l], mh_gain[l], w_b_up[l],
              w_out[l], ln1_g[l], ln1_b[l], w_gate[l], w_up[l], w_down[l], ln2_g[l], ln2_b[l])
        z_ssm = jnp.zeros((bp, SSM_GROUPS, SSM_STATE), f32)
        z_c = jnp.zeros((bp, MLSTM_HEADS, MLSTM_DK, MLSTM_DV), f32)
        z_n = jnp.zeros((bp, MLSTM_HEADS, MLSTM_DK), f32)
        z_m = jnp.zeros((bp, MLSTM_HEADS), f32)
        xp, st_p = _layer(xp, z_ssm, z_ssm, z_c, z_n, z_m, N_META, *lw)
        xs, st_s = _layer(xs, state_ssm_re[l], state_ssm_im[l], state_mlstm_c[l], state_mlstm_n[l],
                          state_mlstm_m[l], xs.shape[1], *lw)
        for lst, a in zip(new_p, st_p):
            lst.append(a)
        for lst, a in zip(new_s, st_s):
            lst.append(a)
    y_prompt = xp[:, N_META:]
    y_sample = xs
    p_ssm_re = jnp.stack(new_p[0])
    p_ssm_im = jnp.stack(new_p[1])
    p_mlstm_c = jnp.stack(new_p[2])
    p_mlstm_n = jnp.stack(new_p[3])
    p_mlstm_m = jnp.stack(new_p[4])
    s_ssm_re = jnp.stack(new_s[0])
    s_ssm_im = jnp.stack(new_s[1])
    s_mlstm_c = jnp.stack(new_s[2])
    s_mlstm_n = jnp.stack(new_s[3])
    s_mlstm_m = jnp.stack(new_s[4])
    return (y_prompt, y_sample, p_ssm_re, p_ssm_im, p_mlstm_c, p_mlstm_n, p_mlstm_m,
            s_ssm_re, s_ssm_im, s_mlstm_c, s_mlstm_n, s_mlstm_m)
```

```python
import functools

import jax
import jax.numpy as jnp
from jax import lax
from jax.experimental import pallas as pl
from jax.experimental.pallas import tpu as pltpu

F32 = jnp.float32
BF16 = jnp.bfloat16

D_MODEL = 2048
N_META = 16
SSM_WIDTH = 1024
SSM_GROUP = 16
SSM_GROUPS = 64
SSM_STATE = 64
SSM_FLAT = SSM_GROUPS * SSM_STATE
SSM_JBLK = 4
HEADS = 4
DK = 256
DV = 256
CHUNK = 128
D_FF = 5632
EPS = 1e-5
ALPHA = 2.0 ** 0.25

S5_TC = 256
S5_NSEG = 8
S5_SEG = S5_TC // S5_NSEG
LANES = 128
SUBLANES = 8
S5_TILES = SSM_FLAT // LANES
S5_TPJ = S5_TILES // SSM_JBLK
S5_GT = 4

V7X_VMEM_BYTES = 64 * 1024 * 1024


def _cparams(n_axes, vmem_mb):
    return pltpu.CompilerParams(
        dimension_semantics=("arbitrary",) * n_axes,
        vmem_limit_bytes=min(vmem_mb * 1024 * 1024, V7X_VMEM_BYTES - 8 * 1024 * 1024),
    )


def _sigmoid(x):
    return 1.0 / (1.0 + jnp.exp(-x))


def _log_sigmoid(x):
    return jnp.minimum(x, 0.0) - jnp.log(1.0 + jnp.exp(-jnp.abs(x)))


def _gelu_tanh(x):
    c = 0.7978845608028654
    return 0.5 * x * (1.0 + jnp.tanh(c * (x + 0.044715 * (x * x * x))))


def _layernorm_rows(x, g, b):
    mu = jnp.mean(x, axis=-1, keepdims=True)
    xc = x - mu
    var = jnp.mean(xc * xc, axis=-1, keepdims=True)
    return xc * lax.rsqrt(var + EPS) * g + b


def _mm_kernel(x_ref, w_ref, o_ref):
    o_ref[...] = jnp.dot(x_ref[...], w_ref[...], preferred_element_type=F32).astype(o_ref.dtype)


def _matmul(x, w, out_dtype, tm, tn, name):
    r, k = x.shape
    n = w.shape[1]
    return pl.pallas_call(
        _mm_kernel,
        grid=(n // tn, r // tm),
        in_specs=[pl.BlockSpec((tm, k), lambda j, i: (i, 0)),
                  pl.BlockSpec((k, tn), lambda j, i: (0, j))],
        out_specs=pl.BlockSpec((tm, tn), lambda j, i: (i, j)),
        out_shape=jax.ShapeDtypeStruct((r, n), out_dtype),
        compiler_params=_cparams(2, 40),
        name=name,
    )(x, w)


def _gates_kernel(x_ref, w_ref, wt_ref, g_ref, gt_ref):
    x = x_ref[...]
    g_ref[...] = jnp.dot(x, w_ref[...], preferred_element_type=F32)
    gt_ref[...] = lax.dot_general(wt_ref[...], x, (((1,), (1,)), ((), ())), preferred_element_type=F32)


def _gates_matmul(x, w, wt, tm, name):
    r, k = x.shape
    n = w.shape[1]
    return pl.pallas_call(
        _gates_kernel,
        grid=(r // tm,),
        in_specs=[pl.BlockSpec((tm, k), lambda i: (i, 0)),
                  pl.BlockSpec((k, n), lambda i: (0, 0)),
                  pl.BlockSpec((n, k), lambda i: (0, 0))],
        out_specs=[pl.BlockSpec((tm, n), lambda i: (i, 0)),
                   pl.BlockSpec((n, tm), lambda i: (0, i))],
        out_shape=[jax.ShapeDtypeStruct((r, n), F32), jax.ShapeDtypeStruct((n, r), F32)],
        compiler_params=_cparams(1, 32),
        name=name,
    )(x, w, wt)


def _s5_discretise(a_re, a_im, log_dt, b_re, b_im, c_re, c_im):
    dt = jnp.exp(log_dt)
    e = jnp.exp(a_re * dt)
    abar_re = e * jnp.cos(a_im * dt)
    abar_im = e * jnp.sin(a_im * dt)
    nr = abar_re - 1.0
    ni = abar_im
    den = a_re * a_re + a_im * a_im
    coef_re = (nr * a_re + ni * a_im) / den
    coef_im = (ni * a_re - nr * a_im) / den
    bb_re = coef_re[..., None] * b_re - coef_im[..., None] * b_im
    bb_im = coef_re[..., None] * b_im + coef_im[..., None] * b_re
    eye = jnp.eye(SSM_GROUPS // SSM_JBLK, dtype=F32)

    def bblk(bb):
        t = bb.reshape(SSM_JBLK, 16, SSM_STATE, SSM_GROUP).transpose(0, 1, 3, 2)
        return (t[:, :, :, None, :] * eye[None, :, None, :, None]).reshape(SSM_JBLK, 256, 1024)

    def cblk(cc):
        t = cc.reshape(SSM_JBLK, 16, SSM_GROUP, SSM_STATE).transpose(0, 1, 3, 2)
        return (t[:, :, :, None, :] * eye[None, :, None, :, None]).reshape(SSM_JBLK, 1024, 256)

    b_blk = jnp.concatenate([bblk(bb_re), bblk(bb_im)], axis=-1).astype(BF16)
    cre_blk = cblk(c_re).astype(BF16)
    cimn_blk = cblk(-c_im).astype(BF16)
    return (abar_re.reshape(1, SSM_FLAT), abar_im.reshape(1, SSM_FLAT), b_blk, cre_blk, cimn_blk)


def _cmul_add(ar, ai, hr, hi, xr, xi):
    return ar * hr - ai * hi + xr, ar * hi + ai * hr + xi


def _s5_project_in(u, b_ref, hre_ref, him_ref, rows):
    for j in range(SSM_JBLK):
        bu = jnp.dot(u[:, j * 256:(j + 1) * 256].astype(BF16), b_ref[j], preferred_element_type=F32)
        for t in range(S5_TPJ):
            hre_ref[j * S5_TPJ + t, 0:rows, :] = bu[:, t * LANES:(t + 1) * LANES]
            him_ref[j * S5_TPJ + t, 0:rows, :] = bu[:, 1024 + t * LANES:1024 + (t + 1) * LANES]


def _s5_project_out(u, hre_ref, him_ref, cre_ref, cimn_ref, d_ref, y_ref, rows):
    for j in range(SSM_JBLK):
        hr = jnp.concatenate([hre_ref[j * S5_TPJ + t, 0:rows, :] for t in range(S5_TPJ)], axis=1).astype(BF16)
        hi = jnp.concatenate([him_ref[j * S5_TPJ + t, 0:rows, :] for t in range(S5_TPJ)], axis=1).astype(BF16)
        y = jnp.dot(hr, cre_ref[j], preferred_element_type=F32)
        y = y + jnp.dot(hi, cimn_ref[j], preferred_element_type=F32)
        sl = slice(j * 256, (j + 1) * 256)
        y_ref[:, sl] = y + d_ref[:, sl] * u[:, sl]


def _s5_prompt_kernel(u_ref, um_ref, b_ref, cre_ref, cimn_ref, are_ref, aim_ref, d_ref,
                      y_ref, fre_ref, fim_ref,
                      hre_ref, him_ref, pre_ref, pim_ref, cre_s, cim_s, hsre_ref, hsim_ref):
    b_id = pl.program_id(0)
    c_id = pl.program_id(1)
    n_c = pl.num_programs(1)

    @pl.when(jnp.logical_and(b_id == 0, c_id == 0))
    def _():
        def tile_body(ct, carry):
            ar = are_ref[ct]
            ai = aim_ref[ct]
            pr, pi = ar, ai
            pre_ref[ct, 0:1, :] = pr
            pim_ref[ct, 0:1, :] = pi
            for i in range(1, S5_SEG):
                pr, pi = ar * pr - ai * pi, ar * pi + ai * pr
                pre_ref[ct, i:i + 1, :] = pr
                pim_ref[ct, i:i + 1, :] = pi
            return carry

        lax.fori_loop(0, S5_TILES, tile_body, 0)

    @pl.when(c_id == 0)
    def _():
        _s5_project_in(um_ref[...], b_ref, hre_ref, him_ref, N_META)

        def tile_body(ct, carry):
            ar = are_ref[ct]
            ai = aim_ref[ct]
            hr = jnp.zeros((1, LANES), F32)
            hi = jnp.zeros((1, LANES), F32)
            for t in range(N_META):
                hr, hi = _cmul_add(ar, ai, hr, hi, hre_ref[ct, t:t + 1, :], him_ref[ct, t:t + 1, :])
            cre_s[ct] = hr
            cim_s[ct] = hi
            return carry

        lax.fori_loop(0, S5_TILES, tile_body, 0)

    u = u_ref[...]
    _s5_project_in(u, b_ref, hre_ref, him_ref, S5_TC)

    def group_body(g, carry):
        tiles = [g * S5_GT + t for t in range(S5_GT)]
        full = (S5_NSEG, LANES)
        ar = [jnp.broadcast_to(are_ref[ct], full) for ct in tiles]
        ai = [jnp.broadcast_to(aim_ref[ct], full) for ct in tiles]

        def local_step(i, hs):
            rows = pl.ds(i, S5_NSEG, stride=S5_SEG)
            out_r, out_i = [], []
            for t, ct in enumerate(tiles):
                nr, ni = _cmul_add(ar[t], ai[t], hs[t], hs[S5_GT + t], hre_ref[ct, rows, :], him_ref[ct, rows, :])
                hre_ref[ct, rows, :] = nr
                him_ref[ct, rows, :] = ni
                out_r.append(nr)
                out_i.append(ni)
            return tuple(out_r + out_i)

        ends = lax.fori_loop(0, S5_SEG, local_step, tuple(jnp.zeros(full, F32) for _ in range(2 * S5_GT)), unroll=2)

        for t, ct in enumerate(tiles):
            asr = pre_ref[ct, S5_SEG - 1:S5_SEG, :]
            asi = pim_ref[ct, S5_SEG - 1:S5_SEG, :]
            hr = cre_s[ct]
            hi = cim_s[ct]
            for j in range(S5_NSEG):
                hsre_ref[ct, j:j + 1, :] = hr
                hsim_ref[ct, j:j + 1, :] = hi
                hr, hi = _cmul_add(asr, asi, hr, hi, ends[t][j:j + 1, :], ends[S5_GT + t][j:j + 1, :])
            cre_s[ct] = hr
            cim_s[ct] = hi

        sr = [hsre_ref[ct] for ct in tiles]
        si = [hsim_ref[ct] for ct in tiles]

        def fix_step(i, c2):
            rows = pl.ds(i, S5_NSEG, stride=S5_SEG)
            for t, ct in enumerate(tiles):
                pr = jnp.broadcast_to(pre_ref[ct, pl.ds(i, 1), :], full)
                pi = jnp.broadcast_to(pim_ref[ct, pl.ds(i, 1), :], full)
                hre_ref[ct, rows, :] = hre_ref[ct, rows, :] + (pr * sr[t] - pi * si[t])
                him_ref[ct, rows, :] = him_ref[ct, rows, :] + (pr * si[t] + pi * sr[t])
            return c2

        lax.fori_loop(0, S5_SEG, fix_step, 0, unroll=2)
        return carry

    lax.fori_loop(0, S5_TILES // S5_GT, group_body, 0)

    _s5_project_out(u, hre_ref, him_ref, cre_ref, cimn_ref, d_ref, y_ref, S5_TC)

    @pl.when(c_id == n_c - 1)
    def _():
        for ct in range(S5_TILES):
            fre_ref[0, :, ct * LANES:(ct + 1) * LANES] = cre_s[ct]
            fim_ref[0, :, ct * LANES:(ct + 1) * LANES] = cim_s[ct]


def _s5_prompt(u_arr, u_col_block, u_meta, abar_re, abar_im, b_blk, cre_blk, cimn_blk, d_skip, batch, seq):
    n_c = seq // S5_TC
    const3 = lambda b, c: (0, 0, 0)
    const2 = lambda b, c: (0, 0)
    return pl.pallas_call(
        _s5_prompt_kernel,
        grid=(batch, n_c),
        in_specs=[pl.BlockSpec((S5_TC, SSM_WIDTH), lambda b, c: (b * n_c + c, u_col_block)),
                  pl.BlockSpec((N_META, SSM_WIDTH), const2),
                  pl.BlockSpec((SSM_JBLK, 256, 2048), const3),
                  pl.BlockSpec((SSM_JBLK, 1024, 256), const3),
                  pl.BlockSpec((SSM_JBLK, 1024, 256), const3),
                  pl.BlockSpec((S5_TILES, 1, LANES), const3),
                  pl.BlockSpec((S5_TILES, 1, LANES), const3),
                  pl.BlockSpec((1, SSM_WIDTH), const2)],
        out_specs=[pl.BlockSpec((S5_TC, SSM_WIDTH), lambda b, c: (b * n_c + c, 0)),
                   pl.BlockSpec((1, 1, SSM_FLAT), lambda b, c: (b, 0, 0)),
                   pl.BlockSpec((1, 1, SSM_FLAT), lambda b, c: (b, 0, 0))],
        out_shape=[jax.ShapeDtypeStruct((batch * seq, SSM_WIDTH), F32),
                   jax.ShapeDtypeStruct((batch, 1, SSM_FLAT), F32),
                   jax.ShapeDtypeStruct((batch, 1, SSM_FLAT), F32)],
        scratch_shapes=[pltpu.VMEM((S5_TILES, S5_TC, LANES), F32), pltpu.VMEM((S5_TILES, S5_TC, LANES), F32),
                        pltpu.VMEM((S5_TILES, S5_SEG, LANES), F32), pltpu.VMEM((S5_TILES, S5_SEG, LANES), F32),
                        pltpu.VMEM((S5_TILES, 1, LANES), F32), pltpu.VMEM((S5_TILES, 1, LANES), F32),
                        pltpu.VMEM((S5_TILES, S5_NSEG, LANES), F32), pltpu.VMEM((S5_TILES, S5_NSEG, LANES), F32)],
        compiler_params=_cparams(2, 48),
        name="s5_prompt",
    )(u_arr, u_meta, b_blk, cre_blk, cimn_blk, abar_re.reshape(S5_TILES, 1, LANES),
      abar_im.reshape(S5_TILES, 1, LANES), d_skip)


def _s5_sample_kernel(u_ref, h0re_ref, h0im_ref, b_ref, cre_ref, cimn_ref, are_ref, aim_ref, d_ref,
                      y_ref, fre_ref, fim_ref, hre_ref, him_ref):
    rows = u_ref.shape[0]
    u = u_ref[...]
    _s5_project_in(u, b_ref, hre_ref, him_ref, rows)
    for ct in range(S5_TILES):
        lanes = slice(ct * LANES, (ct + 1) * LANES)
        nr, ni = _cmul_add(are_ref[:, lanes], aim_ref[:, lanes], h0re_ref[:, lanes], h0im_ref[:, lanes],
                           hre_ref[ct], him_ref[ct])
        hre_ref[ct] = nr
        him_ref[ct] = ni
        fre_ref[:, lanes] = nr
        fim_ref[:, lanes] = ni
    _s5_project_out(u, hre_ref, him_ref, cre_ref, cimn_ref, d_ref, y_ref, rows)


def _s5_sample(u, h0_re, h0_im, abar_re, abar_im, b_blk, cre_blk, cimn_blk, d_skip):
    rows = u.shape[0]
    return pl.pallas_call(
        _s5_sample_kernel,
        out_shape=[jax.ShapeDtypeStruct((rows, SSM_WIDTH), F32),
                   jax.ShapeDtypeStruct((rows, SSM_FLAT), F32),
                   jax.ShapeDtypeStruct((rows, SSM_FLAT), F32)],
        scratch_shapes=[pltpu.VMEM((S5_TILES, rows, LANES), F32), pltpu.VMEM((S5_TILES, rows, LANES), F32)],
        compiler_params=_cparams(0, 48),
        name="s5_sample",
    )(u, h0_re, h0_im, b_blk, cre_blk, cimn_blk, abar_re, abar_im, d_skip)


def _split3(x):
    hi = x.astype(BF16)
    r1 = x - hi.astype(F32)
    mid = r1.astype(BF16)
    lo = (r1 - mid.astype(F32)).astype(BF16)
    return hi, mid, lo


def _tri(n, lower):
    r = lax.broadcasted_iota(jnp.int32, (n, n), 0)
    c = lax.broadcasted_iota(jnp.int32, (n, n), 1)
    return (r >= c) if lower else (r <= c)


def _gate_columns(gcol, bias_row):
    n = gcol.shape[0]
    g = gcol + bias_row
    fl = _log_sigmoid(g[:, HEADS:])
    tril = jnp.where(_tri(n, True), 1.0, 0.0).astype(BF16)
    b = sum(jnp.dot(tril, p, preferred_element_type=F32) for p in _split3(fl))
    return g[:, :HEADS], b


def _gate_rows(grow, bias_col):
    n = grow.shape[1]
    g = grow + bias_col
    fl = _log_sigmoid(g[HEADS:, :])
    triu = jnp.where(_tri(n, False), 1.0, 0.0).astype(BF16)
    b = sum(jnp.dot(p, triu, preferred_element_type=F32) for p in _split3(fl))
    return g[:HEADS, :], b


def _mlstm_state_update(k, v, ig_col, b_col, c_old, n_old, m_old):
    n_rows = k.shape[0]
    b_last = b_col[n_rows - 1:n_rows, :]
    e_col = b_last - b_col + ig_col
    m_new = jnp.maximum(b_last + m_old, jnp.max(e_col, axis=0, keepdims=True))
    w_end = jnp.exp(e_col - m_new)
    carry = jnp.exp(b_last + m_old - m_new)
    kw = k * w_end
    kv = lax.dot_general(kw.astype(BF16), v, (((0,), (0,)), ((), ())), preferred_element_type=F32)
    c_new = carry * c_old + kv
    n_new = carry * n_old + jnp.sum(kw, axis=0, keepdims=True)
    return c_new, n_new, m_new


def _mlstm_chunk_out(q, k_bf, v, ig_row, b_row, b_col, c_old, n_old, m_old):
    n_rows = q.shape[0]
    dmat = jnp.where(_tri(n_rows, True), b_col + (ig_row - b_row), -jnp.inf)
    inter = b_col + m_old
    m_t = jnp.maximum(inter, jnp.max(dmat, axis=-1, keepdims=True))
    w_intra = jnp.exp(dmat - m_t)
    w_inter = jnp.exp(inter - m_t)
    s = lax.dot_general(q, k_bf, (((1,), (1,)), ((), ())), preferred_element_type=F32) * w_intra
    num = w_inter * jnp.dot(q, c_old.astype(BF16), preferred_element_type=F32)
    num = num + jnp.dot(s.astype(BF16), v, preferred_element_type=F32)
    den = w_inter * jnp.sum(q.astype(F32) * n_old, axis=-1, keepdims=True) + jnp.sum(s, axis=-1, keepdims=True)
    return num / jnp.maximum(jnp.abs(den), jnp.exp(-m_t))


def _head_norm_gate(h, o, gain):
    mu = jnp.mean(h, axis=-1, keepdims=True)
    hc = h - mu
    var = jnp.mean(hc * hc, axis=-1, keepdims=True)
    return _sigmoid(o) * (hc * lax.rsqrt(var + EPS) * gain)


def _mlstm_prompt_kernel(q_ref, k_ref, v_ref, o_ref, gc_ref, gr_ref, km_ref, vm_ref, gcm_ref,
                         brow_ref, bcol_ref, gain_ref,
                         y_ref, cf_ref, nf_ref, mf_ref,
                         c_s, n_s, m_s):
    c_id = pl.program_id(1)
    n_c = pl.num_programs(1)
    k_scale = DK ** -0.5

    @pl.when(c_id == 0)
    def _():
        ig_c, b_c = _gate_columns(gcm_ref[...], brow_ref[...])
        for h in range(HEADS):
            hs = slice(h * DK, (h + 1) * DK)
            k = km_ref[:, hs].astype(F32) * k_scale
            c_new, n_new, m_new = _mlstm_state_update(
                k, vm_ref[:, hs], ig_c[:, h:h + 1], b_c[:, h:h + 1],
                jnp.zeros((DK, DV), F32), jnp.zeros((1, DK), F32), jnp.zeros((1, 1), F32))
            c_s[h] = c_new
            n_s[h] = n_new
            m_s[h] = jnp.broadcast_to(m_new, (8, 128))

    ig_c, b_c = _gate_columns(gc_ref[...], brow_ref[...])
    ig_r, b_r = _gate_rows(gr_ref[...], bcol_ref[...])
    for h in range(HEADS):
        hs = slice(h * DK, (h + 1) * DK)
        q = q_ref[:, hs]
        kf = k_ref[:, hs].astype(F32) * k_scale
        v = v_ref[:, hs]
        c_old = c_s[h]
        n_old = n_s[h]
        m_old = m_s[h][0:1, 0:1]
        hid = _mlstm_chunk_out(q, kf.astype(BF16), v, ig_r[h:h + 1, :], b_r[h:h + 1, :], b_c[:, h:h + 1],
                               c_old, n_old, m_old)
        y_ref[:, hs] = _head_norm_gate(hid, o_ref[:, hs], gain_ref[:, hs]).astype(y_ref.dtype)
        c_new, n_new, m_new = _mlstm_state_update(kf, v, ig_c[:, h:h + 1], b_c[:, h:h + 1], c_old, n_old, m_old)
        c_s[h] = c_new
        n_s[h] = n_new
        m_s[h] = jnp.broadcast_to(m_new, (8, 128))

    @pl.when(c_id == n_c - 1)
    def _():
        cf_ref[0] = c_s[...]
        nf_ref[0] = n_s[...]
        for h in range(HEADS):
            mf_ref[0, h:h + 1, :] = m_s[h][0:1, :]


def _mlstm_prompt(qkv, o_arr, o_col_block, gcol, grow, qkv_meta, gcol_meta, bias_row, bias_col, gain, batch, seq):
    n_c = seq // CHUNK
    width = HEADS * DK
    row = lambda b, c: b * n_c + c
    const2 = lambda b, c: (0, 0)
    return pl.pallas_call(
        _mlstm_prompt_kernel,
        grid=(batch, n_c),
        in_specs=[pl.BlockSpec((CHUNK, width), lambda b, c: (row(b, c), 0)),
                  pl.BlockSpec((CHUNK, width), lambda b, c: (row(b, c), 1)),
                  pl.BlockSpec((CHUNK, width), lambda b, c: (row(b, c), 2)),
                  pl.BlockSpec((CHUNK, width), lambda b, c: (row(b, c), o_col_block)),
                  pl.BlockSpec((CHUNK, 2 * HEADS), lambda b, c: (row(b, c), 0)),
                  pl.BlockSpec((2 * HEADS, CHUNK), lambda b, c: (0, row(b, c))),
                  pl.BlockSpec((N_META, width), lambda b, c: (0, 1)),
                  pl.BlockSpec((N_META, width), lambda b, c: (0, 2)),
                  pl.BlockSpec((N_META, 2 * HEADS), const2),
                  pl.BlockSpec((1, 2 * HEADS), const2),
                  pl.BlockSpec((2 * HEADS, 1), const2),
                  pl.BlockSpec((1, width), const2)],
        out_specs=[pl.BlockSpec((CHUNK, width), lambda b, c: (row(b, c), 0)),
                   pl.BlockSpec((1, HEADS, DK, DV), lambda b, c: (b, 0, 0, 0)),
                   pl.BlockSpec((1, HEADS, 1, DK), lambda b, c: (b, 0, 0, 0)),
                   pl.BlockSpec((1, HEADS, 128), lambda b, c: (b, 0, 0))],
        out_shape=[jax.ShapeDtypeStruct((batch * seq, width), BF16),
                   jax.ShapeDtypeStruct((batch, HEADS, DK, DV), F32),
                   jax.ShapeDtypeStruct((batch, HEADS, 1, DK), F32),
                   jax.ShapeDtypeStruct((batch, HEADS, 128), F32)],
        scratch_shapes=[pltpu.VMEM((HEADS, DK, DV), F32), pltpu.VMEM((HEADS, 1, DK), F32),
                        pltpu.VMEM((HEADS, 8, 128), F32)],
        compiler_params=_cparams(2, 32),
        name="mlstm_prompt",
    )(qkv, qkv, qkv, o_arr, gcol, grow, qkv_meta, qkv_meta, gcol_meta, bias_row, bias_col, gain)


MLSTM_SB = 8


def _mlstm_sample_kernel(qkv_ref, o_ref, g_ref, c_ref, n_ref, m_ref, brow_ref, gain_ref,
                         y_ref, co_ref, no_ref, mo_ref):
    k_scale = DK ** -0.5
    eye = _tri(DK, True) & _tri(DK, False)
    width = HEADS * DK

    def body(i, carry):
        g = g_ref[pl.ds(i, 1), :] + brow_ref[...]
        m_row = m_ref[pl.ds(i, 1), :]
        for h in range(HEADS):
            hs = slice(h * DK, (h + 1) * DK)
            ig = g[:, h:h + 1]
            fl = _log_sigmoid(g[:, HEADS + h:HEADS + h + 1])
            m_old = m_row[:, h:h + 1]
            m_new = jnp.maximum(fl + m_old, ig)
            w_in = jnp.exp(ig - m_new)
            w_ca = jnp.exp(fl + m_old - m_new)
            q = qkv_ref[pl.ds(i, 1), hs].astype(BF16)
            k = (qkv_ref[pl.ds(i, 1), width + h * DK:width + (h + 1) * DK] * k_scale).astype(BF16)
            v = qkv_ref[pl.ds(i, 1), 2 * width + h * DK:2 * width + (h + 1) * DK].astype(BF16)
            qf = q.astype(F32)
            kf = k.astype(F32)
            vf = v.astype(F32)
            c_old = c_ref[i, h]
            n_old = n_ref[i, h]
            s = jnp.sum(qf * kf, axis=-1, keepdims=True) * w_in
            qc = jnp.dot(jnp.broadcast_to(q, (8, DK)), c_old.astype(BF16), preferred_element_type=F32)[0:1, :]
            num = w_ca * qc + s * vf
            den = w_ca * jnp.sum(qf * n_old, axis=-1, keepdims=True) + s
            hid = num / jnp.maximum(jnp.abs(den), jnp.exp(-m_new))
            kw = kf * w_in
            kdiag = jnp.where(eye, jnp.broadcast_to(kw, (DK, DK)), 0.0).astype(BF16)
            kv = jnp.dot(kdiag, jnp.broadcast_to(v, (DK, DV)), preferred_element_type=F32)
            co_ref[i, h] = w_ca * c_old + kv
            no_ref[i, h] = w_ca * n_old + kw
            mo_ref[pl.ds(i, 1), h:h + 1] = m_new
            y_ref[pl.ds(i, 1), hs] = _head_norm_gate(hid, o_ref[pl.ds(i, 1), hs], gain_ref[:, hs])
        return carry

    lax.fori_loop(0, MLSTM_SB, body, 0)


def _mlstm_sample(qkv, o_arr, gcol, c0, n0, m0, bias_row, gain):
    nb = qkv.shape[0]
    width = HEADS * DK
    sb = MLSTM_SB
    const2 = lambda i: (0, 0)
    return pl.pallas_call(
        _mlstm_sample_kernel,
        grid=(nb // sb,),
        in_specs=[pl.BlockSpec((sb, 3 * width), lambda i: (i, 0)),
                  pl.BlockSpec((sb, width), lambda i: (i, 0)),
                  pl.BlockSpec((sb, 2 * HEADS), lambda i: (i, 0)),
                  pl.BlockSpec((sb, HEADS, DK, DV), lambda i: (i, 0, 0, 0)),
                  pl.BlockSpec((sb, HEADS, 1, DK), lambda i: (i, 0, 0, 0)),
                  pl.BlockSpec((sb, HEADS), lambda i: (i, 0)),
                  pl.BlockSpec((1, 2 * HEADS), const2),
                  pl.BlockSpec((1, width), const2)],
        out_specs=[pl.BlockSpec((sb, width), lambda i: (i, 0)),
                   pl.BlockSpec((sb, HEADS, DK, DV), lambda i: (i, 0, 0, 0)),
                   pl.BlockSpec((sb, HEADS, 1, DK), lambda i: (i, 0, 0, 0)),
                   pl.BlockSpec((sb, HEADS), lambda i: (i, 0))],
        out_shape=[jax.ShapeDtypeStruct((nb, width), F32),
                   jax.ShapeDtypeStruct((nb, HEADS, DK, DV), F32),
                   jax.ShapeDtypeStruct((nb, HEADS, 1, DK), F32),
                   jax.ShapeDtypeStruct((nb, HEADS), F32)],
        compiler_params=_cparams(1, 48),
        name="mlstm_sample",
    )(qkv, o_arr, gcol, c0, n0, m0, bias_row, gain)


def _merge_kernel(ya_ref, yb_ref, ga_ref, gb_ref, x_ref, wglu_ref, bglu_ref, wa_ref, wb_ref, wout_ref,
                  g1_ref, b1_ref, o_ref):
    g = _gelu_tanh(ya_ref[...])
    z = jnp.dot(g.astype(BF16), wglu_ref[...], preferred_element_type=F32) + bglu_ref[...]
    out_a = g * _sigmoid(z)
    up_a = jnp.dot(out_a.astype(BF16), wa_ref[...], preferred_element_type=F32)
    up_b = jnp.dot(yb_ref[...], wb_ref[...], preferred_element_type=F32)
    mix = _sigmoid(ga_ref[...]) * up_a + _sigmoid(gb_ref[...]) * up_b
    mo = jnp.dot(mix.astype(BF16), wout_ref[...], preferred_element_type=F32)
    o_ref[...] = _layernorm_rows(ALPHA * x_ref[...] + mo, g1_ref[...], b1_ref[...])


def _merge(ya, yb, gab, x, w_glu, b_glu, w_a_up, w_b_up, w_out, ln_g, ln_b, tm):
    r = x.shape[0]
    const2 = lambda i: (0, 0)
    resident = functools.partial(pl.BlockSpec, index_map=const2, pipeline_mode=pl.Buffered(1))
    return pl.pallas_call(
        _merge_kernel,
        grid=(r // tm,),
        in_specs=[pl.BlockSpec((tm, SSM_WIDTH), lambda i: (i, 0)),
                  pl.BlockSpec((tm, HEADS * DV), lambda i: (i, 0)),
                  pl.BlockSpec((tm, D_MODEL), lambda i: (i, 0)),
                  pl.BlockSpec((tm, D_MODEL), lambda i: (i, 1)),
                  pl.BlockSpec((tm, D_MODEL), lambda i: (i, 0)),
                  resident((SSM_WIDTH, SSM_WIDTH)),
                  resident((1, SSM_WIDTH)),
                  resident((SSM_WIDTH, D_MODEL)),
                  resident((HEADS * DV, D_MODEL)),
                  resident((D_MODEL, D_MODEL)),
                  resident((1, D_MODEL)),
                  resident((1, D_MODEL))],
        out_specs=pl.BlockSpec((tm, D_MODEL), lambda i: (i, 0)),
        out_shape=jax.ShapeDtypeStruct((r, D_MODEL), F32),
        compiler_params=_cparams(1, 52),
        name="merge_ln1",
    )(ya, yb, gab, gab, x, w_glu, b_glu, w_a_up, w_b_up, w_out, ln_g, ln_b)


def _ffn_kernel(x_ref, wg_ref, wu_ref, wd_ref, g2_ref, b2_ref, o_ref, xb_s, acc_s):
    f_id = pl.program_id(1)
    n_f = pl.num_programs(1)

    @pl.when(f_id == 0)
    def _():
        xb_s[...] = x_ref[...].astype(BF16)
        acc_s[...] = jnp.zeros_like(acc_s)

    xb = xb_s[...]
    hg = jnp.dot(xb, wg_ref[...], preferred_element_type=F32)
    hu = jnp.dot(xb, wu_ref[...], preferred_element_type=F32)
    act = (hg * _sigmoid(hg)) * hu
    acc_s[...] += jnp.dot(act.astype(BF16), wd_ref[...], preferred_element_type=F32)

    @pl.when(f_id == n_f - 1)
    def _():
        o_ref[...] = _layernorm_rows(ALPHA * x_ref[...] + acc_s[...], g2_ref[...], b2_ref[...])


def _ffn(x, w_gate, w_up, w_down, ln_g, ln_b, tm, tf):
    r = x.shape[0]
    const2 = lambda i, f: (0, 0)
    return pl.pallas_call(
        _ffn_kernel,
        grid=(r // tm, D_FF // tf),
        in_specs=[pl.BlockSpec((tm, D_MODEL), lambda i, f: (i, 0)),
                  pl.BlockSpec((D_MODEL, tf), lambda i, f: (0, f)),
                  pl.BlockSpec((D_MODEL, tf), lambda i, f: (0, f)),
                  pl.BlockSpec((tf, D_MODEL), lambda i, f: (f, 0)),
                  pl.BlockSpec((1, D_MODEL), const2),
                  pl.BlockSpec((1, D_MODEL), const2)],
        out_specs=pl.BlockSpec((tm, D_MODEL), lambda i, f: (i, 0)),
        out_shape=jax.ShapeDtypeStruct((r, D_MODEL), F32),
        scratch_shapes=[pltpu.VMEM((tm, D_MODEL), BF16), pltpu.VMEM((tm, D_MODEL), F32)],
        compiler_params=_cparams(2, 48),
        name="ffn_ln2",
    )(x, w_gate, w_up, w_down, ln_g, ln_b)


def kernel(x_prompt, x_sample, state_ssm_re, state_ssm_im, state_mlstm_c, state_mlstm_n, state_mlstm_m,
           meta_tokens, w_in, b_if, ssm_a_re, ssm_a_im, ssm_log_dt, ssm_b_re, ssm_b_im, ssm_c_re, ssm_c_im,
           ssm_d, w_glu, b_glu, w_a_up, mh_gain, w_b_up, w_out, ln1_g, ln1_b, w_gate, w_up, w_down,
           ln2_g, ln2_b):
    batch, seq, _ = x_prompt.shape
    nb = x_sample.shape[0]
    width = HEADS * DK

    w_in_b = w_in[0].astype(BF16)
    c_u, c_q, c_o, c_if, c_gab = 0, SSM_WIDTH, SSM_WIDTH + 3 * width, SSM_WIDTH + 4 * width, SSM_WIDTH + 4 * width + 2 * HEADS
    w_uo = jnp.concatenate([w_in_b[:, c_u:c_q], w_in_b[:, c_o:c_if]], axis=1)
    w_qkv = w_in_b[:, c_q:c_o]
    w_if = w_in_b[:, c_if:c_gab]
    w_gab = w_in_b[:, c_gab:]
    w_glu_b = w_glu[0].astype(BF16)
    w_a_b = w_a_up[0].astype(BF16)
    w_b_b = w_b_up[0].astype(BF16)
    w_out_b = w_out[0].astype(BF16)
    w_gate_b = w_gate[0].astype(BF16)
    w_up_b = w_up[0].astype(BF16)
    w_down_b = w_down[0].astype(BF16)
    bias_row = b_if[0].reshape(1, 2 * HEADS)
    bias_col = b_if[0].reshape(2 * HEADS, 1)
    gain = mh_gain[0].reshape(1, width)
    d_skip = ssm_d[0].reshape(1, SSM_WIDTH)
    b_glu_r = b_glu[0].reshape(1, SSM_WIDTH)
    ln1 = (ln1_g[0].reshape(1, D_MODEL), ln1_b[0].reshape(1, D_MODEL))
    ln2 = (ln2_g[0].reshape(1, D_MODEL), ln2_b[0].reshape(1, D_MODEL))
    abar_re, abar_im, b_blk, cre_blk, cimn_blk = _s5_discretise(
        ssm_a_re[0], ssm_a_im[0], ssm_log_dt[0], ssm_b_re[0], ssm_b_im[0], ssm_c_re[0], ssm_c_im[0])

    xp = x_prompt.reshape(batch * seq, D_MODEL)
    xp_b = xp.astype(BF16)
    xs = x_sample.reshape(nb, D_MODEL)
    xsm_b = jnp.concatenate([xs, meta_tokens], axis=0).astype(BF16)
    n_small = nb + N_META

    uo_p = _matmul(xp_b, w_uo, F32, 512, 1024, "proj_uo_prompt")
    qkv_p = _matmul(xp_b, w_qkv, BF16, 512, 1024, "proj_qkv_prompt")
    gab_p = _matmul(xp_b, w_gab, F32, 512, 1024, "proj_gab_prompt")
    gcol_p, grow_p = _gates_matmul(xp_b, w_if, w_if.T, 1024, "proj_gates_prompt")

    uo_s = _matmul(xsm_b, w_uo, F32, n_small, 1024, "proj_uo_small")
    qkv_s = _matmul(xsm_b, w_qkv, F32, n_small, 1024, "proj_qkv_small")
    gab_s = _matmul(xsm_b, w_gab, F32, n_small, 1024, "proj_gab_small")
    gcol_s, _ = _gates_matmul(xsm_b, w_if, w_if.T, n_small, "proj_gates_small")

    u_meta = uo_s[nb:, :SSM_WIDTH]
    qkv_meta = qkv_s[nb:].astype(BF16)
    gcol_meta = gcol_s[nb:]

    ya_p, pf_re, pf_im = _s5_prompt(uo_p, 0, u_meta, abar_re, abar_im, b_blk, cre_blk, cimn_blk, d_skip, batch, seq)
    yb_p, pc, pn, pm = _mlstm_prompt(qkv_p, uo_p, 1, gcol_p, grow_p, qkv_meta, gcol_meta, bias_row, bias_col,
                                     gain, batch, seq)
    x1_p = _merge(ya_p, yb_p, gab_p, xp, w_glu_b, b_glu_r, w_a_b, w_b_b, w_out_b, *ln1, tm=256)
    y_p = _ffn(x1_p, w_gate_b, w_up_b, w_down_b, *ln2, tm=512, tf=512)

    ya_s, sf_re, sf_im = _s5_sample(uo_s[:nb, :SSM_WIDTH], state_ssm_re[0].reshape(nb, SSM_FLAT),
                                    state_ssm_im[0].reshape(nb, SSM_FLAT), abar_re, abar_im,
                                    b_blk, cre_blk, cimn_blk, d_skip)
    yb_s, sc, sn, sm = _mlstm_sample(qkv_s[:nb], uo_s[:nb, SSM_WIDTH:], gcol_s[:nb], state_mlstm_c[0],
                                     state_mlstm_n[0].reshape(nb, HEADS, 1, DK), state_mlstm_m[0], bias_row, gain)
    x1_s = _merge(ya_s, yb_s.astype(BF16), gab_s[:nb], xs, w_glu_b, b_glu_r, w_a_b, w_b_b, w_out_b, *ln1, tm=nb)
    y_s = _ffn(x1_s, w_gate_b, w_up_b, w_down_b, *ln2, tm=nb, tf=512)

    return (y_p.reshape(batch, seq, D_MODEL),
            y_s.reshape(nb, 1, D_MODEL),
            pf_re.reshape(1, batch, SSM_GROUPS, SSM_STATE),
            pf_im.reshape(1, batch, SSM_GROUPS, SSM_STATE),
            pc.reshape(1, batch, HEADS, DK, DV),
            pn.reshape(1, batch, HEADS, DK),
            pm[:, :, 0].reshape(1, batch, HEADS),
            sf_re.reshape(1, nb, SSM_GROUPS, SSM_STATE),
            sf_im.reshape(1, nb, SSM_GROUPS, SSM_STATE),
            sc.reshape(1, nb, HEADS, DK, DV),
            sn.reshape(1, nb, HEADS, DK),
            sm.reshape(1, nb, HEADS))
```

```python
import functools

import jax
import jax.numpy as jnp
import numpy as np
from jax import lax
from jax.experimental import pallas as pl
from jax.experimental.pallas import tpu as pltpu

F32 = jnp.float32
BF16 = jnp.bfloat16

D_MODEL = 2048
N_META = 16
SSM_WIDTH = 1024
SSM_GROUP = 16
SSM_GROUPS = 64
SSM_STATE = 64
SSM_FLAT = SSM_GROUPS * SSM_STATE
SSM_JBLK = 4
HEADS = 4
DK = 256
DV = 256
CHUNK = 128
D_FF = 5632
EPS = 1e-5
ALPHA = 2.0 ** 0.25

S5_TC = 256
S5_NSEG = 8
S5_SEG = S5_TC // S5_NSEG
S5_LW = 512

V7X_VMEM_BYTES = 64 * 1024 * 1024


def _cparams(n_axes, vmem_mb):
    return pltpu.CompilerParams(
        dimension_semantics=("arbitrary",) * n_axes,
        vmem_limit_bytes=min(vmem_mb * 1024 * 1024, V7X_VMEM_BYTES - 8 * 1024 * 1024),
    )


def _sigmoid(x):
    return 1.0 / (1.0 + jnp.exp(-x))


def _log_sigmoid(x):
    return jnp.minimum(x, 0.0) - jnp.log(1.0 + jnp.exp(-jnp.abs(x)))


def _gelu_tanh(x):
    c = 0.7978845608028654
    return 0.5 * x * (1.0 + jnp.tanh(c * (x + 0.044715 * (x * x * x))))


def _layernorm_rows(x, g, b):
    mu = jnp.mean(x, axis=-1, keepdims=True)
    xc = x - mu
    var = jnp.mean(xc * xc, axis=-1, keepdims=True)
    return xc * lax.rsqrt(var + EPS) * g + b


def _mm_kernel(x_ref, w_ref, o_ref):
    o_ref[...] = jnp.dot(x_ref[...], w_ref[...], preferred_element_type=F32).astype(o_ref.dtype)


def _matmul(x, w, out_dtype, tm, tn, name):
    r, k = x.shape
    n = w.shape[1]
    return pl.pallas_call(
        _mm_kernel,
        grid=(n // tn, r // tm),
        in_specs=[pl.BlockSpec((tm, k), lambda j, i: (i, 0)),
                  pl.BlockSpec((k, tn), lambda j, i: (0, j))],
        out_specs=pl.BlockSpec((tm, tn), lambda j, i: (i, j)),
        out_shape=jax.ShapeDtypeStruct((r, n), out_dtype),
        compiler_params=_cparams(2, 40),
        name=name,
    )(x, w)


def _gates_kernel(x_ref, w_ref, wt_ref, g_ref, gt_ref):
    x = x_ref[...]
    g_ref[...] = jnp.dot(x, w_ref[...], preferred_element_type=F32)
    gt_ref[...] = lax.dot_general(wt_ref[...], x, (((1,), (1,)), ((), ())), preferred_element_type=F32)


def _gates_matmul(x, w, wt, tm, name):
    r, k = x.shape
    n = w.shape[1]
    return pl.pallas_call(
        _gates_kernel,
        grid=(r // tm,),
        in_specs=[pl.BlockSpec((tm, k), lambda i: (i, 0)),
                  pl.BlockSpec((k, n), lambda i: (0, 0)),
                  pl.BlockSpec((n, k), lambda i: (0, 0))],
        out_specs=[pl.BlockSpec((tm, n), lambda i: (i, 0)),
                   pl.BlockSpec((n, tm), lambda i: (0, i))],
        out_shape=[jax.ShapeDtypeStruct((r, n), F32), jax.ShapeDtypeStruct((n, r), F32)],
        compiler_params=_cparams(1, 32),
        name=name,
    )(x, w, wt)


def _s5_discretise(a_re, a_im, log_dt, b_re, b_im, c_re, c_im):
    dt = jnp.exp(log_dt)
    e = jnp.exp(a_re * dt)
    abar_re = e * jnp.cos(a_im * dt)
    abar_im = e * jnp.sin(a_im * dt)
    nr = abar_re - 1.0
    ni = abar_im
    den = a_re * a_re + a_im * a_im
    coef_re = (nr * a_re + ni * a_im) / den
    coef_im = (ni * a_re - nr * a_im) / den
    bb_re = coef_re[..., None] * b_re - coef_im[..., None] * b_im
    bb_im = coef_re[..., None] * b_im + coef_im[..., None] * b_re
    eye = jnp.eye(SSM_GROUPS // SSM_JBLK, dtype=F32)

    def bblk(bb):
        t = bb.reshape(SSM_JBLK, 16, SSM_STATE, SSM_GROUP).transpose(0, 1, 3, 2)
        return (t[:, :, :, None, :] * eye[None, :, None, :, None]).reshape(SSM_JBLK, 256, 1024)

    def cblk(cc):
        t = cc.reshape(SSM_JBLK, 16, SSM_GROUP, SSM_STATE).transpose(0, 1, 3, 2)
        return (t[:, :, :, None, :] * eye[None, :, None, :, None]).reshape(SSM_JBLK, 1024, 256)

    b_blk = jnp.concatenate([bblk(bb_re), bblk(bb_im)], axis=-1).astype(BF16)
    cre_blk = cblk(c_re).astype(BF16)
    cimn_blk = cblk(-c_im).astype(BF16)
    return (abar_re.reshape(1, SSM_FLAT), abar_im.reshape(1, SSM_FLAT), b_blk, cre_blk, cimn_blk)


def _cmul_add(ar, ai, hr, hi, xr, xi):
    return ar * hr - ai * hi + xr, ar * hi + ai * hr + xi


def _s5_project_in(u, b_ref, hre_ref, him_ref, rows):
    for j in range(SSM_JBLK):
        bu = jnp.dot(u[:, j * 256:(j + 1) * 256], b_ref[j], preferred_element_type=F32)
        hre_ref[0:rows, j * 1024:(j + 1) * 1024] = bu[:, :1024]
        him_ref[0:rows, j * 1024:(j + 1) * 1024] = bu[:, 1024:]


def _s5_project_out(hre_ref, him_ref, cre_ref, cimn_ref, rows):
    ys = []
    for j in range(SSM_JBLK):
        hr = hre_ref[0:rows, j * 1024:(j + 1) * 1024].astype(BF16)
        hi = him_ref[0:rows, j * 1024:(j + 1) * 1024].astype(BF16)
        y = jnp.dot(hr, cre_ref[j], preferred_element_type=F32)
        ys.append(y + jnp.dot(hi, cimn_ref[j], preferred_element_type=F32))
    return jnp.concatenate(ys, axis=1)


def _s5_prompt_kernel(u_ref, um_ref, perm_ref, permt_ref, b_ref, cre_ref, cimn_ref, are_ref, aim_ref, d_ref,
                      y_ref, fre_ref, fim_ref,
                      hre_ref, him_ref, pre_ref, pim_ref, cre_s, cim_s, hsre_ref, hsim_ref):
    b_id = pl.program_id(0)
    c_id = pl.program_id(1)
    n_c = pl.num_programs(1)

    @pl.when(jnp.logical_and(b_id == 0, c_id == 0))
    def _():
        ar = are_ref[...]
        ai = aim_ref[...]
        pre_ref[0:1, :] = ar
        pim_ref[0:1, :] = ai

        def body(i, carry):
            pr, pi = carry
            nr = ar * pr - ai * pi
            ni = ar * pi + ai * pr
            pre_ref[pl.ds(i, 1), :] = nr
            pim_ref[pl.ds(i, 1), :] = ni
            return nr, ni

        lax.fori_loop(1, S5_SEG, body, (ar, ai))

    @pl.when(c_id == 0)
    def _():
        _s5_project_in(um_ref[...].astype(BF16), b_ref, hre_ref, him_ref, N_META)
        ar = are_ref[...]
        ai = aim_ref[...]
        hr = jnp.zeros((1, SSM_FLAT), F32)
        hi = jnp.zeros((1, SSM_FLAT), F32)
        for t in range(N_META):
            hr, hi = _cmul_add(ar, ai, hr, hi, hre_ref[t:t + 1, :], him_ref[t:t + 1, :])
        cre_s[...] = hr
        cim_s[...] = hi

    u = u_ref[...]
    u_perm = jnp.dot(perm_ref[...], u.astype(BF16), preferred_element_type=F32).astype(BF16)
    _s5_project_in(u_perm, b_ref, hre_ref, him_ref, S5_TC)

    full = (S5_NSEG, S5_LW)
    for lg in range(SSM_FLAT // S5_LW):
        lanes = slice(lg * S5_LW, (lg + 1) * S5_LW)
        ar = jnp.broadcast_to(are_ref[:, lanes], full)
        ai = jnp.broadcast_to(aim_ref[:, lanes], full)

        def local_step(k, carry):
            rows = pl.ds(pl.multiple_of(k * S5_NSEG, S5_NSEG), S5_NSEG)
            nr, ni = _cmul_add(ar, ai, carry[0], carry[1], hre_ref[rows, lanes], him_ref[rows, lanes])
            hre_ref[rows, lanes] = nr
            him_ref[rows, lanes] = ni
            return nr, ni

        er, ei = lax.fori_loop(0, S5_SEG, local_step, (jnp.zeros(full, F32), jnp.zeros(full, F32)), unroll=4)

        asr = pre_ref[S5_SEG - 1:S5_SEG, lanes]
        asi = pim_ref[S5_SEG - 1:S5_SEG, lanes]
        hr = cre_s[:, lanes]
        hi = cim_s[:, lanes]
        for j in range(S5_NSEG):
            hsre_ref[j:j + 1, lanes] = hr
            hsim_ref[j:j + 1, lanes] = hi
            hr, hi = _cmul_add(asr, asi, hr, hi, er[j:j + 1, :], ei[j:j + 1, :])
        cre_s[:, lanes] = hr
        cim_s[:, lanes] = hi

        sr = hsre_ref[:, lanes]
        si = hsim_ref[:, lanes]

        def fix_step(k, carry):
            rows = pl.ds(pl.multiple_of(k * S5_NSEG, S5_NSEG), S5_NSEG)
            pr = jnp.broadcast_to(pre_ref[pl.ds(k, 1), lanes], full)
            pi = jnp.broadcast_to(pim_ref[pl.ds(k, 1), lanes], full)
            hre_ref[rows, lanes] = hre_ref[rows, lanes] + (pr * sr - pi * si)
            him_ref[rows, lanes] = him_ref[rows, lanes] + (pr * si + pi * sr)
            return carry

        lax.fori_loop(0, S5_SEG, fix_step, 0, unroll=4)

    y_perm = _s5_project_out(hre_ref, him_ref, cre_ref, cimn_ref, S5_TC)
    permt = permt_ref[...]
    y = sum(jnp.dot(permt, piece, preferred_element_type=F32) for piece in _split3(y_perm))
    y_ref[...] = y + d_ref[...] * u

    @pl.when(c_id == n_c - 1)
    def _():
        fre_ref[0] = cre_s[...]
        fim_ref[0] = cim_s[...]


def _s5_prompt(u_arr, u_col_block, u_meta, abar_re, abar_im, b_blk, cre_blk, cimn_blk, d_skip, batch, seq):
    n_c = seq // S5_TC
    const3 = lambda b, c: (0, 0, 0)
    const2 = lambda b, c: (0, 0)
    rows = np.arange(S5_TC)
    perm_np = np.zeros((S5_TC, S5_TC), np.float32)
    perm_np[rows, (rows % S5_NSEG) * S5_SEG + rows // S5_NSEG] = 1.0
    perm = jnp.asarray(perm_np, dtype=BF16)
    permt = jnp.asarray(perm_np.T, dtype=BF16)
    return pl.pallas_call(
        _s5_prompt_kernel,
        grid=(batch, n_c),
        in_specs=[pl.BlockSpec((S5_TC, SSM_WIDTH), lambda b, c: (b * n_c + c, u_col_block)),
                  pl.BlockSpec((N_META, SSM_WIDTH), const2),
                  pl.BlockSpec((S5_TC, S5_TC), const2),
                  pl.BlockSpec((S5_TC, S5_TC), const2),
                  pl.BlockSpec((SSM_JBLK, 256, 2048), const3),
                  pl.BlockSpec((SSM_JBLK, 1024, 256), const3),
                  pl.BlockSpec((SSM_JBLK, 1024, 256), const3),
                  pl.BlockSpec((1, SSM_FLAT), const2),
                  pl.BlockSpec((1, SSM_FLAT), const2),
                  pl.BlockSpec((1, SSM_WIDTH), const2)],
        out_specs=[pl.BlockSpec((S5_TC, SSM_WIDTH), lambda b, c: (b * n_c + c, 0)),
                   pl.BlockSpec((1, 1, SSM_FLAT), lambda b, c: (b, 0, 0)),
                   pl.BlockSpec((1, 1, SSM_FLAT), lambda b, c: (b, 0, 0))],
        out_shape=[jax.ShapeDtypeStruct((batch * seq, SSM_WIDTH), F32),
                   jax.ShapeDtypeStruct((batch, 1, SSM_FLAT), F32),
                   jax.ShapeDtypeStruct((batch, 1, SSM_FLAT), F32)],
        scratch_shapes=[pltpu.VMEM((S5_TC, SSM_FLAT), F32), pltpu.VMEM((S5_TC, SSM_FLAT), F32),
                        pltpu.VMEM((S5_SEG, SSM_FLAT), F32), pltpu.VMEM((S5_SEG, SSM_FLAT), F32),
                        pltpu.VMEM((1, SSM_FLAT), F32), pltpu.VMEM((1, SSM_FLAT), F32),
                        pltpu.VMEM((S5_NSEG, SSM_FLAT), F32), pltpu.VMEM((S5_NSEG, SSM_FLAT), F32)],
        compiler_params=_cparams(2, 48),
        name="s5_prompt",
    )(u_arr, u_meta, perm, permt, b_blk, cre_blk, cimn_blk, abar_re, abar_im, d_skip)


def _s5_sample_kernel(u_ref, h0re_ref, h0im_ref, b_ref, cre_ref, cimn_ref, are_ref, aim_ref, d_ref,
                      y_ref, hre_ref, him_ref):
    rows = u_ref.shape[0]
    u = u_ref[...]
    _s5_project_in(u.astype(BF16), b_ref, hre_ref, him_ref, rows)
    nr, ni = _cmul_add(are_ref[...], aim_ref[...], h0re_ref[...], h0im_ref[...], hre_ref[...], him_ref[...])
    hre_ref[...] = nr
    him_ref[...] = ni
    y_ref[...] = _s5_project_out(hre_ref, him_ref, cre_ref, cimn_ref, rows) + d_ref[...] * u


def _s5_sample(u, h0_re, h0_im, abar_re, abar_im, b_blk, cre_blk, cimn_blk, d_skip):
    rows = u.shape[0]
    return pl.pallas_call(
        _s5_sample_kernel,
        out_shape=[jax.ShapeDtypeStruct((rows, SSM_WIDTH), F32),
                   jax.ShapeDtypeStruct((rows, SSM_FLAT), F32),
                   jax.ShapeDtypeStruct((rows, SSM_FLAT), F32)],
        compiler_params=_cparams(0, 48),
        name="s5_sample",
    )(u, h0_re, h0_im, b_blk, cre_blk, cimn_blk, abar_re, abar_im, d_skip)


def _split3(x):
    hi = x.astype(BF16)
    r1 = x - hi.astype(F32)
    mid = r1.astype(BF16)
    lo = (r1 - mid.astype(F32)).astype(BF16)
    return hi, mid, lo


def _tri(n, lower):
    r = lax.broadcasted_iota(jnp.int32, (n, n), 0)
    c = lax.broadcasted_iota(jnp.int32, (n, n), 1)
    return (r >= c) if lower else (r <= c)


def _gate_columns(gcol, bias_row):
    n = gcol.shape[0]
    g = gcol + bias_row
    fl = _log_sigmoid(g[:, HEADS:])
    tril = jnp.where(_tri(n, True), 1.0, 0.0).astype(BF16)
    b = sum(jnp.dot(tril, p, preferred_element_type=F32) for p in _split3(fl))
    return g[:, :HEADS], b


def _gate_rows(grow, bias_col):
    n = grow.shape[1]
    g = grow + bias_col
    fl = _log_sigmoid(g[HEADS:, :])
    triu = jnp.where(_tri(n, False), 1.0, 0.0).astype(BF16)
    b = sum(jnp.dot(p, triu, preferred_element_type=F32) for p in _split3(fl))
    return g[:HEADS, :], b


def _mlstm_state_update(k, v, ig_col, b_col, c_old, n_old, m_old):
    n_rows = k.shape[0]
    b_last = b_col[n_rows - 1:n_rows, :]
    e_col = b_last - b_col + ig_col
    m_new = jnp.maximum(b_last + m_old, jnp.max(e_col, axis=0, keepdims=True))
    w_end = jnp.exp(e_col - m_new)
    carry = jnp.exp(b_last + m_old - m_new)
    kw = k * w_end
    kv = lax.dot_general(kw.astype(BF16), v, (((0,), (0,)), ((), ())), preferred_element_type=F32)
    c_new = carry * c_old + kv
    n_new = carry * n_old + jnp.sum(kw, axis=0, keepdims=True)
    return c_new, n_new, m_new


def _mlstm_chunk_out(q, k_bf, v, ig_row, b_row, b_col, c_old, n_old, m_old):
    n_rows = q.shape[0]
    dmat = jnp.where(_tri(n_rows, True), b_col + (ig_row - b_row), -jnp.inf)
    inter = b_col + m_old
    m_t = jnp.maximum(inter, jnp.max(dmat, axis=-1, keepdims=True))
    w_intra = jnp.exp(dmat - m_t)
    w_inter = jnp.exp(inter - m_t)
    s = lax.dot_general(q, k_bf, (((1,), (1,)), ((), ())), preferred_element_type=F32) * w_intra
    num = w_inter * jnp.dot(q, c_old.astype(BF16), preferred_element_type=F32)
    num = num + jnp.dot(s.astype(BF16), v, preferred_element_type=F32)
    den = w_inter * jnp.sum(q.astype(F32) * n_old, axis=-1, keepdims=True) + jnp.sum(s, axis=-1, keepdims=True)
    return num / jnp.maximum(jnp.abs(den), jnp.exp(-m_t))


def _head_norm_gate(h, o, gain):
    mu = jnp.mean(h, axis=-1, keepdims=True)
    hc = h - mu
    var = jnp.mean(hc * hc, axis=-1, keepdims=True)
    return _sigmoid(o) * (hc * lax.rsqrt(var + EPS) * gain)


def _mlstm_prompt_kernel(q_ref, k_ref, v_ref, o_ref, gc_ref, gr_ref, km_ref, vm_ref, gcm_ref,
                         brow_ref, bcol_ref, gain_ref,
                         y_ref, cf_ref, nf_ref, mf_ref,
                         c_s, n_s, m_s):
    c_id = pl.program_id(1)
    n_c = pl.num_programs(1)
    k_scale = DK ** -0.5

    @pl.when(c_id == 0)
    def _():
        ig_c, b_c = _gate_columns(gcm_ref[...], brow_ref[...])
        for h in range(HEADS):
            hs = slice(h * DK, (h + 1) * DK)
            k = km_ref[:, hs].astype(F32) * k_scale
            c_new, n_new, m_new = _mlstm_state_update(
                k, vm_ref[:, hs], ig_c[:, h:h + 1], b_c[:, h:h + 1],
                jnp.zeros((DK, DV), F32), jnp.zeros((1, DK), F32), jnp.zeros((1, 1), F32))
            c_s[h] = c_new
            n_s[h] = n_new
            m_s[h] = jnp.broadcast_to(m_new, (8, 128))

    ig_c, b_c = _gate_columns(gc_ref[...], brow_ref[...])
    ig_r, b_r = _gate_rows(gr_ref[...], bcol_ref[...])
    for h in range(HEADS):
        hs = slice(h * DK, (h + 1) * DK)
        q = q_ref[:, hs]
        kf = k_ref[:, hs].astype(F32) * k_scale
        v = v_ref[:, hs]
        c_old = c_s[h]
        n_old = n_s[h]
        m_old = m_s[h][0:1, 0:1]
        hid = _mlstm_chunk_out(q, kf.astype(BF16), v, ig_r[h:h + 1, :], b_r[h:h + 1, :], b_c[:, h:h + 1],
                               c_old, n_old, m_old)
        y_ref[:, hs] = _head_norm_gate(hid, o_ref[:, hs], gain_ref[:, hs]).astype(y_ref.dtype)
        c_new, n_new, m_new = _mlstm_state_update(kf, v, ig_c[:, h:h + 1], b_c[:, h:h + 1], c_old, n_old, m_old)
        c_s[h] = c_new
        n_s[h] = n_new
        m_s[h] = jnp.broadcast_to(m_new, (8, 128))

    @pl.when(c_id == n_c - 1)
    def _():
        cf_ref[0] = c_s[...]
        nf_ref[0] = n_s[...]
        for h in range(HEADS):
            mf_ref[0, h:h + 1, :] = m_s[h][0:1, :]


def _mlstm_prompt(qkv, o_arr, o_col_block, gcol, grow, qkv_meta, gcol_meta, bias_row, bias_col, gain, batch, seq):
    n_c = seq // CHUNK
    width = HEADS * DK
    row = lambda b, c: b * n_c + c
    const2 = lambda b, c: (0, 0)
    return pl.pallas_call(
        _mlstm_prompt_kernel,
        grid=(batch, n_c),
        in_specs=[pl.BlockSpec((CHUNK, width), lambda b, c: (row(b, c), 0)),
                  pl.BlockSpec((CHUNK, width), lambda b, c: (row(b, c), 1)),
                  pl.BlockSpec((CHUNK, width), lambda b, c: (row(b, c), 2)),
                  pl.BlockSpec((CHUNK, width), lambda b, c: (row(b, c), o_col_block)),
                  pl.BlockSpec((CHUNK, 2 * HEADS), lambda b, c: (row(b, c), 0)),
                  pl.BlockSpec((2 * HEADS, CHUNK), lambda b, c: (0, row(b, c))),
                  pl.BlockSpec((N_META, width), lambda b, c: (0, 1)),
                  pl.BlockSpec((N_META, width), lambda b, c: (0, 2)),
                  pl.BlockSpec((N_META, 2 * HEADS), const2),
                  pl.BlockSpec((1, 2 * HEADS), const2),
                  pl.BlockSpec((2 * HEADS, 1), const2),
                  pl.BlockSpec((1, width), const2)],
        out_specs=[pl.BlockSpec((CHUNK, width), lambda b, c: (row(b, c), 0)),
                   pl.BlockSpec((1, HEADS, DK, DV), lambda b, c: (b, 0, 0, 0)),
                   pl.BlockSpec((1, HEADS, 1, DK), lambda b, c: (b, 0, 0, 0)),
                   pl.BlockSpec((1, HEADS, 128), lambda b, c: (b, 0, 0))],
        out_shape=[jax.ShapeDtypeStruct((batch * seq, width), BF16),
                   jax.ShapeDtypeStruct((batch, HEADS, DK, DV), F32),
                   jax.ShapeDtypeStruct((batch, HEADS, 1, DK), F32),
                   jax.ShapeDtypeStruct((batch, HEADS, 128), F32)],
        scratch_shapes=[pltpu.VMEM((HEADS, DK, DV), F32), pltpu.VMEM((HEADS, 1, DK), F32),
                        pltpu.VMEM((HEADS, 8, 128), F32)],
        compiler_params=_cparams(2, 32),
        name="mlstm_prompt",
    )(qkv, qkv, qkv, o_arr, gcol, grow, qkv_meta, qkv_meta, gcol_meta, bias_row, bias_col, gain)


MLSTM_SB = 8


def _mlstm_sample_kernel(qkv_ref, o_ref, g_ref, c_ref, n_ref, m_ref, brow_ref, gain_ref,
                         y_ref, co_ref, no_ref, mo_ref):
    k_scale = DK ** -0.5
    eye = _tri(DK, True) & _tri(DK, False)
    width = HEADS * DK

    def body(i, carry):
        g = g_ref[pl.ds(i, 1), :] + brow_ref[...]
        m_row = m_ref[pl.ds(i, 1), :]
        for h in range(HEADS):
            hs = slice(h * DK, (h + 1) * DK)
            ig = g[:, h:h + 1]
            fl = _log_sigmoid(g[:, HEADS + h:HEADS + h + 1])
            m_old = m_row[:, h:h + 1]
            m_new = jnp.maximum(fl + m_old, ig)
            w_in = jnp.exp(ig - m_new)
            w_ca = jnp.exp(fl + m_old - m_new)
            q = qkv_ref[pl.ds(i, 1), hs].astype(BF16)
            k = (qkv_ref[pl.ds(i, 1), width + h * DK:width + (h + 1) * DK] * k_scale).astype(BF16)
            v = qkv_ref[pl.ds(i, 1), 2 * width + h * DK:2 * width + (h + 1) * DK].astype(BF16)
            qf = q.astype(F32)
            kf = k.astype(F32)
            vf = v.astype(F32)
            c_old = c_ref[i, h]
            n_old = n_ref[i, h]
            s = jnp.sum(qf * kf, axis=-1, keepdims=True) * w_in
            qc = jnp.dot(jnp.broadcast_to(q, (8, DK)), c_old.astype(BF16), preferred_element_type=F32)[0:1, :]
            num = w_ca * qc + s * vf
            den = w_ca * jnp.sum(qf * n_old, axis=-1, keepdims=True) + s
            hid = num / jnp.maximum(jnp.abs(den), jnp.exp(-m_new))
            kw = kf * w_in
            kdiag = jnp.where(eye, jnp.broadcast_to(kw, (DK, DK)), 0.0).astype(BF16)
            kv = jnp.dot(kdiag, jnp.broadcast_to(v, (DK, DV)), preferred_element_type=F32)
            co_ref[i, h] = w_ca * c_old + kv
            no_ref[i, h] = w_ca * n_old + kw
            mo_ref[pl.ds(i, 1), h:h + 1] = m_new
            y_ref[pl.ds(i, 1), hs] = _head_norm_gate(hid, o_ref[pl.ds(i, 1), hs], gain_ref[:, hs])
        return carry

    lax.fori_loop(0, MLSTM_SB, body, 0)


def _mlstm_sample(qkv, o_arr, gcol, c0, n0, m0, bias_row, gain):
    nb = qkv.shape[0]
    width = HEADS * DK
    sb = MLSTM_SB
    const2 = lambda i: (0, 0)
    return pl.pallas_call(
        _mlstm_sample_kernel,
        grid=(nb // sb,),
        in_specs=[pl.BlockSpec((sb, 3 * width), lambda i: (i, 0)),
                  pl.BlockSpec((sb, width), lambda i: (i, 0)),
                  pl.BlockSpec((sb, 2 * HEADS), lambda i: (i, 0)),
                  pl.BlockSpec((sb, HEADS, DK, DV), lambda i: (i, 0, 0, 0)),
                  pl.BlockSpec((sb, HEADS, 1, DK), lambda i: (i, 0, 0, 0)),
                  pl.BlockSpec((sb, HEADS), lambda i: (i, 0)),
                  pl.BlockSpec((1, 2 * HEADS), const2),
                  pl.BlockSpec((1, width), const2)],
        out_specs=[pl.BlockSpec((sb, width), lambda i: (i, 0)),
                   pl.BlockSpec((sb, HEADS, DK, DV), lambda i: (i, 0, 0, 0)),
                   pl.BlockSpec((sb, HEADS, 1, DK), lambda i: (i, 0, 0, 0)),
                   pl.BlockSpec((sb, HEADS), lambda i: (i, 0))],
        out_shape=[jax.ShapeDtypeStruct((nb, width), F32),
                   jax.ShapeDtypeStruct((nb, HEADS, DK, DV), F32),
                   jax.ShapeDtypeStruct((nb, HEADS, 1, DK), F32),
                   jax.ShapeDtypeStruct((nb, HEADS), F32)],
        compiler_params=_cparams(1, 48),
        name="mlstm_sample",
    )(qkv, o_arr, gcol, c0, n0, m0, bias_row, gain)


def _merge_kernel(ya_ref, yb_ref, ga_ref, gb_ref, x_ref, wglu_ref, bglu_ref, wa_ref, wb_ref, wout_ref,
                  g1_ref, b1_ref, o_ref):
    g = _gelu_tanh(ya_ref[...])
    z = jnp.dot(g.astype(BF16), wglu_ref[...], preferred_element_type=F32) + bglu_ref[...]
    out_a = g * _sigmoid(z)
    up_a = jnp.dot(out_a.astype(BF16), wa_ref[...], preferred_element_type=F32)
    up_b = jnp.dot(yb_ref[...], wb_ref[...], preferred_element_type=F32)
    mix = _sigmoid(ga_ref[...]) * up_a + _sigmoid(gb_ref[...]) * up_b
    mo = jnp.dot(mix.astype(BF16), wout_ref[...], preferred_element_type=F32)
    o_ref[...] = _layernorm_rows(ALPHA * x_ref[...] + mo, g1_ref[...], b1_ref[...])


def _merge(ya, yb, gab, x, w_glu, b_glu, w_a_up, w_b_up, w_out, ln_g, ln_b, tm):
    r = x.shape[0]
    const2 = lambda i: (0, 0)
    resident = functools.partial(pl.BlockSpec, index_map=const2, pipeline_mode=pl.Buffered(1))
    return pl.pallas_call(
        _merge_kernel,
        grid=(r // tm,),
        in_specs=[pl.BlockSpec((tm, SSM_WIDTH), lambda i: (i, 0)),
                  pl.BlockSpec((tm, HEADS * DV), lambda i: (i, 0)),
                  pl.BlockSpec((tm, D_MODEL), lambda i: (i, 0)),
                  pl.BlockSpec((tm, D_MODEL), lambda i: (i, 1)),
                  pl.BlockSpec((tm, D_MODEL), lambda i: (i, 0)),
                  resident((SSM_WIDTH, SSM_WIDTH)),
                  resident((1, SSM_WIDTH)),
                  resident((SSM_WIDTH, D_MODEL)),
                  resident((HEADS * DV, D_MODEL)),
                  resident((D_MODEL, D_MODEL)),
                  resident((1, D_MODEL)),
                  resident((1, D_MODEL))],
        out_specs=pl.BlockSpec((tm, D_MODEL), lambda i: (i, 0)),
        out_shape=jax.ShapeDtypeStruct((r, D_MODEL), F32),
        compiler_params=_cparams(1, 52),
        name="merge_ln1",
    )(ya, yb, gab, gab, x, w_glu, b_glu, w_a_up, w_b_up, w_out, ln_g, ln_b)


def _ffn_kernel(x_ref, wg_ref, wu_ref, wd_ref, g2_ref, b2_ref, o_ref, xb_s, acc_s):
    f_id = pl.program_id(1)
    n_f = pl.num_programs(1)

    @pl.when(f_id == 0)
    def _():
        xb_s[...] = x_ref[...].astype(BF16)
        acc_s[...] = jnp.zeros_like(acc_s)

    xb = xb_s[...]
    hg = jnp.dot(xb, wg_ref[...], preferred_element_type=F32)
    hu = jnp.dot(xb, wu_ref[...], preferred_element_type=F32)
    act = (hg * _sigmoid(hg)) * hu
    acc_s[...] += jnp.dot(act.astype(BF16), wd_ref[...], preferred_element_type=F32)

    @pl.when(f_id == n_f - 1)
    def _():
        o_ref[...] = _layernorm_rows(ALPHA * x_ref[...] + acc_s[...], g2_ref[...], b2_ref[...])


def _ffn(x, w_gate, w_up, w_down, ln_g, ln_b, tm, tf):
    r = x.shape[0]
    const2 = lambda i, f: (0, 0)
    return pl.pallas_call(
        _ffn_kernel,
        grid=(r // tm, D_FF // tf),
        in_specs=[pl.BlockSpec((tm, D_MODEL), lambda i, f: (i, 0)),
                  pl.BlockSpec((D_MODEL, tf), lambda i, f: (0, f)),
                  pl.BlockSpec((D_MODEL, tf), lambda i, f: (0, f)),
                  pl.BlockSpec((tf, D_MODEL), lambda i, f: (f, 0)),
                  pl.BlockSpec((1, D_MODEL), const2),
                  pl.BlockSpec((1, D_MODEL), const2)],
        out_specs=pl.BlockSpec((tm, D_MODEL), lambda i, f: (i, 0)),
        out_shape=jax.ShapeDtypeStruct((r, D_MODEL), F32),
        scratch_shapes=[pltpu.VMEM((tm, D_MODEL), BF16), pltpu.VMEM((tm, D_MODEL), F32)],
        compiler_params=_cparams(2, 48),
        name="ffn_ln2",
    )(x, w_gate, w_up, w_down, ln_g, ln_b)


def kernel(x_prompt, x_sample, state_ssm_re, state_ssm_im, state_mlstm_c, state_mlstm_n, state_mlstm_m,
           meta_tokens, w_in, b_if, ssm_a_re, ssm_a_im, ssm_log_dt, ssm_b_re, ssm_b_im, ssm_c_re, ssm_c_im,
           ssm_d, w_glu, b_glu, w_a_up, mh_gain, w_b_up, w_out, ln1_g, ln1_b, w_gate, w_up, w_down,
           ln2_g, ln2_b):
    batch, seq, _ = x_prompt.shape
    nb = x_sample.shape[0]
    width = HEADS * DK

    w_in_b = w_in[0].astype(BF16)
    c_u, c_q, c_o, c_if, c_gab = 0, SSM_WIDTH, SSM_WIDTH + 3 * width, SSM_WIDTH + 4 * width, SSM_WIDTH + 4 * width + 2 * HEADS
    w_uo = jnp.concatenate([w_in_b[:, c_u:c_q], w_in_b[:, c_o:c_if]], axis=1)
    w_qkv = w_in_b[:, c_q:c_o]
    w_if = w_in_b[:, c_if:c_gab]
    w_gab = w_in_b[:, c_gab:]
    w_glu_b = w_glu[0].astype(BF16)
    w_a_b = w_a_up[0].astype(BF16)
    w_b_b = w_b_up[0].astype(BF16)
    w_out_b = w_out[0].astype(BF16)
    w_gate_b = w_gate[0].astype(BF16)
    w_up_b = w_up[0].astype(BF16)
    w_down_b = w_down[0].astype(BF16)
    bias_row = b_if[0].reshape(1, 2 * HEADS)
    bias_col = b_if[0].reshape(2 * HEADS, 1)
    gain = mh_gain[0].reshape(1, width)
    d_skip = ssm_d[0].reshape(1, SSM_WIDTH)
    b_glu_r = b_glu[0].reshape(1, SSM_WIDTH)
    ln1 = (ln1_g[0].reshape(1, D_MODEL), ln1_b[0].reshape(1, D_MODEL))
    ln2 = (ln2_g[0].reshape(1, D_MODEL), ln2_b[0].reshape(1, D_MODEL))
    abar_re, abar_im, b_blk, cre_blk, cimn_blk = _s5_discretise(
        ssm_a_re[0], ssm_a_im[0], ssm_log_dt[0], ssm_b_re[0], ssm_b_im[0], ssm_c_re[0], ssm_c_im[0])

    xp = x_prompt.reshape(batch * seq, D_MODEL)
    xp_b = xp.astype(BF16)
    xs = x_sample.reshape(nb, D_MODEL)
    xsm_b = jnp.concatenate([xs, meta_tokens], axis=0).astype(BF16)
    n_small = nb + N_META

    uo_p = _matmul(xp_b, w_uo, F32, 512, 1024, "proj_uo_prompt")
    qkv_p = _matmul(xp_b, w_qkv, BF16, 512, 1024, "proj_qkv_prompt")
    gab_p = _matmul(xp_b, w_gab, F32, 512, 1024, "proj_gab_prompt")
    gcol_p, grow_p = _gates_matmul(xp_b, w_if, w_if.T, 1024, "proj_gates_prompt")

    uo_s = _matmul(xsm_b, w_uo, F32, n_small, 1024, "proj_uo_small")
    qkv_s = _matmul(xsm_b, w_qkv, F32, n_small, 1024, "proj_qkv_small")
    gab_s = _matmul(xsm_b, w_gab, F32, n_small, 1024, "proj_gab_small")
    gcol_s, _ = _gates_matmul(xsm_b, w_if, w_if.T, n_small, "proj_gates_small")

    u_meta = uo_s[nb:, :SSM_WIDTH]
    qkv_meta = qkv_s[nb:].astype(BF16)
    gcol_meta = gcol_s[nb:]

    ya_p, pf_re, pf_im = _s5_prompt(uo_p, 0, u_meta, abar_re, abar_im, b_blk, cre_blk, cimn_blk, d_skip, batch, seq)
    yb_p, pc, pn, pm = _mlstm_prompt(qkv_p, uo_p, 1, gcol_p, grow_p, qkv_meta, gcol_meta, bias_row, bias_col,
                                     gain, batch, seq)
    x1_p = _merge(ya_p, yb_p, gab_p, xp, w_glu_b, b_glu_r, w_a_b, w_b_b, w_out_b, *ln1, tm=256)
    y_p = _ffn(x1_p, w_gate_b, w_up_b, w_down_b, *ln2, tm=512, tf=512)

    ya_s, sf_re, sf_im = _s5_sample(uo_s[:nb, :SSM_WIDTH], state_ssm_re[0].reshape(nb, SSM_FLAT),
                                    state_ssm_im[0].reshape(nb, SSM_FLAT), abar_re, abar_im,
                                    b_blk, cre_blk, cimn_blk, d_skip)
    yb_s, sc, sn, sm = _mlstm_sample(qkv_s[:nb], uo_s[:nb, SSM_WIDTH:], gcol_s[:nb], state_mlstm_c[0],
                                     state_mlstm_n[0].reshape(nb, HEADS, 1, DK), state_mlstm_m[0], bias_row, gain)
    x1_s = _merge(ya_s, yb_s.astype(BF16), gab_s[:nb], xs, w_glu_b, b_glu_r, w_a_b, w_b_b, w_out_b, *ln1, tm=nb)
    y_s = _ffn(x1_s, w_gate_b, w_up_b, w_down_b, *ln2, tm=nb, tf=512)

    return (y_p.reshape(batch, seq, D_MODEL),
            y_s.reshape(nb, 1, D_MODEL),
            pf_re.reshape(1, batch, SSM_GROUPS, SSM_STATE),
            pf_im.reshape(1, batch, SSM_GROUPS, SSM_STATE),
            pc.reshape(1, batch, HEADS, DK, DV),
            pn.reshape(1, batch, HEADS, DK),
            pm[:, :, 0].reshape(1, batch, HEADS),
            sf_re.reshape(1, nb, SSM_GROUPS, SSM_STATE),
            sf_im.reshape(1, nb, SSM_GROUPS, SSM_STATE),
            sc.reshape(1, nb, HEADS, DK, DV),
            sn.reshape(1, nb, HEADS, DK),
            sm.reshape(1, nb, HEADS))
```

```python
import functools

import jax
import jax.numpy as jnp
import numpy as np
from jax import lax
from jax.experimental import pallas as pl
from jax.experimental.pallas import tpu as pltpu

F32 = jnp.float32
BF16 = jnp.bfloat16

D_MODEL = 2048
N_META = 16
SSM_WIDTH = 1024
SSM_GROUP = 16
SSM_GROUPS = 64
SSM_STATE = 64
SSM_FLAT = SSM_GROUPS * SSM_STATE
SSM_JBLK = 4
HEADS = 4
DK = 256
DV = 256
CHUNK = 128
D_FF = 5632
EPS = 1e-5
ALPHA = 2.0 ** 0.25

S5_TC = 256
S5_NSEG = 8
S5_SEG = S5_TC // S5_NSEG
S5_LW = 512

V7X_SCOPED_VMEM_MAX_BYTES = 60000 * 1024


def _cparams(n_axes, vmem_mb):
    return pltpu.CompilerParams(
        dimension_semantics=("arbitrary",) * n_axes,
        vmem_limit_bytes=min(vmem_mb * 1024 * 1024, V7X_SCOPED_VMEM_MAX_BYTES),
    )


def _sigmoid(x):
    return 1.0 / (1.0 + jnp.exp(-x))


def _log_sigmoid(x):
    return jnp.minimum(x, 0.0) - jnp.log(1.0 + jnp.exp(-jnp.abs(x)))


def _gelu_tanh(x):
    c = 0.7978845608028654
    return 0.5 * x * (1.0 + jnp.tanh(c * (x + 0.044715 * (x * x * x))))


def _layernorm_rows(x, g, b):
    mu = jnp.mean(x, axis=-1, keepdims=True)
    xc = x - mu
    var = jnp.mean(xc * xc, axis=-1, keepdims=True)
    return xc * lax.rsqrt(var + EPS) * g + b


PROJ_TN = 1024
TAIL_COL0 = SSM_WIDTH + 4 * HEADS * DK
TAIL_SHIFT = 2 * HEADS


def _proj_kernel(x_ref, w_ref, o_ref, wb_s):
    @pl.when(pl.program_id(1) == 0)
    def _():
        wb_s[...] = w_ref[...].astype(BF16)

    o_ref[...] = jnp.dot(x_ref[...].astype(BF16), wb_s[...], preferred_element_type=F32).astype(o_ref.dtype)


def _proj_gates_kernel(x_ref, w_ref, wg_ref, wgt_ref, o_ref, g_ref, gt_ref, wb_s):
    @pl.when(pl.program_id(1) == 0)
    def _():
        wb_s[...] = w_ref[...].astype(BF16)

    xb = x_ref[...].astype(BF16)
    o_ref[...] = jnp.dot(xb, wb_s[...], preferred_element_type=F32).astype(o_ref.dtype)
    g_ref[...] = jnp.dot(xb, wg_ref[...], preferred_element_type=F32)
    gt_ref[...] = lax.dot_general(wgt_ref[...], xb, (((1,), (1,)), ((), ())), preferred_element_type=F32)


def _proj_tail_kernel(x_ref, wa_ref, wn_ref, o_ref, wb_s):
    @pl.when(pl.program_id(1) == 0)
    def _():
        w = jnp.concatenate([wa_ref[...], wn_ref[...]], axis=1)
        wb_s[...] = w[:, TAIL_SHIFT:TAIL_SHIFT + PROJ_TN].astype(BF16)

    o_ref[...] = jnp.dot(x_ref[...].astype(BF16), wb_s[...], preferred_element_type=F32).astype(o_ref.dtype)


def _proj(x, w3, col_blocks, out_dtype, tm, name, gates=None):
    r, k = x.shape
    first, step, count = col_blocks
    tn = PROJ_TN
    in_specs = [pl.BlockSpec((tm, k), lambda j, i: (i, 0)),
                pl.BlockSpec((None, k, tn), lambda j, i: (0, 0, first + step * j))]
    out_specs = pl.BlockSpec((tm, tn), lambda j, i: (i, j))
    out_shape = jax.ShapeDtypeStruct((r, count * tn), out_dtype)
    args = (x, w3)
    body = _proj_kernel
    if gates is not None:
        ng = gates[0].shape[1]
        in_specs += [pl.BlockSpec((k, ng), lambda j, i: (0, 0)), pl.BlockSpec((ng, k), lambda j, i: (0, 0))]
        out_specs = [out_specs, pl.BlockSpec((tm, ng), lambda j, i: (i, 0)), pl.BlockSpec((ng, tm), lambda j, i: (0, i))]
        out_shape = [out_shape, jax.ShapeDtypeStruct((r, ng), F32), jax.ShapeDtypeStruct((ng, r), F32)]
        args = (x, w3) + tuple(gates)
        body = _proj_gates_kernel
    return pl.pallas_call(
        body,
        grid=(count, r // tm),
        in_specs=in_specs,
        out_specs=out_specs,
        out_shape=out_shape,
        scratch_shapes=[pltpu.VMEM((k, tn), BF16)],
        compiler_params=_cparams(2, 52),
        name=name,
    )(*args)


def _proj_tail(x, w3, count, tm, name):
    r, k = x.shape
    tn = PROJ_TN
    first = TAIL_COL0 // tn
    first_next = (TAIL_COL0 + tn) // 128
    return pl.pallas_call(
        _proj_tail_kernel,
        grid=(count, r // tm),
        in_specs=[pl.BlockSpec((tm, k), lambda j, i: (i, 0)),
                  pl.BlockSpec((None, k, tn), lambda j, i: (0, 0, first + j)),
                  pl.BlockSpec((None, k, 128), lambda j, i: (0, 0, first_next + (tn // 128) * j))],
        out_specs=pl.BlockSpec((tm, tn), lambda j, i: (i, j)),
        out_shape=jax.ShapeDtypeStruct((r, count * tn), F32),
        scratch_shapes=[pltpu.VMEM((k, tn), BF16)],
        compiler_params=_cparams(2, 52),
        name=name,
    )(x, w3, w3)


def _s5_discretise(a_re, a_im, log_dt, b_re, b_im, c_re, c_im):
    dt = jnp.exp(log_dt)
    e = jnp.exp(a_re * dt)
    abar_re = e * jnp.cos(a_im * dt)
    abar_im = e * jnp.sin(a_im * dt)
    nr = abar_re - 1.0
    ni = abar_im
    den = a_re * a_re + a_im * a_im
    coef_re = (nr * a_re + ni * a_im) / den
    coef_im = (ni * a_re - nr * a_im) / den
    bb_re = coef_re[..., None] * b_re - coef_im[..., None] * b_im
    bb_im = coef_re[..., None] * b_im + coef_im[..., None] * b_re
    gpb = SSM_GROUPS // SSM_JBLK

    def b_rows(bb):
        return bb.reshape(SSM_JBLK, gpb, SSM_STATE, SSM_GROUP).transpose(0, 1, 3, 2).reshape(SSM_JBLK, 256, SSM_STATE)

    def c_rows(cc):
        return cc.reshape(SSM_JBLK, gpb, SSM_GROUP, SSM_STATE).transpose(0, 1, 3, 2).reshape(SSM_JBLK, 1024, SSM_GROUP)

    bt = jnp.concatenate([jnp.tile(b_rows(bb_re), (1, 1, 2)), jnp.tile(b_rows(bb_im), (1, 1, 2))], axis=-1)
    ct = jnp.concatenate([jnp.tile(c_rows(c_re), (1, 1, 8)), jnp.tile(c_rows(-c_im), (1, 1, 8))], axis=-1)
    return abar_re.reshape(1, SSM_FLAT), abar_im.reshape(1, SSM_FLAT), bt, ct


def _s5_expand_weights(bt_ref, ct_ref, bw_s, cwr_s, cwi_s):
    rb = lax.broadcasted_iota(jnp.int32, (256, 2048), 0)
    cb = lax.broadcasted_iota(jnp.int32, (256, 2048), 1)
    bmask = (rb >> 4) == ((cb & 1023) >> 6)
    rc = lax.broadcasted_iota(jnp.int32, (1024, 256), 0)
    cc = lax.broadcasted_iota(jnp.int32, (1024, 256), 1)
    cmask = (rc >> 6) == (cc >> 4)
    for j in range(SSM_JBLK):
        bt = bt_ref[j]
        wide = jnp.concatenate([bt[:, :128]] * 8 + [bt[:, 128:]] * 8, axis=1)
        bw_s[j] = jnp.where(bmask, wide, 0.0).astype(BF16)
        ct = ct_ref[j]
        cwr_s[j] = jnp.where(cmask, jnp.concatenate([ct[:, :128]] * 2, axis=1), 0.0).astype(BF16)
        cwi_s[j] = jnp.where(cmask, jnp.concatenate([ct[:, 128:]] * 2, axis=1), 0.0).astype(BF16)


def _cmul_add(ar, ai, hr, hi, xr, xi):
    return ar * hr - ai * hi + xr, ar * hi + ai * hr + xi


def _s5_project_in(u, b_ref, hre_ref, him_ref, rows):
    for j in range(SSM_JBLK):
        bu = jnp.dot(u[:, j * 256:(j + 1) * 256], b_ref[j], preferred_element_type=F32)
        hre_ref[0:rows, j * 1024:(j + 1) * 1024] = bu[:, :1024]
        him_ref[0:rows, j * 1024:(j + 1) * 1024] = bu[:, 1024:]


def _s5_project_out(hre_ref, him_ref, cre_ref, cimn_ref, rows):
    ys = []
    for j in range(SSM_JBLK):
        hr = hre_ref[0:rows, j * 1024:(j + 1) * 1024].astype(BF16)
        hi = him_ref[0:rows, j * 1024:(j + 1) * 1024].astype(BF16)
        y = jnp.dot(hr, cre_ref[j], preferred_element_type=F32)
        ys.append(y + jnp.dot(hi, cimn_ref[j], preferred_element_type=F32))
    return jnp.concatenate(ys, axis=1)


def _s5_prompt_kernel(u_ref, um_ref, perm_ref, permt_ref, bt_ref, ct_ref, are_ref, aim_ref, d_ref,
                      y_ref, fre_ref, fim_ref,
                      hre_ref, him_ref, pre_ref, pim_ref, cre_s, cim_s, hsre_ref, hsim_ref,
                      b_ref, cre_ref, cimn_ref):
    b_id = pl.program_id(0)
    c_id = pl.program_id(1)
    n_c = pl.num_programs(1)

    @pl.when(jnp.logical_and(b_id == 0, c_id == 0))
    def _():
        _s5_expand_weights(bt_ref, ct_ref, b_ref, cre_ref, cimn_ref)
        ar = are_ref[...]
        ai = aim_ref[...]
        pre_ref[0:1, :] = ar
        pim_ref[0:1, :] = ai

        def body(i, carry):
            pr, pi = carry
            nr = ar * pr - ai * pi
            ni = ar * pi + ai * pr
            pre_ref[pl.ds(i, 1), :] = nr
            pim_ref[pl.ds(i, 1), :] = ni
            return nr, ni

        lax.fori_loop(1, S5_SEG, body, (ar, ai))

    @pl.when(c_id == 0)
    def _():
        _s5_project_in(um_ref[...].astype(BF16), b_ref, hre_ref, him_ref, N_META)
        ar = are_ref[...]
        ai = aim_ref[...]
        hr = jnp.zeros((1, SSM_FLAT), F32)
        hi = jnp.zeros((1, SSM_FLAT), F32)
        for t in range(N_META):
            hr, hi = _cmul_add(ar, ai, hr, hi, hre_ref[t:t + 1, :], him_ref[t:t + 1, :])
        cre_s[...] = hr
        cim_s[...] = hi

    u = u_ref[...]
    u_perm = jnp.dot(perm_ref[...], u.astype(BF16), preferred_element_type=F32).astype(BF16)
    _s5_project_in(u_perm, b_ref, hre_ref, him_ref, S5_TC)

    full = (S5_NSEG, S5_LW)
    for lg in range(SSM_FLAT // S5_LW):
        lanes = slice(lg * S5_LW, (lg + 1) * S5_LW)
        ar = jnp.broadcast_to(are_ref[:, lanes], full)
        ai = jnp.broadcast_to(aim_ref[:, lanes], full)

        def local_step(k, carry):
            rows = pl.ds(pl.multiple_of(k * S5_NSEG, S5_NSEG), S5_NSEG)
            nr, ni = _cmul_add(ar, ai, carry[0], carry[1], hre_ref[rows, lanes], him_ref[rows, lanes])
            hre_ref[rows, lanes] = nr
            him_ref[rows, lanes] = ni
            return nr, ni

        er, ei = lax.fori_loop(0, S5_SEG, local_step, (jnp.zeros(full, F32), jnp.zeros(full, F32)), unroll=4)

        asr = pre_ref[S5_SEG - 1:S5_SEG, lanes]
        asi = pim_ref[S5_SEG - 1:S5_SEG, lanes]
        hr = cre_s[:, lanes]
        hi = cim_s[:, lanes]
        for j in range(S5_NSEG):
            hsre_ref[j:j + 1, lanes] = hr
            hsim_ref[j:j + 1, lanes] = hi
            hr, hi = _cmul_add(asr, asi, hr, hi, er[j:j + 1, :], ei[j:j + 1, :])
        cre_s[:, lanes] = hr
        cim_s[:, lanes] = hi

        sr = hsre_ref[:, lanes]
        si = hsim_ref[:, lanes]

        def fix_step(k, carry):
            rows = pl.ds(pl.multiple_of(k * S5_NSEG, S5_NSEG), S5_NSEG)
            pr = jnp.broadcast_to(pre_ref[pl.ds(k, 1), lanes], full)
            pi = jnp.broadcast_to(pim_ref[pl.ds(k, 1), lanes], full)
            hre_ref[rows, lanes] = hre_ref[rows, lanes] + (pr * sr - pi * si)
            him_ref[rows, lanes] = him_ref[rows, lanes] + (pr * si + pi * sr)
            return carry

        lax.fori_loop(0, S5_SEG, fix_step, 0, unroll=4)

    y_perm = _s5_project_out(hre_ref, him_ref, cre_ref, cimn_ref, S5_TC)
    permt = permt_ref[...]
    y = sum(jnp.dot(permt, piece, preferred_element_type=F32) for piece in _split3(y_perm))
    y_ref[...] = y + d_ref[...] * u

    @pl.when(c_id == n_c - 1)
    def _():
        fre_ref[0] = cre_s[...]
        fim_ref[0] = cim_s[...]


def _s5_weight_scratch():
    return [pltpu.VMEM((SSM_JBLK, 256, 2048), BF16), pltpu.VMEM((SSM_JBLK, 1024, 256), BF16),
            pltpu.VMEM((SSM_JBLK, 1024, 256), BF16)]


def _s5_prompt(u_arr, u_col_block, u_meta, abar_re, abar_im, bt, ct, d_skip, batch, seq):
    n_c = seq // S5_TC
    const3 = lambda b, c: (0, 0, 0)
    const2 = lambda b, c: (0, 0)
    rows = np.arange(S5_TC)
    perm_np = np.zeros((S5_TC, S5_TC), np.float32)
    perm_np[rows, (rows % S5_NSEG) * S5_SEG + rows // S5_NSEG] = 1.0
    perm = jnp.asarray(perm_np, dtype=BF16)
    permt = jnp.asarray(perm_np.T, dtype=BF16)
    return pl.pallas_call(
        _s5_prompt_kernel,
        grid=(batch, n_c),
        in_specs=[pl.BlockSpec((S5_TC, SSM_WIDTH), lambda b, c: (b * n_c + c, u_col_block)),
                  pl.BlockSpec((N_META, SSM_WIDTH), const2),
                  pl.BlockSpec((S5_TC, S5_TC), const2),
                  pl.BlockSpec((S5_TC, S5_TC), const2),
                  pl.BlockSpec((SSM_JBLK, 256, 256), const3),
                  pl.BlockSpec((SSM_JBLK, 1024, 256), const3),
                  pl.BlockSpec((1, SSM_FLAT), const2),
                  pl.BlockSpec((1, SSM_FLAT), const2),
                  pl.BlockSpec((1, SSM_WIDTH), const2)],
        out_specs=[pl.BlockSpec((S5_TC, SSM_WIDTH), lambda b, c: (b * n_c + c, 0)),
                   pl.BlockSpec((1, 1, SSM_FLAT), lambda b, c: (b, 0, 0)),
                   pl.BlockSpec((1, 1, SSM_FLAT), lambda b, c: (b, 0, 0))],
        out_shape=[jax.ShapeDtypeStruct((batch * seq, SSM_WIDTH), F32),
                   jax.ShapeDtypeStruct((batch, 1, SSM_FLAT), F32),
                   jax.ShapeDtypeStruct((batch, 1, SSM_FLAT), F32)],
        scratch_shapes=[pltpu.VMEM((S5_TC, SSM_FLAT), F32), pltpu.VMEM((S5_TC, SSM_FLAT), F32),
                        pltpu.VMEM((S5_SEG, SSM_FLAT), F32), pltpu.VMEM((S5_SEG, SSM_FLAT), F32),
                        pltpu.VMEM((1, SSM_FLAT), F32), pltpu.VMEM((1, SSM_FLAT), F32),
                        pltpu.VMEM((S5_NSEG, SSM_FLAT), F32), pltpu.VMEM((S5_NSEG, SSM_FLAT), F32)]
                       + _s5_weight_scratch(),
        compiler_params=_cparams(2, 48),
        name="s5_prompt",
    )(u_arr, u_meta, perm, permt, bt, ct, abar_re, abar_im, d_skip)


def _s5_sample_kernel(u_ref, h0re_ref, h0im_ref, bt_ref, ct_ref, are_ref, aim_ref, d_ref,
                      y_ref, hre_ref, him_ref, b_ref, cre_ref, cimn_ref):
    rows = u_ref.shape[0]
    _s5_expand_weights(bt_ref, ct_ref, b_ref, cre_ref, cimn_ref)
    u = u_ref[...]
    _s5_project_in(u.astype(BF16), b_ref, hre_ref, him_ref, rows)
    nr, ni = _cmul_add(are_ref[...], aim_ref[...], h0re_ref[...], h0im_ref[...], hre_ref[...], him_ref[...])
    hre_ref[...] = nr
    him_ref[...] = ni
    y_ref[...] = _s5_project_out(hre_ref, him_ref, cre_ref, cimn_ref, rows) + d_ref[...] * u


def _s5_sample(u, h0_re, h0_im, abar_re, abar_im, bt, ct, d_skip):
    rows = u.shape[0]
    return pl.pallas_call(
        _s5_sample_kernel,
        out_shape=[jax.ShapeDtypeStruct((rows, SSM_WIDTH), F32),
                   jax.ShapeDtypeStruct((rows, SSM_FLAT), F32),
                   jax.ShapeDtypeStruct((rows, SSM_FLAT), F32)],
        scratch_shapes=_s5_weight_scratch(),
        compiler_params=_cparams(0, 48),
        name="s5_sample",
    )(u, h0_re, h0_im, bt, ct, abar_re, abar_im, d_skip)


def _split3(x):
    hi = x.astype(BF16)
    r1 = x - hi.astype(F32)
    mid = r1.astype(BF16)
    lo = (r1 - mid.astype(F32)).astype(BF16)
    return hi, mid, lo


def _tri(n, lower):
    r = lax.broadcasted_iota(jnp.int32, (n, n), 0)
    c = lax.broadcasted_iota(jnp.int32, (n, n), 1)
    return (r >= c) if lower else (r <= c)


def _gate_columns(gcol, bias_row):
    n = gcol.shape[0]
    g = gcol + bias_row
    fl = _log_sigmoid(g[:, HEADS:])
    tril = jnp.where(_tri(n, True), 1.0, 0.0).astype(BF16)
    b = sum(jnp.dot(tril, p, preferred_element_type=F32) for p in _split3(fl))
    return g[:, :HEADS], b


def _gate_rows(grow, bias_col):
    n = grow.shape[1]
    g = grow + bias_col
    fl = _log_sigmoid(g[HEADS:, :])
    triu = jnp.where(_tri(n, False), 1.0, 0.0).astype(BF16)
    b = sum(jnp.dot(p, triu, preferred_element_type=F32) for p in _split3(fl))
    return g[:HEADS, :], b


def _mlstm_state_update(k, v, ig_col, b_col, c_old, n_old, m_old):
    n_rows = k.shape[0]
    b_last = b_col[n_rows - 1:n_rows, :]
    e_col = b_last - b_col + ig_col
    m_new = jnp.maximum(b_last + m_old, jnp.max(e_col, axis=0, keepdims=True))
    w_end = jnp.exp(e_col - m_new)
    carry = jnp.exp(b_last + m_old - m_new)
    kw = k * w_end
    kv = lax.dot_general(kw.astype(BF16), v, (((0,), (0,)), ((), ())), preferred_element_type=F32)
    c_new = carry * c_old + kv
    n_new = carry * n_old + jnp.sum(kw, axis=0, keepdims=True)
    return c_new, n_new, m_new


def _mlstm_chunk_out(q, k_bf, v, ig_row, b_row, b_col, c_old, n_old, m_old):
    n_rows = q.shape[0]
    dmat = jnp.where(_tri(n_rows, True), b_col + (ig_row - b_row), -jnp.inf)
    inter = b_col + m_old
    m_t = jnp.maximum(inter, jnp.max(dmat, axis=-1, keepdims=True))
    w_intra = jnp.exp(dmat - m_t)
    w_inter = jnp.exp(inter - m_t)
    s = lax.dot_general(q, k_bf, (((1,), (1,)), ((), ())), preferred_element_type=F32) * w_intra
    num = w_inter * jnp.dot(q, c_old.astype(BF16), preferred_element_type=F32)
    num = num + jnp.dot(s.astype(BF16), v, preferred_element_type=F32)
    den = w_inter * jnp.sum(q.astype(F32) * n_old, axis=-1, keepdims=True) + jnp.sum(s, axis=-1, keepdims=True)
    return num / jnp.maximum(jnp.abs(den), jnp.exp(-m_t))


def _head_norm_gate(h, o, gain):
    mu = jnp.mean(h, axis=-1, keepdims=True)
    hc = h - mu
    var = jnp.mean(hc * hc, axis=-1, keepdims=True)
    return _sigmoid(o) * (hc * lax.rsqrt(var + EPS) * gain)


def _mlstm_prompt_kernel(q_ref, k_ref, v_ref, o_ref, gc_ref, gr_ref, km_ref, vm_ref, gcm_ref,
                         brow_ref, bcol_ref, gain_ref,
                         y_ref, cf_ref, nf_ref, mf_ref,
                         c_s, n_s, m_s):
    c_id = pl.program_id(1)
    n_c = pl.num_programs(1)
    k_scale = DK ** -0.5

    @pl.when(c_id == 0)
    def _():
        ig_c, b_c = _gate_columns(gcm_ref[...], brow_ref[...])
        for h in range(HEADS):
            hs = slice(h * DK, (h + 1) * DK)
            k = km_ref[:, hs].astype(F32) * k_scale
            c_new, n_new, m_new = _mlstm_state_update(
                k, vm_ref[:, hs], ig_c[:, h:h + 1], b_c[:, h:h + 1],
                jnp.zeros((DK, DV), F32), jnp.zeros((1, DK), F32), jnp.zeros((1, 1), F32))
            c_s[h] = c_new
            n_s[h] = n_new
            m_s[h] = jnp.broadcast_to(m_new, (8, 128))

    ig_c, b_c = _gate_columns(gc_ref[...], brow_ref[...])
    ig_r, b_r = _gate_rows(gr_ref[...], bcol_ref[...])
    for h in range(HEADS):
        hs = slice(h * DK, (h + 1) * DK)
        q = q_ref[:, hs]
        kf = k_ref[:, hs].astype(F32) * k_scale
        v = v_ref[:, hs]
        c_old = c_s[h]
        n_old = n_s[h]
        m_old = m_s[h][0:1, 0:1]
        hid = _mlstm_chunk_out(q, kf.astype(BF16), v, ig_r[h:h + 1, :], b_r[h:h + 1, :], b_c[:, h:h + 1],
                               c_old, n_old, m_old)
        y_ref[:, hs] = _head_norm_gate(hid, o_ref[:, hs], gain_ref[:, hs]).astype(y_ref.dtype)
        c_new, n_new, m_new = _mlstm_state_update(kf, v, ig_c[:, h:h + 1], b_c[:, h:h + 1], c_old, n_old, m_old)
        c_s[h] = c_new
        n_s[h] = n_new
        m_s[h] = jnp.broadcast_to(m_new, (8, 128))

    @pl.when(c_id == n_c - 1)
    def _():
        cf_ref[0] = c_s[...]
        nf_ref[0] = n_s[...]
        for h in range(HEADS):
            mf_ref[0, h:h + 1, :] = m_s[h][0:1, :]


def _mlstm_prompt(qkv, o_arr, o_col_block, gcol, grow, qkv_meta, gcol_meta, bias_row, bias_col, gain, batch, seq):
    n_c = seq // CHUNK
    width = HEADS * DK
    row = lambda b, c: b * n_c + c
    const2 = lambda b, c: (0, 0)
    return pl.pallas_call(
        _mlstm_prompt_kernel,
        grid=(batch, n_c),
        in_specs=[pl.BlockSpec((CHUNK, width), lambda b, c: (row(b, c), 0)),
                  pl.BlockSpec((CHUNK, width), lambda b, c: (row(b, c), 1)),
                  pl.BlockSpec((CHUNK, width), lambda b, c: (row(b, c), 2)),
                  pl.BlockSpec((CHUNK, width), lambda b, c: (row(b, c), o_col_block)),
                  pl.BlockSpec((CHUNK, 2 * HEADS), lambda b, c: (row(b, c), 0)),
                  pl.BlockSpec((2 * HEADS, CHUNK), lambda b, c: (0, row(b, c))),
                  pl.BlockSpec((N_META, width), lambda b, c: (0, 1)),
                  pl.BlockSpec((N_META, width), lambda b, c: (0, 2)),
                  pl.BlockSpec((N_META, 2 * HEADS), const2),
                  pl.BlockSpec((1, 2 * HEADS), const2),
                  pl.BlockSpec((2 * HEADS, 1), const2),
                  pl.BlockSpec((1, width), const2)],
        out_specs=[pl.BlockSpec((CHUNK, width), lambda b, c: (row(b, c), 0)),
                   pl.BlockSpec((1, HEADS, DK, DV), lambda b, c: (b, 0, 0, 0)),
                   pl.BlockSpec((1, HEADS, 1, DK), lambda b, c: (b, 0, 0, 0)),
                   pl.BlockSpec((1, HEADS, 128), lambda b, c: (b, 0, 0))],
        out_shape=[jax.ShapeDtypeStruct((batch * seq, width), BF16),
                   jax.ShapeDtypeStruct((batch, HEADS, DK, DV), F32),
                   jax.ShapeDtypeStruct((batch, HEADS, 1, DK), F32),
                   jax.ShapeDtypeStruct((batch, HEADS, 128), F32)],
        scratch_shapes=[pltpu.VMEM((HEADS, DK, DV), F32), pltpu.VMEM((HEADS, 1, DK), F32),
                        pltpu.VMEM((HEADS, 8, 128), F32)],
        compiler_params=_cparams(2, 32),
        name="mlstm_prompt",
    )(qkv, qkv, qkv, o_arr, gcol, grow, qkv_meta, qkv_meta, gcol_meta, bias_row, bias_col, gain)


MLSTM_SB = 8


def _mlstm_sample_kernel(qkv_ref, o_ref, g_ref, c_ref, n_ref, m_ref, brow_ref, gain_ref,
                         y_ref, co_ref, no_ref, mo_ref):
    k_scale = DK ** -0.5
    eye = _tri(DK, True) & _tri(DK, False)
    width = HEADS * DK

    def body(i, carry):
        g = g_ref[pl.ds(i, 1), :] + brow_ref[...]
        m_row = m_ref[pl.ds(i, 1), :]
        for h in range(HEADS):
            hs = slice(h * DK, (h + 1) * DK)
            ig = g[:, h:h + 1]
            fl = _log_sigmoid(g[:, HEADS + h:HEADS + h + 1])
            m_old = m_row[:, h:h + 1]
            m_new = jnp.maximum(fl + m_old, ig)
            w_in = jnp.exp(ig - m_new)
            w_ca = jnp.exp(fl + m_old - m_new)
            q = qkv_ref[pl.ds(i, 1), hs].astype(BF16)
            k = (qkv_ref[pl.ds(i, 1), width + h * DK:width + (h + 1) * DK] * k_scale).astype(BF16)
            v = qkv_ref[pl.ds(i, 1), 2 * width + h * DK:2 * width + (h + 1) * DK].astype(BF16)
            qf = q.astype(F32)
            kf = k.astype(F32)
            vf = v.astype(F32)
            c_old = c_ref[i, h]
            n_old = n_ref[i, h]
            s = jnp.sum(qf * kf, axis=-1, keepdims=True) * w_in
            qc = jnp.dot(jnp.broadcast_to(q, (8, DK)), c_old.astype(BF16), preferred_element_type=F32)[0:1, :]
            num = w_ca * qc + s * vf
            den = w_ca * jnp.sum(qf * n_old, axis=-1, keepdims=True) + s
            hid = num / jnp.maximum(jnp.abs(den), jnp.exp(-m_new))
            kw = kf * w_in
            kdiag = jnp.where(eye, jnp.broadcast_to(kw, (DK, DK)), 0.0).astype(BF16)
            kv = jnp.dot(kdiag, jnp.broadcast_to(v, (DK, DV)), preferred_element_type=F32)
            co_ref[i, h] = w_ca * c_old + kv
            no_ref[i, h] = w_ca * n_old + kw
            mo_ref[pl.ds(i, 1), h:h + 1] = m_new
            y_ref[pl.ds(i, 1), hs] = _head_norm_gate(hid, o_ref[pl.ds(i, 1), hs], gain_ref[:, hs])
        return carry

    lax.fori_loop(0, MLSTM_SB, body, 0)


def _mlstm_sample(qkv, o_arr, gcol, c0, n0, m0, bias_row, gain):
    nb = qkv.shape[0]
    width = HEADS * DK
    sb = MLSTM_SB
    const2 = lambda i: (0, 0)
    return pl.pallas_call(
        _mlstm_sample_kernel,
        grid=(nb // sb,),
        in_specs=[pl.BlockSpec((sb, 3 * width), lambda i: (i, 0)),
                  pl.BlockSpec((sb, width), lambda i: (i, 0)),
                  pl.BlockSpec((sb, 2 * HEADS), lambda i: (i, 0)),
                  pl.BlockSpec((sb, HEADS, DK, DV), lambda i: (i, 0, 0, 0)),
                  pl.BlockSpec((sb, HEADS, 1, DK), lambda i: (i, 0, 0, 0)),
                  pl.BlockSpec((sb, HEADS), lambda i: (i, 0)),
                  pl.BlockSpec((1, 2 * HEADS), const2),
                  pl.BlockSpec((1, width), const2)],
        out_specs=[pl.BlockSpec((sb, width), lambda i: (i, 0)),
                   pl.BlockSpec((sb, HEADS, DK, DV), lambda i: (i, 0, 0, 0)),
                   pl.BlockSpec((sb, HEADS, 1, DK), lambda i: (i, 0, 0, 0)),
                   pl.BlockSpec((sb, HEADS), lambda i: (i, 0))],
        out_shape=[jax.ShapeDtypeStruct((nb, width), F32),
                   jax.ShapeDtypeStruct((nb, HEADS, DK, DV), F32),
                   jax.ShapeDtypeStruct((nb, HEADS, 1, DK), F32),
                   jax.ShapeDtypeStruct((nb, HEADS), F32)],
        compiler_params=_cparams(1, 48),
        name="mlstm_sample",
    )(qkv, o_arr, gcol, c0, n0, m0, bias_row, gain)


def _merge_kernel(ya_ref, yb_ref, ga_ref, gb_ref, x_ref, wglu_ref, bglu_ref, wa_ref, wb_ref, wout_ref,
                  g1_ref, b1_ref, o_ref):
    g = _gelu_tanh(ya_ref[...])
    z = jnp.dot(g.astype(BF16), wglu_ref[...], preferred_element_type=F32) + bglu_ref[...]
    out_a = g * _sigmoid(z)
    up_a = jnp.dot(out_a.astype(BF16), wa_ref[...], preferred_element_type=F32)
    up_b = jnp.dot(yb_ref[...], wb_ref[...], preferred_element_type=F32)
    mix = _sigmoid(ga_ref[...]) * up_a + _sigmoid(gb_ref[...]) * up_b
    mo = jnp.dot(mix.astype(BF16), wout_ref[...], preferred_element_type=F32)
    o_ref[...] = _layernorm_rows(ALPHA * x_ref[...] + mo, g1_ref[...], b1_ref[...])


def _merge(ya, yb, gab, x, w_glu, b_glu, w_a_up, w_b_up, w_out, ln_g, ln_b, tm):
    r = x.shape[0]
    const2 = lambda i: (0, 0)
    resident = functools.partial(pl.BlockSpec, index_map=const2, pipeline_mode=pl.Buffered(1))
    return pl.pallas_call(
        _merge_kernel,
        grid=(r // tm,),
        in_specs=[pl.BlockSpec((tm, SSM_WIDTH), lambda i: (i, 0)),
                  pl.BlockSpec((tm, HEADS * DV), lambda i: (i, 0)),
                  pl.BlockSpec((tm, D_MODEL), lambda i: (i, 0)),
                  pl.BlockSpec((tm, D_MODEL), lambda i: (i, 1)),
                  pl.BlockSpec((tm, D_MODEL), lambda i: (i, 0)),
                  resident((SSM_WIDTH, SSM_WIDTH)),
                  resident((1, SSM_WIDTH)),
                  resident((SSM_WIDTH, D_MODEL)),
                  resident((HEADS * DV, D_MODEL)),
                  resident((D_MODEL, D_MODEL)),
                  resident((1, D_MODEL)),
                  resident((1, D_MODEL))],
        out_specs=pl.BlockSpec((tm, D_MODEL), lambda i: (i, 0)),
        out_shape=jax.ShapeDtypeStruct((r, D_MODEL), F32),
        compiler_params=_cparams(1, 52),
        name="merge_ln1",
    )(ya, yb, gab, gab, x, w_glu, b_glu, w_a_up, w_b_up, w_out, ln_g, ln_b)


def _ffn_kernel(x_ref, wg_ref, wu_ref, wd_ref, g2_ref, b2_ref, o_ref, xb_s):
    f_id = pl.program_id(1)
    n_f = pl.num_programs(1)

    @pl.when(f_id == 0)
    def _():
        xb_s[...] = x_ref[...].astype(BF16)
        o_ref[...] = jnp.zeros_like(o_ref)

    xb = xb_s[...]
    hg = jnp.dot(xb, wg_ref[...].astype(BF16), preferred_element_type=F32)
    hu = jnp.dot(xb, wu_ref[...].astype(BF16), preferred_element_type=F32)
    act = (hg * _sigmoid(hg)) * hu
    o_ref[...] += jnp.dot(act.astype(BF16), wd_ref[...].astype(BF16), preferred_element_type=F32)

    @pl.when(f_id == n_f - 1)
    def _():
        o_ref[...] = _layernorm_rows(ALPHA * x_ref[...] + o_ref[...], g2_ref[...], b2_ref[...])


def _ffn(x, w_gate3, w_up3, w_down3, ln_g, ln_b, tm, tf):
    r = x.shape[0]
    const2 = lambda i, f: (0, 0)
    return pl.pallas_call(
        _ffn_kernel,
        grid=(r // tm, D_FF // tf),
        in_specs=[pl.BlockSpec((tm, D_MODEL), lambda i, f: (i, 0)),
                  pl.BlockSpec((None, D_MODEL, tf), lambda i, f: (0, 0, f)),
                  pl.BlockSpec((None, D_MODEL, tf), lambda i, f: (0, 0, f)),
                  pl.BlockSpec((None, tf, D_MODEL), lambda i, f: (0, f, 0)),
                  pl.BlockSpec((1, D_MODEL), const2),
                  pl.BlockSpec((1, D_MODEL), const2)],
        out_specs=pl.BlockSpec((tm, D_MODEL), lambda i, f: (i, 0)),
        out_shape=jax.ShapeDtypeStruct((r, D_MODEL), F32),
        scratch_shapes=[pltpu.VMEM((tm, D_MODEL), BF16)],
        compiler_params=_cparams(2, 59),
        name="ffn_ln2",
    )(x, w_gate3, w_up3, w_down3, ln_g, ln_b)


def kernel(x_prompt, x_sample, state_ssm_re, state_ssm_im, state_mlstm_c, state_mlstm_n, state_mlstm_m,
           meta_tokens, w_in, b_if, ssm_a_re, ssm_a_im, ssm_log_dt, ssm_b_re, ssm_b_im, ssm_c_re, ssm_c_im,
           ssm_d, w_glu, b_glu, w_a_up, mh_gain, w_b_up, w_out, ln1_g, ln1_b, w_gate, w_up, w_down,
           ln2_g, ln2_b):
    batch, seq, _ = x_prompt.shape
    nb = x_sample.shape[0]
    width = HEADS * DK

    w_if = w_in[0, :, TAIL_COL0:TAIL_COL0 + TAIL_SHIFT].astype(BF16)
    w_glu_b = w_glu[0].astype(BF16)
    w_a_b = w_a_up[0].astype(BF16)
    w_b_b = w_b_up[0].astype(BF16)
    w_out_b = w_out[0].astype(BF16)
    bias_row = b_if[0].reshape(1, 2 * HEADS)
    bias_col = b_if[0].reshape(2 * HEADS, 1)
    gain = mh_gain[0].reshape(1, width)
    d_skip = ssm_d[0].reshape(1, SSM_WIDTH)
    b_glu_r = b_glu[0].reshape(1, SSM_WIDTH)
    ln1 = (ln1_g[0].reshape(1, D_MODEL), ln1_b[0].reshape(1, D_MODEL))
    ln2 = (ln2_g[0].reshape(1, D_MODEL), ln2_b[0].reshape(1, D_MODEL))
    abar_re, abar_im, bt, ct = _s5_discretise(
        ssm_a_re[0], ssm_a_im[0], ssm_log_dt[0], ssm_b_re[0], ssm_b_im[0], ssm_c_re[0], ssm_c_im[0])

    xp = x_prompt.reshape(batch * seq, D_MODEL)
    xs = x_sample.reshape(nb, D_MODEL)
    xsm = jnp.concatenate([xs, meta_tokens], axis=0)
    n_small = nb + N_META
    gates = (w_if, w_if.T)
    uo_blocks, qkv_blocks, n_tail = (0, 4, 2), (1, 1, 3), 2 * D_MODEL // PROJ_TN

    uo_p, gcol_p, grow_p = _proj(xp, w_in, uo_blocks, F32, 1024, "proj_uo_prompt", gates)
    qkv_p = _proj(xp, w_in, qkv_blocks, BF16, 1024, "proj_qkv_prompt")
    gab_p = _proj_tail(xp, w_in, n_tail, 1024, "proj_gab_prompt")

    uo_s, gcol_s, _ = _proj(xsm, w_in, uo_blocks, F32, n_small, "proj_uo_small", gates)
    qkv_s = _proj(xsm, w_in, qkv_blocks, F32, n_small, "proj_qkv_small")
    gab_s = _proj_tail(xsm, w_in, n_tail, n_small, "proj_gab_small")

    u_meta = uo_s[nb:, :SSM_WIDTH]
    qkv_meta = qkv_s[nb:].astype(BF16)
    gcol_meta = gcol_s[nb:]

    ya_p, pf_re, pf_im = _s5_prompt(uo_p, 0, u_meta, abar_re, abar_im, bt, ct, d_skip, batch, seq)
    yb_p, pc, pn, pm = _mlstm_prompt(qkv_p, uo_p, 1, gcol_p, grow_p, qkv_meta, gcol_meta, bias_row, bias_col,
                                     gain, batch, seq)
    x1_p = _merge(ya_p, yb_p, gab_p, xp, w_glu_b, b_glu_r, w_a_b, w_b_b, w_out_b, *ln1, tm=256)
    y_p = _ffn(x1_p, w_gate, w_up, w_down, *ln2, tm=1024, tf=256)

    ya_s, sf_re, sf_im = _s5_sample(uo_s[:nb, :SSM_WIDTH], state_ssm_re[0].reshape(nb, SSM_FLAT),
                                    state_ssm_im[0].reshape(nb, SSM_FLAT), abar_re, abar_im, bt, ct, d_skip)
    yb_s, sc, sn, sm = _mlstm_sample(qkv_s[:nb], uo_s[:nb, SSM_WIDTH:], gcol_s[:nb], state_mlstm_c[0],
                                     state_mlstm_n[0].reshape(nb, HEADS, 1, DK), state_mlstm_m[0], bias_row, gain)
    x1_s = _merge(ya_s, yb_s.astype(BF16), gab_s[:nb], xs, w_glu_b, b_glu_r, w_a_b, w_b_b, w_out_b, *ln1, tm=nb)
    y_s = _ffn(x1_s, w_gate, w_up, w_down, *ln2, tm=nb, tf=512)

    return (y_p.reshape(batch, seq, D_MODEL),
            y_s.reshape(nb, 1, D_MODEL),
            pf_re.reshape(1, batch, SSM_GROUPS, SSM_STATE),
            pf_im.reshape(1, batch, SSM_GROUPS, SSM_STATE),
            pc.reshape(1, batch, HEADS, DK, DV),
            pn.reshape(1, batch, HEADS, DK),
            pm[:, :, 0].reshape(1, batch, HEADS),
            sf_re.reshape(1, nb, SSM_GROUPS, SSM_STATE),
            sf_im.reshape(1, nb, SSM_GROUPS, SSM_STATE),
            sc.reshape(1, nb, HEADS, DK, DV),
            sn.reshape(1, nb, HEADS, DK),
            sm.reshape(1, nb, HEADS))
```

```python
import functools

import jax
import jax.numpy as jnp
import numpy as np
from jax import lax
from jax.experimental import pallas as pl
from jax.experimental.pallas import tpu as pltpu

F32 = jnp.float32
BF16 = jnp.bfloat16

D_MODEL = 2048
N_META = 16
SSM_WIDTH = 1024
SSM_GROUP = 16
SSM_GROUPS = 64
SSM_STATE = 64
SSM_FLAT = SSM_GROUPS * SSM_STATE
SSM_JBLK = 4
HEADS = 4
DK = 256
DV = 256
CHUNK = 128
D_FF = 5632
EPS = 1e-5
ALPHA = 2.0 ** 0.25

S5_TC = 256
S5_NSEG = 8
S5_SEG = S5_TC // S5_NSEG
S5_LW = 512

V7X_SCOPED_VMEM_MAX_BYTES = 60000 * 1024


def _cparams(n_axes, vmem_mb):
    return pltpu.CompilerParams(
        dimension_semantics=("arbitrary",) * n_axes,
        vmem_limit_bytes=min(vmem_mb * 1024 * 1024, V7X_SCOPED_VMEM_MAX_BYTES),
    )


def _sigmoid(x):
    return 1.0 / (1.0 + jnp.exp(-x))


def _log_sigmoid(x):
    return jnp.minimum(x, 0.0) - jnp.log(1.0 + jnp.exp(-jnp.abs(x)))


def _gelu_tanh(x):
    c = 0.7978845608028654
    return 0.5 * x * (1.0 + jnp.tanh(c * (x + 0.044715 * (x * x * x))))


def _layernorm_rows(x, g, b):
    mu = jnp.mean(x, axis=-1, keepdims=True)
    xc = x - mu
    var = jnp.mean(xc * xc, axis=-1, keepdims=True)
    return xc * lax.rsqrt(var + EPS) * g + b


PROJ_TN = 1024
GATE_ROWS = 2 * HEADS
TAIL_ROW0 = SSM_WIDTH + 4 * HEADS * DK
NT_DIMS = (((1,), (1,)), ((), ()))


def _proj_kernel(x_ref, wt_ref, o_ref, wb_s):
    @pl.when(pl.program_id(1) == 0)
    def _():
        wb_s[...] = wt_ref[...].astype(BF16)

    o_ref[...] = lax.dot_general(x_ref[...].astype(BF16), wb_s[...], NT_DIMS,
                                 preferred_element_type=F32).astype(o_ref.dtype)


def _proj_gates_kernel(x_ref, wt_ref, wg_ref, o_ref, g_ref, gt_ref, wb_s):
    @pl.when(pl.program_id(1) == 0)
    def _():
        wb_s[...] = wt_ref[...].astype(BF16)

    xb = x_ref[...].astype(BF16)
    o_ref[...] = lax.dot_general(xb, wb_s[...], NT_DIMS, preferred_element_type=F32).astype(o_ref.dtype)
    wg = wg_ref[...].astype(BF16)
    g_ref[...] = lax.dot_general(xb, wg, NT_DIMS, preferred_element_type=F32)
    gt_ref[...] = lax.dot_general(wg, xb, NT_DIMS, preferred_element_type=F32)


def _proj(x, wt3, blocks, out_dtype, tm, name, with_gates=False):
    r, k = x.shape
    first, step, count = blocks
    tn = PROJ_TN
    nm = r // tm
    in_specs = [pl.BlockSpec((tm, k), lambda j, i: (i, 0)),
                pl.BlockSpec((None, tn, k), lambda j, i: (0, first + step * j, 0))]
    out_specs = pl.BlockSpec((tm, tn), lambda j, i: (i, j))
    out_shape = jax.ShapeDtypeStruct((r, count * tn), out_dtype)
    args = (x, wt3)
    body = _proj_kernel
    if with_gates:
        ng = GATE_ROWS
        in_specs += [pl.BlockSpec((None, ng, k), lambda j, i: (0, TAIL_ROW0 // ng, 0))]
        out_specs = [out_specs, pl.BlockSpec((tm, ng), lambda j, i: (j * nm + i, 0)),
                     pl.BlockSpec((ng, tm), lambda j, i: (0, j * nm + i))]
        out_shape = [out_shape, jax.ShapeDtypeStruct((count * r, ng), F32), jax.ShapeDtypeStruct((ng, count * r), F32)]
        args = (x, wt3, wt3)
        body = _proj_gates_kernel
    return pl.pallas_call(
        body,
        grid=(count, nm),
        in_specs=in_specs,
        out_specs=out_specs,
        out_shape=out_shape,
        scratch_shapes=[pltpu.VMEM((tn, k), BF16)],
        compiler_params=_cparams(2, 52),
        name=name,
    )(*args)


def _s5_discretise(a_re, a_im, log_dt, b_re, b_im, c_re, c_im):
    dt = jnp.exp(log_dt)
    e = jnp.exp(a_re * dt)
    abar_re = e * jnp.cos(a_im * dt)
    abar_im = e * jnp.sin(a_im * dt)
    nr = abar_re - 1.0
    ni = abar_im
    den = a_re * a_re + a_im * a_im
    coef_re = (nr * a_re + ni * a_im) / den
    coef_im = (ni * a_re - nr * a_im) / den
    bb_re = coef_re[..., None] * b_re - coef_im[..., None] * b_im
    bb_im = coef_re[..., None] * b_im + coef_im[..., None] * b_re
    gpb = SSM_GROUPS // SSM_JBLK

    def b_rows(bb):
        return bb.reshape(SSM_JBLK, gpb, SSM_STATE, SSM_GROUP).transpose(0, 1, 3, 2).reshape(SSM_JBLK, 256, SSM_STATE)

    def c_rows(cc):
        return cc.reshape(SSM_JBLK, gpb, SSM_GROUP, SSM_STATE).transpose(0, 1, 3, 2).reshape(SSM_JBLK, 1024, SSM_GROUP)

    bt = jnp.concatenate([jnp.tile(b_rows(bb_re), (1, 1, 2)), jnp.tile(b_rows(bb_im), (1, 1, 2))], axis=-1)
    ct = jnp.concatenate([jnp.tile(c_rows(c_re), (1, 1, 8)), jnp.tile(c_rows(-c_im), (1, 1, 8))], axis=-1)
    return abar_re.reshape(1, SSM_FLAT), abar_im.reshape(1, SSM_FLAT), bt, ct


def _s5_expand_weights(bt_ref, ct_ref, bw_s, cwr_s, cwi_s):
    rb = lax.broadcasted_iota(jnp.int32, (256, 2048), 0)
    cb = lax.broadcasted_iota(jnp.int32, (256, 2048), 1)
    bmask = (rb >> 4) == ((cb & 1023) >> 6)
    rc = lax.broadcasted_iota(jnp.int32, (1024, 256), 0)
    cc = lax.broadcasted_iota(jnp.int32, (1024, 256), 1)
    cmask = (rc >> 6) == (cc >> 4)
    for j in range(SSM_JBLK):
        bt = bt_ref[j]
        wide = jnp.concatenate([bt[:, :128]] * 8 + [bt[:, 128:]] * 8, axis=1)
        bw_s[j] = jnp.where(bmask, wide, 0.0).astype(BF16)
        ct = ct_ref[j]
        cwr_s[j] = jnp.where(cmask, jnp.concatenate([ct[:, :128]] * 2, axis=1), 0.0).astype(BF16)
        cwi_s[j] = jnp.where(cmask, jnp.concatenate([ct[:, 128:]] * 2, axis=1), 0.0).astype(BF16)


def _cmul_add(ar, ai, hr, hi, xr, xi):
    return ar * hr - ai * hi + xr, ar * hi + ai * hr + xi


def _s5_project_in(u, b_ref, hre_ref, him_ref, rows):
    for j in range(SSM_JBLK):
        bu = jnp.dot(u[:, j * 256:(j + 1) * 256], b_ref[j], preferred_element_type=F32)
        hre_ref[0:rows, j * 1024:(j + 1) * 1024] = bu[:, :1024]
        him_ref[0:rows, j * 1024:(j + 1) * 1024] = bu[:, 1024:]


def _s5_project_out(hre_ref, him_ref, cre_ref, cimn_ref, rows):
    ys = []
    for j in range(SSM_JBLK):
        hr = hre_ref[0:rows, j * 1024:(j + 1) * 1024].astype(BF16)
        hi = him_ref[0:rows, j * 1024:(j + 1) * 1024].astype(BF16)
        y = jnp.dot(hr, cre_ref[j], preferred_element_type=F32)
        ys.append(y + jnp.dot(hi, cimn_ref[j], preferred_element_type=F32))
    return jnp.concatenate(ys, axis=1)


def _s5_prompt_kernel(u_ref, um_ref, perm_ref, permt_ref, bt_ref, ct_ref, are_ref, aim_ref, d_ref,
                      y_ref, fre_ref, fim_ref,
                      hre_ref, him_ref, pre_ref, pim_ref, cre_s, cim_s, hsre_ref, hsim_ref,
                      b_ref, cre_ref, cimn_ref):
    b_id = pl.program_id(0)
    c_id = pl.program_id(1)
    n_c = pl.num_programs(1)

    @pl.when(jnp.logical_and(b_id == 0, c_id == 0))
    def _():
        _s5_expand_weights(bt_ref, ct_ref, b_ref, cre_ref, cimn_ref)
        ar = are_ref[...]
        ai = aim_ref[...]
        pre_ref[0:1, :] = ar
        pim_ref[0:1, :] = ai

        def body(i, carry):
            pr, pi = carry
            nr = ar * pr - ai * pi
            ni = ar * pi + ai * pr
            pre_ref[pl.ds(i, 1), :] = nr
            pim_ref[pl.ds(i, 1), :] = ni
            return nr, ni

        lax.fori_loop(1, S5_SEG, body, (ar, ai))

    @pl.when(c_id == 0)
    def _():
        _s5_project_in(um_ref[...].astype(BF16), b_ref, hre_ref, him_ref, N_META)
        ar = are_ref[...]
        ai = aim_ref[...]
        hr = jnp.zeros((1, SSM_FLAT), F32)
        hi = jnp.zeros((1, SSM_FLAT), F32)
        for t in range(N_META):
            hr, hi = _cmul_add(ar, ai, hr, hi, hre_ref[t:t + 1, :], him_ref[t:t + 1, :])
        cre_s[...] = hr
        cim_s[...] = hi

    u = u_ref[...]
    u_perm = jnp.dot(perm_ref[...], u.astype(BF16), preferred_element_type=F32).astype(BF16)
    _s5_project_in(u_perm, b_ref, hre_ref, him_ref, S5_TC)

    full = (S5_NSEG, S5_LW)
    for lg in range(SSM_FLAT // S5_LW):
        lanes = slice(lg * S5_LW, (lg + 1) * S5_LW)
        ar = jnp.broadcast_to(are_ref[:, lanes], full)
        ai = jnp.broadcast_to(aim_ref[:, lanes], full)

        def local_step(k, carry):
            rows = pl.ds(pl.multiple_of(k * S5_NSEG, S5_NSEG), S5_NSEG)
            nr, ni = _cmul_add(ar, ai, carry[0], carry[1], hre_ref[rows, lanes], him_ref[rows, lanes])
            hre_ref[rows, lanes] = nr
            him_ref[rows, lanes] = ni
            return nr, ni

        er, ei = lax.fori_loop(0, S5_SEG, local_step, (jnp.zeros(full, F32), jnp.zeros(full, F32)), unroll=4)

        asr = pre_ref[S5_SEG - 1:S5_SEG, lanes]
        asi = pim_ref[S5_SEG - 1:S5_SEG, lanes]
        hr = cre_s[:, lanes]
        hi = cim_s[:, lanes]
        for j in range(S5_NSEG):
            hsre_ref[j:j + 1, lanes] = hr
            hsim_ref[j:j + 1, lanes] = hi
            hr, hi = _cmul_add(asr, asi, hr, hi, er[j:j + 1, :], ei[j:j + 1, :])
        cre_s[:, lanes] = hr
        cim_s[:, lanes] = hi

        sr = hsre_ref[:, lanes]
        si = hsim_ref[:, lanes]

        def fix_step(k, carry):
            rows = pl.ds(pl.multiple_of(k * S5_NSEG, S5_NSEG), S5_NSEG)
            pr = jnp.broadcast_to(pre_ref[pl.ds(k, 1), lanes], full)
            pi = jnp.broadcast_to(pim_ref[pl.ds(k, 1), lanes], full)
            hre_ref[rows, lanes] = hre_ref[rows, lanes] + (pr * sr - pi * si)
            him_ref[rows, lanes] = him_ref[rows, lanes] + (pr * si + pi * sr)
            return carry

        lax.fori_loop(0, S5_SEG, fix_step, 0, unroll=4)

    y_perm = _s5_project_out(hre_ref, him_ref, cre_ref, cimn_ref, S5_TC)
    permt = permt_ref[...]
    y = sum(jnp.dot(permt, piece, preferred_element_type=F32) for piece in _split3(y_perm))
    y_ref[...] = y + d_ref[...] * u

    @pl.when(c_id == n_c - 1)
    def _():
        fre_ref[0] = cre_s[...]
        fim_ref[0] = cim_s[...]


def _s5_weight_scratch():
    return [pltpu.VMEM((SSM_JBLK, 256, 2048), BF16), pltpu.VMEM((SSM_JBLK, 1024, 256), BF16),
            pltpu.VMEM((SSM_JBLK, 1024, 256), BF16)]


def _s5_prompt(u_arr, u_col_block, u_meta, abar_re, abar_im, bt, ct, d_skip, batch, seq):
    n_c = seq // S5_TC
    const3 = lambda b, c: (0, 0, 0)
    const2 = lambda b, c: (0, 0)
    rows = np.arange(S5_TC)
    perm_np = np.zeros((S5_TC, S5_TC), np.float32)
    perm_np[rows, (rows % S5_NSEG) * S5_SEG + rows // S5_NSEG] = 1.0
    perm = jnp.asarray(perm_np, dtype=BF16)
    permt = jnp.asarray(perm_np.T, dtype=BF16)
    return pl.pallas_call(
        _s5_prompt_kernel,
        grid=(batch, n_c),
        in_specs=[pl.BlockSpec((S5_TC, SSM_WIDTH), lambda b, c: (b * n_c + c, u_col_block)),
                  pl.BlockSpec((N_META, SSM_WIDTH), const2),
                  pl.BlockSpec((S5_TC, S5_TC), const2),
                  pl.BlockSpec((S5_TC, S5_TC), const2),
                  pl.BlockSpec((SSM_JBLK, 256, 256), const3),
                  pl.BlockSpec((SSM_JBLK, 1024, 256), const3),
                  pl.BlockSpec((1, SSM_FLAT), const2),
                  pl.BlockSpec((1, SSM_FLAT), const2),
                  pl.BlockSpec((1, SSM_WIDTH), const2)],
        out_specs=[pl.BlockSpec((S5_TC, SSM_WIDTH), lambda b, c: (b * n_c + c, 0)),
                   pl.BlockSpec((1, 1, SSM_FLAT), lambda b, c: (b, 0, 0)),
                   pl.BlockSpec((1, 1, SSM_FLAT), lambda b, c: (b, 0, 0))],
        out_shape=[jax.ShapeDtypeStruct((batch * seq, SSM_WIDTH), F32),
                   jax.ShapeDtypeStruct((batch, 1, SSM_FLAT), F32),
                   jax.ShapeDtypeStruct((batch, 1, SSM_FLAT), F32)],
        scratch_shapes=[pltpu.VMEM((S5_TC, SSM_FLAT), F32), pltpu.VMEM((S5_TC, SSM_FLAT), F32),
                        pltpu.VMEM((S5_SEG, SSM_FLAT), F32), pltpu.VMEM((S5_SEG, SSM_FLAT), F32),
                        pltpu.VMEM((1, SSM_FLAT), F32), pltpu.VMEM((1, SSM_FLAT), F32),
                        pltpu.VMEM((S5_NSEG, SSM_FLAT), F32), pltpu.VMEM((S5_NSEG, SSM_FLAT), F32)]
                       + _s5_weight_scratch(),
        compiler_params=_cparams(2, 48),
        name="s5_prompt",
    )(u_arr, u_meta, perm, permt, bt, ct, abar_re, abar_im, d_skip)


def _s5_sample_kernel(u_ref, h0re_ref, h0im_ref, bt_ref, ct_ref, are_ref, aim_ref, d_ref,
                      y_ref, hre_ref, him_ref, b_ref, cre_ref, cimn_ref):
    rows = u_ref.shape[0]
    _s5_expand_weights(bt_ref, ct_ref, b_ref, cre_ref, cimn_ref)
    u = u_ref[...]
    _s5_project_in(u.astype(BF16), b_ref, hre_ref, him_ref, rows)
    nr, ni = _cmul_add(are_ref[...], aim_ref[...], h0re_ref[...], h0im_ref[...], hre_ref[...], him_ref[...])
    hre_ref[...] = nr
    him_ref[...] = ni
    y_ref[...] = _s5_project_out(hre_ref, him_ref, cre_ref, cimn_ref, rows) + d_ref[...] * u


def _s5_sample(u, h0_re, h0_im, abar_re, abar_im, bt, ct, d_skip):
    rows = u.shape[0]
    return pl.pallas_call(
        _s5_sample_kernel,
        out_shape=[jax.ShapeDtypeStruct((rows, SSM_WIDTH), F32),
                   jax.ShapeDtypeStruct((rows, SSM_FLAT), F32),
                   jax.ShapeDtypeStruct((rows, SSM_FLAT), F32)],
        scratch_shapes=_s5_weight_scratch(),
        compiler_params=_cparams(0, 48),
        name="s5_sample",
    )(u, h0_re, h0_im, bt, ct, abar_re, abar_im, d_skip)


def _split3(x):
    hi = x.astype(BF16)
    r1 = x - hi.astype(F32)
    mid = r1.astype(BF16)
    lo = (r1 - mid.astype(F32)).astype(BF16)
    return hi, mid, lo


def _tri(n, lower):
    r = lax.broadcasted_iota(jnp.int32, (n, n), 0)
    c = lax.broadcasted_iota(jnp.int32, (n, n), 1)
    return (r >= c) if lower else (r <= c)


def _gate_columns(gcol, bias_row):
    n = gcol.shape[0]
    g = gcol + bias_row
    fl = _log_sigmoid(g[:, HEADS:])
    tril = jnp.where(_tri(n, True), 1.0, 0.0).astype(BF16)
    b = sum(jnp.dot(tril, p, preferred_element_type=F32) for p in _split3(fl))
    return g[:, :HEADS], b


def _gate_rows(grow, bias_col):
    n = grow.shape[1]
    g = grow + bias_col
    fl = _log_sigmoid(g[HEADS:, :])
    triu = jnp.where(_tri(n, False), 1.0, 0.0).astype(BF16)
    b = sum(jnp.dot(p, triu, preferred_element_type=F32) for p in _split3(fl))
    return g[:HEADS, :], b


def _mlstm_state_update(k, v, ig_col, b_col, c_old, n_old, m_old):
    n_rows = k.shape[0]
    b_last = b_col[n_rows - 1:n_rows, :]
    e_col = b_last - b_col + ig_col
    m_new = jnp.maximum(b_last + m_old, jnp.max(e_col, axis=0, keepdims=True))
    w_end = jnp.exp(e_col - m_new)
    carry = jnp.exp(b_last + m_old - m_new)
    kw = k * w_end
    kv = lax.dot_general(kw.astype(BF16), v, (((0,), (0,)), ((), ())), preferred_element_type=F32)
    c_new = carry * c_old + kv
    n_new = carry * n_old + jnp.sum(kw, axis=0, keepdims=True)
    return c_new, n_new, m_new


def _mlstm_chunk_out(q, k_bf, v, ig_row, b_row, b_col, c_old, n_old, m_old):
    n_rows = q.shape[0]
    dmat = jnp.where(_tri(n_rows, True), b_col + (ig_row - b_row), -jnp.inf)
    inter = b_col + m_old
    m_t = jnp.maximum(inter, jnp.max(dmat, axis=-1, keepdims=True))
    w_intra = jnp.exp(dmat - m_t)
    w_inter = jnp.exp(inter - m_t)
    s = lax.dot_general(q, k_bf, (((1,), (1,)), ((), ())), preferred_element_type=F32) * w_intra
    num = w_inter * jnp.dot(q, c_old.astype(BF16), preferred_element_type=F32)
    num = num + jnp.dot(s.astype(BF16), v, preferred_element_type=F32)
    den = w_inter * jnp.sum(q.astype(F32) * n_old, axis=-1, keepdims=True) + jnp.sum(s, axis=-1, keepdims=True)
    return num / jnp.maximum(jnp.abs(den), jnp.exp(-m_t))


def _head_norm_gate(h, o, gain):
    mu = jnp.mean(h, axis=-1, keepdims=True)
    hc = h - mu
    var = jnp.mean(hc * hc, axis=-1, keepdims=True)
    return _sigmoid(o) * (hc * lax.rsqrt(var + EPS) * gain)


def _mlstm_prompt_kernel(q_ref, k_ref, v_ref, o_ref, gc_ref, gr_ref, km_ref, vm_ref, gcm_ref,
                         brow_ref, bcol_ref, gain_ref,
                         y_ref, cf_ref, nf_ref, mf_ref,
                         c_s, n_s, m_s):
    c_id = pl.program_id(1)
    n_c = pl.num_programs(1)
    k_scale = DK ** -0.5

    @pl.when(c_id == 0)
    def _():
        ig_c, b_c = _gate_columns(gcm_ref[...], brow_ref[...])
        for h in range(HEADS):
            hs = slice(h * DK, (h + 1) * DK)
            k = km_ref[:, hs].astype(F32) * k_scale
            c_new, n_new, m_new = _mlstm_state_update(
                k, vm_ref[:, hs], ig_c[:, h:h + 1], b_c[:, h:h + 1],
                jnp.zeros((DK, DV), F32), jnp.zeros((1, DK), F32), jnp.zeros((1, 1), F32))
            c_s[h] = c_new
            n_s[h] = n_new
            m_s[h] = jnp.broadcast_to(m_new, (8, 128))

    ig_c, b_c = _gate_columns(gc_ref[...], brow_ref[...])
    ig_r, b_r = _gate_rows(gr_ref[...], bcol_ref[...])
    for h in range(HEADS):
        hs = slice(h * DK, (h + 1) * DK)
        q = q_ref[:, hs]
        kf = k_ref[:, hs].astype(F32) * k_scale
        v = v_ref[:, hs]
        c_old = c_s[h]
        n_old = n_s[h]
        m_old = m_s[h][0:1, 0:1]
        hid = _mlstm_chunk_out(q, kf.astype(BF16), v, ig_r[h:h + 1, :], b_r[h:h + 1, :], b_c[:, h:h + 1],
                               c_old, n_old, m_old)
        y_ref[:, hs] = _head_norm_gate(hid, o_ref[:, hs], gain_ref[:, hs]).astype(y_ref.dtype)
        c_new, n_new, m_new = _mlstm_state_update(kf, v, ig_c[:, h:h + 1], b_c[:, h:h + 1], c_old, n_old, m_old)
        c_s[h] = c_new
        n_s[h] = n_new
        m_s[h] = jnp.broadcast_to(m_new, (8, 128))

    @pl.when(c_id == n_c - 1)
    def _():
        cf_ref[0] = c_s[...]
        nf_ref[0] = n_s[...]
        for h in range(HEADS):
            mf_ref[0, h:h + 1, :] = m_s[h][0:1, :]


def _mlstm_prompt(qkv, o_arr, o_col_block, gcol, grow, qkv_meta, gcol_meta, bias_row, bias_col, gain, batch, seq):
    n_c = seq // CHUNK
    width = HEADS * DK
    row = lambda b, c: b * n_c + c
    const2 = lambda b, c: (0, 0)
    return pl.pallas_call(
        _mlstm_prompt_kernel,
        grid=(batch, n_c),
        in_specs=[pl.BlockSpec((CHUNK, width), lambda b, c: (row(b, c), 0)),
                  pl.BlockSpec((CHUNK, width), lambda b, c: (row(b, c), 1)),
                  pl.BlockSpec((CHUNK, width), lambda b, c: (row(b, c), 2)),
                  pl.BlockSpec((CHUNK, width), lambda b, c: (row(b, c), o_col_block)),
                  pl.BlockSpec((CHUNK, 2 * HEADS), lambda b, c: (row(b, c), 0)),
                  pl.BlockSpec((2 * HEADS, CHUNK), lambda b, c: (0, row(b, c))),
                  pl.BlockSpec((N_META, width), lambda b, c: (0, 1)),
                  pl.BlockSpec((N_META, width), lambda b, c: (0, 2)),
                  pl.BlockSpec((N_META, 2 * HEADS), const2),
                  pl.BlockSpec((1, 2 * HEADS), const2),
                  pl.BlockSpec((2 * HEADS, 1), const2),
                  pl.BlockSpec((1, width), const2)],
        out_specs=[pl.BlockSpec((CHUNK, width), lambda b, c: (row(b, c), 0)),
                   pl.BlockSpec((1, HEADS, DK, DV), lambda b, c: (b, 0, 0, 0)),
                   pl.BlockSpec((1, HEADS, 1, DK), lambda b, c: (b, 0, 0, 0)),
                   pl.BlockSpec((1, HEADS, 128), lambda b, c: (b, 0, 0))],
        out_shape=[jax.ShapeDtypeStruct((batch * seq, width), BF16),
                   jax.ShapeDtypeStruct((batch, HEADS, DK, DV), F32),
                   jax.ShapeDtypeStruct((batch, HEADS, 1, DK), F32),
                   jax.ShapeDtypeStruct((batch, HEADS, 128), F32)],
        scratch_shapes=[pltpu.VMEM((HEADS, DK, DV), F32), pltpu.VMEM((HEADS, 1, DK), F32),
                        pltpu.VMEM((HEADS, 8, 128), F32)],
        compiler_params=_cparams(2, 32),
        name="mlstm_prompt",
    )(qkv, qkv, qkv, o_arr, gcol, grow, qkv_meta, qkv_meta, gcol_meta, bias_row, bias_col, gain)


MLSTM_SB = 8


def _mlstm_sample_kernel(qkv_ref, o_ref, g_ref, c_ref, n_ref, m_ref, brow_ref, gain_ref,
                         y_ref, co_ref, no_ref, mo_ref):
    k_scale = DK ** -0.5
    eye = _tri(DK, True) & _tri(DK, False)
    width = HEADS * DK

    def body(i, carry):
        g = g_ref[pl.ds(i, 1), :] + brow_ref[...]
        m_row = m_ref[pl.ds(i, 1), :]
        for h in range(HEADS):
            hs = slice(h * DK, (h + 1) * DK)
            ig = g[:, h:h + 1]
            fl = _log_sigmoid(g[:, HEADS + h:HEADS + h + 1])
            m_old = m_row[:, h:h + 1]
            m_new = jnp.maximum(fl + m_old, ig)
            w_in = jnp.exp(ig - m_new)
            w_ca = jnp.exp(fl + m_old - m_new)
            q = qkv_ref[pl.ds(i, 1), hs].astype(BF16)
            k = (qkv_ref[pl.ds(i, 1), width + h * DK:width + (h + 1) * DK] * k_scale).astype(BF16)
            v = qkv_ref[pl.ds(i, 1), 2 * width + h * DK:2 * width + (h + 1) * DK].astype(BF16)
            qf = q.astype(F32)
            kf = k.astype(F32)
            vf = v.astype(F32)
            c_old = c_ref[i, h]
            n_old = n_ref[i, h]
            s = jnp.sum(qf * kf, axis=-1, keepdims=True) * w_in
            qc = jnp.dot(jnp.broadcast_to(q, (8, DK)), c_old.astype(BF16), preferred_element_type=F32)[0:1, :]
            num = w_ca * qc + s * vf
            den = w_ca * jnp.sum(qf * n_old, axis=-1, keepdims=True) + s
            hid = num / jnp.maximum(jnp.abs(den), jnp.exp(-m_new))
            kw = kf * w_in
            kdiag = jnp.where(eye, jnp.broadcast_to(kw, (DK, DK)), 0.0).astype(BF16)
            kv = jnp.dot(kdiag, jnp.broadcast_to(v, (DK, DV)), preferred_element_type=F32)
            co_ref[i, h] = w_ca * c_old + kv
            no_ref[i, h] = w_ca * n_old + kw
            mo_ref[pl.ds(i, 1), h:h + 1] = m_new
            y_ref[pl.ds(i, 1), hs] = _head_norm_gate(hid, o_ref[pl.ds(i, 1), hs], gain_ref[:, hs])
        return carry

    lax.fori_loop(0, MLSTM_SB, body, 0)


def _mlstm_sample(qkv, o_arr, gcol, c0, n0, m0, bias_row, gain):
    nb = qkv.shape[0]
    width = HEADS * DK
    sb = MLSTM_SB
    const2 = lambda i: (0, 0)
    return pl.pallas_call(
        _mlstm_sample_kernel,
        grid=(nb // sb,),
        in_specs=[pl.BlockSpec((sb, 3 * width), lambda i: (i, 0)),
                  pl.BlockSpec((sb, width), lambda i: (i, 0)),
                  pl.BlockSpec((sb, 2 * HEADS), lambda i: (i, 0)),
                  pl.BlockSpec((sb, HEADS, DK, DV), lambda i: (i, 0, 0, 0)),
                  pl.BlockSpec((sb, HEADS, 1, DK), lambda i: (i, 0, 0, 0)),
                  pl.BlockSpec((sb, HEADS), lambda i: (i, 0)),
                  pl.BlockSpec((1, 2 * HEADS), const2),
                  pl.BlockSpec((1, width), const2)],
        out_specs=[pl.BlockSpec((sb, width), lambda i: (i, 0)),
                   pl.BlockSpec((sb, HEADS, DK, DV), lambda i: (i, 0, 0, 0)),
                   pl.BlockSpec((sb, HEADS, 1, DK), lambda i: (i, 0, 0, 0)),
                   pl.BlockSpec((sb, HEADS), lambda i: (i, 0))],
        out_shape=[jax.ShapeDtypeStruct((nb, width), F32),
                   jax.ShapeDtypeStruct((nb, HEADS, DK, DV), F32),
                   jax.ShapeDtypeStruct((nb, HEADS, 1, DK), F32),
                   jax.ShapeDtypeStruct((nb, HEADS), F32)],
        compiler_params=_cparams(1, 48),
        name="mlstm_sample",
    )(qkv, o_arr, gcol, c0, n0, m0, bias_row, gain)


def _merge_kernel(ya_ref, yb_ref, x_ref, wgab_ref, wglu_ref, bglu_ref, wa_ref, wb_ref, wout_ref,
                  g1_ref, b1_ref, o_ref):
    x = x_ref[...]
    gab = lax.dot_general(x.astype(BF16), wgab_ref[...], NT_DIMS, preferred_element_type=F32)
    g = _gelu_tanh(ya_ref[...])
    z = jnp.dot(g.astype(BF16), wglu_ref[...], preferred_element_type=F32) + bglu_ref[...]
    out_a = g * _sigmoid(z)
    up_a = jnp.dot(out_a.astype(BF16), wa_ref[...], preferred_element_type=F32)
    up_b = jnp.dot(yb_ref[...], wb_ref[...], preferred_element_type=F32)
    mix = _sigmoid(gab[:, :D_MODEL]) * up_a + _sigmoid(gab[:, D_MODEL:]) * up_b
    mo = jnp.dot(mix.astype(BF16), wout_ref[...], preferred_element_type=F32)
    o_ref[...] = _layernorm_rows(ALPHA * x + mo, g1_ref[...], b1_ref[...])


def _merge(ya, yb, x, wt_gab, w_glu, b_glu, w_a_up, w_b_up, w_out, ln_g, ln_b, tm):
    r = x.shape[0]
    const2 = lambda i: (0, 0)
    resident = functools.partial(pl.BlockSpec, index_map=const2, pipeline_mode=pl.Buffered(1))
    return pl.pallas_call(
        _merge_kernel,
        grid=(r // tm,),
        in_specs=[pl.BlockSpec((tm, SSM_WIDTH), lambda i: (i, 0)),
                  pl.BlockSpec((tm, HEADS * DV), lambda i: (i, 0)),
                  pl.BlockSpec((tm, D_MODEL), lambda i: (i, 0)),
                  resident((2 * D_MODEL, D_MODEL)),
                  resident((SSM_WIDTH, SSM_WIDTH)),
                  resident((1, SSM_WIDTH)),
                  resident((SSM_WIDTH, D_MODEL)),
                  resident((HEADS * DV, D_MODEL)),
                  resident((D_MODEL, D_MODEL)),
                  resident((1, D_MODEL)),
                  resident((1, D_MODEL))],
        out_specs=pl.BlockSpec((tm, D_MODEL), lambda i: (i, 0)),
        out_shape=jax.ShapeDtypeStruct((r, D_MODEL), F32),
        compiler_params=_cparams(1, 58),
        name="merge_ln1",
    )(ya, yb, x, wt_gab, w_glu, b_glu, w_a_up, w_b_up, w_out, ln_g, ln_b)


def _ffn_kernel(x_ref, wg_ref, wu_ref, wd_ref, g2_ref, b2_ref, o_ref, xb_s):
    f_id = pl.program_id(1)
    n_f = pl.num_programs(1)

    @pl.when(f_id == 0)
    def _():
        xb_s[...] = x_ref[...].astype(BF16)
        o_ref[...] = jnp.zeros_like(o_ref)

    xb = xb_s[...]
    hg = jnp.dot(xb, wg_ref[...].astype(BF16), preferred_element_type=F32)
    hu = jnp.dot(xb, wu_ref[...].astype(BF16), preferred_element_type=F32)
    act = (hg * _sigmoid(hg)) * hu
    o_ref[...] += jnp.dot(act.astype(BF16), wd_ref[...].astype(BF16), preferred_element_type=F32)

    @pl.when(f_id == n_f - 1)
    def _():
        o_ref[...] = _layernorm_rows(ALPHA * x_ref[...] + o_ref[...], g2_ref[...], b2_ref[...])


def _ffn(x, w_gate3, w_up3, w_down3, ln_g, ln_b, tm, tf):
    r = x.shape[0]
    const2 = lambda i, f: (0, 0)
    return pl.pallas_call(
        _ffn_kernel,
        grid=(r // tm, D_FF // tf),
        in_specs=[pl.BlockSpec((tm, D_MODEL), lambda i, f: (i, 0)),
                  pl.BlockSpec((None, D_MODEL, tf), lambda i, f: (0, 0, f)),
                  pl.BlockSpec((None, D_MODEL, tf), lambda i, f: (0, 0, f)),
                  pl.BlockSpec((None, tf, D_MODEL), lambda i, f: (0, f, 0)),
                  pl.BlockSpec((1, D_MODEL), const2),
                  pl.BlockSpec((1, D_MODEL), const2)],
        out_specs=pl.BlockSpec((tm, D_MODEL), lambda i, f: (i, 0)),
        out_shape=jax.ShapeDtypeStruct((r, D_MODEL), F32),
        scratch_shapes=[pltpu.VMEM((tm, D_MODEL), BF16)],
        compiler_params=_cparams(2, 59),
        name="ffn_ln2",
    )(x, w_gate3, w_up3, w_down3, ln_g, ln_b)


def kernel(x_prompt, x_sample, state_ssm_re, state_ssm_im, state_mlstm_c, state_mlstm_n, state_mlstm_m,
           meta_tokens, w_in, b_if, ssm_a_re, ssm_a_im, ssm_log_dt, ssm_b_re, ssm_b_im, ssm_c_re, ssm_c_im,
           ssm_d, w_glu, b_glu, w_a_up, mh_gain, w_b_up, w_out, ln1_g, ln1_b, w_gate, w_up, w_down,
           ln2_g, ln2_b):
    batch, seq, _ = x_prompt.shape
    nb = x_sample.shape[0]
    width = HEADS * DK

    w_glu_b = w_glu[0].astype(BF16)
    w_a_b = w_a_up[0].astype(BF16)
    w_b_b = w_b_up[0].astype(BF16)
    w_out_b = w_out[0].astype(BF16)
    bias_row = b_if[0].reshape(1, 2 * HEADS)
    bias_col = b_if[0].reshape(2 * HEADS, 1)
    gain = mh_gain[0].reshape(1, width)
    d_skip = ssm_d[0].reshape(1, SSM_WIDTH)
    b_glu_r = b_glu[0].reshape(1, SSM_WIDTH)
    ln1 = (ln1_g[0].reshape(1, D_MODEL), ln1_b[0].reshape(1, D_MODEL))
    ln2 = (ln2_g[0].reshape(1, D_MODEL), ln2_b[0].reshape(1, D_MODEL))
    abar_re, abar_im, bt, ct = _s5_discretise(
        ssm_a_re[0], ssm_a_im[0], ssm_log_dt[0], ssm_b_re[0], ssm_b_im[0], ssm_c_re[0], ssm_c_im[0])

    xp = x_prompt.reshape(batch * seq, D_MODEL)
    xs = x_sample.reshape(nb, D_MODEL)
    n_small = 2 * nb
    xsm = jnp.concatenate([xs, meta_tokens, jnp.zeros((n_small - nb - N_META, D_MODEL), F32)], axis=0)
    w_in_t = jnp.swapaxes(w_in, 1, 2)
    wt_gab = w_in_t[0, TAIL_ROW0 + GATE_ROWS:, :].astype(BF16)
    uo_blocks, qkv_blocks = (0, 4, 2), (1, 1, 3)

    uo_p, gcol_p, grow_p = _proj(xp, w_in_t, uo_blocks, F32, 1024, "proj_uo_prompt", with_gates=True)
    qkv_p = _proj(xp, w_in_t, qkv_blocks, BF16, 1024, "proj_qkv_prompt")

    uo_s, gcol_s, _ = _proj(xsm, w_in_t, uo_blocks, F32, n_small, "proj_uo_small", with_gates=True)
    qkv_s = _proj(xsm, w_in_t, qkv_blocks, F32, n_small, "proj_qkv_small")

    u_meta = uo_s[nb:nb + N_META, :SSM_WIDTH]
    qkv_meta = qkv_s[nb:nb + N_META].astype(BF16)
    gcol_meta = gcol_s[nb:nb + N_META]

    ya_p, pf_re, pf_im = _s5_prompt(uo_p, 0, u_meta, abar_re, abar_im, bt, ct, d_skip, batch, seq)
    yb_p, pc, pn, pm = _mlstm_prompt(qkv_p, uo_p, 1, gcol_p, grow_p, qkv_meta, gcol_meta, bias_row, bias_col,
                                     gain, batch, seq)
    x1_p = _merge(ya_p, yb_p, xp, wt_gab, w_glu_b, b_glu_r, w_a_b, w_b_b, w_out_b, *ln1, tm=256)
    y_p = _ffn(x1_p, w_gate, w_up, w_down, *ln2, tm=1024, tf=256)

    ya_s, sf_re, sf_im = _s5_sample(uo_s[:nb, :SSM_WIDTH], state_ssm_re[0].reshape(nb, SSM_FLAT),
                                    state_ssm_im[0].reshape(nb, SSM_FLAT), abar_re, abar_im, bt, ct, d_skip)
    yb_s, sc, sn, sm = _mlstm_sample(qkv_s[:nb], uo_s[:nb, SSM_WIDTH:], gcol_s[:nb], state_mlstm_c[0],
                                     state_mlstm_n[0].reshape(nb, HEADS, 1, DK), state_mlstm_m[0], bias_row, gain)
    x1_s = _merge(ya_s, yb_s.astype(BF16), xs, wt_gab, w_glu_b, b_glu_r, w_a_b, w_b_b, w_out_b, *ln1, tm=nb)
    y_s = _ffn(x1_s, w_gate, w_up, w_down, *ln2, tm=nb, tf=512)

    return (y_p.reshape(batch, seq, D_MODEL),
            y_s.reshape(nb, 1, D_MODEL),
            pf_re.reshape(1, batch, SSM_GROUPS, SSM_STATE),
            pf_im.reshape(1, batch, SSM_GROUPS, SSM_STATE),
            pc.reshape(1, batch, HEADS, DK, DV),
            pn.reshape(1, batch, HEADS, DK),
            pm[:, :, 0].reshape(1, batch, HEADS),
            sf_re.reshape(1, nb, SSM_GROUPS, SSM_STATE),
            sf_im.reshape(1, nb, SSM_GROUPS, SSM_STATE),
            sc.reshape(1, nb, HEADS, DK, DV),
            sn.reshape(1, nb, HEADS, DK),
            sm.reshape(1, nb, HEADS))
```

```python
import functools

import jax
import jax.numpy as jnp
import numpy as np
from jax import lax
from jax.experimental import pallas as pl
from jax.experimental.pallas import tpu as pltpu

F32 = jnp.float32
BF16 = jnp.bfloat16

D_MODEL = 2048
N_META = 16
SSM_WIDTH = 1024
SSM_GROUP = 16
SSM_GROUPS = 64
SSM_STATE = 64
SSM_FLAT = SSM_GROUPS * SSM_STATE
SSM_JBLK = 4
HEADS = 4
DK = 256
DV = 256
CHUNK = 128
D_FF = 5632
EPS = 1e-5
ALPHA = 2.0 ** 0.25

S5_TC = 256
S5_NSEG = 8
S5_SEG = S5_TC // S5_NSEG
S5_LW = 512

V7X_SCOPED_VMEM_MAX_BYTES = 60000 * 1024


def _cparams(n_axes, vmem_mb):
    return pltpu.CompilerParams(
        dimension_semantics=("arbitrary",) * n_axes,
        vmem_limit_bytes=min(vmem_mb * 1024 * 1024, V7X_SCOPED_VMEM_MAX_BYTES),
    )


def _sigmoid(x):
    return 1.0 / (1.0 + jnp.exp(-x))


def _log_sigmoid(x):
    return jnp.minimum(x, 0.0) - jnp.log(1.0 + jnp.exp(-jnp.abs(x)))


def _gelu_tanh(x):
    c = 0.7978845608028654
    return 0.5 * x * (1.0 + jnp.tanh(c * (x + 0.044715 * (x * x * x))))


def _layernorm_rows(x, g, b):
    mu = jnp.mean(x, axis=-1, keepdims=True)
    xc = x - mu
    var = jnp.mean(xc * xc, axis=-1, keepdims=True)
    return xc * lax.rsqrt(var + EPS) * g + b


PROJ_TN = 1024
GATE_ROWS = 2 * HEADS
TAIL_ROW0 = SSM_WIDTH + 4 * HEADS * DK
NT_DIMS = (((1,), (1,)), ((), ()))


def _proj_kernel(x_ref, wt_ref, o_ref, wb_s):
    @pl.when(pl.program_id(1) == 0)
    def _():
        wb_s[...] = wt_ref[...].astype(BF16)

    o_ref[...] = lax.dot_general(x_ref[...].astype(BF16), wb_s[...], NT_DIMS,
                                 preferred_element_type=F32).astype(o_ref.dtype)


def _proj_gates_kernel(x_ref, wt_ref, wg_ref, o_ref, g_ref, gt_ref, wb_s):
    @pl.when(pl.program_id(1) == 0)
    def _():
        wb_s[...] = wt_ref[...].astype(BF16)

    xb = x_ref[...].astype(BF16)
    o_ref[...] = lax.dot_general(xb, wb_s[...], NT_DIMS, preferred_element_type=F32).astype(o_ref.dtype)
    wg = wg_ref[...].astype(BF16)
    g_ref[...] = lax.dot_general(xb, wg, NT_DIMS, preferred_element_type=F32)
    gt_ref[...] = lax.dot_general(wg, xb, NT_DIMS, preferred_element_type=F32)


def _proj(x, wt3, blocks, out_dtype, tm, name, with_gates=False):
    r, k = x.shape
    first, step, count = blocks
    tn = PROJ_TN
    nm = r // tm
    in_specs = [pl.BlockSpec((tm, k), lambda j, i: (i, 0)),
                pl.BlockSpec((None, tn, k), lambda j, i: (0, first + step * j, 0))]
    out_specs = pl.BlockSpec((tm, tn), lambda j, i: (i, j))
    out_shape = jax.ShapeDtypeStruct((r, count * tn), out_dtype)
    args = (x, wt3)
    body = _proj_kernel
    if with_gates:
        ng = GATE_ROWS
        in_specs += [pl.BlockSpec((None, ng, k), lambda j, i: (0, TAIL_ROW0 // ng, 0))]
        out_specs = [out_specs, pl.BlockSpec((tm, ng), lambda j, i: (j * nm + i, 0)),
                     pl.BlockSpec((ng, tm), lambda j, i: (0, j * nm + i))]
        out_shape = [out_shape, jax.ShapeDtypeStruct((count * r, ng), F32), jax.ShapeDtypeStruct((ng, count * r), F32)]
        args = (x, wt3, wt3)
        body = _proj_gates_kernel
    return pl.pallas_call(
        body,
        grid=(count, nm),
        in_specs=in_specs,
        out_specs=out_specs,
        out_shape=out_shape,
        scratch_shapes=[pltpu.VMEM((tn, k), BF16)],
        compiler_params=_cparams(2, 52),
        name=name,
    )(*args)


def _s5_discretise(a_re, a_im, log_dt, b_re, b_im, c_re, c_im):
    dt = jnp.exp(log_dt)
    e = jnp.exp(a_re * dt)
    abar_re = e * jnp.cos(a_im * dt)
    abar_im = e * jnp.sin(a_im * dt)
    nr = abar_re - 1.0
    ni = abar_im
    den = a_re * a_re + a_im * a_im
    coef_re = (nr * a_re + ni * a_im) / den
    coef_im = (ni * a_re - nr * a_im) / den
    bb_re = coef_re[..., None] * b_re - coef_im[..., None] * b_im
    bb_im = coef_re[..., None] * b_im + coef_im[..., None] * b_re
    gpb = SSM_GROUPS // SSM_JBLK

    def b_rows(bb):
        return bb.reshape(SSM_JBLK, gpb, SSM_STATE, SSM_GROUP).transpose(0, 1, 3, 2).reshape(SSM_JBLK, 256, SSM_STATE)

    def c_rows(cc):
        return cc.reshape(SSM_JBLK, gpb, SSM_GROUP, SSM_STATE).transpose(0, 1, 3, 2).reshape(SSM_JBLK, 1024, SSM_GROUP)

    bt = jnp.concatenate([jnp.tile(b_rows(bb_re), (1, 1, 2)), jnp.tile(b_rows(bb_im), (1, 1, 2))], axis=-1)
    ct = jnp.concatenate([jnp.tile(c_rows(c_re), (1, 1, 8)), jnp.tile(c_rows(-c_im), (1, 1, 8))], axis=-1)
    return abar_re.reshape(1, SSM_FLAT), abar_im.reshape(1, SSM_FLAT), bt, ct


def _s5_expand_weights(bt_ref, ct_ref, bw_s, cwr_s, cwi_s):
    rb = lax.broadcasted_iota(jnp.int32, (256, 2048), 0)
    cb = lax.broadcasted_iota(jnp.int32, (256, 2048), 1)
    bmask = (rb >> 4) == ((cb & 1023) >> 6)
    rc = lax.broadcasted_iota(jnp.int32, (1024, 256), 0)
    cc = lax.broadcasted_iota(jnp.int32, (1024, 256), 1)
    cmask = (rc >> 6) == (cc >> 4)
    for j in range(SSM_JBLK):
        bt = bt_ref[j]
        wide = jnp.concatenate([bt[:, :128]] * 8 + [bt[:, 128:]] * 8, axis=1)
        bw_s[j] = jnp.where(bmask, wide, 0.0).astype(BF16)
        ct = ct_ref[j]
        cwr_s[j] = jnp.where(cmask, jnp.concatenate([ct[:, :128]] * 2, axis=1), 0.0).astype(BF16)
        cwi_s[j] = jnp.where(cmask, jnp.concatenate([ct[:, 128:]] * 2, axis=1), 0.0).astype(BF16)


def _cmul_add(ar, ai, hr, hi, xr, xi):
    return ar * hr - ai * hi + xr, ar * hi + ai * hr + xi


def _s5_project_in(u, b_ref, hre_ref, him_ref, rows):
    for j in range(SSM_JBLK):
        bu = jnp.dot(u[:, j * 256:(j + 1) * 256], b_ref[j], preferred_element_type=F32)
        hre_ref[0:rows, j * 1024:(j + 1) * 1024] = bu[:, :1024]
        him_ref[0:rows, j * 1024:(j + 1) * 1024] = bu[:, 1024:]


def _s5_project_out(hre_ref, him_ref, cre_ref, cimn_ref, rows):
    ys = []
    for j in range(SSM_JBLK):
        hr = hre_ref[0:rows, j * 1024:(j + 1) * 1024].astype(BF16)
        hi = him_ref[0:rows, j * 1024:(j + 1) * 1024].astype(BF16)
        y = jnp.dot(hr, cre_ref[j], preferred_element_type=F32)
        ys.append(y + jnp.dot(hi, cimn_ref[j], preferred_element_type=F32))
    return jnp.concatenate(ys, axis=1)


def _s5_prompt_kernel(u_ref, um_ref, perm_ref, permt_ref, bt_ref, ct_ref, are_ref, aim_ref, d_ref,
                      y_ref, fre_ref, fim_ref,
                      hre_ref, him_ref, pre_ref, pim_ref, cre_s, cim_s, hsre_ref, hsim_ref,
                      b_ref, cre_ref, cimn_ref):
    b_id = pl.program_id(0)
    c_id = pl.program_id(1)
    n_c = pl.num_programs(1)

    @pl.when(jnp.logical_and(b_id == 0, c_id == 0))
    def _():
        _s5_expand_weights(bt_ref, ct_ref, b_ref, cre_ref, cimn_ref)
        ar = are_ref[...]
        ai = aim_ref[...]
        pre_ref[0:1, :] = ar
        pim_ref[0:1, :] = ai

        def body(i, carry):
            pr, pi = carry
            nr = ar * pr - ai * pi
            ni = ar * pi + ai * pr
            pre_ref[pl.ds(i, 1), :] = nr
            pim_ref[pl.ds(i, 1), :] = ni
            return nr, ni

        lax.fori_loop(1, S5_SEG, body, (ar, ai))

    @pl.when(c_id == 0)
    def _():
        _s5_project_in(um_ref[...].astype(BF16), b_ref, hre_ref, him_ref, N_META)
        ar = are_ref[...]
        ai = aim_ref[...]
        hr = jnp.zeros((1, SSM_FLAT), F32)
        hi = jnp.zeros((1, SSM_FLAT), F32)
        for t in range(N_META):
            hr, hi = _cmul_add(ar, ai, hr, hi, hre_ref[t:t + 1, :], him_ref[t:t + 1, :])
        cre_s[...] = hr
        cim_s[...] = hi

    u = u_ref[...]
    u_perm = jnp.dot(perm_ref[...], u.astype(BF16), preferred_element_type=F32).astype(BF16)
    _s5_project_in(u_perm, b_ref, hre_ref, him_ref, S5_TC)

    full = (S5_NSEG, S5_LW)
    for lg in range(SSM_FLAT // S5_LW):
        lanes = slice(lg * S5_LW, (lg + 1) * S5_LW)
        ar = jnp.broadcast_to(are_ref[:, lanes], full)
        ai = jnp.broadcast_to(aim_ref[:, lanes], full)

        def local_step(k, carry):
            rows = pl.ds(pl.multiple_of(k * S5_NSEG, S5_NSEG), S5_NSEG)
            nr, ni = _cmul_add(ar, ai, carry[0], carry[1], hre_ref[rows, lanes], him_ref[rows, lanes])
            hre_ref[rows, lanes] = nr
            him_ref[rows, lanes] = ni
            return nr, ni

        er, ei = lax.fori_loop(0, S5_SEG, local_step, (jnp.zeros(full, F32), jnp.zeros(full, F32)), unroll=4)

        asr = pre_ref[S5_SEG - 1:S5_SEG, lanes]
        asi = pim_ref[S5_SEG - 1:S5_SEG, lanes]
        hr = cre_s[:, lanes]
        hi = cim_s[:, lanes]
        for j in range(S5_NSEG):
            hsre_ref[j:j + 1, lanes] = hr
            hsim_ref[j:j + 1, lanes] = hi
            hr, hi = _cmul_add(asr, asi, hr, hi, er[j:j + 1, :], ei[j:j + 1, :])
        cre_s[:, lanes] = hr
        cim_s[:, lanes] = hi

        sr = hsre_ref[:, lanes]
        si = hsim_ref[:, lanes]

        def fix_step(k, carry):
            rows = pl.ds(pl.multiple_of(k * S5_NSEG, S5_NSEG), S5_NSEG)
            pr = jnp.broadcast_to(pre_ref[pl.ds(k, 1), lanes], full)
            pi = jnp.broadcast_to(pim_ref[pl.ds(k, 1), lanes], full)
            hre_ref[rows, lanes] = hre_ref[rows, lanes] + (pr * sr - pi * si)
            him_ref[rows, lanes] = him_ref[rows, lanes] + (pr * si + pi * sr)
            return carry

        lax.fori_loop(0, S5_SEG, fix_step, 0, unroll=4)

    y_perm = _s5_project_out(hre_ref, him_ref, cre_ref, cimn_ref, S5_TC)
    permt = permt_ref[...]
    y = sum(jnp.dot(permt, piece, preferred_element_type=F32) for piece in _split3(y_perm))
    y_ref[...] = y + d_ref[...] * u

    @pl.when(c_id == n_c - 1)
    def _():
        fre_ref[0] = cre_s[...]
        fim_ref[0] = cim_s[...]


def _s5_weight_scratch():
    return [pltpu.VMEM((SSM_JBLK, 256, 2048), BF16), pltpu.VMEM((SSM_JBLK, 1024, 256), BF16),
            pltpu.VMEM((SSM_JBLK, 1024, 256), BF16)]


def _s5_prompt(u_arr, u_col_block, u_meta, abar_re, abar_im, bt, ct, d_skip, batch, seq):
    n_c = seq // S5_TC
    const3 = lambda b, c: (0, 0, 0)
    const2 = lambda b, c: (0, 0)
    rows = np.arange(S5_TC)
    perm_np = np.zeros((S5_TC, S5_TC), np.float32)
    perm_np[rows, (rows % S5_NSEG) * S5_SEG + rows // S5_NSEG] = 1.0
    perm = jnp.asarray(perm_np, dtype=BF16)
    permt = jnp.asarray(perm_np.T, dtype=BF16)
    return pl.pallas_call(
        _s5_prompt_kernel,
        grid=(batch, n_c),
        in_specs=[pl.BlockSpec((S5_TC, SSM_WIDTH), lambda b, c: (b * n_c + c, u_col_block)),
                  pl.BlockSpec((N_META, SSM_WIDTH), const2),
                  pl.BlockSpec((S5_TC, S5_TC), const2),
                  pl.BlockSpec((S5_TC, S5_TC), const2),
                  pl.BlockSpec((SSM_JBLK, 256, 256), const3),
                  pl.BlockSpec((SSM_JBLK, 1024, 256), const3),
                  pl.BlockSpec((1, SSM_FLAT), const2),
                  pl.BlockSpec((1, SSM_FLAT), const2),
                  pl.BlockSpec((1, SSM_WIDTH), const2)],
        out_specs=[pl.BlockSpec((S5_TC, SSM_WIDTH), lambda b, c: (b * n_c + c, 0)),
                   pl.BlockSpec((1, 1, SSM_FLAT), lambda b, c: (b, 0, 0)),
                   pl.BlockSpec((1, 1, SSM_FLAT), lambda b, c: (b, 0, 0))],
        out_shape=[jax.ShapeDtypeStruct((batch * seq, SSM_WIDTH), F32),
                   jax.ShapeDtypeStruct((batch, 1, SSM_FLAT), F32),
                   jax.ShapeDtypeStruct((batch, 1, SSM_FLAT), F32)],
        scratch_shapes=[pltpu.VMEM((S5_TC, SSM_FLAT), F32), pltpu.VMEM((S5_TC, SSM_FLAT), F32),
                        pltpu.VMEM((S5_SEG, SSM_FLAT), F32), pltpu.VMEM((S5_SEG, SSM_FLAT), F32),
                        pltpu.VMEM((1, SSM_FLAT), F32), pltpu.VMEM((1, SSM_FLAT), F32),
                        pltpu.VMEM((S5_NSEG, SSM_FLAT), F32), pltpu.VMEM((S5_NSEG, SSM_FLAT), F32)]
                       + _s5_weight_scratch(),
        compiler_params=_cparams(2, 48),
        name="s5_prompt",
    )(u_arr, u_meta, perm, permt, bt, ct, abar_re, abar_im, d_skip)


def _s5_sample_kernel(u_ref, h0re_ref, h0im_ref, bt_ref, ct_ref, are_ref, aim_ref, d_ref,
                      y_ref, hre_ref, him_ref, b_ref, cre_ref, cimn_ref):
    rows = u_ref.shape[0]
    _s5_expand_weights(bt_ref, ct_ref, b_ref, cre_ref, cimn_ref)
    u = u_ref[...]
    _s5_project_in(u.astype(BF16), b_ref, hre_ref, him_ref, rows)
    nr, ni = _cmul_add(are_ref[...], aim_ref[...], h0re_ref[...], h0im_ref[...], hre_ref[...], him_ref[...])
    hre_ref[...] = nr
    him_ref[...] = ni
    y_ref[...] = _s5_project_out(hre_ref, him_ref, cre_ref, cimn_ref, rows) + d_ref[...] * u


def _s5_sample(u, h0_re, h0_im, abar_re, abar_im, bt, ct, d_skip):
    rows = u.shape[0]
    return pl.pallas_call(
        _s5_sample_kernel,
        out_shape=[jax.ShapeDtypeStruct((rows, SSM_WIDTH), F32),
                   jax.ShapeDtypeStruct((rows, SSM_FLAT), F32),
                   jax.ShapeDtypeStruct((rows, SSM_FLAT), F32)],
        scratch_shapes=_s5_weight_scratch(),
        compiler_params=_cparams(0, 48),
        name="s5_sample",
    )(u, h0_re, h0_im, bt, ct, abar_re, abar_im, d_skip)


def _split3(x):
    hi = x.astype(BF16)
    r1 = x - hi.astype(F32)
    mid = r1.astype(BF16)
    lo = (r1 - mid.astype(F32)).astype(BF16)
    return hi, mid, lo


def _tri(n, lower):
    r = lax.broadcasted_iota(jnp.int32, (n, n), 0)
    c = lax.broadcasted_iota(jnp.int32, (n, n), 1)
    return (r >= c) if lower else (r <= c)


def _gate_columns(gcol, bias_row):
    n = gcol.shape[0]
    g = gcol + bias_row
    fl = _log_sigmoid(g[:, HEADS:])
    tril = jnp.where(_tri(n, True), 1.0, 0.0).astype(BF16)
    b = sum(jnp.dot(tril, p, preferred_element_type=F32) for p in _split3(fl))
    return g[:, :HEADS], b


def _gate_rows(grow, bias_col):
    n = grow.shape[1]
    g = grow + bias_col
    fl = _log_sigmoid(g[HEADS:, :])
    triu = jnp.where(_tri(n, False), 1.0, 0.0).astype(BF16)
    b = sum(jnp.dot(p, triu, preferred_element_type=F32) for p in _split3(fl))
    return g[:HEADS, :], b


MLSTM_GROUP = 2


def _eye_bf16(n):
    return jnp.where(_tri(n, True) & _tri(n, False), 1.0, 0.0).astype(BF16)


def _mlstm_state_update(k_bf, v, ig_col, b_col, c_old, n_old, m_old, k_t=None, gate_rows=None):
    n_rows = k_bf.shape[0]
    k_scale = DK ** -0.5
    b_last = b_col[n_rows - 1:n_rows, :]
    e_col = b_last - b_col + ig_col
    m_new = jnp.maximum(b_last + m_old, jnp.max(e_col, axis=0, keepdims=True))
    w_end = jnp.exp(e_col - m_new) * k_scale
    carry = jnp.exp(b_last + m_old - m_new)
    if k_t is None:
        k_t = lax.dot_general(_eye_bf16(DK), k_bf, NT_DIMS, preferred_element_type=F32)
    if gate_rows is None:
        kv = jnp.dot(k_t.astype(BF16), (v.astype(F32) * w_end).astype(BF16), preferred_element_type=F32)
    else:
        ig_row, b_row = gate_rows
        w_end_row = jnp.exp(b_last - b_row + ig_row - m_new) * k_scale
        kv = jnp.dot((k_t * w_end_row).astype(BF16), v, preferred_element_type=F32)
    c_new = carry * c_old + kv
    n_new = carry * n_old + jnp.sum(k_bf.astype(F32) * w_end, axis=0, keepdims=True)
    return c_new, n_new, m_new


def _mlstm_decay(ig_row, b_row, b_col, m_old):
    n_rows = b_col.shape[0]
    dmat = jnp.where(_tri(n_rows, True), b_col + (ig_row - b_row), -jnp.inf)
    inter = b_col + m_old
    m_t = jnp.maximum(inter, jnp.max(dmat, axis=-1, keepdims=True))
    return jnp.exp(dmat - m_t), jnp.exp(inter - m_t), m_t


def _head_norm_gate(h, o, gain):
    mu = jnp.mean(h, axis=-1, keepdims=True)
    hc = h - mu
    var = jnp.mean(hc * hc, axis=-1, keepdims=True)
    return _sigmoid(o) * (hc * lax.rsqrt(var + EPS) * gain)


def _mlstm_prompt_kernel(batch, q_ref, k_ref, v_ref, o_ref, gc_ref, *rest):
    gr_refs = rest[:batch]
    km_ref, vm_ref, gcm_ref, brow_ref, bcol_ref, gain_ref, y_ref, cf_ref, nf_ref, mf_ref = rest[batch:]
    k_scale = DK ** -0.5

    @pl.when(pl.program_id(0) == 0)
    def _():
        ig_c, b_c = _gate_columns(gcm_ref[...], brow_ref[...])
        for h in range(HEADS):
            hs = slice(h * DK, (h + 1) * DK)
            c_new, n_new, m_new = _mlstm_state_update(
                km_ref[:, hs], vm_ref[:, hs], ig_c[:, h:h + 1], b_c[:, h:h + 1],
                jnp.zeros((DK, DV), F32), jnp.zeros((1, DK), F32), jnp.zeros((1, 1), F32))
            for b in range(batch):
                cf_ref[b, h] = c_new
                nf_ref[b, h] = n_new
                mf_ref[b, h:h + 1, :] = jnp.broadcast_to(m_new, (1, 128))

    eye = _eye_bf16(DK)
    for b0 in range(0, batch, MLSTM_GROUP):
        gates = {}
        for b in range(b0, b0 + MLSTM_GROUP):
            gates[b] = _gate_columns(gc_ref[b], brow_ref[...]) + _gate_rows(gr_refs[b][...], bcol_ref[...])
        chains = [(b, h) for b in range(b0, b0 + MLSTM_GROUP) for h in range(HEADS)]

        ops = []
        for b, h in chains:
            hs = slice(h * DK, (h + 1) * DK)
            q = q_ref[b, :, hs]
            k_bf = k_ref[b, :, hs]
            c_old = cf_ref[b, h]
            ops.append(dict(
                q=q, k_bf=k_bf, v=v_ref[b, :, hs], c_old=c_old, n_old=nf_ref[b, h], m_old=mf_ref[b, h:h + 1, 0:1],
                s_raw=lax.dot_general(q, k_bf, NT_DIMS, preferred_element_type=F32),
                qc=jnp.dot(q, c_old.astype(BF16), preferred_element_type=F32),
                k_t=lax.dot_general(eye, k_bf, NT_DIMS, preferred_element_type=F32)))

        for (b, h), c in zip(chains, ops):
            ig_c, b_c, ig_r, b_r = gates[b]
            c["w_intra"], c["w_inter"], c["m_t"] = _mlstm_decay(ig_r[h:h + 1, :], b_r[h:h + 1, :], b_c[:, h:h + 1],
                                                                c["m_old"])

        for (b, h), c in zip(chains, ops):
            hs = slice(h * DK, (h + 1) * DK)
            s = c["s_raw"] * (c["w_intra"] * k_scale)
            num = c["w_inter"] * c["qc"] + jnp.dot(s.astype(BF16), c["v"], preferred_element_type=F32)
            den = (c["w_inter"] * jnp.sum(c["q"].astype(F32) * c["n_old"], axis=-1, keepdims=True)
                   + jnp.sum(s, axis=-1, keepdims=True))
            hid = num / jnp.maximum(jnp.abs(den), jnp.exp(-c["m_t"]))
            y_ref[b, :, hs] = _head_norm_gate(hid, o_ref[b, :, hs], gain_ref[:, hs]).astype(y_ref.dtype)

        for (b, h), c in zip(chains, ops):
            ig_c, b_c, ig_r, b_r = gates[b]
            c_new, n_new, m_new = _mlstm_state_update(c["k_bf"], c["v"], ig_c[:, h:h + 1], b_c[:, h:h + 1],
                                                      c["c_old"], c["n_old"], c["m_old"], k_t=c["k_t"],
                                                      gate_rows=(ig_r[h:h + 1, :], b_r[h:h + 1, :]))
            cf_ref[b, h] = c_new
            nf_ref[b, h] = n_new
            mf_ref[b, h:h + 1, :] = jnp.broadcast_to(m_new, (1, 128))


def _mlstm_prompt(qkv, o_arr, o_col_block, gcol, grow, qkv_meta, gcol_meta, bias_row, bias_col, gain, batch, seq):
    n_c = seq // CHUNK
    width = HEADS * DK
    const2 = lambda c: (0, 0)
    qkv3 = qkv.reshape(batch, seq, 3 * width)
    o3 = o_arr.reshape(batch, seq, o_arr.shape[1])
    gcol3 = gcol.reshape(gcol.shape[0] // seq, seq, 2 * HEADS)
    state0 = lambda c: (0, 0, 0, 0)
    return pl.pallas_call(
        functools.partial(_mlstm_prompt_kernel, batch),
        grid=(n_c,),
        in_specs=[pl.BlockSpec((batch, CHUNK, width), lambda c: (0, c, 0)),
                  pl.BlockSpec((batch, CHUNK, width), lambda c: (0, c, 1)),
                  pl.BlockSpec((batch, CHUNK, width), lambda c: (0, c, 2)),
                  pl.BlockSpec((batch, CHUNK, width), lambda c: (0, c, o_col_block)),
                  pl.BlockSpec((batch, CHUNK, 2 * HEADS), lambda c: (0, c, 0))]
                 + [pl.BlockSpec((2 * HEADS, CHUNK), functools.partial(lambda b, c: (0, b * n_c + c), b))
                    for b in range(batch)]
                 + [pl.BlockSpec((N_META, width), lambda c: (0, 1)),
                    pl.BlockSpec((N_META, width), lambda c: (0, 2)),
                    pl.BlockSpec((N_META, 2 * HEADS), const2),
                    pl.BlockSpec((1, 2 * HEADS), const2),
                    pl.BlockSpec((2 * HEADS, 1), const2),
                    pl.BlockSpec((1, width), const2)],
        out_specs=[pl.BlockSpec((batch, CHUNK, width), lambda c: (0, c, 0)),
                   pl.BlockSpec((batch, HEADS, DK, DV), state0),
                   pl.BlockSpec((batch, HEADS, 1, DK), state0),
                   pl.BlockSpec((batch, HEADS, 128), lambda c: (0, 0, 0))],
        out_shape=[jax.ShapeDtypeStruct((batch, seq, width), BF16),
                   jax.ShapeDtypeStruct((batch, HEADS, DK, DV), F32),
                   jax.ShapeDtypeStruct((batch, HEADS, 1, DK), F32),
                   jax.ShapeDtypeStruct((batch, HEADS, 128), F32)],
        compiler_params=_cparams(1, 40),
        name="mlstm_prompt",
    )(qkv3, qkv3, qkv3, o3, gcol3, *([grow] * batch), qkv_meta, qkv_meta, gcol_meta, bias_row, bias_col, gain)


MLSTM_SB = 8


def _mlstm_sample_kernel(qkv_ref, o_ref, g_ref, c_ref, n_ref, m_ref, brow_ref, gain_ref,
                         y_ref, co_ref, no_ref, mo_ref):
    k_scale = DK ** -0.5
    eye = _tri(DK, True) & _tri(DK, False)
    width = HEADS * DK

    def body(i, carry):
        g = g_ref[pl.ds(i, 1), :] + brow_ref[...]
        m_row = m_ref[pl.ds(i, 1), :]
        for h in range(HEADS):
            hs = slice(h * DK, (h + 1) * DK)
            ig = g[:, h:h + 1]
            fl = _log_sigmoid(g[:, HEADS + h:HEADS + h + 1])
            m_old = m_row[:, h:h + 1]
            m_new = jnp.maximum(fl + m_old, ig)
            w_in = jnp.exp(ig - m_new)
            w_ca = jnp.exp(fl + m_old - m_new)
            q = qkv_ref[pl.ds(i, 1), hs].astype(BF16)
            k = (qkv_ref[pl.ds(i, 1), width + h * DK:width + (h + 1) * DK] * k_scale).astype(BF16)
            v = qkv_ref[pl.ds(i, 1), 2 * width + h * DK:2 * width + (h + 1) * DK].astype(BF16)
            qf = q.astype(F32)
            kf = k.astype(F32)
            vf = v.astype(F32)
            c_old = c_ref[i, h]
            n_old = n_ref[i, h]
            s = jnp.sum(qf * kf, axis=-1, keepdims=True) * w_in
            qc = jnp.dot(jnp.broadcast_to(q, (8, DK)), c_old.astype(BF16), preferred_element_type=F32)[0:1, :]
            num = w_ca * qc + s * vf
            den = w_ca * jnp.sum(qf * n_old, axis=-1, keepdims=True) + s
            hid = num / jnp.maximum(jnp.abs(den), jnp.exp(-m_new))
            kw = kf * w_in
            kdiag = jnp.where(eye, jnp.broadcast_to(kw, (DK, DK)), 0.0).astype(BF16)
            kv = jnp.dot(kdiag, jnp.broadcast_to(v, (DK, DV)), preferred_element_type=F32)
            co_ref[i, h] = w_ca * c_old + kv
            no_ref[i, h] = w_ca * n_old + kw
            mo_ref[pl.ds(i, 1), h:h + 1] = m_new
            y_ref[pl.ds(i, 1), hs] = _head_norm_gate(hid, o_ref[pl.ds(i, 1), hs], gain_ref[:, hs])
        return carry

    lax.fori_loop(0, MLSTM_SB, body, 0)


def _mlstm_sample(qkv, o_arr, gcol, c0, n0, m0, bias_row, gain):
    nb = qkv.shape[0]
    width = HEADS * DK
    sb = MLSTM_SB
    const2 = lambda i: (0, 0)
    return pl.pallas_call(
        _mlstm_sample_kernel,
        grid=(nb // sb,),
        in_specs=[pl.BlockSpec((sb, 3 * width), lambda i: (i, 0)),
                  pl.BlockSpec((sb, width), lambda i: (i, 0)),
                  pl.BlockSpec((sb, 2 * HEADS), lambda i: (i, 0)),
                  pl.BlockSpec((sb, HEADS, DK, DV), lambda i: (i, 0, 0, 0)),
                  pl.BlockSpec((sb, HEADS, 1, DK), lambda i: (i, 0, 0, 0)),
                  pl.BlockSpec((sb, HEADS), lambda i: (i, 0)),
                  pl.BlockSpec((1, 2 * HEADS), const2),
                  pl.BlockSpec((1, width), const2)],
        out_specs=[pl.BlockSpec((sb, width), lambda i: (i, 0)),
                   pl.BlockSpec((sb, HEADS, DK, DV), lambda i: (i, 0, 0, 0)),
                   pl.BlockSpec((sb, HEADS, 1, DK), lambda i: (i, 0, 0, 0)),
                   pl.BlockSpec((sb, HEADS), lambda i: (i, 0))],
        out_shape=[jax.ShapeDtypeStruct((nb, width), F32),
                   jax.ShapeDtypeStruct((nb, HEADS, DK, DV), F32),
                   jax.ShapeDtypeStruct((nb, HEADS, 1, DK), F32),
                   jax.ShapeDtypeStruct((nb, HEADS), F32)],
        compiler_params=_cparams(1, 48),
        name="mlstm_sample",
    )(qkv, o_arr, gcol, c0, n0, m0, bias_row, gain)


def _merge_kernel(ya_ref, yb_ref, x_ref, wgab_ref, wglu_ref, bglu_ref, wa_ref, wb_ref, wout_ref,
                  g1_ref, b1_ref, o_ref):
    x = x_ref[...]
    gab = lax.dot_general(x.astype(BF16), wgab_ref[...], NT_DIMS, preferred_element_type=F32)
    g = _gelu_tanh(ya_ref[...])
    z = jnp.dot(g.astype(BF16), wglu_ref[...], preferred_element_type=F32) + bglu_ref[...]
    out_a = g * _sigmoid(z)
    up_a = jnp.dot(out_a.astype(BF16), wa_ref[...], preferred_element_type=F32)
    up_b = jnp.dot(yb_ref[...], wb_ref[...], preferred_element_type=F32)
    mix = _sigmoid(gab[:, :D_MODEL]) * up_a + _sigmoid(gab[:, D_MODEL:]) * up_b
    mo = jnp.dot(mix.astype(BF16), wout_ref[...], preferred_element_type=F32)
    o_ref[...] = _layernorm_rows(ALPHA * x + mo, g1_ref[...], b1_ref[...])


def _merge(ya, yb, x, wt_gab, w_glu, b_glu, w_a_up, w_b_up, w_out, ln_g, ln_b, tm):
    r = x.shape[0]
    const2 = lambda i: (0, 0)
    resident = functools.partial(pl.BlockSpec, index_map=const2, pipeline_mode=pl.Buffered(1))
    return pl.pallas_call(
        _merge_kernel,
        grid=(r // tm,),
        in_specs=[pl.BlockSpec((tm, SSM_WIDTH), lambda i: (i, 0)),
                  pl.BlockSpec((tm, HEADS * DV), lambda i: (i, 0)),
                  pl.BlockSpec((tm, D_MODEL), lambda i: (i, 0)),
                  resident((2 * D_MODEL, D_MODEL)),
                  resident((SSM_WIDTH, SSM_WIDTH)),
                  resident((1, SSM_WIDTH)),
                  resident((SSM_WIDTH, D_MODEL)),
                  resident((HEADS * DV, D_MODEL)),
                  resident((D_MODEL, D_MODEL)),
                  resident((1, D_MODEL)),
                  resident((1, D_MODEL))],
        out_specs=pl.BlockSpec((tm, D_MODEL), lambda i: (i, 0)),
        out_shape=jax.ShapeDtypeStruct((r, D_MODEL), F32),
        compiler_params=_cparams(1, 58),
        name="merge_ln1",
    )(ya, yb, x, wt_gab, w_glu, b_glu, w_a_up, w_b_up, w_out, ln_g, ln_b)


def _ffn_kernel(x_ref, wg_ref, wu_ref, wd_ref, g2_ref, b2_ref, o_ref, xb_s):
    f_id = pl.program_id(1)
    n_f = pl.num_programs(1)

    @pl.when(f_id == 0)
    def _():
        xb_s[...] = x_ref[...].astype(BF16)
        o_ref[...] = jnp.zeros_like(o_ref)

    xb = xb_s[...]
    hg = jnp.dot(xb, wg_ref[...].astype(BF16), preferred_element_type=F32)
    hu = jnp.dot(xb, wu_ref[...].astype(BF16), preferred_element_type=F32)
    act = (hg * _sigmoid(hg)) * hu
    o_ref[...] += jnp.dot(act.astype(BF16), wd_ref[...].astype(BF16), preferred_element_type=F32)

    @pl.when(f_id == n_f - 1)
    def _():
        o_ref[...] = _layernorm_rows(ALPHA * x_ref[...] + o_ref[...], g2_ref[...], b2_ref[...])


def _ffn(x, w_gate3, w_up3, w_down3, ln_g, ln_b, tm, tf):
    r = x.shape[0]
    const2 = lambda i, f: (0, 0)
    return pl.pallas_call(
        _ffn_kernel,
        grid=(r // tm, D_FF // tf),
        in_specs=[pl.BlockSpec((tm, D_MODEL), lambda i, f: (i, 0)),
                  pl.BlockSpec((None, D_MODEL, tf), lambda i, f: (0, 0, f)),
                  pl.BlockSpec((None, D_MODEL, tf), lambda i, f: (0, 0, f)),
                  pl.BlockSpec((None, tf, D_MODEL), lambda i, f: (0, f, 0)),
                  pl.BlockSpec((1, D_MODEL), const2),
                  pl.BlockSpec((1, D_MODEL), const2)],
        out_specs=pl.BlockSpec((tm, D_MODEL), lambda i, f: (i, 0)),
        out_shape=jax.ShapeDtypeStruct((r, D_MODEL), F32),
        scratch_shapes=[pltpu.VMEM((tm, D_MODEL), BF16)],
        compiler_params=_cparams(2, 59),
        name="ffn_ln2",
    )(x, w_gate3, w_up3, w_down3, ln_g, ln_b)


def kernel(x_prompt, x_sample, state_ssm_re, state_ssm_im, state_mlstm_c, state_mlstm_n, state_mlstm_m,
           meta_tokens, w_in, b_if, ssm_a_re, ssm_a_im, ssm_log_dt, ssm_b_re, ssm_b_im, ssm_c_re, ssm_c_im,
           ssm_d, w_glu, b_glu, w_a_up, mh_gain, w_b_up, w_out, ln1_g, ln1_b, w_gate, w_up, w_down,
           ln2_g, ln2_b):
    batch, seq, _ = x_prompt.shape
    nb = x_sample.shape[0]
    width = HEADS * DK

    w_glu_b = w_glu[0].astype(BF16)
    w_a_b = w_a_up[0].astype(BF16)
    w_b_b = w_b_up[0].astype(BF16)
    w_out_b = w_out[0].astype(BF16)
    bias_row = b_if[0].reshape(1, 2 * HEADS)
    bias_col = b_if[0].reshape(2 * HEADS, 1)
    gain = mh_gain[0].reshape(1, width)
    d_skip = ssm_d[0].reshape(1, SSM_WIDTH)
    b_glu_r = b_glu[0].reshape(1, SSM_WIDTH)
    ln1 = (ln1_g[0].reshape(1, D_MODEL), ln1_b[0].reshape(1, D_MODEL))
    ln2 = (ln2_g[0].reshape(1, D_MODEL), ln2_b[0].reshape(1, D_MODEL))
    abar_re, abar_im, bt, ct = _s5_discretise(
        ssm_a_re[0], ssm_a_im[0], ssm_log_dt[0], ssm_b_re[0], ssm_b_im[0], ssm_c_re[0], ssm_c_im[0])

    xp = x_prompt.reshape(batch * seq, D_MODEL)
    xs = x_sample.reshape(nb, D_MODEL)
    n_small = 2 * nb
    xsm = jnp.concatenate([xs, meta_tokens, jnp.zeros((n_small - nb - N_META, D_MODEL), F32)], axis=0)
    w_in_t = jnp.swapaxes(w_in, 1, 2)
    wt_gab = w_in_t[0, TAIL_ROW0 + GATE_ROWS:, :].astype(BF16)
    uo_blocks, qkv_blocks = (0, 4, 2), (1, 1, 3)

    uo_p, gcol_p, grow_p = _proj(xp, w_in_t, uo_blocks, F32, 1024, "proj_uo_prompt", with_gates=True)
    qkv_p = _proj(xp, w_in_t, qkv_blocks, BF16, 1024, "proj_qkv_prompt")

    uo_s, gcol_s, _ = _proj(xsm, w_in_t, uo_blocks, F32, n_small, "proj_uo_small", with_gates=True)
    qkv_s = _proj(xsm, w_in_t, qkv_blocks, F32, n_small, "proj_qkv_small")

    u_meta = uo_s[nb:nb + N_META, :SSM_WIDTH]
    qkv_meta = qkv_s[nb:nb + N_META].astype(BF16)
    gcol_meta = gcol_s[nb:nb + N_META]

    ya_p, pf_re, pf_im = _s5_prompt(uo_p, 0, u_meta, abar_re, abar_im, bt, ct, d_skip, batch, seq)
    yb_p, pc, pn, pm = _mlstm_prompt(qkv_p, uo_p, 1, gcol_p, grow_p, qkv_meta, gcol_meta, bias_row, bias_col,
                                     gain, batch, seq)
    x1_p = _merge(ya_p, yb_p.reshape(batch * seq, width), xp, wt_gab, w_glu_b, b_glu_r, w_a_b, w_b_b, w_out_b, *ln1, tm=256)
    y_p = _ffn(x1_p, w_gate, w_up, w_down, *ln2, tm=1024, tf=256)

    ya_s, sf_re, sf_im = _s5_sample(uo_s[:nb, :SSM_WIDTH], state_ssm_re[0].reshape(nb, SSM_FLAT),
                                    state_ssm_im[0].reshape(nb, SSM_FLAT), abar_re, abar_im, bt, ct, d_skip)
    yb_s, sc, sn, sm = _mlstm_sample(qkv_s[:nb], uo_s[:nb, SSM_WIDTH:], gcol_s[:nb], state_mlstm_c[0],
                                     state_mlstm_n[0].reshape(nb, HEADS, 1, DK), state_mlstm_m[0], bias_row, gain)
    x1_s = _merge(ya_s, yb_s.astype(BF16), xs, wt_gab, w_glu_b, b_glu_r, w_a_b, w_b_b, w_out_b, *ln1, tm=nb)
    y_s = _ffn(x1_s, w_gate, w_up, w_down, *ln2, tm=nb, tf=512)

    return (y_p.reshape(batch, seq, D_MODEL),
            y_s.reshape(nb, 1, D_MODEL),
            pf_re.reshape(1, batch, SSM_GROUPS, SSM_STATE),
            pf_im.reshape(1, batch, SSM_GROUPS, SSM_STATE),
            pc.reshape(1, batch, HEADS, DK, DV),
            pn.reshape(1, batch, HEADS, DK),
            pm[:, :, 0].reshape(1, batch, HEADS),
            sf_re.reshape(1, nb, SSM_GROUPS, SSM_STATE),
            sf_im.reshape(1, nb, SSM_GROUPS, SSM_STATE),
            sc.reshape(1, nb, HEADS, DK, DV),
            sn.reshape(1, nb, HEADS, DK),
            sm.reshape(1, nb, HEADS))
```

```python
import functools

import jax
import jax.numpy as jnp
import numpy as np
from jax import lax
from jax.experimental import pallas as pl
from jax.experimental.pallas import tpu as pltpu

F32 = jnp.float32
BF16 = jnp.bfloat16

D_MODEL = 2048
N_META = 16
SSM_WIDTH = 1024
SSM_GROUP = 16
SSM_GROUPS = 64
SSM_STATE = 64
SSM_FLAT = SSM_GROUPS * SSM_STATE
SSM_JBLK = 4
HEADS = 4
DK = 256
DV = 256
CHUNK = 128
D_FF = 5632
EPS = 1e-5
ALPHA = 2.0 ** 0.25

S5_TC = 256
S5_NSEG = 8
S5_SEG = S5_TC // S5_NSEG
S5_LW = 512

V7X_SCOPED_VMEM_MAX_BYTES = 60000 * 1024


def _cparams(n_axes, vmem_mb):
    return pltpu.CompilerParams(
        dimension_semantics=("arbitrary",) * n_axes,
        vmem_limit_bytes=min(vmem_mb * 1024 * 1024, V7X_SCOPED_VMEM_MAX_BYTES),
    )


def _sigmoid(x):
    return 1.0 / (1.0 + jnp.exp(-x))


def _log_sigmoid(x):
    return jnp.minimum(x, 0.0) - jnp.log(1.0 + jnp.exp(-jnp.abs(x)))


def _gelu_tanh(x):
    c = 0.7978845608028654
    return 0.5 * x * (1.0 + jnp.tanh(c * (x + 0.044715 * (x * x * x))))


def _layernorm_rows(x, g, b):
    mu = jnp.mean(x, axis=-1, keepdims=True)
    xc = x - mu
    var = jnp.mean(xc * xc, axis=-1, keepdims=True)
    return xc * lax.rsqrt(var + EPS) * g + b


PROJ_TN = 1024
GATE_ROWS = 2 * HEADS
TAIL_ROW0 = SSM_WIDTH + 4 * HEADS * DK
NT_DIMS = (((1,), (1,)), ((), ()))


def _proj_kernel(x_ref, wt_ref, o_ref, wb_s):
    @pl.when(pl.program_id(1) == 0)
    def _():
        wb_s[...] = wt_ref[...].astype(BF16)

    o_ref[...] = lax.dot_general(x_ref[...].astype(BF16), wb_s[...], NT_DIMS,
                                 preferred_element_type=F32).astype(o_ref.dtype)


def _proj_gates_kernel(x_ref, wt_ref, wg_ref, o_ref, g_ref, gt_ref, wb_s):
    @pl.when(pl.program_id(1) == 0)
    def _():
        wb_s[...] = wt_ref[...].astype(BF16)

    xb = x_ref[...].astype(BF16)
    o_ref[...] = lax.dot_general(xb, wb_s[...], NT_DIMS, preferred_element_type=F32).astype(o_ref.dtype)
    wg = wg_ref[...].astype(BF16)
    g_ref[...] = lax.dot_general(xb, wg, NT_DIMS, preferred_element_type=F32)
    gt_ref[...] = lax.dot_general(wg, xb, NT_DIMS, preferred_element_type=F32)


def _proj(x, wt3, blocks, out_dtype, tm, name, with_gates=False):
    r, k = x.shape
    first, step, count = blocks
    tn = PROJ_TN
    nm = r // tm
    in_specs = [pl.BlockSpec((tm, k), lambda j, i: (i, 0)),
                pl.BlockSpec((None, tn, k), lambda j, i: (0, first + step * j, 0))]
    out_specs = pl.BlockSpec((tm, tn), lambda j, i: (i, j))
    out_shape = jax.ShapeDtypeStruct((r, count * tn), out_dtype)
    args = (x, wt3)
    body = _proj_kernel
    if with_gates:
        ng = GATE_ROWS
        in_specs += [pl.BlockSpec((None, ng, k), lambda j, i: (0, TAIL_ROW0 // ng, 0))]
        out_specs = [out_specs, pl.BlockSpec((tm, ng), lambda j, i: (j * nm + i, 0)),
                     pl.BlockSpec((ng, tm), lambda j, i: (0, j * nm + i))]
        out_shape = [out_shape, jax.ShapeDtypeStruct((count * r, ng), F32), jax.ShapeDtypeStruct((ng, count * r), F32)]
        args = (x, wt3, wt3)
        body = _proj_gates_kernel
    return pl.pallas_call(
        body,
        grid=(count, nm),
        in_specs=in_specs,
        out_specs=out_specs,
        out_shape=out_shape,
        scratch_shapes=[pltpu.VMEM((tn, k), BF16)],
        compiler_params=_cparams(2, 52),
        name=name,
    )(*args)


def _s5_discretise(a_re, a_im, log_dt, b_re, b_im, c_re, c_im):
    dt = jnp.exp(log_dt)
    e = jnp.exp(a_re * dt)
    abar_re = e * jnp.cos(a_im * dt)
    abar_im = e * jnp.sin(a_im * dt)
    nr = abar_re - 1.0
    ni = abar_im
    den = a_re * a_re + a_im * a_im
    coef_re = (nr * a_re + ni * a_im) / den
    coef_im = (ni * a_re - nr * a_im) / den
    bb_re = coef_re[..., None] * b_re - coef_im[..., None] * b_im
    bb_im = coef_re[..., None] * b_im + coef_im[..., None] * b_re
    gpb = SSM_GROUPS // SSM_JBLK

    def b_rows(bb):
        return bb.reshape(SSM_JBLK, gpb, SSM_STATE, SSM_GROUP).transpose(0, 1, 3, 2).reshape(SSM_JBLK, 256, SSM_STATE)

    def c_rows(cc):
        return cc.reshape(SSM_JBLK, gpb, SSM_GROUP, SSM_STATE).transpose(0, 1, 3, 2).reshape(SSM_JBLK, 1024, SSM_GROUP)

    bt = jnp.concatenate([jnp.tile(b_rows(bb_re), (1, 1, 2)), jnp.tile(b_rows(bb_im), (1, 1, 2))], axis=-1)
    ct = jnp.concatenate([jnp.tile(c_rows(c_re), (1, 1, 8)), jnp.tile(c_rows(-c_im), (1, 1, 8))], axis=-1)
    return abar_re.reshape(1, SSM_FLAT), abar_im.reshape(1, SSM_FLAT), bt, ct


def _s5_expand_weights(bt_ref, ct_ref, bw_s, cwr_s, cwi_s):
    rb = lax.broadcasted_iota(jnp.int32, (256, 2048), 0)
    cb = lax.broadcasted_iota(jnp.int32, (256, 2048), 1)
    bmask = (rb >> 4) == ((cb & 1023) >> 6)
    rc = lax.broadcasted_iota(jnp.int32, (1024, 256), 0)
    cc = lax.broadcasted_iota(jnp.int32, (1024, 256), 1)
    cmask = (rc >> 6) == (cc >> 4)
    for j in range(SSM_JBLK):
        bt = bt_ref[j]
        wide = jnp.concatenate([bt[:, :128]] * 8 + [bt[:, 128:]] * 8, axis=1)
        bw_s[j] = jnp.where(bmask, wide, 0.0).astype(BF16)
        ct = ct_ref[j]
        cwr_s[j] = jnp.where(cmask, jnp.concatenate([ct[:, :128]] * 2, axis=1), 0.0).astype(BF16)
        cwi_s[j] = jnp.where(cmask, jnp.concatenate([ct[:, 128:]] * 2, axis=1), 0.0).astype(BF16)


def _cmul_add(ar, ai, hr, hi, xr, xi):
    return ar * hr - ai * hi + xr, ar * hi + ai * hr + xi


def _s5_project_in(u, b_ref, hre_ref, him_ref, rows):
    for j in range(SSM_JBLK):
        bu = jnp.dot(u[:, j * 256:(j + 1) * 256], b_ref[j], preferred_element_type=F32)
        hre_ref[0:rows, j * 1024:(j + 1) * 1024] = bu[:, :1024]
        him_ref[0:rows, j * 1024:(j + 1) * 1024] = bu[:, 1024:]


def _s5_project_out(hre_ref, him_ref, cre_ref, cimn_ref, rows):
    ys = []
    for j in range(SSM_JBLK):
        hr = hre_ref[0:rows, j * 1024:(j + 1) * 1024].astype(BF16)
        hi = him_ref[0:rows, j * 1024:(j + 1) * 1024].astype(BF16)
        y = jnp.dot(hr, cre_ref[j], preferred_element_type=F32)
        ys.append(y + jnp.dot(hi, cimn_ref[j], preferred_element_type=F32))
    return jnp.concatenate(ys, axis=1)


def _s5_prompt_kernel(u_ref, um_ref, perm_ref, permt_ref, bt_ref, ct_ref, are_ref, aim_ref, d_ref,
                      y_ref, fre_ref, fim_ref,
                      hre_ref, him_ref, pre_ref, pim_ref, cre_s, cim_s, hsre_ref, hsim_ref,
                      b_ref, cre_ref, cimn_ref):
    b_id = pl.program_id(0)
    c_id = pl.program_id(1)
    n_c = pl.num_programs(1)

    @pl.when(jnp.logical_and(b_id == 0, c_id == 0))
    def _():
        _s5_expand_weights(bt_ref, ct_ref, b_ref, cre_ref, cimn_ref)
        ar = jnp.broadcast_to(are_ref[...], (S5_NSEG, SSM_FLAT))
        ai = jnp.broadcast_to(aim_ref[...], (S5_NSEG, SSM_FLAT))
        pre_ref[0:S5_NSEG, :] = ar
        pim_ref[0:S5_NSEG, :] = ai

        def body(i, carry):
            pr, pi = carry
            nr = ar * pr - ai * pi
            ni = ar * pi + ai * pr
            rows = pl.ds(pl.multiple_of(i * S5_NSEG, S5_NSEG), S5_NSEG)
            pre_ref[rows, :] = nr
            pim_ref[rows, :] = ni
            return nr, ni

        lax.fori_loop(1, S5_SEG, body, (ar, ai))

    @pl.when(c_id == 0)
    def _():
        _s5_project_in(um_ref[...].astype(BF16), b_ref, hre_ref, him_ref, N_META)
        ar = are_ref[...]
        ai = aim_ref[...]
        hr = jnp.zeros((1, SSM_FLAT), F32)
        hi = jnp.zeros((1, SSM_FLAT), F32)
        for t in range(N_META):
            hr, hi = _cmul_add(ar, ai, hr, hi, hre_ref[t:t + 1, :], him_ref[t:t + 1, :])
        cre_s[...] = hr
        cim_s[...] = hi

    u = u_ref[...]
    u_perm = jnp.dot(perm_ref[...], u.astype(BF16), preferred_element_type=F32).astype(BF16)
    _s5_project_in(u_perm, b_ref, hre_ref, him_ref, S5_TC)

    full = (S5_NSEG, S5_LW)
    for lg in range(SSM_FLAT // S5_LW):
        lanes = slice(lg * S5_LW, (lg + 1) * S5_LW)
        ar = jnp.broadcast_to(are_ref[:, lanes], full)
        ai = jnp.broadcast_to(aim_ref[:, lanes], full)

        def local_step(k, carry):
            rows = pl.ds(pl.multiple_of(k * S5_NSEG, S5_NSEG), S5_NSEG)
            nr, ni = _cmul_add(ar, ai, carry[0], carry[1], hre_ref[rows, lanes], him_ref[rows, lanes])
            hre_ref[rows, lanes] = nr
            him_ref[rows, lanes] = ni
            return nr, ni

        er, ei = lax.fori_loop(0, S5_SEG, local_step, (jnp.zeros(full, F32), jnp.zeros(full, F32)), unroll=4)

        asr = pre_ref[S5_TC - 1:S5_TC, lanes]
        asi = pim_ref[S5_TC - 1:S5_TC, lanes]
        hr = cre_s[:, lanes]
        hi = cim_s[:, lanes]
        for j in range(S5_NSEG):
            hsre_ref[j:j + 1, lanes] = hr
            hsim_ref[j:j + 1, lanes] = hi
            hr, hi = _cmul_add(asr, asi, hr, hi, er[j:j + 1, :], ei[j:j + 1, :])
        cre_s[:, lanes] = hr
        cim_s[:, lanes] = hi

        sr = hsre_ref[:, lanes]
        si = hsim_ref[:, lanes]

        def fix_step(k, carry):
            rows = pl.ds(pl.multiple_of(k * S5_NSEG, S5_NSEG), S5_NSEG)
            pr = pre_ref[rows, lanes]
            pi = pim_ref[rows, lanes]
            hre_ref[rows, lanes] = hre_ref[rows, lanes] + (pr * sr - pi * si)
            him_ref[rows, lanes] = him_ref[rows, lanes] + (pr * si + pi * sr)
            return carry

        lax.fori_loop(0, S5_SEG, fix_step, 0, unroll=4)

    y_perm = _s5_project_out(hre_ref, him_ref, cre_ref, cimn_ref, S5_TC)
    permt = permt_ref[...]
    y = sum(jnp.dot(permt, piece, preferred_element_type=F32) for piece in _split3(y_perm))
    y_ref[...] = y + d_ref[...] * u

    @pl.when(c_id == n_c - 1)
    def _():
        fre_ref[0] = cre_s[...]
        fim_ref[0] = cim_s[...]


def _s5_weight_scratch():
    return [pltpu.VMEM((SSM_JBLK, 256, 2048), BF16), pltpu.VMEM((SSM_JBLK, 1024, 256), BF16),
            pltpu.VMEM((SSM_JBLK, 1024, 256), BF16)]


def _s5_prompt(u_arr, u_col_block, u_meta, abar_re, abar_im, bt, ct, d_skip, batch, seq):
    n_c = seq // S5_TC
    const3 = lambda b, c: (0, 0, 0)
    const2 = lambda b, c: (0, 0)
    rows = np.arange(S5_TC)
    perm_np = np.zeros((S5_TC, S5_TC), np.float32)
    perm_np[rows, (rows % S5_NSEG) * S5_SEG + rows // S5_NSEG] = 1.0
    perm = jnp.asarray(perm_np, dtype=BF16)
    permt = jnp.asarray(perm_np.T, dtype=BF16)
    return pl.pallas_call(
        _s5_prompt_kernel,
        grid=(batch, n_c),
        in_specs=[pl.BlockSpec((S5_TC, SSM_WIDTH), lambda b, c: (b * n_c + c, u_col_block)),
                  pl.BlockSpec((N_META, SSM_WIDTH), const2),
                  pl.BlockSpec((S5_TC, S5_TC), const2),
                  pl.BlockSpec((S5_TC, S5_TC), const2),
                  pl.BlockSpec((SSM_JBLK, 256, 256), const3),
                  pl.BlockSpec((SSM_JBLK, 1024, 256), const3),
                  pl.BlockSpec((1, SSM_FLAT), const2),
                  pl.BlockSpec((1, SSM_FLAT), const2),
                  pl.BlockSpec((1, SSM_WIDTH), const2)],
        out_specs=[pl.BlockSpec((S5_TC, SSM_WIDTH), lambda b, c: (b * n_c + c, 0)),
                   pl.BlockSpec((1, 1, SSM_FLAT), lambda b, c: (b, 0, 0)),
                   pl.BlockSpec((1, 1, SSM_FLAT), lambda b, c: (b, 0, 0))],
        out_shape=[jax.ShapeDtypeStruct((batch * seq, SSM_WIDTH), F32),
                   jax.ShapeDtypeStruct((batch, 1, SSM_FLAT), F32),
                   jax.ShapeDtypeStruct((batch, 1, SSM_FLAT), F32)],
        scratch_shapes=[pltpu.VMEM((S5_TC, SSM_FLAT), F32), pltpu.VMEM((S5_TC, SSM_FLAT), F32),
                        pltpu.VMEM((S5_TC, SSM_FLAT), F32), pltpu.VMEM((S5_TC, SSM_FLAT), F32),
                        pltpu.VMEM((1, SSM_FLAT), F32), pltpu.VMEM((1, SSM_FLAT), F32),
                        pltpu.VMEM((S5_NSEG, SSM_FLAT), F32), pltpu.VMEM((S5_NSEG, SSM_FLAT), F32)]
                       + _s5_weight_scratch(),
        compiler_params=_cparams(2, 48),
        name="s5_prompt",
    )(u_arr, u_meta, perm, permt, bt, ct, abar_re, abar_im, d_skip)


def _s5_sample_kernel(u_ref, h0re_ref, h0im_ref, bt_ref, ct_ref, are_ref, aim_ref, d_ref,
                      y_ref, hre_ref, him_ref, b_ref, cre_ref, cimn_ref):
    rows = u_ref.shape[0]
    _s5_expand_weights(bt_ref, ct_ref, b_ref, cre_ref, cimn_ref)
    u = u_ref[...]
    _s5_project_in(u.astype(BF16), b_ref, hre_ref, him_ref, rows)
    nr, ni = _cmul_add(are_ref[...], aim_ref[...], h0re_ref[...], h0im_ref[...], hre_ref[...], him_ref[...])
    hre_ref[...] = nr
    him_ref[...] = ni
    y_ref[...] = _s5_project_out(hre_ref, him_ref, cre_ref, cimn_ref, rows) + d_ref[...] * u


def _s5_sample(u, h0_re, h0_im, abar_re, abar_im, bt, ct, d_skip):
    rows = u.shape[0]
    return pl.pallas_call(
        _s5_sample_kernel,
        out_shape=[jax.ShapeDtypeStruct((rows, SSM_WIDTH), F32),
                   jax.ShapeDtypeStruct((rows, SSM_FLAT), F32),
                   jax.ShapeDtypeStruct((rows, SSM_FLAT), F32)],
        scratch_shapes=_s5_weight_scratch(),
        compiler_params=_cparams(0, 48),
        name="s5_sample",
    )(u, h0_re, h0_im, bt, ct, abar_re, abar_im, d_skip)


def _split3(x):
    hi = x.astype(BF16)
    r1 = x - hi.astype(F32)
    mid = r1.astype(BF16)
    lo = (r1 - mid.astype(F32)).astype(BF16)
    return hi, mid, lo


def _tri(n, lower):
    r = lax.broadcasted_iota(jnp.int32, (n, n), 0)
    c = lax.broadcasted_iota(jnp.int32, (n, n), 1)
    return (r >= c) if lower else (r <= c)


def _gate_columns(gcol, bias_row):
    n = gcol.shape[0]
    g = gcol + bias_row
    fl = _log_sigmoid(g[:, HEADS:])
    tril = jnp.where(_tri(n, True), 1.0, 0.0).astype(BF16)
    b = sum(jnp.dot(tril, p, preferred_element_type=F32) for p in _split3(fl))
    return g[:, :HEADS], b


def _gate_rows(grow, bias_col):
    n = grow.shape[1]
    g = grow + bias_col
    fl = _log_sigmoid(g[HEADS:, :])
    triu = jnp.where(_tri(n, False), 1.0, 0.0).astype(BF16)
    b = sum(jnp.dot(p, triu, preferred_element_type=F32) for p in _split3(fl))
    return g[:HEADS, :], b


MLSTM_GROUP = 2


def _eye_bf16(n):
    return jnp.where(_tri(n, True) & _tri(n, False), 1.0, 0.0).astype(BF16)


def _mlstm_state_update(k_bf, v, ig_col, b_col, c_old, n_old, m_old, k_t=None, gate_rows=None):
    n_rows = k_bf.shape[0]
    k_scale = DK ** -0.5
    b_last = b_col[n_rows - 1:n_rows, :]
    e_col = b_last - b_col + ig_col
    m_new = jnp.maximum(b_last + m_old, jnp.max(e_col, axis=0, keepdims=True))
    w_end = jnp.exp(e_col - m_new) * k_scale
    carry = jnp.exp(b_last + m_old - m_new)
    if k_t is None:
        k_t = lax.dot_general(_eye_bf16(DK), k_bf, NT_DIMS, preferred_element_type=F32)
    if gate_rows is None:
        kv = jnp.dot(k_t.astype(BF16), (v.astype(F32) * w_end).astype(BF16), preferred_element_type=F32)
    else:
        ig_row, b_row = gate_rows
        w_end_row = jnp.exp(b_last - b_row + ig_row - m_new) * k_scale
        kv = jnp.dot((k_t * w_end_row).astype(BF16), v, preferred_element_type=F32)
    c_new = carry * c_old + kv
    n_new = carry * n_old + jnp.sum(k_bf.astype(F32) * w_end, axis=0, keepdims=True)
    return c_new, n_new, m_new


def _mlstm_decay(ig_row, b_row, b_col, m_old):
    n_rows = b_col.shape[0]
    dmat = jnp.where(_tri(n_rows, True), b_col + (ig_row - b_row), -jnp.inf)
    inter = b_col + m_old
    m_t = jnp.maximum(inter, jnp.max(dmat, axis=-1, keepdims=True))
    return jnp.exp(dmat - m_t), jnp.exp(inter - m_t), m_t


def _head_norm_gate(h, o, gain):
    mu = jnp.mean(h, axis=-1, keepdims=True)
    hc = h - mu
    var = jnp.mean(hc * hc, axis=-1, keepdims=True)
    return _sigmoid(o) * (hc * lax.rsqrt(var + EPS) * gain)


def _mlstm_prompt_kernel(batch, q_ref, k_ref, v_ref, o_ref, gc_ref, *rest):
    gr_refs = rest[:batch]
    km_ref, vm_ref, gcm_ref, brow_ref, bcol_ref, gain_ref, y_ref, cf_ref, nf_ref, mf_ref = rest[batch:]
    k_scale = DK ** -0.5

    @pl.when(pl.program_id(0) == 0)
    def _():
        ig_c, b_c = _gate_columns(gcm_ref[...], brow_ref[...])
        for h in range(HEADS):
            hs = slice(h * DK, (h + 1) * DK)
            c_new, n_new, m_new = _mlstm_state_update(
                km_ref[:, hs], vm_ref[:, hs], ig_c[:, h:h + 1], b_c[:, h:h + 1],
                jnp.zeros((DK, DV), F32), jnp.zeros((1, DK), F32), jnp.zeros((1, 1), F32))
            for b in range(batch):
                cf_ref[b, h] = c_new
                nf_ref[b, h] = n_new
                mf_ref[b, h:h + 1, :] = jnp.broadcast_to(m_new, (1, 128))

    eye = _eye_bf16(DK)
    for b0 in range(0, batch, MLSTM_GROUP):
        gates = {}
        for b in range(b0, b0 + MLSTM_GROUP):
            gates[b] = _gate_columns(gc_ref[b], brow_ref[...]) + _gate_rows(gr_refs[b][...], bcol_ref[...])
        chains = [(b, h) for b in range(b0, b0 + MLSTM_GROUP) for h in range(HEADS)]

        ops = []
        for b, h in chains:
            hs = slice(h * DK, (h + 1) * DK)
            q = q_ref[b, :, hs]
            k_bf = k_ref[b, :, hs]
            c_old = cf_ref[b, h]
            ops.append(dict(
                q=q, k_bf=k_bf, v=v_ref[b, :, hs], c_old=c_old, n_old=nf_ref[b, h], m_old=mf_ref[b, h:h + 1, 0:1],
                s_raw=lax.dot_general(q, k_bf, NT_DIMS, preferred_element_type=F32),
                qc=jnp.dot(q, c_old.astype(BF16), preferred_element_type=F32),
                k_t=lax.dot_general(eye, k_bf, NT_DIMS, preferred_element_type=F32)))

        for (b, h), c in zip(chains, ops):
            ig_c, b_c, ig_r, b_r = gates[b]
            c["w_intra"], c["w_inter"], c["m_t"] = _mlstm_decay(ig_r[h:h + 1, :], b_r[h:h + 1, :], b_c[:, h:h + 1],
                                                                c["m_old"])

        for (b, h), c in zip(chains, ops):
            hs = slice(h * DK, (h + 1) * DK)
            s = c["s_raw"] * (c["w_intra"] * k_scale)
            num = c["w_inter"] * c["qc"] + jnp.dot(s.astype(BF16), c["v"], preferred_element_type=F32)
            den = (c["w_inter"] * jnp.sum(c["q"].astype(F32) * c["n_old"], axis=-1, keepdims=True)
                   + jnp.sum(s, axis=-1, keepdims=True))
            hid = num / jnp.maximum(jnp.abs(den), jnp.exp(-c["m_t"]))
            y_ref[b, :, hs] = _head_norm_gate(hid, o_ref[b, :, hs], gain_ref[:, hs]).astype(y_ref.dtype)

        for (b, h), c in zip(chains, ops):
            ig_c, b_c, ig_r, b_r = gates[b]
            c_new, n_new, m_new = _mlstm_state_update(c["k_bf"], c["v"], ig_c[:, h:h + 1], b_c[:, h:h + 1],
                                                      c["c_old"], c["n_old"], c["m_old"], k_t=c["k_t"],
                                                      gate_rows=(ig_r[h:h + 1, :], b_r[h:h + 1, :]))
            cf_ref[b, h] = c_new
            nf_ref[b, h] = n_new
            mf_ref[b, h:h + 1, :] = jnp.broadcast_to(m_new, (1, 128))


def _mlstm_prompt(qkv, o_arr, o_col_block, gcol, grow, qkv_meta, gcol_meta, bias_row, bias_col, gain, batch, seq):
    n_c = seq // CHUNK
    width = HEADS * DK
    const2 = lambda c: (0, 0)
    qkv3 = qkv.reshape(batch, seq, 3 * width)
    o3 = o_arr.reshape(batch, seq, o_arr.shape[1])
    gcol3 = gcol.reshape(gcol.shape[0] // seq, seq, 2 * HEADS)
    state0 = lambda c: (0, 0, 0, 0)
    return pl.pallas_call(
        functools.partial(_mlstm_prompt_kernel, batch),
        grid=(n_c,),
        in_specs=[pl.BlockSpec((batch, CHUNK, width), lambda c: (0, c, 0)),
                  pl.BlockSpec((batch, CHUNK, width), lambda c: (0, c, 1)),
                  pl.BlockSpec((batch, CHUNK, width), lambda c: (0, c, 2)),
                  pl.BlockSpec((batch, CHUNK, width), lambda c: (0, c, o_col_block)),
                  pl.BlockSpec((batch, CHUNK, 2 * HEADS), lambda c: (0, c, 0))]
                 + [pl.BlockSpec((2 * HEADS, CHUNK), functools.partial(lambda b, c: (0, b * n_c + c), b))
                    for b in range(batch)]
                 + [pl.BlockSpec((N_META, width), lambda c: (0, 1)),
                    pl.BlockSpec((N_META, width), lambda c: (0, 2)),
                    pl.BlockSpec((N_META, 2 * HEADS), const2),
                    pl.BlockSpec((1, 2 * HEADS), const2),
                    pl.BlockSpec((2 * HEADS, 1), const2),
                    pl.BlockSpec((1, width), const2)],
        out_specs=[pl.BlockSpec((batch, CHUNK, width), lambda c: (0, c, 0)),
                   pl.BlockSpec((batch, HEADS, DK, DV), state0),
                   pl.BlockSpec((batch, HEADS, 1, DK), state0),
                   pl.BlockSpec((batch, HEADS, 128), lambda c: (0, 0, 0))],
        out_shape=[jax.ShapeDtypeStruct((batch, seq, width), BF16),
                   jax.ShapeDtypeStruct((batch, HEADS, DK, DV), F32),
                   jax.ShapeDtypeStruct((batch, HEADS, 1, DK), F32),
                   jax.ShapeDtypeStruct((batch, HEADS, 128), F32)],
        compiler_params=_cparams(1, 40),
        name="mlstm_prompt",
    )(qkv3, qkv3, qkv3, o3, gcol3, *([grow] * batch), qkv_meta, qkv_meta, gcol_meta, bias_row, bias_col, gain)


MLSTM_SB = 8


def _mlstm_sample_kernel(qkv_ref, o_ref, g_ref, c_ref, n_ref, m_ref, brow_ref, gain_ref,
                         y_ref, co_ref, no_ref, mo_ref):
    sb = MLSTM_SB
    k_scale = DK ** -0.5
    eye = _eye_bf16(DK)
    width = HEADS * DK
    pad = 2 * sb
    row_id = lax.broadcasted_iota(jnp.int32, (pad, DV), 0)

    g_all = g_ref[...] + brow_ref[...]
    ig_all = g_all[:, :HEADS]
    fl_all = _log_sigmoid(g_all[:, HEADS:])
    m_old_all = m_ref[...]
    m_new_all = jnp.maximum(fl_all + m_old_all, ig_all)
    w_in_all = jnp.exp(ig_all - m_new_all)
    w_ca_all = jnp.exp(fl_all + m_old_all - m_new_all)
    floor_all = jnp.exp(-m_new_all)
    mo_ref[...] = m_new_all

    for h in range(HEADS):
        hs = slice(h * DK, (h + 1) * DK)
        qk_rows = jnp.concatenate([qkv_ref[:, hs], qkv_ref[:, width + h * DK:width + (h + 1) * DK] * k_scale],
                                  axis=0).astype(BF16)
        qk_cols = lax.dot_general(eye, qk_rows, NT_DIMS, preferred_element_type=F32).astype(BF16)
        for i in range(sb):
            w_in = w_in_all[i:i + 1, h:h + 1]
            w_ca = w_ca_all[i:i + 1, h:h + 1]
            q = qk_rows[i:i + 1, :]
            qf = q.astype(F32)
            k = qk_rows[sb + i:sb + i + 1, :].astype(F32)
            v = qkv_ref[i:i + 1, 2 * width + h * DK:2 * width + (h + 1) * DK].astype(BF16).astype(F32)
            c_old = c_ref[i, h]
            n_old = n_ref[i, h]
            qc = jnp.dot(jnp.broadcast_to(q, (pad, DK)), c_old.astype(BF16), preferred_element_type=F32)[0:1, :]
            v_rows = jnp.where(row_id == sb + i, jnp.broadcast_to(v * w_in, (pad, DV)), 0.0).astype(BF16)
            kv = jnp.dot(qk_cols, v_rows, preferred_element_type=F32)
            s = jnp.sum(qf * k, axis=-1, keepdims=True) * w_in
            num = w_ca * qc + s * v
            den = w_ca * jnp.sum(qf * n_old, axis=-1, keepdims=True) + s
            hid = num / jnp.maximum(jnp.abs(den), floor_all[i:i + 1, h:h + 1])
            co_ref[i, h] = w_ca * c_old + kv
            no_ref[i, h] = w_ca * n_old + k * w_in
            y_ref[i:i + 1, hs] = _head_norm_gate(hid, o_ref[i:i + 1, hs], gain_ref[:, hs])


def _mlstm_sample(qkv, o_arr, gcol, c0, n0, m0, bias_row, gain):
    nb = qkv.shape[0]
    width = HEADS * DK
    sb = MLSTM_SB
    const2 = lambda i: (0, 0)
    return pl.pallas_call(
        _mlstm_sample_kernel,
        grid=(nb // sb,),
        in_specs=[pl.BlockSpec((sb, 3 * width), lambda i: (i, 0)),
                  pl.BlockSpec((sb, width), lambda i: (i, 0)),
                  pl.BlockSpec((sb, 2 * HEADS), lambda i: (i, 0)),
                  pl.BlockSpec((sb, HEADS, DK, DV), lambda i: (i, 0, 0, 0)),
                  pl.BlockSpec((sb, HEADS, 1, DK), lambda i: (i, 0, 0, 0)),
                  pl.BlockSpec((sb, HEADS), lambda i: (i, 0)),
                  pl.BlockSpec((1, 2 * HEADS), const2),
                  pl.BlockSpec((1, width), const2)],
        out_specs=[pl.BlockSpec((sb, width), lambda i: (i, 0)),
                   pl.BlockSpec((sb, HEADS, DK, DV), lambda i: (i, 0, 0, 0)),
                   pl.BlockSpec((sb, HEADS, 1, DK), lambda i: (i, 0, 0, 0)),
                   pl.BlockSpec((sb, HEADS), lambda i: (i, 0))],
        out_shape=[jax.ShapeDtypeStruct((nb, width), F32),
                   jax.ShapeDtypeStruct((nb, HEADS, DK, DV), F32),
                   jax.ShapeDtypeStruct((nb, HEADS, 1, DK), F32),
                   jax.ShapeDtypeStruct((nb, HEADS), F32)],
        compiler_params=_cparams(1, 48),
        name="mlstm_sample",
    )(qkv, o_arr, gcol, c0, n0, m0, bias_row, gain)


def _merge_kernel(ya_ref, yb_ref, x_ref, wgab_ref, wglu_ref, bglu_ref, wa_ref, wb_ref, wout_ref,
                  g1_ref, b1_ref, o_ref):
    x = x_ref[...]
    gab = lax.dot_general(x.astype(BF16), wgab_ref[...], NT_DIMS, preferred_element_type=F32)
    g = _gelu_tanh(ya_ref[...])
    z = jnp.dot(g.astype(BF16), wglu_ref[...], preferred_element_type=F32) + bglu_ref[...]
    out_a = g * _sigmoid(z)
    up_a = jnp.dot(out_a.astype(BF16), wa_ref[...], preferred_element_type=F32)
    up_b = jnp.dot(yb_ref[...], wb_ref[...], preferred_element_type=F32)
    mix = _sigmoid(gab[:, :D_MODEL]) * up_a + _sigmoid(gab[:, D_MODEL:]) * up_b
    mo = jnp.dot(mix.astype(BF16), wout_ref[...], preferred_element_type=F32)
    o_ref[...] = _layernorm_rows(ALPHA * x + mo, g1_ref[...], b1_ref[...])


def _merge(ya, yb, x, wt_gab, w_glu, b_glu, w_a_up, w_b_up, w_out, ln_g, ln_b, tm):
    r = x.shape[0]
    const2 = lambda i: (0, 0)
    resident = functools.partial(pl.BlockSpec, index_map=const2, pipeline_mode=pl.Buffered(1))
    return pl.pallas_call(
        _merge_kernel,
        grid=(r // tm,),
        in_specs=[pl.BlockSpec((tm, SSM_WIDTH), lambda i: (i, 0)),
                  pl.BlockSpec((tm, HEADS * DV), lambda i: (i, 0)),
                  pl.BlockSpec((tm, D_MODEL), lambda i: (i, 0)),
                  resident((2 * D_MODEL, D_MODEL)),
                  resident((SSM_WIDTH, SSM_WIDTH)),
                  resident((1, SSM_WIDTH)),
                  resident((SSM_WIDTH, D_MODEL)),
                  resident((HEADS * DV, D_MODEL)),
                  resident((D_MODEL, D_MODEL)),
                  resident((1, D_MODEL)),
                  resident((1, D_MODEL))],
        out_specs=pl.BlockSpec((tm, D_MODEL), lambda i: (i, 0)),
        out_shape=jax.ShapeDtypeStruct((r, D_MODEL), F32),
        compiler_params=_cparams(1, 58),
        name="merge_ln1",
    )(ya, yb, x, wt_gab, w_glu, b_glu, w_a_up, w_b_up, w_out, ln_g, ln_b)


def _ffn_kernel(x_ref, wg_ref, wu_ref, wd_ref, g2_ref, b2_ref, o_ref, xb_s):
    f_id = pl.program_id(1)
    n_f = pl.num_programs(1)

    @pl.when(f_id == 0)
    def _():
        xb_s[...] = x_ref[...].astype(BF16)
        o_ref[...] = jnp.zeros_like(o_ref)

    xb = xb_s[...]
    hg = jnp.dot(xb, wg_ref[...].astype(BF16), preferred_element_type=F32)
    hu = jnp.dot(xb, wu_ref[...].astype(BF16), preferred_element_type=F32)
    act = (hg * _sigmoid(hg)) * hu
    o_ref[...] += jnp.dot(act.astype(BF16), wd_ref[...].astype(BF16), preferred_element_type=F32)

    @pl.when(f_id == n_f - 1)
    def _():
        o_ref[...] = _layernorm_rows(ALPHA * x_ref[...] + o_ref[...], g2_ref[...], b2_ref[...])


def _ffn(x, w_gate3, w_up3, w_down3, ln_g, ln_b, tm, tf):
    r = x.shape[0]
    const2 = lambda i, f: (0, 0)
    return pl.pallas_call(
        _ffn_kernel,
        grid=(r // tm, D_FF // tf),
        in_specs=[pl.BlockSpec((tm, D_MODEL), lambda i, f: (i, 0)),
                  pl.BlockSpec((None, D_MODEL, tf), lambda i, f: (0, 0, f)),
                  pl.BlockSpec((None, D_MODEL, tf), lambda i, f: (0, 0, f)),
                  pl.BlockSpec((None, tf, D_MODEL), lambda i, f: (0, f, 0)),
                  pl.BlockSpec((1, D_MODEL), const2),
                  pl.BlockSpec((1, D_MODEL), const2)],
        out_specs=pl.BlockSpec((tm, D_MODEL), lambda i, f: (i, 0)),
        out_shape=jax.ShapeDtypeStruct((r, D_MODEL), F32),
        scratch_shapes=[pltpu.VMEM((tm, D_MODEL), BF16)],
        compiler_params=_cparams(2, 59),
        name="ffn_ln2",
    )(x, w_gate3, w_up3, w_down3, ln_g, ln_b)


def kernel(x_prompt, x_sample, state_ssm_re, state_ssm_im, state_mlstm_c, state_mlstm_n, state_mlstm_m,
           meta_tokens, w_in, b_if, ssm_a_re, ssm_a_im, ssm_log_dt, ssm_b_re, ssm_b_im, ssm_c_re, ssm_c_im,
           ssm_d, w_glu, b_glu, w_a_up, mh_gain, w_b_up, w_out, ln1_g, ln1_b, w_gate, w_up, w_down,
           ln2_g, ln2_b):
    batch, seq, _ = x_prompt.shape
    nb = x_sample.shape[0]
    width = HEADS * DK

    w_glu_b = w_glu[0].astype(BF16)
    w_a_b = w_a_up[0].astype(BF16)
    w_b_b = w_b_up[0].astype(BF16)
    w_out_b = w_out[0].astype(BF16)
    bias_row = b_if[0].reshape(1, 2 * HEADS)
    bias_col = b_if[0].reshape(2 * HEADS, 1)
    gain = mh_gain[0].reshape(1, width)
    d_skip = ssm_d[0].reshape(1, SSM_WIDTH)
    b_glu_r = b_glu[0].reshape(1, SSM_WIDTH)
    ln1 = (ln1_g[0].reshape(1, D_MODEL), ln1_b[0].reshape(1, D_MODEL))
    ln2 = (ln2_g[0].reshape(1, D_MODEL), ln2_b[0].reshape(1, D_MODEL))
    abar_re, abar_im, bt, ct = _s5_discretise(
        ssm_a_re[0], ssm_a_im[0], ssm_log_dt[0], ssm_b_re[0], ssm_b_im[0], ssm_c_re[0], ssm_c_im[0])

    xp = x_prompt.reshape(batch * seq, D_MODEL)
    xs = x_sample.reshape(nb, D_MODEL)
    n_small = 2 * nb
    xsm = jnp.concatenate([xs, meta_tokens, jnp.zeros((n_small - nb - N_META, D_MODEL), F32)], axis=0)
    w_in_t = jnp.swapaxes(w_in, 1, 2)
    wt_gab = w_in_t[0, TAIL_ROW0 + GATE_ROWS:, :].astype(BF16)
    uo_blocks, qkv_blocks = (0, 4, 2), (1, 1, 3)

    uo_p, gcol_p, grow_p = _proj(xp, w_in_t, uo_blocks, F32, 1024, "proj_uo_prompt", with_gates=True)
    qkv_p = _proj(xp, w_in_t, qkv_blocks, BF16, 1024, "proj_qkv_prompt")

    uo_s, gcol_s, _ = _proj(xsm, w_in_t, uo_blocks, F32, n_small, "proj_uo_small", with_gates=True)
    qkv_s = _proj(xsm, w_in_t, qkv_blocks, F32, n_small, "proj_qkv_small")

    u_meta = uo_s[nb:nb + N_META, :SSM_WIDTH]
    qkv_meta = qkv_s[nb:nb + N_META].astype(BF16)
    gcol_meta = gcol_s[nb:nb + N_META]

    ya_p, pf_re, pf_im = _s5_prompt(uo_p, 0, u_meta, abar_re, abar_im, bt, ct, d_skip, batch, seq)
    yb_p, pc, pn, pm = _mlstm_prompt(qkv_p, uo_p, 1, gcol_p, grow_p, qkv_meta, gcol_meta, bias_row, bias_col,
                                     gain, batch, seq)
    x1_p = _merge(ya_p, yb_p.reshape(batch * seq, width), xp, wt_gab, w_glu_b, b_glu_r, w_a_b, w_b_b, w_out_b, *ln1, tm=256)
    y_p = _ffn(x1_p, w_gate, w_up, w_down, *ln2, tm=1024, tf=256)

    ya_s, sf_re, sf_im = _s5_sample(uo_s[:nb, :SSM_WIDTH], state_ssm_re[0].reshape(nb, SSM_FLAT),
                                    state_ssm_im[0].reshape(nb, SSM_FLAT), abar_re, abar_im, bt, ct, d_skip)
    yb_s, sc, sn, sm = _mlstm_sample(qkv_s[:nb], uo_s[:nb, SSM_WIDTH:], gcol_s[:nb], state_mlstm_c[0],
                                     state_mlstm_n[0].reshape(nb, HEADS, 1, DK), state_mlstm_m[0], bias_row, gain)
    x1_s = _merge(ya_s, yb_s.astype(BF16), xs, wt_gab, w_glu_b, b_glu_r, w_a_b, w_b_b, w_out_b, *ln1, tm=nb)
    y_s = _ffn(x1_s, w_gate, w_up, w_down, *ln2, tm=nb, tf=512)

    return (y_p.reshape(batch, seq, D_MODEL),
            y_s.reshape(nb, 1, D_MODEL),
            pf_re.reshape(1, batch, SSM_GROUPS, SSM_STATE),
            pf_im.reshape(1, batch, SSM_GROUPS, SSM_STATE),
            pc.reshape(1, batch, HEADS, DK, DV),
            pn.reshape(1, batch, HEADS, DK),
            pm[:, :, 0].reshape(1, batch, HEADS),
            sf_re.reshape(1, nb, SSM_GROUPS, SSM_STATE),
            sf_im.reshape(1, nb, SSM_GROUPS, SSM_STATE),
            sc.reshape(1, nb, HEADS, DK, DV),
            sn.reshape(1, nb, HEADS, DK),
            sm.reshape(1, nb, HEADS))
```

```python
import functools

import jax
import jax.numpy as jnp
import numpy as np
from jax import lax
from jax.experimental import pallas as pl
from jax.experimental.pallas import tpu as pltpu

F32 = jnp.float32
BF16 = jnp.bfloat16

D_MODEL = 2048
N_META = 16
SSM_WIDTH = 1024
SSM_GROUP = 16
SSM_GROUPS = 64
SSM_STATE = 64
SSM_FLAT = SSM_GROUPS * SSM_STATE
SSM_JBLK = 4
HEADS = 4
DK = 256
DV = 256
CHUNK = 128
D_FF = 5632
EPS = 1e-5
ALPHA = 2.0 ** 0.25

S5_TC = 256
S5_NSEG = 8
S5_SEG = S5_TC // S5_NSEG
S5_LW = 512

V7X_SCOPED_VMEM_MAX_BYTES = 60000 * 1024


def _cparams(n_axes, vmem_mb):
    return pltpu.CompilerParams(
        dimension_semantics=("arbitrary",) * n_axes,
        vmem_limit_bytes=min(vmem_mb * 1024 * 1024, V7X_SCOPED_VMEM_MAX_BYTES),
    )


def _sigmoid(x):
    return 1.0 / (1.0 + jnp.exp(-x))


def _log_sigmoid(x):
    return jnp.minimum(x, 0.0) - jnp.log(1.0 + jnp.exp(-jnp.abs(x)))


def _gelu_tanh(x):
    c = 0.7978845608028654
    return 0.5 * x * (1.0 + jnp.tanh(c * (x + 0.044715 * (x * x * x))))


def _layernorm_rows(x, g, b):
    mu = jnp.mean(x, axis=-1, keepdims=True)
    xc = x - mu
    var = jnp.mean(xc * xc, axis=-1, keepdims=True)
    return xc * lax.rsqrt(var + EPS) * g + b


PROJ_TN = 1024
GATE_ROWS = 2 * HEADS
TAIL_ROW0 = SSM_WIDTH + 4 * HEADS * DK
NT_DIMS = (((1,), (1,)), ((), ()))


def _proj_kernel(with_gates, x_ref, xs_ref, wt_ref, *rest):
    if with_gates:
        wg_ref, o_ref, os_ref, g_ref, gt_ref, gs_ref, wb_s = rest
    else:
        o_ref, os_ref, wb_s = rest
    i = pl.program_id(1)

    @pl.when(i == 0)
    def _():
        wb_s[...] = wt_ref[...].astype(BF16)

    xb = x_ref[...].astype(BF16)
    o_ref[...] = lax.dot_general(xb, wb_s[...], NT_DIMS, preferred_element_type=F32).astype(o_ref.dtype)
    if with_gates:
        wg = wg_ref[...].astype(BF16)
        g_ref[...] = lax.dot_general(xb, wg, NT_DIMS, preferred_element_type=F32)
        gt_ref[...] = lax.dot_general(wg, xb, NT_DIMS, preferred_element_type=F32)

    @pl.when(i == pl.num_programs(1) - 1)
    def _():
        xsb = xs_ref[...].astype(BF16)
        os_ref[...] = lax.dot_general(xsb, wb_s[...], NT_DIMS, preferred_element_type=F32).astype(os_ref.dtype)
        if with_gates:
            gs_ref[...] = lax.dot_general(xsb, wg_ref[...].astype(BF16), NT_DIMS, preferred_element_type=F32)


def _proj(x, xs, wt3, blocks, out_dtype, small_dtype, tm, name, with_gates=False):
    r, k = x.shape
    rs = xs.shape[0]
    first, step, count = blocks
    tn = PROJ_TN
    nm = r // tm
    in_specs = [pl.BlockSpec((tm, k), lambda j, i: (i, 0)),
                pl.BlockSpec((rs, k), lambda j, i: (0, 0)),
                pl.BlockSpec((None, tn, k), lambda j, i: (0, first + step * j, 0))]
    out_specs = [pl.BlockSpec((tm, tn), lambda j, i: (i, j)),
                 pl.BlockSpec((rs, tn), lambda j, i: (0, j))]
    out_shape = [jax.ShapeDtypeStruct((r, count * tn), out_dtype),
                 jax.ShapeDtypeStruct((rs, count * tn), small_dtype)]
    args = (x, xs, wt3)
    if with_gates:
        ng = GATE_ROWS
        in_specs += [pl.BlockSpec((None, ng, k), lambda j, i: (0, TAIL_ROW0 // ng, 0))]
        out_specs += [pl.BlockSpec((tm, ng), lambda j, i: (j * nm + i, 0)),
                      pl.BlockSpec((ng, tm), lambda j, i: (0, j * nm + i)),
                      pl.BlockSpec((rs, ng), lambda j, i: (j, 0))]
        out_shape += [jax.ShapeDtypeStruct((count * r, ng), F32), jax.ShapeDtypeStruct((ng, count * r), F32),
                      jax.ShapeDtypeStruct((count * rs, ng), F32)]
        args = (x, xs, wt3, wt3)
    return pl.pallas_call(
        functools.partial(_proj_kernel, with_gates),
        grid=(count, nm),
        in_specs=in_specs,
        out_specs=out_specs,
        out_shape=out_shape,
        scratch_shapes=[pltpu.VMEM((tn, k), BF16)],
        compiler_params=_cparams(2, 56),
        name=name,
    )(*args)


def _s5_discretise(a_re, a_im, log_dt, b_re, b_im, c_re, c_im):
    dt = jnp.exp(log_dt)
    e = jnp.exp(a_re * dt)
    abar_re = e * jnp.cos(a_im * dt)
    abar_im = e * jnp.sin(a_im * dt)
    nr = abar_re - 1.0
    ni = abar_im
    den = a_re * a_re + a_im * a_im
    coef_re = (nr * a_re + ni * a_im) / den
    coef_im = (ni * a_re - nr * a_im) / den
    bb_re = coef_re[..., None] * b_re - coef_im[..., None] * b_im
    bb_im = coef_re[..., None] * b_im + coef_im[..., None] * b_re
    gpb = SSM_GROUPS // SSM_JBLK

    def b_rows(bb):
        return bb.reshape(SSM_JBLK, gpb, SSM_STATE, SSM_GROUP).transpose(0, 1, 3, 2).reshape(SSM_JBLK, 256, SSM_STATE)

    def c_rows(cc):
        return cc.reshape(SSM_JBLK, gpb, SSM_GROUP, SSM_STATE).transpose(0, 1, 3, 2).reshape(SSM_JBLK, 1024, SSM_GROUP)

    bt = jnp.concatenate([jnp.tile(b_rows(bb_re), (1, 1, 2)), jnp.tile(b_rows(bb_im), (1, 1, 2))], axis=-1)
    ct = jnp.concatenate([jnp.tile(c_rows(c_re), (1, 1, 8)), jnp.tile(c_rows(-c_im), (1, 1, 8))], axis=-1)
    return abar_re.reshape(1, SSM_FLAT), abar_im.reshape(1, SSM_FLAT), bt, ct


def _s5_expand_weights(bt_ref, ct_ref, bw_s, cwr_s, cwi_s):
    rb = lax.broadcasted_iota(jnp.int32, (256, 2048), 0)
    cb = lax.broadcasted_iota(jnp.int32, (256, 2048), 1)
    bmask = (rb >> 4) == ((cb & 1023) >> 6)
    rc = lax.broadcasted_iota(jnp.int32, (1024, 256), 0)
    cc = lax.broadcasted_iota(jnp.int32, (1024, 256), 1)
    cmask = (rc >> 6) == (cc >> 4)
    for j in range(SSM_JBLK):
        bt = bt_ref[j]
        wide = jnp.concatenate([bt[:, :128]] * 8 + [bt[:, 128:]] * 8, axis=1)
        bw_s[j] = jnp.where(bmask, wide, 0.0).astype(BF16)
        ct = ct_ref[j]
        cwr_s[j] = jnp.where(cmask, jnp.concatenate([ct[:, :128]] * 2, axis=1), 0.0).astype(BF16)
        cwi_s[j] = jnp.where(cmask, jnp.concatenate([ct[:, 128:]] * 2, axis=1), 0.0).astype(BF16)


def _cmul_add(ar, ai, hr, hi, xr, xi):
    return ar * hr - ai * hi + xr, ar * hi + ai * hr + xi


def _s5_project_in(u, b_ref, hre_ref, him_ref, rows):
    for j in range(SSM_JBLK):
        bu = jnp.dot(u[:, j * 256:(j + 1) * 256], b_ref[j], preferred_element_type=F32)
        hre_ref[0:rows, j * 1024:(j + 1) * 1024] = bu[:, :1024]
        him_ref[0:rows, j * 1024:(j + 1) * 1024] = bu[:, 1024:]


def _s5_project_out(hre_ref, him_ref, cre_ref, cimn_ref, rows):
    ys = []
    for j in range(SSM_JBLK):
        hr = hre_ref[0:rows, j * 1024:(j + 1) * 1024].astype(BF16)
        hi = him_ref[0:rows, j * 1024:(j + 1) * 1024].astype(BF16)
        y = jnp.dot(hr, cre_ref[j], preferred_element_type=F32)
        ys.append(y + jnp.dot(hi, cimn_ref[j], preferred_element_type=F32))
    return jnp.concatenate(ys, axis=1)


def _s5_prompt_kernel(u_ref, um_ref, perm_ref, permt_ref, bt_ref, ct_ref, are_ref, aim_ref, d_ref,
                      y_ref, fre_ref, fim_ref,
                      hre_ref, him_ref, pre_ref, pim_ref, cre_s, cim_s, hsre_ref, hsim_ref,
                      b_ref, cre_ref, cimn_ref):
    b_id = pl.program_id(0)
    c_id = pl.program_id(1)
    n_c = pl.num_programs(1)

    @pl.when(jnp.logical_and(b_id == 0, c_id == 0))
    def _():
        _s5_expand_weights(bt_ref, ct_ref, b_ref, cre_ref, cimn_ref)
        ar = jnp.broadcast_to(are_ref[...], (S5_NSEG, SSM_FLAT))
        ai = jnp.broadcast_to(aim_ref[...], (S5_NSEG, SSM_FLAT))
        pre_ref[0:S5_NSEG, :] = ar
        pim_ref[0:S5_NSEG, :] = ai

        def body(i, carry):
            pr, pi = carry
            nr = ar * pr - ai * pi
            ni = ar * pi + ai * pr
            rows = pl.ds(pl.multiple_of(i * S5_NSEG, S5_NSEG), S5_NSEG)
            pre_ref[rows, :] = nr
            pim_ref[rows, :] = ni
            return nr, ni

        lax.fori_loop(1, S5_SEG, body, (ar, ai))

    @pl.when(c_id == 0)
    def _():
        _s5_project_in(um_ref[...].astype(BF16), b_ref, hre_ref, him_ref, N_META)
        ar = are_ref[...]
        ai = aim_ref[...]
        hr = jnp.zeros((1, SSM_FLAT), F32)
        hi = jnp.zeros((1, SSM_FLAT), F32)
        for t in range(N_META):
            hr, hi = _cmul_add(ar, ai, hr, hi, hre_ref[t:t + 1, :], him_ref[t:t + 1, :])
        cre_s[...] = hr
        cim_s[...] = hi

    u = u_ref[...]
    u_perm = jnp.dot(perm_ref[...], u.astype(BF16), preferred_element_type=F32).astype(BF16)
    _s5_project_in(u_perm, b_ref, hre_ref, him_ref, S5_TC)

    full = (S5_NSEG, S5_LW)
    for lg in range(SSM_FLAT // S5_LW):
        lanes = slice(lg * S5_LW, (lg + 1) * S5_LW)
        ar = jnp.broadcast_to(are_ref[:, lanes], full)
        ai = jnp.broadcast_to(aim_ref[:, lanes], full)

        def local_step(k, carry):
            rows = pl.ds(pl.multiple_of(k * S5_NSEG, S5_NSEG), S5_NSEG)
            nr, ni = _cmul_add(ar, ai, carry[0], carry[1], hre_ref[rows, lanes], him_ref[rows, lanes])
            hre_ref[rows, lanes] = nr
            him_ref[rows, lanes] = ni
            return nr, ni

        er, ei = lax.fori_loop(0, S5_SEG, local_step, (jnp.zeros(full, F32), jnp.zeros(full, F32)), unroll=4)

        asr = pre_ref[S5_TC - 1:S5_TC, lanes]
        asi = pim_ref[S5_TC - 1:S5_TC, lanes]
        hr = cre_s[:, lanes]
        hi = cim_s[:, lanes]
        for j in range(S5_NSEG):
            hsre_ref[j:j + 1, lanes] = hr
            hsim_ref[j:j + 1, lanes] = hi
            hr, hi = _cmul_add(asr, asi, hr, hi, er[j:j + 1, :], ei[j:j + 1, :])
        cre_s[:, lanes] = hr
        cim_s[:, lanes] = hi

        sr = hsre_ref[:, lanes]
        si = hsim_ref[:, lanes]

        def fix_step(k, carry):
            rows = pl.ds(pl.multiple_of(k * S5_NSEG, S5_NSEG), S5_NSEG)
            pr = pre_ref[rows, lanes]
            pi = pim_ref[rows, lanes]
            hre_ref[rows, lanes] = hre_ref[rows, lanes] + (pr * sr - pi * si)
            him_ref[rows, lanes] = him_ref[rows, lanes] + (pr * si + pi * sr)
            return carry

        lax.fori_loop(0, S5_SEG, fix_step, 0, unroll=4)

    y_perm = _s5_project_out(hre_ref, him_ref, cre_ref, cimn_ref, S5_TC)
    permt = permt_ref[...]
    y = sum(jnp.dot(permt, piece, preferred_element_type=F32) for piece in _split3(y_perm))
    y_ref[...] = y + d_ref[...] * u

    @pl.when(c_id == n_c - 1)
    def _():
        fre_ref[0] = cre_s[...]
        fim_ref[0] = cim_s[...]


def _s5_weight_scratch():
    return [pltpu.VMEM((SSM_JBLK, 256, 2048), BF16), pltpu.VMEM((SSM_JBLK, 1024, 256), BF16),
            pltpu.VMEM((SSM_JBLK, 1024, 256), BF16)]


def _s5_prompt(u_arr, u_col_block, u_meta, abar_re, abar_im, bt, ct, d_skip, batch, seq):
    n_c = seq // S5_TC
    const3 = lambda b, c: (0, 0, 0)
    const2 = lambda b, c: (0, 0)
    rows = np.arange(S5_TC)
    perm_np = np.zeros((S5_TC, S5_TC), np.float32)
    perm_np[rows, (rows % S5_NSEG) * S5_SEG + rows // S5_NSEG] = 1.0
    perm = jnp.asarray(perm_np, dtype=BF16)
    permt = jnp.asarray(perm_np.T, dtype=BF16)
    return pl.pallas_call(
        _s5_prompt_kernel,
        grid=(batch, n_c),
        in_specs=[pl.BlockSpec((S5_TC, SSM_WIDTH), lambda b, c: (b * n_c + c, u_col_block)),
                  pl.BlockSpec((N_META, SSM_WIDTH), const2),
                  pl.BlockSpec((S5_TC, S5_TC), const2),
                  pl.BlockSpec((S5_TC, S5_TC), const2),
                  pl.BlockSpec((SSM_JBLK, 256, 256), const3),
                  pl.BlockSpec((SSM_JBLK, 1024, 256), const3),
                  pl.BlockSpec((1, SSM_FLAT), const2),
                  pl.BlockSpec((1, SSM_FLAT), const2),
                  pl.BlockSpec((1, SSM_WIDTH), const2)],
        out_specs=[pl.BlockSpec((S5_TC, SSM_WIDTH), lambda b, c: (b * n_c + c, 0)),
                   pl.BlockSpec((1, 1, SSM_FLAT), lambda b, c: (b, 0, 0)),
                   pl.BlockSpec((1, 1, SSM_FLAT), lambda b, c: (b, 0, 0))],
        out_shape=[jax.ShapeDtypeStruct((batch * seq, SSM_WIDTH), F32),
                   jax.ShapeDtypeStruct((batch, 1, SSM_FLAT), F32),
                   jax.ShapeDtypeStruct((batch, 1, SSM_FLAT), F32)],
        scratch_shapes=[pltpu.VMEM((S5_TC, SSM_FLAT), F32), pltpu.VMEM((S5_TC, SSM_FLAT), F32),
                        pltpu.VMEM((S5_TC, SSM_FLAT), F32), pltpu.VMEM((S5_TC, SSM_FLAT), F32),
                        pltpu.VMEM((1, SSM_FLAT), F32), pltpu.VMEM((1, SSM_FLAT), F32),
                        pltpu.VMEM((S5_NSEG, SSM_FLAT), F32), pltpu.VMEM((S5_NSEG, SSM_FLAT), F32)]
                       + _s5_weight_scratch(),
        compiler_params=_cparams(2, 48),
        name="s5_prompt",
    )(u_arr, u_meta, perm, permt, bt, ct, abar_re, abar_im, d_skip)


def _s5_sample_kernel(u_ref, h0re_ref, h0im_ref, bt_ref, ct_ref, are_ref, aim_ref, d_ref,
                      y_ref, hre_ref, him_ref, b_ref, cre_ref, cimn_ref):
    rows = u_ref.shape[0]
    _s5_expand_weights(bt_ref, ct_ref, b_ref, cre_ref, cimn_ref)
    u = u_ref[...]
    _s5_project_in(u.astype(BF16), b_ref, hre_ref, him_ref, rows)
    nr, ni = _cmul_add(are_ref[...], aim_ref[...], h0re_ref[...], h0im_ref[...], hre_ref[...], him_ref[...])
    hre_ref[...] = nr
    him_ref[...] = ni
    y_ref[...] = _s5_project_out(hre_ref, him_ref, cre_ref, cimn_ref, rows) + d_ref[...] * u


def _s5_sample(u, h0_re, h0_im, abar_re, abar_im, bt, ct, d_skip):
    rows = u.shape[0]
    return pl.pallas_call(
        _s5_sample_kernel,
        out_shape=[jax.ShapeDtypeStruct((rows, SSM_WIDTH), F32),
                   jax.ShapeDtypeStruct((rows, SSM_FLAT), F32),
                   jax.ShapeDtypeStruct((rows, SSM_FLAT), F32)],
        scratch_shapes=_s5_weight_scratch(),
        compiler_params=_cparams(0, 48),
        name="s5_sample",
    )(u, h0_re, h0_im, bt, ct, abar_re, abar_im, d_skip)


def _split3(x):
    hi = x.astype(BF16)
    r1 = x - hi.astype(F32)
    mid = r1.astype(BF16)
    lo = (r1 - mid.astype(F32)).astype(BF16)
    return hi, mid, lo


def _tri(n, lower):
    r = lax.broadcasted_iota(jnp.int32, (n, n), 0)
    c = lax.broadcasted_iota(jnp.int32, (n, n), 1)
    return (r >= c) if lower else (r <= c)


def _gate_columns(gcol, bias_row):
    n = gcol.shape[0]
    g = gcol + bias_row
    fl = _log_sigmoid(g[:, HEADS:])
    tril = jnp.where(_tri(n, True), 1.0, 0.0).astype(BF16)
    b = sum(jnp.dot(tril, p, preferred_element_type=F32) for p in _split3(fl))
    return g[:, :HEADS], b


def _gate_rows(grow, bias_col):
    n = grow.shape[1]
    g = grow + bias_col
    fl = _log_sigmoid(g[HEADS:, :])
    triu = jnp.where(_tri(n, False), 1.0, 0.0).astype(BF16)
    b = sum(jnp.dot(p, triu, preferred_element_type=F32) for p in _split3(fl))
    return g[:HEADS, :], b


MLSTM_GROUP = 2


def _eye_bf16(n):
    return jnp.where(_tri(n, True) & _tri(n, False), 1.0, 0.0).astype(BF16)


def _mlstm_state_update(k_bf, v, ig_col, b_col, c_old, n_old, m_old, k_t=None, gate_rows=None):
    n_rows = k_bf.shape[0]
    k_scale = DK ** -0.5
    b_last = b_col[n_rows - 1:n_rows, :]
    e_col = b_last - b_col + ig_col
    m_new = jnp.maximum(b_last + m_old, jnp.max(e_col, axis=0, keepdims=True))
    w_end = jnp.exp(e_col - m_new) * k_scale
    carry = jnp.exp(b_last + m_old - m_new)
    if k_t is None:
        k_t = lax.dot_general(_eye_bf16(DK), k_bf, NT_DIMS, preferred_element_type=F32)
    if gate_rows is None:
        kv = jnp.dot(k_t.astype(BF16), (v.astype(F32) * w_end).astype(BF16), preferred_element_type=F32)
    else:
        ig_row, b_row = gate_rows
        w_end_row = jnp.exp(b_last - b_row + ig_row - m_new) * k_scale
        kv = jnp.dot((k_t * w_end_row).astype(BF16), v, preferred_element_type=F32)
    c_new = carry * c_old + kv
    n_new = carry * n_old + jnp.sum(k_bf.astype(F32) * w_end, axis=0, keepdims=True)
    return c_new, n_new, m_new


def _mlstm_decay(ig_row, b_row, b_col, m_old):
    n_rows = b_col.shape[0]
    dmat = jnp.where(_tri(n_rows, True), b_col + (ig_row - b_row), -jnp.inf)
    inter = b_col + m_old
    m_t = jnp.maximum(inter, jnp.max(dmat, axis=-1, keepdims=True))
    return jnp.exp(dmat - m_t), jnp.exp(inter - m_t), m_t


def _head_norm_gate(h, o, gain):
    mu = jnp.mean(h, axis=-1, keepdims=True)
    hc = h - mu
    var = jnp.mean(hc * hc, axis=-1, keepdims=True)
    return _sigmoid(o) * (hc * lax.rsqrt(var + EPS) * gain)


def _mlstm_prompt_kernel(batch, q_ref, k_ref, v_ref, o_ref, gc_ref, *rest):
    gr_refs = rest[:batch]
    km_ref, vm_ref, gcm_ref, brow_ref, bcol_ref, gain_ref, y_ref, cf_ref, nf_ref, mf_ref = rest[batch:]
    k_scale = DK ** -0.5

    @pl.when(pl.program_id(0) == 0)
    def _():
        ig_c, b_c = _gate_columns(gcm_ref[...], brow_ref[...])
        for h in range(HEADS):
            hs = slice(h * DK, (h + 1) * DK)
            c_new, n_new, m_new = _mlstm_state_update(
                km_ref[:, hs], vm_ref[:, hs], ig_c[:, h:h + 1], b_c[:, h:h + 1],
                jnp.zeros((DK, DV), F32), jnp.zeros((1, DK), F32), jnp.zeros((1, 1), F32))
            for b in range(batch):
                cf_ref[b, h] = c_new
                nf_ref[b, h] = n_new
                mf_ref[b, h:h + 1, :] = jnp.broadcast_to(m_new, (1, 128))

    eye = _eye_bf16(DK)
    for b0 in range(0, batch, MLSTM_GROUP):
        gates = {}
        for b in range(b0, b0 + MLSTM_GROUP):
            gates[b] = _gate_columns(gc_ref[b], brow_ref[...]) + _gate_rows(gr_refs[b][...], bcol_ref[...])
        chains = [(b, h) for b in range(b0, b0 + MLSTM_GROUP) for h in range(HEADS)]

        ops = []
        for b, h in chains:
            hs = slice(h * DK, (h + 1) * DK)
            q = q_ref[b, :, hs]
            k_bf = k_ref[b, :, hs]
            c_old = cf_ref[b, h]
            ops.append(dict(
                q=q, k_bf=k_bf, v=v_ref[b, :, hs], c_old=c_old, n_old=nf_ref[b, h], m_old=mf_ref[b, h:h + 1, 0:1],
                s_raw=lax.dot_general(q, k_bf, NT_DIMS, preferred_element_type=F32),
                qc=jnp.dot(q, c_old.astype(BF16), preferred_element_type=F32),
                k_t=lax.dot_general(eye, k_bf, NT_DIMS, preferred_element_type=F32)))

        for (b, h), c in zip(chains, ops):
            ig_c, b_c, ig_r, b_r = gates[b]
            c["w_intra"], c["w_inter"], c["m_t"] = _mlstm_decay(ig_r[h:h + 1, :], b_r[h:h + 1, :], b_c[:, h:h + 1],
                                                                c["m_old"])

        for (b, h), c in zip(chains, ops):
            hs = slice(h * DK, (h + 1) * DK)
            s = c["s_raw"] * (c["w_intra"] * k_scale)
            num = c["w_inter"] * c["qc"] + jnp.dot(s.astype(BF16), c["v"], preferred_element_type=F32)
            den = (c["w_inter"] * jnp.sum(c["q"].astype(F32) * c["n_old"], axis=-1, keepdims=True)
                   + jnp.sum(s, axis=-1, keepdims=True))
            hid = num / jnp.maximum(jnp.abs(den), jnp.exp(-c["m_t"]))
            y_ref[b, :, hs] = _head_norm_gate(hid, o_ref[b, :, hs], gain_ref[:, hs]).astype(y_ref.dtype)

        for (b, h), c in zip(chains, ops):
            ig_c, b_c, ig_r, b_r = gates[b]
            c_new, n_new, m_new = _mlstm_state_update(c["k_bf"], c["v"], ig_c[:, h:h + 1], b_c[:, h:h + 1],
                                                      c["c_old"], c["n_old"], c["m_old"], k_t=c["k_t"],
                                                      gate_rows=(ig_r[h:h + 1, :], b_r[h:h + 1, :]))
            cf_ref[b, h] = c_new
            nf_ref[b, h] = n_new
            mf_ref[b, h:h + 1, :] = jnp.broadcast_to(m_new, (1, 128))


def _mlstm_prompt(qkv, o_arr, o_col_block, gcol, grow, qkv_meta, gcol_meta, bias_row, bias_col, gain, batch, seq):
    n_c = seq // CHUNK
    width = HEADS * DK
    const2 = lambda c: (0, 0)
    qkv3 = qkv.reshape(batch, seq, 3 * width)
    o3 = o_arr.reshape(batch, seq, o_arr.shape[1])
    gcol3 = gcol.reshape(gcol.shape[0] // seq, seq, 2 * HEADS)
    state0 = lambda c: (0, 0, 0, 0)
    return pl.pallas_call(
        functools.partial(_mlstm_prompt_kernel, batch),
        grid=(n_c,),
        in_specs=[pl.BlockSpec((batch, CHUNK, width), lambda c: (0, c, 0)),
                  pl.BlockSpec((batch, CHUNK, width), lambda c: (0, c, 1)),
                  pl.BlockSpec((batch, CHUNK, width), lambda c: (0, c, 2)),
                  pl.BlockSpec((batch, CHUNK, width), lambda c: (0, c, o_col_block)),
                  pl.BlockSpec((batch, CHUNK, 2 * HEADS), lambda c: (0, c, 0))]
                 + [pl.BlockSpec((2 * HEADS, CHUNK), functools.partial(lambda b, c: (0, b * n_c + c), b))
                    for b in range(batch)]
                 + [pl.BlockSpec((N_META, width), lambda c: (0, 1)),
                    pl.BlockSpec((N_META, width), lambda c: (0, 2)),
                    pl.BlockSpec((N_META, 2 * HEADS), const2),
                    pl.BlockSpec((1, 2 * HEADS), const2),
                    pl.BlockSpec((2 * HEADS, 1), const2),
                    pl.BlockSpec((1, width), const2)],
        out_specs=[pl.BlockSpec((batch, CHUNK, width), lambda c: (0, c, 0)),
                   pl.BlockSpec((batch, HEADS, DK, DV), state0),
                   pl.BlockSpec((batch, HEADS, 1, DK), state0),
                   pl.BlockSpec((batch, HEADS, 128), lambda c: (0, 0, 0))],
        out_shape=[jax.ShapeDtypeStruct((batch, seq, width), BF16),
                   jax.ShapeDtypeStruct((batch, HEADS, DK, DV), F32),
                   jax.ShapeDtypeStruct((batch, HEADS, 1, DK), F32),
                   jax.ShapeDtypeStruct((batch, HEADS, 128), F32)],
        compiler_params=_cparams(1, 40),
        name="mlstm_prompt",
    )(qkv3, qkv3, qkv3, o3, gcol3, *([grow] * batch), qkv_meta, qkv_meta, gcol_meta, bias_row, bias_col, gain)


MLSTM_SB = 8


def _mlstm_sample_kernel(qkv_ref, o_ref, g_ref, c_ref, n_ref, m_ref, brow_ref, gain_ref,
                         y_ref, co_ref, no_ref, mo_ref):
    sb = MLSTM_SB
    k_scale = DK ** -0.5
    eye = _eye_bf16(DK)
    width = HEADS * DK
    pad = 2 * sb
    row_id = lax.broadcasted_iota(jnp.int32, (pad, DV), 0)

    g_all = g_ref[...] + brow_ref[...]
    ig_all = g_all[:, :HEADS]
    fl_all = _log_sigmoid(g_all[:, HEADS:])
    m_old_all = m_ref[...]
    m_new_all = jnp.maximum(fl_all + m_old_all, ig_all)
    w_in_all = jnp.exp(ig_all - m_new_all)
    w_ca_all = jnp.exp(fl_all + m_old_all - m_new_all)
    floor_all = jnp.exp(-m_new_all)
    mo_ref[...] = m_new_all

    for h in range(HEADS):
        hs = slice(h * DK, (h + 1) * DK)
        qk_rows = jnp.concatenate([qkv_ref[:, hs], qkv_ref[:, width + h * DK:width + (h + 1) * DK] * k_scale],
                                  axis=0).astype(BF16)
        qk_cols = lax.dot_general(eye, qk_rows, NT_DIMS, preferred_element_type=F32).astype(BF16)
        for i in range(sb):
            w_in = w_in_all[i:i + 1, h:h + 1]
            w_ca = w_ca_all[i:i + 1, h:h + 1]
            q = qk_rows[i:i + 1, :]
            qf = q.astype(F32)
            k = qk_rows[sb + i:sb + i + 1, :].astype(F32)
            v = qkv_ref[i:i + 1, 2 * width + h * DK:2 * width + (h + 1) * DK].astype(BF16).astype(F32)
            c_old = c_ref[i, h]
            n_old = n_ref[i, h]
            qc = jnp.dot(jnp.broadcast_to(q, (pad, DK)), c_old.astype(BF16), preferred_element_type=F32)[0:1, :]
            v_rows = jnp.where(row_id == sb + i, jnp.broadcast_to(v * w_in, (pad, DV)), 0.0).astype(BF16)
            kv = jnp.dot(qk_cols, v_rows, preferred_element_type=F32)
            s = jnp.sum(qf * k, axis=-1, keepdims=True) * w_in
            num = w_ca * qc + s * v
            den = w_ca * jnp.sum(qf * n_old, axis=-1, keepdims=True) + s
            hid = num / jnp.maximum(jnp.abs(den), floor_all[i:i + 1, h:h + 1])
            co_ref[i, h] = w_ca * c_old + kv
            no_ref[i, h] = w_ca * n_old + k * w_in
            y_ref[i:i + 1, hs] = _head_norm_gate(hid, o_ref[i:i + 1, hs], gain_ref[:, hs])


def _mlstm_sample(qkv, o_arr, gcol, c0, n0, m0, bias_row, gain):
    nb = qkv.shape[0]
    width = HEADS * DK
    sb = MLSTM_SB
    const2 = lambda i: (0, 0)
    return pl.pallas_call(
        _mlstm_sample_kernel,
        grid=(nb // sb,),
        in_specs=[pl.BlockSpec((sb, 3 * width), lambda i: (i, 0)),
                  pl.BlockSpec((sb, width), lambda i: (i, 0)),
                  pl.BlockSpec((sb, 2 * HEADS), lambda i: (i, 0)),
                  pl.BlockSpec((sb, HEADS, DK, DV), lambda i: (i, 0, 0, 0)),
                  pl.BlockSpec((sb, HEADS, 1, DK), lambda i: (i, 0, 0, 0)),
                  pl.BlockSpec((sb, HEADS), lambda i: (i, 0)),
                  pl.BlockSpec((1, 2 * HEADS), const2),
                  pl.BlockSpec((1, width), const2)],
        out_specs=[pl.BlockSpec((sb, width), lambda i: (i, 0)),
                   pl.BlockSpec((sb, HEADS, DK, DV), lambda i: (i, 0, 0, 0)),
                   pl.BlockSpec((sb, HEADS, 1, DK), lambda i: (i, 0, 0, 0)),
                   pl.BlockSpec((sb, HEADS), lambda i: (i, 0))],
        out_shape=[jax.ShapeDtypeStruct((nb, width), F32),
                   jax.ShapeDtypeStruct((nb, HEADS, DK, DV), F32),
                   jax.ShapeDtypeStruct((nb, HEADS, 1, DK), F32),
                   jax.ShapeDtypeStruct((nb, HEADS), F32)],
        compiler_params=_cparams(1, 48),
        name="mlstm_sample",
    )(qkv, o_arr, gcol, c0, n0, m0, bias_row, gain)


def _merge_rows(ya, yb, x, wgab_ref, wglu_ref, bglu_ref, wa_ref, wb_ref, wout_ref, g1_ref, b1_ref):
    gab = lax.dot_general(x.astype(BF16), wgab_ref[...], NT_DIMS, preferred_element_type=F32)
    g = _gelu_tanh(ya)
    z = jnp.dot(g.astype(BF16), wglu_ref[...], preferred_element_type=F32) + bglu_ref[...]
    out_a = g * _sigmoid(z)
    up_a = jnp.dot(out_a.astype(BF16), wa_ref[...], preferred_element_type=F32)
    up_b = jnp.dot(yb, wb_ref[...], preferred_element_type=F32)
    mix = _sigmoid(gab[:, :D_MODEL]) * up_a + _sigmoid(gab[:, D_MODEL:]) * up_b
    mo = jnp.dot(mix.astype(BF16), wout_ref[...], preferred_element_type=F32)
    return _layernorm_rows(ALPHA * x + mo, g1_ref[...], b1_ref[...])


def _merge_kernel(ya_ref, yb_ref, x_ref, yas_ref, ybs_ref, xs_ref, *rest):
    weights, (o_ref, os_ref) = rest[:-2], rest[-2:]
    o_ref[...] = _merge_rows(ya_ref[...], yb_ref[...], x_ref[...], *weights)

    @pl.when(pl.program_id(0) == pl.num_programs(0) - 1)
    def _():
        os_ref[...] = _merge_rows(yas_ref[...], ybs_ref[...].astype(BF16), xs_ref[...], *weights)


def _merge(ya, yb, x, ya_s, yb_s, x_s, wt_gab, w_glu, b_glu, w_a_up, w_b_up, w_out, ln_g, ln_b, tm):
    r = x.shape[0]
    rs = x_s.shape[0]
    const2 = lambda i: (0, 0)
    resident = functools.partial(pl.BlockSpec, index_map=const2, pipeline_mode=pl.Buffered(1))
    return pl.pallas_call(
        _merge_kernel,
        grid=(r // tm,),
        in_specs=[pl.BlockSpec((tm, SSM_WIDTH), lambda i: (i, 0)),
                  pl.BlockSpec((tm, HEADS * DV), lambda i: (i, 0)),
                  pl.BlockSpec((tm, D_MODEL), lambda i: (i, 0)),
                  resident((rs, SSM_WIDTH)),
                  resident((rs, HEADS * DV)),
                  resident((rs, D_MODEL)),
                  resident((2 * D_MODEL, D_MODEL)),
                  resident((SSM_WIDTH, SSM_WIDTH)),
                  resident((1, SSM_WIDTH)),
                  resident((SSM_WIDTH, D_MODEL)),
                  resident((HEADS * DV, D_MODEL)),
                  resident((D_MODEL, D_MODEL)),
                  resident((1, D_MODEL)),
                  resident((1, D_MODEL))],
        out_specs=[pl.BlockSpec((tm, D_MODEL), lambda i: (i, 0)),
                   pl.BlockSpec((rs, D_MODEL), const2)],
        out_shape=[jax.ShapeDtypeStruct((r, D_MODEL), F32), jax.ShapeDtypeStruct((rs, D_MODEL), F32)],
        compiler_params=_cparams(1, 58),
        name="merge_ln1",
    )(ya, yb, x, ya_s, yb_s, x_s, wt_gab, w_glu, b_glu, w_a_up, w_b_up, w_out, ln_g, ln_b)


def _ffn_kernel(x_ref, wg_ref, wu_ref, wd_ref, g2_ref, b2_ref, o_ref, xb_s):
    f_id = pl.program_id(1)
    n_f = pl.num_programs(1)

    @pl.when(f_id == 0)
    def _():
        xb_s[...] = x_ref[...].astype(BF16)
        o_ref[...] = jnp.zeros_like(o_ref)

    xb = xb_s[...]
    hg = jnp.dot(xb, wg_ref[...].astype(BF16), preferred_element_type=F32)
    hu = jnp.dot(xb, wu_ref[...].astype(BF16), preferred_element_type=F32)
    act = (hg * _sigmoid(hg)) * hu
    o_ref[...] += jnp.dot(act.astype(BF16), wd_ref[...].astype(BF16), preferred_element_type=F32)

    @pl.when(f_id == n_f - 1)
    def _():
        o_ref[...] = _layernorm_rows(ALPHA * x_ref[...] + o_ref[...], g2_ref[...], b2_ref[...])


def _ffn(x, w_gate3, w_up3, w_down3, ln_g, ln_b, tm, tf):
    r = x.shape[0]
    const2 = lambda i, f: (0, 0)
    return pl.pallas_call(
        _ffn_kernel,
        grid=(r // tm, D_FF // tf),
        in_specs=[pl.BlockSpec((tm, D_MODEL), lambda i, f: (i, 0)),
                  pl.BlockSpec((None, D_MODEL, tf), lambda i, f: (0, 0, f)),
                  pl.BlockSpec((None, D_MODEL, tf), lambda i, f: (0, 0, f)),
                  pl.BlockSpec((None, tf, D_MODEL), lambda i, f: (0, f, 0)),
                  pl.BlockSpec((1, D_MODEL), const2),
                  pl.BlockSpec((1, D_MODEL), const2)],
        out_specs=pl.BlockSpec((tm, D_MODEL), lambda i, f: (i, 0)),
        out_shape=jax.ShapeDtypeStruct((r, D_MODEL), F32),
        scratch_shapes=[pltpu.VMEM((tm, D_MODEL), BF16)],
        compiler_params=_cparams(2, 59),
        name="ffn_ln2",
    )(x, w_gate3, w_up3, w_down3, ln_g, ln_b)


def kernel(x_prompt, x_sample, state_ssm_re, state_ssm_im, state_mlstm_c, state_mlstm_n, state_mlstm_m,
           meta_tokens, w_in, b_if, ssm_a_re, ssm_a_im, ssm_log_dt, ssm_b_re, ssm_b_im, ssm_c_re, ssm_c_im,
           ssm_d, w_glu, b_glu, w_a_up, mh_gain, w_b_up, w_out, ln1_g, ln1_b, w_gate, w_up, w_down,
           ln2_g, ln2_b):
    batch, seq, _ = x_prompt.shape
    nb = x_sample.shape[0]
    width = HEADS * DK

    w_glu_b = w_glu[0].astype(BF16)
    w_a_b = w_a_up[0].astype(BF16)
    w_b_b = w_b_up[0].astype(BF16)
    w_out_b = w_out[0].astype(BF16)
    bias_row = b_if[0].reshape(1, 2 * HEADS)
    bias_col = b_if[0].reshape(2 * HEADS, 1)
    gain = mh_gain[0].reshape(1, width)
    d_skip = ssm_d[0].reshape(1, SSM_WIDTH)
    b_glu_r = b_glu[0].reshape(1, SSM_WIDTH)
    ln1 = (ln1_g[0].reshape(1, D_MODEL), ln1_b[0].reshape(1, D_MODEL))
    ln2 = (ln2_g[0].reshape(1, D_MODEL), ln2_b[0].reshape(1, D_MODEL))
    abar_re, abar_im, bt, ct = _s5_discretise(
        ssm_a_re[0], ssm_a_im[0], ssm_log_dt[0], ssm_b_re[0], ssm_b_im[0], ssm_c_re[0], ssm_c_im[0])

    xp = x_prompt.reshape(batch * seq, D_MODEL)
    xs = x_sample.reshape(nb, D_MODEL)
    xsm = jnp.concatenate([xs, meta_tokens], axis=0)
    w_in_t = jnp.swapaxes(w_in, 1, 2)
    wt_gab = w_in_t[0, TAIL_ROW0 + GATE_ROWS:, :].astype(BF16)
    uo_blocks, qkv_blocks = (0, 4, 2), (1, 1, 3)

    uo_p, uo_s, gcol_p, grow_p, gcol_s = _proj(xp, xsm, w_in_t, uo_blocks, F32, F32, 1024, "proj_uo", with_gates=True)
    qkv_p, qkv_s = _proj(xp, xsm, w_in_t, qkv_blocks, BF16, F32, 1024, "proj_qkv")

    u_meta = uo_s[nb:nb + N_META, :SSM_WIDTH]
    qkv_meta = qkv_s[nb:nb + N_META].astype(BF16)
    gcol_meta = gcol_s[nb:nb + N_META]

    ya_p, pf_re, pf_im = _s5_prompt(uo_p, 0, u_meta, abar_re, abar_im, bt, ct, d_skip, batch, seq)
    yb_p, pc, pn, pm = _mlstm_prompt(qkv_p, uo_p, 1, gcol_p, grow_p, qkv_meta, gcol_meta, bias_row, bias_col,
                                     gain, batch, seq)
    ya_s, sf_re, sf_im = _s5_sample(uo_s[:nb, :SSM_WIDTH], state_ssm_re[0].reshape(nb, SSM_FLAT),
                                    state_ssm_im[0].reshape(nb, SSM_FLAT), abar_re, abar_im, bt, ct, d_skip)
    yb_s, sc, sn, sm = _mlstm_sample(qkv_s[:nb], uo_s[:nb, SSM_WIDTH:], gcol_s[:nb], state_mlstm_c[0],
                                     state_mlstm_n[0].reshape(nb, HEADS, 1, DK), state_mlstm_m[0], bias_row, gain)

    x1_p, x1_s = _merge(ya_p, yb_p.reshape(batch * seq, width), xp, ya_s, yb_s, xs,
                        wt_gab, w_glu_b, b_glu_r, w_a_b, w_b_b, w_out_b, *ln1, tm=256)
    y_p = _ffn(x1_p, w_gate, w_up, w_down, *ln2, tm=1024, tf=256)
    y_s = _ffn(x1_s, w_gate, w_up, w_down, *ln2, tm=nb, tf=512)

    return (y_p.reshape(batch, seq, D_MODEL),
            y_s.reshape(nb, 1, D_MODEL),
            pf_re.reshape(1, batch, SSM_GROUPS, SSM_STATE),
            pf_im.reshape(1, batch, SSM_GROUPS, SSM_STATE),
            pc.reshape(1, batch, HEADS, DK, DV),
            pn.reshape(1, batch, HEADS, DK),
            pm[:, :, 0].reshape(1, batch, HEADS),
            sf_re.reshape(1, nb, SSM_GROUPS, SSM_STATE),
            sf_im.reshape(1, nb, SSM_GROUPS, SSM_STATE),
            sc.reshape(1, nb, HEADS, DK, DV),
            sn.reshape(1, nb, HEADS, DK),
            sm.reshape(1, nb, HEADS))
```

```python
import functools

import jax
import jax.numpy as jnp
import numpy as np
from jax import lax
from jax.experimental import pallas as pl
from jax.experimental.pallas import tpu as pltpu

F32 = jnp.float32
BF16 = jnp.bfloat16

D_MODEL = 2048
N_META = 16
SSM_WIDTH = 1024
SSM_GROUP = 16
SSM_GROUPS = 64
SSM_STATE = 64
SSM_FLAT = SSM_GROUPS * SSM_STATE
SSM_JBLK = 4
HEADS = 4
DK = 256
DV = 256
CHUNK = 128
D_FF = 5632
EPS = 1e-5
ALPHA = 2.0 ** 0.25

S5_TC = 256
S5_NSEG = 8
S5_SEG = S5_TC // S5_NSEG
S5_LW = 512

V7X_SCOPED_VMEM_MAX_BYTES = 60000 * 1024


def _cparams(n_axes, vmem_mb):
    return pltpu.CompilerParams(
        dimension_semantics=("arbitrary",) * n_axes,
        vmem_limit_bytes=min(vmem_mb * 1024 * 1024, V7X_SCOPED_VMEM_MAX_BYTES),
    )


def _sigmoid(x):
    return 1.0 / (1.0 + jnp.exp(-x))


def _log_sigmoid(x):
    return jnp.minimum(x, 0.0) - jnp.log(1.0 + jnp.exp(-jnp.abs(x)))


def _gelu_tanh(x):
    c = 0.7978845608028654
    return 0.5 * x * (1.0 + jnp.tanh(c * (x + 0.044715 * (x * x * x))))


def _layernorm_rows(x, g, b):
    mu = jnp.mean(x, axis=-1, keepdims=True)
    xc = x - mu
    var = jnp.mean(xc * xc, axis=-1, keepdims=True)
    return xc * lax.rsqrt(var + EPS) * g + b


PROJ_TN = 1024
GATE_ROWS = 2 * HEADS
TAIL_ROW0 = SSM_WIDTH + 4 * HEADS * DK
NT_DIMS = (((1,), (1,)), ((), ()))


def _proj_kernel(with_gates, x_ref, xs_ref, wt_ref, *rest):
    if with_gates:
        wg_ref, o_ref, os_ref, g_ref, gt_ref, gs_ref, wb_s = rest
    else:
        o_ref, os_ref, wb_s = rest
    i = pl.program_id(1)

    @pl.when(i == 0)
    def _():
        wb_s[...] = wt_ref[...].astype(BF16)

    xb = x_ref[...].astype(BF16)
    o_ref[...] = lax.dot_general(xb, wb_s[...], NT_DIMS, preferred_element_type=F32).astype(o_ref.dtype)
    if with_gates:
        wg = wg_ref[...].astype(BF16)
        g_ref[...] = lax.dot_general(xb, wg, NT_DIMS, preferred_element_type=F32)
        gt_ref[...] = lax.dot_general(wg, xb, NT_DIMS, preferred_element_type=F32)

    @pl.when(i == pl.num_programs(1) - 1)
    def _():
        xsb = xs_ref[...].astype(BF16)
        os_ref[...] = lax.dot_general(xsb, wb_s[...], NT_DIMS, preferred_element_type=F32).astype(os_ref.dtype)
        if with_gates:
            gs_ref[...] = lax.dot_general(xsb, wg_ref[...].astype(BF16), NT_DIMS, preferred_element_type=F32)


def _proj(x, xs, wt3, blocks, out_dtype, small_dtype, tm, name, with_gates=False):
    r, k = x.shape
    rs = xs.shape[0]
    first, step, count = blocks
    tn = PROJ_TN
    nm = r // tm
    in_specs = [pl.BlockSpec((tm, k), lambda j, i: (i, 0)),
                pl.BlockSpec((rs, k), lambda j, i: (0, 0)),
                pl.BlockSpec((None, tn, k), lambda j, i: (0, first + step * j, 0))]
    out_specs = [pl.BlockSpec((tm, tn), lambda j, i: (i, j)),
                 pl.BlockSpec((rs, tn), lambda j, i: (0, j))]
    out_shape = [jax.ShapeDtypeStruct((r, count * tn), out_dtype),
                 jax.ShapeDtypeStruct((rs, count * tn), small_dtype)]
    args = (x, xs, wt3)
    if with_gates:
        ng = GATE_ROWS
        in_specs += [pl.BlockSpec((None, ng, k), lambda j, i: (0, TAIL_ROW0 // ng, 0))]
        out_specs += [pl.BlockSpec((tm, ng), lambda j, i: (j * nm + i, 0)),
                      pl.BlockSpec((ng, tm), lambda j, i: (0, j * nm + i)),
                      pl.BlockSpec((rs, ng), lambda j, i: (j, 0))]
        out_shape += [jax.ShapeDtypeStruct((count * r, ng), F32), jax.ShapeDtypeStruct((ng, count * r), F32),
                      jax.ShapeDtypeStruct((count * rs, ng), F32)]
        args = (x, xs, wt3, wt3)
    return pl.pallas_call(
        functools.partial(_proj_kernel, with_gates),
        grid=(count, nm),
        in_specs=in_specs,
        out_specs=out_specs,
        out_shape=out_shape,
        scratch_shapes=[pltpu.VMEM((tn, k), BF16)],
        compiler_params=_cparams(2, 56),
        name=name,
    )(*args)


def _s5_discretise(a_re, a_im, log_dt, b_re, b_im, c_re, c_im):
    dt = jnp.exp(log_dt)
    e = jnp.exp(a_re * dt)
    abar_re = e * jnp.cos(a_im * dt)
    abar_im = e * jnp.sin(a_im * dt)
    nr = abar_re - 1.0
    ni = abar_im
    den = a_re * a_re + a_im * a_im
    coef_re = (nr * a_re + ni * a_im) / den
    coef_im = (ni * a_re - nr * a_im) / den
    bb_re = coef_re[..., None] * b_re - coef_im[..., None] * b_im
    bb_im = coef_re[..., None] * b_im + coef_im[..., None] * b_re
    gpb = SSM_GROUPS // SSM_JBLK

    def b_rows(bb):
        return bb.reshape(SSM_JBLK, gpb, SSM_STATE, SSM_GROUP).transpose(0, 1, 3, 2).reshape(SSM_JBLK, 256, SSM_STATE)

    def c_rows(cc):
        return cc.reshape(SSM_JBLK, gpb, SSM_GROUP, SSM_STATE).transpose(0, 1, 3, 2).reshape(SSM_JBLK, 1024, SSM_GROUP)

    bt = jnp.concatenate([jnp.tile(b_rows(bb_re), (1, 1, 2)), jnp.tile(b_rows(bb_im), (1, 1, 2))], axis=-1)
    ct = jnp.concatenate([jnp.tile(c_rows(c_re), (1, 1, 8)), jnp.tile(c_rows(-c_im), (1, 1, 8))], axis=-1)
    return abar_re.reshape(1, SSM_FLAT), abar_im.reshape(1, SSM_FLAT), bt, ct


def _s5_expand_weights(bt_ref, ct_ref, bw_s, cwr_s, cwi_s):
    rb = lax.broadcasted_iota(jnp.int32, (256, 2048), 0)
    cb = lax.broadcasted_iota(jnp.int32, (256, 2048), 1)
    bmask = (rb >> 4) == ((cb & 1023) >> 6)
    rc = lax.broadcasted_iota(jnp.int32, (1024, 256), 0)
    cc = lax.broadcasted_iota(jnp.int32, (1024, 256), 1)
    cmask = (rc >> 6) == (cc >> 4)
    for j in range(SSM_JBLK):
        bt = bt_ref[j]
        wide = jnp.concatenate([bt[:, :128]] * 8 + [bt[:, 128:]] * 8, axis=1)
        bw_s[j] = jnp.where(bmask, wide, 0.0).astype(BF16)
        ct = ct_ref[j]
        cwr_s[j] = jnp.where(cmask, jnp.concatenate([ct[:, :128]] * 2, axis=1), 0.0).astype(BF16)
        cwi_s[j] = jnp.where(cmask, jnp.concatenate([ct[:, 128:]] * 2, axis=1), 0.0).astype(BF16)


def _cmul_add(ar, ai, hr, hi, xr, xi):
    return ar * hr - ai * hi + xr, ar * hi + ai * hr + xi


def _s5_project_in(u, b_ref, hre_ref, him_ref, rows):
    for j in range(SSM_JBLK):
        bu = jnp.dot(u[:, j * 256:(j + 1) * 256], b_ref[j], preferred_element_type=F32)
        hre_ref[0:rows, j * 1024:(j + 1) * 1024] = bu[:, :1024]
        him_ref[0:rows, j * 1024:(j + 1) * 1024] = bu[:, 1024:]


def _s5_project_out(hre_ref, him_ref, cre_ref, cimn_ref, rows):
    ys = []
    for j in range(SSM_JBLK):
        hr = hre_ref[0:rows, j * 1024:(j + 1) * 1024].astype(BF16)
        hi = him_ref[0:rows, j * 1024:(j + 1) * 1024].astype(BF16)
        y = jnp.dot(hr, cre_ref[j], preferred_element_type=F32)
        ys.append(y + jnp.dot(hi, cimn_ref[j], preferred_element_type=F32))
    return jnp.concatenate(ys, axis=1)


def _s5_prompt_kernel(u_ref, um_ref, perm_ref, permt_ref, bt_ref, ct_ref, are_ref, aim_ref, d_ref,
                      y_ref, fre_ref, fim_ref,
                      hre_ref, him_ref, pre_ref, pim_ref, cre_s, cim_s, hsre_ref, hsim_ref,
                      b_ref, cre_ref, cimn_ref):
    b_id = pl.program_id(0)
    c_id = pl.program_id(1)
    n_c = pl.num_programs(1)

    @pl.when(jnp.logical_and(b_id == 0, c_id == 0))
    def _():
        _s5_expand_weights(bt_ref, ct_ref, b_ref, cre_ref, cimn_ref)
        ar = jnp.broadcast_to(are_ref[...], (S5_NSEG, SSM_FLAT))
        ai = jnp.broadcast_to(aim_ref[...], (S5_NSEG, SSM_FLAT))
        pre_ref[0:S5_NSEG, :] = ar
        pim_ref[0:S5_NSEG, :] = ai

        def body(i, carry):
            pr, pi = carry
            nr = ar * pr - ai * pi
            ni = ar * pi + ai * pr
            rows = pl.ds(pl.multiple_of(i * S5_NSEG, S5_NSEG), S5_NSEG)
            pre_ref[rows, :] = nr
            pim_ref[rows, :] = ni
            return nr, ni

        lax.fori_loop(1, S5_SEG, body, (ar, ai))

    @pl.when(c_id == 0)
    def _():
        _s5_project_in(um_ref[...].astype(BF16), b_ref, hre_ref, him_ref, N_META)
        ar = are_ref[...]
        ai = aim_ref[...]
        hr = jnp.zeros((1, SSM_FLAT), F32)
        hi = jnp.zeros((1, SSM_FLAT), F32)
        for t in range(N_META):
            hr, hi = _cmul_add(ar, ai, hr, hi, hre_ref[t:t + 1, :], him_ref[t:t + 1, :])
        cre_s[...] = hr
        cim_s[...] = hi

    u = u_ref[...]
    u_perm = jnp.dot(perm_ref[...], u.astype(BF16), preferred_element_type=F32).astype(BF16)
    _s5_project_in(u_perm, b_ref, hre_ref, him_ref, S5_TC)

    full = (S5_NSEG, S5_LW)
    for lg in range(SSM_FLAT // S5_LW):
        lanes = slice(lg * S5_LW, (lg + 1) * S5_LW)
        ar = jnp.broadcast_to(are_ref[:, lanes], full)
        ai = jnp.broadcast_to(aim_ref[:, lanes], full)

        def local_step(k, carry):
            rows = pl.ds(pl.multiple_of(k * S5_NSEG, S5_NSEG), S5_NSEG)
            nr, ni = _cmul_add(ar, ai, carry[0], carry[1], hre_ref[rows, lanes], him_ref[rows, lanes])
            hre_ref[rows, lanes] = nr
            him_ref[rows, lanes] = ni
            return nr, ni

        er, ei = lax.fori_loop(0, S5_SEG, local_step, (jnp.zeros(full, F32), jnp.zeros(full, F32)), unroll=4)

        asr = pre_ref[S5_TC - 1:S5_TC, lanes]
        asi = pim_ref[S5_TC - 1:S5_TC, lanes]
        hr = cre_s[:, lanes]
        hi = cim_s[:, lanes]
        for j in range(S5_NSEG):
            hsre_ref[j:j + 1, lanes] = hr
            hsim_ref[j:j + 1, lanes] = hi
            hr, hi = _cmul_add(asr, asi, hr, hi, er[j:j + 1, :], ei[j:j + 1, :])
        cre_s[:, lanes] = hr
        cim_s[:, lanes] = hi

        sr = hsre_ref[:, lanes]
        si = hsim_ref[:, lanes]

        def fix_step(k, carry):
            rows = pl.ds(pl.multiple_of(k * S5_NSEG, S5_NSEG), S5_NSEG)
            pr = pre_ref[rows, lanes]
            pi = pim_ref[rows, lanes]
            hre_ref[rows, lanes] = hre_ref[rows, lanes] + (pr * sr - pi * si)
            him_ref[rows, lanes] = him_ref[rows, lanes] + (pr * si + pi * sr)
            return carry

        lax.fori_loop(0, S5_SEG, fix_step, 0, unroll=4)

    y_perm = _s5_project_out(hre_ref, him_ref, cre_ref, cimn_ref, S5_TC)
    permt = permt_ref[...]
    y = sum(jnp.dot(permt, piece, preferred_element_type=F32) for piece in _split3(y_perm))
    y_ref[...] = y + d_ref[...] * u

    @pl.when(c_id == n_c - 1)
    def _():
        fre_ref[0] = cre_s[...]
        fim_ref[0] = cim_s[...]


def _s5_weight_scratch():
    return [pltpu.VMEM((SSM_JBLK, 256, 2048), BF16), pltpu.VMEM((SSM_JBLK, 1024, 256), BF16),
            pltpu.VMEM((SSM_JBLK, 1024, 256), BF16)]


def _s5_prompt(u_arr, u_col_block, u_meta, abar_re, abar_im, bt, ct, d_skip, batch, seq):
    n_c = seq // S5_TC
    const3 = lambda b, c: (0, 0, 0)
    const2 = lambda b, c: (0, 0)
    rows = np.arange(S5_TC)
    perm_np = np.zeros((S5_TC, S5_TC), np.float32)
    perm_np[rows, (rows % S5_NSEG) * S5_SEG + rows // S5_NSEG] = 1.0
    perm = jnp.asarray(perm_np, dtype=BF16)
    permt = jnp.asarray(perm_np.T, dtype=BF16)
    return pl.pallas_call(
        _s5_prompt_kernel,
        grid=(batch, n_c),
        in_specs=[pl.BlockSpec((S5_TC, SSM_WIDTH), lambda b, c: (b * n_c + c, u_col_block)),
                  pl.BlockSpec((N_META, SSM_WIDTH), const2),
                  pl.BlockSpec((S5_TC, S5_TC), const2),
                  pl.BlockSpec((S5_TC, S5_TC), const2),
                  pl.BlockSpec((SSM_JBLK, 256, 256), const3),
                  pl.BlockSpec((SSM_JBLK, 1024, 256), const3),
                  pl.BlockSpec((1, SSM_FLAT), const2),
                  pl.BlockSpec((1, SSM_FLAT), const2),
                  pl.BlockSpec((1, SSM_WIDTH), const2)],
        out_specs=[pl.BlockSpec((S5_TC, SSM_WIDTH), lambda b, c: (b * n_c + c, 0)),
                   pl.BlockSpec((1, 1, SSM_FLAT), lambda b, c: (b, 0, 0)),
                   pl.BlockSpec((1, 1, SSM_FLAT), lambda b, c: (b, 0, 0))],
        out_shape=[jax.ShapeDtypeStruct((batch * seq, SSM_WIDTH), F32),
                   jax.ShapeDtypeStruct((batch, 1, SSM_FLAT), F32),
                   jax.ShapeDtypeStruct((batch, 1, SSM_FLAT), F32)],
        scratch_shapes=[pltpu.VMEM((S5_TC, SSM_FLAT), F32), pltpu.VMEM((S5_TC, SSM_FLAT), F32),
                        pltpu.VMEM((S5_TC, SSM_FLAT), F32), pltpu.VMEM((S5_TC, SSM_FLAT), F32),
                        pltpu.VMEM((1, SSM_FLAT), F32), pltpu.VMEM((1, SSM_FLAT), F32),
                        pltpu.VMEM((S5_NSEG, SSM_FLAT), F32), pltpu.VMEM((S5_NSEG, SSM_FLAT), F32)]
                       + _s5_weight_scratch(),
        compiler_params=_cparams(2, 48),
        name="s5_prompt",
    )(u_arr, u_meta, perm, permt, bt, ct, abar_re, abar_im, d_skip)


def _s5_sample_kernel(u_ref, h0re_ref, h0im_ref, bt_ref, ct_ref, are_ref, aim_ref, d_ref,
                      y_ref, hre_ref, him_ref, b_ref, cre_ref, cimn_ref):
    rows = u_ref.shape[0]
    _s5_expand_weights(bt_ref, ct_ref, b_ref, cre_ref, cimn_ref)
    u = u_ref[...]
    _s5_project_in(u.astype(BF16), b_ref, hre_ref, him_ref, rows)
    nr, ni = _cmul_add(are_ref[...], aim_ref[...], h0re_ref[...], h0im_ref[...], hre_ref[...], him_ref[...])
    hre_ref[...] = nr
    him_ref[...] = ni
    y_ref[...] = _s5_project_out(hre_ref, him_ref, cre_ref, cimn_ref, rows) + d_ref[...] * u


def _s5_sample(u, h0_re, h0_im, abar_re, abar_im, bt, ct, d_skip):
    rows = u.shape[0]
    return pl.pallas_call(
        _s5_sample_kernel,
        out_shape=[jax.ShapeDtypeStruct((rows, SSM_WIDTH), F32),
                   jax.ShapeDtypeStruct((rows, SSM_FLAT), F32),
                   jax.ShapeDtypeStruct((rows, SSM_FLAT), F32)],
        scratch_shapes=_s5_weight_scratch(),
        compiler_params=_cparams(0, 48),
        name="s5_sample",
    )(u, h0_re, h0_im, bt, ct, abar_re, abar_im, d_skip)


def _split3(x):
    hi = x.astype(BF16)
    r1 = x - hi.astype(F32)
    mid = r1.astype(BF16)
    lo = (r1 - mid.astype(F32)).astype(BF16)
    return hi, mid, lo


def _tri(n, lower):
    r = lax.broadcasted_iota(jnp.int32, (n, n), 0)
    c = lax.broadcasted_iota(jnp.int32, (n, n), 1)
    return (r >= c) if lower else (r <= c)


def _gate_columns(gcol, bias_row):
    n = gcol.shape[0]
    g = gcol + bias_row
    fl = _log_sigmoid(g[:, HEADS:])
    tril = jnp.where(_tri(n, True), 1.0, 0.0).astype(BF16)
    b = sum(jnp.dot(tril, p, preferred_element_type=F32) for p in _split3(fl))
    return g[:, :HEADS], b


def _gate_rows(grow, bias_col):
    n = grow.shape[1]
    g = grow + bias_col
    fl = _log_sigmoid(g[HEADS:, :])
    triu = jnp.where(_tri(n, False), 1.0, 0.0).astype(BF16)
    b = sum(jnp.dot(p, triu, preferred_element_type=F32) for p in _split3(fl))
    return g[:HEADS, :], b


MLSTM_GROUP = 2


def _eye_bf16(n):
    return jnp.where(_tri(n, True) & _tri(n, False), 1.0, 0.0).astype(BF16)


def _mlstm_state_update(k_bf, v, ig_col, b_col, c_old, n_old, m_old, k_t=None, gate_rows=None):
    n_rows = k_bf.shape[0]
    k_scale = DK ** -0.5
    b_last = b_col[n_rows - 1:n_rows, :]
    e_col = b_last - b_col + ig_col
    m_new = jnp.maximum(b_last + m_old, jnp.max(e_col, axis=0, keepdims=True))
    w_end = jnp.exp(e_col - m_new) * k_scale
    carry = jnp.exp(b_last + m_old - m_new)
    if k_t is None:
        k_t = lax.dot_general(_eye_bf16(DK), k_bf, NT_DIMS, preferred_element_type=F32)
    if gate_rows is None:
        kv = jnp.dot(k_t.astype(BF16), (v.astype(F32) * w_end).astype(BF16), preferred_element_type=F32)
    else:
        ig_row, b_row = gate_rows
        w_end_row = jnp.exp(b_last - b_row + ig_row - m_new) * k_scale
        kv = jnp.dot((k_t * w_end_row).astype(BF16), v, preferred_element_type=F32)
    c_new = carry * c_old + kv
    n_new = carry * n_old + jnp.sum(k_bf.astype(F32) * w_end, axis=0, keepdims=True)
    return c_new, n_new, m_new


def _mlstm_decay(ig_row, b_row, b_col, m_old):
    n_rows = b_col.shape[0]
    dmat = jnp.where(_tri(n_rows, True), b_col + (ig_row - b_row), -jnp.inf)
    inter = b_col + m_old
    m_t = jnp.maximum(inter, jnp.max(dmat, axis=-1, keepdims=True))
    return jnp.exp(dmat - m_t), jnp.exp(inter - m_t), m_t


def _head_norm_gate(h, o, gain):
    mu = jnp.mean(h, axis=-1, keepdims=True)
    hc = h - mu
    var = jnp.mean(hc * hc, axis=-1, keepdims=True)
    return _sigmoid(o) * (hc * lax.rsqrt(var + EPS) * gain)


def _mlstm_prompt_kernel(batch, q_ref, k_ref, v_ref, o_ref, gc_ref, *rest):
    gr_refs = rest[:batch]
    km_ref, vm_ref, gcm_ref, brow_ref, bcol_ref, gain_ref, y_ref, cf_ref, nf_ref, mf_ref = rest[batch:]
    k_scale = DK ** -0.5

    @pl.when(pl.program_id(0) == 0)
    def _():
        ig_c, b_c = _gate_columns(gcm_ref[...], brow_ref[...])
        for h in range(HEADS):
            hs = slice(h * DK, (h + 1) * DK)
            c_new, n_new, m_new = _mlstm_state_update(
                km_ref[:, hs], vm_ref[:, hs], ig_c[:, h:h + 1], b_c[:, h:h + 1],
                jnp.zeros((DK, DV), F32), jnp.zeros((1, DK), F32), jnp.zeros((1, 1), F32))
            for b in range(batch):
                cf_ref[b, h] = c_new
                nf_ref[b, h] = n_new
                mf_ref[b, h:h + 1, :] = jnp.broadcast_to(m_new, (1, 128))

    eye = _eye_bf16(DK)
    for b0 in range(0, batch, MLSTM_GROUP):
        gates = {}
        for b in range(b0, b0 + MLSTM_GROUP):
            gates[b] = _gate_columns(gc_ref[b], brow_ref[...]) + _gate_rows(gr_refs[b][...], bcol_ref[...])
        chains = [(b, h) for b in range(b0, b0 + MLSTM_GROUP) for h in range(HEADS)]

        ops = []
        for b, h in chains:
            hs = slice(h * DK, (h + 1) * DK)
            q = q_ref[b, :, hs]
            k_bf = k_ref[b, :, hs]
            c_old = cf_ref[b, h]
            ops.append(dict(
                q=q, k_bf=k_bf, v=v_ref[b, :, hs], c_old=c_old, n_old=nf_ref[b, h], m_old=mf_ref[b, h:h + 1, 0:1],
                s_raw=lax.dot_general(q, k_bf, NT_DIMS, preferred_element_type=F32),
                qc=jnp.dot(q, c_old.astype(BF16), preferred_element_type=F32),
                k_t=lax.dot_general(eye, k_bf, NT_DIMS, preferred_element_type=F32)))

        for (b, h), c in zip(chains, ops):
            ig_c, b_c, ig_r, b_r = gates[b]
            c["w_intra"], c["w_inter"], c["m_t"] = _mlstm_decay(ig_r[h:h + 1, :], b_r[h:h + 1, :], b_c[:, h:h + 1],
                                                                c["m_old"])

        for (b, h), c in zip(chains, ops):
            hs = slice(h * DK, (h + 1) * DK)
            s = c["s_raw"] * (c["w_intra"] * k_scale)
            num = c["w_inter"] * c["qc"] + jnp.dot(s.astype(BF16), c["v"], preferred_element_type=F32)
            den = (c["w_inter"] * jnp.sum(c["q"].astype(F32) * c["n_old"], axis=-1, keepdims=True)
                   + jnp.sum(s, axis=-1, keepdims=True))
            hid = num / jnp.maximum(jnp.abs(den), jnp.exp(-c["m_t"]))
            y_ref[b, :, hs] = _head_norm_gate(hid, o_ref[b, :, hs], gain_ref[:, hs]).astype(y_ref.dtype)

        for (b, h), c in zip(chains, ops):
            ig_c, b_c, ig_r, b_r = gates[b]
            c_new, n_new, m_new = _mlstm_state_update(c["k_bf"], c["v"], ig_c[:, h:h + 1], b_c[:, h:h + 1],
                                                      c["c_old"], c["n_old"], c["m_old"], k_t=c["k_t"],
                                                      gate_rows=(ig_r[h:h + 1, :], b_r[h:h + 1, :]))
            cf_ref[b, h] = c_new
            nf_ref[b, h] = n_new
            mf_ref[b, h:h + 1, :] = jnp.broadcast_to(m_new, (1, 128))


def _mlstm_prompt(qkv, o_arr, o_col_block, gcol, grow, qkv_meta, gcol_meta, bias_row, bias_col, gain, batch, seq):
    n_c = seq // CHUNK
    width = HEADS * DK
    const2 = lambda c: (0, 0)
    qkv3 = qkv.reshape(batch, seq, 3 * width)
    o3 = o_arr.reshape(batch, seq, o_arr.shape[1])
    gcol3 = gcol.reshape(gcol.shape[0] // seq, seq, 2 * HEADS)
    state0 = lambda c: (0, 0, 0, 0)
    return pl.pallas_call(
        functools.partial(_mlstm_prompt_kernel, batch),
        grid=(n_c,),
        in_specs=[pl.BlockSpec((batch, CHUNK, width), lambda c: (0, c, 0)),
                  pl.BlockSpec((batch, CHUNK, width), lambda c: (0, c, 1)),
                  pl.BlockSpec((batch, CHUNK, width), lambda c: (0, c, 2)),
                  pl.BlockSpec((batch, CHUNK, width), lambda c: (0, c, o_col_block)),
                  pl.BlockSpec((batch, CHUNK, 2 * HEADS), lambda c: (0, c, 0))]
                 + [pl.BlockSpec((2 * HEADS, CHUNK), functools.partial(lambda b, c: (0, b * n_c + c), b))
                    for b in range(batch)]
                 + [pl.BlockSpec((N_META, width), lambda c: (0, 1)),
                    pl.BlockSpec((N_META, width), lambda c: (0, 2)),
                    pl.BlockSpec((N_META, 2 * HEADS), const2),
                    pl.BlockSpec((1, 2 * HEADS), const2),
                    pl.BlockSpec((2 * HEADS, 1), const2),
                    pl.BlockSpec((1, width), const2)],
        out_specs=[pl.BlockSpec((batch, CHUNK, width), lambda c: (0, c, 0)),
                   pl.BlockSpec((batch, HEADS, DK, DV), state0),
                   pl.BlockSpec((batch, HEADS, 1, DK), state0),
                   pl.BlockSpec((batch, HEADS, 128), lambda c: (0, 0, 0))],
        out_shape=[jax.ShapeDtypeStruct((batch, seq, width), BF16),
                   jax.ShapeDtypeStruct((batch, HEADS, DK, DV), F32),
                   jax.ShapeDtypeStruct((batch, HEADS, 1, DK), F32),
                   jax.ShapeDtypeStruct((batch, HEADS, 128), F32)],
        compiler_params=_cparams(1, 40),
        name="mlstm_prompt",
    )(qkv3, qkv3, qkv3, o3, gcol3, *([grow] * batch), qkv_meta, qkv_meta, gcol_meta, bias_row, bias_col, gain)


MLSTM_SB = 8


def _mlstm_sample_kernel(qkv_ref, o_ref, g_ref, c_ref, n_ref, m_ref, brow_ref, gain_ref,
                         y_ref, co_ref, no_ref, mo_ref):
    sb = MLSTM_SB
    k_scale = DK ** -0.5
    eye = _eye_bf16(DK)
    width = HEADS * DK
    pad = 2 * sb
    row_id = lax.broadcasted_iota(jnp.int32, (pad, DV), 0)

    g_all = g_ref[...] + brow_ref[...]
    ig_all = g_all[:, :HEADS]
    fl_all = _log_sigmoid(g_all[:, HEADS:])
    m_old_all = m_ref[...]
    m_new_all = jnp.maximum(fl_all + m_old_all, ig_all)
    w_in_all = jnp.exp(ig_all - m_new_all)
    w_ca_all = jnp.exp(fl_all + m_old_all - m_new_all)
    floor_all = jnp.exp(-m_new_all)
    mo_ref[...] = m_new_all

    for h in range(HEADS):
        hs = slice(h * DK, (h + 1) * DK)
        qk_rows = jnp.concatenate([qkv_ref[:, hs], qkv_ref[:, width + h * DK:width + (h + 1) * DK] * k_scale],
                                  axis=0).astype(BF16)
        qk_cols = lax.dot_general(eye, qk_rows, NT_DIMS, preferred_element_type=F32).astype(BF16)
        for i in range(sb):
            w_in = w_in_all[i:i + 1, h:h + 1]
            w_ca = w_ca_all[i:i + 1, h:h + 1]
            q = qk_rows[i:i + 1, :]
            qf = q.astype(F32)
            k = qk_rows[sb + i:sb + i + 1, :].astype(F32)
            v = qkv_ref[i:i + 1, 2 * width + h * DK:2 * width + (h + 1) * DK].astype(BF16).astype(F32)
            c_old = c_ref[i, h]
            n_old = n_ref[i, h]
            qc = jnp.dot(jnp.broadcast_to(q, (pad, DK)), c_old.astype(BF16), preferred_element_type=F32)[0:1, :]
            v_rows = jnp.where(row_id == sb + i, jnp.broadcast_to(v * w_in, (pad, DV)), 0.0).astype(BF16)
            kv = jnp.dot(qk_cols, v_rows, preferred_element_type=F32)
            s = jnp.sum(qf * k, axis=-1, keepdims=True) * w_in
            num = w_ca * qc + s * v
            den = w_ca * jnp.sum(qf * n_old, axis=-1, keepdims=True) + s
            hid = num / jnp.maximum(jnp.abs(den), floor_all[i:i + 1, h:h + 1])
            co_ref[i, h] = w_ca * c_old + kv
            no_ref[i, h] = w_ca * n_old + k * w_in
            y_ref[i:i + 1, hs] = _head_norm_gate(hid, o_ref[i:i + 1, hs], gain_ref[:, hs])


def _mlstm_sample(qkv, o_arr, gcol, c0, n0, m0, bias_row, gain):
    nb = qkv.shape[0]
    width = HEADS * DK
    sb = MLSTM_SB
    const2 = lambda i: (0, 0)
    return pl.pallas_call(
        _mlstm_sample_kernel,
        grid=(nb // sb,),
        in_specs=[pl.BlockSpec((sb, 3 * width), lambda i: (i, 0)),
                  pl.BlockSpec((sb, width), lambda i: (i, 0)),
                  pl.BlockSpec((sb, 2 * HEADS), lambda i: (i, 0)),
                  pl.BlockSpec((sb, HEADS, DK, DV), lambda i: (i, 0, 0, 0)),
                  pl.BlockSpec((sb, HEADS, 1, DK), lambda i: (i, 0, 0, 0)),
                  pl.BlockSpec((sb, HEADS), lambda i: (i, 0)),
                  pl.BlockSpec((1, 2 * HEADS), const2),
                  pl.BlockSpec((1, width), const2)],
        out_specs=[pl.BlockSpec((sb, width), lambda i: (i, 0)),
                   pl.BlockSpec((sb, HEADS, DK, DV), lambda i: (i, 0, 0, 0)),
                   pl.BlockSpec((sb, HEADS, 1, DK), lambda i: (i, 0, 0, 0)),
                   pl.BlockSpec((sb, HEADS), lambda i: (i, 0))],
        out_shape=[jax.ShapeDtypeStruct((nb, width), F32),
                   jax.ShapeDtypeStruct((nb, HEADS, DK, DV), F32),
                   jax.ShapeDtypeStruct((nb, HEADS, 1, DK), F32),
                   jax.ShapeDtypeStruct((nb, HEADS), F32)],
        compiler_params=_cparams(1, 48),
        name="mlstm_sample",
    )(qkv, o_arr, gcol, c0, n0, m0, bias_row, gain)


def _merge_rows(ya, yb, x, wgab_ref, wglu_ref, bglu_ref, wa_ref, wb_ref, wout_ref, g1_ref, b1_ref):
    xb = x.astype(BF16)
    g = _gelu_tanh(ya)
    z = jnp.dot(g.astype(BF16), wglu_ref[...], preferred_element_type=F32) + bglu_ref[...]
    out_a = g * _sigmoid(z)
    up_a = jnp.dot(out_a.astype(BF16), wa_ref[...], preferred_element_type=F32)
    g_a = lax.dot_general(xb, wgab_ref[0:D_MODEL, :], NT_DIMS, preferred_element_type=F32)
    mix = _sigmoid(g_a) * up_a
    up_b = jnp.dot(yb, wb_ref[...], preferred_element_type=F32)
    g_b = lax.dot_general(xb, wgab_ref[D_MODEL:2 * D_MODEL, :], NT_DIMS, preferred_element_type=F32)
    mix = mix + _sigmoid(g_b) * up_b
    mo = jnp.dot(mix.astype(BF16), wout_ref[...], preferred_element_type=F32)
    return _layernorm_rows(ALPHA * x + mo, g1_ref[...], b1_ref[...])


def _merge_kernel(ya_ref, yb_ref, x_ref, yas_ref, ybs_ref, xs_ref, *rest):
    weights, (o_ref, os_ref, ob_ref, osb_ref) = rest[:-4], rest[-4:]
    x1 = _merge_rows(ya_ref[...], yb_ref[...], x_ref[...], *weights)
    o_ref[...] = x1
    ob_ref[...] = x1.astype(BF16)

    @pl.when(pl.program_id(0) == pl.num_programs(0) - 1)
    def _():
        x1s = _merge_rows(yas_ref[...], ybs_ref[...].astype(BF16), xs_ref[...], *weights)
        os_ref[...] = x1s
        osb_ref[...] = x1s.astype(BF16)


def _merge(ya, yb, x, ya_s, yb_s, x_s, wt_gab, w_glu, b_glu, w_a_up, w_b_up, w_out, ln_g, ln_b, tm):
    r = x.shape[0]
    rs = x_s.shape[0]
    const2 = lambda i: (0, 0)
    resident = functools.partial(pl.BlockSpec, index_map=const2, pipeline_mode=pl.Buffered(1))
    return pl.pallas_call(
        _merge_kernel,
        grid=(r // tm,),
        in_specs=[pl.BlockSpec((tm, SSM_WIDTH), lambda i: (i, 0)),
                  pl.BlockSpec((tm, HEADS * DV), lambda i: (i, 0)),
                  pl.BlockSpec((tm, D_MODEL), lambda i: (i, 0)),
                  resident((rs, SSM_WIDTH)),
                  resident((rs, HEADS * DV)),
                  resident((rs, D_MODEL)),
                  resident((2 * D_MODEL, D_MODEL)),
                  resident((SSM_WIDTH, SSM_WIDTH)),
                  resident((1, SSM_WIDTH)),
                  resident((SSM_WIDTH, D_MODEL)),
                  resident((HEADS * DV, D_MODEL)),
                  resident((D_MODEL, D_MODEL)),
                  resident((1, D_MODEL)),
                  resident((1, D_MODEL))],
        out_specs=[pl.BlockSpec((tm, D_MODEL), lambda i: (i, 0)),
                   pl.BlockSpec((rs, D_MODEL), const2),
                   pl.BlockSpec((tm, D_MODEL), lambda i: (i, 0)),
                   pl.BlockSpec((rs, D_MODEL), const2)],
        out_shape=[jax.ShapeDtypeStruct((r, D_MODEL), F32), jax.ShapeDtypeStruct((rs, D_MODEL), F32),
                   jax.ShapeDtypeStruct((r, D_MODEL), BF16), jax.ShapeDtypeStruct((rs, D_MODEL), BF16)],
        compiler_params=_cparams(1, 59),
        name="merge_ln1",
    )(ya, yb, x, ya_s, yb_s, x_s, wt_gab, w_glu, b_glu, w_a_up, w_b_up, w_out, ln_g, ln_b)


def _swiglu(xb, wg, wu):
    hg = jnp.dot(xb, wg, preferred_element_type=F32)
    hu = jnp.dot(xb, wu, preferred_element_type=F32)
    return ((hg * _sigmoid(hg)) * hu).astype(BF16)


def _ffn_up_kernel(x_ref, xs_ref, wg_ref, wu_ref, wd_ref, h_ref, hs_ref, wdb_ref, wgb_s, wub_s):
    i = pl.program_id(1)

    @pl.when(i == 0)
    def _():
        wgb_s[...] = wg_ref[...].astype(BF16)
        wub_s[...] = wu_ref[...].astype(BF16)

    h_ref[...] = _swiglu(x_ref[...], wgb_s[...], wub_s[...])
    wdb_ref[...] = wd_ref[...].astype(BF16)

    @pl.when(i == pl.num_programs(1) - 1)
    def _():
        hs_ref[...] = _swiglu(xs_ref[...], wgb_s[...], wub_s[...])


def _ffn_up(xb, xsb, w_gate3, w_up3, w_down3, tm, tf):
    r = xb.shape[0]
    rs = xsb.shape[0]
    n_f, n_i = D_FF // tf, r // tm
    wd_rows = D_FF // (n_f * n_i)
    return pl.pallas_call(
        _ffn_up_kernel,
        grid=(n_f, n_i),
        in_specs=[pl.BlockSpec((tm, D_MODEL), lambda f, i: (i, 0)),
                  pl.BlockSpec((rs, D_MODEL), lambda f, i: (0, 0)),
                  pl.BlockSpec((None, D_MODEL, tf), lambda f, i: (0, 0, f)),
                  pl.BlockSpec((None, D_MODEL, tf), lambda f, i: (0, 0, f)),
                  pl.BlockSpec((None, wd_rows, D_MODEL), lambda f, i: (0, f * n_i + i, 0))],
        out_specs=[pl.BlockSpec((tm, tf), lambda f, i: (i, f)),
                   pl.BlockSpec((rs, tf), lambda f, i: (0, f)),
                   pl.BlockSpec((wd_rows, D_MODEL), lambda f, i: (f * n_i + i, 0))],
        out_shape=[jax.ShapeDtypeStruct((r, D_FF), BF16), jax.ShapeDtypeStruct((rs, D_FF), BF16),
                   jax.ShapeDtypeStruct((D_FF, D_MODEL), BF16)],
        scratch_shapes=[pltpu.VMEM((D_MODEL, tf), BF16), pltpu.VMEM((D_MODEL, tf), BF16)],
        compiler_params=_cparams(2, 48),
        name="ffn_up",
    )(xb, xsb, w_gate3, w_up3, w_down3)


def _ffn_down_rows(h, x, wd_ref, g2_ref, b2_ref):
    return _layernorm_rows(ALPHA * x + jnp.dot(h, wd_ref[...], preferred_element_type=F32), g2_ref[...], b2_ref[...])


def _ffn_down_kernel(h_ref, x_ref, hs_ref, xs_ref, wd_ref, g2_ref, b2_ref, o_ref, os_ref):
    o_ref[...] = _ffn_down_rows(h_ref[...], x_ref[...], wd_ref, g2_ref, b2_ref)

    @pl.when(pl.program_id(0) == pl.num_programs(0) - 1)
    def _():
        os_ref[...] = _ffn_down_rows(hs_ref[...], xs_ref[...], wd_ref, g2_ref, b2_ref)


def _ffn_down(h, x, h_s, x_s, wd_b, ln_g, ln_b, tm):
    r = x.shape[0]
    rs = x_s.shape[0]
    const2 = lambda i: (0, 0)
    resident = functools.partial(pl.BlockSpec, index_map=const2, pipeline_mode=pl.Buffered(1))
    return pl.pallas_call(
        _ffn_down_kernel,
        grid=(r // tm,),
        in_specs=[pl.BlockSpec((tm, D_FF), lambda i: (i, 0)),
                  pl.BlockSpec((tm, D_MODEL), lambda i: (i, 0)),
                  resident((rs, D_FF)),
                  resident((rs, D_MODEL)),
                  resident((D_FF, D_MODEL)),
                  resident((1, D_MODEL)),
                  resident((1, D_MODEL))],
        out_specs=[pl.BlockSpec((tm, D_MODEL), lambda i: (i, 0)),
                   pl.BlockSpec((rs, D_MODEL), const2)],
        out_shape=[jax.ShapeDtypeStruct((r, D_MODEL), F32), jax.ShapeDtypeStruct((rs, D_MODEL), F32)],
        compiler_params=_cparams(1, 56),
        name="ffn_down_ln2",
    )(h, x, h_s, x_s, wd_b, ln_g, ln_b)


def kernel(x_prompt, x_sample, state_ssm_re, state_ssm_im, state_mlstm_c, state_mlstm_n, state_mlstm_m,
           meta_tokens, w_in, b_if, ssm_a_re, ssm_a_im, ssm_log_dt, ssm_b_re, ssm_b_im, ssm_c_re, ssm_c_im,
           ssm_d, w_glu, b_glu, w_a_up, mh_gain, w_b_up, w_out, ln1_g, ln1_b, w_gate, w_up, w_down,
           ln2_g, ln2_b):
    batch, seq, _ = x_prompt.shape
    nb = x_sample.shape[0]
    width = HEADS * DK

    w_glu_b = w_glu[0].astype(BF16)
    w_a_b = w_a_up[0].astype(BF16)
    w_b_b = w_b_up[0].astype(BF16)
    w_out_b = w_out[0].astype(BF16)
    bias_row = b_if[0].reshape(1, 2 * HEADS)
    bias_col = b_if[0].reshape(2 * HEADS, 1)
    gain = mh_gain[0].reshape(1, width)
    d_skip = ssm_d[0].reshape(1, SSM_WIDTH)
    b_glu_r = b_glu[0].reshape(1, SSM_WIDTH)
    ln1 = (ln1_g[0].reshape(1, D_MODEL), ln1_b[0].reshape(1, D_MODEL))
    ln2 = (ln2_g[0].reshape(1, D_MODEL), ln2_b[0].reshape(1, D_MODEL))
    abar_re, abar_im, bt, ct = _s5_discretise(
        ssm_a_re[0], ssm_a_im[0], ssm_log_dt[0], ssm_b_re[0], ssm_b_im[0], ssm_c_re[0], ssm_c_im[0])

    xp = x_prompt.reshape(batch * seq, D_MODEL)
    xs = x_sample.reshape(nb, D_MODEL)
    xsm = jnp.concatenate([xs, meta_tokens], axis=0)
    w_in_t = jnp.swapaxes(w_in, 1, 2)
    wt_gab = w_in_t[0, TAIL_ROW0 + GATE_ROWS:, :].astype(BF16)
    uo_blocks, qkv_blocks = (0, 4, 2), (1, 1, 3)

    uo_p, uo_s, gcol_p, grow_p, gcol_s = _proj(xp, xsm, w_in_t, uo_blocks, F32, F32, 1024, "proj_uo", with_gates=True)
    qkv_p, qkv_s = _proj(xp, xsm, w_in_t, qkv_blocks, BF16, F32, 1024, "proj_qkv")

    u_meta = uo_s[nb:nb + N_META, :SSM_WIDTH]
    qkv_meta = qkv_s[nb:nb + N_META].astype(BF16)
    gcol_meta = gcol_s[nb:nb + N_META]

    ya_p, pf_re, pf_im = _s5_prompt(uo_p, 0, u_meta, abar_re, abar_im, bt, ct, d_skip, batch, seq)
    yb_p, pc, pn, pm = _mlstm_prompt(qkv_p, uo_p, 1, gcol_p, grow_p, qkv_meta, gcol_meta, bias_row, bias_col,
                                     gain, batch, seq)
    ya_s, sf_re, sf_im = _s5_sample(uo_s[:nb, :SSM_WIDTH], state_ssm_re[0].reshape(nb, SSM_FLAT),
                                    state_ssm_im[0].reshape(nb, SSM_FLAT), abar_re, abar_im, bt, ct, d_skip)
    yb_s, sc, sn, sm = _mlstm_sample(qkv_s[:nb], uo_s[:nb, SSM_WIDTH:], gcol_s[:nb], state_mlstm_c[0],
                                     state_mlstm_n[0].reshape(nb, HEADS, 1, DK), state_mlstm_m[0], bias_row, gain)

    x1_p, x1_s, x1b_p, x1b_s = _merge(ya_p, yb_p.reshape(batch * seq, width), xp, ya_s, yb_s, xs,
                                      wt_gab, w_glu_b, b_glu_r, w_a_b, w_b_b, w_out_b, *ln1, tm=256)
    h_p, h_s, w_down_b = _ffn_up(x1b_p, x1b_s, w_gate, w_up, w_down, tm=1024, tf=512)
    y_p, y_s = _ffn_down(h_p, x1_p, h_s, x1_s, w_down_b, *ln2, tm=256)

    return (y_p.reshape(batch, seq, D_MODEL),
            y_s.reshape(nb, 1, D_MODEL),
            pf_re.reshape(1, batch, SSM_GROUPS, SSM_STATE),
            pf_im.reshape(1, batch, SSM_GROUPS, SSM_STATE),
            pc.reshape(1, batch, HEADS, DK, DV),
            pn.reshape(1, batch, HEADS, DK),
            pm[:, :, 0].reshape(1, batch, HEADS),
            sf_re.reshape(1, nb, SSM_GROUPS, SSM_STATE),
            sf_im.reshape(1, nb, SSM_GROUPS, SSM_STATE),
            sc.reshape(1, nb, HEADS, DK, DV),
            sn.reshape(1, nb, HEADS, DK),
            sm.reshape(1, nb, HEADS))
```

```python
import functools

import jax
import jax.numpy as jnp
import numpy as np
from jax import lax
from jax.experimental import pallas as pl
from jax.experimental.pallas import tpu as pltpu

F32 = jnp.float32
BF16 = jnp.bfloat16

D_MODEL = 2048
N_META = 16
SSM_WIDTH = 1024
SSM_GROUP = 16
SSM_GROUPS = 64
SSM_STATE = 64
SSM_FLAT = SSM_GROUPS * SSM_STATE
SSM_JBLK = 4
HEADS = 4
DK = 256
DV = 256
CHUNK = 128
D_FF = 5632
EPS = 1e-5
ALPHA = 2.0 ** 0.25

S5_TC = 256
S5_NSEG = 8
S5_SEG = S5_TC // S5_NSEG
S5_LW = 512

V7X_SCOPED_VMEM_MAX_BYTES = 60000 * 1024


def _cparams(n_axes, vmem_mb):
    return pltpu.CompilerParams(
        dimension_semantics=("arbitrary",) * n_axes,
        vmem_limit_bytes=min(vmem_mb * 1024 * 1024, V7X_SCOPED_VMEM_MAX_BYTES),
    )


def _sigmoid(x):
    return 1.0 / (1.0 + jnp.exp(-x))


def _log_sigmoid(x):
    return jnp.minimum(x, 0.0) - jnp.log(1.0 + jnp.exp(-jnp.abs(x)))


def _gelu_tanh(x):
    c = 0.7978845608028654
    return 0.5 * x * (1.0 + jnp.tanh(c * (x + 0.044715 * (x * x * x))))


def _layernorm_rows(x, g, b):
    mu = jnp.mean(x, axis=-1, keepdims=True)
    xc = x - mu
    var = jnp.mean(xc * xc, axis=-1, keepdims=True)
    return xc * lax.rsqrt(var + EPS) * g + b


PROJ_TN = 1024
GATE_ROWS = 2 * HEADS
TAIL_ROW0 = SSM_WIDTH + 4 * HEADS * DK
NT_DIMS = (((1,), (1,)), ((), ()))


def _proj_kernel(with_gates, x_ref, xs_ref, wt_ref, *rest):
    if with_gates:
        wg_ref, o_ref, os_ref, g_ref, gt_ref, gs_ref, wb_s = rest
    else:
        o_ref, os_ref, wb_s = rest
    i = pl.program_id(1)

    @pl.when(i == 0)
    def _():
        wb_s[...] = wt_ref[...].astype(BF16)

    xb = x_ref[...].astype(BF16)
    o_ref[...] = lax.dot_general(xb, wb_s[...], NT_DIMS, preferred_element_type=F32).astype(o_ref.dtype)
    first_block = pl.program_id(0) == 0
    if with_gates:
        @pl.when(first_block)
        def _():
            wg = wg_ref[...].astype(BF16)
            g_ref[...] = lax.dot_general(xb, wg, NT_DIMS, preferred_element_type=F32)
            gt_ref[...] = lax.dot_general(wg, xb, NT_DIMS, preferred_element_type=F32)

    @pl.when(i == pl.num_programs(1) - 1)
    def _():
        xsb = xs_ref[...].astype(BF16)
        os_ref[...] = lax.dot_general(xsb, wb_s[...], NT_DIMS, preferred_element_type=F32).astype(os_ref.dtype)
        if with_gates:
            @pl.when(first_block)
            def _():
                gs_ref[...] = lax.dot_general(xsb, wg_ref[...].astype(BF16), NT_DIMS, preferred_element_type=F32)


def _proj(x, xs, wt3, blocks, out_dtype, small_dtype, tm, name, with_gates=False):
    r, k = x.shape
    rs = xs.shape[0]
    first, step, count = blocks
    tn = PROJ_TN
    nm = r // tm
    in_specs = [pl.BlockSpec((tm, k), lambda j, i: (i, 0)),
                pl.BlockSpec((rs, k), lambda j, i: (0, 0)),
                pl.BlockSpec((None, tn, k), lambda j, i: (0, first + step * j, 0))]
    out_specs = [pl.BlockSpec((tm, tn), lambda j, i: (i, j)),
                 pl.BlockSpec((rs, tn), lambda j, i: (0, j))]
    out_shape = [jax.ShapeDtypeStruct((r, count * tn), out_dtype),
                 jax.ShapeDtypeStruct((rs, count * tn), small_dtype)]
    args = (x, xs, wt3)
    if with_gates:
        ng = GATE_ROWS
        in_specs += [pl.BlockSpec((None, ng, k), lambda j, i: (0, TAIL_ROW0 // ng, 0))]
        gate_block = lambda j, i: jnp.where(j == 0, i, nm - 1)
        out_specs += [pl.BlockSpec((tm, ng), lambda j, i: (gate_block(j, i), 0)),
                      pl.BlockSpec((ng, tm), lambda j, i: (0, gate_block(j, i))),
                      pl.BlockSpec((rs, ng), lambda j, i: (0, 0))]
        out_shape += [jax.ShapeDtypeStruct((r, ng), F32), jax.ShapeDtypeStruct((ng, r), F32),
                      jax.ShapeDtypeStruct((rs, ng), F32)]
        args = (x, xs, wt3, wt3)
    return pl.pallas_call(
        functools.partial(_proj_kernel, with_gates),
        grid=(count, nm),
        in_specs=in_specs,
        out_specs=out_specs,
        out_shape=out_shape,
        scratch_shapes=[pltpu.VMEM((tn, k), BF16)],
        compiler_params=_cparams(2, 56),
        name=name,
    )(*args)


def _s5_discretise(a_re, a_im, log_dt, b_re, b_im, c_re, c_im):
    dt = jnp.exp(log_dt)
    e = jnp.exp(a_re * dt)
    abar_re = e * jnp.cos(a_im * dt)
    abar_im = e * jnp.sin(a_im * dt)
    nr = abar_re - 1.0
    ni = abar_im
    den = a_re * a_re + a_im * a_im
    coef_re = (nr * a_re + ni * a_im) / den
    coef_im = (ni * a_re - nr * a_im) / den
    bb_re = coef_re[..., None] * b_re - coef_im[..., None] * b_im
    bb_im = coef_re[..., None] * b_im + coef_im[..., None] * b_re
    gpb = SSM_GROUPS // SSM_JBLK

    def b_rows(bb):
        return bb.reshape(SSM_JBLK, gpb, SSM_STATE, SSM_GROUP).transpose(0, 1, 3, 2).reshape(SSM_JBLK, 256, SSM_STATE)

    def c_rows(cc):
        return cc.reshape(SSM_JBLK, gpb, SSM_GROUP, SSM_STATE).transpose(0, 1, 3, 2).reshape(SSM_JBLK, 1024, SSM_GROUP)

    bt = jnp.concatenate([jnp.tile(b_rows(bb_re), (1, 1, 2)), jnp.tile(b_rows(bb_im), (1, 1, 2))], axis=-1)
    ct = jnp.concatenate([jnp.tile(c_rows(c_re), (1, 1, 8)), jnp.tile(c_rows(-c_im), (1, 1, 8))], axis=-1)
    return abar_re.reshape(1, SSM_FLAT), abar_im.reshape(1, SSM_FLAT), bt, ct


def _s5_expand_weights(bt_ref, ct_ref, bw_s, cwr_s, cwi_s):
    rb = lax.broadcasted_iota(jnp.int32, (256, 2048), 0)
    cb = lax.broadcasted_iota(jnp.int32, (256, 2048), 1)
    bmask = (rb >> 4) == ((cb & 1023) >> 6)
    rc = lax.broadcasted_iota(jnp.int32, (1024, 256), 0)
    cc = lax.broadcasted_iota(jnp.int32, (1024, 256), 1)
    cmask = (rc >> 6) == (cc >> 4)
    for j in range(SSM_JBLK):
        bt = bt_ref[j]
        wide = jnp.concatenate([bt[:, :128]] * 8 + [bt[:, 128:]] * 8, axis=1)
        bw_s[j] = jnp.where(bmask, wide, 0.0).astype(BF16)
        ct = ct_ref[j]
        cwr_s[j] = jnp.where(cmask, jnp.concatenate([ct[:, :128]] * 2, axis=1), 0.0).astype(BF16)
        cwi_s[j] = jnp.where(cmask, jnp.concatenate([ct[:, 128:]] * 2, axis=1), 0.0).astype(BF16)


def _cmul_add(ar, ai, hr, hi, xr, xi):
    return ar * hr - ai * hi + xr, ar * hi + ai * hr + xi


def _s5_project_in(u, b_ref, hre_ref, him_ref, rows):
    for j in range(SSM_JBLK):
        bu = jnp.dot(u[:, j * 256:(j + 1) * 256], b_ref[j], preferred_element_type=F32)
        hre_ref[0:rows, j * 1024:(j + 1) * 1024] = bu[:, :1024]
        him_ref[0:rows, j * 1024:(j + 1) * 1024] = bu[:, 1024:]


def _s5_project_out(hre_ref, him_ref, cre_ref, cimn_ref, rows):
    ys = []
    for j in range(SSM_JBLK):
        hr = hre_ref[0:rows, j * 1024:(j + 1) * 1024].astype(BF16)
        hi = him_ref[0:rows, j * 1024:(j + 1) * 1024].astype(BF16)
        y = jnp.dot(hr, cre_ref[j], preferred_element_type=F32)
        ys.append(y + jnp.dot(hi, cimn_ref[j], preferred_element_type=F32))
    return jnp.concatenate(ys, axis=1)


def _s5_prompt_kernel(u_ref, um_ref, perm_ref, permt_ref, bt_ref, ct_ref, are_ref, aim_ref, d_ref,
                      wga_ref, wgn_ref, wout_ref, wa_ref, wb_ref, wglu_ref,
                      y_ref, fre_ref, fim_ref,
                      wgab_o, wout_o, wa_o, wb_o, wglu_o,
                      hre_ref, him_ref, pre_ref, pim_ref, cre_s, cim_s, hsre_ref, hsim_ref,
                      b_ref, cre_ref, cimn_ref):
    b_id = pl.program_id(0)
    c_id = pl.program_id(1)
    n_c = pl.num_programs(1)

    wgab_o[...] = jnp.concatenate([wga_ref[GATE_ROWS:, :], wgn_ref[...]], axis=0).astype(BF16)
    wout_o[...] = wout_ref[...].astype(BF16)
    wa_o[...] = wa_ref[...].astype(BF16)
    wb_o[...] = wb_ref[...].astype(BF16)
    wglu_o[...] = wglu_ref[...].astype(BF16)

    @pl.when(jnp.logical_and(b_id == 0, c_id == 0))
    def _():
        _s5_expand_weights(bt_ref, ct_ref, b_ref, cre_ref, cimn_ref)
        ar = jnp.broadcast_to(are_ref[...], (S5_NSEG, SSM_FLAT))
        ai = jnp.broadcast_to(aim_ref[...], (S5_NSEG, SSM_FLAT))
        pre_ref[0:S5_NSEG, :] = ar
        pim_ref[0:S5_NSEG, :] = ai

        def body(i, carry):
            pr, pi = carry
            nr = ar * pr - ai * pi
            ni = ar * pi + ai * pr
            rows = pl.ds(pl.multiple_of(i * S5_NSEG, S5_NSEG), S5_NSEG)
            pre_ref[rows, :] = nr
            pim_ref[rows, :] = ni
            return nr, ni

        lax.fori_loop(1, S5_SEG, body, (ar, ai))

    @pl.when(c_id == 0)
    def _():
        _s5_project_in(um_ref[...].astype(BF16), b_ref, hre_ref, him_ref, N_META)
        ar = are_ref[...]
        ai = aim_ref[...]
        hr = jnp.zeros((1, SSM_FLAT), F32)
        hi = jnp.zeros((1, SSM_FLAT), F32)
        for t in range(N_META):
            hr, hi = _cmul_add(ar, ai, hr, hi, hre_ref[t:t + 1, :], him_ref[t:t + 1, :])
        cre_s[...] = hr
        cim_s[...] = hi

    u = u_ref[...]
    u_perm = jnp.dot(perm_ref[...], u.astype(BF16), preferred_element_type=F32).astype(BF16)
    _s5_project_in(u_perm, b_ref, hre_ref, him_ref, S5_TC)

    full = (S5_NSEG, S5_LW)
    for lg in range(SSM_FLAT // S5_LW):
        lanes = slice(lg * S5_LW, (lg + 1) * S5_LW)
        ar = jnp.broadcast_to(are_ref[:, lanes], full)
        ai = jnp.broadcast_to(aim_ref[:, lanes], full)

        def local_step(k, carry):
            rows = pl.ds(pl.multiple_of(k * S5_NSEG, S5_NSEG), S5_NSEG)
            nr, ni = _cmul_add(ar, ai, carry[0], carry[1], hre_ref[rows, lanes], him_ref[rows, lanes])
            hre_ref[rows, lanes] = nr
            him_ref[rows, lanes] = ni
            return nr, ni

        er, ei = lax.fori_loop(0, S5_SEG, local_step, (jnp.zeros(full, F32), jnp.zeros(full, F32)), unroll=4)

        asr = pre_ref[S5_TC - 1:S5_TC, lanes]
        asi = pim_ref[S5_TC - 1:S5_TC, lanes]
        hr = cre_s[:, lanes]
        hi = cim_s[:, lanes]
        for j in range(S5_NSEG):
            hsre_ref[j:j + 1, lanes] = hr
            hsim_ref[j:j + 1, lanes] = hi
            hr, hi = _cmul_add(asr, asi, hr, hi, er[j:j + 1, :], ei[j:j + 1, :])
        cre_s[:, lanes] = hr
        cim_s[:, lanes] = hi

        sr = hsre_ref[:, lanes]
        si = hsim_ref[:, lanes]

        def fix_step(k, carry):
            rows = pl.ds(pl.multiple_of(k * S5_NSEG, S5_NSEG), S5_NSEG)
            pr = pre_ref[rows, lanes]
            pi = pim_ref[rows, lanes]
            hre_ref[rows, lanes] = hre_ref[rows, lanes] + (pr * sr - pi * si)
            him_ref[rows, lanes] = him_ref[rows, lanes] + (pr * si + pi * sr)
            return carry

        lax.fori_loop(0, S5_SEG, fix_step, 0, unroll=4)

    y_perm = _s5_project_out(hre_ref, him_ref, cre_ref, cimn_ref, S5_TC)
    permt = permt_ref[...]
    y = sum(jnp.dot(permt, piece, preferred_element_type=F32) for piece in _split3(y_perm))
    y_ref[...] = y + d_ref[...] * u

    @pl.when(c_id == n_c - 1)
    def _():
        fre_ref[0] = cre_s[...]
        fim_ref[0] = cim_s[...]


def _s5_weight_scratch():
    return [pltpu.VMEM((SSM_JBLK, 256, 2048), BF16), pltpu.VMEM((SSM_JBLK, 1024, 256), BF16),
            pltpu.VMEM((SSM_JBLK, 1024, 256), BF16)]


def _s5_prompt(u_arr, u_col_block, u_meta, abar_re, abar_im, bt, ct, d_skip, batch, seq,
               w_in_t, w_out3, w_a3, w_b3, w_glu3):
    n_c = seq // S5_TC
    n_steps = batch * n_c
    const3 = lambda b, c: (0, 0, 0)
    const2 = lambda b, c: (0, 0)
    rows = np.arange(S5_TC)
    perm_np = np.zeros((S5_TC, S5_TC), np.float32)
    perm_np[rows, (rows % S5_NSEG) * S5_SEG + rows // S5_NSEG] = 1.0
    perm = jnp.asarray(perm_np, dtype=BF16)
    permt = jnp.asarray(perm_np.T, dtype=BF16)
    step = lambda b, c: b * n_c + c
    gab_rows = 2 * D_MODEL // n_steps
    gab_blk0 = TAIL_ROW0 // gab_rows
    gab_next0 = (TAIL_ROW0 + gab_rows) // GATE_ROWS
    cast_rows = [w.shape[1] // n_steps for w in (w_out3, w_a3, w_b3, w_glu3)]
    cast_in = [pl.BlockSpec((None, gab_rows, D_MODEL), lambda b, c: (0, gab_blk0 + step(b, c), 0)),
               pl.BlockSpec((None, GATE_ROWS, D_MODEL),
                            lambda b, c: (0, gab_next0 + (gab_rows // GATE_ROWS) * step(b, c), 0))]
    cast_in += [pl.BlockSpec((None, nr, w.shape[2]), lambda b, c: (0, step(b, c), 0))
                for nr, w in zip(cast_rows, (w_out3, w_a3, w_b3, w_glu3))]
    cast_out = [pl.BlockSpec((gab_rows, D_MODEL), lambda b, c: (step(b, c), 0))]
    cast_out += [pl.BlockSpec((nr, w.shape[2]), lambda b, c: (step(b, c), 0))
                 for nr, w in zip(cast_rows, (w_out3, w_a3, w_b3, w_glu3))]
    cast_shapes = [jax.ShapeDtypeStruct((2 * D_MODEL, D_MODEL), BF16)]
    cast_shapes += [jax.ShapeDtypeStruct(w.shape[1:], BF16) for w in (w_out3, w_a3, w_b3, w_glu3)]
    return pl.pallas_call(
        _s5_prompt_kernel,
        grid=(batch, n_c),
        in_specs=[pl.BlockSpec((S5_TC, SSM_WIDTH), lambda b, c: (b * n_c + c, u_col_block)),
                  pl.BlockSpec((N_META, SSM_WIDTH), const2),
                  pl.BlockSpec((S5_TC, S5_TC), const2),
                  pl.BlockSpec((S5_TC, S5_TC), const2),
                  pl.BlockSpec((SSM_JBLK, 256, 256), const3),
                  pl.BlockSpec((SSM_JBLK, 1024, 256), const3),
                  pl.BlockSpec((1, SSM_FLAT), const2),
                  pl.BlockSpec((1, SSM_FLAT), const2),
                  pl.BlockSpec((1, SSM_WIDTH), const2)] + cast_in,
        out_specs=[pl.BlockSpec((S5_TC, SSM_WIDTH), lambda b, c: (b * n_c + c, 0)),
                   pl.BlockSpec((1, 1, SSM_FLAT), lambda b, c: (b, 0, 0)),
                   pl.BlockSpec((1, 1, SSM_FLAT), lambda b, c: (b, 0, 0))] + cast_out,
        out_shape=[jax.ShapeDtypeStruct((batch * seq, SSM_WIDTH), F32),
                   jax.ShapeDtypeStruct((batch, 1, SSM_FLAT), F32),
                   jax.ShapeDtypeStruct((batch, 1, SSM_FLAT), F32)] + cast_shapes,
        scratch_shapes=[pltpu.VMEM((S5_TC, SSM_FLAT), F32), pltpu.VMEM((S5_TC, SSM_FLAT), F32),
                        pltpu.VMEM((S5_TC, SSM_FLAT), F32), pltpu.VMEM((S5_TC, SSM_FLAT), F32),
                        pltpu.VMEM((1, SSM_FLAT), F32), pltpu.VMEM((1, SSM_FLAT), F32),
                        pltpu.VMEM((S5_NSEG, SSM_FLAT), F32), pltpu.VMEM((S5_NSEG, SSM_FLAT), F32)]
                       + _s5_weight_scratch(),
        compiler_params=_cparams(2, 48),
        name="s5_prompt",
    )(u_arr, u_meta, perm, permt, bt, ct, abar_re, abar_im, d_skip,
      w_in_t, w_in_t, w_out3, w_a3, w_b3, w_glu3)


def _s5_sample_kernel(u_ref, h0re_ref, h0im_ref, bt_ref, ct_ref, are_ref, aim_ref, d_ref,
                      y_ref, hre_ref, him_ref, b_ref, cre_ref, cimn_ref):
    rows = u_ref.shape[0]
    _s5_expand_weights(bt_ref, ct_ref, b_ref, cre_ref, cimn_ref)
    u = u_ref[...]
    _s5_project_in(u.astype(BF16), b_ref, hre_ref, him_ref, rows)
    nr, ni = _cmul_add(are_ref[...], aim_ref[...], h0re_ref[...], h0im_ref[...], hre_ref[...], him_ref[...])
    hre_ref[...] = nr
    him_ref[...] = ni
    y_ref[...] = _s5_project_out(hre_ref, him_ref, cre_ref, cimn_ref, rows) + d_ref[...] * u


def _s5_sample(u, h0_re, h0_im, abar_re, abar_im, bt, ct, d_skip):
    rows = u.shape[0]
    return pl.pallas_call(
        _s5_sample_kernel,
        out_shape=[jax.ShapeDtypeStruct((rows, SSM_WIDTH), F32),
                   jax.ShapeDtypeStruct((rows, SSM_FLAT), F32),
                   jax.ShapeDtypeStruct((rows, SSM_FLAT), F32)],
        scratch_shapes=_s5_weight_scratch(),
        compiler_params=_cparams(0, 48),
        name="s5_sample",
    )(u, h0_re, h0_im, bt, ct, abar_re, abar_im, d_skip)


def _split3(x):
    hi = x.astype(BF16)
    r1 = x - hi.astype(F32)
    mid = r1.astype(BF16)
    lo = (r1 - mid.astype(F32)).astype(BF16)
    return hi, mid, lo


def _tri(n, lower):
    r = lax.broadcasted_iota(jnp.int32, (n, n), 0)
    c = lax.broadcasted_iota(jnp.int32, (n, n), 1)
    return (r >= c) if lower else (r <= c)


def _gate_columns(gcol, bias_row):
    n = gcol.shape[0]
    g = gcol + bias_row
    fl = _log_sigmoid(g[:, HEADS:])
    tril = jnp.where(_tri(n, True), 1.0, 0.0).astype(BF16)
    b = sum(jnp.dot(tril, p, preferred_element_type=F32) for p in _split3(fl))
    return g[:, :HEADS], b


def _gate_rows(grow, bias_col):
    n = grow.shape[1]
    g = grow + bias_col
    fl = _log_sigmoid(g[HEADS:, :])
    triu = jnp.where(_tri(n, False), 1.0, 0.0).astype(BF16)
    b = sum(jnp.dot(p, triu, preferred_element_type=F32) for p in _split3(fl))
    return g[:HEADS, :], b


MLSTM_GROUP = 2


def _eye_bf16(n):
    return jnp.where(_tri(n, True) & _tri(n, False), 1.0, 0.0).astype(BF16)


def _mlstm_state_update(k_bf, v, ig_col, b_col, c_old, n_old, m_old, k_t=None, gate_rows=None):
    n_rows = k_bf.shape[0]
    k_scale = DK ** -0.5
    b_last = b_col[n_rows - 1:n_rows, :]
    e_col = b_last - b_col + ig_col
    m_new = jnp.maximum(b_last + m_old, jnp.max(e_col, axis=0, keepdims=True))
    w_end = jnp.exp(e_col - m_new) * k_scale
    carry = jnp.exp(b_last + m_old - m_new)
    if k_t is None:
        k_t = lax.dot_general(_eye_bf16(DK), k_bf, NT_DIMS, preferred_element_type=F32)
    if gate_rows is None:
        kv = jnp.dot(k_t.astype(BF16), (v.astype(F32) * w_end).astype(BF16), preferred_element_type=F32)
    else:
        ig_row, b_row = gate_rows
        w_end_row = jnp.exp(b_last - b_row + ig_row - m_new) * k_scale
        kv = jnp.dot((k_t * w_end_row).astype(BF16), v, preferred_element_type=F32)
    c_new = carry * c_old + kv
    n_new = carry * n_old + jnp.sum(k_bf.astype(F32) * w_end, axis=0, keepdims=True)
    return c_new, n_new, m_new


def _mlstm_decay(ig_row, b_row, b_col, m_old):
    n_rows = b_col.shape[0]
    dmat = jnp.where(_tri(n_rows, True), b_col + (ig_row - b_row), -jnp.inf)
    inter = b_col + m_old
    m_t = jnp.maximum(inter, jnp.max(dmat, axis=-1, keepdims=True))
    return jnp.exp(dmat - m_t), jnp.exp(inter - m_t), m_t


def _head_norm_gate(h, o, gain):
    mu = jnp.mean(h, axis=-1, keepdims=True)
    hc = h - mu
    var = jnp.mean(hc * hc, axis=-1, keepdims=True)
    return _sigmoid(o) * (hc * lax.rsqrt(var + EPS) * gain)


def _mlstm_prompt_kernel(batch, q_ref, k_ref, v_ref, o_ref, gc_ref, *rest):
    gr_refs = rest[:batch]
    km_ref, vm_ref, gcm_ref, brow_ref, bcol_ref, gain_ref, y_ref, cf_ref, nf_ref, mf_ref = rest[batch:]
    k_scale = DK ** -0.5

    @pl.when(pl.program_id(0) == 0)
    def _():
        ig_c, b_c = _gate_columns(gcm_ref[...], brow_ref[...])
        for h in range(HEADS):
            hs = slice(h * DK, (h + 1) * DK)
            c_new, n_new, m_new = _mlstm_state_update(
                km_ref[:, hs], vm_ref[:, hs], ig_c[:, h:h + 1], b_c[:, h:h + 1],
                jnp.zeros((DK, DV), F32), jnp.zeros((1, DK), F32), jnp.zeros((1, 1), F32))
            for b in range(batch):
                cf_ref[b, h] = c_new
                nf_ref[b, h] = n_new
                mf_ref[b, h:h + 1, :] = jnp.broadcast_to(m_new, (1, 128))

    eye = _eye_bf16(DK)
    for b0 in range(0, batch, MLSTM_GROUP):
        gates = {}
        for b in range(b0, b0 + MLSTM_GROUP):
            gates[b] = _gate_columns(gc_ref[b], brow_ref[...]) + _gate_rows(gr_refs[b][...], bcol_ref[...])
        chains = [(b, h) for b in range(b0, b0 + MLSTM_GROUP) for h in range(HEADS)]

        ops = []
        for b, h in chains:
            hs = slice(h * DK, (h + 1) * DK)
            q = q_ref[b, :, hs]
            k_bf = k_ref[b, :, hs]
            c_old = cf_ref[b, h]
            ops.append(dict(
                q=q, k_bf=k_bf, v=v_ref[b, :, hs], c_old=c_old, n_old=nf_ref[b, h], m_old=mf_ref[b, h:h + 1, 0:1],
                s_raw=lax.dot_general(q, k_bf, NT_DIMS, preferred_element_type=F32),
                qc=jnp.dot(q, c_old.astype(BF16), preferred_element_type=F32),
                k_t=lax.dot_general(eye, k_bf, NT_DIMS, preferred_element_type=F32)))

        for (b, h), c in zip(chains, ops):
            ig_c, b_c, ig_r, b_r = gates[b]
            c["w_intra"], c["w_inter"], c["m_t"] = _mlstm_decay(ig_r[h:h + 1, :], b_r[h:h + 1, :], b_c[:, h:h + 1],
                                                                c["m_old"])

        for (b, h), c in zip(chains, ops):
            hs = slice(h * DK, (h + 1) * DK)
            s = c["s_raw"] * (c["w_intra"] * k_scale)
            num = c["w_inter"] * c["qc"] + jnp.dot(s.astype(BF16), c["v"], preferred_element_type=F32)
            den = (c["w_inter"] * jnp.sum(c["q"].astype(F32) * c["n_old"], axis=-1, keepdims=True)
                   + jnp.sum(s, axis=-1, keepdims=True))
            hid = num / jnp.maximum(jnp.abs(den), jnp.exp(-c["m_t"]))
            y_ref[b, :, hs] = _head_norm_gate(hid, o_ref[b, :, hs], gain_ref[:, hs]).astype(y_ref.dtype)

        for (b, h), c in zip(chains, ops):
            ig_c, b_c, ig_r, b_r = gates[b]
            c_new, n_new, m_new = _mlstm_state_update(c["k_bf"], c["v"], ig_c[:, h:h + 1], b_c[:, h:h + 1],
                                                      c["c_old"], c["n_old"], c["m_old"], k_t=c["k_t"],
                                                      gate_rows=(ig_r[h:h + 1, :], b_r[h:h + 1, :]))
            cf_ref[b, h] = c_new
            nf_ref[b, h] = n_new
            mf_ref[b, h:h + 1, :] = jnp.broadcast_to(m_new, (1, 128))


def _mlstm_prompt(qkv, o_arr, o_col_block, gcol, grow, qkv_meta, gcol_meta, bias_row, bias_col, gain, batch, seq):
    n_c = seq // CHUNK
    width = HEADS * DK
    const2 = lambda c: (0, 0)
    qkv3 = qkv.reshape(batch, seq, 3 * width)
    o3 = o_arr.reshape(batch, seq, o_arr.shape[1])
    gcol3 = gcol.reshape(gcol.shape[0] // seq, seq, 2 * HEADS)
    state0 = lambda c: (0, 0, 0, 0)
    return pl.pallas_call(
        functools.partial(_mlstm_prompt_kernel, batch),
        grid=(n_c,),
        in_specs=[pl.BlockSpec((batch, CHUNK, width), lambda c: (0, c, 0)),
                  pl.BlockSpec((batch, CHUNK, width), lambda c: (0, c, 1)),
                  pl.BlockSpec((batch, CHUNK, width), lambda c: (0, c, 2)),
                  pl.BlockSpec((batch, CHUNK, width), lambda c: (0, c, o_col_block)),
                  pl.BlockSpec((batch, CHUNK, 2 * HEADS), lambda c: (0, c, 0))]
                 + [pl.BlockSpec((2 * HEADS, CHUNK), functools.partial(lambda b, c: (0, b * n_c + c), b))
                    for b in range(batch)]
                 + [pl.BlockSpec((N_META, width), lambda c: (0, 1)),
                    pl.BlockSpec((N_META, width), lambda c: (0, 2)),
                    pl.BlockSpec((N_META, 2 * HEADS), const2),
                    pl.BlockSpec((1, 2 * HEADS), const2),
                    pl.BlockSpec((2 * HEADS, 1), const2),
                    pl.BlockSpec((1, width), const2)],
        out_specs=[pl.BlockSpec((batch, CHUNK, width), lambda c: (0, c, 0)),
                   pl.BlockSpec((batch, HEADS, DK, DV), state0),
                   pl.BlockSpec((batch, HEADS, 1, DK), state0),
                   pl.BlockSpec((batch, HEADS, 128), lambda c: (0, 0, 0))],
        out_shape=[jax.ShapeDtypeStruct((batch, seq, width), BF16),
                   jax.ShapeDtypeStruct((batch, HEADS, DK, DV), F32),
                   jax.ShapeDtypeStruct((batch, HEADS, 1, DK), F32),
                   jax.ShapeDtypeStruct((batch, HEADS, 128), F32)],
        compiler_params=_cparams(1, 40),
        name="mlstm_prompt",
    )(qkv3, qkv3, qkv3, o3, gcol3, *([grow] * batch), qkv_meta, qkv_meta, gcol_meta, bias_row, bias_col, gain)


MLSTM_SB = 8


def _mlstm_sample_kernel(qkv_ref, o_ref, g_ref, c_ref, n_ref, m_ref, brow_ref, gain_ref,
                         y_ref, co_ref, no_ref, mo_ref):
    sb = MLSTM_SB
    k_scale = DK ** -0.5
    eye = _eye_bf16(DK)
    width = HEADS * DK
    pad = 2 * sb
    row_id = lax.broadcasted_iota(jnp.int32, (pad, DV), 0)

    g_all = g_ref[...] + brow_ref[...]
    ig_all = g_all[:, :HEADS]
    fl_all = _log_sigmoid(g_all[:, HEADS:])
    m_old_all = m_ref[...]
    m_new_all = jnp.maximum(fl_all + m_old_all, ig_all)
    w_in_all = jnp.exp(ig_all - m_new_all)
    w_ca_all = jnp.exp(fl_all + m_old_all - m_new_all)
    floor_all = jnp.exp(-m_new_all)
    mo_ref[...] = m_new_all

    for h in range(HEADS):
        hs = slice(h * DK, (h + 1) * DK)
        qk_rows = jnp.concatenate([qkv_ref[:, hs], qkv_ref[:, width + h * DK:width + (h + 1) * DK] * k_scale],
                                  axis=0).astype(BF16)
        qk_cols = lax.dot_general(eye, qk_rows, NT_DIMS, preferred_element_type=F32).astype(BF16)
        for i in range(sb):
            w_in = w_in_all[i:i + 1, h:h + 1]
            w_ca = w_ca_all[i:i + 1, h:h + 1]
            q = qk_rows[i:i + 1, :]
            qf = q.astype(F32)
            k = qk_rows[sb + i:sb + i + 1, :].astype(F32)
            v = qkv_ref[i:i + 1, 2 * width + h * DK:2 * width + (h + 1) * DK].astype(BF16).astype(F32)
            c_old = c_ref[i, h]
            n_old = n_ref[i, h]
            qc = jnp.dot(jnp.broadcast_to(q, (pad, DK)), c_old.astype(BF16), preferred_element_type=F32)[0:1, :]
            v_rows = jnp.where(row_id == sb + i, jnp.broadcast_to(v * w_in, (pad, DV)), 0.0).astype(BF16)
            kv = jnp.dot(qk_cols, v_rows, preferred_element_type=F32)
            s = jnp.sum(qf * k, axis=-1, keepdims=True) * w_in
            num = w_ca * qc + s * v
            den = w_ca * jnp.sum(qf * n_old, axis=-1, keepdims=True) + s
            hid = num / jnp.maximum(jnp.abs(den), floor_all[i:i + 1, h:h + 1])
            co_ref[i, h] = w_ca * c_old + kv
            no_ref[i, h] = w_ca * n_old + k * w_in
            y_ref[i:i + 1, hs] = _head_norm_gate(hid, o_ref[i:i + 1, hs], gain_ref[:, hs])


def _mlstm_sample(qkv, o_arr, gcol, c0, n0, m0, bias_row, gain):
    nb = qkv.shape[0]
    width = HEADS * DK
    sb = MLSTM_SB
    const2 = lambda i: (0, 0)
    return pl.pallas_call(
        _mlstm_sample_kernel,
        grid=(nb // sb,),
        in_specs=[pl.BlockSpec((sb, 3 * width), lambda i: (i, 0)),
                  pl.BlockSpec((sb, width), lambda i: (i, 0)),
                  pl.BlockSpec((sb, 2 * HEADS), lambda i: (i, 0)),
                  pl.BlockSpec((sb, HEADS, DK, DV), lambda i: (i, 0, 0, 0)),
                  pl.BlockSpec((sb, HEADS, 1, DK), lambda i: (i, 0, 0, 0)),
                  pl.BlockSpec((sb, HEADS), lambda i: (i, 0)),
                  pl.BlockSpec((1, 2 * HEADS), const2),
                  pl.BlockSpec((1, width), const2)],
        out_specs=[pl.BlockSpec((sb, width), lambda i: (i, 0)),
                   pl.BlockSpec((sb, HEADS, DK, DV), lambda i: (i, 0, 0, 0)),
                   pl.BlockSpec((sb, HEADS, 1, DK), lambda i: (i, 0, 0, 0)),
                   pl.BlockSpec((sb, HEADS), lambda i: (i, 0))],
        out_shape=[jax.ShapeDtypeStruct((nb, width), F32),
                   jax.ShapeDtypeStruct((nb, HEADS, DK, DV), F32),
                   jax.ShapeDtypeStruct((nb, HEADS, 1, DK), F32),
                   jax.ShapeDtypeStruct((nb, HEADS), F32)],
        compiler_params=_cparams(1, 48),
        name="mlstm_sample",
    )(qkv, o_arr, gcol, c0, n0, m0, bias_row, gain)


def _merge_rows(ya, yb, x, wgab_ref, wglu_ref, bglu_ref, wa_ref, wb_ref, wout_ref, g1_ref, b1_ref):
    xb = x.astype(BF16)
    g = _gelu_tanh(ya)
    z = jnp.dot(g.astype(BF16), wglu_ref[...], preferred_element_type=F32) + bglu_ref[...]
    out_a = g * _sigmoid(z)
    up_a = jnp.dot(out_a.astype(BF16), wa_ref[...], preferred_element_type=F32)
    g_a = lax.dot_general(xb, wgab_ref[0:D_MODEL, :], NT_DIMS, preferred_element_type=F32)
    mix = _sigmoid(g_a) * up_a
    up_b = jnp.dot(yb, wb_ref[...], preferred_element_type=F32)
    g_b = lax.dot_general(xb, wgab_ref[D_MODEL:2 * D_MODEL, :], NT_DIMS, preferred_element_type=F32)
    mix = mix + _sigmoid(g_b) * up_b
    mo = jnp.dot(mix.astype(BF16), wout_ref[...], preferred_element_type=F32)
    return _layernorm_rows(ALPHA * x + mo, g1_ref[...], b1_ref[...])


def _merge_kernel(ya_ref, yb_ref, x_ref, yas_ref, ybs_ref, xs_ref, *rest):
    weights, (o_ref, os_ref, ob_ref, osb_ref) = rest[:-4], rest[-4:]
    x1 = _merge_rows(ya_ref[...], yb_ref[...], x_ref[...], *weights)
    o_ref[...] = x1
    ob_ref[...] = x1.astype(BF16)

    @pl.when(pl.program_id(0) == pl.num_programs(0) - 1)
    def _():
        x1s = _merge_rows(yas_ref[...], ybs_ref[...].astype(BF16), xs_ref[...], *weights)
        os_ref[...] = x1s
        osb_ref[...] = x1s.astype(BF16)


def _merge(ya, yb, x, ya_s, yb_s, x_s, wt_gab, w_glu, b_glu, w_a_up, w_b_up, w_out, ln_g, ln_b, tm):
    r = x.shape[0]
    rs = x_s.shape[0]
    const2 = lambda i: (0, 0)
    resident = functools.partial(pl.BlockSpec, index_map=const2, pipeline_mode=pl.Buffered(1))
    return pl.pallas_call(
        _merge_kernel,
        grid=(r // tm,),
        in_specs=[pl.BlockSpec((tm, SSM_WIDTH), lambda i: (i, 0)),
                  pl.BlockSpec((tm, HEADS * DV), lambda i: (i, 0)),
                  pl.BlockSpec((tm, D_MODEL), lambda i: (i, 0)),
                  resident((rs, SSM_WIDTH)),
                  resident((rs, HEADS * DV)),
                  resident((rs, D_MODEL)),
                  resident((2 * D_MODEL, D_MODEL)),
                  resident((SSM_WIDTH, SSM_WIDTH)),
                  resident((1, SSM_WIDTH)),
                  resident((SSM_WIDTH, D_MODEL)),
                  resident((HEADS * DV, D_MODEL)),
                  resident((D_MODEL, D_MODEL)),
                  resident((1, D_MODEL)),
                  resident((1, D_MODEL))],
        out_specs=[pl.BlockSpec((tm, D_MODEL), lambda i: (i, 0)),
                   pl.BlockSpec((rs, D_MODEL), const2),
                   pl.BlockSpec((tm, D_MODEL), lambda i: (i, 0)),
                   pl.BlockSpec((rs, D_MODEL), const2)],
        out_shape=[jax.ShapeDtypeStruct((r, D_MODEL), F32), jax.ShapeDtypeStruct((rs, D_MODEL), F32),
                   jax.ShapeDtypeStruct((r, D_MODEL), BF16), jax.ShapeDtypeStruct((rs, D_MODEL), BF16)],
        compiler_params=_cparams(1, 59),
        name="merge_ln1",
    )(ya, yb, x, ya_s, yb_s, x_s, wt_gab, w_glu, b_glu, w_a_up, w_b_up, w_out, ln_g, ln_b)


def _swiglu(xb, wg, wu):
    hg = jnp.dot(xb, wg, preferred_element_type=F32)
    hu = jnp.dot(xb, wu, preferred_element_type=F32)
    return ((hg * _sigmoid(hg)) * hu).astype(BF16)


def _ffn_up_kernel(x_ref, xs_ref, wg_ref, wu_ref, wd_ref, h_ref, hs_ref, wdb_ref, wgb_s, wub_s):
    i = pl.program_id(1)

    @pl.when(i == 0)
    def _():
        wgb_s[...] = wg_ref[...].astype(BF16)
        wub_s[...] = wu_ref[...].astype(BF16)

    h_ref[...] = _swiglu(x_ref[...], wgb_s[...], wub_s[...])
    wdb_ref[...] = wd_ref[...].astype(BF16)

    @pl.when(i == pl.num_programs(1) - 1)
    def _():
        hs_ref[...] = _swiglu(xs_ref[...], wgb_s[...], wub_s[...])


def _ffn_up(xb, xsb, w_gate3, w_up3, w_down3, tm, tf):
    r = xb.shape[0]
    rs = xsb.shape[0]
    n_f, n_i = D_FF // tf, r // tm
    wd_rows = D_FF // (n_f * n_i)
    return pl.pallas_call(
        _ffn_up_kernel,
        grid=(n_f, n_i),
        in_specs=[pl.BlockSpec((tm, D_MODEL), lambda f, i: (i, 0)),
                  pl.BlockSpec((rs, D_MODEL), lambda f, i: (0, 0)),
                  pl.BlockSpec((None, D_MODEL, tf), lambda f, i: (0, 0, f)),
                  pl.BlockSpec((None, D_MODEL, tf), lambda f, i: (0, 0, f)),
                  pl.BlockSpec((None, wd_rows, D_MODEL), lambda f, i: (0, f * n_i + i, 0))],
        out_specs=[pl.BlockSpec((tm, tf), lambda f, i: (i, f)),
                   pl.BlockSpec((rs, tf), lambda f, i: (0, f)),
                   pl.BlockSpec((wd_rows, D_MODEL), lambda f, i: (f * n_i + i, 0))],
        out_shape=[jax.ShapeDtypeStruct((r, D_FF), BF16), jax.ShapeDtypeStruct((rs, D_FF), BF16),
                   jax.ShapeDtypeStruct((D_FF, D_MODEL), BF16)],
        scratch_shapes=[pltpu.VMEM((D_MODEL, tf), BF16), pltpu.VMEM((D_MODEL, tf), BF16)],
        compiler_params=_cparams(2, 58),
        name="ffn_up",
    )(xb, xsb, w_gate3, w_up3, w_down3)


def _ffn_down_rows(h, x, wd_ref, g2_ref, b2_ref):
    return _layernorm_rows(ALPHA * x + jnp.dot(h, wd_ref[...], preferred_element_type=F32), g2_ref[...], b2_ref[...])


def _ffn_down_kernel(h_ref, x_ref, hs_ref, xs_ref, wd_ref, g2_ref, b2_ref, o_ref, os_ref):
    o_ref[...] = _ffn_down_rows(h_ref[...], x_ref[...], wd_ref, g2_ref, b2_ref)

    @pl.when(pl.program_id(0) == pl.num_programs(0) - 1)
    def _():
        os_ref[...] = _ffn_down_rows(hs_ref[...], xs_ref[...], wd_ref, g2_ref, b2_ref)


def _ffn_down(h, x, h_s, x_s, wd_b, ln_g, ln_b, tm):
    r = x.shape[0]
    rs = x_s.shape[0]
    const2 = lambda i: (0, 0)
    resident = functools.partial(pl.BlockSpec, index_map=const2, pipeline_mode=pl.Buffered(1))
    return pl.pallas_call(
        _ffn_down_kernel,
        grid=(r // tm,),
        in_specs=[pl.BlockSpec((tm, D_FF), lambda i: (i, 0)),
                  pl.BlockSpec((tm, D_MODEL), lambda i: (i, 0)),
                  resident((rs, D_FF)),
                  resident((rs, D_MODEL)),
                  resident((D_FF, D_MODEL)),
                  resident((1, D_MODEL)),
                  resident((1, D_MODEL))],
        out_specs=[pl.BlockSpec((tm, D_MODEL), lambda i: (i, 0)),
                   pl.BlockSpec((rs, D_MODEL), const2)],
        out_shape=[jax.ShapeDtypeStruct((r, D_MODEL), F32), jax.ShapeDtypeStruct((rs, D_MODEL), F32)],
        compiler_params=_cparams(1, 56),
        name="ffn_down_ln2",
    )(h, x, h_s, x_s, wd_b, ln_g, ln_b)


def kernel(x_prompt, x_sample, state_ssm_re, state_ssm_im, state_mlstm_c, state_mlstm_n, state_mlstm_m,
           meta_tokens, w_in, b_if, ssm_a_re, ssm_a_im, ssm_log_dt, ssm_b_re, ssm_b_im, ssm_c_re, ssm_c_im,
           ssm_d, w_glu, b_glu, w_a_up, mh_gain, w_b_up, w_out, ln1_g, ln1_b, w_gate, w_up, w_down,
           ln2_g, ln2_b):
    batch, seq, _ = x_prompt.shape
    nb = x_sample.shape[0]
    width = HEADS * DK

    bias_row = b_if[0].reshape(1, 2 * HEADS)
    bias_col = b_if[0].reshape(2 * HEADS, 1)
    gain = mh_gain[0].reshape(1, width)
    d_skip = ssm_d[0].reshape(1, SSM_WIDTH)
    b_glu_r = b_glu[0].reshape(1, SSM_WIDTH)
    ln1 = (ln1_g[0].reshape(1, D_MODEL), ln1_b[0].reshape(1, D_MODEL))
    ln2 = (ln2_g[0].reshape(1, D_MODEL), ln2_b[0].reshape(1, D_MODEL))
    abar_re, abar_im, bt, ct = _s5_discretise(
        ssm_a_re[0], ssm_a_im[0], ssm_log_dt[0], ssm_b_re[0], ssm_b_im[0], ssm_c_re[0], ssm_c_im[0])

    xp = x_prompt.reshape(batch * seq, D_MODEL)
    xs = x_sample.reshape(nb, D_MODEL)
    xsm = jnp.concatenate([xs, meta_tokens], axis=0)
    w_in_t = jnp.swapaxes(w_in, 1, 2)
    uo_blocks, qkv_blocks = (0, 4, 2), (1, 1, 3)

    uo_p, uo_s, gcol_p, grow_p, gcol_s = _proj(xp, xsm, w_in_t, uo_blocks, F32, F32, 1024, "proj_uo", with_gates=True)
    qkv_p, qkv_s = _proj(xp, xsm, w_in_t, qkv_blocks, BF16, F32, 1024, "proj_qkv")

    u_meta = uo_s[nb:nb + N_META, :SSM_WIDTH]
    qkv_meta = qkv_s[nb:nb + N_META].astype(BF16)
    gcol_meta = gcol_s[nb:nb + N_META]

    ya_p, pf_re, pf_im, wt_gab, w_out_b, w_a_b, w_b_b, w_glu_b = _s5_prompt(
        uo_p, 0, u_meta, abar_re, abar_im, bt, ct, d_skip, batch, seq, w_in_t, w_out, w_a_up, w_b_up, w_glu)
    yb_p, pc, pn, pm = _mlstm_prompt(qkv_p, uo_p, 1, gcol_p, grow_p, qkv_meta, gcol_meta, bias_row, bias_col,
                                     gain, batch, seq)
    ya_s, sf_re, sf_im = _s5_sample(uo_s[:nb, :SSM_WIDTH], state_ssm_re[0].reshape(nb, SSM_FLAT),
                                    state_ssm_im[0].reshape(nb, SSM_FLAT), abar_re, abar_im, bt, ct, d_skip)
    yb_s, sc, sn, sm = _mlstm_sample(qkv_s[:nb], uo_s[:nb, SSM_WIDTH:], gcol_s[:nb], state_mlstm_c[0],
                                     state_mlstm_n[0].reshape(nb, HEADS, 1, DK), state_mlstm_m[0], bias_row, gain)

    x1_p, x1_s, x1b_p, x1b_s = _merge(ya_p, yb_p.reshape(batch * seq, width), xp, ya_s, yb_s, xs,
                                      wt_gab, w_glu_b, b_glu_r, w_a_b, w_b_b, w_out_b, *ln1, tm=256)
    h_p, h_s, w_down_b = _ffn_up(x1b_p, x1b_s, w_gate, w_up, w_down, tm=2048, tf=512)
    y_p, y_s = _ffn_down(h_p, x1_p, h_s, x1_s, w_down_b, *ln2, tm=256)

    return (y_p.reshape(batch, seq, D_MODEL),
            y_s.reshape(nb, 1, D_MODEL),
            pf_re.reshape(1, batch, SSM_GROUPS, SSM_STATE),
            pf_im.reshape(1, batch, SSM_GROUPS, SSM_STATE),
            pc.reshape(1, batch, HEADS, DK, DV),
            pn.reshape(1, batch, HEADS, DK),
            pm[:, :, 0].reshape(1, batch, HEADS),
            sf_re.reshape(1, nb, SSM_GROUPS, SSM_STATE),
            sf_im.reshape(1, nb, SSM_GROUPS, SSM_STATE),
            sc.reshape(1, nb, HEADS, DK, DV),
            sn.reshape(1, nb, HEADS, DK),
            sm.reshape(1, nb, HEADS))
```

```python
import functools

import jax
import jax.numpy as jnp
import numpy as np
from jax import lax
from jax.experimental import pallas as pl
from jax.experimental.pallas import tpu as pltpu

F32 = jnp.float32
BF16 = jnp.bfloat16

D_MODEL = 2048
N_META = 16
SSM_WIDTH = 1024
SSM_GROUP = 16
SSM_GROUPS = 64
SSM_STATE = 64
SSM_FLAT = SSM_GROUPS * SSM_STATE
SSM_JBLK = 4
HEADS = 4
DK = 256
DV = 256
CHUNK = 128
D_FF = 5632
EPS = 1e-5
ALPHA = 2.0 ** 0.25

S5_TC = 256
S5_NSEG = 8
S5_SEG = S5_TC // S5_NSEG
S5_LW = 512
S5_CPS = 2

V7X_SCOPED_VMEM_MAX_BYTES = 60000 * 1024


def _cparams(n_axes, vmem_mb):
    return pltpu.CompilerParams(
        dimension_semantics=("arbitrary",) * n_axes,
        vmem_limit_bytes=min(vmem_mb * 1024 * 1024, V7X_SCOPED_VMEM_MAX_BYTES),
    )


def _sigmoid(x):
    return 1.0 / (1.0 + jnp.exp(-x))


def _log_sigmoid(x):
    return jnp.minimum(x, 0.0) - jnp.log(1.0 + jnp.exp(-jnp.abs(x)))


def _gelu_tanh(x):
    c = 0.7978845608028654
    return 0.5 * x * (1.0 + jnp.tanh(c * (x + 0.044715 * (x * x * x))))


def _layernorm_rows(x, g, b):
    mu = jnp.mean(x, axis=-1, keepdims=True)
    xc = x - mu
    var = jnp.mean(xc * xc, axis=-1, keepdims=True)
    return xc * lax.rsqrt(var + EPS) * g + b


PROJ_TN = 1024
GATE_ROWS = 2 * HEADS
TAIL_ROW0 = SSM_WIDTH + 4 * HEADS * DK
NT_DIMS = (((1,), (1,)), ((), ()))


def _proj_kernel(with_gates, x_ref, xs_ref, wt_ref, *rest):
    if with_gates:
        wg_ref, o_ref, os_ref, g_ref, gt_ref, gs_ref, wb_s = rest
    else:
        o_ref, os_ref, wb_s = rest
    i = pl.program_id(1)

    @pl.when(i == 0)
    def _():
        wb_s[...] = wt_ref[...].astype(BF16)

    xb = x_ref[...].astype(BF16)
    o_ref[...] = lax.dot_general(xb, wb_s[...], NT_DIMS, preferred_element_type=F32).astype(o_ref.dtype)
    first_block = pl.program_id(0) == 0
    if with_gates:
        @pl.when(first_block)
        def _():
            wg = wg_ref[...].astype(BF16)
            g_ref[...] = lax.dot_general(xb, wg, NT_DIMS, preferred_element_type=F32)
            gt_ref[...] = lax.dot_general(wg, xb, NT_DIMS, preferred_element_type=F32)

    @pl.when(i == pl.num_programs(1) - 1)
    def _():
        xsb = xs_ref[...].astype(BF16)
        os_ref[...] = lax.dot_general(xsb, wb_s[...], NT_DIMS, preferred_element_type=F32).astype(os_ref.dtype)
        if with_gates:
            @pl.when(first_block)
            def _():
                gs_ref[...] = lax.dot_general(xsb, wg_ref[...].astype(BF16), NT_DIMS, preferred_element_type=F32)


def _proj(x, xs, wt3, blocks, out_dtype, small_dtype, tm, name, with_gates=False):
    r, k = x.shape
    rs = xs.shape[0]
    first, step, count = blocks
    tn = PROJ_TN
    nm = r // tm
    in_specs = [pl.BlockSpec((tm, k), lambda j, i: (i, 0)),
                pl.BlockSpec((rs, k), lambda j, i: (0, 0)),
                pl.BlockSpec((None, tn, k), lambda j, i: (0, first + step * j, 0))]
    out_specs = [pl.BlockSpec((tm, tn), lambda j, i: (i, j)),
                 pl.BlockSpec((rs, tn), lambda j, i: (0, j))]
    out_shape = [jax.ShapeDtypeStruct((r, count * tn), out_dtype),
                 jax.ShapeDtypeStruct((rs, count * tn), small_dtype)]
    args = (x, xs, wt3)
    if with_gates:
        ng = GATE_ROWS
        in_specs += [pl.BlockSpec((None, ng, k), lambda j, i: (0, TAIL_ROW0 // ng, 0))]
        gate_block = lambda j, i: jnp.where(j == 0, i, nm - 1)
        out_specs += [pl.BlockSpec((tm, ng), lambda j, i: (gate_block(j, i), 0)),
                      pl.BlockSpec((ng, tm), lambda j, i: (0, gate_block(j, i))),
                      pl.BlockSpec((rs, ng), lambda j, i: (0, 0))]
        out_shape += [jax.ShapeDtypeStruct((r, ng), F32), jax.ShapeDtypeStruct((ng, r), F32),
                      jax.ShapeDtypeStruct((rs, ng), F32)]
        args = (x, xs, wt3, wt3)
    return pl.pallas_call(
        functools.partial(_proj_kernel, with_gates),
        grid=(count, nm),
        in_specs=in_specs,
        out_specs=out_specs,
        out_shape=out_shape,
        scratch_shapes=[pltpu.VMEM((tn, k), BF16)],
        compiler_params=_cparams(2, 56),
        name=name,
    )(*args)


def _s5_discretise(a_re, a_im, log_dt, b_re, b_im, c_re, c_im):
    dt = jnp.exp(log_dt)
    e = jnp.exp(a_re * dt)
    abar_re = e * jnp.cos(a_im * dt)
    abar_im = e * jnp.sin(a_im * dt)
    nr = abar_re - 1.0
    ni = abar_im
    den = a_re * a_re + a_im * a_im
    coef_re = (nr * a_re + ni * a_im) / den
    coef_im = (ni * a_re - nr * a_im) / den
    bb_re = coef_re[..., None] * b_re - coef_im[..., None] * b_im
    bb_im = coef_re[..., None] * b_im + coef_im[..., None] * b_re
    gpb = SSM_GROUPS // SSM_JBLK

    def b_rows(bb):
        return bb.reshape(SSM_JBLK, gpb, SSM_STATE, SSM_GROUP).transpose(0, 1, 3, 2).reshape(SSM_JBLK, 256, SSM_STATE)

    def c_rows(cc):
        return cc.reshape(SSM_JBLK, gpb, SSM_GROUP, SSM_STATE).transpose(0, 1, 3, 2).reshape(SSM_JBLK, 1024, SSM_GROUP)

    bt = jnp.concatenate([jnp.tile(b_rows(bb_re), (1, 1, 2)), jnp.tile(b_rows(bb_im), (1, 1, 2))], axis=-1)
    ct = jnp.concatenate([jnp.tile(c_rows(c_re), (1, 1, 8)), jnp.tile(c_rows(-c_im), (1, 1, 8))], axis=-1)
    return abar_re.reshape(1, SSM_FLAT), abar_im.reshape(1, SSM_FLAT), bt.astype(BF16), ct.astype(BF16)


def _s5_expand_weights(bt_ref, ct_ref, bw_s, cwr_s, cwi_s):
    rb = lax.broadcasted_iota(jnp.int32, (256, 2048), 0)
    cb = lax.broadcasted_iota(jnp.int32, (256, 2048), 1)
    bmask = (rb >> 4) == ((cb & 1023) >> 6)
    rc = lax.broadcasted_iota(jnp.int32, (1024, 256), 0)
    cc = lax.broadcasted_iota(jnp.int32, (1024, 256), 1)
    cmask = (rc >> 6) == (cc >> 4)
    for j in range(SSM_JBLK):
        bt = bt_ref[j].astype(F32)
        wide = jnp.concatenate([bt[:, :128]] * 8 + [bt[:, 128:]] * 8, axis=1)
        bw_s[j] = jnp.where(bmask, wide, 0.0).astype(BF16)
        ct = ct_ref[j].astype(F32)
        cwr_s[j] = jnp.where(cmask, jnp.concatenate([ct[:, :128]] * 2, axis=1), 0.0).astype(BF16)
        cwi_s[j] = jnp.where(cmask, jnp.concatenate([ct[:, 128:]] * 2, axis=1), 0.0).astype(BF16)


def _cmul_add(ar, ai, hr, hi, xr, xi):
    return ar * hr - ai * hi + xr, ar * hi + ai * hr + xi


def _s5_project_in(u, b_ref, hre_ref, him_ref, rows):
    for j in range(SSM_JBLK):
        bu = jnp.dot(u[:, j * 256:(j + 1) * 256], b_ref[j], preferred_element_type=F32)
        hre_ref[0:rows, j * 1024:(j + 1) * 1024] = bu[:, :1024]
        him_ref[0:rows, j * 1024:(j + 1) * 1024] = bu[:, 1024:]


def _s5_project_out(hre_ref, him_ref, cre_ref, cimn_ref, rows):
    ys = []
    for j in range(SSM_JBLK):
        hr = hre_ref[0:rows, j * 1024:(j + 1) * 1024].astype(BF16)
        hi = him_ref[0:rows, j * 1024:(j + 1) * 1024].astype(BF16)
        y = jnp.dot(hr, cre_ref[j], preferred_element_type=F32)
        ys.append(y + jnp.dot(hi, cimn_ref[j], preferred_element_type=F32))
    return jnp.concatenate(ys, axis=1)


def _s5_scan_chunk(hre_ref, him_ref, hsre_ref, hsim_ref, pre_ref, pim_ref, are_ref, aim_ref, cre_s, cim_s):
    full = (S5_NSEG, S5_LW)
    for lg in range(SSM_FLAT // S5_LW):
        lanes = slice(lg * S5_LW, (lg + 1) * S5_LW)
        ar = jnp.broadcast_to(are_ref[:, lanes], full)
        ai = jnp.broadcast_to(aim_ref[:, lanes], full)

        hr = jnp.zeros(full, F32)
        hi = jnp.zeros(full, F32)
        for k in range(S5_SEG):
            rows = slice(k * S5_NSEG, (k + 1) * S5_NSEG)
            hr, hi = _cmul_add(ar, ai, hr, hi, hre_ref[rows, lanes], him_ref[rows, lanes])
            hre_ref[rows, lanes] = hr
            him_ref[rows, lanes] = hi
        er, ei = hr, hi

        asr = pre_ref[S5_TC - 1:S5_TC, lanes]
        asi = pim_ref[S5_TC - 1:S5_TC, lanes]
        hr = cre_s[:, lanes]
        hi = cim_s[:, lanes]
        for j in range(S5_NSEG):
            hsre_ref[j:j + 1, lanes] = hr
            hsim_ref[j:j + 1, lanes] = hi
            hr, hi = _cmul_add(asr, asi, hr, hi, er[j:j + 1, :], ei[j:j + 1, :])
        cre_s[:, lanes] = hr
        cim_s[:, lanes] = hi

        sr = hsre_ref[:, lanes]
        si = hsim_ref[:, lanes]
        for k in range(S5_SEG):
            rows = slice(k * S5_NSEG, (k + 1) * S5_NSEG)
            pr = pre_ref[rows, lanes]
            pi = pim_ref[rows, lanes]
            hre_ref[rows, lanes] = hre_ref[rows, lanes] + (pr * sr - pi * si)
            him_ref[rows, lanes] = him_ref[rows, lanes] + (pr * si + pi * sr)


N_CAST_IN = 6
N_CAST_OUT = 5


def _weight_cast_specs(n_steps, step_of, w_in_t, others):
    gab_rows = 2 * D_MODEL // n_steps
    gab_blk0 = TAIL_ROW0 // gab_rows
    gab_next0 = (TAIL_ROW0 + gab_rows) // GATE_ROWS
    rows = [w.shape[1] // n_steps for w in others]
    in_specs = [pl.BlockSpec((None, gab_rows, D_MODEL), lambda *g: (0, gab_blk0 + step_of(*g), 0)),
                pl.BlockSpec((None, GATE_ROWS, D_MODEL),
                             lambda *g: (0, gab_next0 + (gab_rows // GATE_ROWS) * step_of(*g), 0))]
    in_specs += [pl.BlockSpec((None, nr, w.shape[2]), lambda *g: (0, step_of(*g), 0)) for nr, w in zip(rows, others)]
    out_specs = [pl.BlockSpec((gab_rows, D_MODEL), lambda *g: (step_of(*g), 0))]
    out_specs += [pl.BlockSpec((nr, w.shape[2]), lambda *g: (step_of(*g), 0)) for nr, w in zip(rows, others)]
    out_shapes = [jax.ShapeDtypeStruct((2 * D_MODEL, D_MODEL), BF16)]
    out_shapes += [jax.ShapeDtypeStruct(w.shape[1:], BF16) for w in others]
    return in_specs, out_specs, out_shapes, (w_in_t, w_in_t) + tuple(others)


def _weight_cast_step(in_refs, out_refs):
    wga_ref, wgn_ref = in_refs[:2]
    out_refs[0][...] = jnp.concatenate([wga_ref[GATE_ROWS:, :], wgn_ref[...]], axis=0).astype(BF16)
    for src, dst in zip(in_refs[2:], out_refs[1:]):
        dst[...] = src[...].astype(BF16)


def _s5_prompt_kernel(u_ref, um_ref, perm_ref, permt_ref, bt_ref, ct_ref, are_ref, aim_ref, d_ref,
                      y_ref, fre_ref, fim_ref,
                      hre_ref, him_ref, hre2_ref, him2_ref, pre_ref, pim_ref, cre_s, cim_s,
                      hsre_ref, hsim_ref, hsre2_ref, hsim2_ref,
                      b_ref, cre_ref, cimn_ref):
    b_id = pl.program_id(0)
    c_id = pl.program_id(1)
    n_c = pl.num_programs(1)

    @pl.when(jnp.logical_and(b_id == 0, c_id == 0))
    def _():
        _s5_expand_weights(bt_ref, ct_ref, b_ref, cre_ref, cimn_ref)
        ar = jnp.broadcast_to(are_ref[...], (S5_NSEG, SSM_FLAT))
        ai = jnp.broadcast_to(aim_ref[...], (S5_NSEG, SSM_FLAT))
        pre_ref[0:S5_NSEG, :] = ar
        pim_ref[0:S5_NSEG, :] = ai

        def body(i, carry):
            pr, pi = carry
            nr = ar * pr - ai * pi
            ni = ar * pi + ai * pr
            rows = pl.ds(pl.multiple_of(i * S5_NSEG, S5_NSEG), S5_NSEG)
            pre_ref[rows, :] = nr
            pim_ref[rows, :] = ni
            return nr, ni

        lax.fori_loop(1, S5_SEG, body, (ar, ai))

    @pl.when(c_id == 0)
    def _():
        _s5_project_in(um_ref[...].astype(BF16), b_ref, hre_ref, him_ref, N_META)
        ar = are_ref[...]
        ai = aim_ref[...]
        hr = jnp.zeros((1, SSM_FLAT), F32)
        hi = jnp.zeros((1, SSM_FLAT), F32)
        for t in range(N_META):
            hr, hi = _cmul_add(ar, ai, hr, hi, hre_ref[t:t + 1, :], him_ref[t:t + 1, :])
        cre_s[...] = hr
        cim_s[...] = hi

    bufs = ((hre_ref, him_ref, hsre_ref, hsim_ref), (hre2_ref, him2_ref, hsre2_ref, hsim2_ref))
    us = [u_ref[n * S5_TC:(n + 1) * S5_TC, :] for n in range(S5_CPS)]
    for n in range(S5_CPS):
        u_perm = jnp.dot(perm_ref[...], us[n].astype(BF16), preferred_element_type=F32).astype(BF16)
        _s5_project_in(u_perm, b_ref, bufs[n][0], bufs[n][1], S5_TC)
    for n in range(S5_CPS):
        hre_n, him_n, hsre_n, hsim_n = bufs[n]
        _s5_scan_chunk(hre_n, him_n, hsre_n, hsim_n, pre_ref, pim_ref, are_ref, aim_ref, cre_s, cim_s)
        y_perm = _s5_project_out(hre_n, him_n, cre_ref, cimn_ref, S5_TC)
        y = sum(jnp.dot(permt_ref[...], piece, preferred_element_type=F32) for piece in _split3(y_perm))
        y_ref[n * S5_TC:(n + 1) * S5_TC, :] = y + d_ref[...] * us[n]

    @pl.when(c_id == n_c - 1)
    def _():
        fre_ref[0] = cre_s[...]
        fim_ref[0] = cim_s[...]


def _s5_weight_scratch():
    return [pltpu.VMEM((SSM_JBLK, 256, 2048), BF16), pltpu.VMEM((SSM_JBLK, 1024, 256), BF16),
            pltpu.VMEM((SSM_JBLK, 1024, 256), BF16)]


def _s5_prompt(u_arr, u_col_block, u_meta, abar_re, abar_im, bt, ct, d_skip, batch, seq):
    rows_step = S5_CPS * S5_TC
    n_c = seq // rows_step
    const3 = lambda b, c: (0, 0, 0)
    const2 = lambda b, c: (0, 0)
    rows = np.arange(S5_TC)
    perm_np = np.zeros((S5_TC, S5_TC), np.float32)
    perm_np[rows, (rows % S5_NSEG) * S5_SEG + rows // S5_NSEG] = 1.0
    perm = jnp.asarray(perm_np, dtype=BF16)
    permt = jnp.asarray(perm_np.T, dtype=BF16)
    return pl.pallas_call(
        _s5_prompt_kernel,
        grid=(batch, n_c),
        in_specs=[pl.BlockSpec((rows_step, SSM_WIDTH), lambda b, c: (b * n_c + c, u_col_block)),
                  pl.BlockSpec((N_META, SSM_WIDTH), const2),
                  pl.BlockSpec((S5_TC, S5_TC), const2),
                  pl.BlockSpec((S5_TC, S5_TC), const2),
                  pl.BlockSpec((SSM_JBLK, 256, 256), const3, pipeline_mode=pl.Buffered(1)),
                  pl.BlockSpec((SSM_JBLK, 1024, 256), const3, pipeline_mode=pl.Buffered(1)),
                  pl.BlockSpec((1, SSM_FLAT), const2),
                  pl.BlockSpec((1, SSM_FLAT), const2),
                  pl.BlockSpec((1, SSM_WIDTH), const2)],
        out_specs=[pl.BlockSpec((rows_step, SSM_WIDTH), lambda b, c: (b * n_c + c, 0)),
                   pl.BlockSpec((1, 1, SSM_FLAT), lambda b, c: (b, 0, 0)),
                   pl.BlockSpec((1, 1, SSM_FLAT), lambda b, c: (b, 0, 0))],
        out_shape=[jax.ShapeDtypeStruct((batch * seq, SSM_WIDTH), F32),
                   jax.ShapeDtypeStruct((batch, 1, SSM_FLAT), F32),
                   jax.ShapeDtypeStruct((batch, 1, SSM_FLAT), F32)],
        scratch_shapes=[pltpu.VMEM((S5_TC, SSM_FLAT), F32)] * 4
                       + [pltpu.VMEM((S5_TC, SSM_FLAT), F32)] * 2
                       + [pltpu.VMEM((1, SSM_FLAT), F32)] * 2
                       + [pltpu.VMEM((S5_NSEG, SSM_FLAT), F32)] * 4
                       + _s5_weight_scratch(),
        compiler_params=_cparams(2, 59),
        name="s5_prompt",
    )(u_arr, u_meta, perm, permt, bt, ct, abar_re, abar_im, d_skip)


def _s5_sample_kernel(u_ref, h0re_ref, h0im_ref, bt_ref, ct_ref, are_ref, aim_ref, d_ref,
                      y_ref, hre_ref, him_ref, b_ref, cre_ref, cimn_ref):
    rows = u_ref.shape[0]
    _s5_expand_weights(bt_ref, ct_ref, b_ref, cre_ref, cimn_ref)
    u = u_ref[...]
    _s5_project_in(u.astype(BF16), b_ref, hre_ref, him_ref, rows)
    nr, ni = _cmul_add(are_ref[...], aim_ref[...], h0re_ref[...], h0im_ref[...], hre_ref[...], him_ref[...])
    hre_ref[...] = nr
    him_ref[...] = ni
    y_ref[...] = _s5_project_out(hre_ref, him_ref, cre_ref, cimn_ref, rows) + d_ref[...] * u


def _s5_sample(u, h0_re, h0_im, abar_re, abar_im, bt, ct, d_skip):
    rows = u.shape[0]
    return pl.pallas_call(
        _s5_sample_kernel,
        out_shape=[jax.ShapeDtypeStruct((rows, SSM_WIDTH), F32),
                   jax.ShapeDtypeStruct((rows, SSM_FLAT), F32),
                   jax.ShapeDtypeStruct((rows, SSM_FLAT), F32)],
        scratch_shapes=_s5_weight_scratch(),
        compiler_params=_cparams(0, 48),
        name="s5_sample",
    )(u, h0_re, h0_im, bt, ct, abar_re, abar_im, d_skip)


def _split3(x):
    hi = x.astype(BF16)
    r1 = x - hi.astype(F32)
    mid = r1.astype(BF16)
    lo = (r1 - mid.astype(F32)).astype(BF16)
    return hi, mid, lo


def _tri(n, lower):
    r = lax.broadcasted_iota(jnp.int32, (n, n), 0)
    c = lax.broadcasted_iota(jnp.int32, (n, n), 1)
    return (r >= c) if lower else (r <= c)


def _gate_columns(gcol, bias_row):
    n = gcol.shape[0]
    g = gcol + bias_row
    fl = _log_sigmoid(g[:, HEADS:])
    tril = jnp.where(_tri(n, True), 1.0, 0.0).astype(BF16)
    b = sum(jnp.dot(tril, p, preferred_element_type=F32) for p in _split3(fl))
    return g[:, :HEADS], b


def _gate_rows(grow, bias_col):
    n = grow.shape[1]
    g = grow + bias_col
    fl = _log_sigmoid(g[HEADS:, :])
    triu = jnp.where(_tri(n, False), 1.0, 0.0).astype(BF16)
    b = sum(jnp.dot(p, triu, preferred_element_type=F32) for p in _split3(fl))
    return g[:HEADS, :], b


MLSTM_GROUP = 2


def _eye_bf16(n):
    return jnp.where(_tri(n, True) & _tri(n, False), 1.0, 0.0).astype(BF16)


def _mlstm_state_update(k_bf, v, ig_col, b_col, c_old, n_old, m_old, k_t=None, gate_rows=None):
    n_rows = k_bf.shape[0]
    k_scale = DK ** -0.5
    b_last = b_col[n_rows - 1:n_rows, :]
    e_col = b_last - b_col + ig_col
    m_new = jnp.maximum(b_last + m_old, jnp.max(e_col, axis=0, keepdims=True))
    w_end = jnp.exp(e_col - m_new) * k_scale
    carry = jnp.exp(b_last + m_old - m_new)
    if k_t is None:
        k_t = lax.dot_general(_eye_bf16(DK), k_bf, NT_DIMS, preferred_element_type=F32)
    if gate_rows is None:
        kv = jnp.dot(k_t.astype(BF16), (v.astype(F32) * w_end).astype(BF16), preferred_element_type=F32)
    else:
        ig_row, b_row = gate_rows
        w_end_row = jnp.exp(b_last - b_row + ig_row - m_new) * k_scale
        kv = jnp.dot((k_t * w_end_row).astype(BF16), v, preferred_element_type=F32)
    c_new = carry * c_old + kv
    n_new = carry * n_old + jnp.sum(k_bf.astype(F32) * w_end, axis=0, keepdims=True)
    return c_new, n_new, m_new


def _mlstm_decay(ig_row, b_row, b_col, m_old):
    n_rows = b_col.shape[0]
    dmat = jnp.where(_tri(n_rows, True), b_col + (ig_row - b_row), -jnp.inf)
    inter = b_col + m_old
    m_t = jnp.maximum(inter, jnp.max(dmat, axis=-1, keepdims=True))
    return jnp.exp(dmat - m_t), jnp.exp(inter - m_t), m_t


def _head_norm_gate(h, o, gain):
    mu = jnp.mean(h, axis=-1, keepdims=True)
    hc = h - mu
    var = jnp.mean(hc * hc, axis=-1, keepdims=True)
    return _sigmoid(o) * (hc * lax.rsqrt(var + EPS) * gain)


def _mlstm_prompt_kernel(batch, q_ref, k_ref, v_ref, o_ref, gc_ref, *rest):
    gr_refs = rest[:batch]
    rest = rest[batch:]
    km_ref, vm_ref, gcm_ref, brow_ref, bcol_ref, gain_ref = rest[:6]
    cast_in = rest[6:6 + N_CAST_IN]
    y_ref, cf_ref, nf_ref, mf_ref = rest[6 + N_CAST_IN:10 + N_CAST_IN]
    cast_out = rest[10 + N_CAST_IN:]
    k_scale = DK ** -0.5

    _weight_cast_step(cast_in, cast_out)

    @pl.when(pl.program_id(0) == 0)
    def _():
        ig_c, b_c = _gate_columns(gcm_ref[...], brow_ref[...])
        for h in range(HEADS):
            hs = slice(h * DK, (h + 1) * DK)
            c_new, n_new, m_new = _mlstm_state_update(
                km_ref[:, hs], vm_ref[:, hs], ig_c[:, h:h + 1], b_c[:, h:h + 1],
                jnp.zeros((DK, DV), F32), jnp.zeros((1, DK), F32), jnp.zeros((1, 1), F32))
            for b in range(batch):
                cf_ref[b, h] = c_new
                nf_ref[b, h] = n_new
                mf_ref[b, h:h + 1, :] = jnp.broadcast_to(m_new, (1, 128))

    eye = _eye_bf16(DK)
    for b0 in range(0, batch, MLSTM_GROUP):
        gates = {}
        for b in range(b0, b0 + MLSTM_GROUP):
            gates[b] = _gate_columns(gc_ref[b], brow_ref[...]) + _gate_rows(gr_refs[b][...], bcol_ref[...])
        chains = [(b, h) for b in range(b0, b0 + MLSTM_GROUP) for h in range(HEADS)]

        ops = []
        for b, h in chains:
            hs = slice(h * DK, (h + 1) * DK)
            q = q_ref[b, :, hs]
            k_bf = k_ref[b, :, hs]
            c_old = cf_ref[b, h]
            ops.append(dict(
                q=q, k_bf=k_bf, v=v_ref[b, :, hs], c_old=c_old, n_old=nf_ref[b, h], m_old=mf_ref[b, h:h + 1, 0:1],
                s_raw=lax.dot_general(q, k_bf, NT_DIMS, preferred_element_type=F32),
                qc=jnp.dot(q, c_old.astype(BF16), preferred_element_type=F32),
                k_t=lax.dot_general(eye, k_bf, NT_DIMS, preferred_element_type=F32)))

        for (b, h), c in zip(chains, ops):
            ig_c, b_c, ig_r, b_r = gates[b]
            c["w_intra"], c["w_inter"], c["m_t"] = _mlstm_decay(ig_r[h:h + 1, :], b_r[h:h + 1, :], b_c[:, h:h + 1],
                                                                c["m_old"])

        for (b, h), c in zip(chains, ops):
            hs = slice(h * DK, (h + 1) * DK)
            s = c["s_raw"] * (c["w_intra"] * k_scale)
            num = c["w_inter"] * c["qc"] + jnp.dot(s.astype(BF16), c["v"], preferred_element_type=F32)
            den = (c["w_inter"] * jnp.sum(c["q"].astype(F32) * c["n_old"], axis=-1, keepdims=True)
                   + jnp.sum(s, axis=-1, keepdims=True))
            hid = num / jnp.maximum(jnp.abs(den), jnp.exp(-c["m_t"]))
            y_ref[b, :, hs] = _head_norm_gate(hid, o_ref[b, :, hs], gain_ref[:, hs]).astype(y_ref.dtype)

        for (b, h), c in zip(chains, ops):
            ig_c, b_c, ig_r, b_r = gates[b]
            c_new, n_new, m_new = _mlstm_state_update(c["k_bf"], c["v"], ig_c[:, h:h + 1], b_c[:, h:h + 1],
                                                      c["c_old"], c["n_old"], c["m_old"], k_t=c["k_t"],
                                                      gate_rows=(ig_r[h:h + 1, :], b_r[h:h + 1, :]))
            cf_ref[b, h] = c_new
            nf_ref[b, h] = n_new
            mf_ref[b, h:h + 1, :] = jnp.broadcast_to(m_new, (1, 128))


def _mlstm_prompt(qkv, o_arr, o_col_block, gcol, grow, qkv_meta, gcol_meta, bias_row, bias_col, gain, batch, seq,
                  w_in_t, cast_weights):
    n_c = seq // CHUNK
    width = HEADS * DK
    const2 = lambda c: (0, 0)
    cast_in, cast_out, cast_shapes, cast_args = _weight_cast_specs(n_c, lambda c: c, w_in_t, cast_weights)
    qkv3 = qkv.reshape(batch, seq, 3 * width)
    o3 = o_arr.reshape(batch, seq, o_arr.shape[1])
    gcol3 = gcol.reshape(gcol.shape[0] // seq, seq, 2 * HEADS)
    state0 = lambda c: (0, 0, 0, 0)
    return pl.pallas_call(
        functools.partial(_mlstm_prompt_kernel, batch),
        grid=(n_c,),
        in_specs=[pl.BlockSpec((batch, CHUNK, width), lambda c: (0, c, 0)),
                  pl.BlockSpec((batch, CHUNK, width), lambda c: (0, c, 1)),
                  pl.BlockSpec((batch, CHUNK, width), lambda c: (0, c, 2)),
                  pl.BlockSpec((batch, CHUNK, width), lambda c: (0, c, o_col_block)),
                  pl.BlockSpec((batch, CHUNK, 2 * HEADS), lambda c: (0, c, 0))]
                 + [pl.BlockSpec((2 * HEADS, CHUNK), functools.partial(lambda b, c: (0, b * n_c + c), b))
                    for b in range(batch)]
                 + [pl.BlockSpec((N_META, width), lambda c: (0, 1)),
                    pl.BlockSpec((N_META, width), lambda c: (0, 2)),
                    pl.BlockSpec((N_META, 2 * HEADS), const2),
                    pl.BlockSpec((1, 2 * HEADS), const2),
                    pl.BlockSpec((2 * HEADS, 1), const2),
                    pl.BlockSpec((1, width), const2)] + cast_in,
        out_specs=[pl.BlockSpec((batch, CHUNK, width), lambda c: (0, c, 0)),
                   pl.BlockSpec((batch, HEADS, DK, DV), state0),
                   pl.BlockSpec((batch, HEADS, 1, DK), state0),
                   pl.BlockSpec((batch, HEADS, 128), lambda c: (0, 0, 0))] + cast_out,
        out_shape=[jax.ShapeDtypeStruct((batch, seq, width), BF16),
                   jax.ShapeDtypeStruct((batch, HEADS, DK, DV), F32),
                   jax.ShapeDtypeStruct((batch, HEADS, 1, DK), F32),
                   jax.ShapeDtypeStruct((batch, HEADS, 128), F32)] + cast_shapes,
        compiler_params=_cparams(1, 52),
        name="mlstm_prompt",
    )(qkv3, qkv3, qkv3, o3, gcol3, *([grow] * batch), qkv_meta, qkv_meta, gcol_meta, bias_row, bias_col, gain,
      *cast_args)


MLSTM_SB = 8


def _mlstm_sample_kernel(qkv_ref, o_ref, g_ref, c_ref, n_ref, m_ref, brow_ref, gain_ref,
                         y_ref, co_ref, no_ref, mo_ref):
    sb = MLSTM_SB
    k_scale = DK ** -0.5
    eye = _eye_bf16(DK)
    width = HEADS * DK
    pad = 2 * sb
    row_id = lax.broadcasted_iota(jnp.int32, (pad, DV), 0)

    g_all = g_ref[...] + brow_ref[...]
    ig_all = g_all[:, :HEADS]
    fl_all = _log_sigmoid(g_all[:, HEADS:])
    m_old_all = m_ref[...]
    m_new_all = jnp.maximum(fl_all + m_old_all, ig_all)
    w_in_all = jnp.exp(ig_all - m_new_all)
    w_ca_all = jnp.exp(fl_all + m_old_all - m_new_all)
    floor_all = jnp.exp(-m_new_all)
    mo_ref[...] = m_new_all

    for h in range(HEADS):
        hs = slice(h * DK, (h + 1) * DK)
        qk_rows = jnp.concatenate([qkv_ref[:, hs], qkv_ref[:, width + h * DK:width + (h + 1) * DK] * k_scale],
                                  axis=0).astype(BF16)
        qk_cols = lax.dot_general(eye, qk_rows, NT_DIMS, preferred_element_type=F32).astype(BF16)
        for i in range(sb):
            w_in = w_in_all[i:i + 1, h:h + 1]
            w_ca = w_ca_all[i:i + 1, h:h + 1]
            q = qk_rows[i:i + 1, :]
            qf = q.astype(F32)
            k = qk_rows[sb + i:sb + i + 1, :].astype(F32)
            v = qkv_ref[i:i + 1, 2 * width + h * DK:2 * width + (h + 1) * DK].astype(BF16).astype(F32)
            c_old = c_ref[i, h]
            n_old = n_ref[i, h]
            qc = jnp.dot(jnp.broadcast_to(q, (pad, DK)), c_old.astype(BF16), preferred_element_type=F32)[0:1, :]
            v_rows = jnp.where(row_id == sb + i, jnp.broadcast_to(v * w_in, (pad, DV)), 0.0).astype(BF16)
            kv = jnp.dot(qk_cols, v_rows, preferred_element_type=F32)
            s = jnp.sum(qf * k, axis=-1, keepdims=True) * w_in
            num = w_ca * qc + s * v
            den = w_ca * jnp.sum(qf * n_old, axis=-1, keepdims=True) + s
            hid = num / jnp.maximum(jnp.abs(den), floor_all[i:i + 1, h:h + 1])
            co_ref[i, h] = w_ca * c_old + kv
            no_ref[i, h] = w_ca * n_old + k * w_in
            y_ref[i:i + 1, hs] = _head_norm_gate(hid, o_ref[i:i + 1, hs], gain_ref[:, hs])


def _mlstm_sample(qkv, o_arr, gcol, c0, n0, m0, bias_row, gain):
    nb = qkv.shape[0]
    width = HEADS * DK
    sb = MLSTM_SB
    const2 = lambda i: (0, 0)
    return pl.pallas_call(
        _mlstm_sample_kernel,
        grid=(nb // sb,),
        in_specs=[pl.BlockSpec((sb, 3 * width), lambda i: (i, 0)),
                  pl.BlockSpec((sb, width), lambda i: (i, 0)),
                  pl.BlockSpec((sb, 2 * HEADS), lambda i: (i, 0)),
                  pl.BlockSpec((sb, HEADS, DK, DV), lambda i: (i, 0, 0, 0)),
                  pl.BlockSpec((sb, HEADS, 1, DK), lambda i: (i, 0, 0, 0)),
                  pl.BlockSpec((sb, HEADS), lambda i: (i, 0)),
                  pl.BlockSpec((1, 2 * HEADS), const2),
                  pl.BlockSpec((1, width), const2)],
        out_specs=[pl.BlockSpec((sb, width), lambda i: (i, 0)),
                   pl.BlockSpec((sb, HEADS, DK, DV), lambda i: (i, 0, 0, 0)),
                   pl.BlockSpec((sb, HEADS, 1, DK), lambda i: (i, 0, 0, 0)),
                   pl.BlockSpec((sb, HEADS), lambda i: (i, 0))],
        out_shape=[jax.ShapeDtypeStruct((nb, width), F32),
                   jax.ShapeDtypeStruct((nb, HEADS, DK, DV), F32),
                   jax.ShapeDtypeStruct((nb, HEADS, 1, DK), F32),
                   jax.ShapeDtypeStruct((nb, HEADS), F32)],
        compiler_params=_cparams(1, 48),
        name="mlstm_sample",
    )(qkv, o_arr, gcol, c0, n0, m0, bias_row, gain)


def _merge_rows(ya, yb, x, wgab_ref, wglu_ref, bglu_ref, wa_ref, wb_ref, wout_ref, g1_ref, b1_ref):
    xb = x.astype(BF16)
    g = _gelu_tanh(ya)
    z = jnp.dot(g.astype(BF16), wglu_ref[...], preferred_element_type=F32) + bglu_ref[...]
    out_a = g * _sigmoid(z)
    up_a = jnp.dot(out_a.astype(BF16), wa_ref[...], preferred_element_type=F32)
    g_a = lax.dot_general(xb, wgab_ref[0:D_MODEL, :], NT_DIMS, preferred_element_type=F32)
    mix = _sigmoid(g_a) * up_a
    up_b = jnp.dot(yb, wb_ref[...], preferred_element_type=F32)
    g_b = lax.dot_general(xb, wgab_ref[D_MODEL:2 * D_MODEL, :], NT_DIMS, preferred_element_type=F32)
    mix = mix + _sigmoid(g_b) * up_b
    mo = jnp.dot(mix.astype(BF16), wout_ref[...], preferred_element_type=F32)
    return _layernorm_rows(ALPHA * x + mo, g1_ref[...], b1_ref[...])


def _merge_kernel(ya_ref, yb_ref, x_ref, yas_ref, ybs_ref, xs_ref, *rest):
    weights, (o_ref, os_ref, ob_ref, osb_ref) = rest[:-4], rest[-4:]
    x1 = _merge_rows(ya_ref[...], yb_ref[...], x_ref[...], *weights)
    o_ref[...] = x1
    ob_ref[...] = x1.astype(BF16)

    @pl.when(pl.program_id(0) == pl.num_programs(0) - 1)
    def _():
        x1s = _merge_rows(yas_ref[...], ybs_ref[...].astype(BF16), xs_ref[...], *weights)
        os_ref[...] = x1s
        osb_ref[...] = x1s.astype(BF16)


def _merge(ya, yb, x, ya_s, yb_s, x_s, wt_gab, w_glu, b_glu, w_a_up, w_b_up, w_out, ln_g, ln_b, tm):
    r = x.shape[0]
    rs = x_s.shape[0]
    const2 = lambda i: (0, 0)
    resident = functools.partial(pl.BlockSpec, index_map=const2, pipeline_mode=pl.Buffered(1))
    return pl.pallas_call(
        _merge_kernel,
        grid=(r // tm,),
        in_specs=[pl.BlockSpec((tm, SSM_WIDTH), lambda i: (i, 0)),
                  pl.BlockSpec((tm, HEADS * DV), lambda i: (i, 0)),
                  pl.BlockSpec((tm, D_MODEL), lambda i: (i, 0)),
                  resident((rs, SSM_WIDTH)),
                  resident((rs, HEADS * DV)),
                  resident((rs, D_MODEL)),
                  resident((2 * D_MODEL, D_MODEL)),
                  resident((SSM_WIDTH, SSM_WIDTH)),
                  resident((1, SSM_WIDTH)),
                  resident((SSM_WIDTH, D_MODEL)),
                  resident((HEADS * DV, D_MODEL)),
                  resident((D_MODEL, D_MODEL)),
                  resident((1, D_MODEL)),
                  resident((1, D_MODEL))],
        out_specs=[pl.BlockSpec((tm, D_MODEL), lambda i: (i, 0)),
                   pl.BlockSpec((rs, D_MODEL), const2),
                   pl.BlockSpec((tm, D_MODEL), lambda i: (i, 0)),
                   pl.BlockSpec((rs, D_MODEL), const2)],
        out_shape=[jax.ShapeDtypeStruct((r, D_MODEL), F32), jax.ShapeDtypeStruct((rs, D_MODEL), F32),
                   jax.ShapeDtypeStruct((r, D_MODEL), BF16), jax.ShapeDtypeStruct((rs, D_MODEL), BF16)],
        compiler_params=_cparams(1, 59),
        name="merge_ln1",
    )(ya, yb, x, ya_s, yb_s, x_s, wt_gab, w_glu, b_glu, w_a_up, w_b_up, w_out, ln_g, ln_b)


def _swiglu(xb, wg, wu):
    hg = jnp.dot(xb, wg, preferred_element_type=F32)
    hu = jnp.dot(xb, wu, preferred_element_type=F32)
    return ((hg * _sigmoid(hg)) * hu).astype(BF16)


def _ffn_up_kernel(x_ref, xs_ref, wg_ref, wu_ref, wd_ref, h_ref, hs_ref, wdb_ref, wgb_s, wub_s):
    i = pl.program_id(1)

    @pl.when(i == 0)
    def _():
        wgb_s[...] = wg_ref[...].astype(BF16)
        wub_s[...] = wu_ref[...].astype(BF16)

    h_ref[...] = _swiglu(x_ref[...], wgb_s[...], wub_s[...])
    wdb_ref[...] = wd_ref[...].astype(BF16)

    @pl.when(i == pl.num_programs(1) - 1)
    def _():
        hs_ref[...] = _swiglu(xs_ref[...], wgb_s[...], wub_s[...])


def _ffn_up(xb, xsb, w_gate3, w_up3, w_down3, tm, tf):
    r = xb.shape[0]
    rs = xsb.shape[0]
    n_f, n_i = D_FF // tf, r // tm
    wd_rows = D_FF // (n_f * n_i)
    return pl.pallas_call(
        _ffn_up_kernel,
        grid=(n_f, n_i),
        in_specs=[pl.BlockSpec((tm, D_MODEL), lambda f, i: (i, 0)),
                  pl.BlockSpec((rs, D_MODEL), lambda f, i: (0, 0)),
                  pl.BlockSpec((None, D_MODEL, tf), lambda f, i: (0, 0, f)),
                  pl.BlockSpec((None, D_MODEL, tf), lambda f, i: (0, 0, f)),
                  pl.BlockSpec((None, wd_rows, D_MODEL), lambda f, i: (0, f * n_i + i, 0))],
        out_specs=[pl.BlockSpec((tm, tf), lambda f, i: (i, f)),
                   pl.BlockSpec((rs, tf), lambda f, i: (0, f)),
                   pl.BlockSpec((wd_rows, D_MODEL), lambda f, i: (f * n_i + i, 0))],
        out_shape=[jax.ShapeDtypeStruct((r, D_FF), BF16), jax.ShapeDtypeStruct((rs, D_FF), BF16),
                   jax.ShapeDtypeStruct((D_FF, D_MODEL), BF16)],
        scratch_shapes=[pltpu.VMEM((D_MODEL, tf), BF16), pltpu.VMEM((D_MODEL, tf), BF16)],
        compiler_params=_cparams(2, 58),
        name="ffn_up",
    )(xb, xsb, w_gate3, w_up3, w_down3)


FFN_DOWN_PIECE = 256


def _ffn_down_rows(h, x, wd_ref, g2_ref, b2_ref):
    return _layernorm_rows(ALPHA * x + jnp.dot(h, wd_ref[...], preferred_element_type=F32), g2_ref[...], b2_ref[...])


def _ffn_down_kernel(h_ref, x_ref, hs_ref, xs_ref, wd_ref, g2_ref, b2_ref, o_ref, os_ref):
    for r0 in range(0, h_ref.shape[0], FFN_DOWN_PIECE):
        rows = slice(r0, r0 + FFN_DOWN_PIECE)
        o_ref[rows, :] = _ffn_down_rows(h_ref[rows, :], x_ref[rows, :], wd_ref, g2_ref, b2_ref)

    @pl.when(pl.program_id(0) == pl.num_programs(0) - 1)
    def _():
        os_ref[...] = _ffn_down_rows(hs_ref[...], xs_ref[...], wd_ref, g2_ref, b2_ref)


def _ffn_down(h, x, h_s, x_s, wd_b, ln_g, ln_b, tm):
    r = x.shape[0]
    rs = x_s.shape[0]
    const2 = lambda i: (0, 0)
    resident = functools.partial(pl.BlockSpec, index_map=const2, pipeline_mode=pl.Buffered(1))
    return pl.pallas_call(
        _ffn_down_kernel,
        grid=(r // tm,),
        in_specs=[pl.BlockSpec((tm, D_FF), lambda i: (i, 0)),
                  pl.BlockSpec((tm, D_MODEL), lambda i: (i, 0)),
                  resident((rs, D_FF)),
                  resident((rs, D_MODEL)),
                  resident((D_FF, D_MODEL)),
                  resident((1, D_MODEL)),
                  resident((1, D_MODEL))],
        out_specs=[pl.BlockSpec((tm, D_MODEL), lambda i: (i, 0)),
                   pl.BlockSpec((rs, D_MODEL), const2)],
        out_shape=[jax.ShapeDtypeStruct((r, D_MODEL), F32), jax.ShapeDtypeStruct((rs, D_MODEL), F32)],
        compiler_params=_cparams(1, 59),
        name="ffn_down_ln2",
    )(h, x, h_s, x_s, wd_b, ln_g, ln_b)


def kernel(x_prompt, x_sample, state_ssm_re, state_ssm_im, state_mlstm_c, state_mlstm_n, state_mlstm_m,
           meta_tokens, w_in, b_if, ssm_a_re, ssm_a_im, ssm_log_dt, ssm_b_re, ssm_b_im, ssm_c_re, ssm_c_im,
           ssm_d, w_glu, b_glu, w_a_up, mh_gain, w_b_up, w_out, ln1_g, ln1_b, w_gate, w_up, w_down,
           ln2_g, ln2_b):
    batch, seq, _ = x_prompt.shape
    nb = x_sample.shape[0]
    width = HEADS * DK

    bias_row = b_if[0].reshape(1, 2 * HEADS)
    bias_col = b_if[0].reshape(2 * HEADS, 1)
    gain = mh_gain[0].reshape(1, width)
    d_skip = ssm_d[0].reshape(1, SSM_WIDTH)
    b_glu_r = b_glu[0].reshape(1, SSM_WIDTH)
    ln1 = (ln1_g[0].reshape(1, D_MODEL), ln1_b[0].reshape(1, D_MODEL))
    ln2 = (ln2_g[0].reshape(1, D_MODEL), ln2_b[0].reshape(1, D_MODEL))
    abar_re, abar_im, bt, ct = _s5_discretise(
        ssm_a_re[0], ssm_a_im[0], ssm_log_dt[0], ssm_b_re[0], ssm_b_im[0], ssm_c_re[0], ssm_c_im[0])

    xp = x_prompt.reshape(batch * seq, D_MODEL)
    xs = x_sample.reshape(nb, D_MODEL)
    xsm = jnp.concatenate([xs, meta_tokens], axis=0)
    w_in_t = jnp.swapaxes(w_in, 1, 2)
    uo_blocks, qkv_blocks = (0, 4, 2), (1, 1, 3)

    uo_p, uo_s, gcol_p, grow_p, gcol_s = _proj(xp, xsm, w_in_t, uo_blocks, F32, F32, 1024, "proj_uo", with_gates=True)
    qkv_p, qkv_s = _proj(xp, xsm, w_in_t, qkv_blocks, BF16, F32, 1024, "proj_qkv")

    u_meta = uo_s[nb:nb + N_META, :SSM_WIDTH]
    qkv_meta = qkv_s[nb:nb + N_META].astype(BF16)
    gcol_meta = gcol_s[nb:nb + N_META]

    ya_p, pf_re, pf_im = _s5_prompt(uo_p, 0, u_meta, abar_re, abar_im, bt, ct, d_skip, batch, seq)
    yb_p, pc, pn, pm, wt_gab, w_out_b, w_a_b, w_b_b, w_glu_b = _mlstm_prompt(
        qkv_p, uo_p, 1, gcol_p, grow_p, qkv_meta, gcol_meta, bias_row, bias_col, gain, batch, seq,
        w_in_t, (w_out, w_a_up, w_b_up, w_glu))
    ya_s, sf_re, sf_im = _s5_sample(uo_s[:nb, :SSM_WIDTH], state_ssm_re[0].reshape(nb, SSM_FLAT),
                                    state_ssm_im[0].reshape(nb, SSM_FLAT), abar_re, abar_im, bt, ct, d_skip)
    yb_s, sc, sn, sm = _mlstm_sample(qkv_s[:nb], uo_s[:nb, SSM_WIDTH:], gcol_s[:nb], state_mlstm_c[0],
                                     state_mlstm_n[0].reshape(nb, HEADS, 1, DK), state_mlstm_m[0], bias_row, gain)

    x1_p, x1_s, x1b_p, x1b_s = _merge(ya_p, yb_p.reshape(batch * seq, width), xp, ya_s, yb_s, xs,
                                      wt_gab, w_glu_b, b_glu_r, w_a_b, w_b_b, w_out_b, *ln1, tm=256)
    h_p, h_s, w_down_b = _ffn_up(x1b_p, x1b_s, w_gate, w_up, w_down, tm=1024, tf=512)
    y_p, y_s = _ffn_down(h_p, x1_p, h_s, x1_s, w_down_b, *ln2, tm=512)

    return (y_p.reshape(batch, seq, D_MODEL),
            y_s.reshape(nb, 1, D_MODEL),
            pf_re.reshape(1, batch, SSM_GROUPS, SSM_STATE),
            pf_im.reshape(1, batch, SSM_GROUPS, SSM_STATE),
            pc.reshape(1, batch, HEADS, DK, DV),
            pn.reshape(1, batch, HEADS, DK),
            pm[:, :, 0].reshape(1, batch, HEADS),
            sf_re.reshape(1, nb, SSM_GROUPS, SSM_STATE),
            sf_im.reshape(1, nb, SSM_GROUPS, SSM_STATE),
            sc.reshape(1, nb, HEADS, DK, DV),
            sn.reshape(1, nb, HEADS, DK),
            sm.reshape(1, nb, HEADS))
```

```python
import functools

import jax
import jax.numpy as jnp
import numpy as np
from jax import lax
from jax.experimental import pallas as pl
from jax.experimental.pallas import tpu as pltpu

F32 = jnp.float32
BF16 = jnp.bfloat16

D_MODEL = 2048
N_META = 16
SSM_WIDTH = 1024
SSM_GROUP = 16
SSM_GROUPS = 64
SSM_STATE = 64
SSM_FLAT = SSM_GROUPS * SSM_STATE
SSM_JBLK = 4
SSM_BLK_IN = SSM_WIDTH // SSM_JBLK
SSM_BLK_STATE = SSM_FLAT // SSM_JBLK
LANE_TILE = 128
HEADS = 4
DK = 256
DV = 256
CHUNK = 128
D_FF = 5632
EPS = 1e-5
ALPHA = 2.0 ** 0.25

S5_TC = 256
S5_NSEG = 8
S5_SEG = S5_TC // S5_NSEG
S5_LW = 512
S5_CPS = 2

V7X_SCOPED_VMEM_MAX_BYTES = 60000 * 1024


def _cparams(n_axes, vmem_mb):
    return pltpu.CompilerParams(
        dimension_semantics=("arbitrary",) * n_axes,
        vmem_limit_bytes=min(vmem_mb * 1024 * 1024, V7X_SCOPED_VMEM_MAX_BYTES),
    )


def _sigmoid(x):
    return 1.0 / (1.0 + jnp.exp(-x))


def _log_sigmoid(x):
    return jnp.minimum(x, 0.0) - jnp.log(1.0 + jnp.exp(-jnp.abs(x)))


def _gelu_tanh(x):
    c = 0.7978845608028654
    return 0.5 * x * (1.0 + jnp.tanh(c * (x + 0.044715 * (x * x * x))))


def _layernorm_rows(x, g, b):
    mu = jnp.mean(x, axis=-1, keepdims=True)
    xc = x - mu
    var = jnp.mean(xc * xc, axis=-1, keepdims=True)
    return xc * lax.rsqrt(var + EPS) * g + b


PROJ_TN = 1024
GATE_ROWS = 2 * HEADS
TAIL_ROW0 = SSM_WIDTH + 4 * HEADS * DK
NT_DIMS = (((1,), (1,)), ((), ()))


def _proj_kernel(with_gates, x_ref, xs_ref, wt_ref, *rest):
    if with_gates:
        wg_ref, o_ref, os_ref, g_ref, gt_ref, gs_ref, wb_s = rest
    else:
        o_ref, os_ref, wb_s = rest
    i = pl.program_id(1)

    @pl.when(i == 0)
    def _():
        wb_s[...] = wt_ref[...].astype(BF16)

    xb = x_ref[...].astype(BF16)
    o_ref[...] = lax.dot_general(xb, wb_s[...], NT_DIMS, preferred_element_type=F32).astype(o_ref.dtype)
    first_block = pl.program_id(0) == 0
    if with_gates:
        @pl.when(first_block)
        def _():
            wg = wg_ref[...].astype(BF16)
            g_ref[...] = lax.dot_general(xb, wg, NT_DIMS, preferred_element_type=F32)
            gt_ref[...] = lax.dot_general(wg, xb, NT_DIMS, preferred_element_type=F32)

    @pl.when(i == pl.num_programs(1) - 1)
    def _():
        xsb = xs_ref[...].astype(BF16)
        os_ref[...] = lax.dot_general(xsb, wb_s[...], NT_DIMS, preferred_element_type=F32).astype(os_ref.dtype)
        if with_gates:
            @pl.when(first_block)
            def _():
                gs_ref[...] = lax.dot_general(xsb, wg_ref[...].astype(BF16), NT_DIMS, preferred_element_type=F32)


def _proj(x, xs, wt3, blocks, out_dtype, small_dtype, tm, name, with_gates=False):
    r, k = x.shape
    rs = xs.shape[0]
    first, step, count = blocks
    tn = PROJ_TN
    nm = r // tm
    in_specs = [pl.BlockSpec((tm, k), lambda j, i: (i, 0)),
                pl.BlockSpec((rs, k), lambda j, i: (0, 0)),
                pl.BlockSpec((None, tn, k), lambda j, i: (0, first + step * j, 0))]
    out_specs = [pl.BlockSpec((tm, tn), lambda j, i: (i, j)),
                 pl.BlockSpec((rs, tn), lambda j, i: (0, j))]
    out_shape = [jax.ShapeDtypeStruct((r, count * tn), out_dtype),
                 jax.ShapeDtypeStruct((rs, count * tn), small_dtype)]
    args = (x, xs, wt3)
    if with_gates:
        ng = GATE_ROWS
        in_specs += [pl.BlockSpec((None, ng, k), lambda j, i: (0, TAIL_ROW0 // ng, 0))]
        gate_block = lambda j, i: jnp.where(j == 0, i, nm - 1)
        out_specs += [pl.BlockSpec((tm, ng), lambda j, i: (gate_block(j, i), 0)),
                      pl.BlockSpec((ng, tm), lambda j, i: (0, gate_block(j, i))),
                      pl.BlockSpec((rs, ng), lambda j, i: (0, 0))]
        out_shape += [jax.ShapeDtypeStruct((r, ng), F32), jax.ShapeDtypeStruct((ng, r), F32),
                      jax.ShapeDtypeStruct((rs, ng), F32)]
        args = (x, xs, wt3, wt3)
    return pl.pallas_call(
        functools.partial(_proj_kernel, with_gates),
        grid=(count, nm),
        in_specs=in_specs,
        out_specs=out_specs,
        out_shape=out_shape,
        scratch_shapes=[pltpu.VMEM((tn, k), BF16)],
        compiler_params=_cparams(2, 56),
        name=name,
    )(*args)


def _s5_discretise(a_re, a_im, log_dt, b_re, b_im, c_re, c_im):
    dt = jnp.exp(log_dt)
    e = jnp.exp(a_re * dt)
    abar_re = e * jnp.cos(a_im * dt)
    abar_im = e * jnp.sin(a_im * dt)
    nr = abar_re - 1.0
    ni = abar_im
    den = a_re * a_re + a_im * a_im
    coef_re = (nr * a_re + ni * a_im) / den
    coef_im = (ni * a_re - nr * a_im) / den
    bb_re = coef_re[..., None] * b_re - coef_im[..., None] * b_im
    bb_im = coef_re[..., None] * b_im + coef_im[..., None] * b_re
    gpb = SSM_GROUPS // SSM_JBLK

    def b_rows(bb):
        t = bb.reshape(SSM_JBLK, gpb, SSM_STATE, SSM_GROUP).transpose(0, 1, 3, 2)
        return t.reshape(SSM_JBLK, SSM_BLK_IN, SSM_STATE)

    def c_rows(cc):
        t = cc.reshape(SSM_JBLK, gpb, SSM_GROUP, SSM_STATE).transpose(0, 1, 3, 2)
        return t.reshape(SSM_JBLK, SSM_BLK_STATE, SSM_GROUP)

    b_rep, c_rep = LANE_TILE // SSM_STATE, LANE_TILE // SSM_GROUP
    bt = jnp.concatenate([jnp.tile(b_rows(bb_re), (1, 1, b_rep)), jnp.tile(b_rows(bb_im), (1, 1, b_rep))], axis=-1)
    ct = jnp.concatenate([jnp.tile(c_rows(c_re), (1, 1, c_rep)), jnp.tile(c_rows(-c_im), (1, 1, c_rep))], axis=-1)
    return abar_re.reshape(1, SSM_FLAT), abar_im.reshape(1, SSM_FLAT), bt.astype(BF16), ct.astype(BF16)


def _s5_expand_weights(bt_ref, ct_ref, bw_s, cwr_s, cwi_s):
    log2 = lambda n: n.bit_length() - 1
    rb = lax.broadcasted_iota(jnp.int32, (SSM_BLK_IN, 2 * SSM_BLK_STATE), 0)
    cb = lax.broadcasted_iota(jnp.int32, (SSM_BLK_IN, 2 * SSM_BLK_STATE), 1)
    bmask = (rb >> log2(SSM_GROUP)) == ((cb & (SSM_BLK_STATE - 1)) >> log2(SSM_STATE))
    rc = lax.broadcasted_iota(jnp.int32, (SSM_BLK_STATE, SSM_BLK_IN), 0)
    cc = lax.broadcasted_iota(jnp.int32, (SSM_BLK_STATE, SSM_BLK_IN), 1)
    cmask = (rc >> log2(SSM_STATE)) == (cc >> log2(SSM_GROUP))
    b_tiles, c_tiles = SSM_BLK_STATE // LANE_TILE, SSM_BLK_IN // LANE_TILE
    for j in range(SSM_JBLK):
        bt = bt_ref[j].astype(F32)
        wide = jnp.concatenate([bt[:, :LANE_TILE]] * b_tiles + [bt[:, LANE_TILE:]] * b_tiles, axis=1)
        bw_s[j] = jnp.where(bmask, wide, 0.0).astype(BF16)
        ct = ct_ref[j].astype(F32)
        cwr_s[j] = jnp.where(cmask, jnp.concatenate([ct[:, :LANE_TILE]] * c_tiles, axis=1), 0.0).astype(BF16)
        cwi_s[j] = jnp.where(cmask, jnp.concatenate([ct[:, LANE_TILE:]] * c_tiles, axis=1), 0.0).astype(BF16)


def _cmul_add(ar, ai, hr, hi, xr, xi):
    return ar * hr - ai * hi + xr, ar * hi + ai * hr + xi


def _s5_project_in(u, b_ref, hre_ref, him_ref, rows):
    for j in range(SSM_JBLK):
        state = slice(j * SSM_BLK_STATE, (j + 1) * SSM_BLK_STATE)
        bu = jnp.dot(u[:, j * SSM_BLK_IN:(j + 1) * SSM_BLK_IN], b_ref[j], preferred_element_type=F32)
        hre_ref[0:rows, state] = bu[:, :SSM_BLK_STATE]
        him_ref[0:rows, state] = bu[:, SSM_BLK_STATE:]


def _s5_project_out(hre_ref, him_ref, cre_ref, cimn_ref, rows):
    ys = []
    for j in range(SSM_JBLK):
        state = slice(j * SSM_BLK_STATE, (j + 1) * SSM_BLK_STATE)
        hr = hre_ref[0:rows, state].astype(BF16)
        hi = him_ref[0:rows, state].astype(BF16)
        y = jnp.dot(hr, cre_ref[j], preferred_element_type=F32)
        ys.append(y + jnp.dot(hi, cimn_ref[j], preferred_element_type=F32))
    return jnp.concatenate(ys, axis=1)


def _s5_scan_chunk(hre_ref, him_ref, hsre_ref, hsim_ref, pre_ref, pim_ref, are_ref, aim_ref, cre_s, cim_s):
    full = (S5_NSEG, S5_LW)
    for lg in range(SSM_FLAT // S5_LW):
        lanes = slice(lg * S5_LW, (lg + 1) * S5_LW)
        ar = jnp.broadcast_to(are_ref[:, lanes], full)
        ai = jnp.broadcast_to(aim_ref[:, lanes], full)

        hr = jnp.zeros(full, F32)
        hi = jnp.zeros(full, F32)
        for k in range(S5_SEG):
            rows = slice(k * S5_NSEG, (k + 1) * S5_NSEG)
            hr, hi = _cmul_add(ar, ai, hr, hi, hre_ref[rows, lanes], him_ref[rows, lanes])
            hre_ref[rows, lanes] = hr
            him_ref[rows, lanes] = hi
        er, ei = hr, hi

        asr = pre_ref[S5_TC - 1:S5_TC, lanes]
        asi = pim_ref[S5_TC - 1:S5_TC, lanes]
        hr = cre_s[:, lanes]
        hi = cim_s[:, lanes]
        for j in range(S5_NSEG):
            hsre_ref[j:j + 1, lanes] = hr
            hsim_ref[j:j + 1, lanes] = hi
            hr, hi = _cmul_add(asr, asi, hr, hi, er[j:j + 1, :], ei[j:j + 1, :])
        cre_s[:, lanes] = hr
        cim_s[:, lanes] = hi

        sr = hsre_ref[:, lanes]
        si = hsim_ref[:, lanes]
        for k in range(S5_SEG):
            rows = slice(k * S5_NSEG, (k + 1) * S5_NSEG)
            pr = pre_ref[rows, lanes]
            pi = pim_ref[rows, lanes]
            hre_ref[rows, lanes] = hre_ref[rows, lanes] + (pr * sr - pi * si)
            him_ref[rows, lanes] = him_ref[rows, lanes] + (pr * si + pi * sr)


N_CAST_IN = 6
N_CAST_OUT = 5


def _weight_cast_specs(n_steps, step_of, w_in_t, others):
    gab_rows = 2 * D_MODEL // n_steps
    gab_blk0 = TAIL_ROW0 // gab_rows
    gab_next0 = (TAIL_ROW0 + gab_rows) // GATE_ROWS
    rows = [w.shape[1] // n_steps for w in others]
    in_specs = [pl.BlockSpec((None, gab_rows, D_MODEL), lambda *g: (0, gab_blk0 + step_of(*g), 0)),
                pl.BlockSpec((None, GATE_ROWS, D_MODEL),
                             lambda *g: (0, gab_next0 + (gab_rows // GATE_ROWS) * step_of(*g), 0))]
    in_specs += [pl.BlockSpec((None, nr, w.shape[2]), lambda *g: (0, step_of(*g), 0)) for nr, w in zip(rows, others)]
    out_specs = [pl.BlockSpec((gab_rows, D_MODEL), lambda *g: (step_of(*g), 0))]
    out_specs += [pl.BlockSpec((nr, w.shape[2]), lambda *g: (step_of(*g), 0)) for nr, w in zip(rows, others)]
    out_shapes = [jax.ShapeDtypeStruct((2 * D_MODEL, D_MODEL), BF16)]
    out_shapes += [jax.ShapeDtypeStruct(w.shape[1:], BF16) for w in others]
    return in_specs, out_specs, out_shapes, (w_in_t, w_in_t) + tuple(others)


def _weight_cast_step(in_refs, out_refs):
    wga_ref, wgn_ref = in_refs[:2]
    out_refs[0][...] = jnp.concatenate([wga_ref[GATE_ROWS:, :], wgn_ref[...]], axis=0).astype(BF16)
    for src, dst in zip(in_refs[2:], out_refs[1:]):
        dst[...] = src[...].astype(BF16)


def _s5_prompt_kernel(u_ref, um_ref, perm_ref, permt_ref, bt_ref, ct_ref, are_ref, aim_ref, d_ref,
                      y_ref, fre_ref, fim_ref,
                      hre_ref, him_ref, hre2_ref, him2_ref, pre_ref, pim_ref, cre_s, cim_s,
                      hsre_ref, hsim_ref, hsre2_ref, hsim2_ref,
                      b_ref, cre_ref, cimn_ref):
    b_id = pl.program_id(0)
    c_id = pl.program_id(1)
    n_c = pl.num_programs(1)

    @pl.when(jnp.logical_and(b_id == 0, c_id == 0))
    def _():
        _s5_expand_weights(bt_ref, ct_ref, b_ref, cre_ref, cimn_ref)
        ar = jnp.broadcast_to(are_ref[...], (S5_NSEG, SSM_FLAT))
        ai = jnp.broadcast_to(aim_ref[...], (S5_NSEG, SSM_FLAT))
        pre_ref[0:S5_NSEG, :] = ar
        pim_ref[0:S5_NSEG, :] = ai

        def body(i, carry):
            pr, pi = carry
            nr = ar * pr - ai * pi
            ni = ar * pi + ai * pr
            rows = pl.ds(pl.multiple_of(i * S5_NSEG, S5_NSEG), S5_NSEG)
            pre_ref[rows, :] = nr
            pim_ref[rows, :] = ni
            return nr, ni

        lax.fori_loop(1, S5_SEG, body, (ar, ai))

    @pl.when(c_id == 0)
    def _():
        _s5_project_in(um_ref[...].astype(BF16), b_ref, hre_ref, him_ref, N_META)
        ar = are_ref[...]
        ai = aim_ref[...]
        hr = jnp.zeros((1, SSM_FLAT), F32)
        hi = jnp.zeros((1, SSM_FLAT), F32)
        for t in range(N_META):
            hr, hi = _cmul_add(ar, ai, hr, hi, hre_ref[t:t + 1, :], him_ref[t:t + 1, :])
        cre_s[...] = hr
        cim_s[...] = hi

    bufs = ((hre_ref, him_ref, hsre_ref, hsim_ref), (hre2_ref, him2_ref, hsre2_ref, hsim2_ref))
    us = [u_ref[n * S5_TC:(n + 1) * S5_TC, :] for n in range(S5_CPS)]
    for n in range(S5_CPS):
        u_perm = jnp.dot(perm_ref[...], us[n].astype(BF16), preferred_element_type=F32).astype(BF16)
        _s5_project_in(u_perm, b_ref, bufs[n][0], bufs[n][1], S5_TC)
    for n in range(S5_CPS):
        hre_n, him_n, hsre_n, hsim_n = bufs[n]
        _s5_scan_chunk(hre_n, him_n, hsre_n, hsim_n, pre_ref, pim_ref, are_ref, aim_ref, cre_s, cim_s)
        y_perm = _s5_project_out(hre_n, him_n, cre_ref, cimn_ref, S5_TC)
        y = sum(jnp.dot(permt_ref[...], piece, preferred_element_type=F32) for piece in _split3(y_perm))
        y_ref[n * S5_TC:(n + 1) * S5_TC, :] = y + d_ref[...] * us[n]

    @pl.when(c_id == n_c - 1)
    def _():
        fre_ref[0] = cre_s[...]
        fim_ref[0] = cim_s[...]


def _s5_weight_scratch():
    return [pltpu.VMEM((SSM_JBLK, SSM_BLK_IN, 2 * SSM_BLK_STATE), BF16),
            pltpu.VMEM((SSM_JBLK, SSM_BLK_STATE, SSM_BLK_IN), BF16),
            pltpu.VMEM((SSM_JBLK, SSM_BLK_STATE, SSM_BLK_IN), BF16)]


def _s5_prompt(u_arr, u_col_block, u_meta, abar_re, abar_im, bt, ct, d_skip, batch, seq):
    rows_step = S5_CPS * S5_TC
    n_c = seq // rows_step
    const3 = lambda b, c: (0, 0, 0)
    const2 = lambda b, c: (0, 0)
    rows = np.arange(S5_TC)
    perm_np = np.zeros((S5_TC, S5_TC), np.float32)
    perm_np[rows, (rows % S5_NSEG) * S5_SEG + rows // S5_NSEG] = 1.0
    perm = jnp.asarray(perm_np, dtype=BF16)
    permt = jnp.asarray(perm_np.T, dtype=BF16)
    return pl.pallas_call(
        _s5_prompt_kernel,
        grid=(batch, n_c),
        in_specs=[pl.BlockSpec((rows_step, SSM_WIDTH), lambda b, c: (b * n_c + c, u_col_block)),
                  pl.BlockSpec((N_META, SSM_WIDTH), const2),
                  pl.BlockSpec((S5_TC, S5_TC), const2),
                  pl.BlockSpec((S5_TC, S5_TC), const2),
                  pl.BlockSpec((SSM_JBLK, SSM_BLK_IN, 2 * LANE_TILE), const3, pipeline_mode=pl.Buffered(1)),
                  pl.BlockSpec((SSM_JBLK, SSM_BLK_STATE, 2 * LANE_TILE), const3, pipeline_mode=pl.Buffered(1)),
                  pl.BlockSpec((1, SSM_FLAT), const2),
                  pl.BlockSpec((1, SSM_FLAT), const2),
                  pl.BlockSpec((1, SSM_WIDTH), const2)],
        out_specs=[pl.BlockSpec((rows_step, SSM_WIDTH), lambda b, c: (b * n_c + c, 0)),
                   pl.BlockSpec((1, 1, SSM_FLAT), lambda b, c: (b, 0, 0)),
                   pl.BlockSpec((1, 1, SSM_FLAT), lambda b, c: (b, 0, 0))],
        out_shape=[jax.ShapeDtypeStruct((batch * seq, SSM_WIDTH), F32),
                   jax.ShapeDtypeStruct((batch, 1, SSM_FLAT), F32),
                   jax.ShapeDtypeStruct((batch, 1, SSM_FLAT), F32)],
        scratch_shapes=[pltpu.VMEM((S5_TC, SSM_FLAT), F32)] * 4
                       + [pltpu.VMEM((S5_TC, SSM_FLAT), F32)] * 2
                       + [pltpu.VMEM((1, SSM_FLAT), F32)] * 2
                       + [pltpu.VMEM((S5_NSEG, SSM_FLAT), F32)] * 4
                       + _s5_weight_scratch(),
        compiler_params=_cparams(2, 59),
        name="s5_prompt",
    )(u_arr, u_meta, perm, permt, bt, ct, abar_re, abar_im, d_skip)


def _s5_sample_kernel(u_ref, h0re_ref, h0im_ref, bt_ref, ct_ref, are_ref, aim_ref, d_ref,
                      y_ref, hre_ref, him_ref, b_ref, cre_ref, cimn_ref):
    rows = u_ref.shape[0]
    _s5_expand_weights(bt_ref, ct_ref, b_ref, cre_ref, cimn_ref)
    u = u_ref[...]
    _s5_project_in(u.astype(BF16), b_ref, hre_ref, him_ref, rows)
    nr, ni = _cmul_add(are_ref[...], aim_ref[...], h0re_ref[...], h0im_ref[...], hre_ref[...], him_ref[...])
    hre_ref[...] = nr
    him_ref[...] = ni
    y_ref[...] = _s5_project_out(hre_ref, him_ref, cre_ref, cimn_ref, rows) + d_ref[...] * u


def _s5_sample(u, h0_re, h0_im, abar_re, abar_im, bt, ct, d_skip):
    rows = u.shape[0]
    return pl.pallas_call(
        _s5_sample_kernel,
        out_shape=[jax.ShapeDtypeStruct((rows, SSM_WIDTH), F32),
                   jax.ShapeDtypeStruct((rows, SSM_FLAT), F32),
                   jax.ShapeDtypeStruct((rows, SSM_FLAT), F32)],
        scratch_shapes=_s5_weight_scratch(),
        compiler_params=_cparams(0, 48),
        name="s5_sample",
    )(u, h0_re, h0_im, bt, ct, abar_re, abar_im, d_skip)


def _split3(x):
    hi = x.astype(BF16)
    r1 = x - hi.astype(F32)
    mid = r1.astype(BF16)
    lo = (r1 - mid.astype(F32)).astype(BF16)
    return hi, mid, lo


def _tri(n, lower):
    r = lax.broadcasted_iota(jnp.int32, (n, n), 0)
    c = lax.broadcasted_iota(jnp.int32, (n, n), 1)
    return (r >= c) if lower else (r <= c)


def _gate_columns(gcol, bias_row):
    n = gcol.shape[0]
    g = gcol + bias_row
    fl = _log_sigmoid(g[:, HEADS:])
    tril = jnp.where(_tri(n, True), 1.0, 0.0).astype(BF16)
    b = sum(jnp.dot(tril, p, preferred_element_type=F32) for p in _split3(fl))
    return g[:, :HEADS], b


def _gate_rows(grow, bias_col):
    n = grow.shape[1]
    g = grow + bias_col
    fl = _log_sigmoid(g[HEADS:, :])
    triu = jnp.where(_tri(n, False), 1.0, 0.0).astype(BF16)
    b = sum(jnp.dot(p, triu, preferred_element_type=F32) for p in _split3(fl))
    return g[:HEADS, :], b


MLSTM_GROUP = 2


def _eye_bf16(n):
    return jnp.where(_tri(n, True) & _tri(n, False), 1.0, 0.0).astype(BF16)


def _mlstm_state_update(k_bf, v, ig_col, b_col, c_old, n_old, m_old, k_t=None, gate_rows=None):
    n_rows = k_bf.shape[0]
    k_scale = DK ** -0.5
    b_last = b_col[n_rows - 1:n_rows, :]
    e_col = b_last - b_col + ig_col
    m_new = jnp.maximum(b_last + m_old, jnp.max(e_col, axis=0, keepdims=True))
    w_end = jnp.exp(e_col - m_new) * k_scale
    carry = jnp.exp(b_last + m_old - m_new)
    if k_t is None:
        k_t = lax.dot_general(_eye_bf16(DK), k_bf, NT_DIMS, preferred_element_type=F32)
    if gate_rows is None:
        kv = jnp.dot(k_t.astype(BF16), (v.astype(F32) * w_end).astype(BF16), preferred_element_type=F32)
    else:
        ig_row, b_row = gate_rows
        w_end_row = jnp.exp(b_last - b_row + ig_row - m_new) * k_scale
        kv = jnp.dot((k_t * w_end_row).astype(BF16), v, preferred_element_type=F32)
    c_new = carry * c_old + kv
    n_new = carry * n_old + jnp.sum(k_bf.astype(F32) * w_end, axis=0, keepdims=True)
    return c_new, n_new, m_new


def _mlstm_decay(ig_row, b_row, b_col, m_old):
    n_rows = b_col.shape[0]
    dmat = jnp.where(_tri(n_rows, True), b_col + (ig_row - b_row), -jnp.inf)
    inter = b_col + m_old
    m_t = jnp.maximum(inter, jnp.max(dmat, axis=-1, keepdims=True))
    return jnp.exp(dmat - m_t), jnp.exp(inter - m_t), m_t


def _head_norm_gate(h, o, gain):
    mu = jnp.mean(h, axis=-1, keepdims=True)
    hc = h - mu
    var = jnp.mean(hc * hc, axis=-1, keepdims=True)
    return _sigmoid(o) * (hc * lax.rsqrt(var + EPS) * gain)


def _mlstm_prompt_kernel(batch, q_ref, k_ref, v_ref, o_ref, gc_ref, *rest):
    gr_refs = rest[:batch]
    rest = rest[batch:]
    km_ref, vm_ref, gcm_ref, brow_ref, bcol_ref, gain_ref = rest[:6]
    cast_in = rest[6:6 + N_CAST_IN]
    y_ref, cf_ref, nf_ref, mf_ref = rest[6 + N_CAST_IN:10 + N_CAST_IN]
    cast_out = rest[10 + N_CAST_IN:]
    k_scale = DK ** -0.5

    _weight_cast_step(cast_in, cast_out)

    @pl.when(pl.program_id(0) == 0)
    def _():
        ig_c, b_c = _gate_columns(gcm_ref[...], brow_ref[...])
        for h in range(HEADS):
            hs = slice(h * DK, (h + 1) * DK)
            c_new, n_new, m_new = _mlstm_state_update(
                km_ref[:, hs], vm_ref[:, hs], ig_c[:, h:h + 1], b_c[:, h:h + 1],
                jnp.zeros((DK, DV), F32), jnp.zeros((1, DK), F32), jnp.zeros((1, 1), F32))
            for b in range(batch):
                cf_ref[b, h] = c_new
                nf_ref[b, h] = n_new
                mf_ref[b, h:h + 1, :] = jnp.broadcast_to(m_new, (1, LANE_TILE))

    eye = _eye_bf16(DK)
    for b0 in range(0, batch, MLSTM_GROUP):
        gates = {}
        for b in range(b0, b0 + MLSTM_GROUP):
            gates[b] = _gate_columns(gc_ref[b], brow_ref[...]) + _gate_rows(gr_refs[b][...], bcol_ref[...])
        chains = [(b, h) for b in range(b0, b0 + MLSTM_GROUP) for h in range(HEADS)]

        ops = []
        for b, h in chains:
            hs = slice(h * DK, (h + 1) * DK)
            q = q_ref[b, :, hs]
            k_bf = k_ref[b, :, hs]
            c_old = cf_ref[b, h]
            ops.append(dict(
                q=q, k_bf=k_bf, v=v_ref[b, :, hs], c_old=c_old, n_old=nf_ref[b, h], m_old=mf_ref[b, h:h + 1, 0:1],
                s_raw=lax.dot_general(q, k_bf, NT_DIMS, preferred_element_type=F32),
                qc=jnp.dot(q, c_old.astype(BF16), preferred_element_type=F32),
                k_t=lax.dot_general(eye, k_bf, NT_DIMS, preferred_element_type=F32)))

        for (b, h), c in zip(chains, ops):
            ig_c, b_c, ig_r, b_r = gates[b]
            c["w_intra"], c["w_inter"], c["m_t"] = _mlstm_decay(ig_r[h:h + 1, :], b_r[h:h + 1, :], b_c[:, h:h + 1],
                                                                c["m_old"])

        for (b, h), c in zip(chains, ops):
            hs = slice(h * DK, (h + 1) * DK)
            s = c["s_raw"] * (c["w_intra"] * k_scale)
            num = c["w_inter"] * c["qc"] + jnp.dot(s.astype(BF16), c["v"], preferred_element_type=F32)
            den = (c["w_inter"] * jnp.sum(c["q"].astype(F32) * c["n_old"], axis=-1, keepdims=True)
                   + jnp.sum(s, axis=-1, keepdims=True))
            hid = num / jnp.maximum(jnp.abs(den), jnp.exp(-c["m_t"]))
            y_ref[b, :, hs] = _head_norm_gate(hid, o_ref[b, :, hs], gain_ref[:, hs]).astype(y_ref.dtype)

        for (b, h), c in zip(chains, ops):
            ig_c, b_c, ig_r, b_r = gates[b]
            c_new, n_new, m_new = _mlstm_state_update(c["k_bf"], c["v"], ig_c[:, h:h + 1], b_c[:, h:h + 1],
                                                      c["c_old"], c["n_old"], c["m_old"], k_t=c["k_t"],
                                                      gate_rows=(ig_r[h:h + 1, :], b_r[h:h + 1, :]))
            cf_ref[b, h] = c_new
            nf_ref[b, h] = n_new
            mf_ref[b, h:h + 1, :] = jnp.broadcast_to(m_new, (1, LANE_TILE))


def _mlstm_prompt(qkv, o_arr, o_col_block, gcol, grow, qkv_meta, gcol_meta, bias_row, bias_col, gain, batch, seq,
                  w_in_t, cast_weights):
    n_c = seq // CHUNK
    width = HEADS * DK
    const2 = lambda c: (0, 0)
    cast_in, cast_out, cast_shapes, cast_args = _weight_cast_specs(n_c, lambda c: c, w_in_t, cast_weights)
    qkv3 = qkv.reshape(batch, seq, 3 * width)
    o3 = o_arr.reshape(batch, seq, o_arr.shape[1])
    gcol3 = gcol.reshape(gcol.shape[0] // seq, seq, 2 * HEADS)
    state0 = lambda c: (0, 0, 0, 0)
    return pl.pallas_call(
        functools.partial(_mlstm_prompt_kernel, batch),
        grid=(n_c,),
        in_specs=[pl.BlockSpec((batch, CHUNK, width), lambda c: (0, c, 0)),
                  pl.BlockSpec((batch, CHUNK, width), lambda c: (0, c, 1)),
                  pl.BlockSpec((batch, CHUNK, width), lambda c: (0, c, 2)),
                  pl.BlockSpec((batch, CHUNK, width), lambda c: (0, c, o_col_block)),
                  pl.BlockSpec((batch, CHUNK, 2 * HEADS), lambda c: (0, c, 0))]
                 + [pl.BlockSpec((2 * HEADS, CHUNK), functools.partial(lambda b, c: (0, b * n_c + c), b))
                    for b in range(batch)]
                 + [pl.BlockSpec((N_META, width), lambda c: (0, 1)),
                    pl.BlockSpec((N_META, width), lambda c: (0, 2)),
                    pl.BlockSpec((N_META, 2 * HEADS), const2),
                    pl.BlockSpec((1, 2 * HEADS), const2),
                    pl.BlockSpec((2 * HEADS, 1), const2),
                    pl.BlockSpec((1, width), const2)] + cast_in,
        out_specs=[pl.BlockSpec((batch, CHUNK, width), lambda c: (0, c, 0)),
                   pl.BlockSpec((batch, HEADS, DK, DV), state0),
                   pl.BlockSpec((batch, HEADS, 1, DK), state0),
                   pl.BlockSpec((batch, HEADS, LANE_TILE), lambda c: (0, 0, 0))] + cast_out,
        out_shape=[jax.ShapeDtypeStruct((batch, seq, width), BF16),
                   jax.ShapeDtypeStruct((batch, HEADS, DK, DV), F32),
                   jax.ShapeDtypeStruct((batch, HEADS, 1, DK), F32),
                   jax.ShapeDtypeStruct((batch, HEADS, LANE_TILE), F32)] + cast_shapes,
        compiler_params=_cparams(1, 52),
        name="mlstm_prompt",
    )(qkv3, qkv3, qkv3, o3, gcol3, *([grow] * batch), qkv_meta, qkv_meta, gcol_meta, bias_row, bias_col, gain,
      *cast_args)


MLSTM_SB = 8


def _mlstm_sample_kernel(qkv_ref, o_ref, g_ref, c_ref, n_ref, m_ref, brow_ref, gain_ref,
                         y_ref, co_ref, no_ref, mo_ref):
    sb = MLSTM_SB
    k_scale = DK ** -0.5
    eye = _eye_bf16(DK)
    width = HEADS * DK
    pad = 2 * sb
    row_id = lax.broadcasted_iota(jnp.int32, (pad, DV), 0)

    g_all = g_ref[...] + brow_ref[...]
    ig_all = g_all[:, :HEADS]
    fl_all = _log_sigmoid(g_all[:, HEADS:])
    m_old_all = m_ref[...]
    m_new_all = jnp.maximum(fl_all + m_old_all, ig_all)
    w_in_all = jnp.exp(ig_all - m_new_all)
    w_ca_all = jnp.exp(fl_all + m_old_all - m_new_all)
    floor_all = jnp.exp(-m_new_all)
    mo_ref[...] = m_new_all

    for h in range(HEADS):
        hs = slice(h * DK, (h + 1) * DK)
        qk_rows = jnp.concatenate([qkv_ref[:, hs], qkv_ref[:, width + h * DK:width + (h + 1) * DK] * k_scale],
                                  axis=0).astype(BF16)
        qk_cols = lax.dot_general(eye, qk_rows, NT_DIMS, preferred_element_type=F32).astype(BF16)
        for i in range(sb):
            w_in = w_in_all[i:i + 1, h:h + 1]
            w_ca = w_ca_all[i:i + 1, h:h + 1]
            q = qk_rows[i:i + 1, :]
            qf = q.astype(F32)
            k = qk_rows[sb + i:sb + i + 1, :].astype(F32)
            v = qkv_ref[i:i + 1, 2 * width + h * DK:2 * width + (h + 1) * DK].astype(BF16).astype(F32)
            c_old = c_ref[i, h]
            n_old = n_ref[i, h]
            qc = jnp.dot(jnp.broadcast_to(q, (pad, DK)), c_old.astype(BF16), preferred_element_type=F32)[0:1, :]
            v_rows = jnp.where(row_id == sb + i, jnp.broadcast_to(v * w_in, (pad, DV)), 0.0).astype(BF16)
            kv = jnp.dot(qk_cols, v_rows, preferred_element_type=F32)
            s = jnp.sum(qf * k, axis=-1, keepdims=True) * w_in
            num = w_ca * qc + s * v
            den = w_ca * jnp.sum(qf * n_old, axis=-1, keepdims=True) + s
            hid = num / jnp.maximum(jnp.abs(den), floor_all[i:i + 1, h:h + 1])
            co_ref[i, h] = w_ca * c_old + kv
            no_ref[i, h] = w_ca * n_old + k * w_in
            y_ref[i:i + 1, hs] = _head_norm_gate(hid, o_ref[i:i + 1, hs], gain_ref[:, hs])


def _mlstm_sample(qkv, o_arr, gcol, c0, n0, m0, bias_row, gain):
    nb = qkv.shape[0]
    width = HEADS * DK
    sb = MLSTM_SB
    const2 = lambda i: (0, 0)
    return pl.pallas_call(
        _mlstm_sample_kernel,
        grid=(nb // sb,),
        in_specs=[pl.BlockSpec((sb, 3 * width), lambda i: (i, 0)),
                  pl.BlockSpec((sb, width), lambda i: (i, 0)),
                  pl.BlockSpec((sb, 2 * HEADS), lambda i: (i, 0)),
                  pl.BlockSpec((sb, HEADS, DK, DV), lambda i: (i, 0, 0, 0)),
                  pl.BlockSpec((sb, HEADS, 1, DK), lambda i: (i, 0, 0, 0)),
                  pl.BlockSpec((sb, HEADS), lambda i: (i, 0)),
                  pl.BlockSpec((1, 2 * HEADS), const2),
                  pl.BlockSpec((1, width), const2)],
        out_specs=[pl.BlockSpec((sb, width), lambda i: (i, 0)),
                   pl.BlockSpec((sb, HEADS, DK, DV), lambda i: (i, 0, 0, 0)),
                   pl.BlockSpec((sb, HEADS, 1, DK), lambda i: (i, 0, 0, 0)),
                   pl.BlockSpec((sb, HEADS), lambda i: (i, 0))],
        out_shape=[jax.ShapeDtypeStruct((nb, width), F32),
                   jax.ShapeDtypeStruct((nb, HEADS, DK, DV), F32),
                   jax.ShapeDtypeStruct((nb, HEADS, 1, DK), F32),
                   jax.ShapeDtypeStruct((nb, HEADS), F32)],
        compiler_params=_cparams(1, 48),
        name="mlstm_sample",
    )(qkv, o_arr, gcol, c0, n0, m0, bias_row, gain)


def _merge_rows(ya, yb, x, wgab_ref, wglu_ref, bglu_ref, wa_ref, wb_ref, wout_ref, g1_ref, b1_ref):
    xb = x.astype(BF16)
    g = _gelu_tanh(ya)
    z = jnp.dot(g.astype(BF16), wglu_ref[...], preferred_element_type=F32) + bglu_ref[...]
    out_a = g * _sigmoid(z)
    up_a = jnp.dot(out_a.astype(BF16), wa_ref[...], preferred_element_type=F32)
    g_a = lax.dot_general(xb, wgab_ref[0:D_MODEL, :], NT_DIMS, preferred_element_type=F32)
    mix = _sigmoid(g_a) * up_a
    up_b = jnp.dot(yb, wb_ref[...], preferred_element_type=F32)
    g_b = lax.dot_general(xb, wgab_ref[D_MODEL:2 * D_MODEL, :], NT_DIMS, preferred_element_type=F32)
    mix = mix + _sigmoid(g_b) * up_b
    mo = jnp.dot(mix.astype(BF16), wout_ref[...], preferred_element_type=F32)
    return _layernorm_rows(ALPHA * x + mo, g1_ref[...], b1_ref[...])


def _merge_kernel(ya_ref, yb_ref, x_ref, yas_ref, ybs_ref, xs_ref, *rest):
    weights, (o_ref, os_ref, ob_ref, osb_ref) = rest[:-4], rest[-4:]
    x1 = _merge_rows(ya_ref[...], yb_ref[...], x_ref[...], *weights)
    o_ref[...] = x1
    ob_ref[...] = x1.astype(BF16)

    @pl.when(pl.program_id(0) == pl.num_programs(0) - 1)
    def _():
        x1s = _merge_rows(yas_ref[...], ybs_ref[...].astype(BF16), xs_ref[...], *weights)
        os_ref[...] = x1s
        osb_ref[...] = x1s.astype(BF16)


def _merge(ya, yb, x, ya_s, yb_s, x_s, wt_gab, w_glu, b_glu, w_a_up, w_b_up, w_out, ln_g, ln_b, tm):
    r = x.shape[0]
    rs = x_s.shape[0]
    const2 = lambda i: (0, 0)
    resident = functools.partial(pl.BlockSpec, index_map=const2, pipeline_mode=pl.Buffered(1))
    return pl.pallas_call(
        _merge_kernel,
        grid=(r // tm,),
        in_specs=[pl.BlockSpec((tm, SSM_WIDTH), lambda i: (i, 0)),
                  pl.BlockSpec((tm, HEADS * DV), lambda i: (i, 0)),
                  pl.BlockSpec((tm, D_MODEL), lambda i: (i, 0)),
                  resident((rs, SSM_WIDTH)),
                  resident((rs, HEADS * DV)),
                  resident((rs, D_MODEL)),
                  resident((2 * D_MODEL, D_MODEL)),
                  resident((SSM_WIDTH, SSM_WIDTH)),
                  resident((1, SSM_WIDTH)),
                  resident((SSM_WIDTH, D_MODEL)),
                  resident((HEADS * DV, D_MODEL)),
                  resident((D_MODEL, D_MODEL)),
                  resident((1, D_MODEL)),
                  resident((1, D_MODEL))],
        out_specs=[pl.BlockSpec((tm, D_MODEL), lambda i: (i, 0)),
                   pl.BlockSpec((rs, D_MODEL), const2),
                   pl.BlockSpec((tm, D_MODEL), lambda i: (i, 0)),
                   pl.BlockSpec((rs, D_MODEL), const2)],
        out_shape=[jax.ShapeDtypeStruct((r, D_MODEL), F32), jax.ShapeDtypeStruct((rs, D_MODEL), F32),
                   jax.ShapeDtypeStruct((r, D_MODEL), BF16), jax.ShapeDtypeStruct((rs, D_MODEL), BF16)],
        compiler_params=_cparams(1, 59),
        name="merge_ln1",
    )(ya, yb, x, ya_s, yb_s, x_s, wt_gab, w_glu, b_glu, w_a_up, w_b_up, w_out, ln_g, ln_b)


def _swiglu(xb, wg, wu):
    hg = jnp.dot(xb, wg, preferred_element_type=F32)
    hu = jnp.dot(xb, wu, preferred_element_type=F32)
    return ((hg * _sigmoid(hg)) * hu).astype(BF16)


FFN_UP_PIECE = 512


def _ffn_up_kernel(x_ref, xs_ref, wg_ref, wu_ref, wd_ref, h_ref, hs_ref, wdb_ref, wgb_s, wub_s):
    i = pl.program_id(1)

    @pl.when(i == 0)
    def _():
        wgb_s[...] = wg_ref[...].astype(BF16)
        wub_s[...] = wu_ref[...].astype(BF16)

    for r0 in range(0, x_ref.shape[0], FFN_UP_PIECE):
        rows = slice(r0, r0 + FFN_UP_PIECE)
        h_ref[rows, :] = _swiglu(x_ref[rows, :], wgb_s[...], wub_s[...])
    wdb_ref[...] = wd_ref[...].astype(BF16)

    @pl.when(i == pl.num_programs(1) - 1)
    def _():
        hs_ref[...] = _swiglu(xs_ref[...], wgb_s[...], wub_s[...])


def _ffn_up(xb, xsb, w_gate3, w_up3, w_down3, tm, tf):
    r = xb.shape[0]
    rs = xsb.shape[0]
    n_f, n_i = D_FF // tf, r // tm
    wd_rows = D_FF // (n_f * n_i)
    return pl.pallas_call(
        _ffn_up_kernel,
        grid=(n_f, n_i),
        in_specs=[pl.BlockSpec((tm, D_MODEL), lambda f, i: (i, 0)),
                  pl.BlockSpec((rs, D_MODEL), lambda f, i: (0, 0)),
                  pl.BlockSpec((None, D_MODEL, tf), lambda f, i: (0, 0, f)),
                  pl.BlockSpec((None, D_MODEL, tf), lambda f, i: (0, 0, f)),
                  pl.BlockSpec((None, wd_rows, D_MODEL), lambda f, i: (0, f * n_i + i, 0))],
        out_specs=[pl.BlockSpec((tm, tf), lambda f, i: (i, f)),
                   pl.BlockSpec((rs, tf), lambda f, i: (0, f)),
                   pl.BlockSpec((wd_rows, D_MODEL), lambda f, i: (f * n_i + i, 0))],
        out_shape=[jax.ShapeDtypeStruct((r, D_FF), BF16), jax.ShapeDtypeStruct((rs, D_FF), BF16),
                   jax.ShapeDtypeStruct((D_FF, D_MODEL), BF16)],
        scratch_shapes=[pltpu.VMEM((D_MODEL, tf), BF16), pltpu.VMEM((D_MODEL, tf), BF16)],
        compiler_params=_cparams(2, 58),
        name="ffn_up",
    )(xb, xsb, w_gate3, w_up3, w_down3)


FFN_DOWN_PIECE = 256


def _ffn_down_rows(h, x, wd_ref, g2_ref, b2_ref):
    return _layernorm_rows(ALPHA * x + jnp.dot(h, wd_ref[...], preferred_element_type=F32), g2_ref[...], b2_ref[...])


def _ffn_down_kernel(h_ref, x_ref, hs_ref, xs_ref, wd_ref, g2_ref, b2_ref, o_ref, os_ref):
    for r0 in range(0, h_ref.shape[0], FFN_DOWN_PIECE):
        rows = slice(r0, r0 + FFN_DOWN_PIECE)
        o_ref[rows, :] = _ffn_down_rows(h_ref[rows, :], x_ref[rows, :], wd_ref, g2_ref, b2_ref)

    @pl.when(pl.program_id(0) == pl.num_programs(0) - 1)
    def _():
        os_ref[...] = _ffn_down_rows(hs_ref[...], xs_ref[...], wd_ref, g2_ref, b2_ref)


def _ffn_down(h, x, h_s, x_s, wd_b, ln_g, ln_b, tm):
    r = x.shape[0]
    rs = x_s.shape[0]
    const2 = lambda i: (0, 0)
    resident = functools.partial(pl.BlockSpec, index_map=const2, pipeline_mode=pl.Buffered(1))
    return pl.pallas_call(
        _ffn_down_kernel,
        grid=(r // tm,),
        in_specs=[pl.BlockSpec((tm, D_FF), lambda i: (i, 0)),
                  pl.BlockSpec((tm, D_MODEL), lambda i: (i, 0)),
                  resident((rs, D_FF)),
                  resident((rs, D_MODEL)),
                  resident((D_FF, D_MODEL)),
                  resident((1, D_MODEL)),
                  resident((1, D_MODEL))],
        out_specs=[pl.BlockSpec((tm, D_MODEL), lambda i: (i, 0)),
                   pl.BlockSpec((rs, D_MODEL), const2)],
        out_shape=[jax.ShapeDtypeStruct((r, D_MODEL), F32), jax.ShapeDtypeStruct((rs, D_MODEL), F32)],
        compiler_params=_cparams(1, 59),
        name="ffn_down_ln2",
    )(h, x, h_s, x_s, wd_b, ln_g, ln_b)


def kernel(x_prompt, x_sample, state_ssm_re, state_ssm_im, state_mlstm_c, state_mlstm_n, state_mlstm_m,
           meta_tokens, w_in, b_if, ssm_a_re, ssm_a_im, ssm_log_dt, ssm_b_re, ssm_b_im, ssm_c_re, ssm_c_im,
           ssm_d, w_glu, b_glu, w_a_up, mh_gain, w_b_up, w_out, ln1_g, ln1_b, w_gate, w_up, w_down,
           ln2_g, ln2_b):
    batch, seq, _ = x_prompt.shape
    nb = x_sample.shape[0]
    width = HEADS * DK

    bias_row = b_if[0].reshape(1, 2 * HEADS)
    bias_col = b_if[0].reshape(2 * HEADS, 1)
    gain = mh_gain[0].reshape(1, width)
    d_skip = ssm_d[0].reshape(1, SSM_WIDTH)
    b_glu_r = b_glu[0].reshape(1, SSM_WIDTH)
    ln1 = (ln1_g[0].reshape(1, D_MODEL), ln1_b[0].reshape(1, D_MODEL))
    ln2 = (ln2_g[0].reshape(1, D_MODEL), ln2_b[0].reshape(1, D_MODEL))
    abar_re, abar_im, bt, ct = _s5_discretise(
        ssm_a_re[0], ssm_a_im[0], ssm_log_dt[0], ssm_b_re[0], ssm_b_im[0], ssm_c_re[0], ssm_c_im[0])

    xp = x_prompt.reshape(batch * seq, D_MODEL)
    xs = x_sample.reshape(nb, D_MODEL)
    xsm = jnp.concatenate([xs, meta_tokens], axis=0)
    w_in_t = jnp.swapaxes(w_in, 1, 2)
    uo_blocks, qkv_blocks = (0, 4, 2), (1, 1, 3)

    uo_p, uo_s, gcol_p, grow_p, gcol_s = _proj(xp, xsm, w_in_t, uo_blocks, F32, F32, 1024, "proj_uo", with_gates=True)
    qkv_p, qkv_s = _proj(xp, xsm, w_in_t, qkv_blocks, BF16, F32, 1024, "proj_qkv")

    u_meta = uo_s[nb:nb + N_META, :SSM_WIDTH]
    qkv_meta = qkv_s[nb:nb + N_META].astype(BF16)
    gcol_meta = gcol_s[nb:nb + N_META]

    ya_p, pf_re, pf_im = _s5_prompt(uo_p, 0, u_meta, abar_re, abar_im, bt, ct, d_skip, batch, seq)
    yb_p, pc, pn, pm, wt_gab, w_out_b, w_a_b, w_b_b, w_glu_b = _mlstm_prompt(
        qkv_p, uo_p, 1, gcol_p, grow_p, qkv_meta, gcol_meta, bias_row, bias_col, gain, batch, seq,
        w_in_t, (w_out, w_a_up, w_b_up, w_glu))
    ya_s, sf_re, sf_im = _s5_sample(uo_s[:nb, :SSM_WIDTH], state_ssm_re[0].reshape(nb, SSM_FLAT),
                                    state_ssm_im[0].reshape(nb, SSM_FLAT), abar_re, abar_im, bt, ct, d_skip)
    yb_s, sc, sn, sm = _mlstm_sample(qkv_s[:nb], uo_s[:nb, SSM_WIDTH:], gcol_s[:nb], state_mlstm_c[0],
                                     state_mlstm_n[0].reshape(nb, HEADS, 1, DK), state_mlstm_m[0], bias_row, gain)

    x1_p, x1_s, x1b_p, x1b_s = _merge(ya_p, yb_p.reshape(batch * seq, width), xp, ya_s, yb_s, xs,
                                      wt_gab, w_glu_b, b_glu_r, w_a_b, w_b_b, w_out_b, *ln1, tm=256)
    h_p, h_s, w_down_b = _ffn_up(x1b_p, x1b_s, w_gate, w_up, w_down, tm=1024, tf=512)
    y_p, y_s = _ffn_down(h_p, x1_p, h_s, x1_s, w_down_b, *ln2, tm=512)

    return (y_p.reshape(batch, seq, D_MODEL),
            y_s.reshape(nb, 1, D_MODEL),
            pf_re.reshape(1, batch, SSM_GROUPS, SSM_STATE),
            pf_im.reshape(1, batch, SSM_GROUPS, SSM_STATE),
            pc.reshape(1, batch, HEADS, DK, DV),
            pn.reshape(1, batch, HEADS, DK),
            pm[:, :, 0].reshape(1, batch, HEADS),
            sf_re.reshape(1, nb, SSM_GROUPS, SSM_STATE),
            sf_im.reshape(1, nb, SSM_GROUPS, SSM_STATE),
            sc.reshape(1, nb, HEADS, DK, DV),
            sn.reshape(1, nb, HEADS, DK),
            sm.reshape(1, nb, HEADS))
```

```python
import functools

import jax
import jax.numpy as jnp
import numpy as np
from jax import lax
from jax.experimental import pallas as pl
from jax.experimental.pallas import tpu as pltpu

F32 = jnp.float32
BF16 = jnp.bfloat16

D_MODEL = 2048
N_META = 16
SSM_WIDTH = 1024
SSM_GROUP = 16
SSM_GROUPS = 64
SSM_STATE = 64
SSM_FLAT = SSM_GROUPS * SSM_STATE
SSM_JBLK = 4
SSM_BLK_IN = SSM_WIDTH // SSM_JBLK
SSM_BLK_STATE = SSM_FLAT // SSM_JBLK
LANE_TILE = 128
HEADS = 4
DK = 256
DV = 256
CHUNK = 128
D_FF = 5632
EPS = 1e-5
ALPHA = 2.0 ** 0.25

S5_TC = 256
S5_NSEG = 8
S5_SEG = S5_TC // S5_NSEG
S5_LW = 512
S5_CPS = 2

V7X_SCOPED_VMEM_MAX_BYTES = 60000 * 1024


def _cparams(n_axes, vmem_mb):
    return pltpu.CompilerParams(
        dimension_semantics=("arbitrary",) * n_axes,
        vmem_limit_bytes=min(vmem_mb * 1024 * 1024, V7X_SCOPED_VMEM_MAX_BYTES),
    )


def _sigmoid(x):
    return 1.0 / (1.0 + jnp.exp(-x))


def _log_sigmoid(x):
    return jnp.minimum(x, 0.0) - jnp.log(1.0 + jnp.exp(-jnp.abs(x)))


def _gelu_tanh(x):
    c = 0.7978845608028654
    return 0.5 * x * (1.0 + jnp.tanh(c * (x + 0.044715 * (x * x * x))))


def _layernorm_rows(x, g, b):
    mu = jnp.mean(x, axis=-1, keepdims=True)
    xc = x - mu
    var = jnp.mean(xc * xc, axis=-1, keepdims=True)
    return xc * lax.rsqrt(var + EPS) * g + b


PROJ_TN = 1024
GATE_ROWS = 2 * HEADS
TAIL_ROW0 = SSM_WIDTH + 4 * HEADS * DK
NT_DIMS = (((1,), (1,)), ((), ()))


RING_SLOTS = 3


def _ring_copy(x_hbm, xbuf, sem, step, n_rows_blocks):
    tm = xbuf.shape[1]
    slot = lax.rem(step, RING_SLOTS)
    row0 = pl.multiple_of(lax.rem(step, n_rows_blocks) * tm, tm)
    return pltpu.make_async_copy(x_hbm.at[pl.ds(row0, tm), :], xbuf.at[slot], sem.at[slot])


def _ring_fetch(x_hbm, xbuf, sem, step, n_steps, n_rows_blocks):
    @pl.when(step == 0)
    def _():
        _ring_copy(x_hbm, xbuf, sem, step, n_rows_blocks).start()
        _ring_copy(x_hbm, xbuf, sem, step + 1, n_rows_blocks).start()

    @pl.when(step + 2 < n_steps)
    def _():
        _ring_copy(x_hbm, xbuf, sem, step + 2, n_rows_blocks).start()

    _ring_copy(x_hbm, xbuf, sem, step, n_rows_blocks).wait()
    return xbuf[lax.rem(step, RING_SLOTS)]


def _proj_kernel(with_gates, x_ref, xs_ref, wt_ref, *rest):
    i = pl.program_id(1)
    if with_gates:
        wg_ref, o_ref, os_ref, g_ref, gt_ref, gs_ref, wb_s = rest
        x_blk = x_ref[...]
    else:
        o_ref, os_ref, wb_s, xbuf, sem = rest
        n_i = pl.num_programs(1)
        x_blk = _ring_fetch(x_ref, xbuf, sem, pl.program_id(0) * n_i + i, pl.num_programs(0) * n_i, n_i)

    @pl.when(i == 0)
    def _():
        wb_s[...] = wt_ref[...].astype(BF16)

    xb = x_blk.astype(BF16)
    o_ref[...] = lax.dot_general(xb, wb_s[...], NT_DIMS, preferred_element_type=F32).astype(o_ref.dtype)
    first_block = pl.program_id(0) == 0
    if with_gates:
        @pl.when(first_block)
        def _():
            wg = wg_ref[...].astype(BF16)
            g_ref[...] = lax.dot_general(xb, wg, NT_DIMS, preferred_element_type=F32)
            gt_ref[...] = lax.dot_general(wg, xb, NT_DIMS, preferred_element_type=F32)

    @pl.when(i == pl.num_programs(1) - 1)
    def _():
        xsb = xs_ref[...].astype(BF16)
        os_ref[...] = lax.dot_general(xsb, wb_s[...], NT_DIMS, preferred_element_type=F32).astype(os_ref.dtype)
        if with_gates:
            @pl.when(first_block)
            def _():
                gs_ref[...] = lax.dot_general(xsb, wg_ref[...].astype(BF16), NT_DIMS, preferred_element_type=F32)


def _proj(x, xs, wt3, blocks, out_dtype, small_dtype, tm, name, with_gates=False):
    r, k = x.shape
    rs = xs.shape[0]
    first, step, count = blocks
    tn = PROJ_TN
    nm = r // tm
    scratch = [pltpu.VMEM((tn, k), BF16)]
    if with_gates:
        x_spec = pl.BlockSpec((tm, k), lambda j, i: (i, 0))
    else:
        x_spec = pl.BlockSpec(memory_space=pl.ANY)
        scratch += [pltpu.VMEM((RING_SLOTS, tm, k), x.dtype), pltpu.SemaphoreType.DMA((RING_SLOTS,))]
    in_specs = [x_spec,
                pl.BlockSpec((rs, k), lambda j, i: (0, 0)),
                pl.BlockSpec((None, tn, k), lambda j, i: (0, first + step * j, 0))]
    out_specs = [pl.BlockSpec((tm, tn), lambda j, i: (i, j)),
                 pl.BlockSpec((rs, tn), lambda j, i: (0, j))]
    out_shape = [jax.ShapeDtypeStruct((r, count * tn), out_dtype),
                 jax.ShapeDtypeStruct((rs, count * tn), small_dtype)]
    args = (x, xs, wt3)
    if with_gates:
        ng = GATE_ROWS
        in_specs += [pl.BlockSpec((None, ng, k), lambda j, i: (0, TAIL_ROW0 // ng, 0))]
        gate_block = lambda j, i: jnp.where(j == 0, i, nm - 1)
        out_specs += [pl.BlockSpec((tm, ng), lambda j, i: (gate_block(j, i), 0)),
                      pl.BlockSpec((ng, tm), lambda j, i: (0, gate_block(j, i))),
                      pl.BlockSpec((rs, ng), lambda j, i: (0, 0))]
        out_shape += [jax.ShapeDtypeStruct((r, ng), F32), jax.ShapeDtypeStruct((ng, r), F32),
                      jax.ShapeDtypeStruct((rs, ng), F32)]
        args = (x, xs, wt3, wt3)
    return pl.pallas_call(
        functools.partial(_proj_kernel, with_gates),
        grid=(count, nm),
        in_specs=in_specs,
        out_specs=out_specs,
        out_shape=out_shape,
        scratch_shapes=scratch,
        compiler_params=_cparams(2, 59),
        name=name,
    )(*args)


def _s5_discretise(a_re, a_im, log_dt, b_re, b_im, c_re, c_im):
    dt = jnp.exp(log_dt)
    e = jnp.exp(a_re * dt)
    abar_re = e * jnp.cos(a_im * dt)
    abar_im = e * jnp.sin(a_im * dt)
    nr = abar_re - 1.0
    ni = abar_im
    den = a_re * a_re + a_im * a_im
    coef_re = (nr * a_re + ni * a_im) / den
    coef_im = (ni * a_re - nr * a_im) / den
    bb_re = coef_re[..., None] * b_re - coef_im[..., None] * b_im
    bb_im = coef_re[..., None] * b_im + coef_im[..., None] * b_re
    gpb = SSM_GROUPS // SSM_JBLK

    def b_rows(bb):
        t = bb.reshape(SSM_JBLK, gpb, SSM_STATE, SSM_GROUP).transpose(0, 1, 3, 2)
        return t.reshape(SSM_JBLK, SSM_BLK_IN, SSM_STATE)

    def c_rows(cc):
        t = cc.reshape(SSM_JBLK, gpb, SSM_GROUP, SSM_STATE).transpose(0, 1, 3, 2)
        return t.reshape(SSM_JBLK, SSM_BLK_STATE, SSM_GROUP)

    b_rep, c_rep = LANE_TILE // SSM_STATE, LANE_TILE // SSM_GROUP
    bt = jnp.concatenate([jnp.tile(b_rows(bb_re), (1, 1, b_rep)), jnp.tile(b_rows(bb_im), (1, 1, b_rep))], axis=-1)
    ct = jnp.concatenate([jnp.tile(c_rows(c_re), (1, 1, c_rep)), jnp.tile(c_rows(-c_im), (1, 1, c_rep))], axis=-1)
    return abar_re.reshape(1, SSM_FLAT), abar_im.reshape(1, SSM_FLAT), bt.astype(BF16), ct.astype(BF16)


def _s5_expand_weights(bt_ref, ct_ref, bw_s, cwr_s, cwi_s):
    log2 = lambda n: n.bit_length() - 1
    rb = lax.broadcasted_iota(jnp.int32, (SSM_BLK_IN, 2 * SSM_BLK_STATE), 0)
    cb = lax.broadcasted_iota(jnp.int32, (SSM_BLK_IN, 2 * SSM_BLK_STATE), 1)
    bmask = (rb >> log2(SSM_GROUP)) == ((cb & (SSM_BLK_STATE - 1)) >> log2(SSM_STATE))
    rc = lax.broadcasted_iota(jnp.int32, (SSM_BLK_STATE, SSM_BLK_IN), 0)
    cc = lax.broadcasted_iota(jnp.int32, (SSM_BLK_STATE, SSM_BLK_IN), 1)
    cmask = (rc >> log2(SSM_STATE)) == (cc >> log2(SSM_GROUP))
    b_tiles, c_tiles = SSM_BLK_STATE // LANE_TILE, SSM_BLK_IN // LANE_TILE
    for j in range(SSM_JBLK):
        bt = bt_ref[j].astype(F32)
        wide = jnp.concatenate([bt[:, :LANE_TILE]] * b_tiles + [bt[:, LANE_TILE:]] * b_tiles, axis=1)
        bw_s[j] = jnp.where(bmask, wide, 0.0).astype(BF16)
        ct = ct_ref[j].astype(F32)
        cwr_s[j] = jnp.where(cmask, jnp.concatenate([ct[:, :LANE_TILE]] * c_tiles, axis=1), 0.0).astype(BF16)
        cwi_s[j] = jnp.where(cmask, jnp.concatenate([ct[:, LANE_TILE:]] * c_tiles, axis=1), 0.0).astype(BF16)


def _cmul_add(ar, ai, hr, hi, xr, xi):
    return ar * hr - ai * hi + xr, ar * hi + ai * hr + xi


def _s5_project_in(u, b_ref, hre_ref, him_ref, rows):
    for j in range(SSM_JBLK):
        state = slice(j * SSM_BLK_STATE, (j + 1) * SSM_BLK_STATE)
        bu = jnp.dot(u[:, j * SSM_BLK_IN:(j + 1) * SSM_BLK_IN], b_ref[j], preferred_element_type=F32)
        hre_ref[0:rows, state] = bu[:, :SSM_BLK_STATE]
        him_ref[0:rows, state] = bu[:, SSM_BLK_STATE:]


def _s5_project_out(hre_ref, him_ref, cre_ref, cimn_ref, rows):
    ys = []
    for j in range(SSM_JBLK):
        state = slice(j * SSM_BLK_STATE, (j + 1) * SSM_BLK_STATE)
        hr = hre_ref[0:rows, state].astype(BF16)
        hi = him_ref[0:rows, state].astype(BF16)
        y = jnp.dot(hr, cre_ref[j], preferred_element_type=F32)
        ys.append(y + jnp.dot(hi, cimn_ref[j], preferred_element_type=F32))
    return jnp.concatenate(ys, axis=1)


def _s5_scan_chunk(hre_ref, him_ref, hsre_ref, hsim_ref, pre_ref, pim_ref, are_ref, aim_ref, cre_s, cim_s):
    full = (S5_NSEG, S5_LW)
    for lg in range(SSM_FLAT // S5_LW):
        lanes = slice(lg * S5_LW, (lg + 1) * S5_LW)
        ar = jnp.broadcast_to(are_ref[:, lanes], full)
        ai = jnp.broadcast_to(aim_ref[:, lanes], full)

        hr = jnp.zeros(full, F32)
        hi = jnp.zeros(full, F32)
        for k in range(S5_SEG):
            rows = slice(k * S5_NSEG, (k + 1) * S5_NSEG)
            hr, hi = _cmul_add(ar, ai, hr, hi, hre_ref[rows, lanes], him_ref[rows, lanes])
            hre_ref[rows, lanes] = hr
            him_ref[rows, lanes] = hi
        er, ei = hr, hi

        asr = pre_ref[S5_TC - 1:S5_TC, lanes]
        asi = pim_ref[S5_TC - 1:S5_TC, lanes]
        hr = cre_s[:, lanes]
        hi = cim_s[:, lanes]
        for j in range(S5_NSEG):
            hsre_ref[j:j + 1, lanes] = hr
            hsim_ref[j:j + 1, lanes] = hi
            hr, hi = _cmul_add(asr, asi, hr, hi, er[j:j + 1, :], ei[j:j + 1, :])
        cre_s[:, lanes] = hr
        cim_s[:, lanes] = hi

        sr = hsre_ref[:, lanes]
        si = hsim_ref[:, lanes]
        for k in range(S5_SEG):
            rows = slice(k * S5_NSEG, (k + 1) * S5_NSEG)
            pr = pre_ref[rows, lanes]
            pi = pim_ref[rows, lanes]
            hre_ref[rows, lanes] = hre_ref[rows, lanes] + (pr * sr - pi * si)
            him_ref[rows, lanes] = him_ref[rows, lanes] + (pr * si + pi * sr)


N_CAST_IN = 6
N_CAST_OUT = 5


def _weight_cast_specs(n_steps, step_of, w_in_t, others):
    gab_rows = 2 * D_MODEL // n_steps
    gab_blk0 = TAIL_ROW0 // gab_rows
    gab_next0 = (TAIL_ROW0 + gab_rows) // GATE_ROWS
    rows = [w.shape[1] // n_steps for w in others]
    in_specs = [pl.BlockSpec((None, gab_rows, D_MODEL), lambda *g: (0, gab_blk0 + step_of(*g), 0)),
                pl.BlockSpec((None, GATE_ROWS, D_MODEL),
                             lambda *g: (0, gab_next0 + (gab_rows // GATE_ROWS) * step_of(*g), 0))]
    in_specs += [pl.BlockSpec((None, nr, w.shape[2]), lambda *g: (0, step_of(*g), 0)) for nr, w in zip(rows, others)]
    out_specs = [pl.BlockSpec((gab_rows, D_MODEL), lambda *g: (step_of(*g), 0))]
    out_specs += [pl.BlockSpec((nr, w.shape[2]), lambda *g: (step_of(*g), 0)) for nr, w in zip(rows, others)]
    out_shapes = [jax.ShapeDtypeStruct((2 * D_MODEL, D_MODEL), BF16)]
    out_shapes += [jax.ShapeDtypeStruct(w.shape[1:], BF16) for w in others]
    return in_specs, out_specs, out_shapes, (w_in_t, w_in_t) + tuple(others)


def _weight_cast_step(in_refs, out_refs):
    wga_ref, wgn_ref = in_refs[:2]
    out_refs[0][...] = jnp.concatenate([wga_ref[GATE_ROWS:, :], wgn_ref[...]], axis=0).astype(BF16)
    for src, dst in zip(in_refs[2:], out_refs[1:]):
        dst[...] = src[...].astype(BF16)


def _s5_prompt_kernel(u_ref, um_ref, perm_ref, permt_ref, bt_ref, ct_ref, are_ref, aim_ref, d_ref,
                      y_ref, fre_ref, fim_ref,
                      hre_ref, him_ref, hre2_ref, him2_ref, pre_ref, pim_ref, cre_s, cim_s,
                      hsre_ref, hsim_ref, hsre2_ref, hsim2_ref,
                      b_ref, cre_ref, cimn_ref):
    b_id = pl.program_id(0)
    c_id = pl.program_id(1)
    n_c = pl.num_programs(1)

    @pl.when(jnp.logical_and(b_id == 0, c_id == 0))
    def _():
        _s5_expand_weights(bt_ref, ct_ref, b_ref, cre_ref, cimn_ref)
        ar = jnp.broadcast_to(are_ref[...], (S5_NSEG, SSM_FLAT))
        ai = jnp.broadcast_to(aim_ref[...], (S5_NSEG, SSM_FLAT))
        pre_ref[0:S5_NSEG, :] = ar
        pim_ref[0:S5_NSEG, :] = ai

        def body(i, carry):
            pr, pi = carry
            nr = ar * pr - ai * pi
            ni = ar * pi + ai * pr
            rows = pl.ds(pl.multiple_of(i * S5_NSEG, S5_NSEG), S5_NSEG)
            pre_ref[rows, :] = nr
            pim_ref[rows, :] = ni
            return nr, ni

        lax.fori_loop(1, S5_SEG, body, (ar, ai))

    @pl.when(c_id == 0)
    def _():
        _s5_project_in(um_ref[...].astype(BF16), b_ref, hre_ref, him_ref, N_META)
        ar = are_ref[...]
        ai = aim_ref[...]
        hr = jnp.zeros((1, SSM_FLAT), F32)
        hi = jnp.zeros((1, SSM_FLAT), F32)
        for t in range(N_META):
            hr, hi = _cmul_add(ar, ai, hr, hi, hre_ref[t:t + 1, :], him_ref[t:t + 1, :])
        cre_s[...] = hr
        cim_s[...] = hi

    bufs = ((hre_ref, him_ref, hsre_ref, hsim_ref), (hre2_ref, him2_ref, hsre2_ref, hsim2_ref))
    us = [u_ref[n * S5_TC:(n + 1) * S5_TC, :] for n in range(S5_CPS)]
    for n in range(S5_CPS):
        u_perm = jnp.dot(perm_ref[...], us[n].astype(BF16), preferred_element_type=F32).astype(BF16)
        _s5_project_in(u_perm, b_ref, bufs[n][0], bufs[n][1], S5_TC)
    for n in range(S5_CPS):
        hre_n, him_n, hsre_n, hsim_n = bufs[n]
        _s5_scan_chunk(hre_n, him_n, hsre_n, hsim_n, pre_ref, pim_ref, are_ref, aim_ref, cre_s, cim_s)
        y_perm = _s5_project_out(hre_n, him_n, cre_ref, cimn_ref, S5_TC)
        y = sum(jnp.dot(permt_ref[...], piece, preferred_element_type=F32) for piece in _split3(y_perm))
        y_ref[n * S5_TC:(n + 1) * S5_TC, :] = y + d_ref[...] * us[n]

    @pl.when(c_id == n_c - 1)
    def _():
        fre_ref[0] = cre_s[...]
        fim_ref[0] = cim_s[...]


def _s5_weight_scratch():
    return [pltpu.VMEM((SSM_JBLK, SSM_BLK_IN, 2 * SSM_BLK_STATE), BF16),
            pltpu.VMEM((SSM_JBLK, SSM_BLK_STATE, SSM_BLK_IN), BF16),
            pltpu.VMEM((SSM_JBLK, SSM_BLK_STATE, SSM_BLK_IN), BF16)]


def _s5_prompt(u_arr, u_col_block, u_meta, abar_re, abar_im, bt, ct, d_skip, batch, seq):
    rows_step = S5_CPS * S5_TC
    n_c = seq // rows_step
    const3 = lambda b, c: (0, 0, 0)
    const2 = lambda b, c: (0, 0)
    rows = np.arange(S5_TC)
    perm_np = np.zeros((S5_TC, S5_TC), np.float32)
    perm_np[rows, (rows % S5_NSEG) * S5_SEG + rows // S5_NSEG] = 1.0
    perm = jnp.asarray(perm_np, dtype=BF16)
    permt = jnp.asarray(perm_np.T, dtype=BF16)
    return pl.pallas_call(
        _s5_prompt_kernel,
        grid=(batch, n_c),
        in_specs=[pl.BlockSpec((rows_step, SSM_WIDTH), lambda b, c: (b * n_c + c, u_col_block)),
                  pl.BlockSpec((N_META, SSM_WIDTH), const2),
                  pl.BlockSpec((S5_TC, S5_TC), const2),
                  pl.BlockSpec((S5_TC, S5_TC), const2),
                  pl.BlockSpec((SSM_JBLK, SSM_BLK_IN, 2 * LANE_TILE), const3, pipeline_mode=pl.Buffered(1)),
                  pl.BlockSpec((SSM_JBLK, SSM_BLK_STATE, 2 * LANE_TILE), const3, pipeline_mode=pl.Buffered(1)),
                  pl.BlockSpec((1, SSM_FLAT), const2),
                  pl.BlockSpec((1, SSM_FLAT), const2),
                  pl.BlockSpec((1, SSM_WIDTH), const2)],
        out_specs=[pl.BlockSpec((rows_step, SSM_WIDTH), lambda b, c: (b * n_c + c, 0)),
                   pl.BlockSpec((1, 1, SSM_FLAT), lambda b, c: (b, 0, 0)),
                   pl.BlockSpec((1, 1, SSM_FLAT), lambda b, c: (b, 0, 0))],
        out_shape=[jax.ShapeDtypeStruct((batch * seq, SSM_WIDTH), F32),
                   jax.ShapeDtypeStruct((batch, 1, SSM_FLAT), F32),
                   jax.ShapeDtypeStruct((batch, 1, SSM_FLAT), F32)],
        scratch_shapes=[pltpu.VMEM((S5_TC, SSM_FLAT), F32)] * 4
                       + [pltpu.VMEM((S5_TC, SSM_FLAT), F32)] * 2
                       + [pltpu.VMEM((1, SSM_FLAT), F32)] * 2
                       + [pltpu.VMEM((S5_NSEG, SSM_FLAT), F32)] * 4
                       + _s5_weight_scratch(),
        compiler_params=_cparams(2, 59),
        name="s5_prompt",
    )(u_arr, u_meta, perm, permt, bt, ct, abar_re, abar_im, d_skip)


def _s5_sample_kernel(u_ref, h0re_ref, h0im_ref, bt_ref, ct_ref, are_ref, aim_ref, d_ref,
                      y_ref, hre_ref, him_ref, b_ref, cre_ref, cimn_ref):
    rows = u_ref.shape[0]
    _s5_expand_weights(bt_ref, ct_ref, b_ref, cre_ref, cimn_ref)
    u = u_ref[...]
    _s5_project_in(u.astype(BF16), b_ref, hre_ref, him_ref, rows)
    nr, ni = _cmul_add(are_ref[...], aim_ref[...], h0re_ref[...], h0im_ref[...], hre_ref[...], him_ref[...])
    hre_ref[...] = nr
    him_ref[...] = ni
    y_ref[...] = _s5_project_out(hre_ref, him_ref, cre_ref, cimn_ref, rows) + d_ref[...] * u


def _s5_sample(u, h0_re, h0_im, abar_re, abar_im, bt, ct, d_skip):
    rows = u.shape[0]
    return pl.pallas_call(
        _s5_sample_kernel,
        out_shape=[jax.ShapeDtypeStruct((rows, SSM_WIDTH), F32),
                   jax.ShapeDtypeStruct((rows, SSM_FLAT), F32),
                   jax.ShapeDtypeStruct((rows, SSM_FLAT), F32)],
        scratch_shapes=_s5_weight_scratch(),
        compiler_params=_cparams(0, 48),
        name="s5_sample",
    )(u, h0_re, h0_im, bt, ct, abar_re, abar_im, d_skip)


def _split3(x):
    hi = x.astype(BF16)
    r1 = x - hi.astype(F32)
    mid = r1.astype(BF16)
    lo = (r1 - mid.astype(F32)).astype(BF16)
    return hi, mid, lo


def _tri(n, lower):
    r = lax.broadcasted_iota(jnp.int32, (n, n), 0)
    c = lax.broadcasted_iota(jnp.int32, (n, n), 1)
    return (r >= c) if lower else (r <= c)


def _gate_columns(gcol, bias_row):
    n = gcol.shape[0]
    g = gcol + bias_row
    fl = _log_sigmoid(g[:, HEADS:])
    tril = jnp.where(_tri(n, True), 1.0, 0.0).astype(BF16)
    b = sum(jnp.dot(tril, p, preferred_element_type=F32) for p in _split3(fl))
    return g[:, :HEADS], b


def _gate_rows(grow, bias_col):
    n = grow.shape[1]
    g = grow + bias_col
    fl = _log_sigmoid(g[HEADS:, :])
    triu = jnp.where(_tri(n, False), 1.0, 0.0).astype(BF16)
    b = sum(jnp.dot(p, triu, preferred_element_type=F32) for p in _split3(fl))
    return g[:HEADS, :], b


MLSTM_GROUP = 2


def _eye_bf16(n):
    return jnp.where(_tri(n, True) & _tri(n, False), 1.0, 0.0).astype(BF16)


def _mlstm_state_update(k_bf, v, ig_col, b_col, c_old, n_old, m_old, k_t=None, gate_rows=None):
    n_rows = k_bf.shape[0]
    k_scale = DK ** -0.5
    b_last = b_col[n_rows - 1:n_rows, :]
    e_col = b_last - b_col + ig_col
    m_new = jnp.maximum(b_last + m_old, jnp.max(e_col, axis=0, keepdims=True))
    w_end = jnp.exp(e_col - m_new) * k_scale
    carry = jnp.exp(b_last + m_old - m_new)
    if k_t is None:
        k_t = lax.dot_general(_eye_bf16(DK), k_bf, NT_DIMS, preferred_element_type=F32)
    if gate_rows is None:
        kv = jnp.dot(k_t.astype(BF16), (v.astype(F32) * w_end).astype(BF16), preferred_element_type=F32)
    else:
        ig_row, b_row = gate_rows
        w_end_row = jnp.exp(b_last - b_row + ig_row - m_new) * k_scale
        kv = jnp.dot((k_t * w_end_row).astype(BF16), v, preferred_element_type=F32)
    c_new = carry * c_old + kv
    n_new = carry * n_old + jnp.sum(k_bf.astype(F32) * w_end, axis=0, keepdims=True)
    return c_new, n_new, m_new


def _mlstm_decay(ig_row, b_row, b_col, m_old):
    n_rows = b_col.shape[0]
    dmat = jnp.where(_tri(n_rows, True), b_col + (ig_row - b_row), -jnp.inf)
    inter = b_col + m_old
    m_t = jnp.maximum(inter, jnp.max(dmat, axis=-1, keepdims=True))
    return jnp.exp(dmat - m_t), jnp.exp(inter - m_t), m_t


def _head_norm_gate(h, o, gain):
    mu = jnp.mean(h, axis=-1, keepdims=True)
    hc = h - mu
    var = jnp.mean(hc * hc, axis=-1, keepdims=True)
    return _sigmoid(o) * (hc * lax.rsqrt(var + EPS) * gain)


def _mlstm_prompt_kernel(batch, q_ref, k_ref, v_ref, o_ref, gc_ref, *rest):
    gr_refs = rest[:batch]
    rest = rest[batch:]
    km_ref, vm_ref, gcm_ref, brow_ref, bcol_ref, gain_ref = rest[:6]
    cast_in = rest[6:6 + N_CAST_IN]
    y_ref, cf_ref, nf_ref, mf_ref = rest[6 + N_CAST_IN:10 + N_CAST_IN]
    cast_out = rest[10 + N_CAST_IN:]
    k_scale = DK ** -0.5

    _weight_cast_step(cast_in, cast_out)

    @pl.when(pl.program_id(0) == 0)
    def _():
        ig_c, b_c = _gate_columns(gcm_ref[...], brow_ref[...])
        for h in range(HEADS):
            hs = slice(h * DK, (h + 1) * DK)
            c_new, n_new, m_new = _mlstm_state_update(
                km_ref[:, hs], vm_ref[:, hs], ig_c[:, h:h + 1], b_c[:, h:h + 1],
                jnp.zeros((DK, DV), F32), jnp.zeros((1, DK), F32), jnp.zeros((1, 1), F32))
            for b in range(batch):
                cf_ref[b, h] = c_new
                nf_ref[b, h] = n_new
                mf_ref[b, h:h + 1, :] = jnp.broadcast_to(m_new, (1, LANE_TILE))

    eye = _eye_bf16(DK)
    for b0 in range(0, batch, MLSTM_GROUP):
        gates = {}
        for b in range(b0, b0 + MLSTM_GROUP):
            gates[b] = _gate_columns(gc_ref[b], brow_ref[...]) + _gate_rows(gr_refs[b][...], bcol_ref[...])
        chains = [(b, h) for b in range(b0, b0 + MLSTM_GROUP) for h in range(HEADS)]

        ops = []
        for b, h in chains:
            hs = slice(h * DK, (h + 1) * DK)
            q = q_ref[b, :, hs]
            k_bf = k_ref[b, :, hs]
            c_old = cf_ref[b, h]
            ops.append(dict(
                q=q, k_bf=k_bf, v=v_ref[b, :, hs], c_old=c_old, n_old=nf_ref[b, h], m_old=mf_ref[b, h:h + 1, 0:1],
                s_raw=lax.dot_general(q, k_bf, NT_DIMS, preferred_element_type=F32),
                qc=jnp.dot(q, c_old.astype(BF16), preferred_element_type=F32),
                k_t=lax.dot_general(eye, k_bf, NT_DIMS, preferred_element_type=F32)))

        for (b, h), c in zip(chains, ops):
            ig_c, b_c, ig_r, b_r = gates[b]
            c["w_intra"], c["w_inter"], c["m_t"] = _mlstm_decay(ig_r[h:h + 1, :], b_r[h:h + 1, :], b_c[:, h:h + 1],
                                                                c["m_old"])

        for (b, h), c in zip(chains, ops):
            hs = slice(h * DK, (h + 1) * DK)
            s = c["s_raw"] * (c["w_intra"] * k_scale)
            num = c["w_inter"] * c["qc"] + jnp.dot(s.astype(BF16), c["v"], preferred_element_type=F32)
            den = (c["w_inter"] * jnp.sum(c["q"].astype(F32) * c["n_old"], axis=-1, keepdims=True)
                   + jnp.sum(s, axis=-1, keepdims=True))
            hid = num / jnp.maximum(jnp.abs(den), jnp.exp(-c["m_t"]))
            y_ref[b, :, hs] = _head_norm_gate(hid, o_ref[b, :, hs], gain_ref[:, hs]).astype(y_ref.dtype)

        for (b, h), c in zip(chains, ops):
            ig_c, b_c, ig_r, b_r = gates[b]
            c_new, n_new, m_new = _mlstm_state_update(c["k_bf"], c["v"], ig_c[:, h:h + 1], b_c[:, h:h + 1],
                                                      c["c_old"], c["n_old"], c["m_old"], k_t=c["k_t"],
                                                      gate_rows=(ig_r[h:h + 1, :], b_r[h:h + 1, :]))
            cf_ref[b, h] = c_new
            nf_ref[b, h] = n_new
            mf_ref[b, h:h + 1, :] = jnp.broadcast_to(m_new, (1, LANE_TILE))


def _mlstm_prompt(qkv, o_arr, o_col_block, gcol, grow, qkv_meta, gcol_meta, bias_row, bias_col, gain, batch, seq,
                  w_in_t, cast_weights):
    n_c = seq // CHUNK
    width = HEADS * DK
    const2 = lambda c: (0, 0)
    cast_in, cast_out, cast_shapes, cast_args = _weight_cast_specs(n_c, lambda c: c, w_in_t, cast_weights)
    qkv3 = qkv.reshape(batch, seq, 3 * width)
    o3 = o_arr.reshape(batch, seq, o_arr.shape[1])
    gcol3 = gcol.reshape(gcol.shape[0] // seq, seq, 2 * HEADS)
    state0 = lambda c: (0, 0, 0, 0)
    return pl.pallas_call(
        functools.partial(_mlstm_prompt_kernel, batch),
        grid=(n_c,),
        in_specs=[pl.BlockSpec((batch, CHUNK, width), lambda c: (0, c, 0)),
                  pl.BlockSpec((batch, CHUNK, width), lambda c: (0, c, 1)),
                  pl.BlockSpec((batch, CHUNK, width), lambda c: (0, c, 2)),
                  pl.BlockSpec((batch, CHUNK, width), lambda c: (0, c, o_col_block)),
                  pl.BlockSpec((batch, CHUNK, 2 * HEADS), lambda c: (0, c, 0))]
                 + [pl.BlockSpec((2 * HEADS, CHUNK), functools.partial(lambda b, c: (0, b * n_c + c), b))
                    for b in range(batch)]
                 + [pl.BlockSpec((N_META, width), lambda c: (0, 1)),
                    pl.BlockSpec((N_META, width), lambda c: (0, 2)),
                    pl.BlockSpec((N_META, 2 * HEADS), const2),
                    pl.BlockSpec((1, 2 * HEADS), const2),
                    pl.BlockSpec((2 * HEADS, 1), const2),
                    pl.BlockSpec((1, width), const2)] + cast_in,
        out_specs=[pl.BlockSpec((batch, CHUNK, width), lambda c: (0, c, 0)),
                   pl.BlockSpec((batch, HEADS, DK, DV), state0),
                   pl.BlockSpec((batch, HEADS, 1, DK), state0),
                   pl.BlockSpec((batch, HEADS, LANE_TILE), lambda c: (0, 0, 0))] + cast_out,
        out_shape=[jax.ShapeDtypeStruct((batch, seq, width), BF16),
                   jax.ShapeDtypeStruct((batch, HEADS, DK, DV), F32),
                   jax.ShapeDtypeStruct((batch, HEADS, 1, DK), F32),
                   jax.ShapeDtypeStruct((batch, HEADS, LANE_TILE), F32)] + cast_shapes,
        compiler_params=_cparams(1, 52),
        name="mlstm_prompt",
    )(qkv3, qkv3, qkv3, o3, gcol3, *([grow] * batch), qkv_meta, qkv_meta, gcol_meta, bias_row, bias_col, gain,
      *cast_args)


MLSTM_SB = 8


def _mlstm_sample_kernel(qkv_ref, o_ref, g_ref, c_ref, n_ref, m_ref, brow_ref, gain_ref,
                         y_ref, co_ref, no_ref, mo_ref):
    sb = MLSTM_SB
    k_scale = DK ** -0.5
    eye = _eye_bf16(DK)
    width = HEADS * DK
    pad = 2 * sb
    row_id = lax.broadcasted_iota(jnp.int32, (pad, DV), 0)

    g_all = g_ref[...] + brow_ref[...]
    ig_all = g_all[:, :HEADS]
    fl_all = _log_sigmoid(g_all[:, HEADS:])
    m_old_all = m_ref[...]
    m_new_all = jnp.maximum(fl_all + m_old_all, ig_all)
    w_in_all = jnp.exp(ig_all - m_new_all)
    w_ca_all = jnp.exp(fl_all + m_old_all - m_new_all)
    floor_all = jnp.exp(-m_new_all)
    mo_ref[...] = m_new_all

    for h in range(HEADS):
        hs = slice(h * DK, (h + 1) * DK)
        qk_rows = jnp.concatenate([qkv_ref[:, hs], qkv_ref[:, width + h * DK:width + (h + 1) * DK] * k_scale],
                                  axis=0).astype(BF16)
        qk_cols = lax.dot_general(eye, qk_rows, NT_DIMS, preferred_element_type=F32).astype(BF16)
        for i in range(sb):
            w_in = w_in_all[i:i + 1, h:h + 1]
            w_ca = w_ca_all[i:i + 1, h:h + 1]
            q = qk_rows[i:i + 1, :]
            qf = q.astype(F32)
            k = qk_rows[sb + i:sb + i + 1, :].astype(F32)
            v = qkv_ref[i:i + 1, 2 * width + h * DK:2 * width + (h + 1) * DK].astype(BF16).astype(F32)
            c_old = c_ref[i, h]
            n_old = n_ref[i, h]
            qc = jnp.dot(jnp.broadcast_to(q, (pad, DK)), c_old.astype(BF16), preferred_element_type=F32)[0:1, :]
            v_rows = jnp.where(row_id == sb + i, jnp.broadcast_to(v * w_in, (pad, DV)), 0.0).astype(BF16)
            kv = jnp.dot(qk_cols, v_rows, preferred_element_type=F32)
            s = jnp.sum(qf * k, axis=-1, keepdims=True) * w_in
            num = w_ca * qc + s * v
            den = w_ca * jnp.sum(qf * n_old, axis=-1, keepdims=True) + s
            hid = num / jnp.maximum(jnp.abs(den), floor_all[i:i + 1, h:h + 1])
            co_ref[i, h] = w_ca * c_old + kv
            no_ref[i, h] = w_ca * n_old + k * w_in
            y_ref[i:i + 1, hs] = _head_norm_gate(hid, o_ref[i:i + 1, hs], gain_ref[:, hs])


def _mlstm_sample(qkv, o_arr, gcol, c0, n0, m0, bias_row, gain):
    nb = qkv.shape[0]
    width = HEADS * DK
    sb = MLSTM_SB
    const2 = lambda i: (0, 0)
    return pl.pallas_call(
        _mlstm_sample_kernel,
        grid=(nb // sb,),
        in_specs=[pl.BlockSpec((sb, 3 * width), lambda i: (i, 0)),
                  pl.BlockSpec((sb, width), lambda i: (i, 0)),
                  pl.BlockSpec((sb, 2 * HEADS), lambda i: (i, 0)),
                  pl.BlockSpec((sb, HEADS, DK, DV), lambda i: (i, 0, 0, 0)),
                  pl.BlockSpec((sb, HEADS, 1, DK), lambda i: (i, 0, 0, 0)),
                  pl.BlockSpec((sb, HEADS), lambda i: (i, 0)),
                  pl.BlockSpec((1, 2 * HEADS), const2),
                  pl.BlockSpec((1, width), const2)],
        out_specs=[pl.BlockSpec((sb, width), lambda i: (i, 0)),
                   pl.BlockSpec((sb, HEADS, DK, DV), lambda i: (i, 0, 0, 0)),
                   pl.BlockSpec((sb, HEADS, 1, DK), lambda i: (i, 0, 0, 0)),
                   pl.BlockSpec((sb, HEADS), lambda i: (i, 0))],
        out_shape=[jax.ShapeDtypeStruct((nb, width), F32),
                   jax.ShapeDtypeStruct((nb, HEADS, DK, DV), F32),
                   jax.ShapeDtypeStruct((nb, HEADS, 1, DK), F32),
                   jax.ShapeDtypeStruct((nb, HEADS), F32)],
        compiler_params=_cparams(1, 48),
        name="mlstm_sample",
    )(qkv, o_arr, gcol, c0, n0, m0, bias_row, gain)


def _merge_rows(ya, yb, x, wgab_ref, wglu_ref, bglu_ref, wa_ref, wb_ref, wout_ref, g1_ref, b1_ref):
    xb = x.astype(BF16)
    g = _gelu_tanh(ya)
    z = jnp.dot(g.astype(BF16), wglu_ref[...], preferred_element_type=F32) + bglu_ref[...]
    out_a = g * _sigmoid(z)
    up_a = jnp.dot(out_a.astype(BF16), wa_ref[...], preferred_element_type=F32)
    g_a = lax.dot_general(xb, wgab_ref[0:D_MODEL, :], NT_DIMS, preferred_element_type=F32)
    mix = _sigmoid(g_a) * up_a
    up_b = jnp.dot(yb, wb_ref[...], preferred_element_type=F32)
    g_b = lax.dot_general(xb, wgab_ref[D_MODEL:2 * D_MODEL, :], NT_DIMS, preferred_element_type=F32)
    mix = mix + _sigmoid(g_b) * up_b
    mo = jnp.dot(mix.astype(BF16), wout_ref[...], preferred_element_type=F32)
    return _layernorm_rows(ALPHA * x + mo, g1_ref[...], b1_ref[...])


def _merge_kernel(ya_ref, yb_ref, x_ref, yas_ref, ybs_ref, xs_ref, *rest):
    weights, (o_ref, os_ref, ob_ref, osb_ref) = rest[:-4], rest[-4:]
    x1 = _merge_rows(ya_ref[...], yb_ref[...], x_ref[...], *weights)
    o_ref[...] = x1
    ob_ref[...] = x1.astype(BF16)

    @pl.when(pl.program_id(0) == pl.num_programs(0) - 1)
    def _():
        x1s = _merge_rows(yas_ref[...], ybs_ref[...].astype(BF16), xs_ref[...], *weights)
        os_ref[...] = x1s
        osb_ref[...] = x1s.astype(BF16)


def _merge(ya, yb, x, ya_s, yb_s, x_s, wt_gab, w_glu, b_glu, w_a_up, w_b_up, w_out, ln_g, ln_b, tm):
    r = x.shape[0]
    rs = x_s.shape[0]
    const2 = lambda i: (0, 0)
    resident = functools.partial(pl.BlockSpec, index_map=const2, pipeline_mode=pl.Buffered(1))
    return pl.pallas_call(
        _merge_kernel,
        grid=(r // tm,),
        in_specs=[pl.BlockSpec((tm, SSM_WIDTH), lambda i: (i, 0)),
                  pl.BlockSpec((tm, HEADS * DV), lambda i: (i, 0)),
                  pl.BlockSpec((tm, D_MODEL), lambda i: (i, 0)),
                  resident((rs, SSM_WIDTH)),
                  resident((rs, HEADS * DV)),
                  resident((rs, D_MODEL)),
                  resident((2 * D_MODEL, D_MODEL)),
                  resident((SSM_WIDTH, SSM_WIDTH)),
                  resident((1, SSM_WIDTH)),
                  resident((SSM_WIDTH, D_MODEL)),
                  resident((HEADS * DV, D_MODEL)),
                  resident((D_MODEL, D_MODEL)),
                  resident((1, D_MODEL)),
                  resident((1, D_MODEL))],
        out_specs=[pl.BlockSpec((tm, D_MODEL), lambda i: (i, 0)),
                   pl.BlockSpec((rs, D_MODEL), const2),
                   pl.BlockSpec((tm, D_MODEL), lambda i: (i, 0)),
                   pl.BlockSpec((rs, D_MODEL), const2)],
        out_shape=[jax.ShapeDtypeStruct((r, D_MODEL), F32), jax.ShapeDtypeStruct((rs, D_MODEL), F32),
                   jax.ShapeDtypeStruct((r, D_MODEL), BF16), jax.ShapeDtypeStruct((rs, D_MODEL), BF16)],
        compiler_params=_cparams(1, 59),
        name="merge_ln1",
    )(ya, yb, x, ya_s, yb_s, x_s, wt_gab, w_glu, b_glu, w_a_up, w_b_up, w_out, ln_g, ln_b)


def _swiglu(xb, wg, wu):
    hg = jnp.dot(xb, wg, preferred_element_type=F32)
    hu = jnp.dot(xb, wu, preferred_element_type=F32)
    return ((hg * _sigmoid(hg)) * hu).astype(BF16)


FFN_UP_PIECE = 512


def _ffn_up_kernel(x_ref, xs_ref, wg_ref, wu_ref, wd_ref, h_ref, hs_ref, wdb_ref, wgb_s, wub_s):
    i = pl.program_id(1)

    @pl.when(i == 0)
    def _():
        wgb_s[...] = wg_ref[...].astype(BF16)
        wub_s[...] = wu_ref[...].astype(BF16)

    for r0 in range(0, x_ref.shape[0], FFN_UP_PIECE):
        rows = slice(r0, r0 + FFN_UP_PIECE)
        h_ref[rows, :] = _swiglu(x_ref[rows, :], wgb_s[...], wub_s[...])
    wdb_ref[...] = wd_ref[...].astype(BF16)

    @pl.when(i == pl.num_programs(1) - 1)
    def _():
        hs_ref[...] = _swiglu(xs_ref[...], wgb_s[...], wub_s[...])


def _ffn_up(xb, xsb, w_gate3, w_up3, w_down3, tm, tf):
    r = xb.shape[0]
    rs = xsb.shape[0]
    n_f, n_i = D_FF // tf, r // tm
    wd_rows = D_FF // (n_f * n_i)
    return pl.pallas_call(
        _ffn_up_kernel,
        grid=(n_f, n_i),
        in_specs=[pl.BlockSpec((tm, D_MODEL), lambda f, i: (i, 0)),
                  pl.BlockSpec((rs, D_MODEL), lambda f, i: (0, 0)),
                  pl.BlockSpec((None, D_MODEL, tf), lambda f, i: (0, 0, f)),
                  pl.BlockSpec((None, D_MODEL, tf), lambda f, i: (0, 0, f)),
                  pl.BlockSpec((None, wd_rows, D_MODEL), lambda f, i: (0, f * n_i + i, 0))],
        out_specs=[pl.BlockSpec((tm, tf), lambda f, i: (i, f)),
                   pl.BlockSpec((rs, tf), lambda f, i: (0, f)),
                   pl.BlockSpec((wd_rows, D_MODEL), lambda f, i: (f * n_i + i, 0))],
        out_shape=[jax.ShapeDtypeStruct((r, D_FF), BF16), jax.ShapeDtypeStruct((rs, D_FF), BF16),
                   jax.ShapeDtypeStruct((D_FF, D_MODEL), BF16)],
        scratch_shapes=[pltpu.VMEM((D_MODEL, tf), BF16), pltpu.VMEM((D_MODEL, tf), BF16)],
        compiler_params=_cparams(2, 58),
        name="ffn_up",
    )(xb, xsb, w_gate3, w_up3, w_down3)


FFN_DOWN_PIECE = 256


def _ffn_down_rows(h, x, wd_ref, g2_ref, b2_ref):
    return _layernorm_rows(ALPHA * x + jnp.dot(h, wd_ref[...], preferred_element_type=F32), g2_ref[...], b2_ref[...])


def _ffn_down_kernel(h_ref, x_ref, hs_ref, xs_ref, wd_ref, g2_ref, b2_ref, o_ref, os_ref):
    for r0 in range(0, h_ref.shape[0], FFN_DOWN_PIECE):
        rows = slice(r0, r0 + FFN_DOWN_PIECE)
        o_ref[rows, :] = _ffn_down_rows(h_ref[rows, :], x_ref[rows, :], wd_ref, g2_ref, b2_ref)

    @pl.when(pl.program_id(0) == pl.num_programs(0) - 1)
    def _():
        os_ref[...] = _ffn_down_rows(hs_ref[...], xs_ref[...], wd_ref, g2_ref, b2_ref)


def _ffn_down(h, x, h_s, x_s, wd_b, ln_g, ln_b, tm):
    r = x.shape[0]
    rs = x_s.shape[0]
    const2 = lambda i: (0, 0)
    resident = functools.partial(pl.BlockSpec, index_map=const2, pipeline_mode=pl.Buffered(1))
    return pl.pallas_call(
        _ffn_down_kernel,
        grid=(r // tm,),
        in_specs=[pl.BlockSpec((tm, D_FF), lambda i: (i, 0)),
                  pl.BlockSpec((tm, D_MODEL), lambda i: (i, 0)),
                  resident((rs, D_FF)),
                  resident((rs, D_MODEL)),
                  resident((D_FF, D_MODEL)),
                  resident((1, D_MODEL)),
                  resident((1, D_MODEL))],
        out_specs=[pl.BlockSpec((tm, D_MODEL), lambda i: (i, 0)),
                   pl.BlockSpec((rs, D_MODEL), const2)],
        out_shape=[jax.ShapeDtypeStruct((r, D_MODEL), F32), jax.ShapeDtypeStruct((rs, D_MODEL), F32)],
        compiler_params=_cparams(1, 59),
        name="ffn_down_ln2",
    )(h, x, h_s, x_s, wd_b, ln_g, ln_b)


def kernel(x_prompt, x_sample, state_ssm_re, state_ssm_im, state_mlstm_c, state_mlstm_n, state_mlstm_m,
           meta_tokens, w_in, b_if, ssm_a_re, ssm_a_im, ssm_log_dt, ssm_b_re, ssm_b_im, ssm_c_re, ssm_c_im,
           ssm_d, w_glu, b_glu, w_a_up, mh_gain, w_b_up, w_out, ln1_g, ln1_b, w_gate, w_up, w_down,
           ln2_g, ln2_b):
    batch, seq, _ = x_prompt.shape
    nb = x_sample.shape[0]
    width = HEADS * DK

    bias_row = b_if[0].reshape(1, 2 * HEADS)
    bias_col = b_if[0].reshape(2 * HEADS, 1)
    gain = mh_gain[0].reshape(1, width)
    d_skip = ssm_d[0].reshape(1, SSM_WIDTH)
    b_glu_r = b_glu[0].reshape(1, SSM_WIDTH)
    ln1 = (ln1_g[0].reshape(1, D_MODEL), ln1_b[0].reshape(1, D_MODEL))
    ln2 = (ln2_g[0].reshape(1, D_MODEL), ln2_b[0].reshape(1, D_MODEL))
    abar_re, abar_im, bt, ct = _s5_discretise(
        ssm_a_re[0], ssm_a_im[0], ssm_log_dt[0], ssm_b_re[0], ssm_b_im[0], ssm_c_re[0], ssm_c_im[0])

    xp = x_prompt.reshape(batch * seq, D_MODEL)
    xs = x_sample.reshape(nb, D_MODEL)
    xsm = jnp.concatenate([xs, meta_tokens], axis=0)
    w_in_t = jnp.swapaxes(w_in, 1, 2)
    uo_blocks, qkv_blocks = (0, 4, 2), (1, 1, 3)

    uo_p, uo_s, gcol_p, grow_p, gcol_s = _proj(xp, xsm, w_in_t, uo_blocks, F32, F32, 1024, "proj_uo", with_gates=True)
    qkv_p, qkv_s = _proj(xp, xsm, w_in_t, qkv_blocks, BF16, F32, 1024, "proj_qkv")

    u_meta = uo_s[nb:nb + N_META, :SSM_WIDTH]
    qkv_meta = qkv_s[nb:nb + N_META].astype(BF16)
    gcol_meta = gcol_s[nb:nb + N_META]

    ya_p, pf_re, pf_im = _s5_prompt(uo_p, 0, u_meta, abar_re, abar_im, bt, ct, d_skip, batch, seq)
    yb_p, pc, pn, pm, wt_gab, w_out_b, w_a_b, w_b_b, w_glu_b = _mlstm_prompt(
        qkv_p, uo_p, 1, gcol_p, grow_p, qkv_meta, gcol_meta, bias_row, bias_col, gain, batch, seq,
        w_in_t, (w_out, w_a_up, w_b_up, w_glu))
    ya_s, sf_re, sf_im = _s5_sample(uo_s[:nb, :SSM_WIDTH], state_ssm_re[0].reshape(nb, SSM_FLAT),
                                    state_ssm_im[0].reshape(nb, SSM_FLAT), abar_re, abar_im, bt, ct, d_skip)
    yb_s, sc, sn, sm = _mlstm_sample(qkv_s[:nb], uo_s[:nb, SSM_WIDTH:], gcol_s[:nb], state_mlstm_c[0],
                                     state_mlstm_n[0].reshape(nb, HEADS, 1, DK), state_mlstm_m[0], bias_row, gain)

    x1_p, x1_s, x1b_p, x1b_s = _merge(ya_p, yb_p.reshape(batch * seq, width), xp, ya_s, yb_s, xs,
                                      wt_gab, w_glu_b, b_glu_r, w_a_b, w_b_b, w_out_b, *ln1, tm=256)
    h_p, h_s, w_down_b = _ffn_up(x1b_p, x1b_s, w_gate, w_up, w_down, tm=1024, tf=512)
    y_p, y_s = _ffn_down(h_p, x1_p, h_s, x1_s, w_down_b, *ln2, tm=512)

    return (y_p.reshape(batch, seq, D_MODEL),
            y_s.reshape(nb, 1, D_MODEL),
            pf_re.reshape(1, batch, SSM_GROUPS, SSM_STATE),
            pf_im.reshape(1, batch, SSM_GROUPS, SSM_STATE),
            pc.reshape(1, batch, HEADS, DK, DV),
            pn.reshape(1, batch, HEADS, DK),
            pm[:, :, 0].reshape(1, batch, HEADS),
            sf_re.reshape(1, nb, SSM_GROUPS, SSM_STATE),
            sf_im.reshape(1, nb, SSM_GROUPS, SSM_STATE),
            sc.reshape(1, nb, HEADS, DK, DV),
            sn.reshape(1, nb, HEADS, DK),
            sm.reshape(1, nb, HEADS))
```

```python
import functools

import jax
import jax.numpy as jnp
import numpy as np
from jax import lax
from jax.experimental import pallas as pl
from jax.experimental.pallas import tpu as pltpu

F32 = jnp.float32
BF16 = jnp.bfloat16

D_MODEL = 2048
N_META = 16
SSM_WIDTH = 1024
SSM_GROUP = 16
SSM_GROUPS = 64
SSM_STATE = 64
SSM_FLAT = SSM_GROUPS * SSM_STATE
SSM_JBLK = 4
SSM_BLK_IN = SSM_WIDTH // SSM_JBLK
SSM_BLK_STATE = SSM_FLAT // SSM_JBLK
LANE_TILE = 128
HEADS = 4
DK = 256
DV = 256
CHUNK = 128
D_FF = 5632
EPS = 1e-5
ALPHA = 2.0 ** 0.25

S5_TC = 256
S5_NSEG = 8
S5_SEG = S5_TC // S5_NSEG
S5_LW = 512
S5_CPS = 2

V7X_SCOPED_VMEM_MAX_BYTES = 60000 * 1024


def _cparams(n_axes, vmem_mb):
    return pltpu.CompilerParams(
        dimension_semantics=("arbitrary",) * n_axes,
        vmem_limit_bytes=min(vmem_mb * 1024 * 1024, V7X_SCOPED_VMEM_MAX_BYTES),
    )


def _sigmoid(x):
    return 1.0 / (1.0 + jnp.exp(-x))


def _log_sigmoid(x):
    return jnp.minimum(x, 0.0) - jnp.log(1.0 + jnp.exp(-jnp.abs(x)))


def _gelu_tanh(x):
    c = 0.7978845608028654
    return 0.5 * x * (1.0 + jnp.tanh(c * (x + 0.044715 * (x * x * x))))


def _layernorm_rows(x, g, b):
    mu = jnp.mean(x, axis=-1, keepdims=True)
    xc = x - mu
    var = jnp.mean(xc * xc, axis=-1, keepdims=True)
    return xc * lax.rsqrt(var + EPS) * g + b


PROJ_TN = 1024
GATE_ROWS = 2 * HEADS
TAIL_ROW0 = SSM_WIDTH + 4 * HEADS * DK
NT_DIMS = (((1,), (1,)), ((), ()))


def _proj_kernel(with_gates, x_ref, xs_ref, wt_ref, *rest):
    if with_gates:
        wg_ref, o_ref, os_ref, g_ref, gt_ref, gs_ref, wb_s = rest
    else:
        o_ref, os_ref, wb_s = rest
    i = pl.program_id(1)

    @pl.when(i == 0)
    def _():
        wb_s[...] = wt_ref[...].astype(BF16)

    xb = x_ref[...].astype(BF16)
    o_ref[...] = lax.dot_general(xb, wb_s[...], NT_DIMS, preferred_element_type=F32).astype(o_ref.dtype)
    first_block = pl.program_id(0) == 0
    if with_gates:
        @pl.when(first_block)
        def _():
            wg = wg_ref[...].astype(BF16)
            g_ref[...] = lax.dot_general(xb, wg, NT_DIMS, preferred_element_type=F32)
            gt_ref[...] = lax.dot_general(wg, xb, NT_DIMS, preferred_element_type=F32)

    @pl.when(i == pl.num_programs(1) - 1)
    def _():
        xsb = xs_ref[...].astype(BF16)
        os_ref[...] = lax.dot_general(xsb, wb_s[...], NT_DIMS, preferred_element_type=F32).astype(os_ref.dtype)
        if with_gates:
            @pl.when(first_block)
            def _():
                gs_ref[...] = lax.dot_general(xsb, wg_ref[...].astype(BF16), NT_DIMS, preferred_element_type=F32)


def _proj(x, xs, wt3, blocks, out_dtype, small_dtype, tm, name, with_gates=False):
    r, k = x.shape
    rs = xs.shape[0]
    first, step, count = blocks
    tn = PROJ_TN
    nm = r // tm
    in_specs = [pl.BlockSpec((tm, k), lambda j, i: (i, 0)),
                pl.BlockSpec((rs, k), lambda j, i: (0, 0)),
                pl.BlockSpec((None, tn, k), lambda j, i: (0, first + step * j, 0))]
    out_specs = [pl.BlockSpec((tm, tn), lambda j, i: (i, j)),
                 pl.BlockSpec((rs, tn), lambda j, i: (0, j))]
    out_shape = [jax.ShapeDtypeStruct((r, count * tn), out_dtype),
                 jax.ShapeDtypeStruct((rs, count * tn), small_dtype)]
    args = (x, xs, wt3)
    if with_gates:
        ng = GATE_ROWS
        in_specs += [pl.BlockSpec((None, ng, k), lambda j, i: (0, TAIL_ROW0 // ng, 0))]
        gate_block = lambda j, i: jnp.where(j == 0, i, nm - 1)
        out_specs += [pl.BlockSpec((tm, ng), lambda j, i: (gate_block(j, i), 0)),
                      pl.BlockSpec((ng, tm), lambda j, i: (0, gate_block(j, i))),
                      pl.BlockSpec((rs, ng), lambda j, i: (0, 0))]
        out_shape += [jax.ShapeDtypeStruct((r, ng), F32), jax.ShapeDtypeStruct((ng, r), F32),
                      jax.ShapeDtypeStruct((rs, ng), F32)]
        args = (x, xs, wt3, wt3)
    return pl.pallas_call(
        functools.partial(_proj_kernel, with_gates),
        grid=(count, nm),
        in_specs=in_specs,
        out_specs=out_specs,
        out_shape=out_shape,
        scratch_shapes=[pltpu.VMEM((tn, k), BF16)],
        compiler_params=_cparams(2, 56),
        name=name,
    )(*args)


def _s5_discretise(a_re, a_im, log_dt, b_re, b_im, c_re, c_im):
    dt = jnp.exp(log_dt)
    e = jnp.exp(a_re * dt)
    abar_re = e * jnp.cos(a_im * dt)
    abar_im = e * jnp.sin(a_im * dt)
    nr = abar_re - 1.0
    ni = abar_im
    den = a_re * a_re + a_im * a_im
    coef_re = (nr * a_re + ni * a_im) / den
    coef_im = (ni * a_re - nr * a_im) / den
    bb_re = coef_re[..., None] * b_re - coef_im[..., None] * b_im
    bb_im = coef_re[..., None] * b_im + coef_im[..., None] * b_re
    gpb = SSM_GROUPS // SSM_JBLK

    def b_rows(bb):
        t = bb.reshape(SSM_JBLK, gpb, SSM_STATE, SSM_GROUP).transpose(0, 1, 3, 2)
        return t.reshape(SSM_JBLK, SSM_BLK_IN, SSM_STATE)

    def c_rows(cc):
        t = cc.reshape(SSM_JBLK, gpb, SSM_GROUP, SSM_STATE).transpose(0, 1, 3, 2)
        return t.reshape(SSM_JBLK, SSM_BLK_STATE, SSM_GROUP)

    b_rep, c_rep = LANE_TILE // SSM_STATE, LANE_TILE // SSM_GROUP
    bt = jnp.concatenate([jnp.tile(b_rows(bb_re), (1, 1, b_rep)), jnp.tile(b_rows(bb_im), (1, 1, b_rep))], axis=-1)
    ct = jnp.concatenate([jnp.tile(c_rows(c_re), (1, 1, c_rep)), jnp.tile(c_rows(-c_im), (1, 1, c_rep))], axis=-1)
    return abar_re.reshape(1, SSM_FLAT), abar_im.reshape(1, SSM_FLAT), bt.astype(BF16), ct.astype(BF16)


def _s5_expand_weights(bt_ref, ct_ref, bw_s, cwr_s, cwi_s):
    log2 = lambda n: n.bit_length() - 1
    rb = lax.broadcasted_iota(jnp.int32, (SSM_BLK_IN, 2 * SSM_BLK_STATE), 0)
    cb = lax.broadcasted_iota(jnp.int32, (SSM_BLK_IN, 2 * SSM_BLK_STATE), 1)
    bmask = (rb >> log2(SSM_GROUP)) == ((cb & (SSM_BLK_STATE - 1)) >> log2(SSM_STATE))
    rc = lax.broadcasted_iota(jnp.int32, (SSM_BLK_STATE, SSM_BLK_IN), 0)
    cc = lax.broadcasted_iota(jnp.int32, (SSM_BLK_STATE, SSM_BLK_IN), 1)
    cmask = (rc >> log2(SSM_STATE)) == (cc >> log2(SSM_GROUP))
    b_tiles, c_tiles = SSM_BLK_STATE // LANE_TILE, SSM_BLK_IN // LANE_TILE
    for j in range(SSM_JBLK):
        bt = bt_ref[j].astype(F32)
        wide = jnp.concatenate([bt[:, :LANE_TILE]] * b_tiles + [bt[:, LANE_TILE:]] * b_tiles, axis=1)
        bw_s[j] = jnp.where(bmask, wide, 0.0).astype(BF16)
        ct = ct_ref[j].astype(F32)
        cwr_s[j] = jnp.where(cmask, jnp.concatenate([ct[:, :LANE_TILE]] * c_tiles, axis=1), 0.0).astype(BF16)
        cwi_s[j] = jnp.where(cmask, jnp.concatenate([ct[:, LANE_TILE:]] * c_tiles, axis=1), 0.0).astype(BF16)


def _cmul_add(ar, ai, hr, hi, xr, xi):
    return ar * hr - ai * hi + xr, ar * hi + ai * hr + xi


def _s5_project_in(u, b_ref, hre_ref, him_ref, rows):
    for j in range(SSM_JBLK):
        state = slice(j * SSM_BLK_STATE, (j + 1) * SSM_BLK_STATE)
        bu = jnp.dot(u[:, j * SSM_BLK_IN:(j + 1) * SSM_BLK_IN], b_ref[j], preferred_element_type=F32)
        hre_ref[0:rows, state] = bu[:, :SSM_BLK_STATE]
        him_ref[0:rows, state] = bu[:, SSM_BLK_STATE:]


def _s5_project_out(hre_ref, him_ref, cre_ref, cimn_ref, rows):
    ys = []
    for j in range(SSM_JBLK):
        state = slice(j * SSM_BLK_STATE, (j + 1) * SSM_BLK_STATE)
        hr = hre_ref[0:rows, state].astype(BF16)
        hi = him_ref[0:rows, state].astype(BF16)
        y = jnp.dot(hr, cre_ref[j], preferred_element_type=F32)
        ys.append(y + jnp.dot(hi, cimn_ref[j], preferred_element_type=F32))
    return jnp.concatenate(ys, axis=1)


def _s5_scan_chunk(hre_ref, him_ref, hsre_ref, hsim_ref, pre_ref, pim_ref, are_ref, aim_ref, cre_s, cim_s):
    full = (S5_NSEG, S5_LW)
    for lg in range(SSM_FLAT // S5_LW):
        lanes = slice(lg * S5_LW, (lg + 1) * S5_LW)
        ar = jnp.broadcast_to(are_ref[:, lanes], full)
        ai = jnp.broadcast_to(aim_ref[:, lanes], full)

        hr = jnp.zeros(full, F32)
        hi = jnp.zeros(full, F32)
        for k in range(S5_SEG):
            rows = slice(k * S5_NSEG, (k + 1) * S5_NSEG)
            hr, hi = _cmul_add(ar, ai, hr, hi, hre_ref[rows, lanes], him_ref[rows, lanes])
            hre_ref[rows, lanes] = hr
            him_ref[rows, lanes] = hi
        er, ei = hr, hi

        asr = pre_ref[S5_TC - 1:S5_TC, lanes]
        asi = pim_ref[S5_TC - 1:S5_TC, lanes]
        hr = cre_s[:, lanes]
        hi = cim_s[:, lanes]
        for j in range(S5_NSEG):
            hsre_ref[j:j + 1, lanes] = hr
            hsim_ref[j:j + 1, lanes] = hi
            hr, hi = _cmul_add(asr, asi, hr, hi, er[j:j + 1, :], ei[j:j + 1, :])
        cre_s[:, lanes] = hr
        cim_s[:, lanes] = hi

        sr = hsre_ref[:, lanes]
        si = hsim_ref[:, lanes]
        for k in range(S5_SEG):
            rows = slice(k * S5_NSEG, (k + 1) * S5_NSEG)
            pr = pre_ref[rows, lanes]
            pi = pim_ref[rows, lanes]
            hre_ref[rows, lanes] = hre_ref[rows, lanes] + (pr * sr - pi * si)
            him_ref[rows, lanes] = him_ref[rows, lanes] + (pr * si + pi * sr)


N_CAST_IN = 6
N_CAST_OUT = 5


def _weight_cast_specs(n_steps, step_of, w_in_t, others):
    gab_rows = 2 * D_MODEL // n_steps
    gab_blk0 = TAIL_ROW0 // gab_rows
    gab_next0 = (TAIL_ROW0 + gab_rows) // GATE_ROWS
    rows = [w.shape[1] // n_steps for w in others]
    in_specs = [pl.BlockSpec((None, gab_rows, D_MODEL), lambda *g: (0, gab_blk0 + step_of(*g), 0)),
                pl.BlockSpec((None, GATE_ROWS, D_MODEL),
                             lambda *g: (0, gab_next0 + (gab_rows // GATE_ROWS) * step_of(*g), 0))]
    in_specs += [pl.BlockSpec((None, nr, w.shape[2]), lambda *g: (0, step_of(*g), 0)) for nr, w in zip(rows, others)]
    out_specs = [pl.BlockSpec((gab_rows, D_MODEL), lambda *g: (step_of(*g), 0))]
    out_specs += [pl.BlockSpec((nr, w.shape[2]), lambda *g: (step_of(*g), 0)) for nr, w in zip(rows, others)]
    out_shapes = [jax.ShapeDtypeStruct((2 * D_MODEL, D_MODEL), BF16)]
    out_shapes += [jax.ShapeDtypeStruct(w.shape[1:], BF16) for w in others]
    return in_specs, out_specs, out_shapes, (w_in_t, w_in_t) + tuple(others)


def _weight_cast_step(in_refs, out_refs):
    wga_ref, wgn_ref = in_refs[:2]
    out_refs[0][...] = jnp.concatenate([wga_ref[GATE_ROWS:, :], wgn_ref[...]], axis=0).astype(BF16)
    for src, dst in zip(in_refs[2:], out_refs[1:]):
        dst[...] = src[...].astype(BF16)


def _s5_prompt_kernel(u_ref, um_ref, perm_ref, permt_ref, bt_ref, ct_ref, are_ref, aim_ref, d_ref,
                      y_ref, fre_ref, fim_ref,
                      hre_ref, him_ref, hre2_ref, him2_ref, pre_ref, pim_ref, cre_s, cim_s,
                      hsre_ref, hsim_ref, hsre2_ref, hsim2_ref,
                      b_ref, cre_ref, cimn_ref):
    b_id = pl.program_id(0)
    c_id = pl.program_id(1)
    n_c = pl.num_programs(1)

    @pl.when(jnp.logical_and(b_id == 0, c_id == 0))
    def _():
        _s5_expand_weights(bt_ref, ct_ref, b_ref, cre_ref, cimn_ref)
        ar = jnp.broadcast_to(are_ref[...], (S5_NSEG, SSM_FLAT))
        ai = jnp.broadcast_to(aim_ref[...], (S5_NSEG, SSM_FLAT))
        pre_ref[0:S5_NSEG, :] = ar
        pim_ref[0:S5_NSEG, :] = ai

        def body(i, carry):
            pr, pi = carry
            nr = ar * pr - ai * pi
            ni = ar * pi + ai * pr
            rows = pl.ds(pl.multiple_of(i * S5_NSEG, S5_NSEG), S5_NSEG)
            pre_ref[rows, :] = nr
            pim_ref[rows, :] = ni
            return nr, ni

        lax.fori_loop(1, S5_SEG, body, (ar, ai))

    @pl.when(c_id == 0)
    def _():
        _s5_project_in(um_ref[...].astype(BF16), b_ref, hre_ref, him_ref, N_META)
        ar = are_ref[...]
        ai = aim_ref[...]
        hr = jnp.zeros((1, SSM_FLAT), F32)
        hi = jnp.zeros((1, SSM_FLAT), F32)
        for t in range(N_META):
            hr, hi = _cmul_add(ar, ai, hr, hi, hre_ref[t:t + 1, :], him_ref[t:t + 1, :])
        cre_s[...] = hr
        cim_s[...] = hi

    bufs = ((hre_ref, him_ref, hsre_ref, hsim_ref), (hre2_ref, him2_ref, hsre2_ref, hsim2_ref))
    us = [u_ref[n * S5_TC:(n + 1) * S5_TC, :] for n in range(S5_CPS)]
    for n in range(S5_CPS):
        u_perm = jnp.dot(perm_ref[...], us[n].astype(BF16), preferred_element_type=F32).astype(BF16)
        _s5_project_in(u_perm, b_ref, bufs[n][0], bufs[n][1], S5_TC)
    for n in range(S5_CPS):
        hre_n, him_n, hsre_n, hsim_n = bufs[n]
        _s5_scan_chunk(hre_n, him_n, hsre_n, hsim_n, pre_ref, pim_ref, are_ref, aim_ref, cre_s, cim_s)
        y_perm = _s5_project_out(hre_n, him_n, cre_ref, cimn_ref, S5_TC)
        y = sum(jnp.dot(permt_ref[...], piece, preferred_element_type=F32) for piece in _split3(y_perm))
        y_ref[n * S5_TC:(n + 1) * S5_TC, :] = y + d_ref[...] * us[n]

    @pl.when(c_id == n_c - 1)
    def _():
        fre_ref[0] = cre_s[...]
        fim_ref[0] = cim_s[...]


def _s5_weight_scratch():
    return [pltpu.VMEM((SSM_JBLK, SSM_BLK_IN, 2 * SSM_BLK_STATE), BF16),
            pltpu.VMEM((SSM_JBLK, SSM_BLK_STATE, SSM_BLK_IN), BF16),
            pltpu.VMEM((SSM_JBLK, SSM_BLK_STATE, SSM_BLK_IN), BF16)]


def _s5_prompt(u_arr, u_col_block, u_meta, abar_re, abar_im, bt, ct, d_skip, batch, seq):
    rows_step = S5_CPS * S5_TC
    n_c = seq // rows_step
    const3 = lambda b, c: (0, 0, 0)
    const2 = lambda b, c: (0, 0)
    rows = np.arange(S5_TC)
    perm_np = np.zeros((S5_TC, S5_TC), np.float32)
    perm_np[rows, (rows % S5_NSEG) * S5_SEG + rows // S5_NSEG] = 1.0
    perm = jnp.asarray(perm_np, dtype=BF16)
    permt = jnp.asarray(perm_np.T, dtype=BF16)
    return pl.pallas_call(
        _s5_prompt_kernel,
        grid=(batch, n_c),
        in_specs=[pl.BlockSpec((rows_step, SSM_WIDTH), lambda b, c: (b * n_c + c, u_col_block)),
                  pl.BlockSpec((N_META, SSM_WIDTH), const2),
                  pl.BlockSpec((S5_TC, S5_TC), const2),
                  pl.BlockSpec((S5_TC, S5_TC), const2),
                  pl.BlockSpec((SSM_JBLK, SSM_BLK_IN, 2 * LANE_TILE), const3, pipeline_mode=pl.Buffered(1)),
                  pl.BlockSpec((SSM_JBLK, SSM_BLK_STATE, 2 * LANE_TILE), const3, pipeline_mode=pl.Buffered(1)),
                  pl.BlockSpec((1, SSM_FLAT), const2),
                  pl.BlockSpec((1, SSM_FLAT), const2),
                  pl.BlockSpec((1, SSM_WIDTH), const2)],
        out_specs=[pl.BlockSpec((rows_step, SSM_WIDTH), lambda b, c: (b * n_c + c, 0)),
                   pl.BlockSpec((1, 1, SSM_FLAT), lambda b, c: (b, 0, 0)),
                   pl.BlockSpec((1, 1, SSM_FLAT), lambda b, c: (b, 0, 0))],
        out_shape=[jax.ShapeDtypeStruct((batch * seq, SSM_WIDTH), F32),
                   jax.ShapeDtypeStruct((batch, 1, SSM_FLAT), F32),
                   jax.ShapeDtypeStruct((batch, 1, SSM_FLAT), F32)],
        scratch_shapes=[pltpu.VMEM((S5_TC, SSM_FLAT), F32)] * 4
                       + [pltpu.VMEM((S5_TC, SSM_FLAT), F32)] * 2
                       + [pltpu.VMEM((1, SSM_FLAT), F32)] * 2
                       + [pltpu.VMEM((S5_NSEG, SSM_FLAT), F32)] * 4
                       + _s5_weight_scratch(),
        compiler_params=_cparams(2, 59),
        name="s5_prompt",
    )(u_arr, u_meta, perm, permt, bt, ct, abar_re, abar_im, d_skip)


def _s5_sample_kernel(u_ref, h0re_ref, h0im_ref, bt_ref, ct_ref, are_ref, aim_ref, d_ref,
                      y_ref, hre_ref, him_ref, b_ref, cre_ref, cimn_ref):
    rows = u_ref.shape[0]
    _s5_expand_weights(bt_ref, ct_ref, b_ref, cre_ref, cimn_ref)
    u = u_ref[...]
    _s5_project_in(u.astype(BF16), b_ref, hre_ref, him_ref, rows)
    nr, ni = _cmul_add(are_ref[...], aim_ref[...], h0re_ref[...], h0im_ref[...], hre_ref[...], him_ref[...])
    hre_ref[...] = nr
    him_ref[...] = ni
    y_ref[...] = _s5_project_out(hre_ref, him_ref, cre_ref, cimn_ref, rows) + d_ref[...] * u


def _s5_sample(u, h0_re, h0_im, abar_re, abar_im, bt, ct, d_skip):
    rows = u.shape[0]
    return pl.pallas_call(
        _s5_sample_kernel,
        out_shape=[jax.ShapeDtypeStruct((rows, SSM_WIDTH), F32),
                   jax.ShapeDtypeStruct((rows, SSM_FLAT), F32),
                   jax.ShapeDtypeStruct((rows, SSM_FLAT), F32)],
        scratch_shapes=_s5_weight_scratch(),
        compiler_params=_cparams(0, 48),
        name="s5_sample",
    )(u, h0_re, h0_im, bt, ct, abar_re, abar_im, d_skip)


def _split3(x):
    hi = x.astype(BF16)
    r1 = x - hi.astype(F32)
    mid = r1.astype(BF16)
    lo = (r1 - mid.astype(F32)).astype(BF16)
    return hi, mid, lo


def _tri(n, lower):
    r = lax.broadcasted_iota(jnp.int32, (n, n), 0)
    c = lax.broadcasted_iota(jnp.int32, (n, n), 1)
    return (r >= c) if lower else (r <= c)


def _gate_columns(gcol, bias_row):
    n = gcol.shape[0]
    g = gcol + bias_row
    fl = _log_sigmoid(g[:, HEADS:])
    tril = jnp.where(_tri(n, True), 1.0, 0.0).astype(BF16)
    b = sum(jnp.dot(tril, p, preferred_element_type=F32) for p in _split3(fl))
    return g[:, :HEADS], b


def _gate_rows(grow, bias_col):
    n = grow.shape[1]
    g = grow + bias_col
    fl = _log_sigmoid(g[HEADS:, :])
    triu = jnp.where(_tri(n, False), 1.0, 0.0).astype(BF16)
    b = sum(jnp.dot(p, triu, preferred_element_type=F32) for p in _split3(fl))
    return g[:HEADS, :], b


MLSTM_GROUP = 2


def _eye_bf16(n):
    return jnp.where(_tri(n, True) & _tri(n, False), 1.0, 0.0).astype(BF16)


def _mlstm_state_update(k_bf, v, ig_col, b_col, c_old, n_old, m_old, k_t=None, gate_rows=None):
    n_rows = k_bf.shape[0]
    k_scale = DK ** -0.5
    b_last = b_col[n_rows - 1:n_rows, :]
    e_col = b_last - b_col + ig_col
    m_new = jnp.maximum(b_last + m_old, jnp.max(e_col, axis=0, keepdims=True))
    w_end = jnp.exp(e_col - m_new) * k_scale
    carry = jnp.exp(b_last + m_old - m_new)
    if k_t is None:
        k_t = lax.dot_general(_eye_bf16(DK), k_bf, NT_DIMS, preferred_element_type=F32)
    if gate_rows is None:
        kv = jnp.dot(k_t.astype(BF16), (v.astype(F32) * w_end).astype(BF16), preferred_element_type=F32)
    else:
        ig_row, b_row = gate_rows
        w_end_row = jnp.exp(b_last - b_row + ig_row - m_new) * k_scale
        kv = jnp.dot((k_t * w_end_row).astype(BF16), v, preferred_element_type=F32)
    c_new = carry * c_old + kv
    n_new = carry * n_old + jnp.sum(k_bf.astype(F32) * w_end, axis=0, keepdims=True)
    return c_new, n_new, m_new


def _mlstm_decay(ig_row, b_row, b_col, m_old):
    n_rows = b_col.shape[0]
    dmat = jnp.where(_tri(n_rows, True), b_col + (ig_row - b_row), -jnp.inf)
    inter = b_col + m_old
    m_t = jnp.maximum(inter, jnp.max(dmat, axis=-1, keepdims=True))
    return jnp.exp(dmat - m_t), jnp.exp(inter - m_t), m_t


def _head_norm_gate(h, o, gain):
    mu = jnp.mean(h, axis=-1, keepdims=True)
    hc = h - mu
    var = jnp.mean(hc * hc, axis=-1, keepdims=True)
    return _sigmoid(o) * (hc * lax.rsqrt(var + EPS) * gain)


def _mlstm_prompt_kernel(batch, q_ref, k_ref, v_ref, o_ref, gc_ref, *rest):
    gr_refs = rest[:batch]
    rest = rest[batch:]
    km_ref, vm_ref, gcm_ref, brow_ref, bcol_ref, gain_ref = rest[:6]
    cast_in = rest[6:6 + N_CAST_IN]
    y_ref, cf_ref, nf_ref, mf_ref = rest[6 + N_CAST_IN:10 + N_CAST_IN]
    cast_out = rest[10 + N_CAST_IN:]
    k_scale = DK ** -0.5

    _weight_cast_step(cast_in, cast_out)

    @pl.when(pl.program_id(0) == 0)
    def _():
        ig_c, b_c = _gate_columns(gcm_ref[...], brow_ref[...])
        for h in range(HEADS):
            hs = slice(h * DK, (h + 1) * DK)
            c_new, n_new, m_new = _mlstm_state_update(
                km_ref[:, hs], vm_ref[:, hs], ig_c[:, h:h + 1], b_c[:, h:h + 1],
                jnp.zeros((DK, DV), F32), jnp.zeros((1, DK), F32), jnp.zeros((1, 1), F32))
            for b in range(batch):
                cf_ref[b, h] = c_new
                nf_ref[b, h] = n_new
                mf_ref[b, h:h + 1, :] = jnp.broadcast_to(m_new, (1, LANE_TILE))

    eye = _eye_bf16(DK)
    for b0 in range(0, batch, MLSTM_GROUP):
        gates = {}
        for b in range(b0, b0 + MLSTM_GROUP):
            gates[b] = _gate_columns(gc_ref[b], brow_ref[...]) + _gate_rows(gr_refs[b][...], bcol_ref[...])
        chains = [(b, h) for b in range(b0, b0 + MLSTM_GROUP) for h in range(HEADS)]

        ops = []
        for b, h in chains:
            hs = slice(h * DK, (h + 1) * DK)
            q = q_ref[b, :, hs]
            k_bf = k_ref[b, :, hs]
            c_old = cf_ref[b, h]
            ops.append(dict(
                q=q, k_bf=k_bf, v=v_ref[b, :, hs], c_old=c_old, n_old=nf_ref[b, h], m_old=mf_ref[b, h:h + 1, 0:1],
                s_raw=lax.dot_general(q, k_bf, NT_DIMS, preferred_element_type=F32),
                qc=jnp.dot(q, c_old.astype(BF16), preferred_element_type=F32),
                k_t=lax.dot_general(eye, k_bf, NT_DIMS, preferred_element_type=F32)))

        for (b, h), c in zip(chains, ops):
            ig_c, b_c, ig_r, b_r = gates[b]
            c["w_intra"], c["w_inter"], c["m_t"] = _mlstm_decay(ig_r[h:h + 1, :], b_r[h:h + 1, :], b_c[:, h:h + 1],
                                                                c["m_old"])

        for (b, h), c in zip(chains, ops):
            hs = slice(h * DK, (h + 1) * DK)
            s = c["s_raw"] * (c["w_intra"] * k_scale)
            num = c["w_inter"] * c["qc"] + jnp.dot(s.astype(BF16), c["v"], preferred_element_type=F32)
            den = (c["w_inter"] * jnp.sum(c["q"].astype(F32) * c["n_old"], axis=-1, keepdims=True)
                   + jnp.sum(s, axis=-1, keepdims=True))
            hid = num / jnp.maximum(jnp.abs(den), jnp.exp(-c["m_t"]))
            y_ref[b, :, hs] = _head_norm_gate(hid, o_ref[b, :, hs], gain_ref[:, hs]).astype(y_ref.dtype)

        for (b, h), c in zip(chains, ops):
            ig_c, b_c, ig_r, b_r = gates[b]
            c_new, n_new, m_new = _mlstm_state_update(c["k_bf"], c["v"], ig_c[:, h:h + 1], b_c[:, h:h + 1],
                                                      c["c_old"], c["n_old"], c["m_old"], k_t=c["k_t"],
                                                      gate_rows=(ig_r[h:h + 1, :], b_r[h:h + 1, :]))
            cf_ref[b, h] = c_new
            nf_ref[b, h] = n_new
            mf_ref[b, h:h + 1, :] = jnp.broadcast_to(m_new, (1, LANE_TILE))


def _mlstm_prompt(qkv, o_arr, o_col_block, gcol, grow, qkv_meta, gcol_meta, bias_row, bias_col, gain, batch, seq,
                  w_in_t, cast_weights):
    n_c = seq // CHUNK
    width = HEADS * DK
    const2 = lambda c: (0, 0)
    cast_in, cast_out, cast_shapes, cast_args = _weight_cast_specs(n_c, lambda c: c, w_in_t, cast_weights)
    qkv3 = qkv.reshape(batch, seq, 3 * width)
    o3 = o_arr.reshape(batch, seq, o_arr.shape[1])
    gcol3 = gcol.reshape(gcol.shape[0] // seq, seq, 2 * HEADS)
    state0 = lambda c: (0, 0, 0, 0)
    return pl.pallas_call(
        functools.partial(_mlstm_prompt_kernel, batch),
        grid=(n_c,),
        in_specs=[pl.BlockSpec((batch, CHUNK, width), lambda c: (0, c, 0)),
                  pl.BlockSpec((batch, CHUNK, width), lambda c: (0, c, 1)),
                  pl.BlockSpec((batch, CHUNK, width), lambda c: (0, c, 2)),
                  pl.BlockSpec((batch, CHUNK, width), lambda c: (0, c, o_col_block)),
                  pl.BlockSpec((batch, CHUNK, 2 * HEADS), lambda c: (0, c, 0))]
                 + [pl.BlockSpec((2 * HEADS, CHUNK), functools.partial(lambda b, c: (0, b * n_c + c), b))
                    for b in range(batch)]
                 + [pl.BlockSpec((N_META, width), lambda c: (0, 1)),
                    pl.BlockSpec((N_META, width), lambda c: (0, 2)),
                    pl.BlockSpec((N_META, 2 * HEADS), const2),
                    pl.BlockSpec((1, 2 * HEADS), const2),
                    pl.BlockSpec((2 * HEADS, 1), const2),
                    pl.BlockSpec((1, width), const2)] + cast_in,
        out_specs=[pl.BlockSpec((batch, CHUNK, width), lambda c: (0, c, 0)),
                   pl.BlockSpec((batch, HEADS, DK, DV), state0),
                   pl.BlockSpec((batch, HEADS, 1, DK), state0),
                   pl.BlockSpec((batch, HEADS, LANE_TILE), lambda c: (0, 0, 0))] + cast_out,
        out_shape=[jax.ShapeDtypeStruct((batch, seq, width), BF16),
                   jax.ShapeDtypeStruct((batch, HEADS, DK, DV), F32),
                   jax.ShapeDtypeStruct((batch, HEADS, 1, DK), F32),
                   jax.ShapeDtypeStruct((batch, HEADS, LANE_TILE), F32)] + cast_shapes,
        compiler_params=_cparams(1, 52),
        name="mlstm_prompt",
    )(qkv3, qkv3, qkv3, o3, gcol3, *([grow] * batch), qkv_meta, qkv_meta, gcol_meta, bias_row, bias_col, gain,
      *cast_args)


MLSTM_SB = 8


def _mlstm_sample_kernel(qkv_ref, o_ref, g_ref, c_ref, n_ref, m_ref, brow_ref, gain_ref,
                         y_ref, co_ref, no_ref, mo_ref):
    sb = MLSTM_SB
    k_scale = DK ** -0.5
    eye = _eye_bf16(DK)
    width = HEADS * DK
    pad = 2 * sb
    row_id = lax.broadcasted_iota(jnp.int32, (pad, DV), 0)

    g_all = g_ref[...] + brow_ref[...]
    ig_all = g_all[:, :HEADS]
    fl_all = _log_sigmoid(g_all[:, HEADS:])
    m_old_all = m_ref[...]
    m_new_all = jnp.maximum(fl_all + m_old_all, ig_all)
    w_in_all = jnp.exp(ig_all - m_new_all)
    w_ca_all = jnp.exp(fl_all + m_old_all - m_new_all)
    floor_all = jnp.exp(-m_new_all)
    mo_ref[...] = m_new_all

    for h in range(HEADS):
        hs = slice(h * DK, (h + 1) * DK)
        qk_rows = jnp.concatenate([qkv_ref[:, hs], qkv_ref[:, width + h * DK:width + (h + 1) * DK] * k_scale],
                                  axis=0).astype(BF16)
        qk_cols = lax.dot_general(eye, qk_rows, NT_DIMS, preferred_element_type=F32).astype(BF16)
        for i in range(sb):
            w_in = w_in_all[i:i + 1, h:h + 1]
            w_ca = w_ca_all[i:i + 1, h:h + 1]
            q = qk_rows[i:i + 1, :]
            qf = q.astype(F32)
            k = qk_rows[sb + i:sb + i + 1, :].astype(F32)
            v = qkv_ref[i:i + 1, 2 * width + h * DK:2 * width + (h + 1) * DK].astype(BF16).astype(F32)
            c_old = c_ref[i, h]
            n_old = n_ref[i, h]
            qc = jnp.dot(jnp.broadcast_to(q, (pad, DK)), c_old.astype(BF16), preferred_element_type=F32)[0:1, :]
            v_rows = jnp.where(row_id == sb + i, jnp.broadcast_to(v * w_in, (pad, DV)), 0.0).astype(BF16)
            kv = jnp.dot(qk_cols, v_rows, preferred_element_type=F32)
            s = jnp.sum(qf * k, axis=-1, keepdims=True) * w_in
            num = w_ca * qc + s * v
            den = w_ca * jnp.sum(qf * n_old, axis=-1, keepdims=True) + s
            hid = num / jnp.maximum(jnp.abs(den), floor_all[i:i + 1, h:h + 1])
            co_ref[i, h] = w_ca * c_old + kv
            no_ref[i, h] = w_ca * n_old + k * w_in
            y_ref[i:i + 1, hs] = _head_norm_gate(hid, o_ref[i:i + 1, hs], gain_ref[:, hs])


def _mlstm_sample(qkv, o_arr, gcol, c0, n0, m0, bias_row, gain):
    nb = qkv.shape[0]
    width = HEADS * DK
    sb = MLSTM_SB
    const2 = lambda i: (0, 0)
    return pl.pallas_call(
        _mlstm_sample_kernel,
        grid=(nb // sb,),
        in_specs=[pl.BlockSpec((sb, 3 * width), lambda i: (i, 0)),
                  pl.BlockSpec((sb, width), lambda i: (i, 0)),
                  pl.BlockSpec((sb, 2 * HEADS), lambda i: (i, 0)),
                  pl.BlockSpec((sb, HEADS, DK, DV), lambda i: (i, 0, 0, 0)),
                  pl.BlockSpec((sb, HEADS, 1, DK), lambda i: (i, 0, 0, 0)),
                  pl.BlockSpec((sb, HEADS), lambda i: (i, 0)),
                  pl.BlockSpec((1, 2 * HEADS), const2),
                  pl.BlockSpec((1, width), const2)],
        out_specs=[pl.BlockSpec((sb, width), lambda i: (i, 0)),
                   pl.BlockSpec((sb, HEADS, DK, DV), lambda i: (i, 0, 0, 0)),
                   pl.BlockSpec((sb, HEADS, 1, DK), lambda i: (i, 0, 0, 0)),
                   pl.BlockSpec((sb, HEADS), lambda i: (i, 0))],
        out_shape=[jax.ShapeDtypeStruct((nb, width), F32),
                   jax.ShapeDtypeStruct((nb, HEADS, DK, DV), F32),
                   jax.ShapeDtypeStruct((nb, HEADS, 1, DK), F32),
                   jax.ShapeDtypeStruct((nb, HEADS), F32)],
        compiler_params=_cparams(1, 48),
        name="mlstm_sample",
    )(qkv, o_arr, gcol, c0, n0, m0, bias_row, gain)


def _merge_rows(ya, yb, x, wgab_ref, wglu_ref, bglu_ref, wa_ref, wb_ref, wout_ref, g1_ref, b1_ref):
    xb = x.astype(BF16)
    g = _gelu_tanh(ya)
    z = jnp.dot(g.astype(BF16), wglu_ref[...], preferred_element_type=F32) + bglu_ref[...]
    out_a = g * _sigmoid(z)
    up_a = jnp.dot(out_a.astype(BF16), wa_ref[...], preferred_element_type=F32)
    g_a = lax.dot_general(xb, wgab_ref[0:D_MODEL, :], NT_DIMS, preferred_element_type=F32)
    mix = _sigmoid(g_a) * up_a
    up_b = jnp.dot(yb, wb_ref[...], preferred_element_type=F32)
    g_b = lax.dot_general(xb, wgab_ref[D_MODEL:2 * D_MODEL, :], NT_DIMS, preferred_element_type=F32)
    mix = mix + _sigmoid(g_b) * up_b
    mo = jnp.dot(mix.astype(BF16), wout_ref[...], preferred_element_type=F32)
    return _layernorm_rows(ALPHA * x + mo, g1_ref[...], b1_ref[...])


def _merge_kernel(ya_ref, yb_ref, x_ref, yas_ref, ybs_ref, xs_ref, *rest):
    weights, (o_ref, os_ref, ob_ref, osb_ref) = rest[:-4], rest[-4:]
    x1 = _merge_rows(ya_ref[...], yb_ref[...], x_ref[...], *weights)
    o_ref[...] = x1
    ob_ref[...] = x1.astype(BF16)

    @pl.when(pl.program_id(0) == pl.num_programs(0) - 1)
    def _():
        x1s = _merge_rows(yas_ref[...], ybs_ref[...].astype(BF16), xs_ref[...], *weights)
        os_ref[...] = x1s
        osb_ref[...] = x1s.astype(BF16)


def _merge(ya, yb, x, ya_s, yb_s, x_s, wt_gab, w_glu, b_glu, w_a_up, w_b_up, w_out, ln_g, ln_b, tm):
    r = x.shape[0]
    rs = x_s.shape[0]
    const2 = lambda i: (0, 0)
    resident = functools.partial(pl.BlockSpec, index_map=const2, pipeline_mode=pl.Buffered(1))
    return pl.pallas_call(
        _merge_kernel,
        grid=(r // tm,),
        in_specs=[pl.BlockSpec((tm, SSM_WIDTH), lambda i: (i, 0)),
                  pl.BlockSpec((tm, HEADS * DV), lambda i: (i, 0)),
                  pl.BlockSpec((tm, D_MODEL), lambda i: (i, 0)),
                  resident((rs, SSM_WIDTH)),
                  resident((rs, HEADS * DV)),
                  resident((rs, D_MODEL)),
                  resident((2 * D_MODEL, D_MODEL)),
                  resident((SSM_WIDTH, SSM_WIDTH)),
                  resident((1, SSM_WIDTH)),
                  resident((SSM_WIDTH, D_MODEL)),
                  resident((HEADS * DV, D_MODEL)),
                  resident((D_MODEL, D_MODEL)),
                  resident((1, D_MODEL)),
                  resident((1, D_MODEL))],
        out_specs=[pl.BlockSpec((tm, D_MODEL), lambda i: (i, 0)),
                   pl.BlockSpec((rs, D_MODEL), const2),
                   pl.BlockSpec((tm, D_MODEL), lambda i: (i, 0)),
                   pl.BlockSpec((rs, D_MODEL), const2)],
        out_shape=[jax.ShapeDtypeStruct((r, D_MODEL), F32), jax.ShapeDtypeStruct((rs, D_MODEL), F32),
                   jax.ShapeDtypeStruct((r, D_MODEL), BF16), jax.ShapeDtypeStruct((rs, D_MODEL), BF16)],
        compiler_params=_cparams(1, 59),
        name="merge_ln1",
    )(ya, yb, x, ya_s, yb_s, x_s, wt_gab, w_glu, b_glu, w_a_up, w_b_up, w_out, ln_g, ln_b)


def _swiglu(xb, wg, wu):
    hg = jnp.dot(xb, wg, preferred_element_type=F32)
    hu = jnp.dot(xb, wu, preferred_element_type=F32)
    return ((hg * _sigmoid(hg)) * hu).astype(BF16)


FFN_UP_PIECE = 512


def _ffn_up_kernel(x_ref, xs_ref, wg_ref, wu_ref, wd_ref, h_ref, hs_ref, wdb_ref, wgb_s, wub_s):
    wgb_s[...] = wg_ref[...].astype(BF16)
    wub_s[...] = wu_ref[...].astype(BF16)

    for r0 in range(0, x_ref.shape[0], FFN_UP_PIECE):
        rows = slice(r0, r0 + FFN_UP_PIECE)
        h_ref[rows, :] = _swiglu(x_ref[rows, :], wgb_s[...], wub_s[...])
    wdb_ref[...] = wd_ref[...].astype(BF16)

    @pl.when(pl.program_id(0) == pl.num_programs(0) - 1)
    def _():
        hs_ref[...] = _swiglu(xs_ref[...], wgb_s[...], wub_s[...])


def _ffn_up(xb, xsb, w_gate3, w_up3, w_down3, tm, tf):
    r = xb.shape[0]
    rs = xsb.shape[0]
    n_f, n_i = D_FF // tf, r // tm
    wd_rows = D_FF // (n_f * n_i)
    hs_block = lambda i, f: jnp.where(i == n_i - 1, f, 0)
    return pl.pallas_call(
        _ffn_up_kernel,
        grid=(n_i, n_f),
        in_specs=[pl.BlockSpec((tm, D_MODEL), lambda i, f: (i, 0)),
                  pl.BlockSpec((rs, D_MODEL), lambda i, f: (0, 0)),
                  pl.BlockSpec((None, D_MODEL, tf), lambda i, f: (0, 0, f)),
                  pl.BlockSpec((None, D_MODEL, tf), lambda i, f: (0, 0, f)),
                  pl.BlockSpec((None, wd_rows, D_MODEL), lambda i, f: (0, i * n_f + f, 0))],
        out_specs=[pl.BlockSpec((tm, tf), lambda i, f: (i, f)),
                   pl.BlockSpec((rs, tf), lambda i, f: (0, hs_block(i, f))),
                   pl.BlockSpec((wd_rows, D_MODEL), lambda i, f: (i * n_f + f, 0))],
        out_shape=[jax.ShapeDtypeStruct((r, D_FF), BF16), jax.ShapeDtypeStruct((rs, D_FF), BF16),
                   jax.ShapeDtypeStruct((D_FF, D_MODEL), BF16)],
        scratch_shapes=[pltpu.VMEM((D_MODEL, tf), BF16), pltpu.VMEM((D_MODEL, tf), BF16)],
        compiler_params=_cparams(2, 58),
        name="ffn_up",
    )(xb, xsb, w_gate3, w_up3, w_down3)


FFN_DOWN_PIECE = 256


def _ffn_down_rows(h, x, wd_ref, g2_ref, b2_ref):
    return _layernorm_rows(ALPHA * x + jnp.dot(h, wd_ref[...], preferred_element_type=F32), g2_ref[...], b2_ref[...])


def _ffn_down_kernel(h_ref, x_ref, hs_ref, xs_ref, wd_ref, g2_ref, b2_ref, o_ref, os_ref):
    for r0 in range(0, h_ref.shape[0], FFN_DOWN_PIECE):
        rows = slice(r0, r0 + FFN_DOWN_PIECE)
        o_ref[rows, :] = _ffn_down_rows(h_ref[rows, :], x_ref[rows, :], wd_ref, g2_ref, b2_ref)

    @pl.when(pl.program_id(0) == pl.num_programs(0) - 1)
    def _():
        os_ref[...] = _ffn_down_rows(hs_ref[...], xs_ref[...], wd_ref, g2_ref, b2_ref)


def _ffn_down(h, x, h_s, x_s, wd_b, ln_g, ln_b, tm):
    r = x.shape[0]
    rs = x_s.shape[0]
    const2 = lambda i: (0, 0)
    resident = functools.partial(pl.BlockSpec, index_map=const2, pipeline_mode=pl.Buffered(1))
    return pl.pallas_call(
        _ffn_down_kernel,
        grid=(r // tm,),
        in_specs=[pl.BlockSpec((tm, D_FF), lambda i: (i, 0)),
                  pl.BlockSpec((tm, D_MODEL), lambda i: (i, 0)),
                  resident((rs, D_FF)),
                  resident((rs, D_MODEL)),
                  resident((D_FF, D_MODEL)),
                  resident((1, D_MODEL)),
                  resident((1, D_MODEL))],
        out_specs=[pl.BlockSpec((tm, D_MODEL), lambda i: (i, 0)),
                   pl.BlockSpec((rs, D_MODEL), const2)],
        out_shape=[jax.ShapeDtypeStruct((r, D_MODEL), F32), jax.ShapeDtypeStruct((rs, D_MODEL), F32)],
        compiler_params=_cparams(1, 59),
        name="ffn_down_ln2",
    )(h, x, h_s, x_s, wd_b, ln_g, ln_b)


def kernel(x_prompt, x_sample, state_ssm_re, state_ssm_im, state_mlstm_c, state_mlstm_n, state_mlstm_m,
           meta_tokens, w_in, b_if, ssm_a_re, ssm_a_im, ssm_log_dt, ssm_b_re, ssm_b_im, ssm_c_re, ssm_c_im,
           ssm_d, w_glu, b_glu, w_a_up, mh_gain, w_b_up, w_out, ln1_g, ln1_b, w_gate, w_up, w_down,
           ln2_g, ln2_b):
    batch, seq, _ = x_prompt.shape
    nb = x_sample.shape[0]
    width = HEADS * DK

    bias_row = b_if[0].reshape(1, 2 * HEADS)
    bias_col = b_if[0].reshape(2 * HEADS, 1)
    gain = mh_gain[0].reshape(1, width)
    d_skip = ssm_d[0].reshape(1, SSM_WIDTH)
    b_glu_r = b_glu[0].reshape(1, SSM_WIDTH)
    ln1 = (ln1_g[0].reshape(1, D_MODEL), ln1_b[0].reshape(1, D_MODEL))
    ln2 = (ln2_g[0].reshape(1, D_MODEL), ln2_b[0].reshape(1, D_MODEL))
    abar_re, abar_im, bt, ct = _s5_discretise(
        ssm_a_re[0], ssm_a_im[0], ssm_log_dt[0], ssm_b_re[0], ssm_b_im[0], ssm_c_re[0], ssm_c_im[0])

    xp = x_prompt.reshape(batch * seq, D_MODEL)
    xs = x_sample.reshape(nb, D_MODEL)
    xsm = jnp.concatenate([xs, meta_tokens], axis=0)
    w_in_t = jnp.swapaxes(w_in, 1, 2)
    uo_blocks, qkv_blocks = (0, 4, 2), (1, 1, 3)

    uo_p, uo_s, gcol_p, grow_p, gcol_s = _proj(xp, xsm, w_in_t, uo_blocks, F32, F32, 1024, "proj_uo", with_gates=True)
    qkv_p, qkv_s = _proj(xp, xsm, w_in_t, qkv_blocks, BF16, F32, 1024, "proj_qkv")

    u_meta = uo_s[nb:nb + N_META, :SSM_WIDTH]
    qkv_meta = qkv_s[nb:nb + N_META].astype(BF16)
    gcol_meta = gcol_s[nb:nb + N_META]

    ya_p, pf_re, pf_im = _s5_prompt(uo_p, 0, u_meta, abar_re, abar_im, bt, ct, d_skip, batch, seq)
    yb_p, pc, pn, pm, wt_gab, w_out_b, w_a_b, w_b_b, w_glu_b = _mlstm_prompt(
        qkv_p, uo_p, 1, gcol_p, grow_p, qkv_meta, gcol_meta, bias_row, bias_col, gain, batch, seq,
        w_in_t, (w_out, w_a_up, w_b_up, w_glu))
    ya_s, sf_re, sf_im = _s5_sample(uo_s[:nb, :SSM_WIDTH], state_ssm_re[0].reshape(nb, SSM_FLAT),
                                    state_ssm_im[0].reshape(nb, SSM_FLAT), abar_re, abar_im, bt, ct, d_skip)
    yb_s, sc, sn, sm = _mlstm_sample(qkv_s[:nb], uo_s[:nb, SSM_WIDTH:], gcol_s[:nb], state_mlstm_c[0],
                                     state_mlstm_n[0].reshape(nb, HEADS, 1, DK), state_mlstm_m[0], bias_row, gain)

    x1_p, x1_s, x1b_p, x1b_s = _merge(ya_p, yb_p.reshape(batch * seq, width), xp, ya_s, yb_s, xs,
                                      wt_gab, w_glu_b, b_glu_r, w_a_b, w_b_b, w_out_b, *ln1, tm=256)
    h_p, h_s, w_down_b = _ffn_up(x1b_p, x1b_s, w_gate, w_up, w_down, tm=2048, tf=512)
    y_p, y_s = _ffn_down(h_p, x1_p, h_s, x1_s, w_down_b, *ln2, tm=512)

    return (y_p.reshape(batch, seq, D_MODEL),
            y_s.reshape(nb, 1, D_MODEL),
            pf_re.reshape(1, batch, SSM_GROUPS, SSM_STATE),
            pf_im.reshape(1, batch, SSM_GROUPS, SSM_STATE),
            pc.reshape(1, batch, HEADS, DK, DV),
            pn.reshape(1, batch, HEADS, DK),
            pm[:, :, 0].reshape(1, batch, HEADS),
            sf_re.reshape(1, nb, SSM_GROUPS, SSM_STATE),
            sf_im.reshape(1, nb, SSM_GROUPS, SSM_STATE),
            sc.reshape(1, nb, HEADS, DK, DV),
            sn.reshape(1, nb, HEADS, DK),
            sm.reshape(1, nb, HEADS))
```

```python
import functools

import jax
import jax.numpy as jnp
import numpy as np
from jax import lax
from jax.experimental import pallas as pl
from jax.experimental.pallas import tpu as pltpu

F32 = jnp.float32
BF16 = jnp.bfloat16

D_MODEL = 2048
N_META = 16
SSM_WIDTH = 1024
SSM_GROUP = 16
SSM_GROUPS = 64
SSM_STATE = 64
SSM_FLAT = SSM_GROUPS * SSM_STATE
SSM_JBLK = 4
SSM_BLK_IN = SSM_WIDTH // SSM_JBLK
SSM_BLK_STATE = SSM_FLAT // SSM_JBLK
LANE_TILE = 128
HEADS = 4
DK = 256
DV = 256
CHUNK = 128
D_FF = 5632
EPS = 1e-5
ALPHA = 2.0 ** 0.25

S5_TC = 256
S5_NSEG = 8
S5_SEG = S5_TC // S5_NSEG
S5_LW = 512
S5_CPS = 2

V7X_SCOPED_VMEM_MAX_BYTES = 60000 * 1024


def _cparams(n_axes, vmem_mb):
    return pltpu.CompilerParams(
        dimension_semantics=("arbitrary",) * n_axes,
        vmem_limit_bytes=min(vmem_mb * 1024 * 1024, V7X_SCOPED_VMEM_MAX_BYTES),
    )


def _sigmoid(x):
    return 1.0 / (1.0 + jnp.exp(-x))


def _log_sigmoid(x):
    return jnp.minimum(x, 0.0) - jnp.log(1.0 + jnp.exp(-jnp.abs(x)))


def _gelu_tanh(x):
    c = 0.7978845608028654
    return 0.5 * x * (1.0 + jnp.tanh(c * (x + 0.044715 * (x * x * x))))


def _layernorm_rows(x, g, b):
    mu = jnp.mean(x, axis=-1, keepdims=True)
    xc = x - mu
    var = jnp.mean(xc * xc, axis=-1, keepdims=True)
    return xc * lax.rsqrt(var + EPS) * g + b


PROJ_TN = 1024
GATE_ROWS = 2 * HEADS
TAIL_ROW0 = SSM_WIDTH + 4 * HEADS * DK
NT_DIMS = (((1,), (1,)), ((), ()))


def _proj_kernel(with_gates, x_ref, xs_ref, wt_ref, *rest):
    if with_gates:
        wg_ref, o_ref, os_ref, g_ref, gt_ref, gs_ref, wb_s = rest
    else:
        o_ref, os_ref, wb_s = rest
    i = pl.program_id(1)

    @pl.when(i == 0)
    def _():
        wb_s[...] = wt_ref[...].astype(BF16)

    xb = x_ref[...].astype(BF16)
    o_ref[...] = lax.dot_general(xb, wb_s[...], NT_DIMS, preferred_element_type=F32).astype(o_ref.dtype)
    first_block = pl.program_id(0) == 0
    if with_gates:
        @pl.when(first_block)
        def _():
            wg = wg_ref[...].astype(BF16)
            g_ref[...] = lax.dot_general(xb, wg, NT_DIMS, preferred_element_type=F32)
            gt_ref[...] = lax.dot_general(wg, xb, NT_DIMS, preferred_element_type=F32)

    @pl.when(i == pl.num_programs(1) - 1)
    def _():
        xsb = xs_ref[...].astype(BF16)
        os_ref[...] = lax.dot_general(xsb, wb_s[...], NT_DIMS, preferred_element_type=F32).astype(os_ref.dtype)
        if with_gates:
            @pl.when(first_block)
            def _():
                gs_ref[...] = lax.dot_general(xsb, wg_ref[...].astype(BF16), NT_DIMS, preferred_element_type=F32)


def _proj(x, xs, wt3, blocks, out_dtype, small_dtype, tm, name, with_gates=False):
    r, k = x.shape
    rs = xs.shape[0]
    first, step, count = blocks
    tn = PROJ_TN
    nm = r // tm
    in_specs = [pl.BlockSpec((tm, k), lambda j, i: (i, 0)),
                pl.BlockSpec((rs, k), lambda j, i: (0, 0)),
                pl.BlockSpec((None, tn, k), lambda j, i: (0, first + step * j, 0))]
    out_specs = [pl.BlockSpec((tm, tn), lambda j, i: (i, j)),
                 pl.BlockSpec((rs, tn), lambda j, i: (0, j))]
    out_shape = [jax.ShapeDtypeStruct((r, count * tn), out_dtype),
                 jax.ShapeDtypeStruct((rs, count * tn), small_dtype)]
    args = (x, xs, wt3)
    if with_gates:
        ng = GATE_ROWS
        in_specs += [pl.BlockSpec((None, ng, k), lambda j, i: (0, TAIL_ROW0 // ng, 0))]
        gate_block = lambda j, i: jnp.where(j == 0, i, nm - 1)
        out_specs += [pl.BlockSpec((tm, ng), lambda j, i: (gate_block(j, i), 0)),
                      pl.BlockSpec((ng, tm), lambda j, i: (0, gate_block(j, i))),
                      pl.BlockSpec((rs, ng), lambda j, i: (0, 0))]
        out_shape += [jax.ShapeDtypeStruct((r, ng), F32), jax.ShapeDtypeStruct((ng, r), F32),
                      jax.ShapeDtypeStruct((rs, ng), F32)]
        args = (x, xs, wt3, wt3)
    return pl.pallas_call(
        functools.partial(_proj_kernel, with_gates),
        grid=(count, nm),
        in_specs=in_specs,
        out_specs=out_specs,
        out_shape=out_shape,
        scratch_shapes=[pltpu.VMEM((tn, k), BF16)],
        compiler_params=_cparams(2, 56),
        name=name,
    )(*args)


def _s5_discretise(a_re, a_im, log_dt, b_re, b_im, c_re, c_im):
    dt = jnp.exp(log_dt)
    e = jnp.exp(a_re * dt)
    abar_re = e * jnp.cos(a_im * dt)
    abar_im = e * jnp.sin(a_im * dt)
    nr = abar_re - 1.0
    ni = abar_im
    den = a_re * a_re + a_im * a_im
    coef_re = (nr * a_re + ni * a_im) / den
    coef_im = (ni * a_re - nr * a_im) / den
    bb_re = coef_re[..., None] * b_re - coef_im[..., None] * b_im
    bb_im = coef_re[..., None] * b_im + coef_im[..., None] * b_re
    gpb = SSM_GROUPS // SSM_JBLK

    def b_rows(bb):
        t = bb.reshape(SSM_JBLK, gpb, SSM_STATE, SSM_GROUP).transpose(0, 1, 3, 2)
        return t.reshape(SSM_JBLK, SSM_BLK_IN, SSM_STATE)

    def c_rows(cc):
        t = cc.reshape(SSM_JBLK, gpb, SSM_GROUP, SSM_STATE).transpose(0, 1, 3, 2)
        return t.reshape(SSM_JBLK, SSM_BLK_STATE, SSM_GROUP)

    b_rep, c_rep = LANE_TILE // SSM_STATE, LANE_TILE // SSM_GROUP
    bt = jnp.concatenate([jnp.tile(b_rows(bb_re), (1, 1, b_rep)), jnp.tile(b_rows(bb_im), (1, 1, b_rep))], axis=-1)
    ct = jnp.concatenate([jnp.tile(c_rows(c_re), (1, 1, c_rep)), jnp.tile(c_rows(-c_im), (1, 1, c_rep))], axis=-1)
    return abar_re.reshape(1, SSM_FLAT), abar_im.reshape(1, SSM_FLAT), bt.astype(BF16), ct.astype(BF16)


def _s5_expand_weights(bt_ref, ct_ref, bw_s, cwr_s, cwi_s):
    log2 = lambda n: n.bit_length() - 1
    rb = lax.broadcasted_iota(jnp.int32, (SSM_BLK_IN, 2 * SSM_BLK_STATE), 0)
    cb = lax.broadcasted_iota(jnp.int32, (SSM_BLK_IN, 2 * SSM_BLK_STATE), 1)
    bmask = (rb >> log2(SSM_GROUP)) == ((cb & (SSM_BLK_STATE - 1)) >> log2(SSM_STATE))
    rc = lax.broadcasted_iota(jnp.int32, (SSM_BLK_STATE, SSM_BLK_IN), 0)
    cc = lax.broadcasted_iota(jnp.int32, (SSM_BLK_STATE, SSM_BLK_IN), 1)
    cmask = (rc >> log2(SSM_STATE)) == (cc >> log2(SSM_GROUP))
    b_tiles, c_tiles = SSM_BLK_STATE // LANE_TILE, SSM_BLK_IN // LANE_TILE
    for j in range(SSM_JBLK):
        bt = bt_ref[j].astype(F32)
        wide = jnp.concatenate([bt[:, :LANE_TILE]] * b_tiles + [bt[:, LANE_TILE:]] * b_tiles, axis=1)
        bw_s[j] = jnp.where(bmask, wide, 0.0).astype(BF16)
        ct = ct_ref[j].astype(F32)
        cwr_s[j] = jnp.where(cmask, jnp.concatenate([ct[:, :LANE_TILE]] * c_tiles, axis=1), 0.0).astype(BF16)
        cwi_s[j] = jnp.where(cmask, jnp.concatenate([ct[:, LANE_TILE:]] * c_tiles, axis=1), 0.0).astype(BF16)


def _cmul_add(ar, ai, hr, hi, xr, xi):
    return ar * hr - ai * hi + xr, ar * hi + ai * hr + xi


def _s5_project_in(u, b_ref, hre_ref, him_ref, rows):
    for j in range(SSM_JBLK):
        state = slice(j * SSM_BLK_STATE, (j + 1) * SSM_BLK_STATE)
        bu = jnp.dot(u[:, j * SSM_BLK_IN:(j + 1) * SSM_BLK_IN], b_ref[j], preferred_element_type=F32)
        hre_ref[0:rows, state] = bu[:, :SSM_BLK_STATE]
        him_ref[0:rows, state] = bu[:, SSM_BLK_STATE:]


def _s5_project_out(hre_ref, him_ref, cre_ref, cimn_ref, rows):
    ys = []
    for j in range(SSM_JBLK):
        state = slice(j * SSM_BLK_STATE, (j + 1) * SSM_BLK_STATE)
        hr = hre_ref[0:rows, state].astype(BF16)
        hi = him_ref[0:rows, state].astype(BF16)
        y = jnp.dot(hr, cre_ref[j], preferred_element_type=F32)
        ys.append(y + jnp.dot(hi, cimn_ref[j], preferred_element_type=F32))
    return jnp.concatenate(ys, axis=1)


def _s5_scan_chunk(hre_ref, him_ref, hsre_ref, hsim_ref, pre_ref, pim_ref, are_ref, aim_ref, cre_s, cim_s):
    full = (S5_NSEG, S5_LW)
    for lg in range(SSM_FLAT // S5_LW):
        lanes = slice(lg * S5_LW, (lg + 1) * S5_LW)
        ar = jnp.broadcast_to(are_ref[:, lanes], full)
        ai = jnp.broadcast_to(aim_ref[:, lanes], full)

        hr = jnp.zeros(full, F32)
        hi = jnp.zeros(full, F32)
        for k in range(S5_SEG):
            rows = slice(k * S5_NSEG, (k + 1) * S5_NSEG)
            hr, hi = _cmul_add(ar, ai, hr, hi, hre_ref[rows, lanes], him_ref[rows, lanes])
            hre_ref[rows, lanes] = hr
            him_ref[rows, lanes] = hi
        er, ei = hr, hi

        asr = pre_ref[S5_TC - 1:S5_TC, lanes]
        asi = pim_ref[S5_TC - 1:S5_TC, lanes]
        hr = cre_s[:, lanes]
        hi = cim_s[:, lanes]
        for j in range(S5_NSEG):
            hsre_ref[j:j + 1, lanes] = hr
            hsim_ref[j:j + 1, lanes] = hi
            hr, hi = _cmul_add(asr, asi, hr, hi, er[j:j + 1, :], ei[j:j + 1, :])
        cre_s[:, lanes] = hr
        cim_s[:, lanes] = hi

        sr = hsre_ref[:, lanes]
        si = hsim_ref[:, lanes]
        for k in range(S5_SEG):
            rows = slice(k * S5_NSEG, (k + 1) * S5_NSEG)
            pr = pre_ref[rows, lanes]
            pi = pim_ref[rows, lanes]
            hre_ref[rows, lanes] = hre_ref[rows, lanes] + (pr * sr - pi * si)
            him_ref[rows, lanes] = him_ref[rows, lanes] + (pr * si + pi * sr)


N_CAST_IN = 6


def _weight_cast_specs(n_steps, step_of, w_in_t, others):
    gab_rows = 2 * D_MODEL // n_steps
    gab_blk0 = TAIL_ROW0 // gab_rows
    gab_next0 = (TAIL_ROW0 + gab_rows) // GATE_ROWS
    rows = [w.shape[1] // n_steps for w in others]
    in_specs = [pl.BlockSpec((None, gab_rows, D_MODEL), lambda *g: (0, gab_blk0 + step_of(*g), 0)),
                pl.BlockSpec((None, GATE_ROWS, D_MODEL),
                             lambda *g: (0, gab_next0 + (gab_rows // GATE_ROWS) * step_of(*g), 0))]
    in_specs += [pl.BlockSpec((None, nr, w.shape[2]), lambda *g: (0, step_of(*g), 0)) for nr, w in zip(rows, others)]
    out_specs = [pl.BlockSpec((gab_rows, D_MODEL), lambda *g: (step_of(*g), 0))]
    out_specs += [pl.BlockSpec((nr, w.shape[2]), lambda *g: (step_of(*g), 0)) for nr, w in zip(rows, others)]
    out_shapes = [jax.ShapeDtypeStruct((2 * D_MODEL, D_MODEL), BF16)]
    out_shapes += [jax.ShapeDtypeStruct(w.shape[1:], BF16) for w in others]
    return in_specs, out_specs, out_shapes, (w_in_t, w_in_t) + tuple(others)


def _weight_cast_step(in_refs, out_refs):
    wga_ref, wgn_ref = in_refs[:2]
    out_refs[0][...] = jnp.concatenate([wga_ref[GATE_ROWS:, :], wgn_ref[...]], axis=0).astype(BF16)
    for src, dst in zip(in_refs[2:], out_refs[1:]):
        dst[...] = src[...].astype(BF16)


def _s5_prompt_kernel(u_ref, um_ref, perm_ref, permt_ref, bt_ref, ct_ref, are_ref, aim_ref, d_ref,
                      y_ref, fre_ref, fim_ref,
                      hre_ref, him_ref, hre2_ref, him2_ref, pre_ref, pim_ref, cre_s, cim_s,
                      hsre_ref, hsim_ref, hsre2_ref, hsim2_ref,
                      b_ref, cre_ref, cimn_ref):
    b_id = pl.program_id(0)
    c_id = pl.program_id(1)
    n_c = pl.num_programs(1)

    @pl.when(jnp.logical_and(b_id == 0, c_id == 0))
    def _():
        _s5_expand_weights(bt_ref, ct_ref, b_ref, cre_ref, cimn_ref)
        ar = jnp.broadcast_to(are_ref[...], (S5_NSEG, SSM_FLAT))
        ai = jnp.broadcast_to(aim_ref[...], (S5_NSEG, SSM_FLAT))
        pre_ref[0:S5_NSEG, :] = ar
        pim_ref[0:S5_NSEG, :] = ai

        def body(i, carry):
            pr, pi = carry
            nr = ar * pr - ai * pi
            ni = ar * pi + ai * pr
            rows = pl.ds(pl.multiple_of(i * S5_NSEG, S5_NSEG), S5_NSEG)
            pre_ref[rows, :] = nr
            pim_ref[rows, :] = ni
            return nr, ni

        lax.fori_loop(1, S5_SEG, body, (ar, ai))

    @pl.when(c_id == 0)
    def _():
        _s5_project_in(um_ref[...].astype(BF16), b_ref, hre_ref, him_ref, N_META)
        ar = are_ref[...]
        ai = aim_ref[...]
        hr = jnp.zeros((1, SSM_FLAT), F32)
        hi = jnp.zeros((1, SSM_FLAT), F32)
        for t in range(N_META):
            hr, hi = _cmul_add(ar, ai, hr, hi, hre_ref[t:t + 1, :], him_ref[t:t + 1, :])
        cre_s[...] = hr
        cim_s[...] = hi

    bufs = ((hre_ref, him_ref, hsre_ref, hsim_ref), (hre2_ref, him2_ref, hsre2_ref, hsim2_ref))
    us = [u_ref[n * S5_TC:(n + 1) * S5_TC, :] for n in range(S5_CPS)]
    for n in range(S5_CPS):
        u_perm = jnp.dot(perm_ref[...], us[n].astype(BF16), preferred_element_type=F32).astype(BF16)
        _s5_project_in(u_perm, b_ref, bufs[n][0], bufs[n][1], S5_TC)
    for n in range(S5_CPS):
        hre_n, him_n, hsre_n, hsim_n = bufs[n]
        _s5_scan_chunk(hre_n, him_n, hsre_n, hsim_n, pre_ref, pim_ref, are_ref, aim_ref, cre_s, cim_s)
        y_perm = _s5_project_out(hre_n, him_n, cre_ref, cimn_ref, S5_TC)
        y = sum(jnp.dot(permt_ref[...], piece, preferred_element_type=F32) for piece in _split3(y_perm))
        y_ref[n * S5_TC:(n + 1) * S5_TC, :] = y + d_ref[...] * us[n]

    @pl.when(c_id == n_c - 1)
    def _():
        fre_ref[0] = cre_s[...]
        fim_ref[0] = cim_s[...]


def _s5_weight_scratch():
    return [pltpu.VMEM((SSM_JBLK, SSM_BLK_IN, 2 * SSM_BLK_STATE), BF16),
            pltpu.VMEM((SSM_JBLK, SSM_BLK_STATE, SSM_BLK_IN), BF16),
            pltpu.VMEM((SSM_JBLK, SSM_BLK_STATE, SSM_BLK_IN), BF16)]


def _s5_prompt(u_arr, u_col_block, u_meta, abar_re, abar_im, bt, ct, d_skip, batch, seq):
    rows_step = S5_CPS * S5_TC
    n_c = seq // rows_step
    const3 = lambda b, c: (0, 0, 0)
    const2 = lambda b, c: (0, 0)
    rows = np.arange(S5_TC)
    perm_np = np.zeros((S5_TC, S5_TC), np.float32)
    perm_np[rows, (rows % S5_NSEG) * S5_SEG + rows // S5_NSEG] = 1.0
    perm = jnp.asarray(perm_np, dtype=BF16)
    permt = jnp.asarray(perm_np.T, dtype=BF16)
    return pl.pallas_call(
        _s5_prompt_kernel,
        grid=(batch, n_c),
        in_specs=[pl.BlockSpec((rows_step, SSM_WIDTH), lambda b, c: (b * n_c + c, u_col_block)),
                  pl.BlockSpec((N_META, SSM_WIDTH), const2),
                  pl.BlockSpec((S5_TC, S5_TC), const2),
                  pl.BlockSpec((S5_TC, S5_TC), const2),
                  pl.BlockSpec((SSM_JBLK, SSM_BLK_IN, 2 * LANE_TILE), const3, pipeline_mode=pl.Buffered(1)),
                  pl.BlockSpec((SSM_JBLK, SSM_BLK_STATE, 2 * LANE_TILE), const3, pipeline_mode=pl.Buffered(1)),
                  pl.BlockSpec((1, SSM_FLAT), const2),
                  pl.BlockSpec((1, SSM_FLAT), const2),
                  pl.BlockSpec((1, SSM_WIDTH), const2)],
        out_specs=[pl.BlockSpec((rows_step, SSM_WIDTH), lambda b, c: (b * n_c + c, 0)),
                   pl.BlockSpec((1, 1, SSM_FLAT), lambda b, c: (b, 0, 0)),
                   pl.BlockSpec((1, 1, SSM_FLAT), lambda b, c: (b, 0, 0))],
        out_shape=[jax.ShapeDtypeStruct((batch * seq, SSM_WIDTH), F32),
                   jax.ShapeDtypeStruct((batch, 1, SSM_FLAT), F32),
                   jax.ShapeDtypeStruct((batch, 1, SSM_FLAT), F32)],
        scratch_shapes=[pltpu.VMEM((S5_TC, SSM_FLAT), F32)] * 4
                       + [pltpu.VMEM((S5_TC, SSM_FLAT), F32)] * 2
                       + [pltpu.VMEM((1, SSM_FLAT), F32)] * 2
                       + [pltpu.VMEM((S5_NSEG, SSM_FLAT), F32)] * 4
                       + _s5_weight_scratch(),
        compiler_params=_cparams(2, 59),
        name="s5_prompt",
    )(u_arr, u_meta, perm, permt, bt, ct, abar_re, abar_im, d_skip)


def _s5_sample_kernel(u_ref, h0re_ref, h0im_ref, bt_ref, ct_ref, are_ref, aim_ref, d_ref,
                      y_ref, hre_ref, him_ref, b_ref, cre_ref, cimn_ref):
    rows = u_ref.shape[0]
    _s5_expand_weights(bt_ref, ct_ref, b_ref, cre_ref, cimn_ref)
    u = u_ref[...]
    _s5_project_in(u.astype(BF16), b_ref, hre_ref, him_ref, rows)
    nr, ni = _cmul_add(are_ref[...], aim_ref[...], h0re_ref[...], h0im_ref[...], hre_ref[...], him_ref[...])
    hre_ref[...] = nr
    him_ref[...] = ni
    y_ref[...] = _s5_project_out(hre_ref, him_ref, cre_ref, cimn_ref, rows) + d_ref[...] * u


def _s5_sample(u, h0_re, h0_im, abar_re, abar_im, bt, ct, d_skip):
    rows = u.shape[0]
    return pl.pallas_call(
        _s5_sample_kernel,
        out_shape=[jax.ShapeDtypeStruct((rows, SSM_WIDTH), F32),
                   jax.ShapeDtypeStruct((rows, SSM_FLAT), F32),
                   jax.ShapeDtypeStruct((rows, SSM_FLAT), F32)],
        scratch_shapes=_s5_weight_scratch(),
        compiler_params=_cparams(0, 48),
        name="s5_sample",
    )(u, h0_re, h0_im, bt, ct, abar_re, abar_im, d_skip)


def _split3(x):
    hi = x.astype(BF16)
    r1 = x - hi.astype(F32)
    mid = r1.astype(BF16)
    lo = (r1 - mid.astype(F32)).astype(BF16)
    return hi, mid, lo


def _tri(n, lower):
    r = lax.broadcasted_iota(jnp.int32, (n, n), 0)
    c = lax.broadcasted_iota(jnp.int32, (n, n), 1)
    return (r >= c) if lower else (r <= c)


def _gate_columns(gcol, bias_row):
    n = gcol.shape[0]
    g = gcol + bias_row
    fl = _log_sigmoid(g[:, HEADS:])
    tril = jnp.where(_tri(n, True), 1.0, 0.0).astype(BF16)
    b = sum(jnp.dot(tril, p, preferred_element_type=F32) for p in _split3(fl))
    return g[:, :HEADS], b


def _gate_rows(grow, bias_col):
    n = grow.shape[1]
    g = grow + bias_col
    fl = _log_sigmoid(g[HEADS:, :])
    triu = jnp.where(_tri(n, False), 1.0, 0.0).astype(BF16)
    b = sum(jnp.dot(p, triu, preferred_element_type=F32) for p in _split3(fl))
    return g[:HEADS, :], b


MLSTM_GROUP = 2


def _eye_bf16(n):
    return jnp.where(_tri(n, True) & _tri(n, False), 1.0, 0.0).astype(BF16)


def _mlstm_state_update(k_bf, v, ig_col, b_col, c_old, n_old, m_old, k_t=None, gate_rows=None):
    n_rows = k_bf.shape[0]
    k_scale = DK ** -0.5
    b_last = b_col[n_rows - 1:n_rows, :]
    e_col = b_last - b_col + ig_col
    m_new = jnp.maximum(b_last + m_old, jnp.max(e_col, axis=0, keepdims=True))
    w_end = jnp.exp(e_col - m_new) * k_scale
    carry = jnp.exp(b_last + m_old - m_new)
    if k_t is None:
        k_t = lax.dot_general(_eye_bf16(DK), k_bf, NT_DIMS, preferred_element_type=F32)
    if gate_rows is None:
        kv = jnp.dot(k_t.astype(BF16), (v.astype(F32) * w_end).astype(BF16), preferred_element_type=F32)
    else:
        ig_row, b_row = gate_rows
        w_end_row = jnp.exp(b_last - b_row + ig_row - m_new) * k_scale
        kv = jnp.dot((k_t * w_end_row).astype(BF16), v, preferred_element_type=F32)
    c_new = carry * c_old + kv
    n_new = carry * n_old + jnp.sum(k_bf.astype(F32) * w_end, axis=0, keepdims=True)
    return c_new, n_new, m_new


def _mlstm_decay(ig_row, b_row, b_col, m_old):
    n_rows = b_col.shape[0]
    dmat = jnp.where(_tri(n_rows, True), b_col + (ig_row - b_row), -jnp.inf)
    inter = b_col + m_old
    m_t = jnp.maximum(inter, jnp.max(dmat, axis=-1, keepdims=True))
    return jnp.exp(dmat - m_t), jnp.exp(inter - m_t), m_t


def _head_norm_gate(h, o, gain):
    mu = jnp.mean(h, axis=-1, keepdims=True)
    hc = h - mu
    var = jnp.mean(hc * hc, axis=-1, keepdims=True)
    return _sigmoid(o) * (hc * lax.rsqrt(var + EPS) * gain)


def _mlstm_prompt_kernel(batch, q_ref, k_ref, v_ref, o_ref, gc_ref, *rest):
    gr_refs = rest[:batch]
    rest = rest[batch:]
    km_ref, vm_ref, gcm_ref, brow_ref, bcol_ref, gain_ref = rest[:6]
    cast_in = rest[6:6 + N_CAST_IN]
    y_ref, cf_ref, nf_ref, mf_ref = rest[6 + N_CAST_IN:10 + N_CAST_IN]
    cast_out = rest[10 + N_CAST_IN:]
    k_scale = DK ** -0.5

    _weight_cast_step(cast_in, cast_out)

    @pl.when(pl.program_id(0) == 0)
    def _():
        ig_c, b_c = _gate_columns(gcm_ref[...], brow_ref[...])
        for h in range(HEADS):
            hs = slice(h * DK, (h + 1) * DK)
            c_new, n_new, m_new = _mlstm_state_update(
                km_ref[:, hs], vm_ref[:, hs], ig_c[:, h:h + 1], b_c[:, h:h + 1],
                jnp.zeros((DK, DV), F32), jnp.zeros((1, DK), F32), jnp.zeros((1, 1), F32))
            for b in range(batch):
                cf_ref[b, h] = c_new
                nf_ref[b, h] = n_new
                mf_ref[b, h:h + 1, :] = jnp.broadcast_to(m_new, (1, LANE_TILE))

    eye = _eye_bf16(DK)
    for b0 in range(0, batch, MLSTM_GROUP):
        gates = {}
        for b in range(b0, b0 + MLSTM_GROUP):
            gates[b] = _gate_columns(gc_ref[b], brow_ref[...]) + _gate_rows(gr_refs[b][...], bcol_ref[...])
        chains = [(b, h) for b in range(b0, b0 + MLSTM_GROUP) for h in range(HEADS)]

        ops = []
        for b, h in chains:
            hs = slice(h * DK, (h + 1) * DK)
            q = q_ref[b, :, hs]
            k_bf = k_ref[b, :, hs]
            c_old = cf_ref[b, h]
            ops.append(dict(
                q=q, k_bf=k_bf, v=v_ref[b, :, hs], c_old=c_old, n_old=nf_ref[b, h], m_old=mf_ref[b, h:h + 1, 0:1],
                s_raw=lax.dot_general(q, k_bf, NT_DIMS, preferred_element_type=F32),
                qc=jnp.dot(q, c_old.astype(BF16), preferred_element_type=F32),
                k_t=lax.dot_general(eye, k_bf, NT_DIMS, preferred_element_type=F32)))

        for (b, h), c in zip(chains, ops):
            ig_c, b_c, ig_r, b_r = gates[b]
            c["w_intra"], c["w_inter"], c["m_t"] = _mlstm_decay(ig_r[h:h + 1, :], b_r[h:h + 1, :], b_c[:, h:h + 1],
                                                                c["m_old"])

        for (b, h), c in zip(chains, ops):
            hs = slice(h * DK, (h + 1) * DK)
            s = c["s_raw"] * (c["w_intra"] * k_scale)
            num = c["w_inter"] * c["qc"] + jnp.dot(s.astype(BF16), c["v"], preferred_element_type=F32)
            den = (c["w_inter"] * jnp.sum(c["q"].astype(F32) * c["n_old"], axis=-1, keepdims=True)
                   + jnp.sum(s, axis=-1, keepdims=True))
            hid = num / jnp.maximum(jnp.abs(den), jnp.exp(-c["m_t"]))
            y_ref[b, :, hs] = _head_norm_gate(hid, o_ref[b, :, hs], gain_ref[:, hs]).astype(y_ref.dtype)

        for (b, h), c in zip(chains, ops):
            ig_c, b_c, ig_r, b_r = gates[b]
            c_new, n_new, m_new = _mlstm_state_update(c["k_bf"], c["v"], ig_c[:, h:h + 1], b_c[:, h:h + 1],
                                                      c["c_old"], c["n_old"], c["m_old"], k_t=c["k_t"],
                                                      gate_rows=(ig_r[h:h + 1, :], b_r[h:h + 1, :]))
            cf_ref[b, h] = c_new
            nf_ref[b, h] = n_new
            mf_ref[b, h:h + 1, :] = jnp.broadcast_to(m_new, (1, LANE_TILE))


def _mlstm_prompt(qkv, o_arr, o_col_block, gcol, grow, qkv_meta, gcol_meta, bias_row, bias_col, gain, batch, seq,
                  w_in_t, cast_weights):
    n_c = seq // CHUNK
    width = HEADS * DK
    const2 = lambda c: (0, 0)
    cast_in, cast_out, cast_shapes, cast_args = _weight_cast_specs(n_c, lambda c: c, w_in_t, cast_weights)
    qkv3 = qkv.reshape(batch, seq, 3 * width)
    o3 = o_arr.reshape(batch, seq, o_arr.shape[1])
    gcol3 = gcol.reshape(gcol.shape[0] // seq, seq, 2 * HEADS)
    state0 = lambda c: (0, 0, 0, 0)
    return pl.pallas_call(
        functools.partial(_mlstm_prompt_kernel, batch),
        grid=(n_c,),
        in_specs=[pl.BlockSpec((batch, CHUNK, width), lambda c: (0, c, 0)),
                  pl.BlockSpec((batch, CHUNK, width), lambda c: (0, c, 1)),
                  pl.BlockSpec((batch, CHUNK, width), lambda c: (0, c, 2)),
                  pl.BlockSpec((batch, CHUNK, width), lambda c: (0, c, o_col_block)),
                  pl.BlockSpec((batch, CHUNK, 2 * HEADS), lambda c: (0, c, 0))]
                 + [pl.BlockSpec((2 * HEADS, CHUNK), functools.partial(lambda b, c: (0, b * n_c + c), b))
                    for b in range(batch)]
                 + [pl.BlockSpec((N_META, width), lambda c: (0, 1)),
                    pl.BlockSpec((N_META, width), lambda c: (0, 2)),
                    pl.BlockSpec((N_META, 2 * HEADS), const2),
                    pl.BlockSpec((1, 2 * HEADS), const2),
                    pl.BlockSpec((2 * HEADS, 1), const2),
                    pl.BlockSpec((1, width), const2)] + cast_in,
        out_specs=[pl.BlockSpec((batch, CHUNK, width), lambda c: (0, c, 0)),
                   pl.BlockSpec((batch, HEADS, DK, DV), state0),
                   pl.BlockSpec((batch, HEADS, 1, DK), state0),
                   pl.BlockSpec((batch, HEADS, LANE_TILE), lambda c: (0, 0, 0))] + cast_out,
        out_shape=[jax.ShapeDtypeStruct((batch, seq, width), BF16),
                   jax.ShapeDtypeStruct((batch, HEADS, DK, DV), F32),
                   jax.ShapeDtypeStruct((batch, HEADS, 1, DK), F32),
                   jax.ShapeDtypeStruct((batch, HEADS, LANE_TILE), F32)] + cast_shapes,
        compiler_params=_cparams(1, 52),
        name="mlstm_prompt",
    )(qkv3, qkv3, qkv3, o3, gcol3, *([grow] * batch), qkv_meta, qkv_meta, gcol_meta, bias_row, bias_col, gain,
      *cast_args)


MLSTM_SB = 8


def _mlstm_sample_kernel(qkv_ref, o_ref, g_ref, c_ref, n_ref, m_ref, brow_ref, gain_ref,
                         y_ref, co_ref, no_ref, mo_ref):
    sb = MLSTM_SB
    k_scale = DK ** -0.5
    eye = _eye_bf16(DK)
    width = HEADS * DK
    pad = 2 * sb
    row_id = lax.broadcasted_iota(jnp.int32, (pad, DV), 0)

    g_all = g_ref[...] + brow_ref[...]
    ig_all = g_all[:, :HEADS]
    fl_all = _log_sigmoid(g_all[:, HEADS:])
    m_old_all = m_ref[...]
    m_new_all = jnp.maximum(fl_all + m_old_all, ig_all)
    w_in_all = jnp.exp(ig_all - m_new_all)
    w_ca_all = jnp.exp(fl_all + m_old_all - m_new_all)
    floor_all = jnp.exp(-m_new_all)
    mo_ref[...] = m_new_all

    for h in range(HEADS):
        hs = slice(h * DK, (h + 1) * DK)
        qk_rows = jnp.concatenate([qkv_ref[:, hs], qkv_ref[:, width + h * DK:width + (h + 1) * DK] * k_scale],
                                  axis=0).astype(BF16)
        qk_cols = lax.dot_general(eye, qk_rows, NT_DIMS, preferred_element_type=F32).astype(BF16)
        for i in range(sb):
            w_in = w_in_all[i:i + 1, h:h + 1]
            w_ca = w_ca_all[i:i + 1, h:h + 1]
            q = qk_rows[i:i + 1, :]
            qf = q.astype(F32)
            k = qk_rows[sb + i:sb + i + 1, :].astype(F32)
            v = qkv_ref[i:i + 1, 2 * width + h * DK:2 * width + (h + 1) * DK].astype(BF16).astype(F32)
            c_old = c_ref[i, h]
            n_old = n_ref[i, h]
            qc = jnp.dot(jnp.broadcast_to(q, (pad, DK)), c_old.astype(BF16), preferred_element_type=F32)[0:1, :]
            v_rows = jnp.where(row_id == sb + i, jnp.broadcast_to(v * w_in, (pad, DV)), 0.0).astype(BF16)
            kv = jnp.dot(qk_cols, v_rows, preferred_element_type=F32)
            s = jnp.sum(qf * k, axis=-1, keepdims=True) * w_in
            num = w_ca * qc + s * v
            den = w_ca * jnp.sum(qf * n_old, axis=-1, keepdims=True) + s
            hid = num / jnp.maximum(jnp.abs(den), floor_all[i:i + 1, h:h + 1])
            co_ref[i, h] = w_ca * c_old + kv
            no_ref[i, h] = w_ca * n_old + k * w_in
            y_ref[i:i + 1, hs] = _head_norm_gate(hid, o_ref[i:i + 1, hs], gain_ref[:, hs])


def _mlstm_sample(qkv, o_arr, gcol, c0, n0, m0, bias_row, gain):
    nb = qkv.shape[0]
    width = HEADS * DK
    sb = MLSTM_SB
    const2 = lambda i: (0, 0)
    return pl.pallas_call(
        _mlstm_sample_kernel,
        grid=(nb // sb,),
        in_specs=[pl.BlockSpec((sb, 3 * width), lambda i: (i, 0)),
                  pl.BlockSpec((sb, width), lambda i: (i, 0)),
                  pl.BlockSpec((sb, 2 * HEADS), lambda i: (i, 0)),
                  pl.BlockSpec((sb, HEADS, DK, DV), lambda i: (i, 0, 0, 0)),
                  pl.BlockSpec((sb, HEADS, 1, DK), lambda i: (i, 0, 0, 0)),
                  pl.BlockSpec((sb, HEADS), lambda i: (i, 0)),
                  pl.BlockSpec((1, 2 * HEADS), const2),
                  pl.BlockSpec((1, width), const2)],
        out_specs=[pl.BlockSpec((sb, width), lambda i: (i, 0)),
                   pl.BlockSpec((sb, HEADS, DK, DV), lambda i: (i, 0, 0, 0)),
                   pl.BlockSpec((sb, HEADS, 1, DK), lambda i: (i, 0, 0, 0)),
                   pl.BlockSpec((sb, HEADS), lambda i: (i, 0))],
        out_shape=[jax.ShapeDtypeStruct((nb, width), F32),
                   jax.ShapeDtypeStruct((nb, HEADS, DK, DV), F32),
                   jax.ShapeDtypeStruct((nb, HEADS, 1, DK), F32),
                   jax.ShapeDtypeStruct((nb, HEADS), F32)],
        compiler_params=_cparams(1, 48),
        name="mlstm_sample",
    )(qkv, o_arr, gcol, c0, n0, m0, bias_row, gain)


def _merge_rows(ya, yb, x, wgab_ref, wglu_ref, bglu_ref, wa_ref, wb_ref, wout_ref, g1_ref, b1_ref):
    xb = x.astype(BF16)
    g = _gelu_tanh(ya)
    z = jnp.dot(g.astype(BF16), wglu_ref[...], preferred_element_type=F32) + bglu_ref[...]
    out_a = g * _sigmoid(z)
    up_a = jnp.dot(out_a.astype(BF16), wa_ref[...], preferred_element_type=F32)
    g_a = lax.dot_general(xb, wgab_ref[0:D_MODEL, :], NT_DIMS, preferred_element_type=F32)
    mix = _sigmoid(g_a) * up_a
    up_b = jnp.dot(yb, wb_ref[...], preferred_element_type=F32)
    g_b = lax.dot_general(xb, wgab_ref[D_MODEL:2 * D_MODEL, :], NT_DIMS, preferred_element_type=F32)
    mix = mix + _sigmoid(g_b) * up_b
    mo = jnp.dot(mix.astype(BF16), wout_ref[...], preferred_element_type=F32)
    return _layernorm_rows(ALPHA * x + mo, g1_ref[...], b1_ref[...])


def _merge_kernel(ya_ref, yb_ref, x_ref, yas_ref, ybs_ref, xs_ref, *rest):
    weights, (o_ref, os_ref, ob_ref, osb_ref) = rest[:-4], rest[-4:]
    x1 = _merge_rows(ya_ref[...], yb_ref[...], x_ref[...], *weights)
    o_ref[...] = x1
    ob_ref[...] = x1.astype(BF16)

    @pl.when(pl.program_id(0) == pl.num_programs(0) - 1)
    def _():
        x1s = _merge_rows(yas_ref[...], ybs_ref[...].astype(BF16), xs_ref[...], *weights)
        os_ref[...] = x1s
        osb_ref[...] = x1s.astype(BF16)


def _merge(ya, yb, x, ya_s, yb_s, x_s, wt_gab, w_glu, b_glu, w_a_up, w_b_up, w_out, ln_g, ln_b, tm):
    r = x.shape[0]
    rs = x_s.shape[0]
    const2 = lambda i: (0, 0)
    resident = functools.partial(pl.BlockSpec, index_map=const2, pipeline_mode=pl.Buffered(1))
    return pl.pallas_call(
        _merge_kernel,
        grid=(r // tm,),
        in_specs=[pl.BlockSpec((tm, SSM_WIDTH), lambda i: (i, 0)),
                  pl.BlockSpec((tm, HEADS * DV), lambda i: (i, 0)),
                  pl.BlockSpec((tm, D_MODEL), lambda i: (i, 0)),
                  resident((rs, SSM_WIDTH)),
                  resident((rs, HEADS * DV)),
                  resident((rs, D_MODEL)),
                  resident((2 * D_MODEL, D_MODEL)),
                  resident((SSM_WIDTH, SSM_WIDTH)),
                  resident((1, SSM_WIDTH)),
                  resident((SSM_WIDTH, D_MODEL)),
                  resident((HEADS * DV, D_MODEL)),
                  resident((D_MODEL, D_MODEL)),
                  resident((1, D_MODEL)),
                  resident((1, D_MODEL))],
        out_specs=[pl.BlockSpec((tm, D_MODEL), lambda i: (i, 0)),
                   pl.BlockSpec((rs, D_MODEL), const2),
                   pl.BlockSpec((tm, D_MODEL), lambda i: (i, 0)),
                   pl.BlockSpec((rs, D_MODEL), const2)],
        out_shape=[jax.ShapeDtypeStruct((r, D_MODEL), F32), jax.ShapeDtypeStruct((rs, D_MODEL), F32),
                   jax.ShapeDtypeStruct((r, D_MODEL), BF16), jax.ShapeDtypeStruct((rs, D_MODEL), BF16)],
        compiler_params=_cparams(1, 59),
        name="merge_ln1",
    )(ya, yb, x, ya_s, yb_s, x_s, wt_gab, w_glu, b_glu, w_a_up, w_b_up, w_out, ln_g, ln_b)


def _swiglu(xb, wg, wu):
    hg = jnp.dot(xb, wg, preferred_element_type=F32)
    hu = jnp.dot(xb, wu, preferred_element_type=F32)
    return ((hg * _sigmoid(hg)) * hu).astype(BF16)


FFN_UP_PIECE = 512


def _ffn_up_kernel(x_ref, xs_ref, wg_ref, wu_ref, wd_ref, h_ref, hs_ref, wdb_ref, wgb_s, wub_s):
    wgb_s[...] = wg_ref[...].astype(BF16)
    wub_s[...] = wu_ref[...].astype(BF16)

    for r0 in range(0, x_ref.shape[0], FFN_UP_PIECE):
        rows = slice(r0, r0 + FFN_UP_PIECE)
        h_ref[rows, :] = _swiglu(x_ref[rows, :], wgb_s[...], wub_s[...])
    wdb_ref[...] = wd_ref[...].astype(BF16)

    @pl.when(pl.program_id(0) == pl.num_programs(0) - 1)
    def _():
        hs_ref[...] = _swiglu(xs_ref[...], wgb_s[...], wub_s[...])


def _ffn_up(xb, xsb, w_gate3, w_up3, w_down3, tm, tf):
    r = xb.shape[0]
    rs = xsb.shape[0]
    n_f, n_i = D_FF // tf, r // tm
    wd_rows = D_FF // (n_f * n_i)
    hs_block = lambda i, f: jnp.where(i == n_i - 1, f, 0)
    return pl.pallas_call(
        _ffn_up_kernel,
        grid=(n_i, n_f),
        in_specs=[pl.BlockSpec((tm, D_MODEL), lambda i, f: (i, 0)),
                  pl.BlockSpec((rs, D_MODEL), lambda i, f: (0, 0)),
                  pl.BlockSpec((None, D_MODEL, tf), lambda i, f: (0, 0, f)),
                  pl.BlockSpec((None, D_MODEL, tf), lambda i, f: (0, 0, f)),
                  pl.BlockSpec((None, wd_rows, D_MODEL), lambda i, f: (0, i * n_f + f, 0))],
        out_specs=[pl.BlockSpec((tm, tf), lambda i, f: (i, f)),
                   pl.BlockSpec((rs, tf), lambda i, f: (0, hs_block(i, f))),
                   pl.BlockSpec((wd_rows, D_MODEL), lambda i, f: (i * n_f + f, 0))],
        out_shape=[jax.ShapeDtypeStruct((r, D_FF), BF16), jax.ShapeDtypeStruct((rs, D_FF), BF16),
                   jax.ShapeDtypeStruct((D_FF, D_MODEL), BF16)],
        scratch_shapes=[pltpu.VMEM((D_MODEL, tf), BF16), pltpu.VMEM((D_MODEL, tf), BF16)],
        compiler_params=_cparams(2, 58),
        name="ffn_up",
    )(xb, xsb, w_gate3, w_up3, w_down3)


FFN_DOWN_PIECE = 256


def _ffn_down_rows(h, x, wd_ref, g2_ref, b2_ref):
    return _layernorm_rows(ALPHA * x + jnp.dot(h, wd_ref[...], preferred_element_type=F32), g2_ref[...], b2_ref[...])


def _ffn_down_kernel(h_ref, x_ref, hs_ref, xs_ref, wd_ref, g2_ref, b2_ref, o_ref, os_ref):
    for r0 in range(0, h_ref.shape[0], FFN_DOWN_PIECE):
        rows = slice(r0, r0 + FFN_DOWN_PIECE)
        o_ref[rows, :] = _ffn_down_rows(h_ref[rows, :], x_ref[rows, :], wd_ref, g2_ref, b2_ref)

    @pl.when(pl.program_id(0) == pl.num_programs(0) - 1)
    def _():
        os_ref[...] = _ffn_down_rows(hs_ref[...], xs_ref[...], wd_ref, g2_ref, b2_ref)


def _ffn_down(h, x, h_s, x_s, wd_b, ln_g, ln_b, tm):
    r = x.shape[0]
    rs = x_s.shape[0]
    const2 = lambda i: (0, 0)
    resident = functools.partial(pl.BlockSpec, index_map=const2, pipeline_mode=pl.Buffered(1))
    return pl.pallas_call(
        _ffn_down_kernel,
        grid=(r // tm,),
        in_specs=[pl.BlockSpec((tm, D_FF), lambda i: (i, 0)),
                  pl.BlockSpec((tm, D_MODEL), lambda i: (i, 0)),
                  resident((rs, D_FF)),
                  resident((rs, D_MODEL)),
                  resident((D_FF, D_MODEL)),
                  resident((1, D_MODEL)),
                  resident((1, D_MODEL))],
        out_specs=[pl.BlockSpec((tm, D_MODEL), lambda i: (i, 0)),
                   pl.BlockSpec((rs, D_MODEL), const2)],
        out_shape=[jax.ShapeDtypeStruct((r, D_MODEL), F32), jax.ShapeDtypeStruct((rs, D_MODEL), F32)],
        compiler_params=_cparams(1, 59),
        name="ffn_down_ln2",
    )(h, x, h_s, x_s, wd_b, ln_g, ln_b)


def kernel(x_prompt, x_sample, state_ssm_re, state_ssm_im, state_mlstm_c, state_mlstm_n, state_mlstm_m,
           meta_tokens, w_in, b_if, ssm_a_re, ssm_a_im, ssm_log_dt, ssm_b_re, ssm_b_im, ssm_c_re, ssm_c_im,
           ssm_d, w_glu, b_glu, w_a_up, mh_gain, w_b_up, w_out, ln1_g, ln1_b, w_gate, w_up, w_down,
           ln2_g, ln2_b):
    batch, seq, _ = x_prompt.shape
    nb = x_sample.shape[0]
    width = HEADS * DK

    bias_row = b_if[0].reshape(1, 2 * HEADS)
    bias_col = b_if[0].reshape(2 * HEADS, 1)
    gain = mh_gain[0].reshape(1, width)
    d_skip = ssm_d[0].reshape(1, SSM_WIDTH)
    b_glu_r = b_glu[0].reshape(1, SSM_WIDTH)
    ln1 = (ln1_g[0].reshape(1, D_MODEL), ln1_b[0].reshape(1, D_MODEL))
    ln2 = (ln2_g[0].reshape(1, D_MODEL), ln2_b[0].reshape(1, D_MODEL))
    abar_re, abar_im, bt, ct = _s5_discretise(
        ssm_a_re[0], ssm_a_im[0], ssm_log_dt[0], ssm_b_re[0], ssm_b_im[0], ssm_c_re[0], ssm_c_im[0])

    xp = x_prompt.reshape(batch * seq, D_MODEL)
    xs = x_sample.reshape(nb, D_MODEL)
    xsm = jnp.concatenate([xs, meta_tokens], axis=0)
    w_in_t = jnp.swapaxes(w_in, 1, 2)
    uo_blocks, qkv_blocks = (0, 4, 2), (1, 1, 3)

    uo_p, uo_s, gcol_p, grow_p, gcol_s = _proj(xp, xsm, w_in_t, uo_blocks, F32, F32, 1024, "proj_uo", with_gates=True)
    qkv_p, qkv_s = _proj(xp, xsm, w_in_t, qkv_blocks, BF16, F32, 1024, "proj_qkv")

    u_meta = uo_s[nb:nb + N_META, :SSM_WIDTH]
    qkv_meta = qkv_s[nb:nb + N_META].astype(BF16)
    gcol_meta = gcol_s[nb:nb + N_META]

    ya_p, pf_re, pf_im = _s5_prompt(uo_p, 0, u_meta, abar_re, abar_im, bt, ct, d_skip, batch, seq)
    yb_p, pc, pn, pm, wt_gab, w_out_b, w_a_b, w_b_b, w_glu_b = _mlstm_prompt(
        qkv_p, uo_p, 1, gcol_p, grow_p, qkv_meta, gcol_meta, bias_row, bias_col, gain, batch, seq,
        w_in_t, (w_out, w_a_up, w_b_up, w_glu))
    ya_s, sf_re, sf_im = _s5_sample(uo_s[:nb, :SSM_WIDTH], state_ssm_re[0].reshape(nb, SSM_FLAT),
                                    state_ssm_im[0].reshape(nb, SSM_FLAT), abar_re, abar_im, bt, ct, d_skip)
    yb_s, sc, sn, sm = _mlstm_sample(qkv_s[:nb], uo_s[:nb, SSM_WIDTH:], gcol_s[:nb], state_mlstm_c[0],
                                     state_mlstm_n[0].reshape(nb, HEADS, 1, DK), state_mlstm_m[0], bias_row, gain)

    x1_p, x1_s, x1b_p, x1b_s = _merge(ya_p, yb_p.reshape(batch * seq, width), xp, ya_s, yb_s, xs,
                                      wt_gab, w_glu_b, b_glu_r, w_a_b, w_b_b, w_out_b, *ln1, tm=256)
    h_p, h_s, w_down_b = _ffn_up(x1b_p, x1b_s, w_gate, w_up, w_down, tm=2048, tf=512)
    y_p, y_s = _ffn_down(h_p, x1_p, h_s, x1_s, w_down_b, *ln2, tm=512)

    return (y_p.reshape(batch, seq, D_MODEL),
            y_s.reshape(nb, 1, D_MODEL),
            pf_re.reshape(1, batch, SSM_GROUPS, SSM_STATE),
            pf_im.reshape(1, batch, SSM_GROUPS, SSM_STATE),
            pc.reshape(1, batch, HEADS, DK, DV),
            pn.reshape(1, batch, HEADS, DK),
            pm[:, :, 0].reshape(1, batch, HEADS),
            sf_re.reshape(1, nb, SSM_GROUPS, SSM_STATE),
            sf_im.reshape(1, nb, SSM_GROUPS, SSM_STATE),
            sc.reshape(1, nb, HEADS, DK, DV),
            sn.reshape(1, nb, HEADS, DK),
            sm.reshape(1, nb, HEADS))
```

```python
import functools

import jax
import jax.numpy as jnp
import numpy as np
from jax import lax
from jax.experimental import pallas as pl
from jax.experimental.pallas import tpu as pltpu

F32 = jnp.float32
BF16 = jnp.bfloat16

D_MODEL = 2048
N_META = 16
SSM_WIDTH = 1024
SSM_GROUP = 16
SSM_GROUPS = 64
SSM_STATE = 64
SSM_FLAT = SSM_GROUPS * SSM_STATE
SSM_JBLK = 4
SSM_BLK_IN = SSM_WIDTH // SSM_JBLK
SSM_BLK_STATE = SSM_FLAT // SSM_JBLK
LANE_TILE = 128
HEADS = 4
DK = 256
DV = 256
CHUNK = 128
D_FF = 5632
EPS = 1e-5
ALPHA = 2.0 ** 0.25

S5_TC = 256
S5_NSEG = 8
S5_SEG = S5_TC // S5_NSEG
S5_LW = 512
S5_CPS = 2

V7X_SCOPED_VMEM_MAX_BYTES = 60000 * 1024


def _cparams(n_axes, vmem_mb):
    return pltpu.CompilerParams(
        dimension_semantics=("arbitrary",) * n_axes,
        vmem_limit_bytes=min(vmem_mb * 1024 * 1024, V7X_SCOPED_VMEM_MAX_BYTES),
    )


def _sigmoid(x):
    return 1.0 / (1.0 + jnp.exp(-x))


def _log_sigmoid(x):
    return jnp.minimum(x, 0.0) - jnp.log(1.0 + jnp.exp(-jnp.abs(x)))


def _gelu_tanh(x):
    c = 0.7978845608028654
    return 0.5 * x * (1.0 + jnp.tanh(c * (x + 0.044715 * (x * x * x))))


def _layernorm_rows(x, g, b):
    mu = jnp.mean(x, axis=-1, keepdims=True)
    xc = x - mu
    var = jnp.mean(xc * xc, axis=-1, keepdims=True)
    return xc * lax.rsqrt(var + EPS) * g + b


PROJ_TN = 1024
GATE_ROWS = 2 * HEADS
TAIL_ROW0 = SSM_WIDTH + 4 * HEADS * DK
NT_DIMS = (((1,), (1,)), ((), ()))


def _proj_kernel(with_gates, x_ref, xs_ref, wt_ref, *rest):
    if with_gates:
        wg_ref, o_ref, os_ref, g_ref, gt_ref, gs_ref, wb_s = rest
    else:
        o_ref, os_ref, wb_s = rest
    i = pl.program_id(1)

    @pl.when(i == 0)
    def _():
        wb_s[...] = wt_ref[...].astype(BF16)

    xb = x_ref[...].astype(BF16)
    o_ref[...] = lax.dot_general(xb, wb_s[...], NT_DIMS, preferred_element_type=F32).astype(o_ref.dtype)
    first_block = pl.program_id(0) == 0
    if with_gates:
        @pl.when(first_block)
        def _():
            wg = wg_ref[...].astype(BF16)
            g_ref[...] = lax.dot_general(xb, wg, NT_DIMS, preferred_element_type=F32)
            gt_ref[...] = lax.dot_general(wg, xb, NT_DIMS, preferred_element_type=F32)

    @pl.when(i == pl.num_programs(1) - 1)
    def _():
        xsb = xs_ref[...].astype(BF16)
        os_ref[...] = lax.dot_general(xsb, wb_s[...], NT_DIMS, preferred_element_type=F32).astype(os_ref.dtype)
        if with_gates:
            @pl.when(first_block)
            def _():
                gs_ref[...] = lax.dot_general(xsb, wg_ref[...].astype(BF16), NT_DIMS, preferred_element_type=F32)


def _proj(x, xs, wt3, blocks, out_dtype, small_dtype, tm, name, with_gates=False):
    r, k = x.shape
    rs = xs.shape[0]
    first, step, count = blocks
    tn = PROJ_TN
    nm = r // tm
    in_specs = [pl.BlockSpec((tm, k), lambda j, i: (i, 0)),
                pl.BlockSpec((rs, k), lambda j, i: (0, 0)),
                pl.BlockSpec((None, tn, k), lambda j, i: (0, first + step * j, 0))]
    out_specs = [pl.BlockSpec((tm, tn), lambda j, i: (i, j)),
                 pl.BlockSpec((rs, tn), lambda j, i: (0, j))]
    out_shape = [jax.ShapeDtypeStruct((r, count * tn), out_dtype),
                 jax.ShapeDtypeStruct((rs, count * tn), small_dtype)]
    args = (x, xs, wt3)
    if with_gates:
        ng = GATE_ROWS
        in_specs += [pl.BlockSpec((None, ng, k), lambda j, i: (0, TAIL_ROW0 // ng, 0))]
        gate_block = lambda j, i: jnp.where(j == 0, i, nm - 1)
        out_specs += [pl.BlockSpec((tm, ng), lambda j, i: (gate_block(j, i), 0)),
                      pl.BlockSpec((ng, tm), lambda j, i: (0, gate_block(j, i))),
                      pl.BlockSpec((rs, ng), lambda j, i: (0, 0))]
        out_shape += [jax.ShapeDtypeStruct((r, ng), F32), jax.ShapeDtypeStruct((ng, r), F32),
                      jax.ShapeDtypeStruct((rs, ng), F32)]
        args = (x, xs, wt3, wt3)
    return pl.pallas_call(
        functools.partial(_proj_kernel, with_gates),
        grid=(count, nm),
        in_specs=in_specs,
        out_specs=out_specs,
        out_shape=out_shape,
        scratch_shapes=[pltpu.VMEM((tn, k), BF16)],
        compiler_params=_cparams(2, 56),
        name=name,
    )(*args)


def _s5_discretise(a_re, a_im, log_dt, b_re, b_im, c_re, c_im):
    dt = jnp.exp(log_dt)
    e = jnp.exp(a_re * dt)
    abar_re = e * jnp.cos(a_im * dt)
    abar_im = e * jnp.sin(a_im * dt)
    nr = abar_re - 1.0
    ni = abar_im
    den = a_re * a_re + a_im * a_im
    coef_re = (nr * a_re + ni * a_im) / den
    coef_im = (ni * a_re - nr * a_im) / den
    bb_re = coef_re[..., None] * b_re - coef_im[..., None] * b_im
    bb_im = coef_re[..., None] * b_im + coef_im[..., None] * b_re
    gpb = SSM_GROUPS // SSM_JBLK

    def b_rows(bb):
        t = bb.reshape(SSM_JBLK, gpb, SSM_STATE, SSM_GROUP).transpose(0, 1, 3, 2)
        return t.reshape(SSM_JBLK, SSM_BLK_IN, SSM_STATE)

    def c_rows(cc):
        t = cc.reshape(SSM_JBLK, gpb, SSM_GROUP, SSM_STATE).transpose(0, 1, 3, 2)
        return t.reshape(SSM_JBLK, SSM_BLK_STATE, SSM_GROUP)

    b_rep, c_rep = LANE_TILE // SSM_STATE, LANE_TILE // SSM_GROUP
    bt = jnp.concatenate([jnp.tile(b_rows(bb_re), (1, 1, b_rep)), jnp.tile(b_rows(bb_im), (1, 1, b_rep))], axis=-1)
    ct = jnp.concatenate([jnp.tile(c_rows(c_re), (1, 1, c_rep)), jnp.tile(c_rows(-c_im), (1, 1, c_rep))], axis=-1)
    return abar_re.reshape(1, SSM_FLAT), abar_im.reshape(1, SSM_FLAT), bt.astype(BF16), ct.astype(BF16)


def _s5_expand_weights(bt_ref, ct_ref, bw_s, cwr_s, cwi_s):
    log2 = lambda n: n.bit_length() - 1
    rb = lax.broadcasted_iota(jnp.int32, (SSM_BLK_IN, 2 * SSM_BLK_STATE), 0)
    cb = lax.broadcasted_iota(jnp.int32, (SSM_BLK_IN, 2 * SSM_BLK_STATE), 1)
    bmask = (rb >> log2(SSM_GROUP)) == ((cb & (SSM_BLK_STATE - 1)) >> log2(SSM_STATE))
    rc = lax.broadcasted_iota(jnp.int32, (SSM_BLK_STATE, SSM_BLK_IN), 0)
    cc = lax.broadcasted_iota(jnp.int32, (SSM_BLK_STATE, SSM_BLK_IN), 1)
    cmask = (rc >> log2(SSM_STATE)) == (cc >> log2(SSM_GROUP))
    b_tiles, c_tiles = SSM_BLK_STATE // LANE_TILE, SSM_BLK_IN // LANE_TILE
    for j in range(SSM_JBLK):
        bt = bt_ref[j].astype(F32)
        wide = jnp.concatenate([bt[:, :LANE_TILE]] * b_tiles + [bt[:, LANE_TILE:]] * b_tiles, axis=1)
        bw_s[j] = jnp.where(bmask, wide, 0.0).astype(BF16)
        ct = ct_ref[j].astype(F32)
        cwr_s[j] = jnp.where(cmask, jnp.concatenate([ct[:, :LANE_TILE]] * c_tiles, axis=1), 0.0).astype(BF16)
        cwi_s[j] = jnp.where(cmask, jnp.concatenate([ct[:, LANE_TILE:]] * c_tiles, axis=1), 0.0).astype(BF16)


def _cmul_add(ar, ai, hr, hi, xr, xi):
    return ar * hr - ai * hi + xr, ar * hi + ai * hr + xi


def _s5_project_in(u, b_ref, hre_ref, him_ref, rows):
    for j in range(SSM_JBLK):
        state = slice(j * SSM_BLK_STATE, (j + 1) * SSM_BLK_STATE)
        bu = jnp.dot(u[:, j * SSM_BLK_IN:(j + 1) * SSM_BLK_IN], b_ref[j], preferred_element_type=F32)
        hre_ref[0:rows, state] = bu[:, :SSM_BLK_STATE]
        him_ref[0:rows, state] = bu[:, SSM_BLK_STATE:]


def _s5_project_out(hre_ref, him_ref, cre_ref, cimn_ref, rows):
    ys = []
    for j in range(SSM_JBLK):
        state = slice(j * SSM_BLK_STATE, (j + 1) * SSM_BLK_STATE)
        hr = hre_ref[0:rows, state].astype(BF16)
        hi = him_ref[0:rows, state].astype(BF16)
        y = jnp.dot(hr, cre_ref[j], preferred_element_type=F32)
        ys.append(y + jnp.dot(hi, cimn_ref[j], preferred_element_type=F32))
    return jnp.concatenate(ys, axis=1)


def _s5_scan_chunk(hre_ref, him_ref, hsre_ref, hsim_ref, pre_ref, pim_ref, are_ref, aim_ref, cre_s, cim_s):
    full = (S5_NSEG, S5_LW)
    for lg in range(SSM_FLAT // S5_LW):
        lanes = slice(lg * S5_LW, (lg + 1) * S5_LW)
        ar = jnp.broadcast_to(are_ref[:, lanes], full)
        ai = jnp.broadcast_to(aim_ref[:, lanes], full)

        hr = jnp.zeros(full, F32)
        hi = jnp.zeros(full, F32)
        for k in range(S5_SEG):
            rows = slice(k * S5_NSEG, (k + 1) * S5_NSEG)
            hr, hi = _cmul_add(ar, ai, hr, hi, hre_ref[rows, lanes], him_ref[rows, lanes])
            hre_ref[rows, lanes] = hr
            him_ref[rows, lanes] = hi
        er, ei = hr, hi

        asr = pre_ref[S5_TC - 1:S5_TC, lanes]
        asi = pim_ref[S5_TC - 1:S5_TC, lanes]
        hr = cre_s[:, lanes]
        hi = cim_s[:, lanes]
        for j in range(S5_NSEG):
            hsre_ref[j:j + 1, lanes] = hr
            hsim_ref[j:j + 1, lanes] = hi
            hr, hi = _cmul_add(asr, asi, hr, hi, er[j:j + 1, :], ei[j:j + 1, :])
        cre_s[:, lanes] = hr
        cim_s[:, lanes] = hi

        sr = hsre_ref[:, lanes]
        si = hsim_ref[:, lanes]
        for k in range(S5_SEG):
            rows = slice(k * S5_NSEG, (k + 1) * S5_NSEG)
            pr = pre_ref[rows, lanes]
            pi = pim_ref[rows, lanes]
            hre_ref[rows, lanes] = hre_ref[rows, lanes] + (pr * sr - pi * si)
            him_ref[rows, lanes] = him_ref[rows, lanes] + (pr * si + pi * sr)


N_CAST_IN = 6


def _weight_cast_specs(n_steps, step_of, w_in_t, others):
    gab_rows = 2 * D_MODEL // n_steps
    gab_blk0 = TAIL_ROW0 // gab_rows
    gab_next0 = (TAIL_ROW0 + gab_rows) // GATE_ROWS
    rows = [w.shape[1] // n_steps for w in others]
    in_specs = [pl.BlockSpec((None, gab_rows, D_MODEL), lambda *g: (0, gab_blk0 + step_of(*g), 0)),
                pl.BlockSpec((None, GATE_ROWS, D_MODEL),
                             lambda *g: (0, gab_next0 + (gab_rows // GATE_ROWS) * step_of(*g), 0))]
    in_specs += [pl.BlockSpec((None, nr, w.shape[2]), lambda *g: (0, step_of(*g), 0)) for nr, w in zip(rows, others)]
    out_specs = [pl.BlockSpec((gab_rows, D_MODEL), lambda *g: (step_of(*g), 0))]
    out_specs += [pl.BlockSpec((nr, w.shape[2]), lambda *g: (step_of(*g), 0)) for nr, w in zip(rows, others)]
    out_shapes = [jax.ShapeDtypeStruct((2 * D_MODEL, D_MODEL), BF16)]
    out_shapes += [jax.ShapeDtypeStruct(w.shape[1:], BF16) for w in others]
    return in_specs, out_specs, out_shapes, (w_in_t, w_in_t) + tuple(others)


def _weight_cast_step(in_refs, out_refs):
    wga_ref, wgn_ref = in_refs[:2]
    out_refs[0][...] = jnp.concatenate([wga_ref[GATE_ROWS:, :], wgn_ref[...]], axis=0).astype(BF16)
    for src, dst in zip(in_refs[2:], out_refs[1:]):
        dst[...] = src[...].astype(BF16)


def _s5_prompt_kernel(u_ref, um_ref, perm_ref, permt_ref, bt_ref, ct_ref, are_ref, aim_ref, d_ref,
                      y_ref, fre_ref, fim_ref,
                      hre_ref, him_ref, hre2_ref, him2_ref, pre_ref, pim_ref, cre_s, cim_s,
                      hsre_ref, hsim_ref, hsre2_ref, hsim2_ref,
                      b_ref, cre_ref, cimn_ref):
    b_id = pl.program_id(0)
    c_id = pl.program_id(1)
    n_c = pl.num_programs(1)

    @pl.when(jnp.logical_and(b_id == 0, c_id == 0))
    def _():
        _s5_expand_weights(bt_ref, ct_ref, b_ref, cre_ref, cimn_ref)
        ar = jnp.broadcast_to(are_ref[...], (S5_NSEG, SSM_FLAT))
        ai = jnp.broadcast_to(aim_ref[...], (S5_NSEG, SSM_FLAT))
        pre_ref[0:S5_NSEG, :] = ar
        pim_ref[0:S5_NSEG, :] = ai

        def body(i, carry):
            pr, pi = carry
            nr = ar * pr - ai * pi
            ni = ar * pi + ai * pr
            rows = pl.ds(pl.multiple_of(i * S5_NSEG, S5_NSEG), S5_NSEG)
            pre_ref[rows, :] = nr
            pim_ref[rows, :] = ni
            return nr, ni

        lax.fori_loop(1, S5_SEG, body, (ar, ai))

    @pl.when(c_id == 0)
    def _():
        _s5_project_in(um_ref[...].astype(BF16), b_ref, hre_ref, him_ref, N_META)
        ar = are_ref[...]
        ai = aim_ref[...]
        hr = jnp.zeros((1, SSM_FLAT), F32)
        hi = jnp.zeros((1, SSM_FLAT), F32)
        for t in range(N_META):
            hr, hi = _cmul_add(ar, ai, hr, hi, hre_ref[t:t + 1, :], him_ref[t:t + 1, :])
        cre_s[...] = hr
        cim_s[...] = hi

    bufs = ((hre_ref, him_ref, hsre_ref, hsim_ref), (hre2_ref, him2_ref, hsre2_ref, hsim2_ref))
    us = [u_ref[n * S5_TC:(n + 1) * S5_TC, :] for n in range(S5_CPS)]
    for n in range(S5_CPS):
        u_perm = jnp.dot(perm_ref[...], us[n].astype(BF16), preferred_element_type=F32).astype(BF16)
        _s5_project_in(u_perm, b_ref, bufs[n][0], bufs[n][1], S5_TC)
    for n in range(S5_CPS):
        hre_n, him_n, hsre_n, hsim_n = bufs[n]
        _s5_scan_chunk(hre_n, him_n, hsre_n, hsim_n, pre_ref, pim_ref, are_ref, aim_ref, cre_s, cim_s)
        y_perm = _s5_project_out(hre_n, him_n, cre_ref, cimn_ref, S5_TC)
        y = sum(jnp.dot(permt_ref[...], piece, preferred_element_type=F32) for piece in _split3(y_perm))
        y_ref[n * S5_TC:(n + 1) * S5_TC, :] = y + d_ref[...] * us[n]

    @pl.when(c_id == n_c - 1)
    def _():
        fre_ref[0] = cre_s[...]
        fim_ref[0] = cim_s[...]


def _s5_weight_scratch():
    return [pltpu.VMEM((SSM_JBLK, SSM_BLK_IN, 2 * SSM_BLK_STATE), BF16),
            pltpu.VMEM((SSM_JBLK, SSM_BLK_STATE, SSM_BLK_IN), BF16),
            pltpu.VMEM((SSM_JBLK, SSM_BLK_STATE, SSM_BLK_IN), BF16)]


def _s5_prompt(u_arr, u_col_block, u_meta, abar_re, abar_im, bt, ct, d_skip, batch, seq):
    rows_step = S5_CPS * S5_TC
    n_c = seq // rows_step
    const3 = lambda b, c: (0, 0, 0)
    const2 = lambda b, c: (0, 0)
    rows = np.arange(S5_TC)
    perm_np = np.zeros((S5_TC, S5_TC), np.float32)
    perm_np[rows, (rows % S5_NSEG) * S5_SEG + rows // S5_NSEG] = 1.0
    perm = jnp.asarray(perm_np, dtype=BF16)
    permt = jnp.asarray(perm_np.T, dtype=BF16)
    return pl.pallas_call(
        _s5_prompt_kernel,
        grid=(batch, n_c),
        in_specs=[pl.BlockSpec((rows_step, SSM_WIDTH), lambda b, c: (b * n_c + c, u_col_block)),
                  pl.BlockSpec((N_META, SSM_WIDTH), const2),
                  pl.BlockSpec((S5_TC, S5_TC), const2),
                  pl.BlockSpec((S5_TC, S5_TC), const2),
                  pl.BlockSpec((SSM_JBLK, SSM_BLK_IN, 2 * LANE_TILE), const3, pipeline_mode=pl.Buffered(1)),
                  pl.BlockSpec((SSM_JBLK, SSM_BLK_STATE, 2 * LANE_TILE), const3, pipeline_mode=pl.Buffered(1)),
                  pl.BlockSpec((1, SSM_FLAT), const2),
                  pl.BlockSpec((1, SSM_FLAT), const2),
                  pl.BlockSpec((1, SSM_WIDTH), const2)],
        out_specs=[pl.BlockSpec((rows_step, SSM_WIDTH), lambda b, c: (b * n_c + c, 0)),
                   pl.BlockSpec((1, 1, SSM_FLAT), lambda b, c: (b, 0, 0)),
                   pl.BlockSpec((1, 1, SSM_FLAT), lambda b, c: (b, 0, 0))],
        out_shape=[jax.ShapeDtypeStruct((batch * seq, SSM_WIDTH), F32),
                   jax.ShapeDtypeStruct((batch, 1, SSM_FLAT), F32),
                   jax.ShapeDtypeStruct((batch, 1, SSM_FLAT), F32)],
        scratch_shapes=[pltpu.VMEM((S5_TC, SSM_FLAT), F32)] * 4
                       + [pltpu.VMEM((S5_TC, SSM_FLAT), F32)] * 2
                       + [pltpu.VMEM((1, SSM_FLAT), F32)] * 2
                       + [pltpu.VMEM((S5_NSEG, SSM_FLAT), F32)] * 4
                       + _s5_weight_scratch(),
        compiler_params=_cparams(2, 59),
        name="s5_prompt",
    )(u_arr, u_meta, perm, permt, bt, ct, abar_re, abar_im, d_skip)


def _s5_sample_kernel(u_ref, h0re_ref, h0im_ref, bt_ref, ct_ref, are_ref, aim_ref, d_ref,
                      y_ref, hre_ref, him_ref, b_ref, cre_ref, cimn_ref):
    rows = u_ref.shape[0]
    _s5_expand_weights(bt_ref, ct_ref, b_ref, cre_ref, cimn_ref)
    u = u_ref[...]
    _s5_project_in(u.astype(BF16), b_ref, hre_ref, him_ref, rows)
    nr, ni = _cmul_add(are_ref[...], aim_ref[...], h0re_ref[...], h0im_ref[...], hre_ref[...], him_ref[...])
    hre_ref[...] = nr
    him_ref[...] = ni
    y_ref[...] = _s5_project_out(hre_ref, him_ref, cre_ref, cimn_ref, rows) + d_ref[...] * u


def _s5_sample(u, h0_re, h0_im, abar_re, abar_im, bt, ct, d_skip):
    rows = u.shape[0]
    return pl.pallas_call(
        _s5_sample_kernel,
        out_shape=[jax.ShapeDtypeStruct((rows, SSM_WIDTH), F32),
                   jax.ShapeDtypeStruct((rows, SSM_FLAT), F32),
                   jax.ShapeDtypeStruct((rows, SSM_FLAT), F32)],
        scratch_shapes=_s5_weight_scratch(),
        compiler_params=_cparams(0, 48),
        name="s5_sample",
    )(u, h0_re, h0_im, bt, ct, abar_re, abar_im, d_skip)


def _split3(x):
    hi = x.astype(BF16)
    r1 = x - hi.astype(F32)
    mid = r1.astype(BF16)
    lo = (r1 - mid.astype(F32)).astype(BF16)
    return hi, mid, lo


def _tri(n, lower):
    r = lax.broadcasted_iota(jnp.int32, (n, n), 0)
    c = lax.broadcasted_iota(jnp.int32, (n, n), 1)
    return (r >= c) if lower else (r <= c)


def _gate_columns(gcol, bias_row):
    n = gcol.shape[0]
    g = gcol + bias_row
    fl = _log_sigmoid(g[:, HEADS:])
    tril = jnp.where(_tri(n, True), 1.0, 0.0).astype(BF16)
    b = sum(jnp.dot(tril, p, preferred_element_type=F32) for p in _split3(fl))
    return g[:, :HEADS], b


def _gate_rows(grow, bias_col):
    n = grow.shape[1]
    g = grow + bias_col
    fl = _log_sigmoid(g[HEADS:, :])
    triu = jnp.where(_tri(n, False), 1.0, 0.0).astype(BF16)
    b = sum(jnp.dot(p, triu, preferred_element_type=F32) for p in _split3(fl))
    return g[:HEADS, :], b


MLSTM_GROUP = 4


def _eye_bf16(n):
    return jnp.where(_tri(n, True) & _tri(n, False), 1.0, 0.0).astype(BF16)


def _mlstm_state_update(k_bf, v, ig_col, b_col, c_old, n_old, m_old, k_t=None, gate_rows=None):
    n_rows = k_bf.shape[0]
    k_scale = DK ** -0.5
    b_last = b_col[n_rows - 1:n_rows, :]
    e_col = b_last - b_col + ig_col
    m_new = jnp.maximum(b_last + m_old, jnp.max(e_col, axis=0, keepdims=True))
    w_end = jnp.exp(e_col - m_new) * k_scale
    carry = jnp.exp(b_last + m_old - m_new)
    if k_t is None:
        k_t = lax.dot_general(_eye_bf16(DK), k_bf, NT_DIMS, preferred_element_type=F32)
    if gate_rows is None:
        kv = jnp.dot(k_t.astype(BF16), (v.astype(F32) * w_end).astype(BF16), preferred_element_type=F32)
    else:
        ig_row, b_row = gate_rows
        w_end_row = jnp.exp(b_last - b_row + ig_row - m_new) * k_scale
        kv = jnp.dot((k_t * w_end_row).astype(BF16), v, preferred_element_type=F32)
    c_new = carry * c_old + kv
    n_new = carry * n_old + jnp.sum(k_bf.astype(F32) * w_end, axis=0, keepdims=True)
    return c_new, n_new, m_new


def _mlstm_decay(ig_row, b_row, b_col, m_old):
    n_rows = b_col.shape[0]
    dmat = jnp.where(_tri(n_rows, True), b_col + (ig_row - b_row), -jnp.inf)
    inter = b_col + m_old
    m_t = jnp.maximum(inter, jnp.max(dmat, axis=-1, keepdims=True))
    return jnp.exp(dmat - m_t), jnp.exp(inter - m_t), m_t


def _head_norm_gate(h, o, gain):
    mu = jnp.mean(h, axis=-1, keepdims=True)
    hc = h - mu
    var = jnp.mean(hc * hc, axis=-1, keepdims=True)
    return _sigmoid(o) * (hc * lax.rsqrt(var + EPS) * gain)


def _mlstm_prompt_kernel(batch, q_ref, k_ref, v_ref, o_ref, gc_ref, *rest):
    gr_refs = rest[:batch]
    rest = rest[batch:]
    km_ref, vm_ref, gcm_ref, brow_ref, bcol_ref, gain_ref = rest[:6]
    cast_in = rest[6:6 + N_CAST_IN]
    y_ref, cf_ref, nf_ref, mf_ref = rest[6 + N_CAST_IN:10 + N_CAST_IN]
    cast_out = rest[10 + N_CAST_IN:]
    k_scale = DK ** -0.5

    _weight_cast_step(cast_in, cast_out)

    @pl.when(pl.program_id(0) == 0)
    def _():
        ig_c, b_c = _gate_columns(gcm_ref[...], brow_ref[...])
        for h in range(HEADS):
            hs = slice(h * DK, (h + 1) * DK)
            c_new, n_new, m_new = _mlstm_state_update(
                km_ref[:, hs], vm_ref[:, hs], ig_c[:, h:h + 1], b_c[:, h:h + 1],
                jnp.zeros((DK, DV), F32), jnp.zeros((1, DK), F32), jnp.zeros((1, 1), F32))
            for b in range(batch):
                cf_ref[b, h] = c_new
                nf_ref[b, h] = n_new
                mf_ref[b, h:h + 1, :] = jnp.broadcast_to(m_new, (1, LANE_TILE))

    eye = _eye_bf16(DK)
    for b0 in range(0, batch, MLSTM_GROUP):
        gates = {}
        for b in range(b0, b0 + MLSTM_GROUP):
            gates[b] = _gate_columns(gc_ref[b], brow_ref[...]) + _gate_rows(gr_refs[b][...], bcol_ref[...])
        chains = [(b, h) for b in range(b0, b0 + MLSTM_GROUP) for h in range(HEADS)]

        ops = []
        for b, h in chains:
            hs = slice(h * DK, (h + 1) * DK)
            q = q_ref[b, :, hs]
            k_bf = k_ref[b, :, hs]
            c_old = cf_ref[b, h]
            ops.append(dict(
                q=q, k_bf=k_bf, v=v_ref[b, :, hs], c_old=c_old, n_old=nf_ref[b, h], m_old=mf_ref[b, h:h + 1, 0:1],
                s_raw=lax.dot_general(q, k_bf, NT_DIMS, preferred_element_type=F32),
                qc=jnp.dot(q, c_old.astype(BF16), preferred_element_type=F32),
                k_t=lax.dot_general(eye, k_bf, NT_DIMS, preferred_element_type=F32)))

        for (b, h), c in zip(chains, ops):
            ig_c, b_c, ig_r, b_r = gates[b]
            c["w_intra"], c["w_inter"], c["m_t"] = _mlstm_decay(ig_r[h:h + 1, :], b_r[h:h + 1, :], b_c[:, h:h + 1],
                                                                c["m_old"])

        for (b, h), c in zip(chains, ops):
            hs = slice(h * DK, (h + 1) * DK)
            s = c["s_raw"] * (c["w_intra"] * k_scale)
            num = c["w_inter"] * c["qc"] + jnp.dot(s.astype(BF16), c["v"], preferred_element_type=F32)
            den = (c["w_inter"] * jnp.sum(c["q"].astype(F32) * c["n_old"], axis=-1, keepdims=True)
                   + jnp.sum(s, axis=-1, keepdims=True))
            hid = num / jnp.maximum(jnp.abs(den), jnp.exp(-c["m_t"]))
            y_ref[b, :, hs] = _head_norm_gate(hid, o_ref[b, :, hs], gain_ref[:, hs]).astype(y_ref.dtype)

        for (b, h), c in zip(chains, ops):
            ig_c, b_c, ig_r, b_r = gates[b]
            c_new, n_new, m_new = _mlstm_state_update(c["k_bf"], c["v"], ig_c[:, h:h + 1], b_c[:, h:h + 1],
                                                      c["c_old"], c["n_old"], c["m_old"], k_t=c["k_t"],
                                                      gate_rows=(ig_r[h:h + 1, :], b_r[h:h + 1, :]))
            cf_ref[b, h] = c_new
            nf_ref[b, h] = n_new
            mf_ref[b, h:h + 1, :] = jnp.broadcast_to(m_new, (1, LANE_TILE))


def _mlstm_prompt(qkv, o_arr, o_col_block, gcol, grow, qkv_meta, gcol_meta, bias_row, bias_col, gain, batch, seq,
                  w_in_t, cast_weights):
    n_c = seq // CHUNK
    width = HEADS * DK
    const2 = lambda c: (0, 0)
    cast_in, cast_out, cast_shapes, cast_args = _weight_cast_specs(n_c, lambda c: c, w_in_t, cast_weights)
    qkv3 = qkv.reshape(batch, seq, 3 * width)
    o3 = o_arr.reshape(batch, seq, o_arr.shape[1])
    gcol3 = gcol.reshape(gcol.shape[0] // seq, seq, 2 * HEADS)
    state0 = lambda c: (0, 0, 0, 0)
    return pl.pallas_call(
        functools.partial(_mlstm_prompt_kernel, batch),
        grid=(n_c,),
        in_specs=[pl.BlockSpec((batch, CHUNK, width), lambda c: (0, c, 0)),
                  pl.BlockSpec((batch, CHUNK, width), lambda c: (0, c, 1)),
                  pl.BlockSpec((batch, CHUNK, width), lambda c: (0, c, 2)),
                  pl.BlockSpec((batch, CHUNK, width), lambda c: (0, c, o_col_block)),
                  pl.BlockSpec((batch, CHUNK, 2 * HEADS), lambda c: (0, c, 0))]
                 + [pl.BlockSpec((2 * HEADS, CHUNK), functools.partial(lambda b, c: (0, b * n_c + c), b))
                    for b in range(batch)]
                 + [pl.BlockSpec((N_META, width), lambda c: (0, 1)),
                    pl.BlockSpec((N_META, width), lambda c: (0, 2)),
                    pl.BlockSpec((N_META, 2 * HEADS), const2),
                    pl.BlockSpec((1, 2 * HEADS), const2),
                    pl.BlockSpec((2 * HEADS, 1), const2),
                    pl.BlockSpec((1, width), const2)] + cast_in,
        out_specs=[pl.BlockSpec((batch, CHUNK, width), lambda c: (0, c, 0)),
                   pl.BlockSpec((batch, HEADS, DK, DV), state0),
                   pl.BlockSpec((batch, HEADS, 1, DK), state0),
                   pl.BlockSpec((batch, HEADS, LANE_TILE), lambda c: (0, 0, 0))] + cast_out,
        out_shape=[jax.ShapeDtypeStruct((batch, seq, width), BF16),
                   jax.ShapeDtypeStruct((batch, HEADS, DK, DV), F32),
                   jax.ShapeDtypeStruct((batch, HEADS, 1, DK), F32),
                   jax.ShapeDtypeStruct((batch, HEADS, LANE_TILE), F32)] + cast_shapes,
        compiler_params=_cparams(1, 52),
        name="mlstm_prompt",
    )(qkv3, qkv3, qkv3, o3, gcol3, *([grow] * batch), qkv_meta, qkv_meta, gcol_meta, bias_row, bias_col, gain,
      *cast_args)


MLSTM_SB = 8


def _mlstm_sample_kernel(qkv_ref, o_ref, g_ref, c_ref, n_ref, m_ref, brow_ref, gain_ref,
                         y_ref, co_ref, no_ref, mo_ref):
    sb = MLSTM_SB
    k_scale = DK ** -0.5
    eye = _eye_bf16(DK)
    width = HEADS * DK
    pad = 2 * sb
    row_id = lax.broadcasted_iota(jnp.int32, (pad, DV), 0)

    g_all = g_ref[...] + brow_ref[...]
    ig_all = g_all[:, :HEADS]
    fl_all = _log_sigmoid(g_all[:, HEADS:])
    m_old_all = m_ref[...]
    m_new_all = jnp.maximum(fl_all + m_old_all, ig_all)
    w_in_all = jnp.exp(ig_all - m_new_all)
    w_ca_all = jnp.exp(fl_all + m_old_all - m_new_all)
    floor_all = jnp.exp(-m_new_all)
    mo_ref[...] = m_new_all

    for h in range(HEADS):
        hs = slice(h * DK, (h + 1) * DK)
        qk_rows = jnp.concatenate([qkv_ref[:, hs], qkv_ref[:, width + h * DK:width + (h + 1) * DK] * k_scale],
                                  axis=0).astype(BF16)
        qk_cols = lax.dot_general(eye, qk_rows, NT_DIMS, preferred_element_type=F32).astype(BF16)
        for i in range(sb):
            w_in = w_in_all[i:i + 1, h:h + 1]
            w_ca = w_ca_all[i:i + 1, h:h + 1]
            q = qk_rows[i:i + 1, :]
            qf = q.astype(F32)
            k = qk_rows[sb + i:sb + i + 1, :].astype(F32)
            v = qkv_ref[i:i + 1, 2 * width + h * DK:2 * width + (h + 1) * DK].astype(BF16).astype(F32)
            c_old = c_ref[i, h]
            n_old = n_ref[i, h]
            qc = jnp.dot(jnp.broadcast_to(q, (pad, DK)), c_old.astype(BF16), preferred_element_type=F32)[0:1, :]
            v_rows = jnp.where(row_id == sb + i, jnp.broadcast_to(v * w_in, (pad, DV)), 0.0).astype(BF16)
            kv = jnp.dot(qk_cols, v_rows, preferred_element_type=F32)
            s = jnp.sum(qf * k, axis=-1, keepdims=True) * w_in
            num = w_ca * qc + s * v
            den = w_ca * jnp.sum(qf * n_old, axis=-1, keepdims=True) + s
            hid = num / jnp.maximum(jnp.abs(den), floor_all[i:i + 1, h:h + 1])
            co_ref[i, h] = w_ca * c_old + kv
            no_ref[i, h] = w_ca * n_old + k * w_in
            y_ref[i:i + 1, hs] = _head_norm_gate(hid, o_ref[i:i + 1, hs], gain_ref[:, hs])


def _mlstm_sample(qkv, o_arr, gcol, c0, n0, m0, bias_row, gain):
    nb = qkv.shape[0]
    width = HEADS * DK
    sb = MLSTM_SB
    const2 = lambda i: (0, 0)
    return pl.pallas_call(
        _mlstm_sample_kernel,
        grid=(nb // sb,),
        in_specs=[pl.BlockSpec((sb, 3 * width), lambda i: (i, 0)),
                  pl.BlockSpec((sb, width), lambda i: (i, 0)),
                  pl.BlockSpec((sb, 2 * HEADS), lambda i: (i, 0)),
                  pl.BlockSpec((sb, HEADS, DK, DV), lambda i: (i, 0, 0, 0)),
                  pl.BlockSpec((sb, HEADS, 1, DK), lambda i: (i, 0, 0, 0)),
                  pl.BlockSpec((sb, HEADS), lambda i: (i, 0)),
                  pl.BlockSpec((1, 2 * HEADS), const2),
                  pl.BlockSpec((1, width), const2)],
        out_specs=[pl.BlockSpec((sb, width), lambda i: (i, 0)),
                   pl.BlockSpec((sb, HEADS, DK, DV), lambda i: (i, 0, 0, 0)),
                   pl.BlockSpec((sb, HEADS, 1, DK), lambda i: (i, 0, 0, 0)),
                   pl.BlockSpec((sb, HEADS), lambda i: (i, 0))],
        out_shape=[jax.ShapeDtypeStruct((nb, width), F32),
                   jax.ShapeDtypeStruct((nb, HEADS, DK, DV), F32),
                   jax.ShapeDtypeStruct((nb, HEADS, 1, DK), F32),
                   jax.ShapeDtypeStruct((nb, HEADS), F32)],
        compiler_params=_cparams(1, 48),
        name="mlstm_sample",
    )(qkv, o_arr, gcol, c0, n0, m0, bias_row, gain)


def _merge_rows(ya, yb, x, wgab_ref, wglu_ref, bglu_ref, wa_ref, wb_ref, wout_ref, g1_ref, b1_ref):
    xb = x.astype(BF16)
    g = _gelu_tanh(ya)
    z = jnp.dot(g.astype(BF16), wglu_ref[...], preferred_element_type=F32) + bglu_ref[...]
    out_a = g * _sigmoid(z)
    up_a = jnp.dot(out_a.astype(BF16), wa_ref[...], preferred_element_type=F32)
    g_a = lax.dot_general(xb, wgab_ref[0:D_MODEL, :], NT_DIMS, preferred_element_type=F32)
    mix = _sigmoid(g_a) * up_a
    up_b = jnp.dot(yb, wb_ref[...], preferred_element_type=F32)
    g_b = lax.dot_general(xb, wgab_ref[D_MODEL:2 * D_MODEL, :], NT_DIMS, preferred_element_type=F32)
    mix = mix + _sigmoid(g_b) * up_b
    mo = jnp.dot(mix.astype(BF16), wout_ref[...], preferred_element_type=F32)
    return _layernorm_rows(ALPHA * x + mo, g1_ref[...], b1_ref[...])


def _merge_kernel(ya_ref, yb_ref, x_ref, yas_ref, ybs_ref, xs_ref, *rest):
    weights, (o_ref, os_ref, ob_ref, osb_ref) = rest[:-4], rest[-4:]
    x1 = _merge_rows(ya_ref[...], yb_ref[...], x_ref[...], *weights)
    o_ref[...] = x1
    ob_ref[...] = x1.astype(BF16)

    @pl.when(pl.program_id(0) == pl.num_programs(0) - 1)
    def _():
        x1s = _merge_rows(yas_ref[...], ybs_ref[...].astype(BF16), xs_ref[...], *weights)
        os_ref[...] = x1s
        osb_ref[...] = x1s.astype(BF16)


def _merge(ya, yb, x, ya_s, yb_s, x_s, wt_gab, w_glu, b_glu, w_a_up, w_b_up, w_out, ln_g, ln_b, tm):
    r = x.shape[0]
    rs = x_s.shape[0]
    const2 = lambda i: (0, 0)
    resident = functools.partial(pl.BlockSpec, index_map=const2, pipeline_mode=pl.Buffered(1))
    return pl.pallas_call(
        _merge_kernel,
        grid=(r // tm,),
        in_specs=[pl.BlockSpec((tm, SSM_WIDTH), lambda i: (i, 0)),
                  pl.BlockSpec((tm, HEADS * DV), lambda i: (i, 0)),
                  pl.BlockSpec((tm, D_MODEL), lambda i: (i, 0)),
                  resident((rs, SSM_WIDTH)),
                  resident((rs, HEADS * DV)),
                  resident((rs, D_MODEL)),
                  resident((2 * D_MODEL, D_MODEL)),
                  resident((SSM_WIDTH, SSM_WIDTH)),
                  resident((1, SSM_WIDTH)),
                  resident((SSM_WIDTH, D_MODEL)),
                  resident((HEADS * DV, D_MODEL)),
                  resident((D_MODEL, D_MODEL)),
                  resident((1, D_MODEL)),
                  resident((1, D_MODEL))],
        out_specs=[pl.BlockSpec((tm, D_MODEL), lambda i: (i, 0)),
                   pl.BlockSpec((rs, D_MODEL), const2),
                   pl.BlockSpec((tm, D_MODEL), lambda i: (i, 0)),
                   pl.BlockSpec((rs, D_MODEL), const2)],
        out_shape=[jax.ShapeDtypeStruct((r, D_MODEL), F32), jax.ShapeDtypeStruct((rs, D_MODEL), F32),
                   jax.ShapeDtypeStruct((r, D_MODEL), BF16), jax.ShapeDtypeStruct((rs, D_MODEL), BF16)],
        compiler_params=_cparams(1, 59),
        name="merge_ln1",
    )(ya, yb, x, ya_s, yb_s, x_s, wt_gab, w_glu, b_glu, w_a_up, w_b_up, w_out, ln_g, ln_b)


def _swiglu(xb, wg, wu):
    hg = jnp.dot(xb, wg, preferred_element_type=F32)
    hu = jnp.dot(xb, wu, preferred_element_type=F32)
    return ((hg * _sigmoid(hg)) * hu).astype(BF16)


FFN_UP_PIECE = 512


def _ffn_up_kernel(x_ref, xs_ref, wg_ref, wu_ref, wd_ref, h_ref, hs_ref, wdb_ref, wgb_s, wub_s):
    wgb_s[...] = wg_ref[...].astype(BF16)
    wub_s[...] = wu_ref[...].astype(BF16)

    for r0 in range(0, x_ref.shape[0], FFN_UP_PIECE):
        rows = slice(r0, r0 + FFN_UP_PIECE)
        h_ref[rows, :] = _swiglu(x_ref[rows, :], wgb_s[...], wub_s[...])
    wdb_ref[...] = wd_ref[...].astype(BF16)

    @pl.when(pl.program_id(0) == pl.num_programs(0) - 1)
    def _():
        hs_ref[...] = _swiglu(xs_ref[...], wgb_s[...], wub_s[...])


def _ffn_up(xb, xsb, w_gate3, w_up3, w_down3, tm, tf):
    r = xb.shape[0]
    rs = xsb.shape[0]
    n_f, n_i = D_FF // tf, r // tm
    wd_rows = D_FF // (n_f * n_i)
    hs_block = lambda i, f: jnp.where(i == n_i - 1, f, 0)
    return pl.pallas_call(
        _ffn_up_kernel,
        grid=(n_i, n_f),
        in_specs=[pl.BlockSpec((tm, D_MODEL), lambda i, f: (i, 0), pipeline_mode=pl.Buffered(1)),
                  pl.BlockSpec((rs, D_MODEL), lambda i, f: (0, 0)),
                  pl.BlockSpec((None, D_MODEL, tf), lambda i, f: (0, 0, f)),
                  pl.BlockSpec((None, D_MODEL, tf), lambda i, f: (0, 0, f)),
                  pl.BlockSpec((None, wd_rows, D_MODEL), lambda i, f: (0, i * n_f + f, 0))],
        out_specs=[pl.BlockSpec((tm, tf), lambda i, f: (i, f)),
                   pl.BlockSpec((rs, tf), lambda i, f: (0, hs_block(i, f))),
                   pl.BlockSpec((wd_rows, D_MODEL), lambda i, f: (i * n_f + f, 0))],
        out_shape=[jax.ShapeDtypeStruct((r, D_FF), BF16), jax.ShapeDtypeStruct((rs, D_FF), BF16),
                   jax.ShapeDtypeStruct((D_FF, D_MODEL), BF16)],
        scratch_shapes=[pltpu.VMEM((D_MODEL, tf), BF16), pltpu.VMEM((D_MODEL, tf), BF16)],
        compiler_params=_cparams(2, 58),
        name="ffn_up",
    )(xb, xsb, w_gate3, w_up3, w_down3)


FFN_DOWN_PIECE = 256


def _ffn_down_rows(h, x, wd_ref, g2_ref, b2_ref):
    return _layernorm_rows(ALPHA * x + jnp.dot(h, wd_ref[...], preferred_element_type=F32), g2_ref[...], b2_ref[...])


def _ffn_down_kernel(h_ref, x_ref, hs_ref, xs_ref, wd_ref, g2_ref, b2_ref, o_ref, os_ref):
    for r0 in range(0, h_ref.shape[0], FFN_DOWN_PIECE):
        rows = slice(r0, r0 + FFN_DOWN_PIECE)
        o_ref[rows, :] = _ffn_down_rows(h_ref[rows, :], x_ref[rows, :], wd_ref, g2_ref, b2_ref)

    @pl.when(pl.program_id(0) == pl.num_programs(0) - 1)
    def _():
        os_ref[...] = _ffn_down_rows(hs_ref[...], xs_ref[...], wd_ref, g2_ref, b2_ref)


def _ffn_down(h, x, h_s, x_s, wd_b, ln_g, ln_b, tm):
    r = x.shape[0]
    rs = x_s.shape[0]
    const2 = lambda i: (0, 0)
    resident = functools.partial(pl.BlockSpec, index_map=const2, pipeline_mode=pl.Buffered(1))
    return pl.pallas_call(
        _ffn_down_kernel,
        grid=(r // tm,),
        in_specs=[pl.BlockSpec((tm, D_FF), lambda i: (i, 0)),
                  pl.BlockSpec((tm, D_MODEL), lambda i: (i, 0)),
                  resident((rs, D_FF)),
                  resident((rs, D_MODEL)),
                  resident((D_FF, D_MODEL)),
                  resident((1, D_MODEL)),
                  resident((1, D_MODEL))],
        out_specs=[pl.BlockSpec((tm, D_MODEL), lambda i: (i, 0)),
                   pl.BlockSpec((rs, D_MODEL), const2)],
        out_shape=[jax.ShapeDtypeStruct((r, D_MODEL), F32), jax.ShapeDtypeStruct((rs, D_MODEL), F32)],
        compiler_params=_cparams(1, 59),
        name="ffn_down_ln2",
    )(h, x, h_s, x_s, wd_b, ln_g, ln_b)


def kernel(x_prompt, x_sample, state_ssm_re, state_ssm_im, state_mlstm_c, state_mlstm_n, state_mlstm_m,
           meta_tokens, w_in, b_if, ssm_a_re, ssm_a_im, ssm_log_dt, ssm_b_re, ssm_b_im, ssm_c_re, ssm_c_im,
           ssm_d, w_glu, b_glu, w_a_up, mh_gain, w_b_up, w_out, ln1_g, ln1_b, w_gate, w_up, w_down,
           ln2_g, ln2_b):
    batch, seq, _ = x_prompt.shape
    nb = x_sample.shape[0]
    width = HEADS * DK

    bias_row = b_if[0].reshape(1, 2 * HEADS)
    bias_col = b_if[0].reshape(2 * HEADS, 1)
    gain = mh_gain[0].reshape(1, width)
    d_skip = ssm_d[0].reshape(1, SSM_WIDTH)
    b_glu_r = b_glu[0].reshape(1, SSM_WIDTH)
    ln1 = (ln1_g[0].reshape(1, D_MODEL), ln1_b[0].reshape(1, D_MODEL))
    ln2 = (ln2_g[0].reshape(1, D_MODEL), ln2_b[0].reshape(1, D_MODEL))
    abar_re, abar_im, bt, ct = _s5_discretise(
        ssm_a_re[0], ssm_a_im[0], ssm_log_dt[0], ssm_b_re[0], ssm_b_im[0], ssm_c_re[0], ssm_c_im[0])

    xp = x_prompt.reshape(batch * seq, D_MODEL)
    xs = x_sample.reshape(nb, D_MODEL)
    xsm = jnp.concatenate([xs, meta_tokens], axis=0)
    w_in_t = jnp.swapaxes(w_in, 1, 2)
    uo_blocks, qkv_blocks = (0, 4, 2), (1, 1, 3)

    uo_p, uo_s, gcol_p, grow_p, gcol_s = _proj(xp, xsm, w_in_t, uo_blocks, F32, F32, 1024, "proj_uo", with_gates=True)
    qkv_p, qkv_s = _proj(xp, xsm, w_in_t, qkv_blocks, BF16, F32, 1024, "proj_qkv")

    u_meta = uo_s[nb:nb + N_META, :SSM_WIDTH]
    qkv_meta = qkv_s[nb:nb + N_META].astype(BF16)
    gcol_meta = gcol_s[nb:nb + N_META]

    ya_p, pf_re, pf_im = _s5_prompt(uo_p, 0, u_meta, abar_re, abar_im, bt, ct, d_skip, batch, seq)
    yb_p, pc, pn, pm, wt_gab, w_out_b, w_a_b, w_b_b, w_glu_b = _mlstm_prompt(
        qkv_p, uo_p, 1, gcol_p, grow_p, qkv_meta, gcol_meta, bias_row, bias_col, gain, batch, seq,
        w_in_t, (w_out, w_a_up, w_b_up, w_glu))
    ya_s, sf_re, sf_im = _s5_sample(uo_s[:nb, :SSM_WIDTH], state_ssm_re[0].reshape(nb, SSM_FLAT),
                                    state_ssm_im[0].reshape(nb, SSM_FLAT), abar_re, abar_im, bt, ct, d_skip)
    yb_s, sc, sn, sm = _mlstm_sample(qkv_s[:nb], uo_s[:nb, SSM_WIDTH:], gcol_s[:nb], state_mlstm_c[0],
                                     state_mlstm_n[0].reshape(nb, HEADS, 1, DK), state_mlstm_m[0], bias_row, gain)

    x1_p, x1_s, x1b_p, x1b_s = _merge(ya_p, yb_p.reshape(batch * seq, width), xp, ya_s, yb_s, xs,
                                      wt_gab, w_glu_b, b_glu_r, w_a_b, w_b_b, w_out_b, *ln1, tm=256)
    h_p, h_s, w_down_b = _ffn_up(x1b_p, x1b_s, w_gate, w_up, w_down, tm=4096, tf=512)
    y_p, y_s = _ffn_down(h_p, x1_p, h_s, x1_s, w_down_b, *ln2, tm=512)

    return (y_p.reshape(batch, seq, D_MODEL),
            y_s.reshape(nb, 1, D_MODEL),
            pf_re.reshape(1, batch, SSM_GROUPS, SSM_STATE),
            pf_im.reshape(1, batch, SSM_GROUPS, SSM_STATE),
            pc.reshape(1, batch, HEADS, DK, DV),
            pn.reshape(1, batch, HEADS, DK),
            pm[:, :, 0].reshape(1, batch, HEADS),
            sf_re.reshape(1, nb, SSM_GROUPS, SSM_STATE),
            sf_im.reshape(1, nb, SSM_GROUPS, SSM_STATE),
            sc.reshape(1, nb, HEADS, DK, DV),
            sn.reshape(1, nb, HEADS, DK),
            sm.reshape(1, nb, HEADS))
```

```python
import functools

import jax
import jax.numpy as jnp
import numpy as np
from jax import lax
from jax.experimental import pallas as pl
from jax.experimental.pallas import tpu as pltpu

F32 = jnp.float32
BF16 = jnp.bfloat16

D_MODEL = 2048
N_META = 16
SSM_WIDTH = 1024
SSM_GROUP = 16
SSM_GROUPS = 64
SSM_STATE = 64
SSM_FLAT = SSM_GROUPS * SSM_STATE
SSM_JBLK = 4
SSM_BLK_IN = SSM_WIDTH // SSM_JBLK
SSM_BLK_STATE = SSM_FLAT // SSM_JBLK
LANE_TILE = 128
HEADS = 4
DK = 256
DV = 256
CHUNK = 128
D_FF = 5632
EPS = 1e-5
ALPHA = 2.0 ** 0.25

S5_TC = 256
S5_NSEG = 8
S5_SEG = S5_TC // S5_NSEG
S5_LW = 512
S5_CPS = 2

V7X_SCOPED_VMEM_MAX_BYTES = 60000 * 1024


def _cparams(n_axes, vmem_mb):
    return pltpu.CompilerParams(
        dimension_semantics=("arbitrary",) * n_axes,
        vmem_limit_bytes=min(vmem_mb * 1024 * 1024, V7X_SCOPED_VMEM_MAX_BYTES),
    )


def _sigmoid(x):
    return 1.0 / (1.0 + jnp.exp(-x))


def _log_sigmoid(x):
    return jnp.minimum(x, 0.0) - jnp.log(1.0 + jnp.exp(-jnp.abs(x)))


def _gelu_tanh(x):
    c = 0.7978845608028654
    return 0.5 * x * (1.0 + jnp.tanh(c * (x + 0.044715 * (x * x * x))))


def _layernorm_rows(x, g, b):
    mu = jnp.mean(x, axis=-1, keepdims=True)
    xc = x - mu
    var = jnp.mean(xc * xc, axis=-1, keepdims=True)
    return xc * lax.rsqrt(var + EPS) * g + b


PROJ_TN = 1024
GATE_ROWS = 2 * HEADS
TAIL_ROW0 = SSM_WIDTH + 4 * HEADS * DK
NT_DIMS = (((1,), (1,)), ((), ()))


def _proj_kernel(with_gates, x_ref, xs_ref, wt_ref, *rest):
    if with_gates:
        wg_ref, o_ref, os_ref, g_ref, gt_ref, gs_ref, wb_s = rest
    else:
        o_ref, os_ref, wb_s = rest
    i = pl.program_id(1)

    @pl.when(i == 0)
    def _():
        wb_s[...] = wt_ref[...].astype(BF16)

    xb = x_ref[...].astype(BF16)
    o_ref[...] = lax.dot_general(xb, wb_s[...], NT_DIMS, preferred_element_type=F32).astype(o_ref.dtype)
    first_block = pl.program_id(0) == 0
    if with_gates:
        @pl.when(first_block)
        def _():
            wg = wg_ref[...].astype(BF16)
            g_ref[...] = lax.dot_general(xb, wg, NT_DIMS, preferred_element_type=F32)
            gt_ref[...] = lax.dot_general(wg, xb, NT_DIMS, preferred_element_type=F32)

    @pl.when(i == pl.num_programs(1) - 1)
    def _():
        xsb = xs_ref[...].astype(BF16)
        os_ref[...] = lax.dot_general(xsb, wb_s[...], NT_DIMS, preferred_element_type=F32).astype(os_ref.dtype)
        if with_gates:
            @pl.when(first_block)
            def _():
                gs_ref[...] = lax.dot_general(xsb, wg_ref[...].astype(BF16), NT_DIMS, preferred_element_type=F32)


def _proj(x, xs, wt3, blocks, out_dtype, small_dtype, tm, name, with_gates=False):
    r, k = x.shape
    rs = xs.shape[0]
    first, step, count = blocks
    tn = PROJ_TN
    nm = r // tm
    in_specs = [pl.BlockSpec((tm, k), lambda j, i: (i, 0)),
                pl.BlockSpec((rs, k), lambda j, i: (0, 0)),
                pl.BlockSpec((None, tn, k), lambda j, i: (0, first + step * j, 0))]
    out_specs = [pl.BlockSpec((tm, tn), lambda j, i: (i, j)),
                 pl.BlockSpec((rs, tn), lambda j, i: (0, j))]
    out_shape = [jax.ShapeDtypeStruct((r, count * tn), out_dtype),
                 jax.ShapeDtypeStruct((rs, count * tn), small_dtype)]
    args = (x, xs, wt3)
    if with_gates:
        ng = GATE_ROWS
        in_specs += [pl.BlockSpec((None, ng, k), lambda j, i: (0, TAIL_ROW0 // ng, 0))]
        gate_block = lambda j, i: jnp.where(j == 0, i, nm - 1)
        out_specs += [pl.BlockSpec((tm, ng), lambda j, i: (gate_block(j, i), 0)),
                      pl.BlockSpec((ng, tm), lambda j, i: (0, gate_block(j, i))),
                      pl.BlockSpec((rs, ng), lambda j, i: (0, 0))]
        out_shape += [jax.ShapeDtypeStruct((r, ng), F32), jax.ShapeDtypeStruct((ng, r), F32),
                      jax.ShapeDtypeStruct((rs, ng), F32)]
        args = (x, xs, wt3, wt3)
    return pl.pallas_call(
        functools.partial(_proj_kernel, with_gates),
        grid=(count, nm),
        in_specs=in_specs,
        out_specs=out_specs,
        out_shape=out_shape,
        scratch_shapes=[pltpu.VMEM((tn, k), BF16)],
        compiler_params=_cparams(2, 56),
        name=name,
    )(*args)


def _s5_discretise(a_re, a_im, log_dt, b_re, b_im, c_re, c_im):
    dt = jnp.exp(log_dt)
    e = jnp.exp(a_re * dt)
    abar_re = e * jnp.cos(a_im * dt)
    abar_im = e * jnp.sin(a_im * dt)
    nr = abar_re - 1.0
    ni = abar_im
    den = a_re * a_re + a_im * a_im
    coef_re = (nr * a_re + ni * a_im) / den
    coef_im = (ni * a_re - nr * a_im) / den
    bb_re = coef_re[..., None] * b_re - coef_im[..., None] * b_im
    bb_im = coef_re[..., None] * b_im + coef_im[..., None] * b_re
    gpb = SSM_GROUPS // SSM_JBLK

    def b_rows(bb):
        t = bb.reshape(SSM_JBLK, gpb, SSM_STATE, SSM_GROUP).transpose(0, 1, 3, 2)
        return t.reshape(SSM_JBLK, SSM_BLK_IN, SSM_STATE)

    def c_rows(cc):
        t = cc.reshape(SSM_JBLK, gpb, SSM_GROUP, SSM_STATE).transpose(0, 1, 3, 2)
        return t.reshape(SSM_JBLK, SSM_BLK_STATE, SSM_GROUP)

    b_rep, c_rep = LANE_TILE // SSM_STATE, LANE_TILE // SSM_GROUP
    bt = jnp.concatenate([jnp.tile(b_rows(bb_re), (1, 1, b_rep)), jnp.tile(b_rows(bb_im), (1, 1, b_rep))], axis=-1)
    ct = jnp.concatenate([jnp.tile(c_rows(c_re), (1, 1, c_rep)), jnp.tile(c_rows(-c_im), (1, 1, c_rep))], axis=-1)
    return abar_re.reshape(1, SSM_FLAT), abar_im.reshape(1, SSM_FLAT), bt.astype(BF16), ct.astype(BF16)


def _s5_expand_weights(bt_ref, ct_ref, bw_s, cwr_s, cwi_s):
    log2 = lambda n: n.bit_length() - 1
    rb = lax.broadcasted_iota(jnp.int32, (SSM_BLK_IN, 2 * SSM_BLK_STATE), 0)
    cb = lax.broadcasted_iota(jnp.int32, (SSM_BLK_IN, 2 * SSM_BLK_STATE), 1)
    bmask = (rb >> log2(SSM_GROUP)) == ((cb & (SSM_BLK_STATE - 1)) >> log2(SSM_STATE))
    rc = lax.broadcasted_iota(jnp.int32, (SSM_BLK_STATE, SSM_BLK_IN), 0)
    cc = lax.broadcasted_iota(jnp.int32, (SSM_BLK_STATE, SSM_BLK_IN), 1)
    cmask = (rc >> log2(SSM_STATE)) == (cc >> log2(SSM_GROUP))
    b_tiles, c_tiles = SSM_BLK_STATE // LANE_TILE, SSM_BLK_IN // LANE_TILE
    for j in range(SSM_JBLK):
        bt = bt_ref[j].astype(F32)
        wide = jnp.concatenate([bt[:, :LANE_TILE]] * b_tiles + [bt[:, LANE_TILE:]] * b_tiles, axis=1)
        bw_s[j] = jnp.where(bmask, wide, 0.0).astype(BF16)
        ct = ct_ref[j].astype(F32)
        cwr_s[j] = jnp.where(cmask, jnp.concatenate([ct[:, :LANE_TILE]] * c_tiles, axis=1), 0.0).astype(BF16)
        cwi_s[j] = jnp.where(cmask, jnp.concatenate([ct[:, LANE_TILE:]] * c_tiles, axis=1), 0.0).astype(BF16)


def _cmul_add(ar, ai, hr, hi, xr, xi):
    return ar * hr - ai * hi + xr, ar * hi + ai * hr + xi


def _s5_project_in(u, b_ref, hre_ref, him_ref, rows):
    for j in range(SSM_JBLK):
        state = slice(j * SSM_BLK_STATE, (j + 1) * SSM_BLK_STATE)
        bu = jnp.dot(u[:, j * SSM_BLK_IN:(j + 1) * SSM_BLK_IN], b_ref[j], preferred_element_type=F32)
        hre_ref[0:rows, state] = bu[:, :SSM_BLK_STATE]
        him_ref[0:rows, state] = bu[:, SSM_BLK_STATE:]


def _s5_project_out(hre_ref, him_ref, cre_ref, cimn_ref, rows):
    ys = []
    for j in range(SSM_JBLK):
        state = slice(j * SSM_BLK_STATE, (j + 1) * SSM_BLK_STATE)
        hr = hre_ref[0:rows, state].astype(BF16)
        hi = him_ref[0:rows, state].astype(BF16)
        y = jnp.dot(hr, cre_ref[j], preferred_element_type=F32)
        ys.append(y + jnp.dot(hi, cimn_ref[j], preferred_element_type=F32))
    return jnp.concatenate(ys, axis=1)


def _s5_scan_chunk(hre_ref, him_ref, hsre_ref, hsim_ref, pre_ref, pim_ref, are_ref, aim_ref, cre_s, cim_s):
    full = (S5_NSEG, S5_LW)
    for lg in range(SSM_FLAT // S5_LW):
        lanes = slice(lg * S5_LW, (lg + 1) * S5_LW)
        ar = jnp.broadcast_to(are_ref[:, lanes], full)
        ai = jnp.broadcast_to(aim_ref[:, lanes], full)

        hr = jnp.zeros(full, F32)
        hi = jnp.zeros(full, F32)
        for k in range(S5_SEG):
            rows = slice(k * S5_NSEG, (k + 1) * S5_NSEG)
            hr, hi = _cmul_add(ar, ai, hr, hi, hre_ref[rows, lanes], him_ref[rows, lanes])
            hre_ref[rows, lanes] = hr
            him_ref[rows, lanes] = hi
        er, ei = hr, hi

        asr = pre_ref[S5_TC - 1:S5_TC, lanes]
        asi = pim_ref[S5_TC - 1:S5_TC, lanes]
        hr = cre_s[:, lanes]
        hi = cim_s[:, lanes]
        for j in range(S5_NSEG):
            hsre_ref[j:j + 1, lanes] = hr
            hsim_ref[j:j + 1, lanes] = hi
            hr, hi = _cmul_add(asr, asi, hr, hi, er[j:j + 1, :], ei[j:j + 1, :])
        cre_s[:, lanes] = hr
        cim_s[:, lanes] = hi

        sr = hsre_ref[:, lanes]
        si = hsim_ref[:, lanes]
        for k in range(S5_SEG):
            rows = slice(k * S5_NSEG, (k + 1) * S5_NSEG)
            pr = pre_ref[rows, lanes]
            pi = pim_ref[rows, lanes]
            hre_ref[rows, lanes] = hre_ref[rows, lanes] + (pr * sr - pi * si)
            him_ref[rows, lanes] = him_ref[rows, lanes] + (pr * si + pi * sr)


N_CAST_IN = 6


def _weight_cast_specs(n_steps, step_of, w_in_t, others):
    gab_rows = 2 * D_MODEL // n_steps
    gab_blk0 = TAIL_ROW0 // gab_rows
    gab_next0 = (TAIL_ROW0 + gab_rows) // GATE_ROWS
    rows = [w.shape[1] // n_steps for w in others]
    in_specs = [pl.BlockSpec((None, gab_rows, D_MODEL), lambda *g: (0, gab_blk0 + step_of(*g), 0)),
                pl.BlockSpec((None, GATE_ROWS, D_MODEL),
                             lambda *g: (0, gab_next0 + (gab_rows // GATE_ROWS) * step_of(*g), 0))]
    in_specs += [pl.BlockSpec((None, nr, w.shape[2]), lambda *g: (0, step_of(*g), 0)) for nr, w in zip(rows, others)]
    out_specs = [pl.BlockSpec((gab_rows, D_MODEL), lambda *g: (step_of(*g), 0))]
    out_specs += [pl.BlockSpec((nr, w.shape[2]), lambda *g: (step_of(*g), 0)) for nr, w in zip(rows, others)]
    out_shapes = [jax.ShapeDtypeStruct((2 * D_MODEL, D_MODEL), BF16)]
    out_shapes += [jax.ShapeDtypeStruct(w.shape[1:], BF16) for w in others]
    return in_specs, out_specs, out_shapes, (w_in_t, w_in_t) + tuple(others)


def _weight_cast_step(in_refs, out_refs):
    wga_ref, wgn_ref = in_refs[:2]
    out_refs[0][...] = jnp.concatenate([wga_ref[GATE_ROWS:, :], wgn_ref[...]], axis=0).astype(BF16)
    for src, dst in zip(in_refs[2:], out_refs[1:]):
        dst[...] = src[...].astype(BF16)


def _s5_prompt_kernel(u_ref, um_ref, perm_ref, permt_ref, bt_ref, ct_ref, are_ref, aim_ref, d_ref,
                      y_ref, fre_ref, fim_ref,
                      hre_ref, him_ref, hre2_ref, him2_ref, pre_ref, pim_ref, cre_s, cim_s,
                      hsre_ref, hsim_ref, hsre2_ref, hsim2_ref,
                      b_ref, cre_ref, cimn_ref):
    b_id = pl.program_id(0)
    c_id = pl.program_id(1)
    n_c = pl.num_programs(1)

    @pl.when(jnp.logical_and(b_id == 0, c_id == 0))
    def _():
        _s5_expand_weights(bt_ref, ct_ref, b_ref, cre_ref, cimn_ref)
        ar = jnp.broadcast_to(are_ref[...], (S5_NSEG, SSM_FLAT))
        ai = jnp.broadcast_to(aim_ref[...], (S5_NSEG, SSM_FLAT))
        pre_ref[0:S5_NSEG, :] = ar
        pim_ref[0:S5_NSEG, :] = ai

        def body(i, carry):
            pr, pi = carry
            nr = ar * pr - ai * pi
            ni = ar * pi + ai * pr
            rows = pl.ds(pl.multiple_of(i * S5_NSEG, S5_NSEG), S5_NSEG)
            pre_ref[rows, :] = nr
            pim_ref[rows, :] = ni
            return nr, ni

        lax.fori_loop(1, S5_SEG, body, (ar, ai))

    @pl.when(c_id == 0)
    def _():
        _s5_project_in(um_ref[...].astype(BF16), b_ref, hre_ref, him_ref, N_META)
        ar = are_ref[...]
        ai = aim_ref[...]
        hr = jnp.zeros((1, SSM_FLAT), F32)
        hi = jnp.zeros((1, SSM_FLAT), F32)
        for t in range(N_META):
            hr, hi = _cmul_add(ar, ai, hr, hi, hre_ref[t:t + 1, :], him_ref[t:t + 1, :])
        cre_s[...] = hr
        cim_s[...] = hi

    bufs = ((hre_ref, him_ref, hsre_ref, hsim_ref), (hre2_ref, him2_ref, hsre2_ref, hsim2_ref))
    us = [u_ref[n * S5_TC:(n + 1) * S5_TC, :] for n in range(S5_CPS)]
    for n in range(S5_CPS):
        u_perm = jnp.dot(perm_ref[...], us[n].astype(BF16), preferred_element_type=F32).astype(BF16)
        _s5_project_in(u_perm, b_ref, bufs[n][0], bufs[n][1], S5_TC)
    for n in range(S5_CPS):
        hre_n, him_n, hsre_n, hsim_n = bufs[n]
        _s5_scan_chunk(hre_n, him_n, hsre_n, hsim_n, pre_ref, pim_ref, are_ref, aim_ref, cre_s, cim_s)
        y_perm = _s5_project_out(hre_n, him_n, cre_ref, cimn_ref, S5_TC)
        y = sum(jnp.dot(permt_ref[...], piece, preferred_element_type=F32) for piece in _split3(y_perm))
        y_ref[n * S5_TC:(n + 1) * S5_TC, :] = y + d_ref[...] * us[n]

    @pl.when(c_id == n_c - 1)
    def _():
        fre_ref[0] = cre_s[...]
        fim_ref[0] = cim_s[...]


def _s5_weight_scratch():
    return [pltpu.VMEM((SSM_JBLK, SSM_BLK_IN, 2 * SSM_BLK_STATE), BF16),
            pltpu.VMEM((SSM_JBLK, SSM_BLK_STATE, SSM_BLK_IN), BF16),
            pltpu.VMEM((SSM_JBLK, SSM_BLK_STATE, SSM_BLK_IN), BF16)]


def _s5_prompt(u_arr, u_col_block, u_meta, abar_re, abar_im, bt, ct, d_skip, batch, seq):
    rows_step = S5_CPS * S5_TC
    n_c = seq // rows_step
    const3 = lambda b, c: (0, 0, 0)
    const2 = lambda b, c: (0, 0)
    rows = np.arange(S5_TC)
    perm_np = np.zeros((S5_TC, S5_TC), np.float32)
    perm_np[rows, (rows % S5_NSEG) * S5_SEG + rows // S5_NSEG] = 1.0
    perm = jnp.asarray(perm_np, dtype=BF16)
    permt = jnp.asarray(perm_np.T, dtype=BF16)
    return pl.pallas_call(
        _s5_prompt_kernel,
        grid=(batch, n_c),
        in_specs=[pl.BlockSpec((rows_step, SSM_WIDTH), lambda b, c: (b * n_c + c, u_col_block)),
                  pl.BlockSpec((N_META, SSM_WIDTH), const2),
                  pl.BlockSpec((S5_TC, S5_TC), const2),
                  pl.BlockSpec((S5_TC, S5_TC), const2),
                  pl.BlockSpec((SSM_JBLK, SSM_BLK_IN, 2 * LANE_TILE), const3, pipeline_mode=pl.Buffered(1)),
                  pl.BlockSpec((SSM_JBLK, SSM_BLK_STATE, 2 * LANE_TILE), const3, pipeline_mode=pl.Buffered(1)),
                  pl.BlockSpec((1, SSM_FLAT), const2),
                  pl.BlockSpec((1, SSM_FLAT), const2),
                  pl.BlockSpec((1, SSM_WIDTH), const2)],
        out_specs=[pl.BlockSpec((rows_step, SSM_WIDTH), lambda b, c: (b * n_c + c, 0)),
                   pl.BlockSpec((1, 1, SSM_FLAT), lambda b, c: (b, 0, 0)),
                   pl.BlockSpec((1, 1, SSM_FLAT), lambda b, c: (b, 0, 0))],
        out_shape=[jax.ShapeDtypeStruct((batch * seq, SSM_WIDTH), F32),
                   jax.ShapeDtypeStruct((batch, 1, SSM_FLAT), F32),
                   jax.ShapeDtypeStruct((batch, 1, SSM_FLAT), F32)],
        scratch_shapes=[pltpu.VMEM((S5_TC, SSM_FLAT), F32)] * 4
                       + [pltpu.VMEM((S5_TC, SSM_FLAT), F32)] * 2
                       + [pltpu.VMEM((1, SSM_FLAT), F32)] * 2
                       + [pltpu.VMEM((S5_NSEG, SSM_FLAT), F32)] * 4
                       + _s5_weight_scratch(),
        compiler_params=_cparams(2, 59),
        name="s5_prompt",
    )(u_arr, u_meta, perm, permt, bt, ct, abar_re, abar_im, d_skip)


def _s5_sample_kernel(u_ref, h0re_ref, h0im_ref, bt_ref, ct_ref, are_ref, aim_ref, d_ref,
                      y_ref, hre_ref, him_ref, b_ref, cre_ref, cimn_ref):
    rows = u_ref.shape[0]
    _s5_expand_weights(bt_ref, ct_ref, b_ref, cre_ref, cimn_ref)
    u = u_ref[...]
    _s5_project_in(u.astype(BF16), b_ref, hre_ref, him_ref, rows)
    nr, ni = _cmul_add(are_ref[...], aim_ref[...], h0re_ref[...], h0im_ref[...], hre_ref[...], him_ref[...])
    hre_ref[...] = nr
    him_ref[...] = ni
    y_ref[...] = _s5_project_out(hre_ref, him_ref, cre_ref, cimn_ref, rows) + d_ref[...] * u


def _s5_sample(u, h0_re, h0_im, abar_re, abar_im, bt, ct, d_skip):
    rows = u.shape[0]
    return pl.pallas_call(
        _s5_sample_kernel,
        out_shape=[jax.ShapeDtypeStruct((rows, SSM_WIDTH), F32),
                   jax.ShapeDtypeStruct((rows, SSM_FLAT), F32),
                   jax.ShapeDtypeStruct((rows, SSM_FLAT), F32)],
        scratch_shapes=_s5_weight_scratch(),
        compiler_params=_cparams(0, 48),
        name="s5_sample",
    )(u, h0_re, h0_im, bt, ct, abar_re, abar_im, d_skip)


def _split3(x):
    hi = x.astype(BF16)
    r1 = x - hi.astype(F32)
    mid = r1.astype(BF16)
    lo = (r1 - mid.astype(F32)).astype(BF16)
    return hi, mid, lo


def _tri(n, lower):
    r = lax.broadcasted_iota(jnp.int32, (n, n), 0)
    c = lax.broadcasted_iota(jnp.int32, (n, n), 1)
    return (r >= c) if lower else (r <= c)


def _gate_columns(gcol, bias_row):
    n = gcol.shape[0]
    g = gcol + bias_row
    fl = _log_sigmoid(g[:, HEADS:])
    tril = jnp.where(_tri(n, True), 1.0, 0.0).astype(BF16)
    b = sum(jnp.dot(tril, p, preferred_element_type=F32) for p in _split3(fl))
    return g[:, :HEADS], b


def _gate_rows(grow, bias_col):
    n = grow.shape[1]
    g = grow + bias_col
    fl = _log_sigmoid(g[HEADS:, :])
    triu = jnp.where(_tri(n, False), 1.0, 0.0).astype(BF16)
    b = sum(jnp.dot(p, triu, preferred_element_type=F32) for p in _split3(fl))
    return g[:HEADS, :], b


MLSTM_GROUP = 4


def _eye_bf16(n):
    return jnp.where(_tri(n, True) & _tri(n, False), 1.0, 0.0).astype(BF16)


def _mlstm_state_update(k_bf, v, ig_col, b_col, c_old, n_old, m_old, k_t=None, gate_rows=None):
    n_rows = k_bf.shape[0]
    k_scale = DK ** -0.5
    b_last = b_col[n_rows - 1:n_rows, :]
    e_col = b_last - b_col + ig_col
    m_new = jnp.maximum(b_last + m_old, jnp.max(e_col, axis=0, keepdims=True))
    w_end = jnp.exp(e_col - m_new) * k_scale
    carry = jnp.exp(b_last + m_old - m_new)
    if k_t is None:
        k_t = lax.dot_general(_eye_bf16(DK), k_bf, NT_DIMS, preferred_element_type=F32)
    if gate_rows is None:
        kv = jnp.dot(k_t.astype(BF16), (v.astype(F32) * w_end).astype(BF16), preferred_element_type=F32)
    else:
        ig_row, b_row = gate_rows
        w_end_row = jnp.exp(b_last - b_row + ig_row - m_new) * k_scale
        kv = jnp.dot((k_t * w_end_row).astype(BF16), v, preferred_element_type=F32)
    c_new = carry * c_old + kv
    n_new = carry * n_old + jnp.sum(k_bf.astype(F32) * w_end, axis=0, keepdims=True)
    return c_new, n_new, m_new


def _mlstm_decay(ig_row, b_row, b_col, m_old):
    n_rows = b_col.shape[0]
    dmat = jnp.where(_tri(n_rows, True), b_col + (ig_row - b_row), -jnp.inf)
    inter = b_col + m_old
    m_t = jnp.maximum(inter, jnp.max(dmat, axis=-1, keepdims=True))
    return jnp.exp(dmat - m_t), jnp.exp(inter - m_t), m_t


def _head_norm_gate(h, o, gain):
    mu = jnp.mean(h, axis=-1, keepdims=True)
    hc = h - mu
    var = jnp.mean(hc * hc, axis=-1, keepdims=True)
    return _sigmoid(o) * (hc * lax.rsqrt(var + EPS) * gain)


def _mlstm_prompt_kernel(batch, q_ref, k_ref, v_ref, o_ref, gc_ref, *rest):
    gr_refs = rest[:batch]
    rest = rest[batch:]
    km_ref, vm_ref, gcm_ref, brow_ref, bcol_ref, gain_ref = rest[:6]
    cast_in = rest[6:6 + N_CAST_IN]
    y_ref, cf_ref, nf_ref, mf_ref = rest[6 + N_CAST_IN:10 + N_CAST_IN]
    cast_out = rest[10 + N_CAST_IN:]
    k_scale = DK ** -0.5

    _weight_cast_step(cast_in, cast_out)

    @pl.when(pl.program_id(0) == 0)
    def _():
        ig_c, b_c = _gate_columns(gcm_ref[...], brow_ref[...])
        for h in range(HEADS):
            hs = slice(h * DK, (h + 1) * DK)
            c_new, n_new, m_new = _mlstm_state_update(
                km_ref[:, hs], vm_ref[:, hs], ig_c[:, h:h + 1], b_c[:, h:h + 1],
                jnp.zeros((DK, DV), F32), jnp.zeros((1, DK), F32), jnp.zeros((1, 1), F32))
            for b in range(batch):
                cf_ref[b, h] = c_new
                nf_ref[b, h] = n_new
                mf_ref[b, h:h + 1, :] = jnp.broadcast_to(m_new, (1, LANE_TILE))

    eye = _eye_bf16(DK)
    for b0 in range(0, batch, MLSTM_GROUP):
        gates = {}
        for b in range(b0, b0 + MLSTM_GROUP):
            gates[b] = _gate_columns(gc_ref[b], brow_ref[...]) + _gate_rows(gr_refs[b][...], bcol_ref[...])
        chains = [(b, h) for b in range(b0, b0 + MLSTM_GROUP) for h in range(HEADS)]

        ops = []
        for b, h in chains:
            hs = slice(h * DK, (h + 1) * DK)
            q = q_ref[b, :, hs]
            k_bf = k_ref[b, :, hs]
            c_old = cf_ref[b, h]
            ops.append(dict(
                q=q, k_bf=k_bf, v=v_ref[b, :, hs], c_old=c_old, n_old=nf_ref[b, h], m_old=mf_ref[b, h:h + 1, 0:1],
                s_raw=lax.dot_general(q, k_bf, NT_DIMS, preferred_element_type=F32),
                qc=jnp.dot(q, c_old.astype(BF16), preferred_element_type=F32),
                k_t=lax.dot_general(eye, k_bf, NT_DIMS, preferred_element_type=F32)))

        for (b, h), c in zip(chains, ops):
            ig_c, b_c, ig_r, b_r = gates[b]
            c["w_intra"], c["w_inter"], c["m_t"] = _mlstm_decay(ig_r[h:h + 1, :], b_r[h:h + 1, :], b_c[:, h:h + 1],
                                                                c["m_old"])

        for (b, h), c in zip(chains, ops):
            hs = slice(h * DK, (h + 1) * DK)
            s = c["s_raw"] * (c["w_intra"] * k_scale)
            num = c["w_inter"] * c["qc"] + jnp.dot(s.astype(BF16), c["v"], preferred_element_type=F32)
            den = (c["w_inter"] * jnp.sum(c["q"].astype(F32) * c["n_old"], axis=-1, keepdims=True)
                   + jnp.sum(s, axis=-1, keepdims=True))
            hid = num / jnp.maximum(jnp.abs(den), jnp.exp(-c["m_t"]))
            y_ref[b, :, hs] = _head_norm_gate(hid, o_ref[b, :, hs], gain_ref[:, hs]).astype(y_ref.dtype)

        for (b, h), c in zip(chains, ops):
            ig_c, b_c, ig_r, b_r = gates[b]
            c_new, n_new, m_new = _mlstm_state_update(c["k_bf"], c["v"], ig_c[:, h:h + 1], b_c[:, h:h + 1],
                                                      c["c_old"], c["n_old"], c["m_old"], k_t=c["k_t"],
                                                      gate_rows=(ig_r[h:h + 1, :], b_r[h:h + 1, :]))
            cf_ref[b, h] = c_new
            nf_ref[b, h] = n_new
            mf_ref[b, h:h + 1, :] = jnp.broadcast_to(m_new, (1, LANE_TILE))


def _mlstm_prompt(qkv, o_arr, o_col_block, gcol, grow, qkv_meta, gcol_meta, bias_row, bias_col, gain, batch, seq,
                  w_in_t, cast_weights):
    n_c = seq // CHUNK
    width = HEADS * DK
    const2 = lambda c: (0, 0)
    cast_in, cast_out, cast_shapes, cast_args = _weight_cast_specs(n_c, lambda c: c, w_in_t, cast_weights)
    qkv3 = qkv.reshape(batch, seq, 3 * width)
    o3 = o_arr.reshape(batch, seq, o_arr.shape[1])
    gcol3 = gcol.reshape(gcol.shape[0] // seq, seq, 2 * HEADS)
    state0 = lambda c: (0, 0, 0, 0)
    return pl.pallas_call(
        functools.partial(_mlstm_prompt_kernel, batch),
        grid=(n_c,),
        in_specs=[pl.BlockSpec((batch, CHUNK, width), lambda c: (0, c, 0)),
                  pl.BlockSpec((batch, CHUNK, width), lambda c: (0, c, 1)),
                  pl.BlockSpec((batch, CHUNK, width), lambda c: (0, c, 2)),
                  pl.BlockSpec((batch, CHUNK, width), lambda c: (0, c, o_col_block)),
                  pl.BlockSpec((batch, CHUNK, 2 * HEADS), lambda c: (0, c, 0))]
                 + [pl.BlockSpec((2 * HEADS, CHUNK), functools.partial(lambda b, c: (0, b * n_c + c), b))
                    for b in range(batch)]
                 + [pl.BlockSpec((N_META, width), lambda c: (0, 1)),
                    pl.BlockSpec((N_META, width), lambda c: (0, 2)),
                    pl.BlockSpec((N_META, 2 * HEADS), const2),
                    pl.BlockSpec((1, 2 * HEADS), const2),
                    pl.BlockSpec((2 * HEADS, 1), const2),
                    pl.BlockSpec((1, width), const2)] + cast_in,
        out_specs=[pl.BlockSpec((batch, CHUNK, width), lambda c: (0, c, 0)),
                   pl.BlockSpec((batch, HEADS, DK, DV), state0),
                   pl.BlockSpec((batch, HEADS, 1, DK), state0),
                   pl.BlockSpec((batch, HEADS, LANE_TILE), lambda c: (0, 0, 0))] + cast_out,
        out_shape=[jax.ShapeDtypeStruct((batch, seq, width), BF16),
                   jax.ShapeDtypeStruct((batch, HEADS, DK, DV), F32),
                   jax.ShapeDtypeStruct((batch, HEADS, 1, DK), F32),
                   jax.ShapeDtypeStruct((batch, HEADS, LANE_TILE), F32)] + cast_shapes,
        compiler_params=_cparams(1, 52),
        name="mlstm_prompt",
    )(qkv3, qkv3, qkv3, o3, gcol3, *([grow] * batch), qkv_meta, qkv_meta, gcol_meta, bias_row, bias_col, gain,
      *cast_args)


MLSTM_SB = 8


def _mlstm_sample_kernel(qkv_ref, o_ref, g_ref, c_ref, n_ref, m_ref, brow_ref, gain_ref,
                         y_ref, co_ref, no_ref, mo_ref):
    sb = MLSTM_SB
    k_scale = DK ** -0.5
    eye = _eye_bf16(DK)
    width = HEADS * DK
    pad = 2 * sb
    row_id = lax.broadcasted_iota(jnp.int32, (pad, DV), 0)

    g_all = g_ref[...] + brow_ref[...]
    ig_all = g_all[:, :HEADS]
    fl_all = _log_sigmoid(g_all[:, HEADS:])
    m_old_all = m_ref[...]
    m_new_all = jnp.maximum(fl_all + m_old_all, ig_all)
    w_in_all = jnp.exp(ig_all - m_new_all)
    w_ca_all = jnp.exp(fl_all + m_old_all - m_new_all)
    floor_all = jnp.exp(-m_new_all)
    mo_ref[...] = m_new_all

    for h in range(HEADS):
        hs = slice(h * DK, (h + 1) * DK)
        qk_rows = jnp.concatenate([qkv_ref[:, hs], qkv_ref[:, width + h * DK:width + (h + 1) * DK] * k_scale],
                                  axis=0).astype(BF16)
        qk_cols = lax.dot_general(eye, qk_rows, NT_DIMS, preferred_element_type=F32).astype(BF16)
        for i in range(sb):
            w_in = w_in_all[i:i + 1, h:h + 1]
            w_ca = w_ca_all[i:i + 1, h:h + 1]
            q = qk_rows[i:i + 1, :]
            qf = q.astype(F32)
            k = qk_rows[sb + i:sb + i + 1, :].astype(F32)
            v = qkv_ref[i:i + 1, 2 * width + h * DK:2 * width + (h + 1) * DK].astype(BF16).astype(F32)
            c_old = c_ref[i, h]
            n_old = n_ref[i, h]
            qc = jnp.dot(jnp.broadcast_to(q, (pad, DK)), c_old.astype(BF16), preferred_element_type=F32)[0:1, :]
            v_rows = jnp.where(row_id == sb + i, jnp.broadcast_to(v * w_in, (pad, DV)), 0.0).astype(BF16)
            kv = jnp.dot(qk_cols, v_rows, preferred_element_type=F32)
            s = jnp.sum(qf * k, axis=-1, keepdims=True) * w_in
            num = w_ca * qc + s * v
            den = w_ca * jnp.sum(qf * n_old, axis=-1, keepdims=True) + s
            hid = num / jnp.maximum(jnp.abs(den), floor_all[i:i + 1, h:h + 1])
            co_ref[i, h] = w_ca * c_old + kv
            no_ref[i, h] = w_ca * n_old + k * w_in
            y_ref[i:i + 1, hs] = _head_norm_gate(hid, o_ref[i:i + 1, hs], gain_ref[:, hs])


def _mlstm_sample(qkv, o_arr, gcol, c0, n0, m0, bias_row, gain):
    nb = qkv.shape[0]
    width = HEADS * DK
    sb = MLSTM_SB
    const2 = lambda i: (0, 0)
    return pl.pallas_call(
        _mlstm_sample_kernel,
        grid=(nb // sb,),
        in_specs=[pl.BlockSpec((sb, 3 * width), lambda i: (i, 0)),
                  pl.BlockSpec((sb, width), lambda i: (i, 0)),
                  pl.BlockSpec((sb, 2 * HEADS), lambda i: (i, 0)),
                  pl.BlockSpec((sb, HEADS, DK, DV), lambda i: (i, 0, 0, 0)),
                  pl.BlockSpec((sb, HEADS, 1, DK), lambda i: (i, 0, 0, 0)),
                  pl.BlockSpec((sb, HEADS), lambda i: (i, 0)),
                  pl.BlockSpec((1, 2 * HEADS), const2),
                  pl.BlockSpec((1, width), const2)],
        out_specs=[pl.BlockSpec((sb, width), lambda i: (i, 0)),
                   pl.BlockSpec((sb, HEADS, DK, DV), lambda i: (i, 0, 0, 0)),
                   pl.BlockSpec((sb, HEADS, 1, DK), lambda i: (i, 0, 0, 0)),
                   pl.BlockSpec((sb, HEADS), lambda i: (i, 0))],
        out_shape=[jax.ShapeDtypeStruct((nb, width), F32),
                   jax.ShapeDtypeStruct((nb, HEADS, DK, DV), F32),
                   jax.ShapeDtypeStruct((nb, HEADS, 1, DK), F32),
                   jax.ShapeDtypeStruct((nb, HEADS), F32)],
        compiler_params=_cparams(1, 48),
        name="mlstm_sample",
    )(qkv, o_arr, gcol, c0, n0, m0, bias_row, gain)


def _merge_rows(ya, yb, x, wgab_ref, wglu_ref, bglu_ref, wa_ref, wb_ref, wout_ref, g1_ref, b1_ref):
    xb = x.astype(BF16)
    g = _gelu_tanh(ya)
    z = jnp.dot(g.astype(BF16), wglu_ref[...], preferred_element_type=F32) + bglu_ref[...]
    out_a = g * _sigmoid(z)
    up_a = jnp.dot(out_a.astype(BF16), wa_ref[...], preferred_element_type=F32)
    g_a = lax.dot_general(xb, wgab_ref[0:D_MODEL, :], NT_DIMS, preferred_element_type=F32)
    mix = _sigmoid(g_a) * up_a
    up_b = jnp.dot(yb, wb_ref[...], preferred_element_type=F32)
    g_b = lax.dot_general(xb, wgab_ref[D_MODEL:2 * D_MODEL, :], NT_DIMS, preferred_element_type=F32)
    mix = mix + _sigmoid(g_b) * up_b
    mo = jnp.dot(mix.astype(BF16), wout_ref[...], preferred_element_type=F32)
    return _layernorm_rows(ALPHA * x + mo, g1_ref[...], b1_ref[...])


def _merge_kernel(ya_ref, yb_ref, x_ref, yas_ref, ybs_ref, xs_ref, *rest):
    weights, (o_ref, os_ref, ob_ref, osb_ref) = rest[:-4], rest[-4:]
    x1 = _merge_rows(ya_ref[...], yb_ref[...], x_ref[...], *weights)
    o_ref[...] = x1
    ob_ref[...] = x1.astype(BF16)

    @pl.when(pl.program_id(0) == pl.num_programs(0) - 1)
    def _():
        x1s = _merge_rows(yas_ref[...], ybs_ref[...].astype(BF16), xs_ref[...], *weights)
        os_ref[...] = x1s
        osb_ref[...] = x1s.astype(BF16)


def _merge(ya, yb, x, ya_s, yb_s, x_s, wt_gab, w_glu, b_glu, w_a_up, w_b_up, w_out, ln_g, ln_b, tm):
    r = x.shape[0]
    rs = x_s.shape[0]
    const2 = lambda i: (0, 0)
    resident = functools.partial(pl.BlockSpec, index_map=const2, pipeline_mode=pl.Buffered(1))
    return pl.pallas_call(
        _merge_kernel,
        grid=(r // tm,),
        in_specs=[pl.BlockSpec((tm, SSM_WIDTH), lambda i: (i, 0)),
                  pl.BlockSpec((tm, HEADS * DV), lambda i: (i, 0)),
                  pl.BlockSpec((tm, D_MODEL), lambda i: (i, 0)),
                  resident((rs, SSM_WIDTH)),
                  resident((rs, HEADS * DV)),
                  resident((rs, D_MODEL)),
                  resident((2 * D_MODEL, D_MODEL)),
                  resident((SSM_WIDTH, SSM_WIDTH)),
                  resident((1, SSM_WIDTH)),
                  resident((SSM_WIDTH, D_MODEL)),
                  resident((HEADS * DV, D_MODEL)),
                  resident((D_MODEL, D_MODEL)),
                  resident((1, D_MODEL)),
                  resident((1, D_MODEL))],
        out_specs=[pl.BlockSpec((tm, D_MODEL), lambda i: (i, 0)),
                   pl.BlockSpec((rs, D_MODEL), const2),
                   pl.BlockSpec((tm, D_MODEL), lambda i: (i, 0)),
                   pl.BlockSpec((rs, D_MODEL), const2)],
        out_shape=[jax.ShapeDtypeStruct((r, D_MODEL), F32), jax.ShapeDtypeStruct((rs, D_MODEL), F32),
                   jax.ShapeDtypeStruct((r, D_MODEL), BF16), jax.ShapeDtypeStruct((rs, D_MODEL), BF16)],
        compiler_params=_cparams(1, 59),
        name="merge_ln1",
    )(ya, yb, x, ya_s, yb_s, x_s, wt_gab, w_glu, b_glu, w_a_up, w_b_up, w_out, ln_g, ln_b)


def _swiglu(xb, wg, wu):
    hg = jnp.dot(xb, wg, preferred_element_type=F32)
    hu = jnp.dot(xb, wu, preferred_element_type=F32)
    return ((hg * _sigmoid(hg)) * hu).astype(BF16)


FFN_UP_PIECE = 512


def _ffn_up_kernel(x_ref, xs_ref, wg_ref, wu_ref, wd_ref, h_ref, hs_ref, wdb_ref, wgb_s, wub_s):
    wgb_s[...] = wg_ref[...].astype(BF16)
    wub_s[...] = wu_ref[...].astype(BF16)

    for r0 in range(0, x_ref.shape[0], FFN_UP_PIECE):
        rows = slice(r0, r0 + FFN_UP_PIECE)
        h_ref[rows, :] = _swiglu(x_ref[rows, :], wgb_s[...], wub_s[...])
    wdb_ref[...] = wd_ref[...].astype(BF16)

    @pl.when(pl.program_id(0) == pl.num_programs(0) - 1)
    def _():
        hs_ref[...] = _swiglu(xs_ref[...], wgb_s[...], wub_s[...])


def _ffn_up(xb, xsb, w_gate3, w_up3, w_down3, tm, tf):
    r = xb.shape[0]
    rs = xsb.shape[0]
    n_f, n_i = D_FF // tf, r // tm
    wd_rows = D_FF // (n_f * n_i)
    hs_block = lambda i, f: jnp.where(i == n_i - 1, f, 0)
    return pl.pallas_call(
        _ffn_up_kernel,
        grid=(n_i, n_f),
        in_specs=[pl.BlockSpec((tm, D_MODEL), lambda i, f: (i, 0)),
                  pl.BlockSpec((rs, D_MODEL), lambda i, f: (0, 0)),
                  pl.BlockSpec((None, D_MODEL, tf), lambda i, f: (0, 0, f)),
                  pl.BlockSpec((None, D_MODEL, tf), lambda i, f: (0, 0, f)),
                  pl.BlockSpec((None, wd_rows, D_MODEL), lambda i, f: (0, i * n_f + f, 0))],
        out_specs=[pl.BlockSpec((tm, tf), lambda i, f: (i, f)),
                   pl.BlockSpec((rs, tf), lambda i, f: (0, hs_block(i, f))),
                   pl.BlockSpec((wd_rows, D_MODEL), lambda i, f: (i * n_f + f, 0))],
        out_shape=[jax.ShapeDtypeStruct((r, D_FF), BF16), jax.ShapeDtypeStruct((rs, D_FF), BF16),
                   jax.ShapeDtypeStruct((D_FF, D_MODEL), BF16)],
        scratch_shapes=[pltpu.VMEM((D_MODEL, tf), BF16), pltpu.VMEM((D_MODEL, tf), BF16)],
        compiler_params=_cparams(2, 58),
        name="ffn_up",
    )(xb, xsb, w_gate3, w_up3, w_down3)


FFN_DOWN_PIECE = 256


def _ffn_down_rows(h, x, wd_ref, g2_ref, b2_ref):
    return _layernorm_rows(ALPHA * x + jnp.dot(h, wd_ref[...], preferred_element_type=F32), g2_ref[...], b2_ref[...])


def _ffn_down_kernel(h_ref, x_ref, hs_ref, xs_ref, wd_ref, g2_ref, b2_ref, o_ref, os_ref):
    for r0 in range(0, h_ref.shape[0], FFN_DOWN_PIECE):
        rows = slice(r0, r0 + FFN_DOWN_PIECE)
        o_ref[rows, :] = _ffn_down_rows(h_ref[rows, :], x_ref[rows, :], wd_ref, g2_ref, b2_ref)

    @pl.when(pl.program_id(0) == pl.num_programs(0) - 1)
    def _():
        os_ref[...] = _ffn_down_rows(hs_ref[...], xs_ref[...], wd_ref, g2_ref, b2_ref)


def _ffn_down(h, x, h_s, x_s, wd_b, ln_g, ln_b, tm):
    r = x.shape[0]
    rs = x_s.shape[0]
    const2 = lambda i: (0, 0)
    resident = functools.partial(pl.BlockSpec, index_map=const2, pipeline_mode=pl.Buffered(1))
    return pl.pallas_call(
        _ffn_down_kernel,
        grid=(r // tm,),
        in_specs=[pl.BlockSpec((tm, D_FF), lambda i: (i, 0)),
                  pl.BlockSpec((tm, D_MODEL), lambda i: (i, 0)),
                  resident((rs, D_FF)),
                  resident((rs, D_MODEL)),
                  resident((D_FF, D_MODEL)),
                  resident((1, D_MODEL)),
                  resident((1, D_MODEL))],
        out_specs=[pl.BlockSpec((tm, D_MODEL), lambda i: (i, 0)),
                   pl.BlockSpec((rs, D_MODEL), const2)],
        out_shape=[jax.ShapeDtypeStruct((r, D_MODEL), F32), jax.ShapeDtypeStruct((rs, D_MODEL), F32)],
        compiler_params=_cparams(1, 59),
        name="ffn_down_ln2",
    )(h, x, h_s, x_s, wd_b, ln_g, ln_b)


def kernel(x_prompt, x_sample, state_ssm_re, state_ssm_im, state_mlstm_c, state_mlstm_n, state_mlstm_m,
           meta_tokens, w_in, b_if, ssm_a_re, ssm_a_im, ssm_log_dt, ssm_b_re, ssm_b_im, ssm_c_re, ssm_c_im,
           ssm_d, w_glu, b_glu, w_a_up, mh_gain, w_b_up, w_out, ln1_g, ln1_b, w_gate, w_up, w_down,
           ln2_g, ln2_b):
    batch, seq, _ = x_prompt.shape
    nb = x_sample.shape[0]
    width = HEADS * DK

    bias_row = b_if[0].reshape(1, 2 * HEADS)
    bias_col = b_if[0].reshape(2 * HEADS, 1)
    gain = mh_gain[0].reshape(1, width)
    d_skip = ssm_d[0].reshape(1, SSM_WIDTH)
    b_glu_r = b_glu[0].reshape(1, SSM_WIDTH)
    ln1 = (ln1_g[0].reshape(1, D_MODEL), ln1_b[0].reshape(1, D_MODEL))
    ln2 = (ln2_g[0].reshape(1, D_MODEL), ln2_b[0].reshape(1, D_MODEL))
    abar_re, abar_im, bt, ct = _s5_discretise(
        ssm_a_re[0], ssm_a_im[0], ssm_log_dt[0], ssm_b_re[0], ssm_b_im[0], ssm_c_re[0], ssm_c_im[0])

    xp = x_prompt.reshape(batch * seq, D_MODEL)
    xs = x_sample.reshape(nb, D_MODEL)
    xsm = jnp.concatenate([xs, meta_tokens], axis=0)
    w_in_t = jnp.swapaxes(w_in, 1, 2)
    uo_blocks, qkv_blocks = (0, 4, 2), (1, 1, 3)

    uo_p, uo_s, gcol_p, grow_p, gcol_s = _proj(xp, xsm, w_in_t, uo_blocks, F32, F32, 1024, "proj_uo", with_gates=True)
    qkv_p, qkv_s = _proj(xp, xsm, w_in_t, qkv_blocks, BF16, F32, 1024, "proj_qkv")

    u_meta = uo_s[nb:nb + N_META, :SSM_WIDTH]
    qkv_meta = qkv_s[nb:nb + N_META].astype(BF16)
    gcol_meta = gcol_s[nb:nb + N_META]

    ya_p, pf_re, pf_im = _s5_prompt(uo_p, 0, u_meta, abar_re, abar_im, bt, ct, d_skip, batch, seq)
    yb_p, pc, pn, pm, wt_gab, w_out_b, w_a_b, w_b_b, w_glu_b = _mlstm_prompt(
        qkv_p, uo_p, 1, gcol_p, grow_p, qkv_meta, gcol_meta, bias_row, bias_col, gain, batch, seq,
        w_in_t, (w_out, w_a_up, w_b_up, w_glu))
    ya_s, sf_re, sf_im = _s5_sample(uo_s[:nb, :SSM_WIDTH], state_ssm_re[0].reshape(nb, SSM_FLAT),
                                    state_ssm_im[0].reshape(nb, SSM_FLAT), abar_re, abar_im, bt, ct, d_skip)
    yb_s, sc, sn, sm = _mlstm_sample(qkv_s[:nb], uo_s[:nb, SSM_WIDTH:], gcol_s[:nb], state_mlstm_c[0],
                                     state_mlstm_n[0].reshape(nb, HEADS, 1, DK), state_mlstm_m[0], bias_row, gain)

    x1_p, x1_s, x1b_p, x1b_s = _merge(ya_p, yb_p.reshape(batch * seq, width), xp, ya_s, yb_s, xs,
                                      wt_gab, w_glu_b, b_glu_r, w_a_b, w_b_b, w_out_b, *ln1, tm=256)
    h_p, h_s, w_down_b = _ffn_up(x1b_p, x1b_s, w_gate, w_up, w_down, tm=2048, tf=512)
    y_p, y_s = _ffn_down(h_p, x1_p, h_s, x1_s, w_down_b, *ln2, tm=512)

    return (y_p.reshape(batch, seq, D_MODEL),
            y_s.reshape(nb, 1, D_MODEL),
            pf_re.reshape(1, batch, SSM_GROUPS, SSM_STATE),
            pf_im.reshape(1, batch, SSM_GROUPS, SSM_STATE),
            pc.reshape(1, batch, HEADS, DK, DV),
            pn.reshape(1, batch, HEADS, DK),
            pm[:, :, 0].reshape(1, batch, HEADS),
            sf_re.reshape(1, nb, SSM_GROUPS, SSM_STATE),
            sf_im.reshape(1, nb, SSM_GROUPS, SSM_STATE),
            sc.reshape(1, nb, HEADS, DK, DV),
            sn.reshape(1, nb, HEADS, DK),
            sm.reshape(1, nb, HEADS))
```

```python
import functools

import jax
import jax.numpy as jnp
import numpy as np
from jax import lax
from jax.experimental import pallas as pl
from jax.experimental.pallas import tpu as pltpu

F32 = jnp.float32
BF16 = jnp.bfloat16

D_MODEL = 2048
N_META = 16
SSM_WIDTH = 1024
SSM_GROUP = 16
SSM_GROUPS = 64
SSM_STATE = 64
SSM_FLAT = SSM_GROUPS * SSM_STATE
SSM_JBLK = 4
SSM_BLK_IN = SSM_WIDTH // SSM_JBLK
SSM_BLK_STATE = SSM_FLAT // SSM_JBLK
LANE_TILE = 128
HEADS = 4
DK = 256
DV = 256
CHUNK = 128
D_FF = 5632
EPS = 1e-5
ALPHA = 2.0 ** 0.25

S5_TC = 256
S5_NSEG = 8
S5_SEG = S5_TC // S5_NSEG
S5_LW = 512
S5_CPS = 2

V7X_SCOPED_VMEM_MAX_BYTES = 60000 * 1024


def _cparams(n_axes, vmem_mb):
    return pltpu.CompilerParams(
        dimension_semantics=("arbitrary",) * n_axes,
        vmem_limit_bytes=min(vmem_mb * 1024 * 1024, V7X_SCOPED_VMEM_MAX_BYTES),
    )


def _sigmoid(x):
    return 1.0 / (1.0 + jnp.exp(-x))


def _log_sigmoid(x):
    return jnp.minimum(x, 0.0) - jnp.log(1.0 + jnp.exp(-jnp.abs(x)))


def _gelu_tanh(x):
    c = 0.7978845608028654
    return 0.5 * x * (1.0 + jnp.tanh(c * (x + 0.044715 * (x * x * x))))


def _layernorm_rows(x, g, b):
    mu = jnp.mean(x, axis=-1, keepdims=True)
    xc = x - mu
    var = jnp.mean(xc * xc, axis=-1, keepdims=True)
    return xc * lax.rsqrt(var + EPS) * g + b


PROJ_TN = 1024
GATE_ROWS = 2 * HEADS
TAIL_ROW0 = SSM_WIDTH + 4 * HEADS * DK
NT_DIMS = (((1,), (1,)), ((), ()))


PROJ_BLOCKS = 5
PROJ_O_BLOCK = 4


def _proj_all_kernel(x_ref, xs_ref, wt_ref, wg_ref, uo_ref, qkv_ref, uos_ref, qkvs_ref, g_ref, gt_ref, gs_ref, wb_s):
    j = pl.program_id(0)
    i = pl.program_id(1)
    is_uo = jnp.logical_or(j == 0, j == PROJ_O_BLOCK)
    is_qkv = jnp.logical_not(is_uo)
    last_row_block = i == pl.num_programs(1) - 1

    @pl.when(i == 0)
    def _():
        wb_s[...] = wt_ref[...].astype(BF16)

    xb = x_ref[...].astype(BF16)

    @pl.when(is_uo)
    def _():
        uo_ref[...] = lax.dot_general(xb, wb_s[...], NT_DIMS, preferred_element_type=F32)

    @pl.when(is_qkv)
    def _():
        qkv_ref[...] = lax.dot_general(xb, wb_s[...], NT_DIMS, preferred_element_type=F32).astype(BF16)

    @pl.when(j == 0)
    def _():
        wg = wg_ref[...].astype(BF16)
        g_ref[...] = lax.dot_general(xb, wg, NT_DIMS, preferred_element_type=F32)
        gt_ref[...] = lax.dot_general(wg, xb, NT_DIMS, preferred_element_type=F32)

    @pl.when(jnp.logical_and(last_row_block, is_uo))
    def _():
        xsb = xs_ref[...].astype(BF16)
        uos_ref[...] = lax.dot_general(xsb, wb_s[...], NT_DIMS, preferred_element_type=F32)

    @pl.when(jnp.logical_and(last_row_block, is_qkv))
    def _():
        xsb = xs_ref[...].astype(BF16)
        qkvs_ref[...] = lax.dot_general(xsb, wb_s[...], NT_DIMS, preferred_element_type=F32)

    @pl.when(jnp.logical_and(last_row_block, j == 0))
    def _():
        gs_ref[...] = lax.dot_general(xs_ref[...].astype(BF16), wg_ref[...].astype(BF16), NT_DIMS,
                                      preferred_element_type=F32)


def _proj_all(x, xs, wt3, tm):
    r, k = x.shape
    rs = xs.shape[0]
    tn = PROJ_TN
    ng = GATE_ROWS
    nm = r // tm
    last = nm - 1
    o_blk = PROJ_O_BLOCK
    uo_row = lambda j, i: jnp.where(jnp.logical_or(j == 0, j == o_blk), i, last)
    uo_col = lambda j: jnp.where(j == o_blk, 1, 0)
    qkv_row = lambda j, i: jnp.where(j == 0, 0, jnp.where(j == o_blk, last, i))
    qkv_col = lambda j: jnp.clip(j - 1, 0, 2)
    gate_row = lambda j, i: jnp.where(j == 0, i, last)
    return pl.pallas_call(
        _proj_all_kernel,
        grid=(PROJ_BLOCKS, nm),
        in_specs=[pl.BlockSpec((tm, k), lambda j, i: (i, 0)),
                  pl.BlockSpec((rs, k), lambda j, i: (0, 0)),
                  pl.BlockSpec((None, tn, k), lambda j, i: (0, j, 0)),
                  pl.BlockSpec((None, ng, k), lambda j, i: (0, TAIL_ROW0 // ng, 0))],
        out_specs=[pl.BlockSpec((tm, tn), lambda j, i: (uo_row(j, i), uo_col(j))),
                   pl.BlockSpec((tm, tn), lambda j, i: (qkv_row(j, i), qkv_col(j))),
                   pl.BlockSpec((rs, tn), lambda j, i: (0, uo_col(j))),
                   pl.BlockSpec((rs, tn), lambda j, i: (0, qkv_col(j))),
                   pl.BlockSpec((tm, ng), lambda j, i: (gate_row(j, i), 0)),
                   pl.BlockSpec((ng, tm), lambda j, i: (0, gate_row(j, i))),
                   pl.BlockSpec((rs, ng), lambda j, i: (0, 0))],
        out_shape=[jax.ShapeDtypeStruct((r, 2 * tn), F32), jax.ShapeDtypeStruct((r, 3 * tn), BF16),
                   jax.ShapeDtypeStruct((rs, 2 * tn), F32), jax.ShapeDtypeStruct((rs, 3 * tn), F32),
                   jax.ShapeDtypeStruct((r, ng), F32), jax.ShapeDtypeStruct((ng, r), F32),
                   jax.ShapeDtypeStruct((rs, ng), F32)],
        scratch_shapes=[pltpu.VMEM((tn, k), BF16)],
        compiler_params=_cparams(2, 59),
        name="proj_all",
    )(x, xs, wt3, wt3)


def _s5_discretise(a_re, a_im, log_dt, b_re, b_im, c_re, c_im):
    dt = jnp.exp(log_dt)
    e = jnp.exp(a_re * dt)
    abar_re = e * jnp.cos(a_im * dt)
    abar_im = e * jnp.sin(a_im * dt)
    nr = abar_re - 1.0
    ni = abar_im
    den = a_re * a_re + a_im * a_im
    coef_re = (nr * a_re + ni * a_im) / den
    coef_im = (ni * a_re - nr * a_im) / den
    bb_re = coef_re[..., None] * b_re - coef_im[..., None] * b_im
    bb_im = coef_re[..., None] * b_im + coef_im[..., None] * b_re
    gpb = SSM_GROUPS // SSM_JBLK

    def b_rows(bb):
        t = bb.reshape(SSM_JBLK, gpb, SSM_STATE, SSM_GROUP).transpose(0, 1, 3, 2)
        return t.reshape(SSM_JBLK, SSM_BLK_IN, SSM_STATE)

    def c_rows(cc):
        t = cc.reshape(SSM_JBLK, gpb, SSM_GROUP, SSM_STATE).transpose(0, 1, 3, 2)
        return t.reshape(SSM_JBLK, SSM_BLK_STATE, SSM_GROUP)

    b_rep, c_rep = LANE_TILE // SSM_STATE, LANE_TILE // SSM_GROUP
    bt = jnp.concatenate([jnp.tile(b_rows(bb_re), (1, 1, b_rep)), jnp.tile(b_rows(bb_im), (1, 1, b_rep))], axis=-1)
    ct = jnp.concatenate([jnp.tile(c_rows(c_re), (1, 1, c_rep)), jnp.tile(c_rows(-c_im), (1, 1, c_rep))], axis=-1)
    return abar_re.reshape(1, SSM_FLAT), abar_im.reshape(1, SSM_FLAT), bt.astype(BF16), ct.astype(BF16)


def _s5_expand_weights(bt_ref, ct_ref, bw_s, cwr_s, cwi_s):
    log2 = lambda n: n.bit_length() - 1
    rb = lax.broadcasted_iota(jnp.int32, (SSM_BLK_IN, 2 * SSM_BLK_STATE), 0)
    cb = lax.broadcasted_iota(jnp.int32, (SSM_BLK_IN, 2 * SSM_BLK_STATE), 1)
    bmask = (rb >> log2(SSM_GROUP)) == ((cb & (SSM_BLK_STATE - 1)) >> log2(SSM_STATE))
    rc = lax.broadcasted_iota(jnp.int32, (SSM_BLK_STATE, SSM_BLK_IN), 0)
    cc = lax.broadcasted_iota(jnp.int32, (SSM_BLK_STATE, SSM_BLK_IN), 1)
    cmask = (rc >> log2(SSM_STATE)) == (cc >> log2(SSM_GROUP))
    b_tiles, c_tiles = SSM_BLK_STATE // LANE_TILE, SSM_BLK_IN // LANE_TILE
    for j in range(SSM_JBLK):
        bt = bt_ref[j].astype(F32)
        wide = jnp.concatenate([bt[:, :LANE_TILE]] * b_tiles + [bt[:, LANE_TILE:]] * b_tiles, axis=1)
        bw_s[j] = jnp.where(bmask, wide, 0.0).astype(BF16)
        ct = ct_ref[j].astype(F32)
        cwr_s[j] = jnp.where(cmask, jnp.concatenate([ct[:, :LANE_TILE]] * c_tiles, axis=1), 0.0).astype(BF16)
        cwi_s[j] = jnp.where(cmask, jnp.concatenate([ct[:, LANE_TILE:]] * c_tiles, axis=1), 0.0).astype(BF16)


def _cmul_add(ar, ai, hr, hi, xr, xi):
    return ar * hr - ai * hi + xr, ar * hi + ai * hr + xi


def _s5_project_in(u, b_ref, hre_ref, him_ref, rows):
    for j in range(SSM_JBLK):
        state = slice(j * SSM_BLK_STATE, (j + 1) * SSM_BLK_STATE)
        bu = jnp.dot(u[:, j * SSM_BLK_IN:(j + 1) * SSM_BLK_IN], b_ref[j], preferred_element_type=F32)
        hre_ref[0:rows, state] = bu[:, :SSM_BLK_STATE]
        him_ref[0:rows, state] = bu[:, SSM_BLK_STATE:]


def _s5_project_out(hre_ref, him_ref, cre_ref, cimn_ref, rows):
    ys = []
    for j in range(SSM_JBLK):
        state = slice(j * SSM_BLK_STATE, (j + 1) * SSM_BLK_STATE)
        hr = hre_ref[0:rows, state].astype(BF16)
        hi = him_ref[0:rows, state].astype(BF16)
        y = jnp.dot(hr, cre_ref[j], preferred_element_type=F32)
        ys.append(y + jnp.dot(hi, cimn_ref[j], preferred_element_type=F32))
    return jnp.concatenate(ys, axis=1)


def _s5_scan_chunk(hre_ref, him_ref, hsre_ref, hsim_ref, pre_ref, pim_ref, are_ref, aim_ref, cre_s, cim_s):
    full = (S5_NSEG, S5_LW)
    for lg in range(SSM_FLAT // S5_LW):
        lanes = slice(lg * S5_LW, (lg + 1) * S5_LW)
        ar = jnp.broadcast_to(are_ref[:, lanes], full)
        ai = jnp.broadcast_to(aim_ref[:, lanes], full)

        hr = jnp.zeros(full, F32)
        hi = jnp.zeros(full, F32)
        for k in range(S5_SEG):
            rows = slice(k * S5_NSEG, (k + 1) * S5_NSEG)
            hr, hi = _cmul_add(ar, ai, hr, hi, hre_ref[rows, lanes], him_ref[rows, lanes])
            hre_ref[rows, lanes] = hr
            him_ref[rows, lanes] = hi
        er, ei = hr, hi

        asr = pre_ref[S5_TC - 1:S5_TC, lanes]
        asi = pim_ref[S5_TC - 1:S5_TC, lanes]
        hr = cre_s[:, lanes]
        hi = cim_s[:, lanes]
        for j in range(S5_NSEG):
            hsre_ref[j:j + 1, lanes] = hr
            hsim_ref[j:j + 1, lanes] = hi
            hr, hi = _cmul_add(asr, asi, hr, hi, er[j:j + 1, :], ei[j:j + 1, :])
        cre_s[:, lanes] = hr
        cim_s[:, lanes] = hi

        sr = hsre_ref[:, lanes]
        si = hsim_ref[:, lanes]
        for k in range(S5_SEG):
            rows = slice(k * S5_NSEG, (k + 1) * S5_NSEG)
            pr = pre_ref[rows, lanes]
            pi = pim_ref[rows, lanes]
            hre_ref[rows, lanes] = hre_ref[rows, lanes] + (pr * sr - pi * si)
            him_ref[rows, lanes] = him_ref[rows, lanes] + (pr * si + pi * sr)


N_CAST_IN = 6


def _weight_cast_specs(n_steps, step_of, w_in_t, others):
    gab_rows = 2 * D_MODEL // n_steps
    gab_blk0 = TAIL_ROW0 // gab_rows
    gab_next0 = (TAIL_ROW0 + gab_rows) // GATE_ROWS
    rows = [w.shape[1] // n_steps for w in others]
    in_specs = [pl.BlockSpec((None, gab_rows, D_MODEL), lambda *g: (0, gab_blk0 + step_of(*g), 0)),
                pl.BlockSpec((None, GATE_ROWS, D_MODEL),
                             lambda *g: (0, gab_next0 + (gab_rows // GATE_ROWS) * step_of(*g), 0))]
    in_specs += [pl.BlockSpec((None, nr, w.shape[2]), lambda *g: (0, step_of(*g), 0)) for nr, w in zip(rows, others)]
    out_specs = [pl.BlockSpec((gab_rows, D_MODEL), lambda *g: (step_of(*g), 0))]
    out_specs += [pl.BlockSpec((nr, w.shape[2]), lambda *g: (step_of(*g), 0)) for nr, w in zip(rows, others)]
    out_shapes = [jax.ShapeDtypeStruct((2 * D_MODEL, D_MODEL), BF16)]
    out_shapes += [jax.ShapeDtypeStruct(w.shape[1:], BF16) for w in others]
    return in_specs, out_specs, out_shapes, (w_in_t, w_in_t) + tuple(others)


def _weight_cast_step(in_refs, out_refs):
    wga_ref, wgn_ref = in_refs[:2]
    out_refs[0][...] = jnp.concatenate([wga_ref[GATE_ROWS:, :], wgn_ref[...]], axis=0).astype(BF16)
    for src, dst in zip(in_refs[2:], out_refs[1:]):
        dst[...] = src[...].astype(BF16)


def _s5_prompt_kernel(u_ref, um_ref, perm_ref, permt_ref, bt_ref, ct_ref, are_ref, aim_ref, d_ref,
                      y_ref, fre_ref, fim_ref,
                      hre_ref, him_ref, hre2_ref, him2_ref, pre_ref, pim_ref, cre_s, cim_s,
                      hsre_ref, hsim_ref, hsre2_ref, hsim2_ref,
                      b_ref, cre_ref, cimn_ref):
    b_id = pl.program_id(0)
    c_id = pl.program_id(1)
    n_c = pl.num_programs(1)

    @pl.when(jnp.logical_and(b_id == 0, c_id == 0))
    def _():
        _s5_expand_weights(bt_ref, ct_ref, b_ref, cre_ref, cimn_ref)
        ar = jnp.broadcast_to(are_ref[...], (S5_NSEG, SSM_FLAT))
        ai = jnp.broadcast_to(aim_ref[...], (S5_NSEG, SSM_FLAT))
        pre_ref[0:S5_NSEG, :] = ar
        pim_ref[0:S5_NSEG, :] = ai

        def body(i, carry):
            pr, pi = carry
            nr = ar * pr - ai * pi
            ni = ar * pi + ai * pr
            rows = pl.ds(pl.multiple_of(i * S5_NSEG, S5_NSEG), S5_NSEG)
            pre_ref[rows, :] = nr
            pim_ref[rows, :] = ni
            return nr, ni

        lax.fori_loop(1, S5_SEG, body, (ar, ai))

    @pl.when(c_id == 0)
    def _():
        _s5_project_in(um_ref[...].astype(BF16), b_ref, hre_ref, him_ref, N_META)
        ar = are_ref[...]
        ai = aim_ref[...]
        hr = jnp.zeros((1, SSM_FLAT), F32)
        hi = jnp.zeros((1, SSM_FLAT), F32)
        for t in range(N_META):
            hr, hi = _cmul_add(ar, ai, hr, hi, hre_ref[t:t + 1, :], him_ref[t:t + 1, :])
        cre_s[...] = hr
        cim_s[...] = hi

    bufs = ((hre_ref, him_ref, hsre_ref, hsim_ref), (hre2_ref, him2_ref, hsre2_ref, hsim2_ref))
    us = [u_ref[n * S5_TC:(n + 1) * S5_TC, :] for n in range(S5_CPS)]
    for n in range(S5_CPS):
        u_perm = jnp.dot(perm_ref[...], us[n].astype(BF16), preferred_element_type=F32).astype(BF16)
        _s5_project_in(u_perm, b_ref, bufs[n][0], bufs[n][1], S5_TC)
    for n in range(S5_CPS):
        hre_n, him_n, hsre_n, hsim_n = bufs[n]
        _s5_scan_chunk(hre_n, him_n, hsre_n, hsim_n, pre_ref, pim_ref, are_ref, aim_ref, cre_s, cim_s)
        y_perm = _s5_project_out(hre_n, him_n, cre_ref, cimn_ref, S5_TC)
        y = sum(jnp.dot(permt_ref[...], piece, preferred_element_type=F32) for piece in _split3(y_perm))
        y_ref[n * S5_TC:(n + 1) * S5_TC, :] = y + d_ref[...] * us[n]

    @pl.when(c_id == n_c - 1)
    def _():
        fre_ref[0] = cre_s[...]
        fim_ref[0] = cim_s[...]


def _s5_weight_scratch():
    return [pltpu.VMEM((SSM_JBLK, SSM_BLK_IN, 2 * SSM_BLK_STATE), BF16),
            pltpu.VMEM((SSM_JBLK, SSM_BLK_STATE, SSM_BLK_IN), BF16),
            pltpu.VMEM((SSM_JBLK, SSM_BLK_STATE, SSM_BLK_IN), BF16)]


def _s5_prompt(u_arr, u_col_block, u_meta, abar_re, abar_im, bt, ct, d_skip, batch, seq):
    rows_step = S5_CPS * S5_TC
    n_c = seq // rows_step
    const3 = lambda b, c: (0, 0, 0)
    const2 = lambda b, c: (0, 0)
    rows = np.arange(S5_TC)
    perm_np = np.zeros((S5_TC, S5_TC), np.float32)
    perm_np[rows, (rows % S5_NSEG) * S5_SEG + rows // S5_NSEG] = 1.0
    perm = jnp.asarray(perm_np, dtype=BF16)
    permt = jnp.asarray(perm_np.T, dtype=BF16)
    return pl.pallas_call(
        _s5_prompt_kernel,
        grid=(batch, n_c),
        in_specs=[pl.BlockSpec((rows_step, SSM_WIDTH), lambda b, c: (b * n_c + c, u_col_block)),
                  pl.BlockSpec((N_META, SSM_WIDTH), const2),
                  pl.BlockSpec((S5_TC, S5_TC), const2),
                  pl.BlockSpec((S5_TC, S5_TC), const2),
                  pl.BlockSpec((SSM_JBLK, SSM_BLK_IN, 2 * LANE_TILE), const3, pipeline_mode=pl.Buffered(1)),
                  pl.BlockSpec((SSM_JBLK, SSM_BLK_STATE, 2 * LANE_TILE), const3, pipeline_mode=pl.Buffered(1)),
                  pl.BlockSpec((1, SSM_FLAT), const2),
                  pl.BlockSpec((1, SSM_FLAT), const2),
                  pl.BlockSpec((1, SSM_WIDTH), const2)],
        out_specs=[pl.BlockSpec((rows_step, SSM_WIDTH), lambda b, c: (b * n_c + c, 0)),
                   pl.BlockSpec((1, 1, SSM_FLAT), lambda b, c: (b, 0, 0)),
                   pl.BlockSpec((1, 1, SSM_FLAT), lambda b, c: (b, 0, 0))],
        out_shape=[jax.ShapeDtypeStruct((batch * seq, SSM_WIDTH), F32),
                   jax.ShapeDtypeStruct((batch, 1, SSM_FLAT), F32),
                   jax.ShapeDtypeStruct((batch, 1, SSM_FLAT), F32)],
        scratch_shapes=[pltpu.VMEM((S5_TC, SSM_FLAT), F32)] * 4
                       + [pltpu.VMEM((S5_TC, SSM_FLAT), F32)] * 2
                       + [pltpu.VMEM((1, SSM_FLAT), F32)] * 2
                       + [pltpu.VMEM((S5_NSEG, SSM_FLAT), F32)] * 4
                       + _s5_weight_scratch(),
        compiler_params=_cparams(2, 59),
        name="s5_prompt",
    )(u_arr, u_meta, perm, permt, bt, ct, abar_re, abar_im, d_skip)


def _s5_sample_kernel(u_ref, h0re_ref, h0im_ref, bt_ref, ct_ref, are_ref, aim_ref, d_ref,
                      y_ref, hre_ref, him_ref, b_ref, cre_ref, cimn_ref):
    rows = u_ref.shape[0]
    _s5_expand_weights(bt_ref, ct_ref, b_ref, cre_ref, cimn_ref)
    u = u_ref[...]
    _s5_project_in(u.astype(BF16), b_ref, hre_ref, him_ref, rows)
    nr, ni = _cmul_add(are_ref[...], aim_ref[...], h0re_ref[...], h0im_ref[...], hre_ref[...], him_ref[...])
    hre_ref[...] = nr
    him_ref[...] = ni
    y_ref[...] = _s5_project_out(hre_ref, him_ref, cre_ref, cimn_ref, rows) + d_ref[...] * u


def _s5_sample(u, h0_re, h0_im, abar_re, abar_im, bt, ct, d_skip):
    rows = u.shape[0]
    return pl.pallas_call(
        _s5_sample_kernel,
        out_shape=[jax.ShapeDtypeStruct((rows, SSM_WIDTH), F32),
                   jax.ShapeDtypeStruct((rows, SSM_FLAT), F32),
                   jax.ShapeDtypeStruct((rows, SSM_FLAT), F32)],
        scratch_shapes=_s5_weight_scratch(),
        compiler_params=_cparams(0, 48),
        name="s5_sample",
    )(u, h0_re, h0_im, bt, ct, abar_re, abar_im, d_skip)


def _split3(x):
    hi = x.astype(BF16)
    r1 = x - hi.astype(F32)
    mid = r1.astype(BF16)
    lo = (r1 - mid.astype(F32)).astype(BF16)
    return hi, mid, lo


def _tri(n, lower):
    r = lax.broadcasted_iota(jnp.int32, (n, n), 0)
    c = lax.broadcasted_iota(jnp.int32, (n, n), 1)
    return (r >= c) if lower else (r <= c)


def _gate_columns(gcol, bias_row):
    n = gcol.shape[0]
    g = gcol + bias_row
    fl = _log_sigmoid(g[:, HEADS:])
    tril = jnp.where(_tri(n, True), 1.0, 0.0).astype(BF16)
    b = sum(jnp.dot(tril, p, preferred_element_type=F32) for p in _split3(fl))
    return g[:, :HEADS], b


def _gate_rows(grow, bias_col):
    n = grow.shape[1]
    g = grow + bias_col
    fl = _log_sigmoid(g[HEADS:, :])
    triu = jnp.where(_tri(n, False), 1.0, 0.0).astype(BF16)
    b = sum(jnp.dot(p, triu, preferred_element_type=F32) for p in _split3(fl))
    return g[:HEADS, :], b


MLSTM_GROUP = 4


def _eye_bf16(n):
    return jnp.where(_tri(n, True) & _tri(n, False), 1.0, 0.0).astype(BF16)


def _mlstm_state_update(k_bf, v, ig_col, b_col, c_old, n_old, m_old, k_t=None, gate_rows=None):
    n_rows = k_bf.shape[0]
    k_scale = DK ** -0.5
    b_last = b_col[n_rows - 1:n_rows, :]
    e_col = b_last - b_col + ig_col
    m_new = jnp.maximum(b_last + m_old, jnp.max(e_col, axis=0, keepdims=True))
    w_end = jnp.exp(e_col - m_new) * k_scale
    carry = jnp.exp(b_last + m_old - m_new)
    if k_t is None:
        k_t = lax.dot_general(_eye_bf16(DK), k_bf, NT_DIMS, preferred_element_type=F32)
    if gate_rows is None:
        kv = jnp.dot(k_t.astype(BF16), (v.astype(F32) * w_end).astype(BF16), preferred_element_type=F32)
    else:
        ig_row, b_row = gate_rows
        w_end_row = jnp.exp(b_last - b_row + ig_row - m_new) * k_scale
        kv = jnp.dot((k_t * w_end_row).astype(BF16), v, preferred_element_type=F32)
    c_new = carry * c_old + kv
    n_new = carry * n_old + jnp.sum(k_bf.astype(F32) * w_end, axis=0, keepdims=True)
    return c_new, n_new, m_new


def _mlstm_decay(ig_row, b_row, b_col, m_old):
    n_rows = b_col.shape[0]
    dmat = jnp.where(_tri(n_rows, True), b_col + (ig_row - b_row), -jnp.inf)
    inter = b_col + m_old
    m_t = jnp.maximum(inter, jnp.max(dmat, axis=-1, keepdims=True))
    return jnp.exp(dmat - m_t), jnp.exp(inter - m_t), m_t


def _head_norm_gate(h, o, gain):
    mu = jnp.mean(h, axis=-1, keepdims=True)
    hc = h - mu
    var = jnp.mean(hc * hc, axis=-1, keepdims=True)
    return _sigmoid(o) * (hc * lax.rsqrt(var + EPS) * gain)


def _mlstm_prompt_kernel(batch, q_ref, k_ref, v_ref, o_ref, gc_ref, *rest):
    gr_refs = rest[:batch]
    rest = rest[batch:]
    km_ref, vm_ref, gcm_ref, brow_ref, bcol_ref, gain_ref = rest[:6]
    cast_in = rest[6:6 + N_CAST_IN]
    y_ref, cf_ref, nf_ref, mf_ref = rest[6 + N_CAST_IN:10 + N_CAST_IN]
    cast_out = rest[10 + N_CAST_IN:]
    k_scale = DK ** -0.5

    _weight_cast_step(cast_in, cast_out)

    @pl.when(pl.program_id(0) == 0)
    def _():
        ig_c, b_c = _gate_columns(gcm_ref[...], brow_ref[...])
        for h in range(HEADS):
            hs = slice(h * DK, (h + 1) * DK)
            c_new, n_new, m_new = _mlstm_state_update(
                km_ref[:, hs], vm_ref[:, hs], ig_c[:, h:h + 1], b_c[:, h:h + 1],
                jnp.zeros((DK, DV), F32), jnp.zeros((1, DK), F32), jnp.zeros((1, 1), F32))
            for b in range(batch):
                cf_ref[b, h] = c_new
                nf_ref[b, h] = n_new
                mf_ref[b, h:h + 1, :] = jnp.broadcast_to(m_new, (1, LANE_TILE))

    eye = _eye_bf16(DK)
    for b0 in range(0, batch, MLSTM_GROUP):
        gates = {}
        for b in range(b0, b0 + MLSTM_GROUP):
            gates[b] = _gate_columns(gc_ref[b], brow_ref[...]) + _gate_rows(gr_refs[b][...], bcol_ref[...])
        chains = [(b, h) for b in range(b0, b0 + MLSTM_GROUP) for h in range(HEADS)]

        ops = []
        for b, h in chains:
            hs = slice(h * DK, (h + 1) * DK)
            q = q_ref[b, :, hs]
            k_bf = k_ref[b, :, hs]
            c_old = cf_ref[b, h]
            ops.append(dict(
                q=q, k_bf=k_bf, v=v_ref[b, :, hs], c_old=c_old, n_old=nf_ref[b, h], m_old=mf_ref[b, h:h + 1, 0:1],
                s_raw=lax.dot_general(q, k_bf, NT_DIMS, preferred_element_type=F32),
                qc=jnp.dot(q, c_old.astype(BF16), preferred_element_type=F32),
                k_t=lax.dot_general(eye, k_bf, NT_DIMS, preferred_element_type=F32)))

        for (b, h), c in zip(chains, ops):
            ig_c, b_c, ig_r, b_r = gates[b]
            c["w_intra"], c["w_inter"], c["m_t"] = _mlstm_decay(ig_r[h:h + 1, :], b_r[h:h + 1, :], b_c[:, h:h + 1],
                                                                c["m_old"])

        for (b, h), c in zip(chains, ops):
            hs = slice(h * DK, (h + 1) * DK)
            s = c["s_raw"] * (c["w_intra"] * k_scale)
            num = c["w_inter"] * c["qc"] + jnp.dot(s.astype(BF16), c["v"], preferred_element_type=F32)
            den = (c["w_inter"] * jnp.sum(c["q"].astype(F32) * c["n_old"], axis=-1, keepdims=True)
                   + jnp.sum(s, axis=-1, keepdims=True))
            hid = num / jnp.maximum(jnp.abs(den), jnp.exp(-c["m_t"]))
            y_ref[b, :, hs] = _head_norm_gate(hid, o_ref[b, :, hs], gain_ref[:, hs]).astype(y_ref.dtype)

        for (b, h), c in zip(chains, ops):
            ig_c, b_c, ig_r, b_r = gates[b]
            c_new, n_new, m_new = _mlstm_state_update(c["k_bf"], c["v"], ig_c[:, h:h + 1], b_c[:, h:h + 1],
                                                      c["c_old"], c["n_old"], c["m_old"], k_t=c["k_t"],
                                                      gate_rows=(ig_r[h:h + 1, :], b_r[h:h + 1, :]))
            cf_ref[b, h] = c_new
            nf_ref[b, h] = n_new
            mf_ref[b, h:h + 1, :] = jnp.broadcast_to(m_new, (1, LANE_TILE))


def _mlstm_prompt(qkv, o_arr, o_col_block, gcol, grow, qkv_meta, gcol_meta, bias_row, bias_col, gain, batch, seq,
                  w_in_t, cast_weights):
    n_c = seq // CHUNK
    width = HEADS * DK
    const2 = lambda c: (0, 0)
    cast_in, cast_out, cast_shapes, cast_args = _weight_cast_specs(n_c, lambda c: c, w_in_t, cast_weights)
    qkv3 = qkv.reshape(batch, seq, 3 * width)
    o3 = o_arr.reshape(batch, seq, o_arr.shape[1])
    gcol3 = gcol.reshape(gcol.shape[0] // seq, seq, 2 * HEADS)
    state0 = lambda c: (0, 0, 0, 0)
    return pl.pallas_call(
        functools.partial(_mlstm_prompt_kernel, batch),
        grid=(n_c,),
        in_specs=[pl.BlockSpec((batch, CHUNK, width), lambda c: (0, c, 0)),
                  pl.BlockSpec((batch, CHUNK, width), lambda c: (0, c, 1)),
                  pl.BlockSpec((batch, CHUNK, width), lambda c: (0, c, 2)),
                  pl.BlockSpec((batch, CHUNK, width), lambda c: (0, c, o_col_block)),
                  pl.BlockSpec((batch, CHUNK, 2 * HEADS), lambda c: (0, c, 0))]
                 + [pl.BlockSpec((2 * HEADS, CHUNK), functools.partial(lambda b, c: (0, b * n_c + c), b))
                    for b in range(batch)]
                 + [pl.BlockSpec((N_META, width), lambda c: (0, 1)),
                    pl.BlockSpec((N_META, width), lambda c: (0, 2)),
                    pl.BlockSpec((N_META, 2 * HEADS), const2),
                    pl.BlockSpec((1, 2 * HEADS), const2),
                    pl.BlockSpec((2 * HEADS, 1), const2),
                    pl.BlockSpec((1, width), const2)] + cast_in,
        out_specs=[pl.BlockSpec((batch, CHUNK, width), lambda c: (0, c, 0)),
                   pl.BlockSpec((batch, HEADS, DK, DV), state0),
                   pl.BlockSpec((batch, HEADS, 1, DK), state0),
                   pl.BlockSpec((batch, HEADS, LANE_TILE), lambda c: (0, 0, 0))] + cast_out,
        out_shape=[jax.ShapeDtypeStruct((batch, seq, width), BF16),
                   jax.ShapeDtypeStruct((batch, HEADS, DK, DV), F32),
                   jax.ShapeDtypeStruct((batch, HEADS, 1, DK), F32),
                   jax.ShapeDtypeStruct((batch, HEADS, LANE_TILE), F32)] + cast_shapes,
        compiler_params=_cparams(1, 52),
        name="mlstm_prompt",
    )(qkv3, qkv3, qkv3, o3, gcol3, *([grow] * batch), qkv_meta, qkv_meta, gcol_meta, bias_row, bias_col, gain,
      *cast_args)


MLSTM_SB = 8


def _mlstm_sample_kernel(qkv_ref, o_ref, g_ref, c_ref, n_ref, m_ref, brow_ref, gain_ref,
                         y_ref, co_ref, no_ref, mo_ref):
    sb = MLSTM_SB
    k_scale = DK ** -0.5
    eye = _eye_bf16(DK)
    width = HEADS * DK
    pad = 2 * sb
    row_id = lax.broadcasted_iota(jnp.int32, (pad, DV), 0)

    g_all = g_ref[...] + brow_ref[...]
    ig_all = g_all[:, :HEADS]
    fl_all = _log_sigmoid(g_all[:, HEADS:])
    m_old_all = m_ref[...]
    m_new_all = jnp.maximum(fl_all + m_old_all, ig_all)
    w_in_all = jnp.exp(ig_all - m_new_all)
    w_ca_all = jnp.exp(fl_all + m_old_all - m_new_all)
    floor_all = jnp.exp(-m_new_all)
    mo_ref[...] = m_new_all

    for h in range(HEADS):
        hs = slice(h * DK, (h + 1) * DK)
        qk_rows = jnp.concatenate([qkv_ref[:, hs], qkv_ref[:, width + h * DK:width + (h + 1) * DK] * k_scale],
                                  axis=0).astype(BF16)
        qk_cols = lax.dot_general(eye, qk_rows, NT_DIMS, preferred_element_type=F32).astype(BF16)
        for i in range(sb):
            w_in = w_in_all[i:i + 1, h:h + 1]
            w_ca = w_ca_all[i:i + 1, h:h + 1]
            q = qk_rows[i:i + 1, :]
            qf = q.astype(F32)
            k = qk_rows[sb + i:sb + i + 1, :].astype(F32)
            v = qkv_ref[i:i + 1, 2 * width + h * DK:2 * width + (h + 1) * DK].astype(BF16).astype(F32)
            c_old = c_ref[i, h]
            n_old = n_ref[i, h]
            qc = jnp.dot(jnp.broadcast_to(q, (pad, DK)), c_old.astype(BF16), preferred_element_type=F32)[0:1, :]
            v_rows = jnp.where(row_id == sb + i, jnp.broadcast_to(v * w_in, (pad, DV)), 0.0).astype(BF16)
            kv = jnp.dot(qk_cols, v_rows, preferred_element_type=F32)
            s = jnp.sum(qf * k, axis=-1, keepdims=True) * w_in
            num = w_ca * qc + s * v
            den = w_ca * jnp.sum(qf * n_old, axis=-1, keepdims=True) + s
            hid = num / jnp.maximum(jnp.abs(den), floor_all[i:i + 1, h:h + 1])
            co_ref[i, h] = w_ca * c_old + kv
            no_ref[i, h] = w_ca * n_old + k * w_in
            y_ref[i:i + 1, hs] = _head_norm_gate(hid, o_ref[i:i + 1, hs], gain_ref[:, hs])


def _mlstm_sample(qkv, o_arr, gcol, c0, n0, m0, bias_row, gain):
    nb = qkv.shape[0]
    width = HEADS * DK
    sb = MLSTM_SB
    const2 = lambda i: (0, 0)
    return pl.pallas_call(
        _mlstm_sample_kernel,
        grid=(nb // sb,),
        in_specs=[pl.BlockSpec((sb, 3 * width), lambda i: (i, 0)),
                  pl.BlockSpec((sb, width), lambda i: (i, 0)),
                  pl.BlockSpec((sb, 2 * HEADS), lambda i: (i, 0)),
                  pl.BlockSpec((sb, HEADS, DK, DV), lambda i: (i, 0, 0, 0)),
                  pl.BlockSpec((sb, HEADS, 1, DK), lambda i: (i, 0, 0, 0)),
                  pl.BlockSpec((sb, HEADS), lambda i: (i, 0)),
                  pl.BlockSpec((1, 2 * HEADS), const2),
                  pl.BlockSpec((1, width), const2)],
        out_specs=[pl.BlockSpec((sb, width), lambda i: (i, 0)),
                   pl.BlockSpec((sb, HEADS, DK, DV), lambda i: (i, 0, 0, 0)),
                   pl.BlockSpec((sb, HEADS, 1, DK), lambda i: (i, 0, 0, 0)),
                   pl.BlockSpec((sb, HEADS), lambda i: (i, 0))],
        out_shape=[jax.ShapeDtypeStruct((nb, width), F32),
                   jax.ShapeDtypeStruct((nb, HEADS, DK, DV), F32),
                   jax.ShapeDtypeStruct((nb, HEADS, 1, DK), F32),
                   jax.ShapeDtypeStruct((nb, HEADS), F32)],
        compiler_params=_cparams(1, 48),
        name="mlstm_sample",
    )(qkv, o_arr, gcol, c0, n0, m0, bias_row, gain)


def _merge_rows(ya, yb, x, wgab_ref, wglu_ref, bglu_ref, wa_ref, wb_ref, wout_ref, g1_ref, b1_ref):
    xb = x.astype(BF16)
    g = _gelu_tanh(ya)
    z = jnp.dot(g.astype(BF16), wglu_ref[...], preferred_element_type=F32) + bglu_ref[...]
    out_a = g * _sigmoid(z)
    up_a = jnp.dot(out_a.astype(BF16), wa_ref[...], preferred_element_type=F32)
    g_a = lax.dot_general(xb, wgab_ref[0:D_MODEL, :], NT_DIMS, preferred_element_type=F32)
    mix = _sigmoid(g_a) * up_a
    up_b = jnp.dot(yb, wb_ref[...], preferred_element_type=F32)
    g_b = lax.dot_general(xb, wgab_ref[D_MODEL:2 * D_MODEL, :], NT_DIMS, preferred_element_type=F32)
    mix = mix + _sigmoid(g_b) * up_b
    mo = jnp.dot(mix.astype(BF16), wout_ref[...], preferred_element_type=F32)
    return _layernorm_rows(ALPHA * x + mo, g1_ref[...], b1_ref[...])


def _merge_kernel(ya_ref, yb_ref, x_ref, yas_ref, ybs_ref, xs_ref, *rest):
    weights, (o_ref, os_ref, ob_ref, osb_ref) = rest[:-4], rest[-4:]
    x1 = _merge_rows(ya_ref[...], yb_ref[...], x_ref[...], *weights)
    o_ref[...] = x1
    ob_ref[...] = x1.astype(BF16)

    @pl.when(pl.program_id(0) == pl.num_programs(0) - 1)
    def _():
        x1s = _merge_rows(yas_ref[...], ybs_ref[...].astype(BF16), xs_ref[...], *weights)
        os_ref[...] = x1s
        osb_ref[...] = x1s.astype(BF16)


def _merge(ya, yb, x, ya_s, yb_s, x_s, wt_gab, w_glu, b_glu, w_a_up, w_b_up, w_out, ln_g, ln_b, tm):
    r = x.shape[0]
    rs = x_s.shape[0]
    const2 = lambda i: (0, 0)
    resident = functools.partial(pl.BlockSpec, index_map=const2, pipeline_mode=pl.Buffered(1))
    return pl.pallas_call(
        _merge_kernel,
        grid=(r // tm,),
        in_specs=[pl.BlockSpec((tm, SSM_WIDTH), lambda i: (i, 0)),
                  pl.BlockSpec((tm, HEADS * DV), lambda i: (i, 0)),
                  pl.BlockSpec((tm, D_MODEL), lambda i: (i, 0)),
                  resident((rs, SSM_WIDTH)),
                  resident((rs, HEADS * DV)),
                  resident((rs, D_MODEL)),
                  resident((2 * D_MODEL, D_MODEL)),
                  resident((SSM_WIDTH, SSM_WIDTH)),
                  resident((1, SSM_WIDTH)),
                  resident((SSM_WIDTH, D_MODEL)),
                  resident((HEADS * DV, D_MODEL)),
                  resident((D_MODEL, D_MODEL)),
                  resident((1, D_MODEL)),
                  resident((1, D_MODEL))],
        out_specs=[pl.BlockSpec((tm, D_MODEL), lambda i: (i, 0)),
                   pl.BlockSpec((rs, D_MODEL), const2),
                   pl.BlockSpec((tm, D_MODEL), lambda i: (i, 0)),
                   pl.BlockSpec((rs, D_MODEL), const2)],
        out_shape=[jax.ShapeDtypeStruct((r, D_MODEL), F32), jax.ShapeDtypeStruct((rs, D_MODEL), F32),
                   jax.ShapeDtypeStruct((r, D_MODEL), BF16), jax.ShapeDtypeStruct((rs, D_MODEL), BF16)],
        compiler_params=_cparams(1, 59),
        name="merge_ln1",
    )(ya, yb, x, ya_s, yb_s, x_s, wt_gab, w_glu, b_glu, w_a_up, w_b_up, w_out, ln_g, ln_b)


def _swiglu(xb, wg, wu):
    hg = jnp.dot(xb, wg, preferred_element_type=F32)
    hu = jnp.dot(xb, wu, preferred_element_type=F32)
    return ((hg * _sigmoid(hg)) * hu).astype(BF16)


FFN_UP_PIECE = 512


def _ffn_up_kernel(x_ref, xs_ref, wg_ref, wu_ref, wd_ref, h_ref, hs_ref, wdb_ref, wgb_s, wub_s):
    wgb_s[...] = wg_ref[...].astype(BF16)
    wub_s[...] = wu_ref[...].astype(BF16)

    for r0 in range(0, x_ref.shape[0], FFN_UP_PIECE):
        rows = slice(r0, r0 + FFN_UP_PIECE)
        h_ref[rows, :] = _swiglu(x_ref[rows, :], wgb_s[...], wub_s[...])
    wdb_ref[...] = wd_ref[...].astype(BF16)

    @pl.when(pl.program_id(0) == pl.num_programs(0) - 1)
    def _():
        hs_ref[...] = _swiglu(xs_ref[...], wgb_s[...], wub_s[...])


def _ffn_up(xb, xsb, w_gate3, w_up3, w_down3, tm, tf):
    r = xb.shape[0]
    rs = xsb.shape[0]
    n_f, n_i = D_FF // tf, r // tm
    wd_rows = D_FF // (n_f * n_i)
    hs_block = lambda i, f: jnp.where(i == n_i - 1, f, 0)
    return pl.pallas_call(
        _ffn_up_kernel,
        grid=(n_i, n_f),
        in_specs=[pl.BlockSpec((tm, D_MODEL), lambda i, f: (i, 0)),
                  pl.BlockSpec((rs, D_MODEL), lambda i, f: (0, 0)),
                  pl.BlockSpec((None, D_MODEL, tf), lambda i, f: (0, 0, f)),
                  pl.BlockSpec((None, D_MODEL, tf), lambda i, f: (0, 0, f)),
                  pl.BlockSpec((None, wd_rows, D_MODEL), lambda i, f: (0, i * n_f + f, 0))],
        out_specs=[pl.BlockSpec((tm, tf), lambda i, f: (i, f)),
                   pl.BlockSpec((rs, tf), lambda i, f: (0, hs_block(i, f))),
                   pl.BlockSpec((wd_rows, D_MODEL), lambda i, f: (i * n_f + f, 0))],
        out_shape=[jax.ShapeDtypeStruct((r, D_FF), BF16), jax.ShapeDtypeStruct((rs, D_FF), BF16),
                   jax.ShapeDtypeStruct((D_FF, D_MODEL), BF16)],
        scratch_shapes=[pltpu.VMEM((D_MODEL, tf), BF16), pltpu.VMEM((D_MODEL, tf), BF16)],
        compiler_params=_cparams(2, 58),
        name="ffn_up",
    )(xb, xsb, w_gate3, w_up3, w_down3)


FFN_DOWN_PIECE = 256


def _ffn_down_rows(h, x, wd_ref, g2_ref, b2_ref):
    return _layernorm_rows(ALPHA * x + jnp.dot(h, wd_ref[...], preferred_element_type=F32), g2_ref[...], b2_ref[...])


def _ffn_down_kernel(h_ref, x_ref, hs_ref, xs_ref, wd_ref, g2_ref, b2_ref, o_ref, os_ref):
    for r0 in range(0, h_ref.shape[0], FFN_DOWN_PIECE):
        rows = slice(r0, r0 + FFN_DOWN_PIECE)
        o_ref[rows, :] = _ffn_down_rows(h_ref[rows, :], x_ref[rows, :], wd_ref, g2_ref, b2_ref)

    @pl.when(pl.program_id(0) == pl.num_programs(0) - 1)
    def _():
        os_ref[...] = _ffn_down_rows(hs_ref[...], xs_ref[...], wd_ref, g2_ref, b2_ref)


def _ffn_down(h, x, h_s, x_s, wd_b, ln_g, ln_b, tm):
    r = x.shape[0]
    rs = x_s.shape[0]
    const2 = lambda i: (0, 0)
    resident = functools.partial(pl.BlockSpec, index_map=const2, pipeline_mode=pl.Buffered(1))
    return pl.pallas_call(
        _ffn_down_kernel,
        grid=(r // tm,),
        in_specs=[pl.BlockSpec((tm, D_FF), lambda i: (i, 0)),
                  pl.BlockSpec((tm, D_MODEL), lambda i: (i, 0)),
                  resident((rs, D_FF)),
                  resident((rs, D_MODEL)),
                  resident((D_FF, D_MODEL)),
                  resident((1, D_MODEL)),
                  resident((1, D_MODEL))],
        out_specs=[pl.BlockSpec((tm, D_MODEL), lambda i: (i, 0)),
                   pl.BlockSpec((rs, D_MODEL), const2)],
        out_shape=[jax.ShapeDtypeStruct((r, D_MODEL), F32), jax.ShapeDtypeStruct((rs, D_MODEL), F32)],
        compiler_params=_cparams(1, 59),
        name="ffn_down_ln2",
    )(h, x, h_s, x_s, wd_b, ln_g, ln_b)


def kernel(x_prompt, x_sample, state_ssm_re, state_ssm_im, state_mlstm_c, state_mlstm_n, state_mlstm_m,
           meta_tokens, w_in, b_if, ssm_a_re, ssm_a_im, ssm_log_dt, ssm_b_re, ssm_b_im, ssm_c_re, ssm_c_im,
           ssm_d, w_glu, b_glu, w_a_up, mh_gain, w_b_up, w_out, ln1_g, ln1_b, w_gate, w_up, w_down,
           ln2_g, ln2_b):
    batch, seq, _ = x_prompt.shape
    nb = x_sample.shape[0]
    width = HEADS * DK

    bias_row = b_if[0].reshape(1, 2 * HEADS)
    bias_col = b_if[0].reshape(2 * HEADS, 1)
    gain = mh_gain[0].reshape(1, width)
    d_skip = ssm_d[0].reshape(1, SSM_WIDTH)
    b_glu_r = b_glu[0].reshape(1, SSM_WIDTH)
    ln1 = (ln1_g[0].reshape(1, D_MODEL), ln1_b[0].reshape(1, D_MODEL))
    ln2 = (ln2_g[0].reshape(1, D_MODEL), ln2_b[0].reshape(1, D_MODEL))
    abar_re, abar_im, bt, ct = _s5_discretise(
        ssm_a_re[0], ssm_a_im[0], ssm_log_dt[0], ssm_b_re[0], ssm_b_im[0], ssm_c_re[0], ssm_c_im[0])

    xp = x_prompt.reshape(batch * seq, D_MODEL)
    xs = x_sample.reshape(nb, D_MODEL)
    xsm = jnp.concatenate([xs, meta_tokens], axis=0)
    w_in_t = jnp.swapaxes(w_in, 1, 2)
    uo_p, qkv_p, uo_s, qkv_s, gcol_p, grow_p, gcol_s = _proj_all(xp, xsm, w_in_t, 1024)

    u_meta = uo_s[nb:nb + N_META, :SSM_WIDTH]
    qkv_meta = qkv_s[nb:nb + N_META].astype(BF16)
    gcol_meta = gcol_s[nb:nb + N_META]

    ya_p, pf_re, pf_im = _s5_prompt(uo_p, 0, u_meta, abar_re, abar_im, bt, ct, d_skip, batch, seq)
    yb_p, pc, pn, pm, wt_gab, w_out_b, w_a_b, w_b_b, w_glu_b = _mlstm_prompt(
        qkv_p, uo_p, 1, gcol_p, grow_p, qkv_meta, gcol_meta, bias_row, bias_col, gain, batch, seq,
        w_in_t, (w_out, w_a_up, w_b_up, w_glu))
    ya_s, sf_re, sf_im = _s5_sample(uo_s[:nb, :SSM_WIDTH], state_ssm_re[0].reshape(nb, SSM_FLAT),
                                    state_ssm_im[0].reshape(nb, SSM_FLAT), abar_re, abar_im, bt, ct, d_skip)
    yb_s, sc, sn, sm = _mlstm_sample(qkv_s[:nb], uo_s[:nb, SSM_WIDTH:], gcol_s[:nb], state_mlstm_c[0],
                                     state_mlstm_n[0].reshape(nb, HEADS, 1, DK), state_mlstm_m[0], bias_row, gain)

    x1_p, x1_s, x1b_p, x1b_s = _merge(ya_p, yb_p.reshape(batch * seq, width), xp, ya_s, yb_s, xs,
                                      wt_gab, w_glu_b, b_glu_r, w_a_b, w_b_b, w_out_b, *ln1, tm=256)
    h_p, h_s, w_down_b = _ffn_up(x1b_p, x1b_s, w_gate, w_up, w_down, tm=2048, tf=512)
    y_p, y_s = _ffn_down(h_p, x1_p, h_s, x1_s, w_down_b, *ln2, tm=512)

    return (y_p.reshape(batch, seq, D_MODEL),
            y_s.reshape(nb, 1, D_MODEL),
            pf_re.reshape(1, batch, SSM_GROUPS, SSM_STATE),
            pf_im.reshape(1, batch, SSM_GROUPS, SSM_STATE),
            pc.reshape(1, batch, HEADS, DK, DV),
            pn.reshape(1, batch, HEADS, DK),
            pm[:, :, 0].reshape(1, batch, HEADS),
            sf_re.reshape(1, nb, SSM_GROUPS, SSM_STATE),
            sf_im.reshape(1, nb, SSM_GROUPS, SSM_STATE),
            sc.reshape(1, nb, HEADS, DK, DV),
            sn.reshape(1, nb, HEADS, DK),
            sm.reshape(1, nb, HEADS))
```

```python
import functools

import jax
import jax.numpy as jnp
import numpy as np
from jax import lax
from jax.experimental import pallas as pl
from jax.experimental.pallas import tpu as pltpu

F32 = jnp.float32
BF16 = jnp.bfloat16

D_MODEL = 2048
N_META = 16
SSM_WIDTH = 1024
SSM_GROUP = 16
SSM_GROUPS = 64
SSM_STATE = 64
SSM_FLAT = SSM_GROUPS * SSM_STATE
SSM_JBLK = 4
SSM_BLK_IN = SSM_WIDTH // SSM_JBLK
SSM_BLK_STATE = SSM_FLAT // SSM_JBLK
LANE_TILE = 128
HEADS = 4
DK = 256
DV = 256
CHUNK = 128
D_FF = 5632
EPS = 1e-5
ALPHA = 2.0 ** 0.25

S5_TC = 256
S5_NSEG = 8
S5_SEG = S5_TC // S5_NSEG
S5_LW = 512
S5_CPS = 2

V7X_SCOPED_VMEM_MAX_BYTES = 60000 * 1024


def _cparams(n_axes, vmem_mb):
    return pltpu.CompilerParams(
        dimension_semantics=("arbitrary",) * n_axes,
        vmem_limit_bytes=min(vmem_mb * 1024 * 1024, V7X_SCOPED_VMEM_MAX_BYTES),
    )


def _sigmoid(x):
    return 1.0 / (1.0 + jnp.exp(-x))


def _log_sigmoid(x):
    return jnp.minimum(x, 0.0) - jnp.log(1.0 + jnp.exp(-jnp.abs(x)))


def _gelu_tanh(x):
    c = 0.7978845608028654
    return 0.5 * x * (1.0 + jnp.tanh(c * (x + 0.044715 * (x * x * x))))


def _layernorm_rows(x, g, b):
    mu = jnp.mean(x, axis=-1, keepdims=True)
    xc = x - mu
    var = jnp.mean(xc * xc, axis=-1, keepdims=True)
    return xc * lax.rsqrt(var + EPS) * g + b


PROJ_TN = 1024
GATE_ROWS = 2 * HEADS
TAIL_ROW0 = SSM_WIDTH + 4 * HEADS * DK
NT_DIMS = (((1,), (1,)), ((), ()))


def _proj_kernel(with_gates, x_ref, xs_ref, wt_ref, *rest):
    if with_gates:
        wg_ref, o_ref, os_ref, g_ref, gt_ref, gs_ref, wb_s = rest
    else:
        o_ref, os_ref, wb_s = rest
    i = pl.program_id(1)

    @pl.when(i == 0)
    def _():
        wb_s[...] = wt_ref[...].astype(BF16)

    xb = x_ref[...].astype(BF16)
    o_ref[...] = lax.dot_general(xb, wb_s[...], NT_DIMS, preferred_element_type=F32).astype(o_ref.dtype)
    first_block = pl.program_id(0) == 0
    if with_gates:
        @pl.when(first_block)
        def _():
            wg = wg_ref[...].astype(BF16)
            g_ref[...] = lax.dot_general(xb, wg, NT_DIMS, preferred_element_type=F32)
            gt_ref[...] = lax.dot_general(wg, xb, NT_DIMS, preferred_element_type=F32)

    @pl.when(i == pl.num_programs(1) - 1)
    def _():
        xsb = xs_ref[...].astype(BF16)
        os_ref[...] = lax.dot_general(xsb, wb_s[...], NT_DIMS, preferred_element_type=F32).astype(os_ref.dtype)
        if with_gates:
            @pl.when(first_block)
            def _():
                gs_ref[...] = lax.dot_general(xsb, wg_ref[...].astype(BF16), NT_DIMS, preferred_element_type=F32)


def _proj(x, xs, wt3, blocks, out_dtype, small_dtype, tm, name, with_gates=False):
    r, k = x.shape
    rs = xs.shape[0]
    first, step, count = blocks
    tn = PROJ_TN
    nm = r // tm
    in_specs = [pl.BlockSpec((tm, k), lambda j, i: (i, 0)),
                pl.BlockSpec((rs, k), lambda j, i: (0, 0)),
                pl.BlockSpec((None, tn, k), lambda j, i: (0, first + step * j, 0))]
    out_specs = [pl.BlockSpec((tm, tn), lambda j, i: (i, j)),
                 pl.BlockSpec((rs, tn), lambda j, i: (0, j))]
    out_shape = [jax.ShapeDtypeStruct((r, count * tn), out_dtype),
                 jax.ShapeDtypeStruct((rs, count * tn), small_dtype)]
    args = (x, xs, wt3)
    if with_gates:
        ng = GATE_ROWS
        in_specs += [pl.BlockSpec((None, ng, k), lambda j, i: (0, TAIL_ROW0 // ng, 0))]
        gate_block = lambda j, i: jnp.where(j == 0, i, nm - 1)
        out_specs += [pl.BlockSpec((tm, ng), lambda j, i: (gate_block(j, i), 0)),
                      pl.BlockSpec((ng, tm), lambda j, i: (0, gate_block(j, i))),
                      pl.BlockSpec((rs, ng), lambda j, i: (0, 0))]
        out_shape += [jax.ShapeDtypeStruct((r, ng), F32), jax.ShapeDtypeStruct((ng, r), F32),
                      jax.ShapeDtypeStruct((rs, ng), F32)]
        args = (x, xs, wt3, wt3)
    return pl.pallas_call(
        functools.partial(_proj_kernel, with_gates),
        grid=(count, nm),
        in_specs=in_specs,
        out_specs=out_specs,
        out_shape=out_shape,
        scratch_shapes=[pltpu.VMEM((tn, k), BF16)],
        compiler_params=_cparams(2, 56),
        name=name,
    )(*args)


def _s5_discretise(a_re, a_im, log_dt, b_re, b_im, c_re, c_im):
    dt = jnp.exp(log_dt)
    e = jnp.exp(a_re * dt)
    abar_re = e * jnp.cos(a_im * dt)
    abar_im = e * jnp.sin(a_im * dt)
    nr = abar_re - 1.0
    ni = abar_im
    den = a_re * a_re + a_im * a_im
    coef_re = (nr * a_re + ni * a_im) / den
    coef_im = (ni * a_re - nr * a_im) / den
    bb_re = coef_re[..., None] * b_re - coef_im[..., None] * b_im
    bb_im = coef_re[..., None] * b_im + coef_im[..., None] * b_re
    gpb = SSM_GROUPS // SSM_JBLK

    def b_rows(bb):
        t = bb.reshape(SSM_JBLK, gpb, SSM_STATE, SSM_GROUP).transpose(0, 1, 3, 2)
        return t.reshape(SSM_JBLK, SSM_BLK_IN, SSM_STATE)

    def c_rows(cc):
        t = cc.reshape(SSM_JBLK, gpb, SSM_GROUP, SSM_STATE).transpose(0, 1, 3, 2)
        return t.reshape(SSM_JBLK, SSM_BLK_STATE, SSM_GROUP)

    b_rep, c_rep = LANE_TILE // SSM_STATE, LANE_TILE // SSM_GROUP
    bt = jnp.concatenate([jnp.tile(b_rows(bb_re), (1, 1, b_rep)), jnp.tile(b_rows(bb_im), (1, 1, b_rep))], axis=-1)
    ct = jnp.concatenate([jnp.tile(c_rows(c_re), (1, 1, c_rep)), jnp.tile(c_rows(-c_im), (1, 1, c_rep))], axis=-1)
    return abar_re.reshape(1, SSM_FLAT), abar_im.reshape(1, SSM_FLAT), bt.astype(BF16), ct.astype(BF16)


def _s5_expand_weights(bt_ref, ct_ref, bw_s, cwr_s, cwi_s):
    log2 = lambda n: n.bit_length() - 1
    rb = lax.broadcasted_iota(jnp.int32, (SSM_BLK_IN, 2 * SSM_BLK_STATE), 0)
    cb = lax.broadcasted_iota(jnp.int32, (SSM_BLK_IN, 2 * SSM_BLK_STATE), 1)
    bmask = (rb >> log2(SSM_GROUP)) == ((cb & (SSM_BLK_STATE - 1)) >> log2(SSM_STATE))
    rc = lax.broadcasted_iota(jnp.int32, (SSM_BLK_STATE, SSM_BLK_IN), 0)
    cc = lax.broadcasted_iota(jnp.int32, (SSM_BLK_STATE, SSM_BLK_IN), 1)
    cmask = (rc >> log2(SSM_STATE)) == (cc >> log2(SSM_GROUP))
    b_tiles, c_tiles = SSM_BLK_STATE // LANE_TILE, SSM_BLK_IN // LANE_TILE
    for j in range(SSM_JBLK):
        bt = bt_ref[j].astype(F32)
        wide = jnp.concatenate([bt[:, :LANE_TILE]] * b_tiles + [bt[:, LANE_TILE:]] * b_tiles, axis=1)
        bw_s[j] = jnp.where(bmask, wide, 0.0).astype(BF16)
        ct = ct_ref[j].astype(F32)
        cwr_s[j] = jnp.where(cmask, jnp.concatenate([ct[:, :LANE_TILE]] * c_tiles, axis=1), 0.0).astype(BF16)
        cwi_s[j] = jnp.where(cmask, jnp.concatenate([ct[:, LANE_TILE:]] * c_tiles, axis=1), 0.0).astype(BF16)


def _cmul_add(ar, ai, hr, hi, xr, xi):
    return ar * hr - ai * hi + xr, ar * hi + ai * hr + xi


def _s5_project_in(u, b_ref, hre_ref, him_ref, rows):
    for j in range(SSM_JBLK):
        state = slice(j * SSM_BLK_STATE, (j + 1) * SSM_BLK_STATE)
        bu = jnp.dot(u[:, j * SSM_BLK_IN:(j + 1) * SSM_BLK_IN], b_ref[j], preferred_element_type=F32)
        hre_ref[0:rows, state] = bu[:, :SSM_BLK_STATE]
        him_ref[0:rows, state] = bu[:, SSM_BLK_STATE:]


def _s5_project_out(hre_ref, him_ref, cre_ref, cimn_ref, rows):
    ys = []
    for j in range(SSM_JBLK):
        state = slice(j * SSM_BLK_STATE, (j + 1) * SSM_BLK_STATE)
        hr = hre_ref[0:rows, state].astype(BF16)
        hi = him_ref[0:rows, state].astype(BF16)
        y = jnp.dot(hr, cre_ref[j], preferred_element_type=F32)
        ys.append(y + jnp.dot(hi, cimn_ref[j], preferred_element_type=F32))
    return jnp.concatenate(ys, axis=1)


def _s5_scan_chunk(hre_ref, him_ref, hsre_ref, hsim_ref, pre_ref, pim_ref, are_ref, aim_ref, cre_s, cim_s):
    full = (S5_NSEG, S5_LW)
    for lg in range(SSM_FLAT // S5_LW):
        lanes = slice(lg * S5_LW, (lg + 1) * S5_LW)
        ar = jnp.broadcast_to(are_ref[:, lanes], full)
        ai = jnp.broadcast_to(aim_ref[:, lanes], full)

        hr = jnp.zeros(full, F32)
        hi = jnp.zeros(full, F32)
        for k in range(S5_SEG):
            rows = slice(k * S5_NSEG, (k + 1) * S5_NSEG)
            hr, hi = _cmul_add(ar, ai, hr, hi, hre_ref[rows, lanes], him_ref[rows, lanes])
            hre_ref[rows, lanes] = hr
            him_ref[rows, lanes] = hi
        er, ei = hr, hi

        asr = pre_ref[S5_TC - 1:S5_TC, lanes]
        asi = pim_ref[S5_TC - 1:S5_TC, lanes]
        hr = cre_s[:, lanes]
        hi = cim_s[:, lanes]
        for j in range(S5_NSEG):
            hsre_ref[j:j + 1, lanes] = hr
            hsim_ref[j:j + 1, lanes] = hi
            hr, hi = _cmul_add(asr, asi, hr, hi, er[j:j + 1, :], ei[j:j + 1, :])
        cre_s[:, lanes] = hr
        cim_s[:, lanes] = hi

        sr = hsre_ref[:, lanes]
        si = hsim_ref[:, lanes]
        for k in range(S5_SEG):
            rows = slice(k * S5_NSEG, (k + 1) * S5_NSEG)
            pr = pre_ref[rows, lanes]
            pi = pim_ref[rows, lanes]
            hre_ref[rows, lanes] = hre_ref[rows, lanes] + (pr * sr - pi * si)
            him_ref[rows, lanes] = him_ref[rows, lanes] + (pr * si + pi * sr)


N_CAST_IN = 6


def _weight_cast_specs(n_steps, step_of, w_in_t, others):
    gab_rows = 2 * D_MODEL // n_steps
    gab_blk0 = TAIL_ROW0 // gab_rows
    gab_next0 = (TAIL_ROW0 + gab_rows) // GATE_ROWS
    rows = [w.shape[1] // n_steps for w in others]
    in_specs = [pl.BlockSpec((None, gab_rows, D_MODEL), lambda *g: (0, gab_blk0 + step_of(*g), 0)),
                pl.BlockSpec((None, GATE_ROWS, D_MODEL),
                             lambda *g: (0, gab_next0 + (gab_rows // GATE_ROWS) * step_of(*g), 0))]
    in_specs += [pl.BlockSpec((None, nr, w.shape[2]), lambda *g: (0, step_of(*g), 0)) for nr, w in zip(rows, others)]
    out_specs = [pl.BlockSpec((gab_rows, D_MODEL), lambda *g: (step_of(*g), 0))]
    out_specs += [pl.BlockSpec((nr, w.shape[2]), lambda *g: (step_of(*g), 0)) for nr, w in zip(rows, others)]
    out_shapes = [jax.ShapeDtypeStruct((2 * D_MODEL, D_MODEL), BF16)]
    out_shapes += [jax.ShapeDtypeStruct(w.shape[1:], BF16) for w in others]
    return in_specs, out_specs, out_shapes, (w_in_t, w_in_t) + tuple(others)


def _weight_cast_step(in_refs, out_refs):
    wga_ref, wgn_ref = in_refs[:2]
    out_refs[0][...] = jnp.concatenate([wga_ref[GATE_ROWS:, :], wgn_ref[...]], axis=0).astype(BF16)
    for src, dst in zip(in_refs[2:], out_refs[1:]):
        dst[...] = src[...].astype(BF16)


def _s5_prompt_kernel(u_ref, um_ref, perm_ref, permt_ref, bt_ref, ct_ref, are_ref, aim_ref, d_ref,
                      y_ref, fre_ref, fim_ref,
                      hre_ref, him_ref, hre2_ref, him2_ref, pre_ref, pim_ref, cre_s, cim_s,
                      hsre_ref, hsim_ref, hsre2_ref, hsim2_ref,
                      b_ref, cre_ref, cimn_ref):
    b_id = pl.program_id(0)
    c_id = pl.program_id(1)
    n_c = pl.num_programs(1)

    @pl.when(jnp.logical_and(b_id == 0, c_id == 0))
    def _():
        _s5_expand_weights(bt_ref, ct_ref, b_ref, cre_ref, cimn_ref)
        ar = jnp.broadcast_to(are_ref[...], (S5_NSEG, SSM_FLAT))
        ai = jnp.broadcast_to(aim_ref[...], (S5_NSEG, SSM_FLAT))
        pre_ref[0:S5_NSEG, :] = ar
        pim_ref[0:S5_NSEG, :] = ai

        def body(i, carry):
            pr, pi = carry
            nr = ar * pr - ai * pi
            ni = ar * pi + ai * pr
            rows = pl.ds(pl.multiple_of(i * S5_NSEG, S5_NSEG), S5_NSEG)
            pre_ref[rows, :] = nr
            pim_ref[rows, :] = ni
            return nr, ni

        lax.fori_loop(1, S5_SEG, body, (ar, ai))

    @pl.when(c_id == 0)
    def _():
        _s5_project_in(um_ref[...].astype(BF16), b_ref, hre_ref, him_ref, N_META)
        ar = are_ref[...]
        ai = aim_ref[...]
        hr = jnp.zeros((1, SSM_FLAT), F32)
        hi = jnp.zeros((1, SSM_FLAT), F32)
        for t in range(N_META):
            hr, hi = _cmul_add(ar, ai, hr, hi, hre_ref[t:t + 1, :], him_ref[t:t + 1, :])
        cre_s[...] = hr
        cim_s[...] = hi

    bufs = ((hre_ref, him_ref, hsre_ref, hsim_ref), (hre2_ref, him2_ref, hsre2_ref, hsim2_ref))
    us = [u_ref[n * S5_TC:(n + 1) * S5_TC, :] for n in range(S5_CPS)]
    for n in range(S5_CPS):
        u_perm = jnp.dot(perm_ref[...], us[n].astype(BF16), preferred_element_type=F32).astype(BF16)
        _s5_project_in(u_perm, b_ref, bufs[n][0], bufs[n][1], S5_TC)
    for n in range(S5_CPS):
        hre_n, him_n, hsre_n, hsim_n = bufs[n]
        _s5_scan_chunk(hre_n, him_n, hsre_n, hsim_n, pre_ref, pim_ref, are_ref, aim_ref, cre_s, cim_s)
        y_perm = _s5_project_out(hre_n, him_n, cre_ref, cimn_ref, S5_TC)
        y = sum(jnp.dot(permt_ref[...], piece, preferred_element_type=F32) for piece in _split3(y_perm))
        y_ref[n * S5_TC:(n + 1) * S5_TC, :] = y + d_ref[...] * us[n]

    @pl.when(c_id == n_c - 1)
    def _():
        fre_ref[0] = cre_s[...]
        fim_ref[0] = cim_s[...]


def _s5_weight_scratch():
    return [pltpu.VMEM((SSM_JBLK, SSM_BLK_IN, 2 * SSM_BLK_STATE), BF16),
            pltpu.VMEM((SSM_JBLK, SSM_BLK_STATE, SSM_BLK_IN), BF16),
            pltpu.VMEM((SSM_JBLK, SSM_BLK_STATE, SSM_BLK_IN), BF16)]


def _s5_prompt(u_arr, u_col_block, u_meta, abar_re, abar_im, bt, ct, d_skip, batch, seq):
    rows_step = S5_CPS * S5_TC
    n_c = seq // rows_step
    const3 = lambda b, c: (0, 0, 0)
    const2 = lambda b, c: (0, 0)
    rows = np.arange(S5_TC)
    perm_np = np.zeros((S5_TC, S5_TC), np.float32)
    perm_np[rows, (rows % S5_NSEG) * S5_SEG + rows // S5_NSEG] = 1.0
    perm = jnp.asarray(perm_np, dtype=BF16)
    permt = jnp.asarray(perm_np.T, dtype=BF16)
    return pl.pallas_call(
        _s5_prompt_kernel,
        grid=(batch, n_c),
        in_specs=[pl.BlockSpec((rows_step, SSM_WIDTH), lambda b, c: (b * n_c + c, u_col_block)),
                  pl.BlockSpec((N_META, SSM_WIDTH), const2),
                  pl.BlockSpec((S5_TC, S5_TC), const2),
                  pl.BlockSpec((S5_TC, S5_TC), const2),
                  pl.BlockSpec((SSM_JBLK, SSM_BLK_IN, 2 * LANE_TILE), const3, pipeline_mode=pl.Buffered(1)),
                  pl.BlockSpec((SSM_JBLK, SSM_BLK_STATE, 2 * LANE_TILE), const3, pipeline_mode=pl.Buffered(1)),
                  pl.BlockSpec((1, SSM_FLAT), const2),
                  pl.BlockSpec((1, SSM_FLAT), const2),
                  pl.BlockSpec((1, SSM_WIDTH), const2)],
        out_specs=[pl.BlockSpec((rows_step, SSM_WIDTH), lambda b, c: (b * n_c + c, 0)),
                   pl.BlockSpec((1, 1, SSM_FLAT), lambda b, c: (b, 0, 0)),
                   pl.BlockSpec((1, 1, SSM_FLAT), lambda b, c: (b, 0, 0))],
        out_shape=[jax.ShapeDtypeStruct((batch * seq, SSM_WIDTH), F32),
                   jax.ShapeDtypeStruct((batch, 1, SSM_FLAT), F32),
                   jax.ShapeDtypeStruct((batch, 1, SSM_FLAT), F32)],
        scratch_shapes=[pltpu.VMEM((S5_TC, SSM_FLAT), F32)] * 4
                       + [pltpu.VMEM((S5_TC, SSM_FLAT), F32)] * 2
                       + [pltpu.VMEM((1, SSM_FLAT), F32)] * 2
                       + [pltpu.VMEM((S5_NSEG, SSM_FLAT), F32)] * 4
                       + _s5_weight_scratch(),
        compiler_params=_cparams(2, 59),
        name="s5_prompt",
    )(u_arr, u_meta, perm, permt, bt, ct, abar_re, abar_im, d_skip)


def _s5_sample_kernel(u_ref, h0re_ref, h0im_ref, bt_ref, ct_ref, are_ref, aim_ref, d_ref,
                      y_ref, hre_ref, him_ref, b_ref, cre_ref, cimn_ref):
    rows = u_ref.shape[0]
    _s5_expand_weights(bt_ref, ct_ref, b_ref, cre_ref, cimn_ref)
    u = u_ref[...]
    _s5_project_in(u.astype(BF16), b_ref, hre_ref, him_ref, rows)
    nr, ni = _cmul_add(are_ref[...], aim_ref[...], h0re_ref[...], h0im_ref[...], hre_ref[...], him_ref[...])
    hre_ref[...] = nr
    him_ref[...] = ni
    y_ref[...] = _s5_project_out(hre_ref, him_ref, cre_ref, cimn_ref, rows) + d_ref[...] * u


def _s5_sample(u, h0_re, h0_im, abar_re, abar_im, bt, ct, d_skip):
    rows = u.shape[0]
    return pl.pallas_call(
        _s5_sample_kernel,
        out_shape=[jax.ShapeDtypeStruct((rows, SSM_WIDTH), F32),
                   jax.ShapeDtypeStruct((rows, SSM_FLAT), F32),
                   jax.ShapeDtypeStruct((rows, SSM_FLAT), F32)],
        scratch_shapes=_s5_weight_scratch(),
        compiler_params=_cparams(0, 48),
        name="s5_sample",
    )(u, h0_re, h0_im, bt, ct, abar_re, abar_im, d_skip)


def _split3(x):
    hi = x.astype(BF16)
    r1 = x - hi.astype(F32)
    mid = r1.astype(BF16)
    lo = (r1 - mid.astype(F32)).astype(BF16)
    return hi, mid, lo


def _tri(n, lower):
    r = lax.broadcasted_iota(jnp.int32, (n, n), 0)
    c = lax.broadcasted_iota(jnp.int32, (n, n), 1)
    return (r >= c) if lower else (r <= c)


def _gate_columns(gcol, bias_row):
    n = gcol.shape[0]
    g = gcol + bias_row
    fl = _log_sigmoid(g[:, HEADS:])
    tril = jnp.where(_tri(n, True), 1.0, 0.0).astype(BF16)
    b = sum(jnp.dot(tril, p, preferred_element_type=F32) for p in _split3(fl))
    return g[:, :HEADS], b


def _gate_rows(grow, bias_col):
    n = grow.shape[1]
    g = grow + bias_col
    fl = _log_sigmoid(g[HEADS:, :])
    triu = jnp.where(_tri(n, False), 1.0, 0.0).astype(BF16)
    b = sum(jnp.dot(p, triu, preferred_element_type=F32) for p in _split3(fl))
    return g[:HEADS, :], b


MLSTM_GROUP = 4


def _eye_bf16(n):
    return jnp.where(_tri(n, True) & _tri(n, False), 1.0, 0.0).astype(BF16)


def _mlstm_state_update(k_bf, v, ig_col, b_col, c_old, n_old, m_old, k_t=None, gate_rows=None):
    n_rows = k_bf.shape[0]
    k_scale = DK ** -0.5
    b_last = b_col[n_rows - 1:n_rows, :]
    e_col = b_last - b_col + ig_col
    m_new = jnp.maximum(b_last + m_old, jnp.max(e_col, axis=0, keepdims=True))
    w_end = jnp.exp(e_col - m_new) * k_scale
    carry = jnp.exp(b_last + m_old - m_new)
    if k_t is None:
        k_t = lax.dot_general(_eye_bf16(DK), k_bf, NT_DIMS, preferred_element_type=F32)
    if gate_rows is None:
        kv = jnp.dot(k_t.astype(BF16), (v.astype(F32) * w_end).astype(BF16), preferred_element_type=F32)
    else:
        ig_row, b_row = gate_rows
        w_end_row = jnp.exp(b_last - b_row + ig_row - m_new) * k_scale
        kv = jnp.dot((k_t * w_end_row).astype(BF16), v, preferred_element_type=F32)
    c_new = carry * c_old + kv
    n_new = carry * n_old + jnp.sum(k_bf.astype(F32) * w_end, axis=0, keepdims=True)
    return c_new, n_new, m_new


def _mlstm_decay(ig_row, b_row, b_col, m_old):
    n_rows = b_col.shape[0]
    dmat = jnp.where(_tri(n_rows, True), b_col + (ig_row - b_row), -jnp.inf)
    inter = b_col + m_old
    m_t = jnp.maximum(inter, jnp.max(dmat, axis=-1, keepdims=True))
    return jnp.exp(dmat - m_t), jnp.exp(inter - m_t), m_t


def _head_norm_gate(h, o, gain):
    mu = jnp.mean(h, axis=-1, keepdims=True)
    hc = h - mu
    var = jnp.mean(hc * hc, axis=-1, keepdims=True)
    return _sigmoid(o) * (hc * lax.rsqrt(var + EPS) * gain)


def _mlstm_prompt_kernel(batch, q_ref, k_ref, v_ref, o_ref, gc_ref, *rest):
    gr_refs = rest[:batch]
    rest = rest[batch:]
    km_ref, vm_ref, gcm_ref, brow_ref, bcol_ref, gain_ref = rest[:6]
    cast_in = rest[6:6 + N_CAST_IN]
    y_ref, cf_ref, nf_ref, mf_ref = rest[6 + N_CAST_IN:10 + N_CAST_IN]
    cast_out = rest[10 + N_CAST_IN:]
    k_scale = DK ** -0.5

    _weight_cast_step(cast_in, cast_out)

    @pl.when(pl.program_id(0) == 0)
    def _():
        ig_c, b_c = _gate_columns(gcm_ref[...], brow_ref[...])
        for h in range(HEADS):
            hs = slice(h * DK, (h + 1) * DK)
            c_new, n_new, m_new = _mlstm_state_update(
                km_ref[:, hs], vm_ref[:, hs], ig_c[:, h:h + 1], b_c[:, h:h + 1],
                jnp.zeros((DK, DV), F32), jnp.zeros((1, DK), F32), jnp.zeros((1, 1), F32))
            for b in range(batch):
                cf_ref[b, h] = c_new
                nf_ref[b, h] = n_new
                mf_ref[b, h:h + 1, :] = jnp.broadcast_to(m_new, (1, LANE_TILE))

    eye = _eye_bf16(DK)
    for b0 in range(0, batch, MLSTM_GROUP):
        gates = {}
        for b in range(b0, b0 + MLSTM_GROUP):
            gates[b] = _gate_columns(gc_ref[b], brow_ref[...]) + _gate_rows(gr_refs[b][...], bcol_ref[...])
        chains = [(b, h) for b in range(b0, b0 + MLSTM_GROUP) for h in range(HEADS)]

        ops = []
        for b, h in chains:
            hs = slice(h * DK, (h + 1) * DK)
            q = q_ref[b, :, hs]
            k_bf = k_ref[b, :, hs]
            c_old = cf_ref[b, h]
            ops.append(dict(
                q=q, k_bf=k_bf, v=v_ref[b, :, hs], c_old=c_old, n_old=nf_ref[b, h], m_old=mf_ref[b, h:h + 1, 0:1],
                s_raw=lax.dot_general(q, k_bf, NT_DIMS, preferred_element_type=F32),
                qc=jnp.dot(q, c_old.astype(BF16), preferred_element_type=F32),
                k_t=lax.dot_general(eye, k_bf, NT_DIMS, preferred_element_type=F32)))

        for (b, h), c in zip(chains, ops):
            ig_c, b_c, ig_r, b_r = gates[b]
            c["w_intra"], c["w_inter"], c["m_t"] = _mlstm_decay(ig_r[h:h + 1, :], b_r[h:h + 1, :], b_c[:, h:h + 1],
                                                                c["m_old"])

        for (b, h), c in zip(chains, ops):
            hs = slice(h * DK, (h + 1) * DK)
            s = c["s_raw"] * (c["w_intra"] * k_scale)
            num = c["w_inter"] * c["qc"] + jnp.dot(s.astype(BF16), c["v"], preferred_element_type=F32)
            den = (c["w_inter"] * jnp.sum(c["q"].astype(F32) * c["n_old"], axis=-1, keepdims=True)
                   + jnp.sum(s, axis=-1, keepdims=True))
            hid = num / jnp.maximum(jnp.abs(den), jnp.exp(-c["m_t"]))
            y_ref[b, :, hs] = _head_norm_gate(hid, o_ref[b, :, hs], gain_ref[:, hs]).astype(y_ref.dtype)

        for (b, h), c in zip(chains, ops):
            ig_c, b_c, ig_r, b_r = gates[b]
            c_new, n_new, m_new = _mlstm_state_update(c["k_bf"], c["v"], ig_c[:, h:h + 1], b_c[:, h:h + 1],
                                                      c["c_old"], c["n_old"], c["m_old"], k_t=c["k_t"],
                                                      gate_rows=(ig_r[h:h + 1, :], b_r[h:h + 1, :]))
            cf_ref[b, h] = c_new
            nf_ref[b, h] = n_new
            mf_ref[b, h:h + 1, :] = jnp.broadcast_to(m_new, (1, LANE_TILE))


def _mlstm_prompt(qkv, o_arr, o_col_block, gcol, grow, qkv_meta, gcol_meta, bias_row, bias_col, gain, batch, seq,
                  w_in_t, cast_weights):
    n_c = seq // CHUNK
    width = HEADS * DK
    const2 = lambda c: (0, 0)
    cast_in, cast_out, cast_shapes, cast_args = _weight_cast_specs(n_c, lambda c: c, w_in_t, cast_weights)
    qkv3 = qkv.reshape(batch, seq, 3 * width)
    o3 = o_arr.reshape(batch, seq, o_arr.shape[1])
    gcol3 = gcol.reshape(gcol.shape[0] // seq, seq, 2 * HEADS)
    state0 = lambda c: (0, 0, 0, 0)
    return pl.pallas_call(
        functools.partial(_mlstm_prompt_kernel, batch),
        grid=(n_c,),
        in_specs=[pl.BlockSpec((batch, CHUNK, width), lambda c: (0, c, 0)),
                  pl.BlockSpec((batch, CHUNK, width), lambda c: (0, c, 1)),
                  pl.BlockSpec((batch, CHUNK, width), lambda c: (0, c, 2)),
                  pl.BlockSpec((batch, CHUNK, width), lambda c: (0, c, o_col_block)),
                  pl.BlockSpec((batch, CHUNK, 2 * HEADS), lambda c: (0, c, 0))]
                 + [pl.BlockSpec((2 * HEADS, CHUNK), functools.partial(lambda b, c: (0, b * n_c + c), b))
                    for b in range(batch)]
                 + [pl.BlockSpec((N_META, width), lambda c: (0, 1)),
                    pl.BlockSpec((N_META, width), lambda c: (0, 2)),
                    pl.BlockSpec((N_META, 2 * HEADS), const2),
                    pl.BlockSpec((1, 2 * HEADS), const2),
                    pl.BlockSpec((2 * HEADS, 1), const2),
                    pl.BlockSpec((1, width), const2)] + cast_in,
        out_specs=[pl.BlockSpec((batch, CHUNK, width), lambda c: (0, c, 0)),
                   pl.BlockSpec((batch, HEADS, DK, DV), state0),
                   pl.BlockSpec((batch, HEADS, 1, DK), state0),
                   pl.BlockSpec((batch, HEADS, LANE_TILE), lambda c: (0, 0, 0))] + cast_out,
        out_shape=[jax.ShapeDtypeStruct((batch, seq, width), BF16),
                   jax.ShapeDtypeStruct((batch, HEADS, DK, DV), F32),
                   jax.ShapeDtypeStruct((batch, HEADS, 1, DK), F32),
                   jax.ShapeDtypeStruct((batch, HEADS, LANE_TILE), F32)] + cast_shapes,
        compiler_params=_cparams(1, 52),
        name="mlstm_prompt",
    )(qkv3, qkv3, qkv3, o3, gcol3, *([grow] * batch), qkv_meta, qkv_meta, gcol_meta, bias_row, bias_col, gain,
      *cast_args)


MLSTM_SB = 8


def _mlstm_sample_kernel(qkv_ref, o_ref, g_ref, c_ref, n_ref, m_ref, brow_ref, gain_ref,
                         y_ref, co_ref, no_ref, mo_ref):
    sb = MLSTM_SB
    k_scale = DK ** -0.5
    eye = _eye_bf16(DK)
    width = HEADS * DK
    pad = 2 * sb
    row_id = lax.broadcasted_iota(jnp.int32, (pad, DV), 0)

    g_all = g_ref[...] + brow_ref[...]
    ig_all = g_all[:, :HEADS]
    fl_all = _log_sigmoid(g_all[:, HEADS:])
    m_old_all = m_ref[...]
    m_new_all = jnp.maximum(fl_all + m_old_all, ig_all)
    w_in_all = jnp.exp(ig_all - m_new_all)
    w_ca_all = jnp.exp(fl_all + m_old_all - m_new_all)
    floor_all = jnp.exp(-m_new_all)
    mo_ref[...] = m_new_all

    for h in range(HEADS):
        hs = slice(h * DK, (h + 1) * DK)
        qk_rows = jnp.concatenate([qkv_ref[:, hs], qkv_ref[:, width + h * DK:width + (h + 1) * DK] * k_scale],
                                  axis=0).astype(BF16)
        qk_cols = lax.dot_general(eye, qk_rows, NT_DIMS, preferred_element_type=F32).astype(BF16)
        for i in range(sb):
            w_in = w_in_all[i:i + 1, h:h + 1]
            w_ca = w_ca_all[i:i + 1, h:h + 1]
            q = qk_rows[i:i + 1, :]
            qf = q.astype(F32)
            k = qk_rows[sb + i:sb + i + 1, :].astype(F32)
            v = qkv_ref[i:i + 1, 2 * width + h * DK:2 * width + (h + 1) * DK].astype(BF16).astype(F32)
            c_old = c_ref[i, h]
            n_old = n_ref[i, h]
            qc = jnp.dot(jnp.broadcast_to(q, (pad, DK)), c_old.astype(BF16), preferred_element_type=F32)[0:1, :]
            v_rows = jnp.where(row_id == sb + i, jnp.broadcast_to(v * w_in, (pad, DV)), 0.0).astype(BF16)
            kv = jnp.dot(qk_cols, v_rows, preferred_element_type=F32)
            s = jnp.sum(qf * k, axis=-1, keepdims=True) * w_in
            num = w_ca * qc + s * v
            den = w_ca * jnp.sum(qf * n_old, axis=-1, keepdims=True) + s
            hid = num / jnp.maximum(jnp.abs(den), floor_all[i:i + 1, h:h + 1])
            co_ref[i, h] = w_ca * c_old + kv
            no_ref[i, h] = w_ca * n_old + k * w_in
            y_ref[i:i + 1, hs] = _head_norm_gate(hid, o_ref[i:i + 1, hs], gain_ref[:, hs])


def _mlstm_sample(qkv, o_arr, gcol, c0, n0, m0, bias_row, gain):
    nb = qkv.shape[0]
    width = HEADS * DK
    sb = MLSTM_SB
    const2 = lambda i: (0, 0)
    return pl.pallas_call(
        _mlstm_sample_kernel,
        grid=(nb // sb,),
        in_specs=[pl.BlockSpec((sb, 3 * width), lambda i: (i, 0)),
                  pl.BlockSpec((sb, width), lambda i: (i, 0)),
                  pl.BlockSpec((sb, 2 * HEADS), lambda i: (i, 0)),
                  pl.BlockSpec((sb, HEADS, DK, DV), lambda i: (i, 0, 0, 0)),
                  pl.BlockSpec((sb, HEADS, 1, DK), lambda i: (i, 0, 0, 0)),
                  pl.BlockSpec((sb, HEADS), lambda i: (i, 0)),
                  pl.BlockSpec((1, 2 * HEADS), const2),
                  pl.BlockSpec((1, width), const2)],
        out_specs=[pl.BlockSpec((sb, width), lambda i: (i, 0)),
                   pl.BlockSpec((sb, HEADS, DK, DV), lambda i: (i, 0, 0, 0)),
                   pl.BlockSpec((sb, HEADS, 1, DK), lambda i: (i, 0, 0, 0)),
                   pl.BlockSpec((sb, HEADS), lambda i: (i, 0))],
        out_shape=[jax.ShapeDtypeStruct((nb, width), F32),
                   jax.ShapeDtypeStruct((nb, HEADS, DK, DV), F32),
                   jax.ShapeDtypeStruct((nb, HEADS, 1, DK), F32),
                   jax.ShapeDtypeStruct((nb, HEADS), F32)],
        compiler_params=_cparams(1, 48),
        name="mlstm_sample",
    )(qkv, o_arr, gcol, c0, n0, m0, bias_row, gain)


def _merge_rows(ya, yb, x, wgab_ref, wglu_ref, bglu_ref, wa_ref, wb_ref, wout_ref, g1_ref, b1_ref):
    xb = x.astype(BF16)
    g = _gelu_tanh(ya)
    z = jnp.dot(g.astype(BF16), wglu_ref[...], preferred_element_type=F32) + bglu_ref[...]
    out_a = g * _sigmoid(z)
    up_a = jnp.dot(out_a.astype(BF16), wa_ref[...], preferred_element_type=F32)
    g_a = lax.dot_general(xb, wgab_ref[0:D_MODEL, :], NT_DIMS, preferred_element_type=F32)
    mix = _sigmoid(g_a) * up_a
    up_b = jnp.dot(yb, wb_ref[...], preferred_element_type=F32)
    g_b = lax.dot_general(xb, wgab_ref[D_MODEL:2 * D_MODEL, :], NT_DIMS, preferred_element_type=F32)
    mix = mix + _sigmoid(g_b) * up_b
    mo = jnp.dot(mix.astype(BF16), wout_ref[...], preferred_element_type=F32)
    return _layernorm_rows(ALPHA * x + mo, g1_ref[...], b1_ref[...])


def _merge_kernel(ya_ref, yb_ref, x_ref, yas_ref, ybs_ref, xs_ref, *rest):
    weights, (o_ref, os_ref, ob_ref, osb_ref) = rest[:-4], rest[-4:]
    x1 = _merge_rows(ya_ref[...], yb_ref[...], x_ref[...], *weights)
    o_ref[...] = x1
    ob_ref[...] = x1.astype(BF16)

    @pl.when(pl.program_id(0) == pl.num_programs(0) - 1)
    def _():
        x1s = _merge_rows(yas_ref[...], ybs_ref[...].astype(BF16), xs_ref[...], *weights)
        os_ref[...] = x1s
        osb_ref[...] = x1s.astype(BF16)


def _merge(ya, yb, x, ya_s, yb_s, x_s, wt_gab, w_glu, b_glu, w_a_up, w_b_up, w_out, ln_g, ln_b, tm):
    r = x.shape[0]
    rs = x_s.shape[0]
    const2 = lambda i: (0, 0)
    resident = functools.partial(pl.BlockSpec, index_map=const2, pipeline_mode=pl.Buffered(1))
    return pl.pallas_call(
        _merge_kernel,
        grid=(r // tm,),
        in_specs=[pl.BlockSpec((tm, SSM_WIDTH), lambda i: (i, 0)),
                  pl.BlockSpec((tm, HEADS * DV), lambda i: (i, 0)),
                  pl.BlockSpec((tm, D_MODEL), lambda i: (i, 0)),
                  resident((rs, SSM_WIDTH)),
                  resident((rs, HEADS * DV)),
                  resident((rs, D_MODEL)),
                  resident((2 * D_MODEL, D_MODEL)),
                  resident((SSM_WIDTH, SSM_WIDTH)),
                  resident((1, SSM_WIDTH)),
                  resident((SSM_WIDTH, D_MODEL)),
                  resident((HEADS * DV, D_MODEL)),
                  resident((D_MODEL, D_MODEL)),
                  resident((1, D_MODEL)),
                  resident((1, D_MODEL))],
        out_specs=[pl.BlockSpec((tm, D_MODEL), lambda i: (i, 0)),
                   pl.BlockSpec((rs, D_MODEL), const2),
                   pl.BlockSpec((tm, D_MODEL), lambda i: (i, 0)),
                   pl.BlockSpec((rs, D_MODEL), const2)],
        out_shape=[jax.ShapeDtypeStruct((r, D_MODEL), F32), jax.ShapeDtypeStruct((rs, D_MODEL), F32),
                   jax.ShapeDtypeStruct((r, D_MODEL), BF16), jax.ShapeDtypeStruct((rs, D_MODEL), BF16)],
        compiler_params=_cparams(1, 59),
        name="merge_ln1",
    )(ya, yb, x, ya_s, yb_s, x_s, wt_gab, w_glu, b_glu, w_a_up, w_b_up, w_out, ln_g, ln_b)


def _swiglu(xb, wg, wu):
    hg = jnp.dot(xb, wg, preferred_element_type=F32)
    hu = jnp.dot(xb, wu, preferred_element_type=F32)
    return ((hg * _sigmoid(hg)) * hu).astype(BF16)


FFN_UP_PIECE = 512


def _ffn_up_kernel(x_ref, xs_ref, wg_ref, wu_ref, wd_ref, h_ref, hs_ref, wdb_ref, wgb_s, wub_s):
    wgb_s[...] = wg_ref[...].astype(BF16)
    wub_s[...] = wu_ref[...].astype(BF16)

    for r0 in range(0, x_ref.shape[0], FFN_UP_PIECE):
        rows = slice(r0, r0 + FFN_UP_PIECE)
        h_ref[rows, :] = _swiglu(x_ref[rows, :], wgb_s[...], wub_s[...])
    wdb_ref[...] = wd_ref[...].astype(BF16)

    @pl.when(pl.program_id(0) == pl.num_programs(0) - 1)
    def _():
        hs_ref[...] = _swiglu(xs_ref[...], wgb_s[...], wub_s[...])


def _ffn_up(xb, xsb, w_gate3, w_up3, w_down3, tm, tf):
    r = xb.shape[0]
    rs = xsb.shape[0]
    n_f, n_i = D_FF // tf, r // tm
    wd_rows = D_FF // (n_f * n_i)
    hs_block = lambda i, f: jnp.where(i == n_i - 1, f, 0)
    return pl.pallas_call(
        _ffn_up_kernel,
        grid=(n_i, n_f),
        in_specs=[pl.BlockSpec((tm, D_MODEL), lambda i, f: (i, 0)),
                  pl.BlockSpec((rs, D_MODEL), lambda i, f: (0, 0)),
                  pl.BlockSpec((None, D_MODEL, tf), lambda i, f: (0, 0, f)),
                  pl.BlockSpec((None, D_MODEL, tf), lambda i, f: (0, 0, f)),
                  pl.BlockSpec((None, wd_rows, D_MODEL), lambda i, f: (0, i * n_f + f, 0))],
        out_specs=[pl.BlockSpec((tm, tf), lambda i, f: (i, f)),
                   pl.BlockSpec((rs, tf), lambda i, f: (0, hs_block(i, f))),
                   pl.BlockSpec((wd_rows, D_MODEL), lambda i, f: (i * n_f + f, 0))],
        out_shape=[jax.ShapeDtypeStruct((r, D_FF), BF16), jax.ShapeDtypeStruct((rs, D_FF), BF16),
                   jax.ShapeDtypeStruct((D_FF, D_MODEL), BF16)],
        scratch_shapes=[pltpu.VMEM((D_MODEL, tf), BF16), pltpu.VMEM((D_MODEL, tf), BF16)],
        compiler_params=_cparams(2, 58),
        name="ffn_up",
    )(xb, xsb, w_gate3, w_up3, w_down3)


FFN_DOWN_PIECE = 256


def _ffn_down_rows(h, x, wd_ref, g2_ref, b2_ref):
    return _layernorm_rows(ALPHA * x + jnp.dot(h, wd_ref[...], preferred_element_type=F32), g2_ref[...], b2_ref[...])


def _ffn_down_kernel(h_ref, x_ref, hs_ref, xs_ref, wd_ref, g2_ref, b2_ref, o_ref, os_ref):
    for r0 in range(0, h_ref.shape[0], FFN_DOWN_PIECE):
        rows = slice(r0, r0 + FFN_DOWN_PIECE)
        o_ref[rows, :] = _ffn_down_rows(h_ref[rows, :], x_ref[rows, :], wd_ref, g2_ref, b2_ref)

    @pl.when(pl.program_id(0) == pl.num_programs(0) - 1)
    def _():
        os_ref[...] = _ffn_down_rows(hs_ref[...], xs_ref[...], wd_ref, g2_ref, b2_ref)


def _ffn_down(h, x, h_s, x_s, wd_b, ln_g, ln_b, tm):
    r = x.shape[0]
    rs = x_s.shape[0]
    const2 = lambda i: (0, 0)
    resident = functools.partial(pl.BlockSpec, index_map=const2, pipeline_mode=pl.Buffered(1))
    return pl.pallas_call(
        _ffn_down_kernel,
        grid=(r // tm,),
        in_specs=[pl.BlockSpec((tm, D_FF), lambda i: (i, 0)),
                  pl.BlockSpec((tm, D_MODEL), lambda i: (i, 0)),
                  resident((rs, D_FF)),
                  resident((rs, D_MODEL)),
                  resident((D_FF, D_MODEL)),
                  resident((1, D_MODEL)),
                  resident((1, D_MODEL))],
        out_specs=[pl.BlockSpec((tm, D_MODEL), lambda i: (i, 0)),
                   pl.BlockSpec((rs, D_MODEL), const2)],
        out_shape=[jax.ShapeDtypeStruct((r, D_MODEL), F32), jax.ShapeDtypeStruct((rs, D_MODEL), F32)],
        compiler_params=_cparams(1, 59),
        name="ffn_down_ln2",
    )(h, x, h_s, x_s, wd_b, ln_g, ln_b)


def kernel(x_prompt, x_sample, state_ssm_re, state_ssm_im, state_mlstm_c, state_mlstm_n, state_mlstm_m,
           meta_tokens, w_in, b_if, ssm_a_re, ssm_a_im, ssm_log_dt, ssm_b_re, ssm_b_im, ssm_c_re, ssm_c_im,
           ssm_d, w_glu, b_glu, w_a_up, mh_gain, w_b_up, w_out, ln1_g, ln1_b, w_gate, w_up, w_down,
           ln2_g, ln2_b):
    batch, seq, _ = x_prompt.shape
    nb = x_sample.shape[0]
    width = HEADS * DK

    bias_row = b_if[0].reshape(1, 2 * HEADS)
    bias_col = b_if[0].reshape(2 * HEADS, 1)
    gain = mh_gain[0].reshape(1, width)
    d_skip = ssm_d[0].reshape(1, SSM_WIDTH)
    b_glu_r = b_glu[0].reshape(1, SSM_WIDTH)
    ln1 = (ln1_g[0].reshape(1, D_MODEL), ln1_b[0].reshape(1, D_MODEL))
    ln2 = (ln2_g[0].reshape(1, D_MODEL), ln2_b[0].reshape(1, D_MODEL))
    abar_re, abar_im, bt, ct = _s5_discretise(
        ssm_a_re[0], ssm_a_im[0], ssm_log_dt[0], ssm_b_re[0], ssm_b_im[0], ssm_c_re[0], ssm_c_im[0])

    xp = x_prompt.reshape(batch * seq, D_MODEL)
    xs = x_sample.reshape(nb, D_MODEL)
    xsm = jnp.concatenate([xs, meta_tokens], axis=0)
    w_in_t = jnp.swapaxes(w_in, 1, 2)
    uo_blocks, qkv_blocks = (0, 4, 2), (1, 1, 3)

    uo_p, uo_s, gcol_p, grow_p, gcol_s = _proj(xp, xsm, w_in_t, uo_blocks, F32, F32, 1024, "proj_uo", with_gates=True)
    qkv_p, qkv_s = _proj(xp, xsm, w_in_t, qkv_blocks, BF16, F32, 1024, "proj_qkv")

    u_meta = uo_s[nb:nb + N_META, :SSM_WIDTH]
    qkv_meta = qkv_s[nb:nb + N_META].astype(BF16)
    gcol_meta = gcol_s[nb:nb + N_META]

    ya_p, pf_re, pf_im = _s5_prompt(uo_p, 0, u_meta, abar_re, abar_im, bt, ct, d_skip, batch, seq)
    yb_p, pc, pn, pm, wt_gab, w_out_b, w_a_b, w_b_b, w_glu_b = _mlstm_prompt(
        qkv_p, uo_p, 1, gcol_p, grow_p, qkv_meta, gcol_meta, bias_row, bias_col, gain, batch, seq,
        w_in_t, (w_out, w_a_up, w_b_up, w_glu))
    ya_s, sf_re, sf_im = _s5_sample(uo_s[:nb, :SSM_WIDTH], state_ssm_re[0].reshape(nb, SSM_FLAT),
                                    state_ssm_im[0].reshape(nb, SSM_FLAT), abar_re, abar_im, bt, ct, d_skip)
    yb_s, sc, sn, sm = _mlstm_sample(qkv_s[:nb], uo_s[:nb, SSM_WIDTH:], gcol_s[:nb], state_mlstm_c[0],
                                     state_mlstm_n[0].reshape(nb, HEADS, 1, DK), state_mlstm_m[0], bias_row, gain)

    x1_p, x1_s, x1b_p, x1b_s = _merge(ya_p, yb_p.reshape(batch * seq, width), xp, ya_s, yb_s, xs,
                                      wt_gab, w_glu_b, b_glu_r, w_a_b, w_b_b, w_out_b, *ln1, tm=256)
    h_p, h_s, w_down_b = _ffn_up(x1b_p, x1b_s, w_gate, w_up, w_down, tm=4096, tf=256)
    y_p, y_s = _ffn_down(h_p, x1_p, h_s, x1_s, w_down_b, *ln2, tm=512)

    return (y_p.reshape(batch, seq, D_MODEL),
            y_s.reshape(nb, 1, D_MODEL),
            pf_re.reshape(1, batch, SSM_GROUPS, SSM_STATE),
            pf_im.reshape(1, batch, SSM_GROUPS, SSM_STATE),
            pc.reshape(1, batch, HEADS, DK, DV),
            pn.reshape(1, batch, HEADS, DK),
            pm[:, :, 0].reshape(1, batch, HEADS),
            sf_re.reshape(1, nb, SSM_GROUPS, SSM_STATE),
            sf_im.reshape(1, nb, SSM_GROUPS, SSM_STATE),
            sc.reshape(1, nb, HEADS, DK, DV),
            sn.reshape(1, nb, HEADS, DK),
            sm.reshape(1, nb, HEADS))
```

```python
import functools

import jax
import jax.numpy as jnp
import numpy as np
from jax import lax
from jax.experimental import pallas as pl
from jax.experimental.pallas import tpu as pltpu

F32 = jnp.float32
BF16 = jnp.bfloat16

D_MODEL = 2048
N_META = 16
SSM_WIDTH = 1024
SSM_GROUP = 16
SSM_GROUPS = 64
SSM_STATE = 64
SSM_FLAT = SSM_GROUPS * SSM_STATE
SSM_JBLK = 4
SSM_BLK_IN = SSM_WIDTH // SSM_JBLK
SSM_BLK_STATE = SSM_FLAT // SSM_JBLK
LANE_TILE = 128
HEADS = 4
DK = 256
DV = 256
CHUNK = 128
D_FF = 5632
EPS = 1e-5
ALPHA = 2.0 ** 0.25

S5_TC = 256
S5_NSEG = 8
S5_SEG = S5_TC // S5_NSEG
S5_LW = 512
S5_CPS = 2

V7X_SCOPED_VMEM_MAX_BYTES = 60000 * 1024


def _cparams(n_axes, vmem_mb):
    return pltpu.CompilerParams(
        dimension_semantics=("arbitrary",) * n_axes,
        vmem_limit_bytes=min(vmem_mb * 1024 * 1024, V7X_SCOPED_VMEM_MAX_BYTES),
    )


def _sigmoid(x):
    return 1.0 / (1.0 + jnp.exp(-x))


def _log_sigmoid(x):
    return jnp.minimum(x, 0.0) - jnp.log(1.0 + jnp.exp(-jnp.abs(x)))


def _gelu_tanh(x):
    c = 0.7978845608028654
    return 0.5 * x * (1.0 + jnp.tanh(c * (x + 0.044715 * (x * x * x))))


def _layernorm_rows(x, g, b):
    mu = jnp.mean(x, axis=-1, keepdims=True)
    xc = x - mu
    var = jnp.mean(xc * xc, axis=-1, keepdims=True)
    return xc * lax.rsqrt(var + EPS) * g + b


PROJ_TN = 1024
GATE_ROWS = 2 * HEADS
TAIL_ROW0 = SSM_WIDTH + 4 * HEADS * DK
NT_DIMS = (((1,), (1,)), ((), ()))


def _proj_kernel(with_gates, x_ref, xs_ref, wt_ref, *rest):
    if with_gates:
        wg_ref, o_ref, os_ref, g_ref, gt_ref, gs_ref, wb_s = rest
    else:
        o_ref, os_ref, wb_s = rest
    i = pl.program_id(1)

    @pl.when(i == 0)
    def _():
        wb_s[...] = wt_ref[...].astype(BF16)

    xb = x_ref[...].astype(BF16)
    o_ref[...] = lax.dot_general(xb, wb_s[...], NT_DIMS, preferred_element_type=F32).astype(o_ref.dtype)
    first_block = pl.program_id(0) == 0
    if with_gates:
        @pl.when(first_block)
        def _():
            wg = wg_ref[...].astype(BF16)
            g_ref[...] = lax.dot_general(xb, wg, NT_DIMS, preferred_element_type=F32)
            gt_ref[...] = lax.dot_general(wg, xb, NT_DIMS, preferred_element_type=F32)

    @pl.when(i == pl.num_programs(1) - 1)
    def _():
        xsb = xs_ref[...].astype(BF16)
        os_ref[...] = lax.dot_general(xsb, wb_s[...], NT_DIMS, preferred_element_type=F32).astype(os_ref.dtype)
        if with_gates:
            @pl.when(first_block)
            def _():
                gs_ref[...] = lax.dot_general(xsb, wg_ref[...].astype(BF16), NT_DIMS, preferred_element_type=F32)


def _proj(x, xs, wt3, blocks, out_dtype, small_dtype, tm, name, with_gates=False):
    r, k = x.shape
    rs = xs.shape[0]
    first, step, count = blocks
    tn = PROJ_TN
    nm = r // tm
    in_specs = [pl.BlockSpec((tm, k), lambda j, i: (i, 0)),
                pl.BlockSpec((rs, k), lambda j, i: (0, 0)),
                pl.BlockSpec((None, tn, k), lambda j, i: (0, first + step * j, 0))]
    out_specs = [pl.BlockSpec((tm, tn), lambda j, i: (i, j)),
                 pl.BlockSpec((rs, tn), lambda j, i: (0, j))]
    out_shape = [jax.ShapeDtypeStruct((r, count * tn), out_dtype),
                 jax.ShapeDtypeStruct((rs, count * tn), small_dtype)]
    args = (x, xs, wt3)
    if with_gates:
        ng = GATE_ROWS
        in_specs += [pl.BlockSpec((None, ng, k), lambda j, i: (0, TAIL_ROW0 // ng, 0))]
        gate_block = lambda j, i: jnp.where(j == 0, i, nm - 1)
        out_specs += [pl.BlockSpec((tm, ng), lambda j, i: (gate_block(j, i), 0)),
                      pl.BlockSpec((ng, tm), lambda j, i: (0, gate_block(j, i))),
                      pl.BlockSpec((rs, ng), lambda j, i: (0, 0))]
        out_shape += [jax.ShapeDtypeStruct((r, ng), F32), jax.ShapeDtypeStruct((ng, r), F32),
                      jax.ShapeDtypeStruct((rs, ng), F32)]
        args = (x, xs, wt3, wt3)
    return pl.pallas_call(
        functools.partial(_proj_kernel, with_gates),
        grid=(count, nm),
        in_specs=in_specs,
        out_specs=out_specs,
        out_shape=out_shape,
        scratch_shapes=[pltpu.VMEM((tn, k), BF16)],
        compiler_params=_cparams(2, 56),
        name=name,
    )(*args)


def _s5_discretise(a_re, a_im, log_dt, b_re, b_im, c_re, c_im):
    dt = jnp.exp(log_dt)
    e = jnp.exp(a_re * dt)
    abar_re = e * jnp.cos(a_im * dt)
    abar_im = e * jnp.sin(a_im * dt)
    nr = abar_re - 1.0
    ni = abar_im
    den = a_re * a_re + a_im * a_im
    coef_re = (nr * a_re + ni * a_im) / den
    coef_im = (ni * a_re - nr * a_im) / den
    bb_re = coef_re[..., None] * b_re - coef_im[..., None] * b_im
    bb_im = coef_re[..., None] * b_im + coef_im[..., None] * b_re
    gpb = SSM_GROUPS // SSM_JBLK

    def b_rows(bb):
        t = bb.reshape(SSM_JBLK, gpb, SSM_STATE, SSM_GROUP).transpose(0, 1, 3, 2)
        return t.reshape(SSM_JBLK, SSM_BLK_IN, SSM_STATE)

    def c_rows(cc):
        t = cc.reshape(SSM_JBLK, gpb, SSM_GROUP, SSM_STATE).transpose(0, 1, 3, 2)
        return t.reshape(SSM_JBLK, SSM_BLK_STATE, SSM_GROUP)

    b_rep, c_rep = LANE_TILE // SSM_STATE, LANE_TILE // SSM_GROUP
    bt = jnp.concatenate([jnp.tile(b_rows(bb_re), (1, 1, b_rep)), jnp.tile(b_rows(bb_im), (1, 1, b_rep))], axis=-1)
    ct = jnp.concatenate([jnp.tile(c_rows(c_re), (1, 1, c_rep)), jnp.tile(c_rows(-c_im), (1, 1, c_rep))], axis=-1)
    return abar_re.reshape(1, SSM_FLAT), abar_im.reshape(1, SSM_FLAT), bt.astype(BF16), ct.astype(BF16)


def _s5_expand_weights(bt_ref, ct_ref, bw_s, cwr_s, cwi_s):
    log2 = lambda n: n.bit_length() - 1
    rb = lax.broadcasted_iota(jnp.int32, (SSM_BLK_IN, 2 * SSM_BLK_STATE), 0)
    cb = lax.broadcasted_iota(jnp.int32, (SSM_BLK_IN, 2 * SSM_BLK_STATE), 1)
    bmask = (rb >> log2(SSM_GROUP)) == ((cb & (SSM_BLK_STATE - 1)) >> log2(SSM_STATE))
    rc = lax.broadcasted_iota(jnp.int32, (SSM_BLK_STATE, SSM_BLK_IN), 0)
    cc = lax.broadcasted_iota(jnp.int32, (SSM_BLK_STATE, SSM_BLK_IN), 1)
    cmask = (rc >> log2(SSM_STATE)) == (cc >> log2(SSM_GROUP))
    b_tiles, c_tiles = SSM_BLK_STATE // LANE_TILE, SSM_BLK_IN // LANE_TILE
    for j in range(SSM_JBLK):
        bt = bt_ref[j].astype(F32)
        wide = jnp.concatenate([bt[:, :LANE_TILE]] * b_tiles + [bt[:, LANE_TILE:]] * b_tiles, axis=1)
        bw_s[j] = jnp.where(bmask, wide, 0.0).astype(BF16)
        ct = ct_ref[j].astype(F32)
        cwr_s[j] = jnp.where(cmask, jnp.concatenate([ct[:, :LANE_TILE]] * c_tiles, axis=1), 0.0).astype(BF16)
        cwi_s[j] = jnp.where(cmask, jnp.concatenate([ct[:, LANE_TILE:]] * c_tiles, axis=1), 0.0).astype(BF16)


def _cmul_add(ar, ai, hr, hi, xr, xi):
    return ar * hr - ai * hi + xr, ar * hi + ai * hr + xi


def _s5_project_in(u, b_ref, hre_ref, him_ref, rows):
    for j in range(SSM_JBLK):
        state = slice(j * SSM_BLK_STATE, (j + 1) * SSM_BLK_STATE)
        bu = jnp.dot(u[:, j * SSM_BLK_IN:(j + 1) * SSM_BLK_IN], b_ref[j], preferred_element_type=F32)
        hre_ref[0:rows, state] = bu[:, :SSM_BLK_STATE]
        him_ref[0:rows, state] = bu[:, SSM_BLK_STATE:]


def _s5_project_out(hre_ref, him_ref, cre_ref, cimn_ref, rows):
    ys = []
    for j in range(SSM_JBLK):
        state = slice(j * SSM_BLK_STATE, (j + 1) * SSM_BLK_STATE)
        hr = hre_ref[0:rows, state].astype(BF16)
        hi = him_ref[0:rows, state].astype(BF16)
        y = jnp.dot(hr, cre_ref[j], preferred_element_type=F32)
        ys.append(y + jnp.dot(hi, cimn_ref[j], preferred_element_type=F32))
    return jnp.concatenate(ys, axis=1)


def _s5_scan_chunk(hre_ref, him_ref, hsre_ref, hsim_ref, pre_ref, pim_ref, are_ref, aim_ref, cre_s, cim_s):
    full = (S5_NSEG, S5_LW)
    for lg in range(SSM_FLAT // S5_LW):
        lanes = slice(lg * S5_LW, (lg + 1) * S5_LW)
        ar = jnp.broadcast_to(are_ref[:, lanes], full)
        ai = jnp.broadcast_to(aim_ref[:, lanes], full)

        hr = jnp.zeros(full, F32)
        hi = jnp.zeros(full, F32)
        for k in range(S5_SEG):
            rows = slice(k * S5_NSEG, (k + 1) * S5_NSEG)
            hr, hi = _cmul_add(ar, ai, hr, hi, hre_ref[rows, lanes], him_ref[rows, lanes])
            hre_ref[rows, lanes] = hr
            him_ref[rows, lanes] = hi
        er, ei = hr, hi

        asr = pre_ref[S5_TC - 1:S5_TC, lanes]
        asi = pim_ref[S5_TC - 1:S5_TC, lanes]
        hr = cre_s[:, lanes]
        hi = cim_s[:, lanes]
        for j in range(S5_NSEG):
            hsre_ref[j:j + 1, lanes] = hr
            hsim_ref[j:j + 1, lanes] = hi
            hr, hi = _cmul_add(asr, asi, hr, hi, er[j:j + 1, :], ei[j:j + 1, :])
        cre_s[:, lanes] = hr
        cim_s[:, lanes] = hi

        sr = hsre_ref[:, lanes]
        si = hsim_ref[:, lanes]
        for k in range(S5_SEG):
            rows = slice(k * S5_NSEG, (k + 1) * S5_NSEG)
            pr = pre_ref[rows, lanes]
            pi = pim_ref[rows, lanes]
            hre_ref[rows, lanes] = hre_ref[rows, lanes] + (pr * sr - pi * si)
            him_ref[rows, lanes] = him_ref[rows, lanes] + (pr * si + pi * sr)


N_CAST_IN = 6


def _weight_cast_specs(n_steps, step_of, w_in_t, others):
    gab_rows = 2 * D_MODEL // n_steps
    gab_blk0 = TAIL_ROW0 // gab_rows
    gab_next0 = (TAIL_ROW0 + gab_rows) // GATE_ROWS
    rows = [w.shape[1] // n_steps for w in others]
    in_specs = [pl.BlockSpec((None, gab_rows, D_MODEL), lambda *g: (0, gab_blk0 + step_of(*g), 0)),
                pl.BlockSpec((None, GATE_ROWS, D_MODEL),
                             lambda *g: (0, gab_next0 + (gab_rows // GATE_ROWS) * step_of(*g), 0))]
    in_specs += [pl.BlockSpec((None, nr, w.shape[2]), lambda *g: (0, step_of(*g), 0)) for nr, w in zip(rows, others)]
    out_specs = [pl.BlockSpec((gab_rows, D_MODEL), lambda *g: (step_of(*g), 0))]
    out_specs += [pl.BlockSpec((nr, w.shape[2]), lambda *g: (step_of(*g), 0)) for nr, w in zip(rows, others)]
    out_shapes = [jax.ShapeDtypeStruct((2 * D_MODEL, D_MODEL), BF16)]
    out_shapes += [jax.ShapeDtypeStruct(w.shape[1:], BF16) for w in others]
    return in_specs, out_specs, out_shapes, (w_in_t, w_in_t) + tuple(others)


def _weight_cast_step(in_refs, out_refs):
    wga_ref, wgn_ref = in_refs[:2]
    out_refs[0][...] = jnp.concatenate([wga_ref[GATE_ROWS:, :], wgn_ref[...]], axis=0).astype(BF16)
    for src, dst in zip(in_refs[2:], out_refs[1:]):
        dst[...] = src[...].astype(BF16)


def _s5_prompt_kernel(u_ref, um_ref, perm_ref, permt_ref, bt_ref, ct_ref, are_ref, aim_ref, d_ref,
                      y_ref, fre_ref, fim_ref,
                      hre_ref, him_ref, hre2_ref, him2_ref, pre_ref, pim_ref, cre_s, cim_s,
                      hsre_ref, hsim_ref, hsre2_ref, hsim2_ref,
                      b_ref, cre_ref, cimn_ref):
    b_id = pl.program_id(0)
    c_id = pl.program_id(1)
    n_c = pl.num_programs(1)

    @pl.when(jnp.logical_and(b_id == 0, c_id == 0))
    def _():
        _s5_expand_weights(bt_ref, ct_ref, b_ref, cre_ref, cimn_ref)
        ar = jnp.broadcast_to(are_ref[...], (S5_NSEG, SSM_FLAT))
        ai = jnp.broadcast_to(aim_ref[...], (S5_NSEG, SSM_FLAT))
        pre_ref[0:S5_NSEG, :] = ar
        pim_ref[0:S5_NSEG, :] = ai

        def body(i, carry):
            pr, pi = carry
            nr = ar * pr - ai * pi
            ni = ar * pi + ai * pr
            rows = pl.ds(pl.multiple_of(i * S5_NSEG, S5_NSEG), S5_NSEG)
            pre_ref[rows, :] = nr
            pim_ref[rows, :] = ni
            return nr, ni

        lax.fori_loop(1, S5_SEG, body, (ar, ai))

    @pl.when(c_id == 0)
    def _():
        _s5_project_in(um_ref[...].astype(BF16), b_ref, hre_ref, him_ref, N_META)
        ar = are_ref[...]
        ai = aim_ref[...]
        hr = jnp.zeros((1, SSM_FLAT), F32)
        hi = jnp.zeros((1, SSM_FLAT), F32)
        for t in range(N_META):
            hr, hi = _cmul_add(ar, ai, hr, hi, hre_ref[t:t + 1, :], him_ref[t:t + 1, :])
        cre_s[...] = hr
        cim_s[...] = hi

    bufs = ((hre_ref, him_ref, hsre_ref, hsim_ref), (hre2_ref, him2_ref, hsre2_ref, hsim2_ref))
    us = [u_ref[n * S5_TC:(n + 1) * S5_TC, :] for n in range(S5_CPS)]
    for n in range(S5_CPS):
        u_perm = jnp.dot(perm_ref[...], us[n].astype(BF16), preferred_element_type=F32).astype(BF16)
        _s5_project_in(u_perm, b_ref, bufs[n][0], bufs[n][1], S5_TC)
    for n in range(S5_CPS):
        hre_n, him_n, hsre_n, hsim_n = bufs[n]
        _s5_scan_chunk(hre_n, him_n, hsre_n, hsim_n, pre_ref, pim_ref, are_ref, aim_ref, cre_s, cim_s)
        y_perm = _s5_project_out(hre_n, him_n, cre_ref, cimn_ref, S5_TC)
        y = sum(jnp.dot(permt_ref[...], piece, preferred_element_type=F32) for piece in _split3(y_perm))
        y_ref[n * S5_TC:(n + 1) * S5_TC, :] = y + d_ref[...] * us[n]

    @pl.when(c_id == n_c - 1)
    def _():
        fre_ref[0] = cre_s[...]
        fim_ref[0] = cim_s[...]


def _s5_weight_scratch():
    return [pltpu.VMEM((SSM_JBLK, SSM_BLK_IN, 2 * SSM_BLK_STATE), BF16),
            pltpu.VMEM((SSM_JBLK, SSM_BLK_STATE, SSM_BLK_IN), BF16),
            pltpu.VMEM((SSM_JBLK, SSM_BLK_STATE, SSM_BLK_IN), BF16)]


def _s5_prompt(u_arr, u_col_block, u_meta, abar_re, abar_im, bt, ct, d_skip, batch, seq):
    rows_step = S5_CPS * S5_TC
    n_c = seq // rows_step
    const3 = lambda b, c: (0, 0, 0)
    const2 = lambda b, c: (0, 0)
    rows = np.arange(S5_TC)
    perm_np = np.zeros((S5_TC, S5_TC), np.float32)
    perm_np[rows, (rows % S5_NSEG) * S5_SEG + rows // S5_NSEG] = 1.0
    perm = jnp.asarray(perm_np, dtype=BF16)
    permt = jnp.asarray(perm_np.T, dtype=BF16)
    return pl.pallas_call(
        _s5_prompt_kernel,
        grid=(batch, n_c),
        in_specs=[pl.BlockSpec((rows_step, SSM_WIDTH), lambda b, c: (b * n_c + c, u_col_block)),
                  pl.BlockSpec((N_META, SSM_WIDTH), const2),
                  pl.BlockSpec((S5_TC, S5_TC), const2),
                  pl.BlockSpec((S5_TC, S5_TC), const2),
                  pl.BlockSpec((SSM_JBLK, SSM_BLK_IN, 2 * LANE_TILE), const3, pipeline_mode=pl.Buffered(1)),
                  pl.BlockSpec((SSM_JBLK, SSM_BLK_STATE, 2 * LANE_TILE), const3, pipeline_mode=pl.Buffered(1)),
                  pl.BlockSpec((1, SSM_FLAT), const2),
                  pl.BlockSpec((1, SSM_FLAT), const2),
                  pl.BlockSpec((1, SSM_WIDTH), const2)],
        out_specs=[pl.BlockSpec((rows_step, SSM_WIDTH), lambda b, c: (b * n_c + c, 0)),
                   pl.BlockSpec((1, 1, SSM_FLAT), lambda b, c: (b, 0, 0)),
                   pl.BlockSpec((1, 1, SSM_FLAT), lambda b, c: (b, 0, 0))],
        out_shape=[jax.ShapeDtypeStruct((batch * seq, SSM_WIDTH), F32),
                   jax.ShapeDtypeStruct((batch, 1, SSM_FLAT), F32),
                   jax.ShapeDtypeStruct((batch, 1, SSM_FLAT), F32)],
        scratch_shapes=[pltpu.VMEM((S5_TC, SSM_FLAT), F32)] * 4
                       + [pltpu.VMEM((S5_TC, SSM_FLAT), F32)] * 2
                       + [pltpu.VMEM((1, SSM_FLAT), F32)] * 2
                       + [pltpu.VMEM((S5_NSEG, SSM_FLAT), F32)] * 4
                       + _s5_weight_scratch(),
        compiler_params=_cparams(2, 59),
        name="s5_prompt",
    )(u_arr, u_meta, perm, permt, bt, ct, abar_re, abar_im, d_skip)


def _s5_sample_kernel(u_ref, h0re_ref, h0im_ref, bt_ref, ct_ref, are_ref, aim_ref, d_ref,
                      y_ref, hre_ref, him_ref, b_ref, cre_ref, cimn_ref):
    rows = u_ref.shape[0]
    _s5_expand_weights(bt_ref, ct_ref, b_ref, cre_ref, cimn_ref)
    u = u_ref[...]
    _s5_project_in(u.astype(BF16), b_ref, hre_ref, him_ref, rows)
    nr, ni = _cmul_add(are_ref[...], aim_ref[...], h0re_ref[...], h0im_ref[...], hre_ref[...], him_ref[...])
    hre_ref[...] = nr
    him_ref[...] = ni
    y_ref[...] = _s5_project_out(hre_ref, him_ref, cre_ref, cimn_ref, rows) + d_ref[...] * u


def _s5_sample(u, h0_re, h0_im, abar_re, abar_im, bt, ct, d_skip):
    rows = u.shape[0]
    return pl.pallas_call(
        _s5_sample_kernel,
        out_shape=[jax.ShapeDtypeStruct((rows, SSM_WIDTH), F32),
                   jax.ShapeDtypeStruct((rows, SSM_FLAT), F32),
                   jax.ShapeDtypeStruct((rows, SSM_FLAT), F32)],
        scratch_shapes=_s5_weight_scratch(),
        compiler_params=_cparams(0, 48),
        name="s5_sample",
    )(u, h0_re, h0_im, bt, ct, abar_re, abar_im, d_skip)


def _split3(x):
    hi = x.astype(BF16)
    r1 = x - hi.astype(F32)
    mid = r1.astype(BF16)
    lo = (r1 - mid.astype(F32)).astype(BF16)
    return hi, mid, lo


def _tri(n, lower):
    r = lax.broadcasted_iota(jnp.int32, (n, n), 0)
    c = lax.broadcasted_iota(jnp.int32, (n, n), 1)
    return (r >= c) if lower else (r <= c)


def _gate_columns(gcol, bias_row):
    n = gcol.shape[0]
    g = gcol + bias_row
    fl = _log_sigmoid(g[:, HEADS:])
    tril = jnp.where(_tri(n, True), 1.0, 0.0).astype(BF16)
    b = sum(jnp.dot(tril, p, preferred_element_type=F32) for p in _split3(fl))
    return g[:, :HEADS], b


def _gate_rows(grow, bias_col):
    n = grow.shape[1]
    g = grow + bias_col
    fl = _log_sigmoid(g[HEADS:, :])
    triu = jnp.where(_tri(n, False), 1.0, 0.0).astype(BF16)
    b = sum(jnp.dot(p, triu, preferred_element_type=F32) for p in _split3(fl))
    return g[:HEADS, :], b


MLSTM_GROUP = 4


def _eye_bf16(n):
    return jnp.where(_tri(n, True) & _tri(n, False), 1.0, 0.0).astype(BF16)


def _mlstm_state_update(k_bf, v, ig_col, b_col, c_old, n_old, m_old, k_t=None, gate_rows=None):
    n_rows = k_bf.shape[0]
    k_scale = DK ** -0.5
    b_last = b_col[n_rows - 1:n_rows, :]
    e_col = b_last - b_col + ig_col
    m_new = jnp.maximum(b_last + m_old, jnp.max(e_col, axis=0, keepdims=True))
    w_end = jnp.exp(e_col - m_new) * k_scale
    carry = jnp.exp(b_last + m_old - m_new)
    if k_t is None:
        k_t = lax.dot_general(_eye_bf16(DK), k_bf, NT_DIMS, preferred_element_type=F32)
    if gate_rows is None:
        kv = jnp.dot(k_t.astype(BF16), (v.astype(F32) * w_end).astype(BF16), preferred_element_type=F32)
    else:
        ig_row, b_row = gate_rows
        w_end_row = jnp.exp(b_last - b_row + ig_row - m_new) * k_scale
        kv = jnp.dot((k_t * w_end_row).astype(BF16), v, preferred_element_type=F32)
    c_new = carry * c_old + kv
    n_new = carry * n_old + jnp.sum(k_bf.astype(F32) * w_end, axis=0, keepdims=True)
    return c_new, n_new, m_new


def _mlstm_decay(ig_row, b_row, b_col, m_old):
    n_rows = b_col.shape[0]
    dmat = jnp.where(_tri(n_rows, True), b_col + (ig_row - b_row), -jnp.inf)
    inter = b_col + m_old
    m_t = jnp.maximum(inter, jnp.max(dmat, axis=-1, keepdims=True))
    return jnp.exp(dmat - m_t), jnp.exp(inter - m_t), m_t


def _head_norm_gate(h, o, gain):
    mu = jnp.mean(h, axis=-1, keepdims=True)
    hc = h - mu
    var = jnp.mean(hc * hc, axis=-1, keepdims=True)
    return _sigmoid(o) * (hc * lax.rsqrt(var + EPS) * gain)


def _mlstm_prompt_kernel(batch, q_ref, k_ref, v_ref, o_ref, gc_ref, *rest):
    gr_refs = rest[:batch]
    rest = rest[batch:]
    km_ref, vm_ref, gcm_ref, brow_ref, bcol_ref, gain_ref = rest[:6]
    cast_in = rest[6:6 + N_CAST_IN]
    y_ref, cf_ref, nf_ref, mf_ref = rest[6 + N_CAST_IN:10 + N_CAST_IN]
    cast_out = rest[10 + N_CAST_IN:]
    k_scale = DK ** -0.5

    _weight_cast_step(cast_in, cast_out)

    @pl.when(pl.program_id(0) == 0)
    def _():
        ig_c, b_c = _gate_columns(gcm_ref[...], brow_ref[...])
        for h in range(HEADS):
            hs = slice(h * DK, (h + 1) * DK)
            c_new, n_new, m_new = _mlstm_state_update(
                km_ref[:, hs], vm_ref[:, hs], ig_c[:, h:h + 1], b_c[:, h:h + 1],
                jnp.zeros((DK, DV), F32), jnp.zeros((1, DK), F32), jnp.zeros((1, 1), F32))
            for b in range(batch):
                cf_ref[b, h] = c_new
                nf_ref[b, h] = n_new
                mf_ref[b, h:h + 1, :] = jnp.broadcast_to(m_new, (1, LANE_TILE))

    eye = _eye_bf16(DK)
    for b0 in range(0, batch, MLSTM_GROUP):
        gates = {}
        for b in range(b0, b0 + MLSTM_GROUP):
            gates[b] = _gate_columns(gc_ref[b], brow_ref[...]) + _gate_rows(gr_refs[b][...], bcol_ref[...])
        chains = [(b, h) for b in range(b0, b0 + MLSTM_GROUP) for h in range(HEADS)]

        ops = []
        for b, h in chains:
            hs = slice(h * DK, (h + 1) * DK)
            q = q_ref[b, :, hs]
            k_bf = k_ref[b, :, hs]
            c_old = cf_ref[b, h]
            ops.append(dict(
                q=q, k_bf=k_bf, v=v_ref[b, :, hs], c_old=c_old, n_old=nf_ref[b, h], m_old=mf_ref[b, h:h + 1, 0:1],
                s_raw=lax.dot_general(q, k_bf, NT_DIMS, preferred_element_type=F32),
                qc=jnp.dot(q, c_old.astype(BF16), preferred_element_type=F32),
                k_t=lax.dot_general(eye, k_bf, NT_DIMS, preferred_element_type=F32)))

        for (b, h), c in zip(chains, ops):
            ig_c, b_c, ig_r, b_r = gates[b]
            c["w_intra"], c["w_inter"], c["m_t"] = _mlstm_decay(ig_r[h:h + 1, :], b_r[h:h + 1, :], b_c[:, h:h + 1],
                                                                c["m_old"])

        for (b, h), c in zip(chains, ops):
            hs = slice(h * DK, (h + 1) * DK)
            s = c["s_raw"] * (c["w_intra"] * k_scale)
            num = c["w_inter"] * c["qc"] + jnp.dot(s.astype(BF16), c["v"], preferred_element_type=F32)
            den = (c["w_inter"] * jnp.sum(c["q"].astype(F32) * c["n_old"], axis=-1, keepdims=True)
                   + jnp.sum(s, axis=-1, keepdims=True))
            hid = num / jnp.maximum(jnp.abs(den), jnp.exp(-c["m_t"]))
            y_ref[b, :, hs] = _head_norm_gate(hid, o_ref[b, :, hs], gain_ref[:, hs]).astype(y_ref.dtype)

        for (b, h), c in zip(chains, ops):
            ig_c, b_c, ig_r, b_r = gates[b]
            c_new, n_new, m_new = _mlstm_state_update(c["k_bf"], c["v"], ig_c[:, h:h + 1], b_c[:, h:h + 1],
                                                      c["c_old"], c["n_old"], c["m_old"], k_t=c["k_t"],
                                                      gate_rows=(ig_r[h:h + 1, :], b_r[h:h + 1, :]))
            cf_ref[b, h] = c_new
            nf_ref[b, h] = n_new
            mf_ref[b, h:h + 1, :] = jnp.broadcast_to(m_new, (1, LANE_TILE))


def _mlstm_prompt(qkv, o_arr, o_col_block, gcol, grow, qkv_meta, gcol_meta, bias_row, bias_col, gain, batch, seq,
                  w_in_t, cast_weights):
    n_c = seq // CHUNK
    width = HEADS * DK
    const2 = lambda c: (0, 0)
    cast_in, cast_out, cast_shapes, cast_args = _weight_cast_specs(n_c, lambda c: c, w_in_t, cast_weights)
    qkv3 = qkv.reshape(batch, seq, 3 * width)
    o3 = o_arr.reshape(batch, seq, o_arr.shape[1])
    gcol3 = gcol.reshape(gcol.shape[0] // seq, seq, 2 * HEADS)
    state0 = lambda c: (0, 0, 0, 0)
    return pl.pallas_call(
        functools.partial(_mlstm_prompt_kernel, batch),
        grid=(n_c,),
        in_specs=[pl.BlockSpec((batch, CHUNK, width), lambda c: (0, c, 0)),
                  pl.BlockSpec((batch, CHUNK, width), lambda c: (0, c, 1)),
                  pl.BlockSpec((batch, CHUNK, width), lambda c: (0, c, 2)),
                  pl.BlockSpec((batch, CHUNK, width), lambda c: (0, c, o_col_block)),
                  pl.BlockSpec((batch, CHUNK, 2 * HEADS), lambda c: (0, c, 0))]
                 + [pl.BlockSpec((2 * HEADS, CHUNK), functools.partial(lambda b, c: (0, b * n_c + c), b))
                    for b in range(batch)]
                 + [pl.BlockSpec((N_META, width), lambda c: (0, 1)),
                    pl.BlockSpec((N_META, width), lambda c: (0, 2)),
                    pl.BlockSpec((N_META, 2 * HEADS), const2),
                    pl.BlockSpec((1, 2 * HEADS), const2),
                    pl.BlockSpec((2 * HEADS, 1), const2),
                    pl.BlockSpec((1, width), const2)] + cast_in,
        out_specs=[pl.BlockSpec((batch, CHUNK, width), lambda c: (0, c, 0)),
                   pl.BlockSpec((batch, HEADS, DK, DV), state0),
                   pl.BlockSpec((batch, HEADS, 1, DK), state0),
                   pl.BlockSpec((batch, HEADS, LANE_TILE), lambda c: (0, 0, 0))] + cast_out,
        out_shape=[jax.ShapeDtypeStruct((batch, seq, width), BF16),
                   jax.ShapeDtypeStruct((batch, HEADS, DK, DV), F32),
                   jax.ShapeDtypeStruct((batch, HEADS, 1, DK), F32),
                   jax.ShapeDtypeStruct((batch, HEADS, LANE_TILE), F32)] + cast_shapes,
        compiler_params=_cparams(1, 52),
        name="mlstm_prompt",
    )(qkv3, qkv3, qkv3, o3, gcol3, *([grow] * batch), qkv_meta, qkv_meta, gcol_meta, bias_row, bias_col, gain,
      *cast_args)


MLSTM_SB = 8


def _mlstm_sample_kernel(qkv_ref, o_ref, g_ref, c_ref, n_ref, m_ref, brow_ref, gain_ref,
                         y_ref, co_ref, no_ref, mo_ref):
    sb = MLSTM_SB
    k_scale = DK ** -0.5
    eye = _eye_bf16(DK)
    width = HEADS * DK
    pad = 2 * sb
    row_id = lax.broadcasted_iota(jnp.int32, (pad, DV), 0)

    g_all = g_ref[...] + brow_ref[...]
    ig_all = g_all[:, :HEADS]
    fl_all = _log_sigmoid(g_all[:, HEADS:])
    m_old_all = m_ref[...]
    m_new_all = jnp.maximum(fl_all + m_old_all, ig_all)
    w_in_all = jnp.exp(ig_all - m_new_all)
    w_ca_all = jnp.exp(fl_all + m_old_all - m_new_all)
    floor_all = jnp.exp(-m_new_all)
    mo_ref[...] = m_new_all

    for h in range(HEADS):
        hs = slice(h * DK, (h + 1) * DK)
        qk_rows = jnp.concatenate([qkv_ref[:, hs], qkv_ref[:, width + h * DK:width + (h + 1) * DK] * k_scale],
                                  axis=0).astype(BF16)
        qk_cols = lax.dot_general(eye, qk_rows, NT_DIMS, preferred_element_type=F32).astype(BF16)
        for i in range(sb):
            w_in = w_in_all[i:i + 1, h:h + 1]
            w_ca = w_ca_all[i:i + 1, h:h + 1]
            q = qk_rows[i:i + 1, :]
            qf = q.astype(F32)
            k = qk_rows[sb + i:sb + i + 1, :].astype(F32)
            v = qkv_ref[i:i + 1, 2 * width + h * DK:2 * width + (h + 1) * DK].astype(BF16).astype(F32)
            c_old = c_ref[i, h]
            n_old = n_ref[i, h]
            qc = jnp.dot(jnp.broadcast_to(q, (pad, DK)), c_old.astype(BF16), preferred_element_type=F32)[0:1, :]
            v_rows = jnp.where(row_id == sb + i, jnp.broadcast_to(v * w_in, (pad, DV)), 0.0).astype(BF16)
            kv = jnp.dot(qk_cols, v_rows, preferred_element_type=F32)
            s = jnp.sum(qf * k, axis=-1, keepdims=True) * w_in
            num = w_ca * qc + s * v
            den = w_ca * jnp.sum(qf * n_old, axis=-1, keepdims=True) + s
            hid = num / jnp.maximum(jnp.abs(den), floor_all[i:i + 1, h:h + 1])
            co_ref[i, h] = w_ca * c_old + kv
            no_ref[i, h] = w_ca * n_old + k * w_in
            y_ref[i:i + 1, hs] = _head_norm_gate(hid, o_ref[i:i + 1, hs], gain_ref[:, hs])


def _mlstm_sample(qkv, o_arr, gcol, c0, n0, m0, bias_row, gain):
    nb = qkv.shape[0]
    width = HEADS * DK
    sb = MLSTM_SB
    const2 = lambda i: (0, 0)
    return pl.pallas_call(
        _mlstm_sample_kernel,
        grid=(nb // sb,),
        in_specs=[pl.BlockSpec((sb, 3 * width), lambda i: (i, 0)),
                  pl.BlockSpec((sb, width), lambda i: (i, 0)),
                  pl.BlockSpec((sb, 2 * HEADS), lambda i: (i, 0)),
                  pl.BlockSpec((sb, HEADS, DK, DV), lambda i: (i, 0, 0, 0)),
                  pl.BlockSpec((sb, HEADS, 1, DK), lambda i: (i, 0, 0, 0)),
                  pl.BlockSpec((sb, HEADS), lambda i: (i, 0)),
                  pl.BlockSpec((1, 2 * HEADS), const2),
                  pl.BlockSpec((1, width), const2)],
        out_specs=[pl.BlockSpec((sb, width), lambda i: (i, 0)),
                   pl.BlockSpec((sb, HEADS, DK, DV), lambda i: (i, 0, 0, 0)),
                   pl.BlockSpec((sb, HEADS, 1, DK), lambda i: (i, 0, 0, 0)),
                   pl.BlockSpec((sb, HEADS), lambda i: (i, 0))],
        out_shape=[jax.ShapeDtypeStruct((nb, width), F32),
                   jax.ShapeDtypeStruct((nb, HEADS, DK, DV), F32),
                   jax.ShapeDtypeStruct((nb, HEADS, 1, DK), F32),
                   jax.ShapeDtypeStruct((nb, HEADS), F32)],
        compiler_params=_cparams(1, 48),
        name="mlstm_sample",
    )(qkv, o_arr, gcol, c0, n0, m0, bias_row, gain)


def _merge_rows(ya, yb, x, wgab_ref, wglu_ref, bglu_ref, wa_ref, wb_ref, wout_ref, g1_ref, b1_ref):
    xb = x.astype(BF16)
    g = _gelu_tanh(ya)
    z = jnp.dot(g.astype(BF16), wglu_ref[...], preferred_element_type=F32) + bglu_ref[...]
    out_a = g * _sigmoid(z)
    up_a = jnp.dot(out_a.astype(BF16), wa_ref[...], preferred_element_type=F32)
    g_a = lax.dot_general(xb, wgab_ref[0:D_MODEL, :], NT_DIMS, preferred_element_type=F32)
    mix = _sigmoid(g_a) * up_a
    up_b = jnp.dot(yb, wb_ref[...], preferred_element_type=F32)
    g_b = lax.dot_general(xb, wgab_ref[D_MODEL:2 * D_MODEL, :], NT_DIMS, preferred_element_type=F32)
    mix = mix + _sigmoid(g_b) * up_b
    mo = jnp.dot(mix.astype(BF16), wout_ref[...], preferred_element_type=F32)
    return _layernorm_rows(ALPHA * x + mo, g1_ref[...], b1_ref[...])


def _merge_kernel(ya_ref, yb_ref, x_ref, yas_ref, ybs_ref, xs_ref, *rest):
    weights, (o_ref, os_ref, ob_ref, osb_ref) = rest[:-4], rest[-4:]
    x1 = _merge_rows(ya_ref[...], yb_ref[...], x_ref[...], *weights)
    o_ref[...] = x1
    ob_ref[...] = x1.astype(BF16)

    @pl.when(pl.program_id(0) == pl.num_programs(0) - 1)
    def _():
        x1s = _merge_rows(yas_ref[...], ybs_ref[...].astype(BF16), xs_ref[...], *weights)
        os_ref[...] = x1s
        osb_ref[...] = x1s.astype(BF16)


def _merge(ya, yb, x, ya_s, yb_s, x_s, wt_gab, w_glu, b_glu, w_a_up, w_b_up, w_out, ln_g, ln_b, tm):
    r = x.shape[0]
    rs = x_s.shape[0]
    const2 = lambda i: (0, 0)
    resident = functools.partial(pl.BlockSpec, index_map=const2, pipeline_mode=pl.Buffered(1))
    return pl.pallas_call(
        _merge_kernel,
        grid=(r // tm,),
        in_specs=[pl.BlockSpec((tm, SSM_WIDTH), lambda i: (i, 0)),
                  pl.BlockSpec((tm, HEADS * DV), lambda i: (i, 0)),
                  pl.BlockSpec((tm, D_MODEL), lambda i: (i, 0)),
                  resident((rs, SSM_WIDTH)),
                  resident((rs, HEADS * DV)),
                  resident((rs, D_MODEL)),
                  resident((2 * D_MODEL, D_MODEL)),
                  resident((SSM_WIDTH, SSM_WIDTH)),
                  resident((1, SSM_WIDTH)),
                  resident((SSM_WIDTH, D_MODEL)),
                  resident((HEADS * DV, D_MODEL)),
                  resident((D_MODEL, D_MODEL)),
                  resident((1, D_MODEL)),
                  resident((1, D_MODEL))],
        out_specs=[pl.BlockSpec((tm, D_MODEL), lambda i: (i, 0)),
                   pl.BlockSpec((rs, D_MODEL), const2),
                   pl.BlockSpec((tm, D_MODEL), lambda i: (i, 0)),
                   pl.BlockSpec((rs, D_MODEL), const2)],
        out_shape=[jax.ShapeDtypeStruct((r, D_MODEL), F32), jax.ShapeDtypeStruct((rs, D_MODEL), F32),
                   jax.ShapeDtypeStruct((r, D_MODEL), BF16), jax.ShapeDtypeStruct((rs, D_MODEL), BF16)],
        compiler_params=_cparams(1, 59),
        name="merge_ln1",
    )(ya, yb, x, ya_s, yb_s, x_s, wt_gab, w_glu, b_glu, w_a_up, w_b_up, w_out, ln_g, ln_b)


def _swiglu(xb, wg, wu):
    hg = jnp.dot(xb, wg, preferred_element_type=F32)
    hu = jnp.dot(xb, wu, preferred_element_type=F32)
    return ((hg * _sigmoid(hg)) * hu).astype(BF16)


FFN_UP_PIECE = 1024


def _ffn_up_kernel(x_ref, xs_ref, wg_ref, wu_ref, wd_ref, h_ref, hs_ref, wdb_ref, wgb_s, wub_s):
    wgb_s[...] = wg_ref[...].astype(BF16)
    wub_s[...] = wu_ref[...].astype(BF16)

    for r0 in range(0, x_ref.shape[0], FFN_UP_PIECE):
        rows = slice(r0, r0 + FFN_UP_PIECE)
        h_ref[rows, :] = _swiglu(x_ref[rows, :], wgb_s[...], wub_s[...])
    wdb_ref[...] = wd_ref[...].astype(BF16)

    @pl.when(pl.program_id(0) == pl.num_programs(0) - 1)
    def _():
        hs_ref[...] = _swiglu(xs_ref[...], wgb_s[...], wub_s[...])


def _ffn_up(xb, xsb, w_gate3, w_up3, w_down3, tm, tf):
    r = xb.shape[0]
    rs = xsb.shape[0]
    n_f, n_i = D_FF // tf, r // tm
    wd_rows = D_FF // (n_f * n_i)
    hs_block = lambda i, f: jnp.where(i == n_i - 1, f, 0)
    return pl.pallas_call(
        _ffn_up_kernel,
        grid=(n_i, n_f),
        in_specs=[pl.BlockSpec((tm, D_MODEL), lambda i, f: (i, 0)),
                  pl.BlockSpec((rs, D_MODEL), lambda i, f: (0, 0)),
                  pl.BlockSpec((None, D_MODEL, tf), lambda i, f: (0, 0, f)),
                  pl.BlockSpec((None, D_MODEL, tf), lambda i, f: (0, 0, f)),
                  pl.BlockSpec((None, wd_rows, D_MODEL), lambda i, f: (0, i * n_f + f, 0))],
        out_specs=[pl.BlockSpec((tm, tf), lambda i, f: (i, f)),
                   pl.BlockSpec((rs, tf), lambda i, f: (0, hs_block(i, f))),
                   pl.BlockSpec((wd_rows, D_MODEL), lambda i, f: (i * n_f + f, 0))],
        out_shape=[jax.ShapeDtypeStruct((r, D_FF), BF16), jax.ShapeDtypeStruct((rs, D_FF), BF16),
                   jax.ShapeDtypeStruct((D_FF, D_MODEL), BF16)],
        scratch_shapes=[pltpu.VMEM((D_MODEL, tf), BF16), pltpu.VMEM((D_MODEL, tf), BF16)],
        compiler_params=_cparams(2, 58),
        name="ffn_up",
    )(xb, xsb, w_gate3, w_up3, w_down3)


FFN_DOWN_PIECE = 256


def _ffn_down_rows(h, x, wd_ref, g2_ref, b2_ref):
    return _layernorm_rows(ALPHA * x + jnp.dot(h, wd_ref[...], preferred_element_type=F32), g2_ref[...], b2_ref[...])


def _ffn_down_kernel(h_ref, x_ref, hs_ref, xs_ref, wd_ref, g2_ref, b2_ref, o_ref, os_ref):
    for r0 in range(0, h_ref.shape[0], FFN_DOWN_PIECE):
        rows = slice(r0, r0 + FFN_DOWN_PIECE)
        o_ref[rows, :] = _ffn_down_rows(h_ref[rows, :], x_ref[rows, :], wd_ref, g2_ref, b2_ref)

    @pl.when(pl.program_id(0) == pl.num_programs(0) - 1)
    def _():
        os_ref[...] = _ffn_down_rows(hs_ref[...], xs_ref[...], wd_ref, g2_ref, b2_ref)


def _ffn_down(h, x, h_s, x_s, wd_b, ln_g, ln_b, tm):
    r = x.shape[0]
    rs = x_s.shape[0]
    const2 = lambda i: (0, 0)
    resident = functools.partial(pl.BlockSpec, index_map=const2, pipeline_mode=pl.Buffered(1))
    return pl.pallas_call(
        _ffn_down_kernel,
        grid=(r // tm,),
        in_specs=[pl.BlockSpec((tm, D_FF), lambda i: (i, 0)),
                  pl.BlockSpec((tm, D_MODEL), lambda i: (i, 0)),
                  resident((rs, D_FF)),
                  resident((rs, D_MODEL)),
                  resident((D_FF, D_MODEL)),
                  resident((1, D_MODEL)),
                  resident((1, D_MODEL))],
        out_specs=[pl.BlockSpec((tm, D_MODEL), lambda i: (i, 0)),
                   pl.BlockSpec((rs, D_MODEL), const2)],
        out_shape=[jax.ShapeDtypeStruct((r, D_MODEL), F32), jax.ShapeDtypeStruct((rs, D_MODEL), F32)],
        compiler_params=_cparams(1, 59),
        name="ffn_down_ln2",
    )(h, x, h_s, x_s, wd_b, ln_g, ln_b)


def kernel(x_prompt, x_sample, state_ssm_re, state_ssm_im, state_mlstm_c, state_mlstm_n, state_mlstm_m,
           meta_tokens, w_in, b_if, ssm_a_re, ssm_a_im, ssm_log_dt, ssm_b_re, ssm_b_im, ssm_c_re, ssm_c_im,
           ssm_d, w_glu, b_glu, w_a_up, mh_gain, w_b_up, w_out, ln1_g, ln1_b, w_gate, w_up, w_down,
           ln2_g, ln2_b):
    batch, seq, _ = x_prompt.shape
    nb = x_sample.shape[0]
    width = HEADS * DK

    bias_row = b_if[0].reshape(1, 2 * HEADS)
    bias_col = b_if[0].reshape(2 * HEADS, 1)
    gain = mh_gain[0].reshape(1, width)
    d_skip = ssm_d[0].reshape(1, SSM_WIDTH)
    b_glu_r = b_glu[0].reshape(1, SSM_WIDTH)
    ln1 = (ln1_g[0].reshape(1, D_MODEL), ln1_b[0].reshape(1, D_MODEL))
    ln2 = (ln2_g[0].reshape(1, D_MODEL), ln2_b[0].reshape(1, D_MODEL))
    abar_re, abar_im, bt, ct = _s5_discretise(
        ssm_a_re[0], ssm_a_im[0], ssm_log_dt[0], ssm_b_re[0], ssm_b_im[0], ssm_c_re[0], ssm_c_im[0])

    xp = x_prompt.reshape(batch * seq, D_MODEL)
    xs = x_sample.reshape(nb, D_MODEL)
    xsm = jnp.concatenate([xs, meta_tokens], axis=0)
    w_in_t = jnp.swapaxes(w_in, 1, 2)
    uo_blocks, qkv_blocks = (0, 4, 2), (1, 1, 3)

    uo_p, uo_s, gcol_p, grow_p, gcol_s = _proj(xp, xsm, w_in_t, uo_blocks, F32, F32, 1024, "proj_uo", with_gates=True)
    qkv_p, qkv_s = _proj(xp, xsm, w_in_t, qkv_blocks, BF16, F32, 1024, "proj_qkv")

    u_meta = uo_s[nb:nb + N_META, :SSM_WIDTH]
    qkv_meta = qkv_s[nb:nb + N_META].astype(BF16)
    gcol_meta = gcol_s[nb:nb + N_META]

    ya_p, pf_re, pf_im = _s5_prompt(uo_p, 0, u_meta, abar_re, abar_im, bt, ct, d_skip, batch, seq)
    yb_p, pc, pn, pm, wt_gab, w_out_b, w_a_b, w_b_b, w_glu_b = _mlstm_prompt(
        qkv_p, uo_p, 1, gcol_p, grow_p, qkv_meta, gcol_meta, bias_row, bias_col, gain, batch, seq,
        w_in_t, (w_out, w_a_up, w_b_up, w_glu))
    ya_s, sf_re, sf_im = _s5_sample(uo_s[:nb, :SSM_WIDTH], state_ssm_re[0].reshape(nb, SSM_FLAT),
                                    state_ssm_im[0].reshape(nb, SSM_FLAT), abar_re, abar_im, bt, ct, d_skip)
    yb_s, sc, sn, sm = _mlstm_sample(qkv_s[:nb], uo_s[:nb, SSM_WIDTH:], gcol_s[:nb], state_mlstm_c[0],
                                     state_mlstm_n[0].reshape(nb, HEADS, 1, DK), state_mlstm_m[0], bias_row, gain)

    x1_p, x1_s, x1b_p, x1b_s = _merge(ya_p, yb_p.reshape(batch * seq, width), xp, ya_s, yb_s, xs,
                                      wt_gab, w_glu_b, b_glu_r, w_a_b, w_b_b, w_out_b, *ln1, tm=256)
    h_p, h_s, w_down_b = _ffn_up(x1b_p, x1b_s, w_gate, w_up, w_down, tm=4096, tf=256)
    y_p, y_s = _ffn_down(h_p, x1_p, h_s, x1_s, w_down_b, *ln2, tm=512)

    return (y_p.reshape(batch, seq, D_MODEL),
            y_s.reshape(nb, 1, D_MODEL),
            pf_re.reshape(1, batch, SSM_GROUPS, SSM_STATE),
            pf_im.reshape(1, batch, SSM_GROUPS, SSM_STATE),
            pc.reshape(1, batch, HEADS, DK, DV),
            pn.reshape(1, batch, HEADS, DK),
            pm[:, :, 0].reshape(1, batch, HEADS),
            sf_re.reshape(1, nb, SSM_GROUPS, SSM_STATE),
            sf_im.reshape(1, nb, SSM_GROUPS, SSM_STATE),
            sc.reshape(1, nb, HEADS, DK, DV),
            sn.reshape(1, nb, HEADS, DK),
            sm.reshape(1, nb, HEADS))
```
